```python
import math
import jax, jax.numpy as jnp
from jax import lax
import numpy as np

D_MODEL = 1024
BATCH = 8
SEQ = 2048
DEPTH = 4

D_FF = 2816
FFN_HALF = 0.5
DN_HEAD_DIM = 128
DN_HEADS = D_MODEL // 128
DN_WIDTH = DN_HEADS * DN_HEAD_DIM
DN_CONV = 4
DN_CHUNK = 64
SB_HEAD_DIM = 128
SB_HEADS = D_MODEL // 128
SB_WIDTH = SB_HEADS * SB_HEAD_DIM
SB_BLOCK = 128
IN_SIZES = (3 * DN_WIDTH, DN_WIDTH, DN_HEADS, DN_HEADS, SB_WIDTH, SB_WIDTH, SB_WIDTH, D_MODEL, D_MODEL)
N_IN = 4 * DN_WIDTH + 2 * DN_HEADS + 3 * SB_WIDTH + 2 * D_MODEL
RMS_EPS = 1e-6
L2_EPS = 1e-6

kernel_name = "hybrid_deltanet_stickbreaking_macaron"


def rmsnorm(x, gain):
    xf = x.astype(jnp.float32)
    y = xf * lax.rsqrt(jnp.mean(xf * xf, axis=-1, keepdims=True) + RMS_EPS)
    return (y * gain.astype(jnp.float32)).astype(x.dtype)


def l2norm(x):
    xf = x.astype(jnp.float32)
    return xf * lax.rsqrt(jnp.sum(xf * xf, axis=-1, keepdims=True) + L2_EPS)


def swiglu(h, w_in, w_out):
    gate, up = jnp.split(h @ w_in, 2, axis=-1)
    return (jax.nn.silu(gate) * up) @ w_out


def causal_depthwise_conv(x, w):
    width, ch = w.shape
    return lax.conv_general_dilated(
        x, w[:, None, :], window_strides=(1,), padding=((width - 1, 0),),
        dimension_numbers=("NWC", "WIO", "NWC"), feature_group_count=ch)


def split_cols(proj):
    parts, start = [], 0
    for size in IN_SIZES:
        parts.append(proj[..., start:start + size])
        start += size
    return parts


def to_heads(t, n_heads, head_dim):
    b, s, _ = t.shape
    return t.reshape(b, s, n_heads, head_dim).transpose(0, 2, 1, 3)


def gated_delta_rule_chunked(q, k, v, g, beta):
    b, h, t, dk = q.shape
    dv = v.shape[-1]
    c = DN_CHUNK
    n = t // c
    q = (q * (dk ** -0.5)).reshape(b, h, n, c, dk)
    k = k.reshape(b, h, n, c, dk)
    v = v.astype(jnp.float32).reshape(b, h, n, c, dv)
    beta = beta.reshape(b, h, n, c)
    g = lax.cumsum(g.reshape(b, h, n, c), axis=3)
    idx = jnp.arange(c)
    lower_incl = idx[:, None] >= idx[None, :]
    strict = idx[:, None] > idx[None, :]
    decay = jnp.exp(jnp.where(lower_incl, g[..., :, None] - g[..., None, :], -jnp.inf))
    k_beta = k * beta[..., None]
    lmat = jnp.where(strict, jnp.einsum("bhnid,bhnjd->bhnij", k_beta, k) * decay, 0.0)
    amat = lmat + jnp.eye(c, dtype=jnp.float32)
    rhs = jnp.concatenate([v * beta[..., None], k_beta * jnp.exp(g)[..., None]], axis=-1)
    sol = lax.linalg.triangular_solve(amat, rhs, left_side=True, lower=True, unit_diagonal=True)
    u, w = sol[..., :dv], sol[..., dv:]
    attn_intra = jnp.einsum("bhnid,bhnjd->bhnij", q, k) * decay
    q_dec = q * jnp.exp(g)[..., None]
    g_last = g[..., -1]
    k_dec = k * jnp.exp(g_last[..., None] - g)[..., None]

    def chunk_step(state, inp):
        q_i, k_i, u_i, w_i, a_i, gl_i = inp
        v_new = u_i - jnp.einsum("bhcd,bhdv->bhcv", w_i, state)
        o_i = jnp.einsum("bhcd,bhdv->bhcv", q_i, state) + jnp.einsum("bhij,bhjv->bhiv", a_i, v_new)
        state = state * jnp.exp(gl_i)[..., None, None] + jnp.einsum("bhcd,bhcv->bhdv", k_i, v_new)
        return state, o_i

    chunk_first = lambda arr: jnp.moveaxis(arr, 2, 0)
    s0 = jnp.zeros((b, h, dk, dv), jnp.float32)
    _, o = lax.scan(chunk_step, s0, (chunk_first(q_dec), chunk_first(k_dec), chunk_first(u),
                                     chunk_first(w), chunk_first(attn_intra), chunk_first(g_last)))
    return jnp.moveaxis(o, 0, 2).reshape(b, h, t, dv)


def stick_breaking_attention(q, k, v):
    t = q.shape[2]
    scale = q.shape[-1] ** -0.5
    outs = []
    for blk in range(t // SB_BLOCK):
        t0, t1 = blk * SB_BLOCK, (blk + 1) * SB_BLOCK
        z = jnp.einsum("bhqd,bhkd->bhqk", q[:, :, t0:t1], k[:, :, :t1]).astype(jnp.float32) * scale
        causal = jnp.arange(t1)[None, :] < (t0 + jnp.arange(SB_BLOCK))[:, None]
        log_1mb = jnp.where(causal, -jax.nn.softplus(z), 0.0)
        survive = lax.cumsum(log_1mb, axis=3, reverse=True) - log_1mb
        weights = jnp.where(causal, jnp.exp(jax.nn.log_sigmoid(z) + survive), 0.0)
        outs.append(jnp.einsum("bhqk,bhkd->bhqd", weights.astype(v.dtype), v[:, :, :t1]))
    return jnp.concatenate(outs, axis=2)


def hybrid_mixer(h, w_in, conv_w, a_log, dt_bias, dn_out_norm, sb_q_norm, sb_k_norm,
                 w_branch_a, w_branch_b, w_out):
    b, t, _ = h.shape
    dn_qkv, dn_z, dn_b, dn_a, sb_q, sb_k, sb_v, gate_a, gate_b = split_cols(h @ w_in)
    qkv = jax.nn.silu(causal_depthwise_conv(dn_qkv, conv_w))
    q, k, v = jnp.split(qkv, 3, axis=-1)
    q = l2norm(to_heads(q, DN_HEADS, DN_HEAD_DIM))
    k = l2norm(to_heads(k, DN_HEADS, DN_HEAD_DIM))
    v = to_heads(v, DN_HEADS, DN_HEAD_DIM)
    beta = jax.nn.sigmoid(dn_b.astype(jnp.float32)).transpose(0, 2, 1)
    g = (-jnp.exp(a_log.astype(jnp.float32))
         * jax.nn.softplus(dn_a.astype(jnp.float32) + dt_bias.astype(jnp.float32))).transpose(0, 2, 1)
    o_a = gated_delta_rule_chunked(q, k, v, g, beta).transpose(0, 2, 1, 3)
    o_a = rmsnorm(o_a, dn_out_norm) * jax.nn.silu(dn_z.reshape(b, t, DN_HEADS, DN_HEAD_DIM).astype(jnp.float32))
    y_a = o_a.reshape(b, t, DN_WIDTH).astype(h.dtype) @ w_branch_a
    qb = rmsnorm(to_heads(sb_q, SB_HEADS, SB_HEAD_DIM), sb_q_norm)
    kb = rmsnorm(to_heads(sb_k, SB_HEADS, SB_HEAD_DIM), sb_k_norm)
    vb = to_heads(sb_v, SB_HEADS, SB_HEAD_DIM)
    o_b = stick_breaking_attention(qb, kb, vb).transpose(0, 2, 1, 3).reshape(b, t, SB_WIDTH)
    y_b = o_b @ w_branch_b
    merged = jax.nn.sigmoid(gate_a) * y_a + jax.nn.sigmoid(gate_b) * y_b
    return merged @ w_out


def _fwd_setup_inputs(seed: int = 0) -> dict:
    key = jax.random.key(seed)
    ks = jax.random.split(key, 20)
    L = DEPTH

    def dense(k, shape, fan_in):
        return jax.random.normal(k, shape, jnp.float32) * (fan_in ** -0.5)

    def gain(k, shape):
        return 1.0 + 0.1 * jax.random.normal(k, shape, jnp.float32)

    dt = jnp.exp(jax.random.uniform(ks[8], (L, DN_HEADS), jnp.float32, math.log(1e-3), math.log(1e-1)))
    return {
        "x": jax.random.normal(ks[0], (BATCH, SEQ, D_MODEL), jnp.float32),
        "ffn1_norm": gain(ks[1], (L, D_MODEL)),
        "ffn1_w_in": dense(ks[2], (L, D_MODEL, 2 * D_FF), D_MODEL),
        "ffn1_w_out": dense(ks[3], (L, D_FF, D_MODEL), D_FF),
        "mix_norm": gain(ks[4], (L, D_MODEL)),
        "w_in": dense(ks[5], (L, D_MODEL, N_IN), D_MODEL),
        "dn_conv_w": dense(ks[6], (L, DN_CONV, 3 * DN_WIDTH), DN_CONV),
        "dn_a_log": jnp.log(jax.random.uniform(ks[7], (L, DN_HEADS), jnp.float32, 1.0, 16.0)),
        "dn_dt_bias": dt + jnp.log(-jnp.expm1(-dt)),
        "dn_out_norm": gain(ks[9], (L, DN_HEAD_DIM)),
        "sb_q_norm": gain(ks[10], (L, SB_HEAD_DIM)),
        "sb_k_norm": gain(ks[11], (L, SB_HEAD_DIM)),
        "w_branch_a": dense(ks[12], (L, DN_WIDTH, D_MODEL), DN_WIDTH),
        "w_branch_b": dense(ks[13], (L, SB_WIDTH, D_MODEL), SB_WIDTH),
        "w_out": dense(ks[14], (L, D_MODEL, D_MODEL), D_MODEL),
        "ffn2_norm": gain(ks[15], (L, D_MODEL)),
        "ffn2_w_in": dense(ks[16], (L, D_MODEL, 2 * D_FF), D_MODEL),
        "ffn2_w_out": dense(ks[17], (L, D_FF, D_MODEL), D_FF),
    }


def _fwd_reference(x, ffn1_norm, ffn1_w_in, ffn1_w_out, mix_norm, w_in, dn_conv_w, dn_a_log, dn_dt_bias,
              dn_out_norm, sb_q_norm, sb_k_norm, w_branch_a, w_branch_b, w_out,
              ffn2_norm, ffn2_w_in, ffn2_w_out):
    for l in range(DEPTH):
        x = x + FFN_HALF * swiglu(rmsnorm(x, ffn1_norm[l]), ffn1_w_in[l], ffn1_w_out[l])
        x = x + hybrid_mixer(rmsnorm(x, mix_norm[l]), w_in[l], dn_conv_w[l], dn_a_log[l], dn_dt_bias[l],
                             dn_out_norm[l], sb_q_norm[l], sb_k_norm[l],
                             w_branch_a[l], w_branch_b[l], w_out[l])
        x = x + FFN_HALF * swiglu(rmsnorm(x, ffn2_norm[l]), ffn2_w_in[l], ffn2_w_out[l])
    return x


import jax as _jax
import jax.numpy as _jnp

TWIN_FORMAT = 'train_step'
FWD_PARAMS = ['x', 'ffn1_norm', 'ffn1_w_in', 'ffn1_w_out', 'mix_norm', 'w_in', 'dn_conv_w', 'dn_a_log', 'dn_dt_bias', 'dn_out_norm', 'sb_q_norm', 'sb_k_norm', 'w_branch_a', 'w_branch_b', 'w_out', 'ffn2_norm', 'ffn2_w_in', 'ffn2_w_out']
TWIN_WEIGHTS = ['ffn1_norm', 'ffn1_w_in', 'ffn1_w_out', 'mix_norm', 'w_in', 'dn_conv_w', 'dn_a_log', 'dn_dt_bias', 'dn_out_norm', 'sb_q_norm', 'sb_k_norm', 'w_branch_a', 'w_branch_b', 'w_out', 'ffn2_norm', 'ffn2_w_in', 'ffn2_w_out']
TWIN_DIFF_INPUT = 'x'
TWIN_INPUTS = ['x', 'ffn1_norm', 'ffn1_w_in', 'ffn1_w_out', 'mix_norm', 'w_in', 'dn_conv_w', 'dn_a_log', 'dn_dt_bias', 'dn_out_norm', 'sb_q_norm', 'sb_k_norm', 'w_branch_a', 'w_branch_b', 'w_out', 'ffn2_norm', 'ffn2_w_in', 'ffn2_w_out', 'loss_target', 'm_ffn1_norm', 'm_ffn1_w_in', 'm_ffn1_w_out', 'm_mix_norm', 'm_w_in', 'm_dn_conv_w', 'm_dn_a_log', 'm_dn_dt_bias', 'm_dn_out_norm', 'm_sb_q_norm', 'm_sb_k_norm', 'm_w_branch_a', 'm_w_branch_b', 'm_w_out', 'm_ffn2_norm', 'm_ffn2_w_in', 'm_ffn2_w_out', 'v_ffn1_norm', 'v_ffn1_w_in', 'v_ffn1_w_out', 'v_mix_norm', 'v_w_in', 'v_dn_conv_w', 'v_dn_a_log', 'v_dn_dt_bias', 'v_dn_out_norm', 'v_sb_q_norm', 'v_sb_k_norm', 'v_w_branch_a', 'v_w_branch_b', 'v_w_out', 'v_ffn2_norm', 'v_ffn2_w_in', 'v_ffn2_w_out']
TWIN_OUTPUTS = ['loss', 'grad_x', 'grad_ffn1_norm', 'grad_ffn1_w_in', 'grad_ffn1_w_out', 'grad_mix_norm', 'grad_w_in', 'grad_dn_conv_w', 'grad_dn_a_log', 'grad_dn_dt_bias', 'grad_dn_out_norm', 'grad_sb_q_norm', 'grad_sb_k_norm', 'grad_w_branch_a', 'grad_w_branch_b', 'grad_w_out', 'grad_ffn2_norm', 'grad_ffn2_w_in', 'grad_ffn2_w_out', 'delta_ffn1_norm', 'delta_ffn1_w_in', 'delta_ffn1_w_out', 'delta_mix_norm', 'delta_w_in', 'delta_dn_conv_w', 'delta_dn_a_log', 'delta_dn_dt_bias', 'delta_dn_out_norm', 'delta_sb_q_norm', 'delta_sb_k_norm', 'delta_w_branch_a', 'delta_w_branch_b', 'delta_w_out', 'delta_ffn2_norm', 'delta_ffn2_w_in', 'delta_ffn2_w_out', 'new_m_ffn1_norm', 'new_m_ffn1_w_in', 'new_m_ffn1_w_out', 'new_m_mix_norm', 'new_m_w_in', 'new_m_dn_conv_w', 'new_m_dn_a_log', 'new_m_dn_dt_bias', 'new_m_dn_out_norm', 'new_m_sb_q_norm', 'new_m_sb_k_norm', 'new_m_w_branch_a', 'new_m_w_branch_b', 'new_m_w_out', 'new_m_ffn2_norm', 'new_m_ffn2_w_in', 'new_m_ffn2_w_out', 'new_v_ffn1_norm', 'new_v_ffn1_w_in', 'new_v_ffn1_w_out', 'new_v_mix_norm', 'new_v_w_in', 'new_v_dn_conv_w', 'new_v_dn_a_log', 'new_v_dn_dt_bias', 'new_v_dn_out_norm', 'new_v_sb_q_norm', 'new_v_sb_k_norm', 'new_v_w_branch_a', 'new_v_w_branch_b', 'new_v_w_out', 'new_v_ffn2_norm', 'new_v_ffn2_w_in', 'new_v_ffn2_w_out']
TWIN_LEAF_KINDS = {'loss': 'loss', 'grad_x': 'grad_x', 'grad_ffn1_norm': 'grad_w', 'grad_ffn1_w_in': 'grad_w', 'grad_ffn1_w_out': 'grad_w', 'grad_mix_norm': 'grad_w', 'grad_w_in': 'grad_w', 'grad_dn_conv_w': 'grad_w', 'grad_dn_a_log': 'grad_w', 'grad_dn_dt_bias': 'grad_w', 'grad_dn_out_norm': 'grad_w', 'grad_sb_q_norm': 'grad_w', 'grad_sb_k_norm': 'grad_w', 'grad_w_branch_a': 'grad_w', 'grad_w_branch_b': 'grad_w', 'grad_w_out': 'grad_w', 'grad_ffn2_norm': 'grad_w', 'grad_ffn2_w_in': 'grad_w', 'grad_ffn2_w_out': 'grad_w', 'delta_ffn1_norm': 'delta_w', 'delta_ffn1_w_in': 'delta_w', 'delta_ffn1_w_out': 'delta_w', 'delta_mix_norm': 'delta_w', 'delta_w_in': 'delta_w', 'delta_dn_conv_w': 'delta_w', 'delta_dn_a_log': 'delta_w', 'delta_dn_dt_bias': 'delta_w', 'delta_dn_out_norm': 'delta_w', 'delta_sb_q_norm': 'delta_w', 'delta_sb_k_norm': 'delta_w', 'delta_w_branch_a': 'delta_w', 'delta_w_branch_b': 'delta_w', 'delta_w_out': 'delta_w', 'delta_ffn2_norm': 'delta_w', 'delta_ffn2_w_in': 'delta_w', 'delta_ffn2_w_out': 'delta_w', 'new_m_ffn1_norm': 'new_m', 'new_m_ffn1_w_in': 'new_m', 'new_m_ffn1_w_out': 'new_m', 'new_m_mix_norm': 'new_m', 'new_m_w_in': 'new_m', 'new_m_dn_conv_w': 'new_m', 'new_m_dn_a_log': 'new_m', 'new_m_dn_dt_bias': 'new_m', 'new_m_dn_out_norm': 'new_m', 'new_m_sb_q_norm': 'new_m', 'new_m_sb_k_norm': 'new_m', 'new_m_w_branch_a': 'new_m', 'new_m_w_branch_b': 'new_m', 'new_m_w_out': 'new_m', 'new_m_ffn2_norm': 'new_m', 'new_m_ffn2_w_in': 'new_m', 'new_m_ffn2_w_out': 'new_m', 'new_v_ffn1_norm': 'new_v', 'new_v_ffn1_w_in': 'new_v', 'new_v_ffn1_w_out': 'new_v', 'new_v_mix_norm': 'new_v', 'new_v_w_in': 'new_v', 'new_v_dn_conv_w': 'new_v', 'new_v_dn_a_log': 'new_v', 'new_v_dn_dt_bias': 'new_v', 'new_v_dn_out_norm': 'new_v', 'new_v_sb_q_norm': 'new_v', 'new_v_sb_k_norm': 'new_v', 'new_v_w_branch_a': 'new_v', 'new_v_w_branch_b': 'new_v', 'new_v_w_out': 'new_v', 'new_v_ffn2_norm': 'new_v', 'new_v_ffn2_w_in': 'new_v', 'new_v_ffn2_w_out': 'new_v'}


def _forward(args):
    return _fwd_reference(*[args[k] for k in FWD_PARAMS])


def _output_shape():
    out = _jax.eval_shape(lambda: _forward(_fwd_setup_inputs(0)))
    return out.shape, out.dtype

N_MICROBATCH = 1
ADAM_LR = 0.001
ADAM_B1 = 0.9
ADAM_B2 = 0.999
ADAM_EPS = 1e-08
ADAM_WD = 0.01
ADAM_STEP = 10
PER_EXAMPLE_BATCH_AXIS = {'x': 0, 'loss_target': 0}
SHARED_INPUTS = []
_WEIGHT_DTYPES = {'ffn1_norm': _jnp.float32, 'ffn1_w_in': _jnp.float32, 'ffn1_w_out': _jnp.float32, 'mix_norm': _jnp.float32, 'w_in': _jnp.float32, 'dn_conv_w': _jnp.float32, 'dn_a_log': _jnp.float32, 'dn_dt_bias': _jnp.float32, 'dn_out_norm': _jnp.float32, 'sb_q_norm': _jnp.float32, 'sb_k_norm': _jnp.float32, 'w_branch_a': _jnp.float32, 'w_branch_b': _jnp.float32, 'w_out': _jnp.float32, 'ffn2_norm': _jnp.float32, 'ffn2_w_in': _jnp.float32, 'ffn2_w_out': _jnp.float32}
MOMENT_SCALE = {'ffn1_norm': 3.086460e+00, 'ffn1_w_in': 9.412222e-02, 'ffn1_w_out': 1.562593e-01, 'mix_norm': 4.448559e+00, 'w_in': 1.214744e-01, 'dn_conv_w': 1.832050e-01, 'dn_a_log': 4.126222e+00, 'dn_dt_bias': 3.928221e+00, 'dn_out_norm': 1.407419e+01, 'sb_q_norm': 2.265197e+00, 'sb_k_norm': 2.259877e+00, 'w_branch_a': 2.800091e-01, 'w_branch_b': 1.976840e-01, 'w_out': 3.342893e-01, 'ffn2_norm': 3.097391e+00, 'ffn2_w_in': 7.868897e-02, 'ffn2_w_out': 1.309881e-01}


def _to_microbatches(a, axis):
    t = _jnp.moveaxis(a, axis, 0)
    t = t.reshape((N_MICROBATCH, t.shape[0] // N_MICROBATCH) + t.shape[1:])
    return _jnp.moveaxis(t, 1, axis + 1)


def setup_inputs(seed: int = 0) -> dict:
    inp = _fwd_setup_inputs(seed)
    key = _jax.random.fold_in(_jax.random.key(seed), 7919)
    shape, _ = _output_shape()
    out = dict(inp)
    out["loss_target"] = _jax.random.normal(_jax.random.fold_in(key, 0), shape, _jnp.float32)
    for i, name in enumerate(TWIN_WEIGHTS):
        w = inp[name].astype(_jnp.float32)
        if MOMENT_SCALE is None:
            s = _jnp.sqrt(_jnp.mean(_jnp.square(w)) + 1e-30)
        else:
            s = MOMENT_SCALE[name]
        km, kv = _jax.random.split(_jax.random.fold_in(key, i + 1))
        out[name] = w
        out["m_" + name] = s * _jax.random.normal(km, w.shape, _jnp.float32)
        out["v_" + name] = (s * s) * _jax.random.uniform(kv, w.shape, _jnp.float32, 0.5, 1.5)
    if N_MICROBATCH > 1:
        for name, axis in PER_EXAMPLE_BATCH_AXIS.items():
            out[name] = _to_microbatches(out[name], axis)
    return {'x': out['x'], 'ffn1_norm': out['ffn1_norm'], 'ffn1_w_in': out['ffn1_w_in'], 'ffn1_w_out': out['ffn1_w_out'], 'mix_norm': out['mix_norm'], 'w_in': out['w_in'], 'dn_conv_w': out['dn_conv_w'], 'dn_a_log': out['dn_a_log'], 'dn_dt_bias': out['dn_dt_bias'], 'dn_out_norm': out['dn_out_norm'], 'sb_q_norm': out['sb_q_norm'], 'sb_k_norm': out['sb_k_norm'], 'w_branch_a': out['w_branch_a'], 'w_branch_b': out['w_branch_b'], 'w_out': out['w_out'], 'ffn2_norm': out['ffn2_norm'], 'ffn2_w_in': out['ffn2_w_in'], 'ffn2_w_out': out['ffn2_w_out'], 'loss_target': out['loss_target'], 'm_ffn1_norm': out['m_ffn1_norm'], 'm_ffn1_w_in': out['m_ffn1_w_in'], 'm_ffn1_w_out': out['m_ffn1_w_out'], 'm_mix_norm': out['m_mix_norm'], 'm_w_in': out['m_w_in'], 'm_dn_conv_w': out['m_dn_conv_w'], 'm_dn_a_log': out['m_dn_a_log'], 'm_dn_dt_bias': out['m_dn_dt_bias'], 'm_dn_out_norm': out['m_dn_out_norm'], 'm_sb_q_norm': out['m_sb_q_norm'], 'm_sb_k_norm': out['m_sb_k_norm'], 'm_w_branch_a': out['m_w_branch_a'], 'm_w_branch_b': out['m_w_branch_b'], 'm_w_out': out['m_w_out'], 'm_ffn2_norm': out['m_ffn2_norm'], 'm_ffn2_w_in': out['m_ffn2_w_in'], 'm_ffn2_w_out': out['m_ffn2_w_out'], 'v_ffn1_norm': out['v_ffn1_norm'], 'v_ffn1_w_in': out['v_ffn1_w_in'], 'v_ffn1_w_out': out['v_ffn1_w_out'], 'v_mix_norm': out['v_mix_norm'], 'v_w_in': out['v_w_in'], 'v_dn_conv_w': out['v_dn_conv_w'], 'v_dn_a_log': out['v_dn_a_log'], 'v_dn_dt_bias': out['v_dn_dt_bias'], 'v_dn_out_norm': out['v_dn_out_norm'], 'v_sb_q_norm': out['v_sb_q_norm'], 'v_sb_k_norm': out['v_sb_k_norm'], 'v_w_branch_a': out['v_w_branch_a'], 'v_w_branch_b': out['v_w_branch_b'], 'v_w_out': out['v_w_out'], 'v_ffn2_norm': out['v_ffn2_norm'], 'v_ffn2_w_in': out['v_ffn2_w_in'], 'v_ffn2_w_out': out['v_ffn2_w_out']}


def _loss(weights, diff, rest, loss_target):
    with _jax.named_scope("forward"):
        args = {**rest, TWIN_DIFF_INPUT: diff, **{k: w.astype(_WEIGHT_DTYPES[k]) for k, w in weights.items()}}
        y = _forward(args)
    with _jax.named_scope("loss_head"):
        err = _jnp.square(y.astype(_jnp.float32) - loss_target)
        return 0.5 * _jnp.sum(_jnp.mean(err, axis=-1)) if err.ndim else 0.5 * err


def _adamw(w, g, m, v):
    m = ADAM_B1 * m + (1.0 - ADAM_B1) * g
    v = ADAM_B2 * v + (1.0 - ADAM_B2) * _jnp.square(g)
    m_hat = m / (1.0 - ADAM_B1 ** ADAM_STEP)
    v_hat = v / (1.0 - ADAM_B2 ** ADAM_STEP)
    delta = -ADAM_LR * (m_hat / (_jnp.sqrt(v_hat) + ADAM_EPS) + ADAM_WD * w)
    return delta, m, v


def reference(x, ffn1_norm, ffn1_w_in, ffn1_w_out, mix_norm, w_in, dn_conv_w, dn_a_log, dn_dt_bias, dn_out_norm, sb_q_norm, sb_k_norm, w_branch_a, w_branch_b, w_out, ffn2_norm, ffn2_w_in, ffn2_w_out, loss_target, m_ffn1_norm, m_ffn1_w_in, m_ffn1_w_out, m_mix_norm, m_w_in, m_dn_conv_w, m_dn_a_log, m_dn_dt_bias, m_dn_out_norm, m_sb_q_norm, m_sb_k_norm, m_w_branch_a, m_w_branch_b, m_w_out, m_ffn2_norm, m_ffn2_w_in, m_ffn2_w_out, v_ffn1_norm, v_ffn1_w_in, v_ffn1_w_out, v_mix_norm, v_w_in, v_dn_conv_w, v_dn_a_log, v_dn_dt_bias, v_dn_out_norm, v_sb_q_norm, v_sb_k_norm, v_w_branch_a, v_w_branch_b, v_w_out, v_ffn2_norm, v_ffn2_w_in, v_ffn2_w_out):
    given = dict(x=x, ffn1_norm=ffn1_norm, ffn1_w_in=ffn1_w_in, ffn1_w_out=ffn1_w_out, mix_norm=mix_norm, w_in=w_in, dn_conv_w=dn_conv_w, dn_a_log=dn_a_log, dn_dt_bias=dn_dt_bias, dn_out_norm=dn_out_norm, sb_q_norm=sb_q_norm, sb_k_norm=sb_k_norm, w_branch_a=w_branch_a, w_branch_b=w_branch_b, w_out=w_out, ffn2_norm=ffn2_norm, ffn2_w_in=ffn2_w_in, ffn2_w_out=ffn2_w_out, loss_target=loss_target, m_ffn1_norm=m_ffn1_norm, m_ffn1_w_in=m_ffn1_w_in, m_ffn1_w_out=m_ffn1_w_out, m_mix_norm=m_mix_norm, m_w_in=m_w_in, m_dn_conv_w=m_dn_conv_w, m_dn_a_log=m_dn_a_log, m_dn_dt_bias=m_dn_dt_bias, m_dn_out_norm=m_dn_out_norm, m_sb_q_norm=m_sb_q_norm, m_sb_k_norm=m_sb_k_norm, m_w_branch_a=m_w_branch_a, m_w_branch_b=m_w_branch_b, m_w_out=m_w_out, m_ffn2_norm=m_ffn2_norm, m_ffn2_w_in=m_ffn2_w_in, m_ffn2_w_out=m_ffn2_w_out, v_ffn1_norm=v_ffn1_norm, v_ffn1_w_in=v_ffn1_w_in, v_ffn1_w_out=v_ffn1_w_out, v_mix_norm=v_mix_norm, v_w_in=v_w_in, v_dn_conv_w=v_dn_conv_w, v_dn_a_log=v_dn_a_log, v_dn_dt_bias=v_dn_dt_bias, v_dn_out_norm=v_dn_out_norm, v_sb_q_norm=v_sb_q_norm, v_sb_k_norm=v_sb_k_norm, v_w_branch_a=v_w_branch_a, v_w_branch_b=v_w_branch_b, v_w_out=v_w_out, v_ffn2_norm=v_ffn2_norm, v_ffn2_w_in=v_ffn2_w_in, v_ffn2_w_out=v_ffn2_w_out)
    weights = {n: given[n] for n in TWIN_WEIGHTS}
    shared = {n: given[n] for n in SHARED_INPUTS}
    per_example = {n: given[n] for n in ['x']}
    grad_fn = _jax.value_and_grad(_loss, argnums=(0, 1))

    def one_microbatch(ex, loss_target):
        ex = dict(ex)
        diff = ex.pop(TWIN_DIFF_INPUT)
        return grad_fn(weights, diff, {**shared, **ex}, loss_target)

    if N_MICROBATCH == 1:
        loss, (grad_w, grad_x) = one_microbatch(per_example, given["loss_target"])
    else:
        def body(carry, xs):
            loss_sum, grad_sum = carry
            l_k, (gw_k, gx_k) = one_microbatch(xs[0], xs[1])
            with _jax.named_scope("update"):
                return (loss_sum + l_k, _jax.tree.map(_jnp.add, grad_sum, gw_k)), gx_k

        init = (_jnp.zeros((), _jnp.float32), _jax.tree.map(_jnp.zeros_like, weights))
        (loss, grad_w), grad_x = _jax.lax.scan(body, init, (per_example, given["loss_target"]))
    with _jax.named_scope("update"):
        delta_w, new_m, new_v = {}, {}, {}
        for n in TWIN_WEIGHTS:
            delta_w[n], new_m[n], new_v[n] = _adamw(weights[n], grad_w[n], given["m_" + n], given["v_" + n])
    return (loss, grad_x, *[grad_w[n] for n in TWIN_WEIGHTS], *[delta_w[n] for n in TWIN_WEIGHTS],
            *[new_m[n] for n in TWIN_WEIGHTS], *[new_v[n] for n in TWIN_WEIGHTS])
```

```python
import functools
import math

import jax
import jax.numpy as jnp
import numpy as np
from jax import lax
from jax.experimental import pallas as pl
from jax.experimental.pallas import tpu as pltpu

F32 = jnp.float32
BF16 = jnp.bfloat16

LANES = 128
HEAD_DIM = 128
DN_CHUNK = 64
DN_CONV = 4
CONV_ROWS = 8
SB_BLOCK = 128
FFN_HALF = 0.5
RMS_EPS = 1e-6
L2_EPS = 1e-6
NEG_BIG = -1e30
ADAM_LR = 0.001
ADAM_B1 = 0.9
ADAM_B2 = 0.999
ADAM_EPS = 1e-08
ADAM_WD = 0.01
ADAM_STEP = 10
VMEM_LIMIT = 56 * 1024 * 1024
PACK_WIDTH = 1024
N_CHIPS = 4
N_DEV = 8
MESH = pl.DeviceIdType.MESH

BIG = ("ffn1_w_in", "ffn1_w_out", "w_in", "w_branch_a", "w_branch_b", "w_out", "ffn2_w_in", "ffn2_w_out")
COL_SHARDED = ("ffn1_w_in", "w_in", "ffn2_w_in")
SMALL = ("ffn1_norm", "mix_norm", "dn_a_log", "dn_dt_bias", "dn_out_norm", "sb_q_norm", "sb_k_norm", "ffn2_norm")
WEIGHTS = ("ffn1_norm", "ffn1_w_in", "ffn1_w_out", "mix_norm", "w_in", "dn_conv_w", "dn_a_log", "dn_dt_bias",
           "dn_out_norm", "sb_q_norm", "sb_k_norm", "w_branch_a", "w_branch_b", "w_out", "ffn2_norm", "ffn2_w_in",
           "ffn2_w_out")


def _params(**kw):
    return pltpu.CompilerParams(vmem_limit_bytes=VMEM_LIMIT, **kw)


def _pick(n, options):
    for o in options:
        if n % o == 0:
            return o
    return n


def _const_spec(shape, single=False):
    nd = len(shape)
    if single:
        return pl.BlockSpec(shape, lambda *_: (0,) * nd, pipeline_mode=pl.Buffered(1))
    return pl.BlockSpec(shape, lambda *_: (0,) * nd)


_NN = ((1,), (0,))
_NT = ((1,), (1,))
_TN = ((0,), (0,))


def _dot(a, b, dims):
    return lax.dot_general(a.astype(BF16), b.astype(BF16), (dims, ((), ())), preferred_element_type=F32)


def _mm_nn(a, b):
    return _dot(a, b, _NN)


def _mm_nt(a, b):
    return _dot(a, b, _NT)


def _mm_tn(a, b):
    return _dot(a, b, _TN)


def _split(a):
    hi = a.astype(BF16)
    lo = (a - hi.astype(F32)).astype(BF16)
    return hi, lo


def _dot_precise(a, b, dims):
    dn = (dims, ((), ()))
    ah, al = _split(a)
    bh, bl = _split(b)
    out = lax.dot_general(ah, bh, dn, preferred_element_type=F32)
    out = out + lax.dot_general(ah, bl, dn, preferred_element_type=F32)
    return out + lax.dot_general(al, bh, dn, preferred_element_type=F32)


def _make_diff_mm(dot):
    @jax.custom_vjp
    def nn(a, b):
        return dot(a, b, _NN)

    @jax.custom_vjp
    def nt(a, b):
        return dot(a, b, _NT)

    @jax.custom_vjp
    def tn(a, b):
        return dot(a, b, _TN)

    nn.defvjp(lambda a, b: (dot(a, b, _NN), (a, b)), lambda r, g: (nt(g, r[1]), tn(r[0], g)))
    nt.defvjp(lambda a, b: (dot(a, b, _NT), (a, b)), lambda r, g: (nn(g, r[1]), tn(g, r[0])))
    tn.defvjp(lambda a, b: (dot(a, b, _TN), (a, b)), lambda r, g: (nt(r[1], g), nn(r[0], g)))
    return nn, nt, tn


_d_nn, _d_nt, _d_tn = _make_diff_mm(_dot)
_p_nn, _p_nt, _p_tn = _make_diff_mm(_dot_precise)


def _softplus_raw(x):
    return jnp.maximum(x, 0.0) + jnp.log(1.0 + jnp.exp(-jnp.abs(x)))


@jax.custom_vjp
def _softplus(x):
    return _softplus_raw(x)


_softplus.defvjp(lambda x: (_softplus_raw(x), x), lambda x, g: (g * jax.nn.sigmoid(x),))


def _rms(x, gain, eps):
    return x * lax.rsqrt(jnp.mean(x * x, axis=-1, keepdims=True) + eps) * gain


def _silu(x):
    return x * jax.nn.sigmoid(x)


def _shift_rows_raw(x, k, down):
    n = x.shape[0]
    row = lax.broadcasted_iota(jnp.int32, x.shape, 0)
    if down:
        return jnp.where(row >= k, pltpu.roll(x, k, 0), 0.0)
    return jnp.where(row < n - k, pltpu.roll(x, n - k, 0), 0.0)


@functools.partial(jax.custom_vjp, nondiff_argnums=(1,))
def _shift_down(x, k):
    return _shift_rows_raw(x, k, True)


_shift_down.defvjp(lambda x, k: (_shift_rows_raw(x, k, True), None),
                   lambda k, _, g: (_shift_rows_raw(g, k, False),))


def _ffn_fwd(x, gain, wg, wu, wo):
    t, d = x.shape
    f = wo.shape[0]
    fc = _pick(f, (256, 128))
    rt = _pick(t, (512, 256, 128))

    def body(x_ref, g_ref, wg_ref, wu_ref, wo_ref, o_ref, hs_ref):
        @pl.when(pl.program_id(0) == 0)
        def _():
            for r in range(t // rt):
                rows = pl.ds(r * rt, rt)
                xr = x_ref[rows, :]
                hs_ref[rows, :] = _rms(xr, g_ref[...], RMS_EPS).astype(BF16)
                o_ref[rows, :] = xr

        for r in range(t // rt):
            rows = pl.ds(r * rt, rt)
            h = hs_ref[rows, :]
            a = _mm_nn(h, wg_ref[...])
            b = _mm_nn(h, wu_ref[...])
            o_ref[rows, :] += FFN_HALF * _mm_nn(_silu(a) * b, wo_ref[...])

    return pl.pallas_call(
        body, name="ffn_fwd", grid=(f // fc,),
        in_specs=[_const_spec((t, d), True), _const_spec((1, d)),
                  pl.BlockSpec((d, fc), lambda j: (0, j)), pl.BlockSpec((d, fc), lambda j: (0, j)),
                  pl.BlockSpec((fc, d), lambda j: (j, 0))],
        out_specs=_const_spec((t, d)),
        out_shape=jax.ShapeDtypeStruct((t, d), F32),
        scratch_shapes=[pltpu.VMEM((t, d), BF16)],
        compiler_params=_params(dimension_semantics=("arbitrary",)),
    )(x, gain, wg, wu, wo)


def _ffn_bwd(x, gain, dy, wg, wu, wo):
    t, d = x.shape
    f = wo.shape[0]
    fc = _pick(f, (256, 128))
    nj = f // fc
    rt = _pick(t, (512, 256, 128))
    nr = t // rt

    def body(x_ref, g_ref, dy_ref, wg_ref, wu_ref, wo_ref, dx_ref, dg_ref, dwg_ref, dwu_ref, dwo_ref, hs_ref):
        j = pl.program_id(0)

        @pl.when(j == 0)
        def _():
            for r in range(nr):
                rows = pl.ds(r * rt, rt)
                hs_ref[rows, :] = _rms(x_ref[rows, :], g_ref[...], RMS_EPS).astype(BF16)
                dx_ref[rows, :] = jnp.zeros((rt, d), F32)

        for r in range(nr):
            rows = pl.ds(r * rt, rt)
            h = hs_ref[rows, :]
            dy2 = (FFN_HALF * dy_ref[rows, :]).astype(BF16)
            a = _mm_nn(h, wg_ref[...])
            b = _mm_nn(h, wu_ref[...])
            sig = jax.nn.sigmoid(a)
            sa = a * sig
            ds = _mm_nt(dy2, wo_ref[...])
            da = ds * b * (sig * (1.0 + a * (1.0 - sig)))
            db = ds * sa
            dx_ref[rows, :] += _mm_nt(da, wg_ref[...]) + _mm_nt(db, wu_ref[...])
            dwo_c = _mm_tn(sa * b, dy2)
            dwg_c = _mm_tn(h, da)
            dwu_c = _mm_tn(h, db)
            if r == 0:
                dwo_ref[...] = dwo_c
                dwg_ref[...] = dwg_c
                dwu_ref[...] = dwu_c
            else:
                dwo_ref[...] += dwo_c
                dwg_ref[...] += dwg_c
                dwu_ref[...] += dwu_c

        @pl.when(j == nj - 1)
        def _():
            for r in range(nr):
                rows = pl.ds(r * rt, rt)
                _, vjp = jax.vjp(lambda xx, gg: _rms(xx, gg, RMS_EPS), x_ref[rows, :], g_ref[...])
                dxn, dgr = vjp(dx_ref[rows, :])
                dx_ref[rows, :] = dy_ref[rows, :] + dxn
                if r == 0:
                    dg_ref[...] = dgr
                else:
                    dg_ref[...] += dgr

    return pl.pallas_call(
        body, name="ffn_bwd", grid=(nj,),
        in_specs=[_const_spec((t, d), True), _const_spec((1, d)), _const_spec((t, d), True),
                  pl.BlockSpec((d, fc), lambda j: (0, j)), pl.BlockSpec((d, fc), lambda j: (0, j)),
                  pl.BlockSpec((fc, d), lambda j: (j, 0))],
        out_specs=[_const_spec((t, d)), _const_spec((1, d)),
                   pl.BlockSpec((d, fc), lambda j: (0, j)), pl.BlockSpec((d, fc), lambda j: (0, j)),
                   pl.BlockSpec((fc, d), lambda j: (j, 0))],
        out_shape=[jax.ShapeDtypeStruct((t, d), F32), jax.ShapeDtypeStruct((1, d), F32),
                   jax.ShapeDtypeStruct((d, f), F32), jax.ShapeDtypeStruct((d, f), F32),
                   jax.ShapeDtypeStruct((f, d), F32)],
        scratch_shapes=[pltpu.VMEM((t, d), BF16)],
        compiler_params=_params(dimension_semantics=("arbitrary",)),
    )(x, gain, dy, wg, wu, wo)


def _proj_fwd(x, gain, w, wba):
    t, d = x.shape
    n = w.shape[1]
    nc = _pick(n, (512, 256, 128))
    rt = _pick(t, (512, 256, 128))

    def body(x_ref, g_ref, w_ref, wba_ref, p_ref, ba_ref, hs_ref):
        @pl.when(pl.program_id(0) == 0)
        def _():
            for r in range(t // rt):
                rows = pl.ds(r * rt, rt)
                h = _rms(x_ref[rows, :], g_ref[...], RMS_EPS).astype(BF16)
                hs_ref[rows, :] = h
                ba_ref[rows, :] = _mm_nn(h, wba_ref[...])

        for r in range(t // rt):
            rows = pl.ds(r * rt, rt)
            p_ref[rows, :] = _mm_nn(hs_ref[rows, :], w_ref[...])

    return pl.pallas_call(
        body, name="proj_fwd", grid=(n // nc,),
        in_specs=[_const_spec((t, d), True), _const_spec((1, d)),
                  pl.BlockSpec((d, nc), lambda j: (0, j)), _const_spec((d, LANES))],
        out_specs=[pl.BlockSpec((t, nc), lambda j: (0, j)), _const_spec((t, LANES))],
        out_shape=[jax.ShapeDtypeStruct((t, n), F32), jax.ShapeDtypeStruct((t, LANES), F32)],
        scratch_shapes=[pltpu.VMEM((t, d), BF16)],
        compiler_params=_params(dimension_semantics=("arbitrary",)),
    )(x, gain, w, wba)


def _proj_bwd(x, gain, dres, dp, dba, w, wba):
    t, d = x.shape
    n = w.shape[1]
    nc = _pick(n, (512, 256, 128))
    nj = n // nc
    rt = _pick(t, (512, 256, 128))
    nr = t // rt

    def body(x_ref, g_ref, dres_ref, dp_ref, dba_ref, w_ref, wba_ref, dx_ref, dg_ref, dw_ref, dwba_ref, hs_ref):
        j = pl.program_id(0)

        @pl.when(j == 0)
        def _():
            for r in range(nr):
                rows = pl.ds(r * rt, rt)
                h = _rms(x_ref[rows, :], g_ref[...], RMS_EPS).astype(BF16)
                hs_ref[rows, :] = h
                g = dba_ref[rows, :]
                dx_ref[rows, :] = _mm_nt(g, wba_ref[...])
                if r == 0:
                    dwba_ref[...] = _mm_tn(h, g)
                else:
                    dwba_ref[...] += _mm_tn(h, g)

        for r in range(nr):
            rows = pl.ds(r * rt, rt)
            g = dp_ref[rows, :]
            dx_ref[rows, :] += _mm_nt(g, w_ref[...])
            if r == 0:
                dw_ref[...] = _mm_tn(hs_ref[rows, :], g)
            else:
                dw_ref[...] += _mm_tn(hs_ref[rows, :], g)

        @pl.when(j == nj - 1)
        def _():
            for r in range(nr):
                rows = pl.ds(r * rt, rt)
                _, vjp = jax.vjp(lambda xx, gg: _rms(xx, gg, RMS_EPS), x_ref[rows, :], g_ref[...])
                dxn, dgr = vjp(dx_ref[rows, :])
                dx_ref[rows, :] = dres_ref[rows, :] + dxn
                if r == 0:
                    dg_ref[...] = dgr
                else:
                    dg_ref[...] += dgr

    return pl.pallas_call(
        body, name="proj_bwd", grid=(nj,),
        in_specs=[_const_spec((t, d), True), _const_spec((1, d)), _const_spec((t, d), True),
                  pl.BlockSpec((t, nc), lambda j: (0, j)), _const_spec((t, LANES)),
                  pl.BlockSpec((d, nc), lambda j: (0, j)), _const_spec((d, LANES))],
        out_specs=[_const_spec((t, d)), _const_spec((1, d)),
                   pl.BlockSpec((d, nc), lambda j: (0, j)), _const_spec((d, LANES))],
        out_shape=[jax.ShapeDtypeStruct((t, d), F32), jax.ShapeDtypeStruct((1, d), F32),
                   jax.ShapeDtypeStruct((d, n), F32), jax.ShapeDtypeStruct((d, LANES), F32)],
        scratch_shapes=[pltpu.VMEM((t, d), BF16)],
        compiler_params=_params(dimension_semantics=("arbitrary",)),
    )(x, gain, dres, dp, dba, w, wba)


def _conv_act(x, w0, w1, w2, w3, is_qk):
    y = w3 * x + w2 * _shift_down(x, 1) + w1 * _shift_down(x, 2) + w0 * _shift_down(x, 3)
    y = _silu(y)
    inv = lax.rsqrt(jnp.sum(y * y, axis=-1, keepdims=True) + L2_EPS)
    return y * (is_qk * inv + (1.0 - is_qk))


def _taps(w_ref):
    return tuple(w_ref[i:i + 1, :] for i in range(DN_CONV))


def _dn_prep_fwd(proj, conv_w, n_heads):
    t = proj.shape[0]
    nb = 3 * n_heads

    def body(x_ref, w_ref, o_ref):
        is_qk = jnp.where(pl.program_id(0) < 2 * n_heads, 1.0, 0.0).astype(F32)
        o_ref[...] = _conv_act(x_ref[...], *_taps(w_ref), is_qk)

    return pl.pallas_call(
        body, name="dn_prep_fwd", grid=(nb,),
        in_specs=[pl.BlockSpec((t, HEAD_DIM), lambda i: (0, i)), pl.BlockSpec((CONV_ROWS, HEAD_DIM), lambda i: (0, i))],
        out_specs=pl.BlockSpec((t, HEAD_DIM), lambda i: (0, i)),
        out_shape=jax.ShapeDtypeStruct((t, nb * HEAD_DIM), F32),
        compiler_params=_params(dimension_semantics=("arbitrary",)),
    )(proj, conv_w)


def _dn_prep_bwd(proj, conv_w, dact, n_heads):
    t = proj.shape[0]
    nb = 3 * n_heads

    def body(x_ref, w_ref, g_ref, dx_ref, dw_ref):
        is_qk = jnp.where(pl.program_id(0) < 2 * n_heads, 1.0, 0.0).astype(F32)
        _, vjp = jax.vjp(lambda x, a, b, c, e: _conv_act(x, a, b, c, e, is_qk), x_ref[...], *_taps(w_ref))
        dx, d0, d1, d2, d3 = vjp(g_ref[...])
        dx_ref[...] = dx.astype(BF16)
        dw_ref[...] = jnp.concatenate([d0, d1, d2, d3, jnp.zeros((CONV_ROWS - DN_CONV, HEAD_DIM), F32)], axis=0)

    return pl.pallas_call(
        body, name="dn_prep_bwd", grid=(nb,),
        in_specs=[pl.BlockSpec((t, HEAD_DIM), lambda i: (0, i)), pl.BlockSpec((CONV_ROWS, HEAD_DIM), lambda i: (0, i)),
                  pl.BlockSpec((t, HEAD_DIM), lambda i: (0, i))],
        out_specs=[pl.BlockSpec((t, HEAD_DIM), lambda i: (0, i)), pl.BlockSpec((CONV_ROWS, HEAD_DIM), lambda i: (0, i))],
        out_shape=[jax.ShapeDtypeStruct((t, nb * HEAD_DIM), BF16), jax.ShapeDtypeStruct((CONV_ROWS, nb * HEAD_DIM), F32)],
        compiler_params=_params(dimension_semantics=("arbitrary",)),
    )(proj, conv_w, dact)


def _unit_lower_inverse(lmat, c):
    r = lax.broadcasted_iota(jnp.int32, (c, c), 0)
    q = lax.broadcasted_iota(jnp.int32, (c, c), 1)
    p = jnp.where(r == q, 1.0, 0.0) - lmat
    m = _p_nn(lmat, lmat)
    n = 2
    while True:
        p = p + _p_nn(p, m)
        if 2 * n >= c:
            return p
        m = _p_nn(m, m)
        n *= 2


def _delta_head(h, n_heads, q, k, v, bg, alog, dtb, state):
    c = q.shape[0]
    lane = lax.broadcasted_iota(jnp.int32, (c, LANES), 1)
    beta_all = jax.nn.sigmoid(bg)
    g_all = -jnp.exp(alog) * _softplus(bg + dtb)
    beta = jnp.sum(jnp.where(lane == h, beta_all, 0.0), axis=1, keepdims=True)
    g = jnp.sum(jnp.where(lane == n_heads + h, g_all, 0.0), axis=1, keepdims=True)
    r = lax.broadcasted_iota(jnp.int32, (c, c), 0)
    s = lax.broadcasted_iota(jnp.int32, (c, c), 1)
    g_row = jnp.sum(jnp.where(r == s, g, 0.0), axis=0, keepdims=True)
    gc = jnp.sum(jnp.where(s <= r, g_row, 0.0), axis=1, keepdims=True)
    gr = jnp.sum(jnp.where(r <= s, g, 0.0), axis=0, keepdims=True)
    g_last = jnp.sum(g, axis=0, keepdims=True)
    decay = jnp.exp(jnp.where(r >= s, gc - gr, NEG_BIG))
    qs = q * (HEAD_DIM ** -0.5)
    k_beta = k * beta
    lmat = jnp.where(r > s, _d_nt(k_beta, k) * decay, 0.0)
    tinv = _unit_lower_inverse(lmat, c)
    u = _p_nn(tinv, v * beta)
    w = _p_nn(tinv, k_beta * jnp.exp(gc))
    attn = _d_nt(qs, k) * decay
    v_new = u - _d_nn(w, state)
    o = _d_nn(qs * jnp.exp(gc), state) + _d_nn(attn, v_new)
    new_state = state * jnp.exp(g_last) + _d_tn(k * jnp.exp(g_last - gc), v_new)
    return o, new_state


def _delta_fwd(act, ba, alog, dtb, n_heads):
    t = act.shape[0]
    d = n_heads * HEAD_DIM
    c = DN_CHUNK
    nc = t // c

    def body(q_ref, k_ref, v_ref, bg_ref, al_ref, dt_ref, o_ref, snap_ref, st_ref):
        @pl.when(pl.program_id(0) == 0)
        def _():
            st_ref[...] = jnp.zeros(st_ref.shape, F32)

        snap_ref[0] = st_ref[...]
        for h in range(n_heads):
            sl = slice(h * HEAD_DIM, (h + 1) * HEAD_DIM)
            o, ns = _delta_head(h, n_heads, q_ref[:, sl], k_ref[:, sl], v_ref[:, sl], bg_ref[...],
                                al_ref[...], dt_ref[...], st_ref[h])
            o_ref[:, sl] = o
            st_ref[h] = ns

    return pl.pallas_call(
        body, name="delta_fwd", grid=(nc,),
        in_specs=[pl.BlockSpec((c, d), lambda i: (i, 0)), pl.BlockSpec((c, d), lambda i: (i, 1)),
                  pl.BlockSpec((c, d), lambda i: (i, 2)), pl.BlockSpec((c, LANES), lambda i: (i, 0)),
                  _const_spec((1, LANES)), _const_spec((1, LANES))],
        out_specs=[pl.BlockSpec((c, d), lambda i: (i, 0)),
                   pl.BlockSpec((1, n_heads, HEAD_DIM, HEAD_DIM), lambda i: (i, 0, 0, 0))],
        out_shape=[jax.ShapeDtypeStruct((t, d), F32), jax.ShapeDtypeStruct((nc, n_heads, HEAD_DIM, HEAD_DIM), F32)],
        scratch_shapes=[pltpu.VMEM((n_heads, HEAD_DIM, HEAD_DIM), F32)],
        compiler_params=_params(dimension_semantics=("arbitrary",)),
    )(act, act, act, ba, alog, dtb)


def _delta_bwd(act, ba, alog, dtb, snaps, do, n_heads):
    t = act.shape[0]
    d = n_heads * HEAD_DIM
    c = DN_CHUNK
    nc = t // c

    def body(q_ref, k_ref, v_ref, bg_ref, al_ref, dt_ref, snap_ref, do_ref,
             dact_ref, dbg_ref, dal_ref, ddt_ref, ds_ref):
        @pl.when(pl.program_id(0) == 0)
        def _():
            ds_ref[...] = jnp.zeros(ds_ref.shape, F32)
            dal_ref[...] = jnp.zeros((1, LANES), F32)
            ddt_ref[...] = jnp.zeros((1, LANES), F32)

        dbg = jnp.zeros((c, LANES), F32)
        for h in range(n_heads):
            sl = slice(h * HEAD_DIM, (h + 1) * HEAD_DIM)
            _, vjp = jax.vjp(functools.partial(_delta_head, h, n_heads), q_ref[:, sl], k_ref[:, sl], v_ref[:, sl],
                             bg_ref[...], al_ref[...], dt_ref[...], snap_ref[0, h])
            dq, dk, dv, dbg_h, dal, ddt, dst = vjp((do_ref[:, sl], ds_ref[h]))
            dact_ref[:, h * HEAD_DIM:(h + 1) * HEAD_DIM] = dq
            dact_ref[:, d + h * HEAD_DIM:d + (h + 1) * HEAD_DIM] = dk
            dact_ref[:, 2 * d + h * HEAD_DIM:2 * d + (h + 1) * HEAD_DIM] = dv
            dbg = dbg + dbg_h
            dal_ref[...] += dal
            ddt_ref[...] += ddt
            ds_ref[h] = dst
        dbg_ref[...] = dbg.astype(BF16)

    rev = lambda i: nc - 1 - i
    return pl.pallas_call(
        body, name="delta_bwd", grid=(nc,),
        in_specs=[pl.BlockSpec((c, d), lambda i: (rev(i), 0)), pl.BlockSpec((c, d), lambda i: (rev(i), 1)),
                  pl.BlockSpec((c, d), lambda i: (rev(i), 2)), pl.BlockSpec((c, LANES), lambda i: (rev(i), 0)),
                  _const_spec((1, LANES)), _const_spec((1, LANES)),
                  pl.BlockSpec((1, n_heads, HEAD_DIM, HEAD_DIM), lambda i: (rev(i), 0, 0, 0)),
                  pl.BlockSpec((c, d), lambda i: (rev(i), 0))],
        out_specs=[pl.BlockSpec((c, 3 * d), lambda i: (rev(i), 0)), pl.BlockSpec((c, LANES), lambda i: (rev(i), 0)),
                   _const_spec((1, LANES)), _const_spec((1, LANES))],
        out_shape=[jax.ShapeDtypeStruct((t, 3 * d), F32), jax.ShapeDtypeStruct((t, LANES), BF16),
                   jax.ShapeDtypeStruct((1, LANES), F32), jax.ShapeDtypeStruct((1, LANES), F32)],
        scratch_shapes=[pltpu.VMEM((n_heads, HEAD_DIM, HEAD_DIM), F32)],
        compiler_params=_params(dimension_semantics=("arbitrary",)),
    )(act, act, act, ba, alog, dtb, snaps, do)


def _head_norm2(a, b, ga, gb):
    return _rms(a, ga, RMS_EPS), _rms(b, gb, RMS_EPS)


def _sb_prep_fwd(proj, qn, kn, n_heads):
    t = proj.shape[0]
    d = n_heads * HEAD_DIM
    tm = _pick(t, (256, 128))

    def body(q_ref, k_ref, qn_ref, kn_ref, qo_ref, ko_ref):
        for h in range(n_heads):
            sl = slice(h * HEAD_DIM, (h + 1) * HEAD_DIM)
            qo_ref[:, sl], ko_ref[:, sl] = _head_norm2(q_ref[:, sl], k_ref[:, sl], qn_ref[...], kn_ref[...])

    return pl.pallas_call(
        body, name="sb_prep_fwd", grid=(t // tm,),
        in_specs=[pl.BlockSpec((tm, d), lambda i: (i, 4)), pl.BlockSpec((tm, d), lambda i: (i, 5)),
                  _const_spec((1, HEAD_DIM)), _const_spec((1, HEAD_DIM))],
        out_specs=[pl.BlockSpec((tm, d), lambda i: (i, 0)), pl.BlockSpec((tm, d), lambda i: (i, 0))],
        out_shape=[jax.ShapeDtypeStruct((t, d), F32), jax.ShapeDtypeStruct((t, d), F32)],
        compiler_params=_params(dimension_semantics=("arbitrary",)),
    )(proj, proj, qn, kn)


def _sb_prep_bwd(proj, qn, kn, dq, dk, n_heads):
    t = proj.shape[0]
    d = n_heads * HEAD_DIM
    tm = _pick(t, (256, 128))

    def body(q_ref, k_ref, qn_ref, kn_ref, dq_ref, dk_ref, dqo_ref, dko_ref, dqn_ref, dkn_ref):
        @pl.when(pl.program_id(0) == 0)
        def _():
            dqn_ref[...] = jnp.zeros((1, HEAD_DIM), F32)
            dkn_ref[...] = jnp.zeros((1, HEAD_DIM), F32)

        for h in range(n_heads):
            sl = slice(h * HEAD_DIM, (h + 1) * HEAD_DIM)
            _, vjp = jax.vjp(_head_norm2, q_ref[:, sl], k_ref[:, sl], qn_ref[...], kn_ref[...])
            da, db, dga, dgb = vjp((dq_ref[:, sl], dk_ref[:, sl]))
            dqo_ref[:, sl] = da.astype(BF16)
            dko_ref[:, sl] = db.astype(BF16)
            dqn_ref[...] += dga
            dkn_ref[...] += dgb

    return pl.pallas_call(
        body, name="sb_prep_bwd", grid=(t // tm,),
        in_specs=[pl.BlockSpec((tm, d), lambda i: (i, 4)), pl.BlockSpec((tm, d), lambda i: (i, 5)),
                  _const_spec((1, HEAD_DIM)), _const_spec((1, HEAD_DIM)),
                  pl.BlockSpec((tm, d), lambda i: (i, 0)), pl.BlockSpec((tm, d), lambda i: (i, 0))],
        out_specs=[pl.BlockSpec((tm, d), lambda i: (i, 0)), pl.BlockSpec((tm, d), lambda i: (i, 0)),
                   _const_spec((1, HEAD_DIM)), _const_spec((1, HEAD_DIM))],
        out_shape=[jax.ShapeDtypeStruct((t, d), BF16), jax.ShapeDtypeStruct((t, d), BF16),
                   jax.ShapeDtypeStruct((1, HEAD_DIM), F32), jax.ShapeDtypeStruct((1, HEAD_DIM), F32)],
        compiler_params=_params(dimension_semantics=("arbitrary",)),
    )(proj, proj, qn, kn, dq, dk)


def _cumsum_mm(x, tri):
    hi, lo = _split(x)
    return (lax.dot_general(hi, tri, (_NN, ((), ())), preferred_element_type=F32)
            + lax.dot_general(lo, tri, (_NN, ((), ())), preferred_element_type=F32))


def _sb_scores(q, kj, i, j):
    row = lax.broadcasted_iota(jnp.int32, (SB_BLOCK, SB_BLOCK), 0)
    col = lax.broadcasted_iota(jnp.int32, (SB_BLOCK, SB_BLOCK), 1)
    z = _mm_nt(q, kj) * (HEAD_DIM ** -0.5)
    valid = (col + j * SB_BLOCK) < (row + i * SB_BLOCK)
    return z, valid, _softplus_raw(z)


def _sb_attn_fwd(qb, kb, proj, n_heads):
    t = qb.shape[0]
    d = n_heads * HEAD_DIM
    nq = t // SB_BLOCK
    v_col0 = 6 * n_heads

    def body(q_ref, k_ref, v_ref, o_ref, lt_ref):
        i = pl.program_id(1)
        q = q_ref[...]
        row = lax.broadcasted_iota(jnp.int32, (SB_BLOCK, SB_BLOCK), 0)
        col = lax.broadcasted_iota(jnp.int32, (SB_BLOCK, SB_BLOCK), 1)
        after = jnp.where(row > col, 1.0, 0.0).astype(BF16)

        def step(jj, carry):
            acc, later = carry
            j = i - jj
            rows = pl.ds(pl.multiple_of(j * SB_BLOCK, SB_BLOCK), SB_BLOCK)
            z, valid, sp = _sb_scores(q, k_ref[rows, :], i, j)
            lm = jnp.where(valid, -sp, 0.0)
            w = jnp.where(valid, jnp.exp(z - sp + later + _cumsum_mm(lm, after)), 0.0)
            return acc + _mm_nn(w, v_ref[rows, :]), later + jnp.sum(lm, axis=1, keepdims=True)

        acc, total = lax.fori_loop(0, i + 1, step, (jnp.zeros((SB_BLOCK, HEAD_DIM), F32), jnp.zeros((SB_BLOCK, 1), F32)))
        o_ref[...] = acc
        lt_ref[...] = jnp.broadcast_to(total, (SB_BLOCK, HEAD_DIM))

    return pl.pallas_call(
        body, name="sb_attn_fwd", grid=(n_heads, nq),
        in_specs=[pl.BlockSpec((SB_BLOCK, HEAD_DIM), lambda h, i: (i, h)),
                  pl.BlockSpec((t, HEAD_DIM), lambda h, i: (0, h)),
                  pl.BlockSpec((t, HEAD_DIM), lambda h, i: (0, v_col0 + h))],
        out_specs=[pl.BlockSpec((SB_BLOCK, HEAD_DIM), lambda h, i: (i, h)),
                   pl.BlockSpec((SB_BLOCK, HEAD_DIM), lambda h, i: (i, h))],
        out_shape=[jax.ShapeDtypeStruct((t, d), F32), jax.ShapeDtypeStruct((t, d), F32)],
        compiler_params=_params(dimension_semantics=("arbitrary", "arbitrary")),
    )(qb, kb, proj)


def _sb_attn_bwd(qb, kb, proj, ltot, do, n_heads):
    t = qb.shape[0]
    d = n_heads * HEAD_DIM
    nq = t // SB_BLOCK
    v_col0 = 6 * n_heads
    scale = HEAD_DIM ** -0.5

    def body(q_ref, k_ref, v_ref, lt_ref, do_ref, dq_ref, dk_ref, dv_ref):
        i = pl.program_id(1)

        @pl.when(i == 0)
        def _():
            dk_ref[...] = jnp.zeros((t, HEAD_DIM), F32)
            dv_ref[...] = jnp.zeros((t, HEAD_DIM), F32)

        q = q_ref[...]
        do = do_ref[...]
        total = jnp.max(lt_ref[...], axis=1, keepdims=True)
        row = lax.broadcasted_iota(jnp.int32, (SB_BLOCK, SB_BLOCK), 0)
        col = lax.broadcasted_iota(jnp.int32, (SB_BLOCK, SB_BLOCK), 1)
        upto = jnp.where(row <= col, 1.0, 0.0).astype(BF16)
        before = jnp.where(row < col, 1.0, 0.0).astype(BF16)

        def step(j, carry):
            dq, lm_before, e_before = carry
            rows = pl.ds(pl.multiple_of(j * SB_BLOCK, SB_BLOCK), SB_BLOCK)
            kj = k_ref[rows, :]
            vj = v_ref[rows, :]
            z, valid, sp = _sb_scores(q, kj, i, j)
            sig = jnp.exp(z - sp)
            lm = jnp.where(valid, -sp, 0.0)
            survive = total - (lm_before + _cumsum_mm(lm, upto))
            w = jnp.where(valid, jnp.exp(z - sp + survive), 0.0)
            e = w * _mm_nt(do, vj)
            dv_ref[rows, :] += _mm_tn(w, do)
            e_pre = e_before + _cumsum_mm(e, before)
            dz = jnp.where(valid, e * (1.0 - sig) - e_pre * sig, 0.0) * scale
            dk_ref[rows, :] += _mm_tn(dz, q)
            return (dq + _mm_nn(dz, kj), lm_before + jnp.sum(lm, axis=1, keepdims=True),
                    e_before + jnp.sum(e, axis=1, keepdims=True))

        zero_col = jnp.zeros((SB_BLOCK, 1), F32)
        dq, _, _ = lax.fori_loop(0, i + 1, step, (jnp.zeros((SB_BLOCK, HEAD_DIM), F32), zero_col, zero_col))
        dq_ref[...] = dq

    return pl.pallas_call(
        body, name="sb_attn_bwd", grid=(n_heads, nq),
        in_specs=[pl.BlockSpec((SB_BLOCK, HEAD_DIM), lambda h, i: (i, h)),
                  pl.BlockSpec((t, HEAD_DIM), lambda h, i: (0, h)),
                  pl.BlockSpec((t, HEAD_DIM), lambda h, i: (0, v_col0 + h)),
                  pl.BlockSpec((SB_BLOCK, HEAD_DIM), lambda h, i: (i, h)),
                  pl.BlockSpec((SB_BLOCK, HEAD_DIM), lambda h, i: (i, h))],
        out_specs=[pl.BlockSpec((SB_BLOCK, HEAD_DIM), lambda h, i: (i, h)),
                   pl.BlockSpec((t, HEAD_DIM), lambda h, i: (0, h)),
                   pl.BlockSpec((t, HEAD_DIM), lambda h, i: (0, h))],
        out_shape=[jax.ShapeDtypeStruct((t, d), F32), jax.ShapeDtypeStruct((t, d), F32),
                   jax.ShapeDtypeStruct((t, d), F32)],
        compiler_params=_params(dimension_semantics=("arbitrary", "arbitrary")),
    )(qb, kb, proj, ltot, do)


def _gated_norm(oa, z, gn):
    return _rms(oa, gn, RMS_EPS) * _silu(z)


def _merge_gates(ya, yb, ga, gb):
    return jax.nn.sigmoid(ga) * ya + jax.nn.sigmoid(gb) * yb


def _merge_fwd(x1, oa, proj, ob, gn, wa, wb, wo, n_heads):
    t, d = x1.shape
    tm = _pick(t, (256, 128))

    def body(x_ref, oa_ref, z_ref, ob_ref, ga_ref, gb_ref, gn_ref, wa_ref, wb_ref, wo_ref, o_ref, na_ref):
        for h in range(n_heads):
            sl = slice(h * HEAD_DIM, (h + 1) * HEAD_DIM)
            na_ref[:, sl] = _gated_norm(oa_ref[:, sl], z_ref[:, sl], gn_ref[...]).astype(BF16)
        m = _merge_gates(_mm_nn(na_ref[...], wa_ref[...]), _mm_nn(ob_ref[...], wb_ref[...]), ga_ref[...], gb_ref[...])
        o_ref[...] = x_ref[...] + _mm_nn(m, wo_ref[...])

    tile = lambda k: pl.BlockSpec((tm, d), lambda i: (i, k))
    return pl.pallas_call(
        body, name="merge_fwd", grid=(t // tm,),
        in_specs=[tile(0), tile(0), tile(3), tile(0), tile(7), tile(8), _const_spec((1, HEAD_DIM)),
                  _const_spec((d, d), True), _const_spec((d, d), True), _const_spec((d, d), True)],
        out_specs=tile(0),
        out_shape=jax.ShapeDtypeStruct((t, d), F32),
        scratch_shapes=[pltpu.VMEM((tm, d), BF16)],
        compiler_params=_params(dimension_semantics=("arbitrary",)),
    )(x1, oa, proj, ob, proj, proj, gn, wa, wb, wo)


def _merge_bwd(oa, proj, ob, dy, gn, wa, wb, wo, n_heads):
    t, d = oa.shape
    tm = _pick(t, (256, 128))
    nt = t // tm

    def body(oa_ref, z_ref, ob_ref, ga_ref, gb_ref, dy_ref, gn_ref, wa_ref, wb_ref, wo_ref,
             doa_ref, dz_ref, dob_ref, dga_ref, dgb_ref, dgn_ref, dwa_hbm, dwb_hbm, dwo_hbm,
             na_ref, dna_ref, dwa_ref, dwb_ref, dwo_ref):
        i = pl.program_id(0)

        @pl.when(i == 0)
        def _():
            dgn_ref[...] = jnp.zeros((1, HEAD_DIM), F32)
            dwa_ref[...] = jnp.zeros((d, d), F32)
            dwb_ref[...] = jnp.zeros((d, d), F32)
            dwo_ref[...] = jnp.zeros((d, d), F32)

        for h in range(n_heads):
            sl = slice(h * HEAD_DIM, (h + 1) * HEAD_DIM)
            na_ref[:, sl] = _gated_norm(oa_ref[:, sl], z_ref[:, sl], gn_ref[...]).astype(BF16)
        dy = dy_ref[...].astype(BF16)
        ob = ob_ref[...].astype(BF16)
        ya = _mm_nn(na_ref[...], wa_ref[...])
        yb = _mm_nn(ob, wb_ref[...])
        m, vjp = jax.vjp(_merge_gates, ya, yb, ga_ref[...], gb_ref[...])
        dwo_ref[...] += _mm_tn(m, dy)
        dya, dyb, dga, dgb = vjp(_mm_nt(dy, wo_ref[...]))
        dga_ref[...] = dga.astype(BF16)
        dgb_ref[...] = dgb.astype(BF16)
        dwa_ref[...] += _mm_tn(na_ref[...], dya)
        dwb_ref[...] += _mm_tn(ob, dyb)
        dob_ref[...] = _mm_nt(dyb, wb_ref[...])
        dna_ref[...] = _mm_nt(dya, wa_ref[...])
        for h in range(n_heads):
            sl = slice(h * HEAD_DIM, (h + 1) * HEAD_DIM)
            _, vjp_h = jax.vjp(_gated_norm, oa_ref[:, sl], z_ref[:, sl], gn_ref[...])
            doa, dz, dgn = vjp_h(dna_ref[:, sl])
            doa_ref[:, sl] = doa
            dz_ref[:, sl] = dz.astype(BF16)
            dgn_ref[...] += dgn

        @pl.when(i == nt - 1)
        def _():
            pltpu.sync_copy(dwa_ref, dwa_hbm)
            pltpu.sync_copy(dwb_ref, dwb_hbm)
            pltpu.sync_copy(dwo_ref, dwo_hbm)

    tile = lambda k: pl.BlockSpec((tm, d), lambda i: (i, k))
    any_spec = pl.BlockSpec(memory_space=pl.ANY)
    return pl.pallas_call(
        body, name="merge_bwd", grid=(nt,),
        in_specs=[tile(0), tile(3), tile(0), tile(7), tile(8), tile(0), _const_spec((1, HEAD_DIM)),
                  _const_spec((d, d), True), _const_spec((d, d), True), _const_spec((d, d), True)],
        out_specs=[tile(0), tile(0), tile(0), tile(0), tile(0), _const_spec((1, HEAD_DIM)),
                   any_spec, any_spec, any_spec],
        out_shape=[jax.ShapeDtypeStruct((t, d), F32), jax.ShapeDtypeStruct((t, d), BF16),
                   jax.ShapeDtypeStruct((t, d), F32), jax.ShapeDtypeStruct((t, d), BF16),
                   jax.ShapeDtypeStruct((t, d), BF16), jax.ShapeDtypeStruct((1, HEAD_DIM), F32),
                   jax.ShapeDtypeStruct((d, d), F32), jax.ShapeDtypeStruct((d, d), F32),
                   jax.ShapeDtypeStruct((d, d), F32)],
        scratch_shapes=[pltpu.VMEM((tm, d), BF16), pltpu.VMEM((tm, d), F32),
                        pltpu.VMEM((d, d), F32), pltpu.VMEM((d, d), F32), pltpu.VMEM((d, d), F32)],
        compiler_params=_params(dimension_semantics=("arbitrary",)),
    )(oa, proj, ob, proj, proj, dy, gn, wa, wb, wo)


def _loss_head(y, target):
    t, d = y.shape
    tm = _pick(t, (256, 128))

    def body(y_ref, t_ref, dy_ref, loss_ref):
        @pl.when(pl.program_id(0) == 0)
        def _():
            loss_ref[...] = jnp.zeros((8, LANES), F32)

        err = y_ref[...] - t_ref[...]
        dy_ref[...] = err * (1.0 / d)
        per_token = jnp.sum(err * err, axis=1, keepdims=True) * (1.0 / d)
        loss_ref[...] += 0.5 * jnp.sum(per_token, axis=0, keepdims=True)

    return pl.pallas_call(
        body, name="loss_head", grid=(t // tm,),
        in_specs=[pl.BlockSpec((tm, d), lambda i: (i, 0)), pl.BlockSpec((tm, d), lambda i: (i, 0))],
        out_specs=[pl.BlockSpec((tm, d), lambda i: (i, 0)), _const_spec((8, LANES))],
        out_shape=[jax.ShapeDtypeStruct((t, d), F32), jax.ShapeDtypeStruct((8, LANES), F32)],
        compiler_params=_params(dimension_semantics=("arbitrary",)),
    )(y, target)


def _adamw(w, g, m, v):
    rows, cols = w.shape
    tr = rows
    for cand in (512, 256, 128, 64, 32, 16, 8):
        if rows % cand == 0 and cand * cols * 4 <= 2 * 1024 * 1024:
            tr = cand
            break

    def body(w_ref, g_ref, m_ref, v_ref, d_ref, mo_ref, vo_ref):
        g = g_ref[...]
        m2 = ADAM_B1 * m_ref[...] + (1.0 - ADAM_B1) * g
        v2 = ADAM_B2 * v_ref[...] + (1.0 - ADAM_B2) * (g * g)
        m_hat = m2 / (1.0 - ADAM_B1 ** ADAM_STEP)
        v_hat = v2 / (1.0 - ADAM_B2 ** ADAM_STEP)
        d_ref[...] = -ADAM_LR * (m_hat / (jnp.sqrt(v_hat) + ADAM_EPS) + ADAM_WD * w_ref[...])
        mo_ref[...] = m2
        vo_ref[...] = v2

    spec = pl.BlockSpec((tr, cols), lambda i: (i, 0))
    shape = jax.ShapeDtypeStruct((rows, cols), F32)
    return pl.pallas_call(
        body, name="adamw", grid=(rows // tr,), in_specs=[spec] * 4, out_specs=[spec] * 3,
        out_shape=[shape] * 3, compiler_params=_params(dimension_semantics=("arbitrary",)),
    )(w, g, m, v)


def _place():
    return lax.axis_index("x"), lax.axis_index("y"), lax.axis_index("c")


def _other_chips(x, y):
    return [(1 - x, y), (x, 1 - y), (1 - x, 1 - y)]


def _allgather_shards(mine):
    rows, width = mine.shape
    half_rows = rows // 2

    def body(src, out, send_sems, recv_sems, local_sem):
        x, y, c = _place()
        me, sibling = (x, y, c), (x, y, 1 - c)
        chips = _other_chips(x, y)
        my_half = pl.ds(pl.multiple_of(c * half_rows, 16), half_rows)
        other_half = pl.ds(pl.multiple_of((1 - c) * half_rows, 16), half_rows)

        def copy(k, shard, part, to, from_src=False):
            return pltpu.make_async_remote_copy(
                src_ref=src.at[part] if from_src else out.at[shard, part], dst_ref=out.at[shard, part],
                send_sem=send_sems.at[k], recv_sem=recv_sems.at[k], device_id=to, device_id_type=MESH)

        local = pltpu.make_async_copy(src, out.at[2 * x + y], local_sem)
        local.start()
        first = [copy(j, 2 * x + y, my_half, (*chip, c), from_src=True) for j, chip in enumerate(chips)]
        for cp in first:
            cp.start()
        passed = []
        for j, (cx, cy) in enumerate(chips):
            copy(j, 2 * cx + cy, my_half, me).wait_recv()
            cp = copy(3 + j, 2 * cx + cy, my_half, sibling)
            cp.start()
            passed.append(cp)
        for j, (cx, cy) in enumerate(chips):
            copy(3 + j, 2 * cx + cy, other_half, me).wait_recv()
        for cp in first + passed:
            cp.wait_send()
        local.wait()

    return pl.pallas_call(
        body, name="allgather_shards",
        in_specs=[pl.BlockSpec(memory_space=pl.ANY)], out_specs=pl.BlockSpec(memory_space=pl.ANY),
        out_shape=jax.ShapeDtypeStruct((N_CHIPS, rows, width), mine.dtype),
        scratch_shapes=[pltpu.SemaphoreType.DMA((6,)), pltpu.SemaphoreType.DMA((6,)), pltpu.SemaphoreType.DMA],
    )(mine)


def _swap_halves(grads):
    n, rows, width = grads.shape
    half_rows = rows // 2

    def body(g, got, send_sem, recv_sem):
        x, y, c = _place()
        theirs = pl.ds(pl.multiple_of((1 - c) * half_rows, 16), half_rows)
        cp = pltpu.make_async_remote_copy(src_ref=g.at[:, theirs], dst_ref=got, send_sem=send_sem, recv_sem=recv_sem,
                                          device_id=(x, y, 1 - c), device_id_type=MESH)
        cp.start()
        cp.wait()

    return pl.pallas_call(
        body, name="swap_halves",
        in_specs=[pl.BlockSpec(memory_space=pl.ANY)], out_specs=pl.BlockSpec(memory_space=pl.ANY),
        out_shape=jax.ShapeDtypeStruct((n, half_rows, width), grads.dtype),
        scratch_shapes=[pltpu.SemaphoreType.DMA, pltpu.SemaphoreType.DMA],
    )(grads)


def _add_halves(grads, got, c_idx):
    n, rows, width = grads.shape
    half_rows = rows // 2
    tr = _pick(half_rows, (512, 256, 128, 64, 32, 16))
    nb = half_rows // tr

    def body(c_ref, a_ref, b_ref, o_ref):
        o_ref[...] = (a_ref[...].astype(F32) + b_ref[...].astype(F32)).astype(o_ref.dtype)

    return pl.pallas_call(
        body, name="add_halves",
        grid_spec=pltpu.PrefetchScalarGridSpec(
            num_scalar_prefetch=1, grid=(n, nb),
            in_specs=[pl.BlockSpec((1, tr, width), lambda s, r, c_ref: (s, c_ref[0] * nb + r, 0)),
                      pl.BlockSpec((1, tr, width), lambda s, r, c_ref: (s, r, 0))],
            out_specs=pl.BlockSpec((1, tr, width), lambda s, r, c_ref: (s, r, 0))),
        out_shape=jax.ShapeDtypeStruct((n, half_rows, width), grads.dtype),
        compiler_params=_params(dimension_semantics=("arbitrary", "arbitrary")),
    )(c_idx, grads, got)


def _scatter_partials(part):
    n, half_rows, width = part.shape

    def body(p, got, send_sems, recv_sems):
        x, y, c = _place()
        copies = [pltpu.make_async_remote_copy(src_ref=p.at[2 * cx + cy], dst_ref=got.at[j], send_sem=send_sems.at[j],
                                               recv_sem=recv_sems.at[j], device_id=(cx, cy, c), device_id_type=MESH)
                  for j, (cx, cy) in enumerate(_other_chips(x, y))]
        for cp in copies:
            cp.start()
        for cp in copies:
            cp.wait()

    return pl.pallas_call(
        body, name="scatter_partials",
        in_specs=[pl.BlockSpec(memory_space=pl.ANY)], out_specs=pl.BlockSpec(memory_space=pl.ANY),
        out_shape=jax.ShapeDtypeStruct((n - 1, half_rows, width), part.dtype),
        scratch_shapes=[pltpu.SemaphoreType.DMA((3,)), pltpu.SemaphoreType.DMA((3,))],
    )(part)


def _sum_partials(part, got, s_idx):
    n, half_rows, width = part.shape
    tr = _pick(half_rows, (512, 256, 128, 64, 32, 16))

    def body(s_ref, a_ref, b_ref, o_ref):
        acc = a_ref[0].astype(F32)
        for j in range(n - 1):
            acc = acc + b_ref[j].astype(F32)
        o_ref[...] = acc

    return pl.pallas_call(
        body, name="sum_partials",
        grid_spec=pltpu.PrefetchScalarGridSpec(
            num_scalar_prefetch=1, grid=(half_rows // tr,),
            in_specs=[pl.BlockSpec((1, tr, width), lambda r, s_ref: (s_ref[0], r, 0)),
                      pl.BlockSpec((n - 1, tr, width), lambda r, s_ref: (0, r, 0))],
            out_specs=pl.BlockSpec((tr, width), lambda r, s_ref: (r, 0))),
        out_shape=jax.ShapeDtypeStruct((half_rows, width), F32),
        compiler_params=_params(dimension_semantics=("arbitrary",)),
    )(s_idx, part, got)


def _join_halves(mine):
    half_rows, width = mine.shape

    def body(src, out, send_sem, recv_sem, local_sem):
        x, y, c = _place()
        my_half = pl.ds(pl.multiple_of(c * half_rows, 8), half_rows)
        local = pltpu.make_async_copy(src, out.at[my_half], local_sem)
        local.start()
        cp = pltpu.make_async_remote_copy(src_ref=src, dst_ref=out.at[my_half], send_sem=send_sem, recv_sem=recv_sem,
                                          device_id=(x, y, 1 - c), device_id_type=MESH)
        cp.start()
        cp.wait()
        local.wait()

    return pl.pallas_call(
        body, name="join_halves",
        in_specs=[pl.BlockSpec(memory_space=pl.ANY)], out_specs=pl.BlockSpec(memory_space=pl.ANY),
        out_shape=jax.ShapeDtypeStruct((2 * half_rows, width), F32),
        scratch_shapes=[pltpu.SemaphoreType.DMA, pltpu.SemaphoreType.DMA, pltpu.SemaphoreType.DMA],
    )(mine)


def _allreduce_small(v, name):
    rows = v.shape[0]

    def body(v_ref, o_ref, gath, send_sems, recv_sems):
        x, y, c = _place()
        idx = 4 * x + 2 * y + c
        gath[0] = v_ref[...]
        copies = []
        for r in range(1, N_DEV):
            peer = (1 - x if r & 4 else x, 1 - y if r & 2 else y, 1 - c if r & 1 else c)
            cp = pltpu.make_async_remote_copy(src_ref=v_ref, dst_ref=gath.at[r], send_sem=send_sems.at[r - 1],
                                              recv_sem=recv_sems.at[r - 1], device_id=peer, device_id_type=MESH)
            cp.start()
            copies.append(cp)
        for cp in copies:
            cp.wait()
        acc = gath[idx]
        for a in range(1, N_DEV):
            acc = acc + gath[lax.bitwise_xor(idx, a)]
        o_ref[...] = acc

    vmem = pl.BlockSpec(memory_space=pltpu.VMEM)
    return pl.pallas_call(
        body, name=name, in_specs=[vmem], out_specs=vmem,
        out_shape=jax.ShapeDtypeStruct((rows, LANES), F32),
        scratch_shapes=[pltpu.VMEM((N_DEV, rows, LANES), F32), pltpu.SemaphoreType.DMA((N_DEV - 1,)),
                        pltpu.SemaphoreType.DMA((N_DEV - 1,))],
    )(v)


def _to_rows(flat, width, row_multiple):
    n = flat.shape[-1]
    block = width * row_multiple
    padded = -(-n // block) * block
    flat = jnp.pad(flat, [(0, 0)] * (flat.ndim - 1) + [(0, padded - n)])
    return flat.reshape(flat.shape[:-1] + (padded // width, width))


def _join_shards(name, seg):
    if name in COL_SHARDED:
        n, l, d, c = seg.shape
        return jnp.moveaxis(seg, 0, 2).reshape(l, d, n * c)
    n, l, r, d = seg.shape
    return jnp.moveaxis(seg, 0, 1).reshape(l, n * r, d)


def _split_shards(name, full):
    l, r, c = full.shape
    if name in COL_SHARDED:
        return jnp.moveaxis(full.reshape(l, r, N_CHIPS, c // N_CHIPS), 2, 0).reshape(N_CHIPS, -1)
    return jnp.moveaxis(full.reshape(l, N_CHIPS, r // N_CHIPS, c), 1, 0).reshape(N_CHIPS, -1)


def _pad_small(flat):
    return _to_rows(flat, LANES, 8)


def kernel(x, ffn1_norm, ffn1_w_in, ffn1_w_out, mix_norm, w_in, dn_conv_w, dn_a_log, dn_dt_bias, dn_out_norm, sb_q_norm, sb_k_norm, w_branch_a, w_branch_b, w_out, ffn2_norm, ffn2_w_in, ffn2_w_out, loss_target, m_ffn1_norm, m_ffn1_w_in, m_ffn1_w_out, m_mix_norm, m_w_in, m_dn_conv_w, m_dn_a_log, m_dn_dt_bias, m_dn_out_norm, m_sb_q_norm, m_sb_k_norm, m_w_branch_a, m_w_branch_b, m_w_out, m_ffn2_norm, m_ffn2_w_in, m_ffn2_w_out, v_ffn1_norm, v_ffn1_w_in, v_ffn1_w_out, v_mix_norm, v_w_in, v_dn_conv_w, v_dn_a_log, v_dn_dt_bias, v_dn_out_norm, v_sb_q_norm, v_sb_k_norm, v_w_branch_a, v_w_branch_b, v_w_out, v_ffn2_norm, v_ffn2_w_in, v_ffn2_w_out):
    w = dict(ffn1_norm=ffn1_norm, ffn1_w_in=ffn1_w_in, ffn1_w_out=ffn1_w_out, mix_norm=mix_norm, w_in=w_in,
             dn_conv_w=dn_conv_w, dn_a_log=dn_a_log, dn_dt_bias=dn_dt_bias, dn_out_norm=dn_out_norm,
             sb_q_norm=sb_q_norm, sb_k_norm=sb_k_norm, w_branch_a=w_branch_a, w_branch_b=w_branch_b, w_out=w_out,
             ffn2_norm=ffn2_norm, ffn2_w_in=ffn2_w_in, ffn2_w_out=ffn2_w_out)
    mom = dict(ffn1_norm=m_ffn1_norm, ffn1_w_in=m_ffn1_w_in, ffn1_w_out=m_ffn1_w_out, mix_norm=m_mix_norm, w_in=m_w_in,
               dn_conv_w=m_dn_conv_w, dn_a_log=m_dn_a_log, dn_dt_bias=m_dn_dt_bias, dn_out_norm=m_dn_out_norm,
               sb_q_norm=m_sb_q_norm, sb_k_norm=m_sb_k_norm, w_branch_a=m_w_branch_a, w_branch_b=m_w_branch_b,
               w_out=m_w_out, ffn2_norm=m_ffn2_norm, ffn2_w_in=m_ffn2_w_in, ffn2_w_out=m_ffn2_w_out)
    var = dict(ffn1_norm=v_ffn1_norm, ffn1_w_in=v_ffn1_w_in, ffn1_w_out=v_ffn1_w_out, mix_norm=v_mix_norm, w_in=v_w_in,
               dn_conv_w=v_dn_conv_w, dn_a_log=v_dn_a_log, dn_dt_bias=v_dn_dt_bias, dn_out_norm=v_dn_out_norm,
               sb_q_norm=v_sb_q_norm, sb_k_norm=v_sb_k_norm, w_branch_a=v_w_branch_a, w_branch_b=v_w_branch_b,
               w_out=v_w_out, ffn2_norm=v_ffn2_norm, ffn2_w_in=v_ffn2_w_in, ffn2_w_out=v_ffn2_w_out)

    _, t, d = x.shape
    depth = ffn1_norm.shape[0]
    n_heads = d // HEAD_DIM
    d_ff = ffn1_w_out.shape[1] * N_CHIPS
    conv_cols = dn_conv_w.shape[2]
    assert d % HEAD_DIM == 0 and t % SB_BLOCK == 0 and 2 * n_heads <= LANES
    assert w_in.shape[2] * N_CHIPS == 9 * d + 2 * n_heads and conv_cols * N_CHIPS == 3 * d

    x_idx, y_idx, c_idx = _place()
    shard = 2 * x_idx + y_idx
    c_arr = jnp.reshape(c_idx, (1,)).astype(jnp.int32)
    s_arr = jnp.reshape(shard, (1,)).astype(jnp.int32)

    sizes = [int(np.prod(w[n].shape)) for n in BIG]
    offsets = np.concatenate([[0], np.cumsum(sizes)])
    packed = _to_rows(jnp.concatenate([w[n].astype(BF16).reshape(-1) for n in BIG]), PACK_WIDTH, 32)
    gathered = _allgather_shards(packed).reshape(N_CHIPS, -1)
    full = {n: _join_shards(n, gathered[:, offsets[i]:offsets[i + 1]].reshape((N_CHIPS,) + w[n].shape))
            for i, n in enumerate(BIG)}
    cut = 4 * d
    w_main = jnp.concatenate([full["w_in"][..., :cut], full["w_in"][..., cut + 2 * n_heads:]], axis=-1)
    w_ba = jnp.pad(full["w_in"][..., cut:cut + 2 * n_heads], ((0, 0), (0, 0), (0, LANES - 2 * n_heads)))

    conv_place = lax.dynamic_update_slice(jnp.zeros((depth, DN_CONV, 3 * d), F32), dn_conv_w, (0, 0, shard * conv_cols))
    conv_rows = _pad_small(conv_place.reshape(-1))
    conv_full = (0.5 * _allreduce_small(conv_rows, "allgather_conv")).reshape(-1)[:depth * DN_CONV * 3 * d]
    conv_full = jnp.pad(conv_full.reshape(depth, DN_CONV, 3 * d), ((0, 0), (0, CONV_ROWS - DN_CONV), (0, 0)))

    def head_row(vals):
        return jnp.pad(vals, (n_heads, LANES - 2 * n_heads)).reshape(1, LANES)

    saved = []
    cur = x[0]
    for l in range(depth):
        wg1, wu1 = full["ffn1_w_in"][l, :, :d_ff], full["ffn1_w_in"][l, :, d_ff:]
        wg2, wu2 = full["ffn2_w_in"][l, :, :d_ff], full["ffn2_w_in"][l, :, d_ff:]
        x0 = cur
        x1 = _ffn_fwd(x0, ffn1_norm[l][None], wg1, wu1, full["ffn1_w_out"][l])
        proj, ba = _proj_fwd(x1, mix_norm[l][None], w_main[l], w_ba[l])
        act = _dn_prep_fwd(proj, conv_full[l], n_heads)
        alog, dtb = head_row(dn_a_log[l]), head_row(dn_dt_bias[l])
        oa, snaps = _delta_fwd(act, ba, alog, dtb, n_heads)
        qb, kb = _sb_prep_fwd(proj, sb_q_norm[l][None], sb_k_norm[l][None], n_heads)
        ob, ltot = _sb_attn_fwd(qb, kb, proj, n_heads)
        x2 = _merge_fwd(x1, oa, proj, ob, dn_out_norm[l][None], full["w_branch_a"][l], full["w_branch_b"][l],
                        full["w_out"][l], n_heads)
        cur = _ffn_fwd(x2, ffn2_norm[l][None], wg2, wu2, full["ffn2_w_out"][l])
        saved.append((x0, x1, proj, ba, act, alog, dtb, oa, snaps, qb, kb, ob, ltot, x2, wg1, wu1, wg2, wu2))

    dcur, loss_part = _loss_head(cur, loss_target[0])

    grads = {n: [None] * depth for n in WEIGHTS}
    for l in reversed(range(depth)):
        x0, x1, proj, ba, act, alog, dtb, oa, snaps, qb, kb, ob, ltot, x2, wg1, wu1, wg2, wu2 = saved[l]
        dx2, dg, dwg, dwu, dwo = _ffn_bwd(x2, ffn2_norm[l][None], dcur, wg2, wu2, full["ffn2_w_out"][l])
        grads["ffn2_norm"][l] = dg[0]
        grads["ffn2_w_in"][l] = jnp.concatenate([dwg, dwu], axis=1)
        grads["ffn2_w_out"][l] = dwo
        doa, dz, dob, dga, dgb, dgn, dwa, dwb, dwout = _merge_bwd(
            oa, proj, ob, dx2, dn_out_norm[l][None], full["w_branch_a"][l], full["w_branch_b"][l], full["w_out"][l],
            n_heads)
        grads["dn_out_norm"][l] = dgn[0]
        grads["w_branch_a"][l], grads["w_branch_b"][l], grads["w_out"][l] = dwa, dwb, dwout
        dqb, dkb, dvb = _sb_attn_bwd(qb, kb, proj, ltot, dob, n_heads)
        dsq, dsk, dqn, dkn = _sb_prep_bwd(proj, sb_q_norm[l][None], sb_k_norm[l][None], dqb, dkb, n_heads)
        grads["sb_q_norm"][l], grads["sb_k_norm"][l] = dqn[0], dkn[0]
        dact, dba, dal, ddt = _delta_bwd(act, ba, alog, dtb, snaps, doa, n_heads)
        grads["dn_a_log"][l] = dal[0, n_heads:2 * n_heads]
        grads["dn_dt_bias"][l] = ddt[0, n_heads:2 * n_heads]
        dqkv, dconv = _dn_prep_bwd(proj, conv_full[l], dact, n_heads)
        grads["dn_conv_w"][l] = dconv[:DN_CONV]
        dproj = jnp.concatenate([dqkv, dz, dsq, dsk, dvb.astype(BF16), dga, dgb], axis=1)
        dx1, dg, dwm, dwba = _proj_bwd(x1, mix_norm[l][None], dx2, dproj, dba, w_main[l], w_ba[l])
        grads["mix_norm"][l] = dg[0]
        grads["w_in"][l] = jnp.concatenate([dwm[:, :cut], dwba[:, :2 * n_heads], dwm[:, cut:]], axis=1)
        dcur, dg, dwg, dwu, dwo = _ffn_bwd(x0, ffn1_norm[l][None], dx1, wg1, wu1, full["ffn1_w_out"][l])
        grads["ffn1_norm"][l] = dg[0]
        grads["ffn1_w_in"][l] = jnp.concatenate([dwg, dwu], axis=1)
        grads["ffn1_w_out"][l] = dwo
    grads = {n: jnp.stack(g) for n, g in grads.items()}

    g_pack = _to_rows(jnp.concatenate([_split_shards(n, grads[n]) for n in BIG], axis=1).astype(BF16), PACK_WIDTH, 32)
    part = _add_halves(g_pack, _swap_halves(g_pack), c_arr)
    reduced = _join_halves(_sum_partials(part, _scatter_partials(part), s_arr)).reshape(-1)
    final = {n: reduced[offsets[i]:offsets[i + 1]].reshape(w[n].shape) for i, n in enumerate(BIG)}

    small_names = SMALL + ("dn_conv_w",)
    small_sizes = [int(np.prod(grads[n].shape)) for n in small_names]
    small_off = np.concatenate([[0], np.cumsum(small_sizes)])
    small = jnp.concatenate([grads[n].reshape(-1) for n in small_names] + [loss_part[0, :1]])
    small_sum = _allreduce_small(_pad_small(small), "allreduce_small").reshape(-1)
    for i, n in enumerate(small_names):
        final[n] = small_sum[small_off[i]:small_off[i + 1]].reshape(grads[n].shape)
    final["dn_conv_w"] = lax.dynamic_slice(final["dn_conv_w"], (0, 0, shard * conv_cols), (depth, DN_CONV, conv_cols))
    loss = small_sum[small_off[-1]]

    deltas, new_m, new_v = {}, {}, {}
    for n in WEIGHTS:
        shape = w[n].shape
        flat = (-1, shape[-1])
        dl, m2, v2 = _adamw(w[n].reshape(flat), final[n].reshape(flat), mom[n].reshape(flat), var[n].reshape(flat))
        deltas[n], new_m[n], new_v[n] = dl.reshape(shape), m2.reshape(shape), v2.reshape(shape)

    grad_x = dcur[None]
    return (loss, grad_x, *[final[n] for n in WEIGHTS], *[deltas[n] for n in WEIGHTS],
            *[new_m[n] for n in WEIGHTS], *[new_v[n] for n in WEIGHTS])
```

```python
import functools

import jax
import jax.numpy as jnp
import numpy as np
from jax import lax
from jax.experimental import pallas as pl
from jax.experimental.pallas import tpu as pltpu

F32 = jnp.float32
BF16 = jnp.bfloat16

LANES = 128
HEAD_DIM = 128
DN_CHUNK = 64
DN_CONV = 4
CONV_ROWS = 8
SB_BLOCK = 128
FFN_HALF = 0.5
RMS_EPS = 1e-6
L2_EPS = 1e-6
NEG_BIG = -1e30
ADAM_LR = 0.001
ADAM_B1 = 0.9
ADAM_B2 = 0.999
ADAM_EPS = 1e-08
ADAM_WD = 0.01
ADAM_STEP = 10
VMEM_LIMIT = 56 * 1024 * 1024
N_CHIPS = 4
N_DEV = 8
MESH = pl.DeviceIdType.MESH

BIG = ("ffn1_w_in", "ffn1_w_out", "w_in", "w_branch_a", "w_branch_b", "w_out", "ffn2_w_in", "ffn2_w_out")
COL_SHARDED = ("ffn1_w_in", "w_in", "ffn2_w_in")
SMALL = ("ffn1_norm", "mix_norm", "dn_a_log", "dn_dt_bias", "dn_out_norm", "sb_q_norm", "sb_k_norm", "ffn2_norm")
WEIGHTS = ("ffn1_norm", "ffn1_w_in", "ffn1_w_out", "mix_norm", "w_in", "dn_conv_w", "dn_a_log", "dn_dt_bias",
           "dn_out_norm", "sb_q_norm", "sb_k_norm", "w_branch_a", "w_branch_b", "w_out", "ffn2_norm", "ffn2_w_in",
           "ffn2_w_out")


def _params(**kw):
    return pltpu.CompilerParams(vmem_limit_bytes=VMEM_LIMIT, **kw)


def _pick(n, options):
    for o in options:
        if n % o == 0:
            return o
    return n


def _const_spec(shape, single=False):
    nd = len(shape)
    if single:
        return pl.BlockSpec(shape, lambda *_: (0,) * nd, pipeline_mode=pl.Buffered(1))
    return pl.BlockSpec(shape, lambda *_: (0,) * nd)


_NN = ((1,), (0,))
_NT = ((1,), (1,))
_TN = ((0,), (0,))


def _dot(a, b, dims):
    return lax.dot_general(a.astype(BF16), b.astype(BF16), (dims, ((), ())), preferred_element_type=F32)


def _mm_nn(a, b):
    return _dot(a, b, _NN)


def _mm_nt(a, b):
    return _dot(a, b, _NT)


def _mm_tn(a, b):
    return _dot(a, b, _TN)


def _split(a):
    hi = a.astype(BF16)
    lo = (a - hi.astype(F32)).astype(BF16)
    return hi, lo


def _dot_precise(a, b, dims):
    dn = (dims, ((), ()))
    ah, al = _split(a)
    bh, bl = _split(b)
    out = lax.dot_general(ah, bh, dn, preferred_element_type=F32)
    out = out + lax.dot_general(ah, bl, dn, preferred_element_type=F32)
    return out + lax.dot_general(al, bh, dn, preferred_element_type=F32)


def _make_diff_mm(dot):
    @jax.custom_vjp
    def nn(a, b):
        return dot(a, b, _NN)

    @jax.custom_vjp
    def nt(a, b):
        return dot(a, b, _NT)

    @jax.custom_vjp
    def tn(a, b):
        return dot(a, b, _TN)

    nn.defvjp(lambda a, b: (dot(a, b, _NN), (a, b)), lambda r, g: (nt(g, r[1]), tn(r[0], g)))
    nt.defvjp(lambda a, b: (dot(a, b, _NT), (a, b)), lambda r, g: (nn(g, r[1]), tn(g, r[0])))
    tn.defvjp(lambda a, b: (dot(a, b, _TN), (a, b)), lambda r, g: (nt(r[1], g), nn(r[0], g)))
    return nn, nt, tn


_d_nn, _d_nt, _d_tn = _make_diff_mm(_dot)
_p_nn, _p_nt, _p_tn = _make_diff_mm(_dot_precise)


def _softplus_raw(x):
    return jnp.maximum(x, 0.0) + jnp.log(1.0 + jnp.exp(-jnp.abs(x)))


@jax.custom_vjp
def _softplus(x):
    return _softplus_raw(x)


_softplus.defvjp(lambda x: (_softplus_raw(x), x), lambda x, g: (g * jax.nn.sigmoid(x),))


def _rms(x, gain, eps):
    return x * lax.rsqrt(jnp.mean(x * x, axis=-1, keepdims=True) + eps) * gain


def _silu(x):
    return x * jax.nn.sigmoid(x)


def _shift_rows_raw(x, k, down):
    n = x.shape[0]
    row = lax.broadcasted_iota(jnp.int32, x.shape, 0)
    if down:
        return jnp.where(row >= k, pltpu.roll(x, k, 0), 0.0)
    return jnp.where(row < n - k, pltpu.roll(x, n - k, 0), 0.0)


@functools.partial(jax.custom_vjp, nondiff_argnums=(1,))
def _shift_down(x, k):
    return _shift_rows_raw(x, k, True)


_shift_down.defvjp(lambda x, k: (_shift_rows_raw(x, k, True), None),
                   lambda k, _, g: (_shift_rows_raw(g, k, False),))


def _ffn_fwd(x, gain, wg, wu, wo):
    t, d = x.shape
    f = wo.shape[0]
    fc = _pick(f, (256, 128))
    rt = _pick(t, (512, 256, 128))

    def body(x_ref, g_ref, wg_ref, wu_ref, wo_ref, o_ref, hs_ref):
        @pl.when(pl.program_id(0) == 0)
        def _():
            for r in range(t // rt):
                rows = pl.ds(r * rt, rt)
                xr = x_ref[rows, :]
                hs_ref[rows, :] = _rms(xr, g_ref[...], RMS_EPS).astype(BF16)
                o_ref[rows, :] = xr

        for r in range(t // rt):
            rows = pl.ds(r * rt, rt)
            h = hs_ref[rows, :]
            a = _mm_nn(h, wg_ref[...])
            b = _mm_nn(h, wu_ref[...])
            o_ref[rows, :] += FFN_HALF * _mm_nn(_silu(a) * b, wo_ref[...])

    return pl.pallas_call(
        body, name="ffn_fwd", grid=(f // fc,),
        in_specs=[_const_spec((t, d), True), _const_spec((1, d)),
                  pl.BlockSpec((d, fc), lambda j: (0, j)), pl.BlockSpec((d, fc), lambda j: (0, j)),
                  pl.BlockSpec((fc, d), lambda j: (j, 0))],
        out_specs=_const_spec((t, d)),
        out_shape=jax.ShapeDtypeStruct((t, d), F32),
        scratch_shapes=[pltpu.VMEM((t, d), BF16)],
        compiler_params=_params(dimension_semantics=("arbitrary",)),
    )(x, gain, wg, wu, wo)


def _ffn_bwd(x, gain, dy, wg, wu, wo):
    t, d = x.shape
    f = wo.shape[0]
    fc = _pick(f, (256, 128))
    nj = f // fc
    rt = _pick(t, (512, 256, 128))
    nr = t // rt

    def body(x_ref, g_ref, dy_ref, wg_ref, wu_ref, wo_ref, dx_ref, dg_ref, dwg_ref, dwu_ref, dwo_ref, hs_ref):
        j = pl.program_id(0)

        @pl.when(j == 0)
        def _():
            for r in range(nr):
                rows = pl.ds(r * rt, rt)
                hs_ref[rows, :] = _rms(x_ref[rows, :], g_ref[...], RMS_EPS).astype(BF16)
                dx_ref[rows, :] = jnp.zeros((rt, d), F32)

        for r in range(nr):
            rows = pl.ds(r * rt, rt)
            h = hs_ref[rows, :]
            dy2 = (FFN_HALF * dy_ref[rows, :]).astype(BF16)
            a = _mm_nn(h, wg_ref[...])
            b = _mm_nn(h, wu_ref[...])
            sig = jax.nn.sigmoid(a)
            sa = a * sig
            ds = _mm_nt(dy2, wo_ref[...])
            da = ds * b * (sig * (1.0 + a * (1.0 - sig)))
            db = ds * sa
            dx_ref[rows, :] += _mm_nt(da, wg_ref[...]) + _mm_nt(db, wu_ref[...])
            dwo_c = _mm_tn(sa * b, dy2)
            dwg_c = _mm_tn(h, da)
            dwu_c = _mm_tn(h, db)
            if r == 0:
                dwo_ref[...] = dwo_c
                dwg_ref[...] = dwg_c
                dwu_ref[...] = dwu_c
            else:
                dwo_ref[...] += dwo_c
                dwg_ref[...] += dwg_c
                dwu_ref[...] += dwu_c

        @pl.when(j == nj - 1)
        def _():
            for r in range(nr):
                rows = pl.ds(r * rt, rt)
                _, vjp = jax.vjp(lambda xx, gg: _rms(xx, gg, RMS_EPS), x_ref[rows, :], g_ref[...])
                dxn, dgr = vjp(dx_ref[rows, :])
                dx_ref[rows, :] = dy_ref[rows, :] + dxn
                if r == 0:
                    dg_ref[...] = dgr
                else:
                    dg_ref[...] += dgr

    return pl.pallas_call(
        body, name="ffn_bwd", grid=(nj,),
        in_specs=[_const_spec((t, d), True), _const_spec((1, d)), _const_spec((t, d), True),
                  pl.BlockSpec((d, fc), lambda j: (0, j)), pl.BlockSpec((d, fc), lambda j: (0, j)),
                  pl.BlockSpec((fc, d), lambda j: (j, 0))],
        out_specs=[_const_spec((t, d)), _const_spec((1, d)),
                   pl.BlockSpec((d, fc), lambda j: (0, j)), pl.BlockSpec((d, fc), lambda j: (0, j)),
                   pl.BlockSpec((fc, d), lambda j: (j, 0))],
        out_shape=[jax.ShapeDtypeStruct((t, d), F32), jax.ShapeDtypeStruct((1, d), F32),
                   jax.ShapeDtypeStruct((d, f), F32), jax.ShapeDtypeStruct((d, f), F32),
                   jax.ShapeDtypeStruct((f, d), F32)],
        scratch_shapes=[pltpu.VMEM((t, d), BF16)],
        compiler_params=_params(dimension_semantics=("arbitrary",)),
    )(x, gain, dy, wg, wu, wo)


def _proj_fwd(x, gain, w, wba):
    t, d = x.shape
    n = w.shape[1]
    nc = _pick(n, (512, 256, 128))
    rt = _pick(t, (512, 256, 128))

    def body(x_ref, g_ref, w_ref, wba_ref, p_ref, ba_ref, hs_ref):
        @pl.when(pl.program_id(0) == 0)
        def _():
            for r in range(t // rt):
                rows = pl.ds(r * rt, rt)
                h = _rms(x_ref[rows, :], g_ref[...], RMS_EPS).astype(BF16)
                hs_ref[rows, :] = h
                ba_ref[rows, :] = _mm_nn(h, wba_ref[...])

        for r in range(t // rt):
            rows = pl.ds(r * rt, rt)
            p_ref[rows, :] = _mm_nn(hs_ref[rows, :], w_ref[...])

    return pl.pallas_call(
        body, name="proj_fwd", grid=(n // nc,),
        in_specs=[_const_spec((t, d), True), _const_spec((1, d)),
                  pl.BlockSpec((d, nc), lambda j: (0, j)), _const_spec((d, LANES))],
        out_specs=[pl.BlockSpec((t, nc), lambda j: (0, j)), _const_spec((t, LANES))],
        out_shape=[jax.ShapeDtypeStruct((t, n), F32), jax.ShapeDtypeStruct((t, LANES), F32)],
        scratch_shapes=[pltpu.VMEM((t, d), BF16)],
        compiler_params=_params(dimension_semantics=("arbitrary",)),
    )(x, gain, w, wba)


def _proj_bwd(x, gain, dres, dp, dba, w, wba):
    t, d = x.shape
    n = w.shape[1]
    nc = _pick(n, (512, 256, 128))
    nj = n // nc
    rt = _pick(t, (512, 256, 128))
    nr = t // rt

    def body(x_ref, g_ref, dres_ref, dp_ref, dba_ref, w_ref, wba_ref, dx_ref, dg_ref, dw_ref, dwba_ref, hs_ref):
        j = pl.program_id(0)

        @pl.when(j == 0)
        def _():
            for r in range(nr):
                rows = pl.ds(r * rt, rt)
                h = _rms(x_ref[rows, :], g_ref[...], RMS_EPS).astype(BF16)
                hs_ref[rows, :] = h
                g = dba_ref[rows, :]
                dx_ref[rows, :] = _mm_nt(g, wba_ref[...])
                if r == 0:
                    dwba_ref[...] = _mm_tn(h, g)
                else:
                    dwba_ref[...] += _mm_tn(h, g)

        for r in range(nr):
            rows = pl.ds(r * rt, rt)
            g = dp_ref[rows, :]
            dx_ref[rows, :] += _mm_nt(g, w_ref[...])
            if r == 0:
                dw_ref[...] = _mm_tn(hs_ref[rows, :], g)
            else:
                dw_ref[...] += _mm_tn(hs_ref[rows, :], g)

        @pl.when(j == nj - 1)
        def _():
            for r in range(nr):
                rows = pl.ds(r * rt, rt)
                _, vjp = jax.vjp(lambda xx, gg: _rms(xx, gg, RMS_EPS), x_ref[rows, :], g_ref[...])
                dxn, dgr = vjp(dx_ref[rows, :])
                dx_ref[rows, :] = dres_ref[rows, :] + dxn
                if r == 0:
                    dg_ref[...] = dgr
                else:
                    dg_ref[...] += dgr

    return pl.pallas_call(
        body, name="proj_bwd", grid=(nj,),
        in_specs=[_const_spec((t, d), True), _const_spec((1, d)), _const_spec((t, d), True),
                  pl.BlockSpec((t, nc), lambda j: (0, j)), _const_spec((t, LANES)),
                  pl.BlockSpec((d, nc), lambda j: (0, j)), _const_spec((d, LANES))],
        out_specs=[_const_spec((t, d)), _const_spec((1, d)),
                   pl.BlockSpec((d, nc), lambda j: (0, j)), _const_spec((d, LANES))],
        out_shape=[jax.ShapeDtypeStruct((t, d), F32), jax.ShapeDtypeStruct((1, d), F32),
                   jax.ShapeDtypeStruct((d, n), F32), jax.ShapeDtypeStruct((d, LANES), F32)],
        scratch_shapes=[pltpu.VMEM((t, d), BF16)],
        compiler_params=_params(dimension_semantics=("arbitrary",)),
    )(x, gain, dres, dp, dba, w, wba)


def _conv_act(x, w0, w1, w2, w3, is_qk):
    y = w3 * x + w2 * _shift_down(x, 1) + w1 * _shift_down(x, 2) + w0 * _shift_down(x, 3)
    y = _silu(y)
    inv = lax.rsqrt(jnp.sum(y * y, axis=-1, keepdims=True) + L2_EPS)
    return y * (is_qk * inv + (1.0 - is_qk))


def _taps(w_ref):
    return tuple(w_ref[i:i + 1, :] for i in range(DN_CONV))


def _dn_prep_fwd(proj, conv_w, n_heads):
    t = proj.shape[0]
    nb = 3 * n_heads

    def body(x_ref, w_ref, o_ref):
        is_qk = jnp.where(pl.program_id(0) < 2 * n_heads, 1.0, 0.0).astype(F32)
        o_ref[...] = _conv_act(x_ref[...], *_taps(w_ref), is_qk)

    return pl.pallas_call(
        body, name="dn_prep_fwd", grid=(nb,),
        in_specs=[pl.BlockSpec((t, HEAD_DIM), lambda i: (0, i)), pl.BlockSpec((CONV_ROWS, HEAD_DIM), lambda i: (0, i))],
        out_specs=pl.BlockSpec((t, HEAD_DIM), lambda i: (0, i)),
        out_shape=jax.ShapeDtypeStruct((t, nb * HEAD_DIM), F32),
        compiler_params=_params(dimension_semantics=("arbitrary",)),
    )(proj, conv_w)


def _dn_prep_bwd(proj, conv_w, dact, n_heads):
    t = proj.shape[0]
    nb = 3 * n_heads

    def body(x_ref, w_ref, g_ref, dx_ref, dw_ref):
        is_qk = jnp.where(pl.program_id(0) < 2 * n_heads, 1.0, 0.0).astype(F32)
        _, vjp = jax.vjp(lambda x, a, b, c, e: _conv_act(x, a, b, c, e, is_qk), x_ref[...], *_taps(w_ref))
        dx, d0, d1, d2, d3 = vjp(g_ref[...])
        dx_ref[...] = dx.astype(BF16)
        dw_ref[...] = jnp.concatenate([d0, d1, d2, d3, jnp.zeros((CONV_ROWS - DN_CONV, HEAD_DIM), F32)], axis=0)

    return pl.pallas_call(
        body, name="dn_prep_bwd", grid=(nb,),
        in_specs=[pl.BlockSpec((t, HEAD_DIM), lambda i: (0, i)), pl.BlockSpec((CONV_ROWS, HEAD_DIM), lambda i: (0, i)),
                  pl.BlockSpec((t, HEAD_DIM), lambda i: (0, i))],
        out_specs=[pl.BlockSpec((t, HEAD_DIM), lambda i: (0, i)), pl.BlockSpec((CONV_ROWS, HEAD_DIM), lambda i: (0, i))],
        out_shape=[jax.ShapeDtypeStruct((t, nb * HEAD_DIM), BF16), jax.ShapeDtypeStruct((CONV_ROWS, nb * HEAD_DIM), F32)],
        compiler_params=_params(dimension_semantics=("arbitrary",)),
    )(proj, conv_w, dact)


def _unit_lower_inverse(lmat, c):
    r = lax.broadcasted_iota(jnp.int32, (c, c), 0)
    q = lax.broadcasted_iota(jnp.int32, (c, c), 1)
    p = jnp.where(r == q, 1.0, 0.0) - lmat
    m = _p_nn(lmat, lmat)
    n = 2
    while True:
        p = p + _p_nn(p, m)
        if 2 * n >= c:
            return p
        m = _p_nn(m, m)
        n *= 2


def _delta_head(h, n_heads, q, k, v, bg, alog, dtb, state):
    c = q.shape[0]
    lane = lax.broadcasted_iota(jnp.int32, (c, LANES), 1)
    beta_all = jax.nn.sigmoid(bg)
    g_all = -jnp.exp(alog) * _softplus(bg + dtb)
    beta = jnp.sum(jnp.where(lane == h, beta_all, 0.0), axis=1, keepdims=True)
    g = jnp.sum(jnp.where(lane == n_heads + h, g_all, 0.0), axis=1, keepdims=True)
    r = lax.broadcasted_iota(jnp.int32, (c, c), 0)
    s = lax.broadcasted_iota(jnp.int32, (c, c), 1)
    g_row = jnp.sum(jnp.where(r == s, g, 0.0), axis=0, keepdims=True)
    gc = jnp.sum(jnp.where(s <= r, g_row, 0.0), axis=1, keepdims=True)
    gr = jnp.sum(jnp.where(r <= s, g, 0.0), axis=0, keepdims=True)
    g_last = jnp.sum(g, axis=0, keepdims=True)
    decay = jnp.exp(jnp.where(r >= s, gc - gr, NEG_BIG))
    qs = q * (HEAD_DIM ** -0.5)
    k_beta = k * beta
    lmat = jnp.where(r > s, _d_nt(k_beta, k) * decay, 0.0)
    tinv = _unit_lower_inverse(lmat, c)
    u = _p_nn(tinv, v * beta)
    w = _p_nn(tinv, k_beta * jnp.exp(gc))
    attn = _d_nt(qs, k) * decay
    v_new = u - _d_nn(w, state)
    o = _d_nn(qs * jnp.exp(gc), state) + _d_nn(attn, v_new)
    new_state = state * jnp.exp(g_last) + _d_tn(k * jnp.exp(g_last - gc), v_new)
    return o, new_state


def _delta_fwd(act, ba, alog, dtb, n_heads):
    t = act.shape[0]
    d = n_heads * HEAD_DIM
    c = DN_CHUNK
    nc = t // c

    def body(q_ref, k_ref, v_ref, bg_ref, al_ref, dt_ref, o_ref, snap_ref, st_ref):
        @pl.when(pl.program_id(0) == 0)
        def _():
            st_ref[...] = jnp.zeros(st_ref.shape, F32)

        snap_ref[0] = st_ref[...]
        for h in range(n_heads):
            sl = slice(h * HEAD_DIM, (h + 1) * HEAD_DIM)
            o, ns = _delta_head(h, n_heads, q_ref[:, sl], k_ref[:, sl], v_ref[:, sl], bg_ref[...],
                                al_ref[...], dt_ref[...], st_ref[h])
            o_ref[:, sl] = o
            st_ref[h] = ns

    return pl.pallas_call(
        body, name="delta_fwd", grid=(nc,),
        in_specs=[pl.BlockSpec((c, d), lambda i: (i, 0)), pl.BlockSpec((c, d), lambda i: (i, 1)),
                  pl.BlockSpec((c, d), lambda i: (i, 2)), pl.BlockSpec((c, LANES), lambda i: (i, 0)),
                  _const_spec((1, LANES)), _const_spec((1, LANES))],
        out_specs=[pl.BlockSpec((c, d), lambda i: (i, 0)),
                   pl.BlockSpec((1, n_heads, HEAD_DIM, HEAD_DIM), lambda i: (i, 0, 0, 0))],
        out_shape=[jax.ShapeDtypeStruct((t, d), F32), jax.ShapeDtypeStruct((nc, n_heads, HEAD_DIM, HEAD_DIM), F32)],
        scratch_shapes=[pltpu.VMEM((n_heads, HEAD_DIM, HEAD_DIM), F32)],
        compiler_params=_params(dimension_semantics=("arbitrary",)),
    )(act, act, act, ba, alog, dtb)


def _delta_bwd(act, ba, alog, dtb, snaps, do, n_heads):
    t = act.shape[0]
    d = n_heads * HEAD_DIM
    c = DN_CHUNK
    nc = t // c

    def body(q_ref, k_ref, v_ref, bg_ref, al_ref, dt_ref, snap_ref, do_ref,
             dact_ref, dbg_ref, dal_ref, ddt_ref, ds_ref):
        @pl.when(pl.program_id(0) == 0)
        def _():
            ds_ref[...] = jnp.zeros(ds_ref.shape, F32)
            dal_ref[...] = jnp.zeros((1, LANES), F32)
            ddt_ref[...] = jnp.zeros((1, LANES), F32)

        dbg = jnp.zeros((c, LANES), F32)
        for h in range(n_heads):
            sl = slice(h * HEAD_DIM, (h + 1) * HEAD_DIM)
            _, vjp = jax.vjp(functools.partial(_delta_head, h, n_heads), q_ref[:, sl], k_ref[:, sl], v_ref[:, sl],
                             bg_ref[...], al_ref[...], dt_ref[...], snap_ref[0, h])
            dq, dk, dv, dbg_h, dal, ddt, dst = vjp((do_ref[:, sl], ds_ref[h]))
            dact_ref[:, h * HEAD_DIM:(h + 1) * HEAD_DIM] = dq
            dact_ref[:, d + h * HEAD_DIM:d + (h + 1) * HEAD_DIM] = dk
            dact_ref[:, 2 * d + h * HEAD_DIM:2 * d + (h + 1) * HEAD_DIM] = dv
            dbg = dbg + dbg_h
            dal_ref[...] += dal
            ddt_ref[...] += ddt
            ds_ref[h] = dst
        dbg_ref[...] = dbg.astype(BF16)

    rev = lambda i: nc - 1 - i
    return pl.pallas_call(
        body, name="delta_bwd", grid=(nc,),
        in_specs=[pl.BlockSpec((c, d), lambda i: (rev(i), 0)), pl.BlockSpec((c, d), lambda i: (rev(i), 1)),
                  pl.BlockSpec((c, d), lambda i: (rev(i), 2)), pl.BlockSpec((c, LANES), lambda i: (rev(i), 0)),
                  _const_spec((1, LANES)), _const_spec((1, LANES)),
                  pl.BlockSpec((1, n_heads, HEAD_DIM, HEAD_DIM), lambda i: (rev(i), 0, 0, 0)),
                  pl.BlockSpec((c, d), lambda i: (rev(i), 0))],
        out_specs=[pl.BlockSpec((c, 3 * d), lambda i: (rev(i), 0)), pl.BlockSpec((c, LANES), lambda i: (rev(i), 0)),
                   _const_spec((1, LANES)), _const_spec((1, LANES))],
        out_shape=[jax.ShapeDtypeStruct((t, 3 * d), F32), jax.ShapeDtypeStruct((t, LANES), BF16),
                   jax.ShapeDtypeStruct((1, LANES), F32), jax.ShapeDtypeStruct((1, LANES), F32)],
        scratch_shapes=[pltpu.VMEM((n_heads, HEAD_DIM, HEAD_DIM), F32)],
        compiler_params=_params(dimension_semantics=("arbitrary",)),
    )(act, act, act, ba, alog, dtb, snaps, do)


def _head_norm2(a, b, ga, gb):
    return _rms(a, ga, RMS_EPS), _rms(b, gb, RMS_EPS)


def _sb_prep_fwd(proj, qn, kn, n_heads):
    t = proj.shape[0]
    d = n_heads * HEAD_DIM
    tm = _pick(t, (256, 128))

    def body(q_ref, k_ref, qn_ref, kn_ref, qo_ref, ko_ref):
        for h in range(n_heads):
            sl = slice(h * HEAD_DIM, (h + 1) * HEAD_DIM)
            qo_ref[:, sl], ko_ref[:, sl] = _head_norm2(q_ref[:, sl], k_ref[:, sl], qn_ref[...], kn_ref[...])

    return pl.pallas_call(
        body, name="sb_prep_fwd", grid=(t // tm,),
        in_specs=[pl.BlockSpec((tm, d), lambda i: (i, 4)), pl.BlockSpec((tm, d), lambda i: (i, 5)),
                  _const_spec((1, HEAD_DIM)), _const_spec((1, HEAD_DIM))],
        out_specs=[pl.BlockSpec((tm, d), lambda i: (i, 0)), pl.BlockSpec((tm, d), lambda i: (i, 0))],
        out_shape=[jax.ShapeDtypeStruct((t, d), F32), jax.ShapeDtypeStruct((t, d), F32)],
        compiler_params=_params(dimension_semantics=("arbitrary",)),
    )(proj, proj, qn, kn)


def _sb_prep_bwd(proj, qn, kn, dq, dk, n_heads):
    t = proj.shape[0]
    d = n_heads * HEAD_DIM
    tm = _pick(t, (256, 128))

    def body(q_ref, k_ref, qn_ref, kn_ref, dq_ref, dk_ref, dqo_ref, dko_ref, dqn_ref, dkn_ref):
        @pl.when(pl.program_id(0) == 0)
        def _():
            dqn_ref[...] = jnp.zeros((1, HEAD_DIM), F32)
            dkn_ref[...] = jnp.zeros((1, HEAD_DIM), F32)

        for h in range(n_heads):
            sl = slice(h * HEAD_DIM, (h + 1) * HEAD_DIM)
            _, vjp = jax.vjp(_head_norm2, q_ref[:, sl], k_ref[:, sl], qn_ref[...], kn_ref[...])
            da, db, dga, dgb = vjp((dq_ref[:, sl], dk_ref[:, sl]))
            dqo_ref[:, sl] = da.astype(BF16)
            dko_ref[:, sl] = db.astype(BF16)
            dqn_ref[...] += dga
            dkn_ref[...] += dgb

    return pl.pallas_call(
        body, name="sb_prep_bwd", grid=(t // tm,),
        in_specs=[pl.BlockSpec((tm, d), lambda i: (i, 4)), pl.BlockSpec((tm, d), lambda i: (i, 5)),
                  _const_spec((1, HEAD_DIM)), _const_spec((1, HEAD_DIM)),
                  pl.BlockSpec((tm, d), lambda i: (i, 0)), pl.BlockSpec((tm, d), lambda i: (i, 0))],
        out_specs=[pl.BlockSpec((tm, d), lambda i: (i, 0)), pl.BlockSpec((tm, d), lambda i: (i, 0)),
                   _const_spec((1, HEAD_DIM)), _const_spec((1, HEAD_DIM))],
        out_shape=[jax.ShapeDtypeStruct((t, d), BF16), jax.ShapeDtypeStruct((t, d), BF16),
                   jax.ShapeDtypeStruct((1, HEAD_DIM), F32), jax.ShapeDtypeStruct((1, HEAD_DIM), F32)],
        compiler_params=_params(dimension_semantics=("arbitrary",)),
    )(proj, proj, qn, kn, dq, dk)


def _cumsum_mm(x, tri):
    hi, lo = _split(x)
    return (lax.dot_general(hi, tri, (_NN, ((), ())), preferred_element_type=F32)
            + lax.dot_general(lo, tri, (_NN, ((), ())), preferred_element_type=F32))


def _sb_scores(q, kj, i, j):
    row = lax.broadcasted_iota(jnp.int32, (SB_BLOCK, SB_BLOCK), 0)
    col = lax.broadcasted_iota(jnp.int32, (SB_BLOCK, SB_BLOCK), 1)
    z = _mm_nt(q, kj) * (HEAD_DIM ** -0.5)
    valid = (col + j * SB_BLOCK) < (row + i * SB_BLOCK)
    return z, valid, _softplus_raw(z)


def _sb_attn_fwd(qb, kb, proj, n_heads):
    t = qb.shape[0]
    d = n_heads * HEAD_DIM
    nq = t // SB_BLOCK
    hb = _pick(n_heads, (4, 2, 1))
    wide = hb * HEAD_DIM
    v_col0 = 6 * n_heads // hb

    def body(q_ref, k_ref, v_ref, o_ref, lt_ref):
        i = pl.program_id(1)
        row = lax.broadcasted_iota(jnp.int32, (SB_BLOCK, SB_BLOCK), 0)
        col = lax.broadcasted_iota(jnp.int32, (SB_BLOCK, SB_BLOCK), 1)
        after = jnp.where(row > col, 1.0, 0.0).astype(BF16)
        heads = [slice(h * HEAD_DIM, (h + 1) * HEAD_DIM) for h in range(hb)]
        qs = [q_ref[:, sl] for sl in heads]

        def step(jj, carry):
            j = i - jj
            rows = pl.ds(pl.multiple_of(j * SB_BLOCK, SB_BLOCK), SB_BLOCK)
            out = []
            for h, sl in enumerate(heads):
                acc, later = carry[h]
                z, valid, sp = _sb_scores(qs[h], k_ref[rows, sl], i, j)
                lm = jnp.where(valid, -sp, 0.0)
                w = jnp.where(valid, jnp.exp(z - sp + later + _cumsum_mm(lm, after)), 0.0)
                out.append((acc + _mm_nn(w, v_ref[rows, sl]), later + jnp.sum(lm, axis=1, keepdims=True)))
            return tuple(out)

        init = tuple((jnp.zeros((SB_BLOCK, HEAD_DIM), F32), jnp.zeros((SB_BLOCK, 1), F32)) for _ in heads)
        res = lax.fori_loop(0, i + 1, step, init)
        for h, sl in enumerate(heads):
            o_ref[:, sl] = res[h][0]
            lt_ref[:, sl] = jnp.broadcast_to(res[h][1], (SB_BLOCK, HEAD_DIM))

    return pl.pallas_call(
        body, name="sb_attn_fwd", grid=(n_heads // hb, nq),
        in_specs=[pl.BlockSpec((SB_BLOCK, wide), lambda g, i: (i, g)),
                  pl.BlockSpec((t, wide), lambda g, i: (0, g)),
                  pl.BlockSpec((t, wide), lambda g, i: (0, v_col0 + g))],
        out_specs=[pl.BlockSpec((SB_BLOCK, wide), lambda g, i: (i, g)),
                   pl.BlockSpec((SB_BLOCK, wide), lambda g, i: (i, g))],
        out_shape=[jax.ShapeDtypeStruct((t, d), F32), jax.ShapeDtypeStruct((t, d), F32)],
        compiler_params=_params(dimension_semantics=("arbitrary", "arbitrary")),
    )(qb, kb, proj)


def _sb_attn_bwd(qb, kb, proj, ltot, do, n_heads):
    t = qb.shape[0]
    d = n_heads * HEAD_DIM
    nq = t // SB_BLOCK
    hb = _pick(n_heads, (4, 2, 1))
    wide = hb * HEAD_DIM
    v_col0 = 6 * n_heads // hb
    scale = HEAD_DIM ** -0.5

    def body(q_ref, k_ref, v_ref, lt_ref, do_ref, dq_ref, dk_ref, dv_ref):
        i = pl.program_id(1)

        @pl.when(i == 0)
        def _():
            dk_ref[...] = jnp.zeros((t, wide), F32)
            dv_ref[...] = jnp.zeros((t, wide), F32)

        row = lax.broadcasted_iota(jnp.int32, (SB_BLOCK, SB_BLOCK), 0)
        col = lax.broadcasted_iota(jnp.int32, (SB_BLOCK, SB_BLOCK), 1)
        upto = jnp.where(row <= col, 1.0, 0.0).astype(BF16)
        before = jnp.where(row < col, 1.0, 0.0).astype(BF16)
        heads = [slice(h * HEAD_DIM, (h + 1) * HEAD_DIM) for h in range(hb)]
        qs = [q_ref[:, sl] for sl in heads]
        dos = [do_ref[:, sl] for sl in heads]
        totals = [jnp.max(lt_ref[:, sl], axis=1, keepdims=True) for sl in heads]

        def step(j, carry):
            rows = pl.ds(pl.multiple_of(j * SB_BLOCK, SB_BLOCK), SB_BLOCK)
            out = []
            for h, sl in enumerate(heads):
                dq, lm_before, e_before = carry[h]
                kj = k_ref[rows, sl]
                vj = v_ref[rows, sl]
                z, valid, sp = _sb_scores(qs[h], kj, i, j)
                sig = jnp.exp(z - sp)
                lm = jnp.where(valid, -sp, 0.0)
                survive = totals[h] - (lm_before + _cumsum_mm(lm, upto))
                w = jnp.where(valid, jnp.exp(z - sp + survive), 0.0)
                e = w * _mm_nt(dos[h], vj)
                dv_ref[rows, sl] += _mm_tn(w, dos[h])
                e_pre = e_before + _cumsum_mm(e, before)
                dz = jnp.where(valid, e * (1.0 - sig) - e_pre * sig, 0.0) * scale
                dk_ref[rows, sl] += _mm_tn(dz, qs[h])
                out.append((dq + _mm_nn(dz, kj), lm_before + jnp.sum(lm, axis=1, keepdims=True),
                            e_before + jnp.sum(e, axis=1, keepdims=True)))
            return tuple(out)

        zero_col = jnp.zeros((SB_BLOCK, 1), F32)
        init = tuple((jnp.zeros((SB_BLOCK, HEAD_DIM), F32), zero_col, zero_col) for _ in heads)
        res = lax.fori_loop(0, i + 1, step, init)
        for h, sl in enumerate(heads):
            dq_ref[:, sl] = res[h][0]

    return pl.pallas_call(
        body, name="sb_attn_bwd", grid=(n_heads // hb, nq),
        in_specs=[pl.BlockSpec((SB_BLOCK, wide), lambda g, i: (i, g)),
                  pl.BlockSpec((t, wide), lambda g, i: (0, g)),
                  pl.BlockSpec((t, wide), lambda g, i: (0, v_col0 + g)),
                  pl.BlockSpec((SB_BLOCK, wide), lambda g, i: (i, g)),
                  pl.BlockSpec((SB_BLOCK, wide), lambda g, i: (i, g))],
        out_specs=[pl.BlockSpec((SB_BLOCK, wide), lambda g, i: (i, g)),
                   pl.BlockSpec((t, wide), lambda g, i: (0, g)),
                   pl.BlockSpec((t, wide), lambda g, i: (0, g))],
        out_shape=[jax.ShapeDtypeStruct((t, d), F32), jax.ShapeDtypeStruct((t, d), F32),
                   jax.ShapeDtypeStruct((t, d), F32)],
        compiler_params=_params(dimension_semantics=("arbitrary", "arbitrary")),
    )(qb, kb, proj, ltot, do)


def _gated_norm(oa, z, gn):
    return _rms(oa, gn, RMS_EPS) * _silu(z)


def _merge_gates(ya, yb, ga, gb):
    return jax.nn.sigmoid(ga) * ya + jax.nn.sigmoid(gb) * yb


def _merge_fwd(x1, oa, proj, ob, gn, wa, wb, wo, n_heads):
    t, d = x1.shape
    tm = _pick(t, (256, 128))

    def body(x_ref, oa_ref, z_ref, ob_ref, ga_ref, gb_ref, gn_ref, wa_ref, wb_ref, wo_ref, o_ref, na_ref):
        for h in range(n_heads):
            sl = slice(h * HEAD_DIM, (h + 1) * HEAD_DIM)
            na_ref[:, sl] = _gated_norm(oa_ref[:, sl], z_ref[:, sl], gn_ref[...]).astype(BF16)
        m = _merge_gates(_mm_nn(na_ref[...], wa_ref[...]), _mm_nn(ob_ref[...], wb_ref[...]), ga_ref[...], gb_ref[...])
        o_ref[...] = x_ref[...] + _mm_nn(m, wo_ref[...])

    tile = lambda k: pl.BlockSpec((tm, d), lambda i: (i, k))
    return pl.pallas_call(
        body, name="merge_fwd", grid=(t // tm,),
        in_specs=[tile(0), tile(0), tile(3), tile(0), tile(7), tile(8), _const_spec((1, HEAD_DIM)),
                  _const_spec((d, d), True), _const_spec((d, d), True), _const_spec((d, d), True)],
        out_specs=tile(0),
        out_shape=jax.ShapeDtypeStruct((t, d), F32),
        scratch_shapes=[pltpu.VMEM((tm, d), BF16)],
        compiler_params=_params(dimension_semantics=("arbitrary",)),
    )(x1, oa, proj, ob, proj, proj, gn, wa, wb, wo)


def _merge_bwd(oa, proj, ob, dy, gn, wa, wb, wo, n_heads):
    t, d = oa.shape
    tm = _pick(t, (256, 128))
    nt = t // tm

    def body(oa_ref, z_ref, ob_ref, ga_ref, gb_ref, dy_ref, gn_ref, wa_ref, wb_ref, wo_ref,
             doa_ref, dz_ref, dob_ref, dga_ref, dgb_ref, dgn_ref, dwa_hbm, dwb_hbm, dwo_hbm,
             na_ref, dna_ref, dwa_ref, dwb_ref, dwo_ref):
        i = pl.program_id(0)

        @pl.when(i == 0)
        def _():
            dgn_ref[...] = jnp.zeros((1, HEAD_DIM), F32)
            dwa_ref[...] = jnp.zeros((d, d), F32)
            dwb_ref[...] = jnp.zeros((d, d), F32)
            dwo_ref[...] = jnp.zeros((d, d), F32)

        for h in range(n_heads):
            sl = slice(h * HEAD_DIM, (h + 1) * HEAD_DIM)
            na_ref[:, sl] = _gated_norm(oa_ref[:, sl], z_ref[:, sl], gn_ref[...]).astype(BF16)
        dy = dy_ref[...].astype(BF16)
        ob = ob_ref[...].astype(BF16)
        ya = _mm_nn(na_ref[...], wa_ref[...])
        yb = _mm_nn(ob, wb_ref[...])
        m, vjp = jax.vjp(_merge_gates, ya, yb, ga_ref[...], gb_ref[...])
        dwo_ref[...] += _mm_tn(m, dy)
        dya, dyb, dga, dgb = vjp(_mm_nt(dy, wo_ref[...]))
        dga_ref[...] = dga.astype(BF16)
        dgb_ref[...] = dgb.astype(BF16)
        dwa_ref[...] += _mm_tn(na_ref[...], dya)
        dwb_ref[...] += _mm_tn(ob, dyb)
        dob_ref[...] = _mm_nt(dyb, wb_ref[...])
        dna_ref[...] = _mm_nt(dya, wa_ref[...])
        for h in range(n_heads):
            sl = slice(h * HEAD_DIM, (h + 1) * HEAD_DIM)
            _, vjp_h = jax.vjp(_gated_norm, oa_ref[:, sl], z_ref[:, sl], gn_ref[...])
            doa, dz, dgn = vjp_h(dna_ref[:, sl])
            doa_ref[:, sl] = doa
            dz_ref[:, sl] = dz.astype(BF16)
            dgn_ref[...] += dgn

        @pl.when(i == nt - 1)
        def _():
            pltpu.sync_copy(dwa_ref, dwa_hbm)
            pltpu.sync_copy(dwb_ref, dwb_hbm)
            pltpu.sync_copy(dwo_ref, dwo_hbm)

    tile = lambda k: pl.BlockSpec((tm, d), lambda i: (i, k))
    any_spec = pl.BlockSpec(memory_space=pl.ANY)
    return pl.pallas_call(
        body, name="merge_bwd", grid=(nt,),
        in_specs=[tile(0), tile(3), tile(0), tile(7), tile(8), tile(0), _const_spec((1, HEAD_DIM)),
                  _const_spec((d, d), True), _const_spec((d, d), True), _const_spec((d, d), True)],
        out_specs=[tile(0), tile(0), tile(0), tile(0), tile(0), _const_spec((1, HEAD_DIM)),
                   any_spec, any_spec, any_spec],
        out_shape=[jax.ShapeDtypeStruct((t, d), F32), jax.ShapeDtypeStruct((t, d), BF16),
                   jax.ShapeDtypeStruct((t, d), F32), jax.ShapeDtypeStruct((t, d), BF16),
                   jax.ShapeDtypeStruct((t, d), BF16), jax.ShapeDtypeStruct((1, HEAD_DIM), F32),
                   jax.ShapeDtypeStruct((d, d), F32), jax.ShapeDtypeStruct((d, d), F32),
                   jax.ShapeDtypeStruct((d, d), F32)],
        scratch_shapes=[pltpu.VMEM((tm, d), BF16), pltpu.VMEM((tm, d), F32),
                        pltpu.VMEM((d, d), F32), pltpu.VMEM((d, d), F32), pltpu.VMEM((d, d), F32)],
        compiler_params=_params(dimension_semantics=("arbitrary",)),
    )(oa, proj, ob, proj, proj, dy, gn, wa, wb, wo)


def _loss_head(y, target):
    t, d = y.shape
    tm = _pick(t, (256, 128))

    def body(y_ref, t_ref, dy_ref, loss_ref):
        @pl.when(pl.program_id(0) == 0)
        def _():
            loss_ref[...] = jnp.zeros((8, LANES), F32)

        err = y_ref[...] - t_ref[...]
        dy_ref[...] = err * (1.0 / d)
        per_token = jnp.sum(err * err, axis=1, keepdims=True) * (1.0 / d)
        loss_ref[...] += 0.5 * jnp.sum(per_token, axis=0, keepdims=True)

    return pl.pallas_call(
        body, name="loss_head", grid=(t // tm,),
        in_specs=[pl.BlockSpec((tm, d), lambda i: (i, 0)), pl.BlockSpec((tm, d), lambda i: (i, 0))],
        out_specs=[pl.BlockSpec((tm, d), lambda i: (i, 0)), _const_spec((8, LANES))],
        out_shape=[jax.ShapeDtypeStruct((t, d), F32), jax.ShapeDtypeStruct((8, LANES), F32)],
        compiler_params=_params(dimension_semantics=("arbitrary",)),
    )(y, target)


def _adamw(w, g, m, v):
    rows, cols = w.shape
    tr = rows
    for cand in (512, 256, 128, 64, 32, 16, 8):
        if rows % cand == 0 and cand * cols * 4 <= 2 * 1024 * 1024:
            tr = cand
            break

    def body(w_ref, g_ref, m_ref, v_ref, d_ref, mo_ref, vo_ref):
        g = g_ref[...]
        m2 = ADAM_B1 * m_ref[...] + (1.0 - ADAM_B1) * g
        v2 = ADAM_B2 * v_ref[...] + (1.0 - ADAM_B2) * (g * g)
        m_hat = m2 / (1.0 - ADAM_B1 ** ADAM_STEP)
        v_hat = v2 / (1.0 - ADAM_B2 ** ADAM_STEP)
        d_ref[...] = -ADAM_LR * (m_hat / (jnp.sqrt(v_hat) + ADAM_EPS) + ADAM_WD * w_ref[...])
        mo_ref[...] = m2
        vo_ref[...] = v2

    spec = pl.BlockSpec((tr, cols), lambda i: (i, 0))
    shape = jax.ShapeDtypeStruct((rows, cols), F32)
    return pl.pallas_call(
        body, name="adamw", grid=(rows // tr,), in_specs=[spec] * 4, out_specs=[spec] * 3,
        out_shape=[shape] * 3, compiler_params=_params(dimension_semantics=("arbitrary",)),
    )(w, g, m, v)


def _place():
    return lax.axis_index("x"), lax.axis_index("y"), lax.axis_index("c")


def _other_chips(x, y):
    return [(1 - x, y), (x, 1 - y), (1 - x, 1 - y)]


def _tile_rows(rows, cols, itemsize, cap=1536 * 1024):
    best = None
    for cand in range(16, rows + 1, 16):
        if rows % cand == 0 and cand * cols * itemsize <= cap:
            best = cand
    return best if best is not None else rows


def _allgather_weights(shards):
    n = len(shards)
    half = shards[0].shape[0] // 2

    def body(*refs):
        srcs, outs = refs[:n], refs[n:2 * n]
        send_sems, recv_sems, local_sems = refs[2 * n:]
        x, y, c = _place()
        me, sibling = (x, y, c), (x, y, 1 - c)
        chips = _other_chips(x, y)
        my_half = pl.ds(c * half, half)
        other_half = pl.ds((1 - c) * half, half)

        def copy(w, k, shard, part, to, from_src=False):
            return pltpu.make_async_remote_copy(
                src_ref=srcs[w].at[part] if from_src else outs[w].at[shard, part], dst_ref=outs[w].at[shard, part],
                send_sem=send_sems.at[6 * w + k], recv_sem=recv_sems.at[6 * w + k], device_id=to, device_id_type=MESH)

        local = [pltpu.make_async_copy(srcs[w], outs[w].at[2 * x + y], local_sems.at[w]) for w in range(n)]
        for cp in local:
            cp.start()
        first = [copy(w, j, 2 * x + y, my_half, (*chip, c), from_src=True)
                 for j, chip in enumerate(chips) for w in range(n)]
        for cp in first:
            cp.start()
        passed = []
        for j, (cx, cy) in enumerate(chips):
            for w in range(n):
                copy(w, j, 2 * cx + cy, my_half, me).wait_recv()
                cp = copy(w, 3 + j, 2 * cx + cy, my_half, sibling)
                cp.start()
                passed.append(cp)
        for j, (cx, cy) in enumerate(chips):
            for w in range(n):
                copy(w, 3 + j, 2 * cx + cy, other_half, me).wait_recv()
        for cp in first + passed:
            cp.wait_send()
        for cp in local:
            cp.wait()

    hbm = pl.BlockSpec(memory_space=pl.ANY)
    return pl.pallas_call(
        body, name="allgather_weights", in_specs=[hbm] * n, out_specs=[hbm] * n,
        out_shape=[jax.ShapeDtypeStruct((N_CHIPS,) + s.shape, s.dtype) for s in shards],
        scratch_shapes=[pltpu.SemaphoreType.DMA((6 * n,)), pltpu.SemaphoreType.DMA((6 * n,)),
                        pltpu.SemaphoreType.DMA((n,))],
    )(*shards)


def _swap_halves(grads):
    n = len(grads)
    half = grads[0].shape[1] // 2

    def body(*refs):
        gs, gots = refs[:n], refs[n:2 * n]
        send_sems, recv_sems = refs[2 * n:]
        x, y, c = _place()
        theirs = pl.ds((1 - c) * half, half)
        copies = [pltpu.make_async_remote_copy(src_ref=gs[w].at[:, theirs], dst_ref=gots[w], send_sem=send_sems.at[w],
                                               recv_sem=recv_sems.at[w], device_id=(x, y, 1 - c), device_id_type=MESH)
                  for w in range(n)]
        for cp in copies:
            cp.start()
        for cp in copies:
            cp.wait()

    hbm = pl.BlockSpec(memory_space=pl.ANY)
    return pl.pallas_call(
        body, name="swap_halves", in_specs=[hbm] * n, out_specs=[hbm] * n,
        out_shape=[jax.ShapeDtypeStruct((g.shape[0], half) + g.shape[2:], g.dtype) for g in grads],
        scratch_shapes=[pltpu.SemaphoreType.DMA((n,)), pltpu.SemaphoreType.DMA((n,))],
    )(*grads)


def _add_half(grad, got, c_idx):
    n, depth, rows, cols = grad.shape
    half = depth // 2
    tr = _tile_rows(rows, cols, 2)

    def body(c_ref, a_ref, b_ref, o_ref):
        o_ref[...] = (a_ref[...].astype(F32) + b_ref[...].astype(F32)).astype(o_ref.dtype)

    return pl.pallas_call(
        body, name="add_half",
        grid_spec=pltpu.PrefetchScalarGridSpec(
            num_scalar_prefetch=1, grid=(n, half, rows // tr),
            in_specs=[pl.BlockSpec((1, 1, tr, cols), lambda s, l, r, c_ref: (s, c_ref[0] * half + l, r, 0)),
                      pl.BlockSpec((1, 1, tr, cols), lambda s, l, r, c_ref: (s, l, r, 0))],
            out_specs=pl.BlockSpec((1, 1, tr, cols), lambda s, l, r, c_ref: (s, l, r, 0))),
        out_shape=jax.ShapeDtypeStruct((n, half, rows, cols), grad.dtype),
        compiler_params=_params(dimension_semantics=("arbitrary", "arbitrary", "arbitrary")),
    )(c_idx, grad, got)


def _scatter_partials(parts):
    n = len(parts)

    def body(*refs):
        ps, gots = refs[:n], refs[n:2 * n]
        send_sems, recv_sems = refs[2 * n:]
        x, y, c = _place()
        copies = [pltpu.make_async_remote_copy(src_ref=ps[w].at[2 * cx + cy], dst_ref=gots[w].at[j],
                                               send_sem=send_sems.at[3 * w + j], recv_sem=recv_sems.at[3 * w + j],
                                               device_id=(cx, cy, c), device_id_type=MESH)
                  for j, (cx, cy) in enumerate(_other_chips(x, y)) for w in range(n)]
        for cp in copies:
            cp.start()
        for cp in copies:
            cp.wait()

    hbm = pl.BlockSpec(memory_space=pl.ANY)
    return pl.pallas_call(
        body, name="scatter_partials", in_specs=[hbm] * n, out_specs=[hbm] * n,
        out_shape=[jax.ShapeDtypeStruct((N_CHIPS - 1,) + p.shape[1:], p.dtype) for p in parts],
        scratch_shapes=[pltpu.SemaphoreType.DMA((3 * n,)), pltpu.SemaphoreType.DMA((3 * n,))],
    )(*parts)


def _sum_partials(part, got, s_idx):
    n, half, rows, cols = part.shape
    tr = _tile_rows(rows, cols, 2, cap=1024 * 1024)

    def body(s_ref, a_ref, b_ref, o_ref):
        acc = a_ref[0, 0].astype(F32)
        for j in range(n - 1):
            acc = acc + b_ref[j, 0].astype(F32)
        o_ref[0] = acc

    return pl.pallas_call(
        body, name="sum_partials",
        grid_spec=pltpu.PrefetchScalarGridSpec(
            num_scalar_prefetch=1, grid=(half, rows // tr),
            in_specs=[pl.BlockSpec((1, 1, tr, cols), lambda l, r, s_ref: (s_ref[0], l, r, 0)),
                      pl.BlockSpec((n - 1, 1, tr, cols), lambda l, r, s_ref: (0, l, r, 0))],
            out_specs=pl.BlockSpec((1, tr, cols), lambda l, r, s_ref: (l, r, 0))),
        out_shape=jax.ShapeDtypeStruct((half, rows, cols), F32),
        compiler_params=_params(dimension_semantics=("arbitrary", "arbitrary")),
    )(s_idx, part, got)


def _join_halves(mine):
    n = len(mine)
    half = mine[0].shape[0]

    def body(*refs):
        srcs, outs = refs[:n], refs[n:2 * n]
        send_sems, recv_sems, local_sems = refs[2 * n:]
        x, y, c = _place()
        my_half = pl.ds(c * half, half)
        local = [pltpu.make_async_copy(srcs[w], outs[w].at[my_half], local_sems.at[w]) for w in range(n)]
        copies = [pltpu.make_async_remote_copy(src_ref=srcs[w], dst_ref=outs[w].at[my_half], send_sem=send_sems.at[w],
                                               recv_sem=recv_sems.at[w], device_id=(x, y, 1 - c), device_id_type=MESH)
                  for w in range(n)]
        for cp in local + copies:
            cp.start()
        for cp in copies + local:
            cp.wait()

    hbm = pl.BlockSpec(memory_space=pl.ANY)
    return pl.pallas_call(
        body, name="join_halves", in_specs=[hbm] * n, out_specs=[hbm] * n,
        out_shape=[jax.ShapeDtypeStruct((2 * half,) + m.shape[1:], F32) for m in mine],
        scratch_shapes=[pltpu.SemaphoreType.DMA((n,)), pltpu.SemaphoreType.DMA((n,)), pltpu.SemaphoreType.DMA((n,))],
    )(*mine)


def _allreduce_small(v, name):
    rows = v.shape[0]

    def body(v_ref, o_ref, gath, send_sems, recv_sems):
        x, y, c = _place()
        idx = 4 * x + 2 * y + c
        gath[0] = v_ref[...]
        copies = []
        for r in range(1, N_DEV):
            peer = (1 - x if r & 4 else x, 1 - y if r & 2 else y, 1 - c if r & 1 else c)
            cp = pltpu.make_async_remote_copy(src_ref=v_ref, dst_ref=gath.at[r], send_sem=send_sems.at[r - 1],
                                              recv_sem=recv_sems.at[r - 1], device_id=peer, device_id_type=MESH)
            cp.start()
            copies.append(cp)
        for cp in copies:
            cp.wait()
        acc = gath[idx]
        for a in range(1, N_DEV):
            acc = acc + gath[lax.bitwise_xor(idx, a)]
        o_ref[...] = acc

    vmem = pl.BlockSpec(memory_space=pltpu.VMEM)
    return pl.pallas_call(
        body, name=name, in_specs=[vmem], out_specs=vmem,
        out_shape=jax.ShapeDtypeStruct((rows, LANES), F32),
        scratch_shapes=[pltpu.VMEM((N_DEV, rows, LANES), F32), pltpu.SemaphoreType.DMA((N_DEV - 1,)),
                        pltpu.SemaphoreType.DMA((N_DEV - 1,))],
    )(v)


def _join_shards(name, gathered):
    return jnp.concatenate([gathered[s] for s in range(N_CHIPS)], axis=2 if name in COL_SHARDED else 1)


def _split_shards(name, per_layer):
    def shard(g, s):
        if name in COL_SHARDED:
            width = g.shape[1] // N_CHIPS
            return g[:, s * width:(s + 1) * width]
        height = g.shape[0] // N_CHIPS
        return g[s * height:(s + 1) * height, :]

    return jnp.stack([jnp.stack([shard(g, s).astype(BF16) for g in per_layer]) for s in range(N_CHIPS)])


def _pad_small(flat):
    n = flat.shape[0]
    block = 8 * LANES
    padded = -(-n // block) * block
    return jnp.pad(flat, (0, padded - n)).reshape(padded // LANES, LANES)


def kernel(x, ffn1_norm, ffn1_w_in, ffn1_w_out, mix_norm, w_in, dn_conv_w, dn_a_log, dn_dt_bias, dn_out_norm, sb_q_norm, sb_k_norm, w_branch_a, w_branch_b, w_out, ffn2_norm, ffn2_w_in, ffn2_w_out, loss_target, m_ffn1_norm, m_ffn1_w_in, m_ffn1_w_out, m_mix_norm, m_w_in, m_dn_conv_w, m_dn_a_log, m_dn_dt_bias, m_dn_out_norm, m_sb_q_norm, m_sb_k_norm, m_w_branch_a, m_w_branch_b, m_w_out, m_ffn2_norm, m_ffn2_w_in, m_ffn2_w_out, v_ffn1_norm, v_ffn1_w_in, v_ffn1_w_out, v_mix_norm, v_w_in, v_dn_conv_w, v_dn_a_log, v_dn_dt_bias, v_dn_out_norm, v_sb_q_norm, v_sb_k_norm, v_w_branch_a, v_w_branch_b, v_w_out, v_ffn2_norm, v_ffn2_w_in, v_ffn2_w_out):
    w = dict(ffn1_norm=ffn1_norm, ffn1_w_in=ffn1_w_in, ffn1_w_out=ffn1_w_out, mix_norm=mix_norm, w_in=w_in,
             dn_conv_w=dn_conv_w, dn_a_log=dn_a_log, dn_dt_bias=dn_dt_bias, dn_out_norm=dn_out_norm,
             sb_q_norm=sb_q_norm, sb_k_norm=sb_k_norm, w_branch_a=w_branch_a, w_branch_b=w_branch_b, w_out=w_out,
             ffn2_norm=ffn2_norm, ffn2_w_in=ffn2_w_in, ffn2_w_out=ffn2_w_out)
    mom = dict(ffn1_norm=m_ffn1_norm, ffn1_w_in=m_ffn1_w_in, ffn1_w_out=m_ffn1_w_out, mix_norm=m_mix_norm, w_in=m_w_in,
               dn_conv_w=m_dn_conv_w, dn_a_log=m_dn_a_log, dn_dt_bias=m_dn_dt_bias, dn_out_norm=m_dn_out_norm,
               sb_q_norm=m_sb_q_norm, sb_k_norm=m_sb_k_norm, w_branch_a=m_w_branch_a, w_branch_b=m_w_branch_b,
               w_out=m_w_out, ffn2_norm=m_ffn2_norm, ffn2_w_in=m_ffn2_w_in, ffn2_w_out=m_ffn2_w_out)
    var = dict(ffn1_norm=v_ffn1_norm, ffn1_w_in=v_ffn1_w_in, ffn1_w_out=v_ffn1_w_out, mix_norm=v_mix_norm, w_in=v_w_in,
               dn_conv_w=v_dn_conv_w, dn_a_log=v_dn_a_log, dn_dt_bias=v_dn_dt_bias, dn_out_norm=v_dn_out_norm,
               sb_q_norm=v_sb_q_norm, sb_k_norm=v_sb_k_norm, w_branch_a=v_w_branch_a, w_branch_b=v_w_branch_b,
               w_out=v_w_out, ffn2_norm=v_ffn2_norm, ffn2_w_in=v_ffn2_w_in, ffn2_w_out=v_ffn2_w_out)

    _, t, d = x.shape
    depth = ffn1_norm.shape[0]
    n_heads = d // HEAD_DIM
    d_ff = ffn1_w_out.shape[1] * N_CHIPS
    conv_cols = dn_conv_w.shape[2]
    assert d % HEAD_DIM == 0 and t % SB_BLOCK == 0 and 2 * n_heads <= LANES and depth % 2 == 0
    assert w_in.shape[2] * N_CHIPS == 9 * d + 2 * n_heads and conv_cols * N_CHIPS == 3 * d

    x_idx, y_idx, c_idx = _place()
    shard = 2 * x_idx + y_idx
    c_arr = jnp.reshape(c_idx, (1,)).astype(jnp.int32)
    s_arr = jnp.reshape(shard, (1,)).astype(jnp.int32)

    gathered = _allgather_weights([w[n].astype(BF16) for n in BIG])
    full = {n: _join_shards(n, g) for n, g in zip(BIG, gathered)}
    cut = 4 * d
    w_main = jnp.concatenate([full["w_in"][..., :cut], full["w_in"][..., cut + 2 * n_heads:]], axis=-1)
    w_ba = jnp.pad(full["w_in"][..., cut:cut + 2 * n_heads], ((0, 0), (0, 0), (0, LANES - 2 * n_heads)))

    conv_place = lax.dynamic_update_slice(jnp.zeros((depth, DN_CONV, 3 * d), F32), dn_conv_w, (0, 0, shard * conv_cols))
    conv_rows = _pad_small(conv_place.reshape(-1))
    conv_full = (0.5 * _allreduce_small(conv_rows, "allgather_conv")).reshape(-1)[:depth * DN_CONV * 3 * d]
    conv_full = jnp.pad(conv_full.reshape(depth, DN_CONV, 3 * d), ((0, 0), (0, CONV_ROWS - DN_CONV), (0, 0)))

    def head_row(vals):
        return jnp.pad(vals, (n_heads, LANES - 2 * n_heads)).reshape(1, LANES)

    saved = []
    cur = x[0]
    for l in range(depth):
        wg1, wu1 = full["ffn1_w_in"][l, :, :d_ff], full["ffn1_w_in"][l, :, d_ff:]
        wg2, wu2 = full["ffn2_w_in"][l, :, :d_ff], full["ffn2_w_in"][l, :, d_ff:]
        x0 = cur
        x1 = _ffn_fwd(x0, ffn1_norm[l][None], wg1, wu1, full["ffn1_w_out"][l])
        proj, ba = _proj_fwd(x1, mix_norm[l][None], w_main[l], w_ba[l])
        act = _dn_prep_fwd(proj, conv_full[l], n_heads)
        alog, dtb = head_row(dn_a_log[l]), head_row(dn_dt_bias[l])
        oa, snaps = _delta_fwd(act, ba, alog, dtb, n_heads)
        qb, kb = _sb_prep_fwd(proj, sb_q_norm[l][None], sb_k_norm[l][None], n_heads)
        ob, ltot = _sb_attn_fwd(qb, kb, proj, n_heads)
        x2 = _merge_fwd(x1, oa, proj, ob, dn_out_norm[l][None], full["w_branch_a"][l], full["w_branch_b"][l],
                        full["w_out"][l], n_heads)
        cur = _ffn_fwd(x2, ffn2_norm[l][None], wg2, wu2, full["ffn2_w_out"][l])
        saved.append((x0, x1, proj, ba, act, alog, dtb, oa, snaps, qb, kb, ob, ltot, x2, wg1, wu1, wg2, wu2))

    dcur, loss_part = _loss_head(cur, loss_target[0])

    grads = {n: [None] * depth for n in WEIGHTS}
    for l in reversed(range(depth)):
        x0, x1, proj, ba, act, alog, dtb, oa, snaps, qb, kb, ob, ltot, x2, wg1, wu1, wg2, wu2 = saved[l]
        dx2, dg, dwg, dwu, dwo = _ffn_bwd(x2, ffn2_norm[l][None], dcur, wg2, wu2, full["ffn2_w_out"][l])
        grads["ffn2_norm"][l] = dg[0]
        grads["ffn2_w_in"][l] = jnp.concatenate([dwg, dwu], axis=1)
        grads["ffn2_w_out"][l] = dwo
        doa, dz, dob, dga, dgb, dgn, dwa, dwb, dwout = _merge_bwd(
            oa, proj, ob, dx2, dn_out_norm[l][None], full["w_branch_a"][l], full["w_branch_b"][l], full["w_out"][l],
            n_heads)
        grads["dn_out_norm"][l] = dgn[0]
        grads["w_branch_a"][l], grads["w_branch_b"][l], grads["w_out"][l] = dwa, dwb, dwout
        dqb, dkb, dvb = _sb_attn_bwd(qb, kb, proj, ltot, dob, n_heads)
        dsq, dsk, dqn, dkn = _sb_prep_bwd(proj, sb_q_norm[l][None], sb_k_norm[l][None], dqb, dkb, n_heads)
        grads["sb_q_norm"][l], grads["sb_k_norm"][l] = dqn[0], dkn[0]
        dact, dba, dal, ddt = _delta_bwd(act, ba, alog, dtb, snaps, doa, n_heads)
        grads["dn_a_log"][l] = dal[0, n_heads:2 * n_heads]
        grads["dn_dt_bias"][l] = ddt[0, n_heads:2 * n_heads]
        dqkv, dconv = _dn_prep_bwd(proj, conv_full[l], dact, n_heads)
        grads["dn_conv_w"][l] = dconv[:DN_CONV]
        dproj = jnp.concatenate([dqkv, dz, dsq, dsk, dvb.astype(BF16), dga, dgb], axis=1)
        dx1, dg, dwm, dwba = _proj_bwd(x1, mix_norm[l][None], dx2, dproj, dba, w_main[l], w_ba[l])
        grads["mix_norm"][l] = dg[0]
        grads["w_in"][l] = jnp.concatenate([dwm[:, :cut], dwba[:, :2 * n_heads], dwm[:, cut:]], axis=1)
        dcur, dg, dwg, dwu, dwo = _ffn_bwd(x0, ffn1_norm[l][None], dx1, wg1, wu1, full["ffn1_w_out"][l])
        grads["ffn1_norm"][l] = dg[0]
        grads["ffn1_w_in"][l] = jnp.concatenate([dwg, dwu], axis=1)
        grads["ffn1_w_out"][l] = dwo

    g_major = [_split_shards(n, grads[n]) for n in BIG]
    parts = [_add_half(g, got, c_arr) for g, got in zip(g_major, _swap_halves(g_major))]
    halves = [_sum_partials(p, got, s_arr) for p, got in zip(parts, _scatter_partials(parts))]
    final = dict(zip(BIG, _join_halves(halves)))
    grads = {n: jnp.stack(grads[n]) for n in SMALL + ("dn_conv_w",)}

    small_names = SMALL + ("dn_conv_w",)
    small_sizes = [int(np.prod(grads[n].shape)) for n in small_names]
    small_off = np.concatenate([[0], np.cumsum(small_sizes)])
    small = jnp.concatenate([grads[n].reshape(-1) for n in small_names] + [loss_part[0, :1]])
    small_sum = _allreduce_small(_pad_small(small), "allreduce_small").reshape(-1)
    for i, n in enumerate(small_names):
        final[n] = small_sum[small_off[i]:small_off[i + 1]].reshape(grads[n].shape)
    final["dn_conv_w"] = lax.dynamic_slice(final["dn_conv_w"], (0, 0, shard * conv_cols), (depth, DN_CONV, conv_cols))
    loss = small_sum[small_off[-1]]

    deltas, new_m, new_v = {}, {}, {}
    for n in WEIGHTS:
        shape = w[n].shape
        flat = (-1, shape[-1])
        dl, m2, v2 = _adamw(w[n].reshape(flat), final[n].reshape(flat), mom[n].reshape(flat), var[n].reshape(flat))
        deltas[n], new_m[n], new_v[n] = dl.reshape(shape), m2.reshape(shape), v2.reshape(shape)

    grad_x = dcur[None]
    return (loss, grad_x, *[final[n] for n in WEIGHTS], *[deltas[n] for n in WEIGHTS],
            *[new_m[n] for n in WEIGHTS], *[new_v[n] for n in WEIGHTS])
```

```python
import functools

import jax
import jax.numpy as jnp
import numpy as np
from jax import lax
from jax.experimental import pallas as pl
from jax.experimental.pallas import tpu as pltpu

F32 = jnp.float32
BF16 = jnp.bfloat16

LANES = 128
HEAD_DIM = 128
DN_CHUNK = 64
DN_CONV = 4
CONV_ROWS = 8
SB_BLOCK = 128
FFN_HALF = 0.5
RMS_EPS = 1e-6
L2_EPS = 1e-6
NEG_BIG = -1e30
ADAM_LR = 0.001
ADAM_B1 = 0.9
ADAM_B2 = 0.999
ADAM_EPS = 1e-08
ADAM_WD = 0.01
ADAM_STEP = 10
VMEM_LIMIT = 56 * 1024 * 1024
N_CHIPS = 4
N_DEV = 8
MESH = pl.DeviceIdType.MESH

BIG = ("ffn1_w_in", "ffn1_w_out", "w_in", "w_branch_a", "w_branch_b", "w_out", "ffn2_w_in", "ffn2_w_out")
COL_SHARDED = ("ffn1_w_in", "w_in", "ffn2_w_in")
SMALL = ("ffn1_norm", "mix_norm", "dn_a_log", "dn_dt_bias", "dn_out_norm", "sb_q_norm", "sb_k_norm", "ffn2_norm")
WEIGHTS = ("ffn1_norm", "ffn1_w_in", "ffn1_w_out", "mix_norm", "w_in", "dn_conv_w", "dn_a_log", "dn_dt_bias",
           "dn_out_norm", "sb_q_norm", "sb_k_norm", "w_branch_a", "w_branch_b", "w_out", "ffn2_norm", "ffn2_w_in",
           "ffn2_w_out")


def _params(**kw):
    return pltpu.CompilerParams(vmem_limit_bytes=VMEM_LIMIT, **kw)


def _pick(n, options):
    for o in options:
        if n % o == 0:
            return o
    return n


def _const_spec(shape, single=False):
    nd = len(shape)
    if single:
        return pl.BlockSpec(shape, lambda *_: (0,) * nd, pipeline_mode=pl.Buffered(1))
    return pl.BlockSpec(shape, lambda *_: (0,) * nd)


_NN = ((1,), (0,))
_NT = ((1,), (1,))
_TN = ((0,), (0,))


def _dot(a, b, dims):
    return lax.dot_general(a.astype(BF16), b.astype(BF16), (dims, ((), ())), preferred_element_type=F32)


def _mm_nn(a, b):
    return _dot(a, b, _NN)


def _mm_nt(a, b):
    return _dot(a, b, _NT)


def _mm_tn(a, b):
    return _dot(a, b, _TN)


def _split(a):
    hi = a.astype(BF16)
    lo = (a - hi.astype(F32)).astype(BF16)
    return hi, lo


def _dot_precise(a, b, dims):
    dn = (dims, ((), ()))
    ah, al = _split(a)
    bh, bl = _split(b)
    out = lax.dot_general(ah, bh, dn, preferred_element_type=F32)
    out = out + lax.dot_general(ah, bl, dn, preferred_element_type=F32)
    return out + lax.dot_general(al, bh, dn, preferred_element_type=F32)


def _make_diff_mm(dot):
    @jax.custom_vjp
    def nn(a, b):
        return dot(a, b, _NN)

    @jax.custom_vjp
    def nt(a, b):
        return dot(a, b, _NT)

    @jax.custom_vjp
    def tn(a, b):
        return dot(a, b, _TN)

    nn.defvjp(lambda a, b: (dot(a, b, _NN), (a, b)), lambda r, g: (nt(g, r[1]), tn(r[0], g)))
    nt.defvjp(lambda a, b: (dot(a, b, _NT), (a, b)), lambda r, g: (nn(g, r[1]), tn(g, r[0])))
    tn.defvjp(lambda a, b: (dot(a, b, _TN), (a, b)), lambda r, g: (nt(r[1], g), nn(r[0], g)))
    return nn, nt, tn


_d_nn, _d_nt, _d_tn = _make_diff_mm(_dot)
_p_nn, _p_nt, _p_tn = _make_diff_mm(_dot_precise)


def _softplus_raw(x):
    return jnp.maximum(x, 0.0) + jnp.log(1.0 + jnp.exp(-jnp.abs(x)))


@jax.custom_vjp
def _softplus(x):
    return _softplus_raw(x)


_softplus.defvjp(lambda x: (_softplus_raw(x), x), lambda x, g: (g * jax.nn.sigmoid(x),))


def _rms(x, gain, eps):
    return x * lax.rsqrt(jnp.mean(x * x, axis=-1, keepdims=True) + eps) * gain


def _silu(x):
    return x * jax.nn.sigmoid(x)


def _shift_rows_raw(x, k, down):
    n = x.shape[0]
    row = lax.broadcasted_iota(jnp.int32, x.shape, 0)
    if down:
        return jnp.where(row >= k, pltpu.roll(x, k, 0), 0.0)
    return jnp.where(row < n - k, pltpu.roll(x, n - k, 0), 0.0)


@functools.partial(jax.custom_vjp, nondiff_argnums=(1,))
def _shift_down(x, k):
    return _shift_rows_raw(x, k, True)


_shift_down.defvjp(lambda x, k: (_shift_rows_raw(x, k, True), None),
                   lambda k, _, g: (_shift_rows_raw(g, k, False),))


def _layer_spec(layer, block, index_map, single=False):
    full_map = lambda *a: (layer,) + tuple(index_map(*a))
    if single:
        return pl.BlockSpec((None,) + block, full_map, pipeline_mode=pl.Buffered(1))
    return pl.BlockSpec((None,) + block, full_map)


def _ffn_fwd(x, gain, w_in, w_out, layer):
    t, d = x.shape
    f = w_out.shape[1]
    fc = _pick(f, (256, 128))
    nj = f // fc
    rt = _pick(t, (512, 256, 128))

    def body(x_ref, g_ref, wg_ref, wu_ref, wo_ref, o_ref, hs_ref):
        @pl.when(pl.program_id(0) == 0)
        def _():
            for r in range(t // rt):
                rows = pl.ds(r * rt, rt)
                xr = x_ref[rows, :]
                hs_ref[rows, :] = _rms(xr, g_ref[...], RMS_EPS).astype(BF16)
                o_ref[rows, :] = xr

        for r in range(t // rt):
            rows = pl.ds(r * rt, rt)
            h = hs_ref[rows, :]
            a = _mm_nn(h, wg_ref[...])
            b = _mm_nn(h, wu_ref[...])
            o_ref[rows, :] += FFN_HALF * _mm_nn(_silu(a) * b, wo_ref[...])

    return pl.pallas_call(
        body, name="ffn_fwd", grid=(nj,),
        in_specs=[_const_spec((t, d), True), _const_spec((1, d)),
                  _layer_spec(layer, (d, fc), lambda j: (0, j)), _layer_spec(layer, (d, fc), lambda j: (0, nj + j)),
                  _layer_spec(layer, (fc, d), lambda j: (j, 0))],
        out_specs=_const_spec((t, d)),
        out_shape=jax.ShapeDtypeStruct((t, d), F32),
        scratch_shapes=[pltpu.VMEM((t, d), BF16)],
        compiler_params=_params(dimension_semantics=("arbitrary",)),
    )(x, gain, w_in, w_in, w_out)


def _ffn_bwd(x, gain, dy, w_in, w_out, layer):
    t, d = x.shape
    f = w_out.shape[1]
    fc = _pick(f, (256, 128))
    nj = f // fc
    rt = _pick(t, (512, 256, 128))
    nr = t // rt

    def body(x_ref, g_ref, dy_ref, wg_ref, wu_ref, wo_ref, dx_ref, dg_ref, dwg_ref, dwu_ref, dwo_ref, hs_ref):
        j = pl.program_id(0)

        @pl.when(j == 0)
        def _():
            for r in range(nr):
                rows = pl.ds(r * rt, rt)
                hs_ref[rows, :] = _rms(x_ref[rows, :], g_ref[...], RMS_EPS).astype(BF16)
                dx_ref[rows, :] = jnp.zeros((rt, d), F32)

        for r in range(nr):
            rows = pl.ds(r * rt, rt)
            h = hs_ref[rows, :]
            dy2 = (FFN_HALF * dy_ref[rows, :]).astype(BF16)
            a = _mm_nn(h, wg_ref[...])
            b = _mm_nn(h, wu_ref[...])
            sig = jax.nn.sigmoid(a)
            sa = a * sig
            ds = _mm_nt(dy2, wo_ref[...])
            da = ds * b * (sig * (1.0 + a * (1.0 - sig)))
            db = ds * sa
            dx_ref[rows, :] += _mm_nt(da, wg_ref[...]) + _mm_nt(db, wu_ref[...])
            dwo_c = _mm_tn(sa * b, dy2)
            dwg_c = _mm_tn(h, da)
            dwu_c = _mm_tn(h, db)
            if r == 0:
                dwo_ref[...] = dwo_c
                dwg_ref[...] = dwg_c
                dwu_ref[...] = dwu_c
            else:
                dwo_ref[...] += dwo_c
                dwg_ref[...] += dwg_c
                dwu_ref[...] += dwu_c

        @pl.when(j == nj - 1)
        def _():
            for r in range(nr):
                rows = pl.ds(r * rt, rt)
                _, vjp = jax.vjp(lambda xx, gg: _rms(xx, gg, RMS_EPS), x_ref[rows, :], g_ref[...])
                dxn, dgr = vjp(dx_ref[rows, :])
                dx_ref[rows, :] = dy_ref[rows, :] + dxn
                if r == 0:
                    dg_ref[...] = dgr
                else:
                    dg_ref[...] += dgr

    return pl.pallas_call(
        body, name="ffn_bwd", grid=(nj,),
        in_specs=[_const_spec((t, d), True), _const_spec((1, d)), _const_spec((t, d), True),
                  _layer_spec(layer, (d, fc), lambda j: (0, j)), _layer_spec(layer, (d, fc), lambda j: (0, nj + j)),
                  _layer_spec(layer, (fc, d), lambda j: (j, 0))],
        out_specs=[_const_spec((t, d)), _const_spec((1, d)),
                   pl.BlockSpec((d, fc), lambda j: (0, j)), pl.BlockSpec((d, fc), lambda j: (0, j)),
                   pl.BlockSpec((fc, d), lambda j: (j, 0))],
        out_shape=[jax.ShapeDtypeStruct((t, d), F32), jax.ShapeDtypeStruct((1, d), F32),
                   jax.ShapeDtypeStruct((d, f), F32), jax.ShapeDtypeStruct((d, f), F32),
                   jax.ShapeDtypeStruct((f, d), F32)],
        scratch_shapes=[pltpu.VMEM((t, d), BF16)],
        compiler_params=_params(dimension_semantics=("arbitrary",)),
    )(x, gain, dy, w_in, w_in, w_out)


def _proj_fwd(x, gain, w, wba, layer):
    t, d = x.shape
    n = w.shape[2]
    nc = _pick(n, (512, 256, 128))
    rt = _pick(t, (512, 256, 128))

    def body(x_ref, g_ref, w_ref, wba_ref, p_ref, ba_ref, hs_ref):
        @pl.when(pl.program_id(0) == 0)
        def _():
            for r in range(t // rt):
                rows = pl.ds(r * rt, rt)
                h = _rms(x_ref[rows, :], g_ref[...], RMS_EPS).astype(BF16)
                hs_ref[rows, :] = h
                ba_ref[rows, :] = _mm_nn(h, wba_ref[...])

        for r in range(t // rt):
            rows = pl.ds(r * rt, rt)
            p_ref[rows, :] = _mm_nn(hs_ref[rows, :], w_ref[...])

    return pl.pallas_call(
        body, name="proj_fwd", grid=(n // nc,),
        in_specs=[_const_spec((t, d), True), _const_spec((1, d)),
                  _layer_spec(layer, (d, nc), lambda j: (0, j)), _layer_spec(layer, (d, LANES), lambda j: (0, 0))],
        out_specs=[pl.BlockSpec((t, nc), lambda j: (0, j)), _const_spec((t, LANES))],
        out_shape=[jax.ShapeDtypeStruct((t, n), F32), jax.ShapeDtypeStruct((t, LANES), F32)],
        scratch_shapes=[pltpu.VMEM((t, d), BF16)],
        compiler_params=_params(dimension_semantics=("arbitrary",)),
    )(x, gain, w, wba)


def _proj_bwd(x, gain, dres, dp, dba, w, wba, layer):
    t, d = x.shape
    n = w.shape[2]
    nc = _pick(n, (512, 256, 128))
    nj = n // nc
    rt = _pick(t, (512, 256, 128))
    nr = t // rt

    def body(x_ref, g_ref, dres_ref, dp_ref, dba_ref, w_ref, wba_ref, dx_ref, dg_ref, dw_ref, dwba_ref, hs_ref):
        j = pl.program_id(0)

        @pl.when(j == 0)
        def _():
            for r in range(nr):
                rows = pl.ds(r * rt, rt)
                h = _rms(x_ref[rows, :], g_ref[...], RMS_EPS).astype(BF16)
                hs_ref[rows, :] = h
                g = dba_ref[rows, :]
                dx_ref[rows, :] = _mm_nt(g, wba_ref[...])
                if r == 0:
                    dwba_ref[...] = _mm_tn(h, g)
                else:
                    dwba_ref[...] += _mm_tn(h, g)

        for r in range(nr):
            rows = pl.ds(r * rt, rt)
            g = dp_ref[rows, :]
            dx_ref[rows, :] += _mm_nt(g, w_ref[...])
            if r == 0:
                dw_ref[...] = _mm_tn(hs_ref[rows, :], g)
            else:
                dw_ref[...] += _mm_tn(hs_ref[rows, :], g)

        @pl.when(j == nj - 1)
        def _():
            for r in range(nr):
                rows = pl.ds(r * rt, rt)
                _, vjp = jax.vjp(lambda xx, gg: _rms(xx, gg, RMS_EPS), x_ref[rows, :], g_ref[...])
                dxn, dgr = vjp(dx_ref[rows, :])
                dx_ref[rows, :] = dres_ref[rows, :] + dxn
                if r == 0:
                    dg_ref[...] = dgr
                else:
                    dg_ref[...] += dgr

    return pl.pallas_call(
        body, name="proj_bwd", grid=(nj,),
        in_specs=[_const_spec((t, d), True), _const_spec((1, d)), _const_spec((t, d), True),
                  pl.BlockSpec((t, nc), lambda j: (0, j)), _const_spec((t, LANES)),
                  _layer_spec(layer, (d, nc), lambda j: (0, j)), _layer_spec(layer, (d, LANES), lambda j: (0, 0))],
        out_specs=[_const_spec((t, d)), _const_spec((1, d)),
                   pl.BlockSpec((d, nc), lambda j: (0, j)), _const_spec((d, LANES))],
        out_shape=[jax.ShapeDtypeStruct((t, d), F32), jax.ShapeDtypeStruct((1, d), F32),
                   jax.ShapeDtypeStruct((d, n), F32), jax.ShapeDtypeStruct((d, LANES), F32)],
        scratch_shapes=[pltpu.VMEM((t, d), BF16)],
        compiler_params=_params(dimension_semantics=("arbitrary",)),
    )(x, gain, dres, dp, dba, w, wba)


def _conv_act(x, w0, w1, w2, w3, is_qk):
    y = w3 * x + w2 * _shift_down(x, 1) + w1 * _shift_down(x, 2) + w0 * _shift_down(x, 3)
    y = _silu(y)
    inv = lax.rsqrt(jnp.sum(y * y, axis=-1, keepdims=True) + L2_EPS)
    return y * (is_qk * inv + (1.0 - is_qk))


def _taps(w_ref):
    return tuple(w_ref[i:i + 1, :] for i in range(DN_CONV))


def _dn_prep_fwd(proj, conv_w, n_heads):
    t = proj.shape[0]
    nb = 3 * n_heads

    def body(x_ref, w_ref, o_ref):
        is_qk = jnp.where(pl.program_id(0) < 2 * n_heads, 1.0, 0.0).astype(F32)
        o_ref[...] = _conv_act(x_ref[...], *_taps(w_ref), is_qk)

    return pl.pallas_call(
        body, name="dn_prep_fwd", grid=(nb,),
        in_specs=[pl.BlockSpec((t, HEAD_DIM), lambda i: (0, i)), pl.BlockSpec((CONV_ROWS, HEAD_DIM), lambda i: (0, i))],
        out_specs=pl.BlockSpec((t, HEAD_DIM), lambda i: (0, i)),
        out_shape=jax.ShapeDtypeStruct((t, nb * HEAD_DIM), F32),
        compiler_params=_params(dimension_semantics=("arbitrary",)),
    )(proj, conv_w)


def _dn_prep_bwd(proj, conv_w, dact, n_heads):
    t = proj.shape[0]
    nb = 3 * n_heads

    def body(x_ref, w_ref, g_ref, dx_ref, dw_ref):
        is_qk = jnp.where(pl.program_id(0) < 2 * n_heads, 1.0, 0.0).astype(F32)
        _, vjp = jax.vjp(lambda x, a, b, c, e: _conv_act(x, a, b, c, e, is_qk), x_ref[...], *_taps(w_ref))
        dx, d0, d1, d2, d3 = vjp(g_ref[...])
        dx_ref[...] = dx.astype(BF16)
        dw_ref[...] = jnp.concatenate([d0, d1, d2, d3, jnp.zeros((CONV_ROWS - DN_CONV, HEAD_DIM), F32)], axis=0)

    return pl.pallas_call(
        body, name="dn_prep_bwd", grid=(nb,),
        in_specs=[pl.BlockSpec((t, HEAD_DIM), lambda i: (0, i)), pl.BlockSpec((CONV_ROWS, HEAD_DIM), lambda i: (0, i)),
                  pl.BlockSpec((t, HEAD_DIM), lambda i: (0, i))],
        out_specs=[pl.BlockSpec((t, HEAD_DIM), lambda i: (0, i)), pl.BlockSpec((CONV_ROWS, HEAD_DIM), lambda i: (0, i))],
        out_shape=[jax.ShapeDtypeStruct((t, nb * HEAD_DIM), BF16), jax.ShapeDtypeStruct((CONV_ROWS, nb * HEAD_DIM), F32)],
        compiler_params=_params(dimension_semantics=("arbitrary",)),
    )(proj, conv_w, dact)


def _unit_lower_inverses(lmats, c):
    r = lax.broadcasted_iota(jnp.int32, (c, c), 0)
    q = lax.broadcasted_iota(jnp.int32, (c, c), 1)
    eye = jnp.where(r == q, 1.0, 0.0)
    ps = [eye - l for l in lmats]
    ms = [_p_nn(l, l) for l in lmats]
    n = 2
    while True:
        ps = [p + _p_nn(p, m) for p, m in zip(ps, ms)]
        if 2 * n >= c:
            return ps
        ms = [_p_nn(m, m) for m in ms]
        n *= 2


def _delta_heads(qs, ks, vs, bg, alog, dtb, states):
    n_heads = len(qs)
    heads = range(n_heads)
    c = qs[0].shape[0]
    lane = lax.broadcasted_iota(jnp.int32, (c, LANES), 1)
    r = lax.broadcasted_iota(jnp.int32, (c, c), 0)
    s = lax.broadcasted_iota(jnp.int32, (c, c), 1)
    beta_all = jax.nn.sigmoid(bg)
    g_all = -jnp.exp(alog) * _softplus(bg + dtb)
    beta = [jnp.sum(jnp.where(lane == h, beta_all, 0.0), axis=1, keepdims=True) for h in heads]
    g = [jnp.sum(jnp.where(lane == n_heads + h, g_all, 0.0), axis=1, keepdims=True) for h in heads]
    g_row = [jnp.sum(jnp.where(r == s, g[h], 0.0), axis=0, keepdims=True) for h in heads]
    gc = [jnp.sum(jnp.where(s <= r, g_row[h], 0.0), axis=1, keepdims=True) for h in heads]
    gr = [jnp.sum(jnp.where(r <= s, g[h], 0.0), axis=0, keepdims=True) for h in heads]
    g_last = [jnp.sum(g[h], axis=0, keepdims=True) for h in heads]
    decay = [jnp.exp(jnp.where(r >= s, gc[h] - gr[h], NEG_BIG)) for h in heads]
    q_scaled = [qs[h] * (HEAD_DIM ** -0.5) for h in heads]
    k_beta = [ks[h] * beta[h] for h in heads]
    lmat = [jnp.where(r > s, _d_nt(k_beta[h], ks[h]) * decay[h], 0.0) for h in heads]
    attn = [_d_nt(q_scaled[h], ks[h]) * decay[h] for h in heads]
    tinv = _unit_lower_inverses(lmat, c)
    u = [_p_nn(tinv[h], vs[h] * beta[h]) for h in heads]
    w = [_p_nn(tinv[h], k_beta[h] * jnp.exp(gc[h])) for h in heads]
    v_new = [u[h] - _d_nn(w[h], states[h]) for h in heads]
    o_state = [_d_nn(q_scaled[h] * jnp.exp(gc[h]), states[h]) for h in heads]
    o = [o_state[h] + _d_nn(attn[h], v_new[h]) for h in heads]
    kv = [_d_tn(ks[h] * jnp.exp(g_last[h] - gc[h]), v_new[h]) for h in heads]
    new_states = [states[h] * jnp.exp(g_last[h]) + kv[h] for h in heads]
    return tuple(o), tuple(new_states)


def _delta_fwd(act, ba, alog, dtb, n_heads):
    t = act.shape[0]
    d = n_heads * HEAD_DIM
    c = DN_CHUNK
    nc = t // c

    def body(q_ref, k_ref, v_ref, bg_ref, al_ref, dt_ref, o_ref, snap_ref, st_ref):
        @pl.when(pl.program_id(0) == 0)
        def _():
            st_ref[...] = jnp.zeros(st_ref.shape, F32)

        snap_ref[0] = st_ref[...]
        cols = [slice(h * HEAD_DIM, (h + 1) * HEAD_DIM) for h in range(n_heads)]
        os, new_states = _delta_heads([q_ref[:, sl] for sl in cols], [k_ref[:, sl] for sl in cols],
                                      [v_ref[:, sl] for sl in cols], bg_ref[...], al_ref[...], dt_ref[...],
                                      [st_ref[h] for h in range(n_heads)])
        for h, sl in enumerate(cols):
            o_ref[:, sl] = os[h]
            st_ref[h] = new_states[h]

    return pl.pallas_call(
        body, name="delta_fwd", grid=(nc,),
        in_specs=[pl.BlockSpec((c, d), lambda i: (i, 0)), pl.BlockSpec((c, d), lambda i: (i, 1)),
                  pl.BlockSpec((c, d), lambda i: (i, 2)), pl.BlockSpec((c, LANES), lambda i: (i, 0)),
                  _const_spec((1, LANES)), _const_spec((1, LANES))],
        out_specs=[pl.BlockSpec((c, d), lambda i: (i, 0)),
                   pl.BlockSpec((1, n_heads, HEAD_DIM, HEAD_DIM), lambda i: (i, 0, 0, 0))],
        out_shape=[jax.ShapeDtypeStruct((t, d), F32), jax.ShapeDtypeStruct((nc, n_heads, HEAD_DIM, HEAD_DIM), F32)],
        scratch_shapes=[pltpu.VMEM((n_heads, HEAD_DIM, HEAD_DIM), F32)],
        compiler_params=_params(dimension_semantics=("arbitrary",)),
    )(act, act, act, ba, alog, dtb)


def _delta_bwd(act, ba, alog, dtb, snaps, do, n_heads):
    t = act.shape[0]
    d = n_heads * HEAD_DIM
    c = DN_CHUNK
    nc = t // c

    def body(q_ref, k_ref, v_ref, bg_ref, al_ref, dt_ref, snap_ref, do_ref,
             dact_ref, dbg_ref, dal_ref, ddt_ref, ds_ref):
        @pl.when(pl.program_id(0) == 0)
        def _():
            ds_ref[...] = jnp.zeros(ds_ref.shape, F32)
            dal_ref[...] = jnp.zeros((1, LANES), F32)
            ddt_ref[...] = jnp.zeros((1, LANES), F32)

        heads = range(n_heads)
        cols = [slice(h * HEAD_DIM, (h + 1) * HEAD_DIM) for h in heads]
        _, vjp = jax.vjp(_delta_heads, tuple(q_ref[:, sl] for sl in cols), tuple(k_ref[:, sl] for sl in cols),
                         tuple(v_ref[:, sl] for sl in cols), bg_ref[...], al_ref[...], dt_ref[...],
                         tuple(snap_ref[0, h] for h in heads))
        dq, dk, dv, dbg, dal, ddt, dst = vjp((tuple(do_ref[:, sl] for sl in cols), tuple(ds_ref[h] for h in heads)))
        for h, sl in enumerate(cols):
            dact_ref[:, sl] = dq[h]
            dact_ref[:, d + h * HEAD_DIM:d + (h + 1) * HEAD_DIM] = dk[h]
            dact_ref[:, 2 * d + h * HEAD_DIM:2 * d + (h + 1) * HEAD_DIM] = dv[h]
            ds_ref[h] = dst[h]
        dal_ref[...] += dal
        ddt_ref[...] += ddt
        dbg_ref[...] = dbg.astype(BF16)

    rev = lambda i: nc - 1 - i
    return pl.pallas_call(
        body, name="delta_bwd", grid=(nc,),
        in_specs=[pl.BlockSpec((c, d), lambda i: (rev(i), 0)), pl.BlockSpec((c, d), lambda i: (rev(i), 1)),
                  pl.BlockSpec((c, d), lambda i: (rev(i), 2)), pl.BlockSpec((c, LANES), lambda i: (rev(i), 0)),
                  _const_spec((1, LANES)), _const_spec((1, LANES)),
                  pl.BlockSpec((1, n_heads, HEAD_DIM, HEAD_DIM), lambda i: (rev(i), 0, 0, 0)),
                  pl.BlockSpec((c, d), lambda i: (rev(i), 0))],
        out_specs=[pl.BlockSpec((c, 3 * d), lambda i: (rev(i), 0)), pl.BlockSpec((c, LANES), lambda i: (rev(i), 0)),
                   _const_spec((1, LANES)), _const_spec((1, LANES))],
        out_shape=[jax.ShapeDtypeStruct((t, 3 * d), F32), jax.ShapeDtypeStruct((t, LANES), BF16),
                   jax.ShapeDtypeStruct((1, LANES), F32), jax.ShapeDtypeStruct((1, LANES), F32)],
        scratch_shapes=[pltpu.VMEM((n_heads, HEAD_DIM, HEAD_DIM), F32)],
        compiler_params=_params(dimension_semantics=("arbitrary",)),
    )(act, act, act, ba, alog, dtb, snaps, do)


def _head_norm2(a, b, ga, gb):
    return _rms(a, ga, RMS_EPS), _rms(b, gb, RMS_EPS)


def _sb_prep_fwd(proj, qn, kn, n_heads):
    t = proj.shape[0]
    d = n_heads * HEAD_DIM
    tm = _pick(t, (256, 128))

    def body(q_ref, k_ref, qn_ref, kn_ref, qo_ref, ko_ref):
        for h in range(n_heads):
            sl = slice(h * HEAD_DIM, (h + 1) * HEAD_DIM)
            qo_ref[:, sl], ko_ref[:, sl] = _head_norm2(q_ref[:, sl], k_ref[:, sl], qn_ref[...], kn_ref[...])

    return pl.pallas_call(
        body, name="sb_prep_fwd", grid=(t // tm,),
        in_specs=[pl.BlockSpec((tm, d), lambda i: (i, 4)), pl.BlockSpec((tm, d), lambda i: (i, 5)),
                  _const_spec((1, HEAD_DIM)), _const_spec((1, HEAD_DIM))],
        out_specs=[pl.BlockSpec((tm, d), lambda i: (i, 0)), pl.BlockSpec((tm, d), lambda i: (i, 0))],
        out_shape=[jax.ShapeDtypeStruct((t, d), F32), jax.ShapeDtypeStruct((t, d), F32)],
        compiler_params=_params(dimension_semantics=("arbitrary",)),
    )(proj, proj, qn, kn)


def _sb_prep_bwd(proj, qn, kn, dq, dk, n_heads):
    t = proj.shape[0]
    d = n_heads * HEAD_DIM
    tm = _pick(t, (256, 128))

    def body(q_ref, k_ref, qn_ref, kn_ref, dq_ref, dk_ref, dqo_ref, dko_ref, dqn_ref, dkn_ref):
        @pl.when(pl.program_id(0) == 0)
        def _():
            dqn_ref[...] = jnp.zeros((1, HEAD_DIM), F32)
            dkn_ref[...] = jnp.zeros((1, HEAD_DIM), F32)

        for h in range(n_heads):
            sl = slice(h * HEAD_DIM, (h + 1) * HEAD_DIM)
            _, vjp = jax.vjp(_head_norm2, q_ref[:, sl], k_ref[:, sl], qn_ref[...], kn_ref[...])
            da, db, dga, dgb = vjp((dq_ref[:, sl], dk_ref[:, sl]))
            dqo_ref[:, sl] = da.astype(BF16)
            dko_ref[:, sl] = db.astype(BF16)
            dqn_ref[...] += dga
            dkn_ref[...] += dgb

    return pl.pallas_call(
        body, name="sb_prep_bwd", grid=(t // tm,),
        in_specs=[pl.BlockSpec((tm, d), lambda i: (i, 4)), pl.BlockSpec((tm, d), lambda i: (i, 5)),
                  _const_spec((1, HEAD_DIM)), _const_spec((1, HEAD_DIM)),
                  pl.BlockSpec((tm, d), lambda i: (i, 0)), pl.BlockSpec((tm, d), lambda i: (i, 0))],
        out_specs=[pl.BlockSpec((tm, d), lambda i: (i, 0)), pl.BlockSpec((tm, d), lambda i: (i, 0)),
                   _const_spec((1, HEAD_DIM)), _const_spec((1, HEAD_DIM))],
        out_shape=[jax.ShapeDtypeStruct((t, d), BF16), jax.ShapeDtypeStruct((t, d), BF16),
                   jax.ShapeDtypeStruct((1, HEAD_DIM), F32), jax.ShapeDtypeStruct((1, HEAD_DIM), F32)],
        compiler_params=_params(dimension_semantics=("arbitrary",)),
    )(proj, proj, qn, kn, dq, dk)


def _cumsum_mm(x, tri):
    hi, lo = _split(x)
    return (lax.dot_general(hi, tri, (_NN, ((), ())), preferred_element_type=F32)
            + lax.dot_general(lo, tri, (_NN, ((), ())), preferred_element_type=F32))


def _sb_valid(i, j):
    row = lax.broadcasted_iota(jnp.int32, (SB_BLOCK, SB_BLOCK), 0)
    col = lax.broadcasted_iota(jnp.int32, (SB_BLOCK, SB_BLOCK), 1)
    return (col + j * SB_BLOCK) < (row + i * SB_BLOCK)


def _sb_attn_fwd(qb, kb, proj, n_heads):
    t = qb.shape[0]
    d = n_heads * HEAD_DIM
    nq = t // SB_BLOCK
    hb = _pick(n_heads, (4, 2, 1))
    wide = hb * HEAD_DIM
    v_col0 = 6 * n_heads // hb
    scale = HEAD_DIM ** -0.5

    def body(q_ref, k_ref, v_ref, o_ref, lt_ref):
        i = pl.program_id(1)
        row = lax.broadcasted_iota(jnp.int32, (SB_BLOCK, SB_BLOCK), 0)
        col = lax.broadcasted_iota(jnp.int32, (SB_BLOCK, SB_BLOCK), 1)
        after = jnp.where(row > col, 1.0, 0.0).astype(BF16)
        heads = [slice(h * HEAD_DIM, (h + 1) * HEAD_DIM) for h in range(hb)]
        every = range(hb)
        qs = [q_ref[:, sl].astype(BF16) for sl in heads]

        def step(jj, carry):
            j = i - jj
            rows = pl.ds(pl.multiple_of(j * SB_BLOCK, SB_BLOCK), SB_BLOCK)
            valid = _sb_valid(i, j)
            z = [_mm_nt(qs[h], k_ref[rows, heads[h]]) * scale for h in every]
            sp = [_softplus_raw(z[h]) for h in every]
            lm = [jnp.where(valid, -sp[h], 0.0) for h in every]
            tail = [_cumsum_mm(lm[h], after) for h in every]
            w = [jnp.where(valid, jnp.exp(z[h] - sp[h] + carry[h][1] + tail[h]), 0.0) for h in every]
            pv = [_mm_nn(w[h], v_ref[rows, heads[h]]) for h in every]
            return tuple((carry[h][0] + pv[h], carry[h][1] + jnp.sum(lm[h], axis=1, keepdims=True)) for h in every)

        init = tuple((jnp.zeros((SB_BLOCK, HEAD_DIM), F32), jnp.zeros((SB_BLOCK, 1), F32)) for _ in heads)
        res = lax.fori_loop(0, i + 1, step, init)
        for h, sl in enumerate(heads):
            o_ref[:, sl] = res[h][0]
            lt_ref[:, sl] = jnp.broadcast_to(res[h][1], (SB_BLOCK, HEAD_DIM))

    return pl.pallas_call(
        body, name="sb_attn_fwd", grid=(n_heads // hb, nq),
        in_specs=[pl.BlockSpec((SB_BLOCK, wide), lambda g, i: (i, g)),
                  pl.BlockSpec((t, wide), lambda g, i: (0, g)),
                  pl.BlockSpec((t, wide), lambda g, i: (0, v_col0 + g))],
        out_specs=[pl.BlockSpec((SB_BLOCK, wide), lambda g, i: (i, g)),
                   pl.BlockSpec((SB_BLOCK, wide), lambda g, i: (i, g))],
        out_shape=[jax.ShapeDtypeStruct((t, d), F32), jax.ShapeDtypeStruct((t, d), F32)],
        compiler_params=_params(dimension_semantics=("arbitrary", "arbitrary")),
    )(qb, kb, proj)


def _sb_attn_bwd(qb, kb, proj, ltot, do, n_heads):
    t = qb.shape[0]
    d = n_heads * HEAD_DIM
    nq = t // SB_BLOCK
    hb = _pick(n_heads, (4, 2, 1))
    wide = hb * HEAD_DIM
    v_col0 = 6 * n_heads // hb
    scale = HEAD_DIM ** -0.5

    def body(q_ref, k_ref, v_ref, lt_ref, do_ref, dq_ref, dk_ref, dv_ref):
        i = pl.program_id(1)

        @pl.when(i == 0)
        def _():
            dk_ref[...] = jnp.zeros((t, wide), F32)
            dv_ref[...] = jnp.zeros((t, wide), F32)

        row = lax.broadcasted_iota(jnp.int32, (SB_BLOCK, SB_BLOCK), 0)
        col = lax.broadcasted_iota(jnp.int32, (SB_BLOCK, SB_BLOCK), 1)
        upto = jnp.where(row <= col, 1.0, 0.0).astype(BF16)
        before = jnp.where(row < col, 1.0, 0.0).astype(BF16)
        heads = [slice(h * HEAD_DIM, (h + 1) * HEAD_DIM) for h in range(hb)]
        every = range(hb)
        qs = [q_ref[:, sl].astype(BF16) for sl in heads]
        dos = [do_ref[:, sl].astype(BF16) for sl in heads]
        totals = [jnp.max(lt_ref[:, sl], axis=1, keepdims=True) for sl in heads]

        def step(j, carry):
            rows = pl.ds(pl.multiple_of(j * SB_BLOCK, SB_BLOCK), SB_BLOCK)
            valid = _sb_valid(i, j)
            kj = [k_ref[rows, heads[h]].astype(BF16) for h in every]
            vj = [v_ref[rows, heads[h]].astype(BF16) for h in every]
            z = [_mm_nt(qs[h], kj[h]) * scale for h in every]
            dw = [_mm_nt(dos[h], vj[h]) for h in every]
            sp = [_softplus_raw(z[h]) for h in every]
            lm = [jnp.where(valid, -sp[h], 0.0) for h in every]
            head = [_cumsum_mm(lm[h], upto) for h in every]
            w = [jnp.where(valid, jnp.exp(z[h] - sp[h] + totals[h] - (carry[h][1] + head[h])), 0.0) for h in every]
            e = [w[h] * dw[h] for h in every]
            e_pre = [carry[h][2] + _cumsum_mm(e[h], before) for h in every]
            sig = [jnp.exp(z[h] - sp[h]) for h in every]
            dz = [jnp.where(valid, e[h] * (1.0 - sig[h]) - e_pre[h] * sig[h], 0.0) * scale for h in every]
            for h in every:
                dv_ref[rows, heads[h]] += _mm_tn(w[h], dos[h])
            for h in every:
                dk_ref[rows, heads[h]] += _mm_tn(dz[h], qs[h])
            dq = [_mm_nn(dz[h], kj[h]) for h in every]
            return tuple((carry[h][0] + dq[h], carry[h][1] + jnp.sum(lm[h], axis=1, keepdims=True),
                          carry[h][2] + jnp.sum(e[h], axis=1, keepdims=True)) for h in every)

        zero_col = jnp.zeros((SB_BLOCK, 1), F32)
        init = tuple((jnp.zeros((SB_BLOCK, HEAD_DIM), F32), zero_col, zero_col) for _ in heads)
        res = lax.fori_loop(0, i + 1, step, init)
        for h, sl in enumerate(heads):
            dq_ref[:, sl] = res[h][0]

    return pl.pallas_call(
        body, name="sb_attn_bwd", grid=(n_heads // hb, nq),
        in_specs=[pl.BlockSpec((SB_BLOCK, wide), lambda g, i: (i, g)),
                  pl.BlockSpec((t, wide), lambda g, i: (0, g)),
                  pl.BlockSpec((t, wide), lambda g, i: (0, v_col0 + g)),
                  pl.BlockSpec((SB_BLOCK, wide), lambda g, i: (i, g)),
                  pl.BlockSpec((SB_BLOCK, wide), lambda g, i: (i, g))],
        out_specs=[pl.BlockSpec((SB_BLOCK, wide), lambda g, i: (i, g)),
                   pl.BlockSpec((t, wide), lambda g, i: (0, g)),
                   pl.BlockSpec((t, wide), lambda g, i: (0, g))],
        out_shape=[jax.ShapeDtypeStruct((t, d), F32), jax.ShapeDtypeStruct((t, d), F32),
                   jax.ShapeDtypeStruct((t, d), F32)],
        compiler_params=_params(dimension_semantics=("arbitrary", "arbitrary")),
    )(qb, kb, proj, ltot, do)


def _gated_norm(oa, z, gn):
    return _rms(oa, gn, RMS_EPS) * _silu(z)


def _merge_gates(ya, yb, ga, gb):
    return jax.nn.sigmoid(ga) * ya + jax.nn.sigmoid(gb) * yb


def _merge_fwd(x1, oa, proj, ob, gn, wa, wb, wo, layer, n_heads):
    t, d = x1.shape
    tm = _pick(t, (256, 128))
    square = _layer_spec(layer, (d, d), lambda i: (0, 0), single=True)

    def body(x_ref, oa_ref, z_ref, ob_ref, ga_ref, gb_ref, gn_ref, wa_ref, wb_ref, wo_ref, o_ref, na_ref):
        for h in range(n_heads):
            sl = slice(h * HEAD_DIM, (h + 1) * HEAD_DIM)
            na_ref[:, sl] = _gated_norm(oa_ref[:, sl], z_ref[:, sl], gn_ref[...]).astype(BF16)
        m = _merge_gates(_mm_nn(na_ref[...], wa_ref[...]), _mm_nn(ob_ref[...], wb_ref[...]), ga_ref[...], gb_ref[...])
        o_ref[...] = x_ref[...] + _mm_nn(m, wo_ref[...])

    tile = lambda k: pl.BlockSpec((tm, d), lambda i: (i, k))
    return pl.pallas_call(
        body, name="merge_fwd", grid=(t // tm,),
        in_specs=[tile(0), tile(0), tile(3), tile(0), tile(7), tile(8), _const_spec((1, HEAD_DIM)),
                  square, square, square],
        out_specs=tile(0),
        out_shape=jax.ShapeDtypeStruct((t, d), F32),
        scratch_shapes=[pltpu.VMEM((tm, d), BF16)],
        compiler_params=_params(dimension_semantics=("arbitrary",)),
    )(x1, oa, proj, ob, proj, proj, gn, wa, wb, wo)


def _merge_bwd(oa, proj, ob, dy, gn, wa, wb, wo, layer, n_heads):
    t, d = oa.shape
    tm = _pick(t, (256, 128))
    nt = t // tm
    square = _layer_spec(layer, (d, d), lambda i: (0, 0), single=True)

    def body(oa_ref, z_ref, ob_ref, ga_ref, gb_ref, dy_ref, gn_ref, wa_ref, wb_ref, wo_ref,
             doa_ref, dz_ref, dob_ref, dga_ref, dgb_ref, dgn_ref, dwa_hbm, dwb_hbm, dwo_hbm,
             na_ref, dna_ref, dwa_ref, dwb_ref, dwo_ref):
        i = pl.program_id(0)

        @pl.when(i == 0)
        def _():
            dgn_ref[...] = jnp.zeros((1, HEAD_DIM), F32)
            dwa_ref[...] = jnp.zeros((d, d), F32)
            dwb_ref[...] = jnp.zeros((d, d), F32)
            dwo_ref[...] = jnp.zeros((d, d), F32)

        for h in range(n_heads):
            sl = slice(h * HEAD_DIM, (h + 1) * HEAD_DIM)
            na_ref[:, sl] = _gated_norm(oa_ref[:, sl], z_ref[:, sl], gn_ref[...]).astype(BF16)
        dy = dy_ref[...].astype(BF16)
        ob = ob_ref[...].astype(BF16)
        ya = _mm_nn(na_ref[...], wa_ref[...])
        yb = _mm_nn(ob, wb_ref[...])
        m, vjp = jax.vjp(_merge_gates, ya, yb, ga_ref[...], gb_ref[...])
        dwo_ref[...] += _mm_tn(m, dy)
        dya, dyb, dga, dgb = vjp(_mm_nt(dy, wo_ref[...]))
        dga_ref[...] = dga.astype(BF16)
        dgb_ref[...] = dgb.astype(BF16)
        dwa_ref[...] += _mm_tn(na_ref[...], dya)
        dwb_ref[...] += _mm_tn(ob, dyb)
        dob_ref[...] = _mm_nt(dyb, wb_ref[...])
        dna_ref[...] = _mm_nt(dya, wa_ref[...])
        for h in range(n_heads):
            sl = slice(h * HEAD_DIM, (h + 1) * HEAD_DIM)
            _, vjp_h = jax.vjp(_gated_norm, oa_ref[:, sl], z_ref[:, sl], gn_ref[...])
            doa, dz, dgn = vjp_h(dna_ref[:, sl])
            doa_ref[:, sl] = doa
            dz_ref[:, sl] = dz.astype(BF16)
            dgn_ref[...] += dgn

        @pl.when(i == nt - 1)
        def _():
            pltpu.sync_copy(dwa_ref, dwa_hbm)
            pltpu.sync_copy(dwb_ref, dwb_hbm)
            pltpu.sync_copy(dwo_ref, dwo_hbm)

    tile = lambda k: pl.BlockSpec((tm, d), lambda i: (i, k))
    any_spec = pl.BlockSpec(memory_space=pl.ANY)
    return pl.pallas_call(
        body, name="merge_bwd", grid=(nt,),
        in_specs=[tile(0), tile(3), tile(0), tile(7), tile(8), tile(0), _const_spec((1, HEAD_DIM)),
                  square, square, square],
        out_specs=[tile(0), tile(0), tile(0), tile(0), tile(0), _const_spec((1, HEAD_DIM)),
                   any_spec, any_spec, any_spec],
        out_shape=[jax.ShapeDtypeStruct((t, d), F32), jax.ShapeDtypeStruct((t, d), BF16),
                   jax.ShapeDtypeStruct((t, d), F32), jax.ShapeDtypeStruct((t, d), BF16),
                   jax.ShapeDtypeStruct((t, d), BF16), jax.ShapeDtypeStruct((1, HEAD_DIM), F32),
                   jax.ShapeDtypeStruct((d, d), F32), jax.ShapeDtypeStruct((d, d), F32),
                   jax.ShapeDtypeStruct((d, d), F32)],
        scratch_shapes=[pltpu.VMEM((tm, d), BF16), pltpu.VMEM((tm, d), F32),
                        pltpu.VMEM((d, d), F32), pltpu.VMEM((d, d), F32), pltpu.VMEM((d, d), F32)],
        compiler_params=_params(dimension_semantics=("arbitrary",)),
    )(oa, proj, ob, proj, proj, dy, gn, wa, wb, wo)


def _loss_head(y, target):
    t, d = y.shape
    tm = _pick(t, (256, 128))

    def body(y_ref, t_ref, dy_ref, loss_ref):
        @pl.when(pl.program_id(0) == 0)
        def _():
            loss_ref[...] = jnp.zeros((8, LANES), F32)

        err = y_ref[...] - t_ref[...]
        dy_ref[...] = err * (1.0 / d)
        per_token = jnp.sum(err * err, axis=1, keepdims=True) * (1.0 / d)
        loss_ref[...] += 0.5 * jnp.sum(per_token, axis=0, keepdims=True)

    return pl.pallas_call(
        body, name="loss_head", grid=(t // tm,),
        in_specs=[pl.BlockSpec((tm, d), lambda i: (i, 0)), pl.BlockSpec((tm, d), lambda i: (i, 0))],
        out_specs=[pl.BlockSpec((tm, d), lambda i: (i, 0)), _const_spec((8, LANES))],
        out_shape=[jax.ShapeDtypeStruct((t, d), F32), jax.ShapeDtypeStruct((8, LANES), F32)],
        compiler_params=_params(dimension_semantics=("arbitrary",)),
    )(y, target)


def _adamw(w, g, m, v):
    rows, cols = w.shape
    tr = rows
    for cand in (512, 256, 128, 64, 32, 16, 8):
        if rows % cand == 0 and cand * cols * 4 <= 2 * 1024 * 1024:
            tr = cand
            break

    def body(w_ref, g_ref, m_ref, v_ref, d_ref, mo_ref, vo_ref):
        g = g_ref[...]
        m2 = ADAM_B1 * m_ref[...] + (1.0 - ADAM_B1) * g
        v2 = ADAM_B2 * v_ref[...] + (1.0 - ADAM_B2) * (g * g)
        m_hat = m2 / (1.0 - ADAM_B1 ** ADAM_STEP)
        v_hat = v2 / (1.0 - ADAM_B2 ** ADAM_STEP)
        d_ref[...] = -ADAM_LR * (m_hat / (jnp.sqrt(v_hat) + ADAM_EPS) + ADAM_WD * w_ref[...])
        mo_ref[...] = m2
        vo_ref[...] = v2

    spec = pl.BlockSpec((tr, cols), lambda i: (i, 0))
    shape = jax.ShapeDtypeStruct((rows, cols), F32)
    return pl.pallas_call(
        body, name="adamw", grid=(rows // tr,), in_specs=[spec] * 4, out_specs=[spec] * 3,
        out_shape=[shape] * 3, compiler_params=_params(dimension_semantics=("arbitrary",)),
    )(w, g, m, v)


def _place():
    return lax.axis_index("x"), lax.axis_index("y"), lax.axis_index("c")


def _other_chips(x, y):
    return [(1 - x, y), (x, 1 - y), (1 - x, 1 - y)]


def _tile_rows(rows, cols, itemsize, cap=1536 * 1024):
    best = None
    for cand in range(16, rows + 1, 16):
        if rows % cand == 0 and cand * cols * itemsize <= cap:
            best = cand
    return best if best is not None else rows


def _allgather_weights(shards):
    n = len(shards)
    half = shards[0].shape[0] // 2

    def body(*refs):
        srcs, outs = refs[:n], refs[n:2 * n]
        send_sems, recv_sems = refs[2 * n:]
        x, y, c = _place()
        me, sibling = (x, y, c), (x, y, 1 - c)
        chips = _other_chips(x, y)
        my_half = pl.ds(c * half, half)
        other_half = pl.ds((1 - c) * half, half)

        def copy(w, k, shard, part, to, from_src=False):
            return pltpu.make_async_remote_copy(
                src_ref=srcs[w].at[part] if from_src else outs[w].at[shard, part], dst_ref=outs[w].at[shard, part],
                send_sem=send_sems.at[6 * w + k], recv_sem=recv_sems.at[6 * w + k], device_id=to, device_id_type=MESH)

        first = [copy(w, j, 2 * x + y, my_half, (*chip, c), from_src=True)
                 for j, chip in enumerate(chips) for w in range(n)]
        for cp in first:
            cp.start()
        passed = []
        for j, (cx, cy) in enumerate(chips):
            for w in range(n):
                copy(w, j, 2 * cx + cy, my_half, me).wait_recv()
                cp = copy(w, 3 + j, 2 * cx + cy, my_half, sibling)
                cp.start()
                passed.append(cp)
        for j, (cx, cy) in enumerate(chips):
            for w in range(n):
                copy(w, 3 + j, 2 * cx + cy, other_half, me).wait_recv()
        for cp in first + passed:
            cp.wait_send()

    hbm = pl.BlockSpec(memory_space=pl.ANY)
    return pl.pallas_call(
        body, name="allgather_weights", in_specs=[hbm] * n, out_specs=[hbm] * n,
        out_shape=[jax.ShapeDtypeStruct((N_CHIPS,) + s.shape, s.dtype) for s in shards],
        scratch_shapes=[pltpu.SemaphoreType.DMA((6 * n,)), pltpu.SemaphoreType.DMA((6 * n,))],
    )(*shards)


def _swap_halves(grads):
    n = len(grads)
    half = grads[0].shape[1] // 2

    def body(*refs):
        gs, gots = refs[:n], refs[n:2 * n]
        send_sems, recv_sems = refs[2 * n:]
        x, y, c = _place()
        theirs = pl.ds((1 - c) * half, half)
        copies = [pltpu.make_async_remote_copy(src_ref=gs[w].at[:, theirs], dst_ref=gots[w], send_sem=send_sems.at[w],
                                               recv_sem=recv_sems.at[w], device_id=(x, y, 1 - c), device_id_type=MESH)
                  for w in range(n)]
        for cp in copies:
            cp.start()
        for cp in copies:
            cp.wait()

    hbm = pl.BlockSpec(memory_space=pl.ANY)
    return pl.pallas_call(
        body, name="swap_halves", in_specs=[hbm] * n, out_specs=[hbm] * n,
        out_shape=[jax.ShapeDtypeStruct((g.shape[0], half) + g.shape[2:], g.dtype) for g in grads],
        scratch_shapes=[pltpu.SemaphoreType.DMA((n,)), pltpu.SemaphoreType.DMA((n,))],
    )(*grads)


def _add_half(grad, got, c_idx):
    n, depth, rows, cols = grad.shape
    half = depth // 2
    tr = _tile_rows(rows, cols, 2)

    def body(c_ref, a_ref, b_ref, o_ref):
        o_ref[...] = (a_ref[...].astype(F32) + b_ref[...].astype(F32)).astype(o_ref.dtype)

    return pl.pallas_call(
        body, name="add_half",
        grid_spec=pltpu.PrefetchScalarGridSpec(
            num_scalar_prefetch=1, grid=(n, half, rows // tr),
            in_specs=[pl.BlockSpec((1, 1, tr, cols), lambda s, l, r, c_ref: (s, c_ref[0] * half + l, r, 0)),
                      pl.BlockSpec((1, 1, tr, cols), lambda s, l, r, c_ref: (s, l, r, 0))],
            out_specs=pl.BlockSpec((1, 1, tr, cols), lambda s, l, r, c_ref: (s, l, r, 0))),
        out_shape=jax.ShapeDtypeStruct((n, half, rows, cols), grad.dtype),
        compiler_params=_params(dimension_semantics=("arbitrary", "arbitrary", "arbitrary")),
    )(c_idx, grad, got)


def _scatter_partials(parts):
    n = len(parts)

    def body(*refs):
        ps, gots = refs[:n], refs[n:2 * n]
        send_sems, recv_sems = refs[2 * n:]
        x, y, c = _place()
        copies = [pltpu.make_async_remote_copy(src_ref=ps[w].at[2 * cx + cy], dst_ref=gots[w].at[j],
                                               send_sem=send_sems.at[3 * w + j], recv_sem=recv_sems.at[3 * w + j],
                                               device_id=(cx, cy, c), device_id_type=MESH)
                  for j, (cx, cy) in enumerate(_other_chips(x, y)) for w in range(n)]
        for cp in copies:
            cp.start()
        for cp in copies:
            cp.wait()

    hbm = pl.BlockSpec(memory_space=pl.ANY)
    return pl.pallas_call(
        body, name="scatter_partials", in_specs=[hbm] * n, out_specs=[hbm] * n,
        out_shape=[jax.ShapeDtypeStruct((N_CHIPS - 1,) + p.shape[1:], p.dtype) for p in parts],
        scratch_shapes=[pltpu.SemaphoreType.DMA((3 * n,)), pltpu.SemaphoreType.DMA((3 * n,))],
    )(*parts)


def _sum_partials(part, got, s_idx, c_idx):
    n, half, rows, cols = part.shape
    tr = _tile_rows(rows, cols, 2, cap=1024 * 1024)

    def body(s_ref, c_ref, a_ref, b_ref, o_ref):
        acc = a_ref[0, 0].astype(F32)
        for j in range(n - 1):
            acc = acc + b_ref[j, 0].astype(F32)
        o_ref[0] = acc

    return pl.pallas_call(
        body, name="sum_partials",
        grid_spec=pltpu.PrefetchScalarGridSpec(
            num_scalar_prefetch=2, grid=(half, rows // tr),
            in_specs=[pl.BlockSpec((1, 1, tr, cols), lambda l, r, s_ref, c_ref: (s_ref[0], l, r, 0)),
                      pl.BlockSpec((n - 1, 1, tr, cols), lambda l, r, s_ref, c_ref: (0, l, r, 0))],
            out_specs=pl.BlockSpec((1, tr, cols), lambda l, r, s_ref, c_ref: (c_ref[0] * half + l, r, 0))),
        out_shape=jax.ShapeDtypeStruct((2 * half, rows, cols), F32),
        compiler_params=_params(dimension_semantics=("arbitrary", "arbitrary")),
    )(s_idx, c_idx, part, got)


def _join_halves(bufs):
    n = len(bufs)
    half = bufs[0].shape[0] // 2

    def body(*refs):
        outs = refs[n:2 * n]
        send_sems, recv_sems = refs[2 * n:]
        x, y, c = _place()
        my_half = pl.ds(c * half, half)
        copies = [pltpu.make_async_remote_copy(src_ref=outs[w].at[my_half], dst_ref=outs[w].at[my_half],
                                               send_sem=send_sems.at[w], recv_sem=recv_sems.at[w],
                                               device_id=(x, y, 1 - c), device_id_type=MESH)
                  for w in range(n)]
        for cp in copies:
            cp.start()
        for cp in copies:
            cp.wait()

    hbm = pl.BlockSpec(memory_space=pl.ANY)
    return pl.pallas_call(
        body, name="join_halves", in_specs=[hbm] * n, out_specs=[hbm] * n,
        out_shape=[jax.ShapeDtypeStruct(b.shape, b.dtype) for b in bufs],
        input_output_aliases={w: w for w in range(n)},
        scratch_shapes=[pltpu.SemaphoreType.DMA((n,)), pltpu.SemaphoreType.DMA((n,))],
    )(*bufs)


def _allreduce_small(v, name):
    rows = v.shape[0]

    def body(v_ref, o_ref, gath, send_sems, recv_sems):
        x, y, c = _place()
        idx = 4 * x + 2 * y + c
        gath[0] = v_ref[...]
        copies = []
        for r in range(1, N_DEV):
            peer = (1 - x if r & 4 else x, 1 - y if r & 2 else y, 1 - c if r & 1 else c)
            cp = pltpu.make_async_remote_copy(src_ref=v_ref, dst_ref=gath.at[r], send_sem=send_sems.at[r - 1],
                                              recv_sem=recv_sems.at[r - 1], device_id=peer, device_id_type=MESH)
            cp.start()
            copies.append(cp)
        for cp in copies:
            cp.wait()
        acc = gath[idx]
        for a in range(1, N_DEV):
            acc = acc + gath[lax.bitwise_xor(idx, a)]
        o_ref[...] = acc

    vmem = pl.BlockSpec(memory_space=pltpu.VMEM)
    return pl.pallas_call(
        body, name=name, in_specs=[vmem], out_specs=vmem,
        out_shape=jax.ShapeDtypeStruct((rows, LANES), F32),
        scratch_shapes=[pltpu.VMEM((N_DEV, rows, LANES), F32), pltpu.SemaphoreType.DMA((N_DEV - 1,)),
                        pltpu.SemaphoreType.DMA((N_DEV - 1,))],
    )(v)


def _join_shards(name, gathered):
    return jnp.concatenate([gathered[s] for s in range(N_CHIPS)], axis=2 if name in COL_SHARDED else 1)


def _split_shards(name, per_layer):
    def shard(g, s):
        if name in COL_SHARDED:
            width = g.shape[1] // N_CHIPS
            return g[:, s * width:(s + 1) * width]
        height = g.shape[0] // N_CHIPS
        return g[s * height:(s + 1) * height, :]

    return jnp.stack([jnp.stack([shard(g, s).astype(BF16) for g in per_layer]) for s in range(N_CHIPS)])


def _pad_small(flat):
    n = flat.shape[0]
    block = 8 * LANES
    padded = -(-n // block) * block
    return jnp.pad(flat, (0, padded - n)).reshape(padded // LANES, LANES)


def kernel(x, ffn1_norm, ffn1_w_in, ffn1_w_out, mix_norm, w_in, dn_conv_w, dn_a_log, dn_dt_bias, dn_out_norm, sb_q_norm, sb_k_norm, w_branch_a, w_branch_b, w_out, ffn2_norm, ffn2_w_in, ffn2_w_out, loss_target, m_ffn1_norm, m_ffn1_w_in, m_ffn1_w_out, m_mix_norm, m_w_in, m_dn_conv_w, m_dn_a_log, m_dn_dt_bias, m_dn_out_norm, m_sb_q_norm, m_sb_k_norm, m_w_branch_a, m_w_branch_b, m_w_out, m_ffn2_norm, m_ffn2_w_in, m_ffn2_w_out, v_ffn1_norm, v_ffn1_w_in, v_ffn1_w_out, v_mix_norm, v_w_in, v_dn_conv_w, v_dn_a_log, v_dn_dt_bias, v_dn_out_norm, v_sb_q_norm, v_sb_k_norm, v_w_branch_a, v_w_branch_b, v_w_out, v_ffn2_norm, v_ffn2_w_in, v_ffn2_w_out):
    w = dict(ffn1_norm=ffn1_norm, ffn1_w_in=ffn1_w_in, ffn1_w_out=ffn1_w_out, mix_norm=mix_norm, w_in=w_in,
             dn_conv_w=dn_conv_w, dn_a_log=dn_a_log, dn_dt_bias=dn_dt_bias, dn_out_norm=dn_out_norm,
             sb_q_norm=sb_q_norm, sb_k_norm=sb_k_norm, w_branch_a=w_branch_a, w_branch_b=w_branch_b, w_out=w_out,
             ffn2_norm=ffn2_norm, ffn2_w_in=ffn2_w_in, ffn2_w_out=ffn2_w_out)
    mom = dict(ffn1_norm=m_ffn1_norm, ffn1_w_in=m_ffn1_w_in, ffn1_w_out=m_ffn1_w_out, mix_norm=m_mix_norm, w_in=m_w_in,
               dn_conv_w=m_dn_conv_w, dn_a_log=m_dn_a_log, dn_dt_bias=m_dn_dt_bias, dn_out_norm=m_dn_out_norm,
               sb_q_norm=m_sb_q_norm, sb_k_norm=m_sb_k_norm, w_branch_a=m_w_branch_a, w_branch_b=m_w_branch_b,
               w_out=m_w_out, ffn2_norm=m_ffn2_norm, ffn2_w_in=m_ffn2_w_in, ffn2_w_out=m_ffn2_w_out)
    var = dict(ffn1_norm=v_ffn1_norm, ffn1_w_in=v_ffn1_w_in, ffn1_w_out=v_ffn1_w_out, mix_norm=v_mix_norm, w_in=v_w_in,
               dn_conv_w=v_dn_conv_w, dn_a_log=v_dn_a_log, dn_dt_bias=v_dn_dt_bias, dn_out_norm=v_dn_out_norm,
               sb_q_norm=v_sb_q_norm, sb_k_norm=v_sb_k_norm, w_branch_a=v_w_branch_a, w_branch_b=v_w_branch_b,
               w_out=v_w_out, ffn2_norm=v_ffn2_norm, ffn2_w_in=v_ffn2_w_in, ffn2_w_out=v_ffn2_w_out)

    _, t, d = x.shape
    depth = ffn1_norm.shape[0]
    n_heads = d // HEAD_DIM
    conv_cols = dn_conv_w.shape[2]
    assert d % HEAD_DIM == 0 and t % SB_BLOCK == 0 and 2 * n_heads <= LANES and depth % 2 == 0
    assert w_in.shape[2] * N_CHIPS == 9 * d + 2 * n_heads and conv_cols * N_CHIPS == 3 * d

    x_idx, y_idx, c_idx = _place()
    shard = 2 * x_idx + y_idx
    c_arr = jnp.reshape(c_idx, (1,)).astype(jnp.int32)
    s_arr = jnp.reshape(shard, (1,)).astype(jnp.int32)

    mine = [w[n].astype(BF16) for n in BIG]
    gathered = [lax.dynamic_update_slice(g, m[None], (shard, 0, 0, 0)) for g, m in zip(_allgather_weights(mine), mine)]
    full = {n: _join_shards(n, g) for n, g in zip(BIG, gathered)}
    cut = 4 * d
    w_main = jnp.concatenate([full["w_in"][..., :cut], full["w_in"][..., cut + 2 * n_heads:]], axis=-1)
    w_ba = jnp.pad(full["w_in"][..., cut:cut + 2 * n_heads], ((0, 0), (0, 0), (0, LANES - 2 * n_heads)))

    conv_place = lax.dynamic_update_slice(jnp.zeros((depth, DN_CONV, 3 * d), F32), dn_conv_w, (0, 0, shard * conv_cols))
    conv_rows = _pad_small(conv_place.reshape(-1))
    conv_full = (0.5 * _allreduce_small(conv_rows, "allgather_conv")).reshape(-1)[:depth * DN_CONV * 3 * d]
    conv_full = jnp.pad(conv_full.reshape(depth, DN_CONV, 3 * d), ((0, 0), (0, CONV_ROWS - DN_CONV), (0, 0)))

    def head_row(vals):
        return jnp.pad(vals, (n_heads, LANES - 2 * n_heads)).reshape(1, LANES)

    saved = []
    cur = x[0]
    for l in range(depth):
        x0 = cur
        x1 = _ffn_fwd(x0, ffn1_norm[l][None], full["ffn1_w_in"], full["ffn1_w_out"], l)
        proj, ba = _proj_fwd(x1, mix_norm[l][None], w_main, w_ba, l)
        act = _dn_prep_fwd(proj, conv_full[l], n_heads)
        alog, dtb = head_row(dn_a_log[l]), head_row(dn_dt_bias[l])
        oa, snaps = _delta_fwd(act, ba, alog, dtb, n_heads)
        qb, kb = _sb_prep_fwd(proj, sb_q_norm[l][None], sb_k_norm[l][None], n_heads)
        ob, ltot = _sb_attn_fwd(qb, kb, proj, n_heads)
        x2 = _merge_fwd(x1, oa, proj, ob, dn_out_norm[l][None], full["w_branch_a"], full["w_branch_b"],
                        full["w_out"], l, n_heads)
        cur = _ffn_fwd(x2, ffn2_norm[l][None], full["ffn2_w_in"], full["ffn2_w_out"], l)
        saved.append((x0, x1, proj, ba, act, alog, dtb, oa, snaps, qb, kb, ob, ltot, x2))

    dcur, loss_part = _loss_head(cur, loss_target[0])

    grads = {n: [None] * depth for n in WEIGHTS}
    for l in reversed(range(depth)):
        x0, x1, proj, ba, act, alog, dtb, oa, snaps, qb, kb, ob, ltot, x2 = saved[l]
        dx2, dg, dwg, dwu, dwo = _ffn_bwd(x2, ffn2_norm[l][None], dcur, full["ffn2_w_in"], full["ffn2_w_out"], l)
        grads["ffn2_norm"][l] = dg[0]
        grads["ffn2_w_in"][l] = jnp.concatenate([dwg, dwu], axis=1)
        grads["ffn2_w_out"][l] = dwo
        doa, dz, dob, dga, dgb, dgn, dwa, dwb, dwout = _merge_bwd(
            oa, proj, ob, dx2, dn_out_norm[l][None], full["w_branch_a"], full["w_branch_b"], full["w_out"], l,
            n_heads)
        grads["dn_out_norm"][l] = dgn[0]
        grads["w_branch_a"][l], grads["w_branch_b"][l], grads["w_out"][l] = dwa, dwb, dwout
        dqb, dkb, dvb = _sb_attn_bwd(qb, kb, proj, ltot, dob, n_heads)
        dsq, dsk, dqn, dkn = _sb_prep_bwd(proj, sb_q_norm[l][None], sb_k_norm[l][None], dqb, dkb, n_heads)
        grads["sb_q_norm"][l], grads["sb_k_norm"][l] = dqn[0], dkn[0]
        dact, dba, dal, ddt = _delta_bwd(act, ba, alog, dtb, snaps, doa, n_heads)
        grads["dn_a_log"][l] = dal[0, n_heads:2 * n_heads]
        grads["dn_dt_bias"][l] = ddt[0, n_heads:2 * n_heads]
        dqkv, dconv = _dn_prep_bwd(proj, conv_full[l], dact, n_heads)
        grads["dn_conv_w"][l] = dconv[:DN_CONV]
        dproj = jnp.concatenate([dqkv, dz, dsq, dsk, dvb.astype(BF16), dga, dgb], axis=1)
        dx1, dg, dwm, dwba = _proj_bwd(x1, mix_norm[l][None], dx2, dproj, dba, w_main, w_ba, l)
        grads["mix_norm"][l] = dg[0]
        grads["w_in"][l] = jnp.concatenate([dwm[:, :cut], dwba[:, :2 * n_heads], dwm[:, cut:]], axis=1)
        dcur, dg, dwg, dwu, dwo = _ffn_bwd(x0, ffn1_norm[l][None], dx1, full["ffn1_w_in"], full["ffn1_w_out"], l)
        grads["ffn1_norm"][l] = dg[0]
        grads["ffn1_w_in"][l] = jnp.concatenate([dwg, dwu], axis=1)
        grads["ffn1_w_out"][l] = dwo

    g_major = [_split_shards(n, grads[n]) for n in BIG]
    parts = [_add_half(g, got, c_arr) for g, got in zip(g_major, _swap_halves(g_major))]
    halves = [_sum_partials(p, got, s_arr, c_arr) for p, got in zip(parts, _scatter_partials(parts))]
    final = dict(zip(BIG, _join_halves(halves)))
    grads = {n: jnp.stack(grads[n]) for n in SMALL + ("dn_conv_w",)}

    small_names = SMALL + ("dn_conv_w",)
    small_sizes = [int(np.prod(grads[n].shape)) for n in small_names]
    small_off = np.concatenate([[0], np.cumsum(small_sizes)])
    small = jnp.concatenate([grads[n].reshape(-1) for n in small_names] + [loss_part[0, :1]])
    small_sum = _allreduce_small(_pad_small(small), "allreduce_small").reshape(-1)
    for i, n in enumerate(small_names):
        final[n] = small_sum[small_off[i]:small_off[i + 1]].reshape(grads[n].shape)
    final["dn_conv_w"] = lax.dynamic_slice(final["dn_conv_w"], (0, 0, shard * conv_cols), (depth, DN_CONV, conv_cols))
    loss = small_sum[small_off[-1]]

    deltas, new_m, new_v = {}, {}, {}
    for n in WEIGHTS:
        shape = w[n].shape
        flat = (-1, shape[-1])
        dl, m2, v2 = _adamw(w[n].reshape(flat), final[n].reshape(flat), mom[n].reshape(flat), var[n].reshape(flat))
        deltas[n], new_m[n], new_v[n] = dl.reshape(shape), m2.reshape(shape), v2.reshape(shape)

    grad_x = dcur[None]
    return (loss, grad_x, *[final[n] for n in WEIGHTS], *[deltas[n] for n in WEIGHTS],
            *[new_m[n] for n in WEIGHTS], *[new_v[n] for n in WEIGHTS])
```

```python
import functools

import jax
import jax.numpy as jnp
import numpy as np
from jax import lax
from jax.experimental import pallas as pl
from jax.experimental.pallas import tpu as pltpu
from jax.experimental.pallas import tpu_sc as plsc

F32 = jnp.float32
BF16 = jnp.bfloat16

LANES = 128
HEAD_DIM = 128
DN_CHUNK = 64
DN_CONV = 4
CONV_ROWS = 8
SB_BLOCK = 128
FFN_HALF = 0.5
RMS_EPS = 1e-6
L2_EPS = 1e-6
NEG_BIG = -1e30
ADAM_LR = 0.001
ADAM_B1 = 0.9
ADAM_B2 = 0.999
ADAM_EPS = 1e-08
ADAM_WD = 0.01
ADAM_STEP = 10
VMEM_LIMIT = 56 * 1024 * 1024
N_CHIPS = 4
N_DEV = 8
MESH = pl.DeviceIdType.MESH

BIG = ("ffn1_w_in", "ffn1_w_out", "w_in", "w_branch_a", "w_branch_b", "w_out", "ffn2_w_in", "ffn2_w_out")
COL_SHARDED = ("ffn1_w_in", "w_in", "ffn2_w_in")
SMALL = ("ffn1_norm", "mix_norm", "dn_a_log", "dn_dt_bias", "dn_out_norm", "sb_q_norm", "sb_k_norm", "ffn2_norm")
WEIGHTS = ("ffn1_norm", "ffn1_w_in", "ffn1_w_out", "mix_norm", "w_in", "dn_conv_w", "dn_a_log", "dn_dt_bias",
           "dn_out_norm", "sb_q_norm", "sb_k_norm", "w_branch_a", "w_branch_b", "w_out", "ffn2_norm", "ffn2_w_in",
           "ffn2_w_out")


def _params(**kw):
    return pltpu.CompilerParams(vmem_limit_bytes=VMEM_LIMIT, **kw)


def _pick(n, options):
    for o in options:
        if n % o == 0:
            return o
    return n


def _const_spec(shape, single=False):
    nd = len(shape)
    if single:
        return pl.BlockSpec(shape, lambda *_: (0,) * nd, pipeline_mode=pl.Buffered(1))
    return pl.BlockSpec(shape, lambda *_: (0,) * nd)


_NN = ((1,), (0,))
_NT = ((1,), (1,))
_TN = ((0,), (0,))


def _dot(a, b, dims):
    return lax.dot_general(a.astype(BF16), b.astype(BF16), (dims, ((), ())), preferred_element_type=F32)


def _mm_nn(a, b):
    return _dot(a, b, _NN)


def _mm_nt(a, b):
    return _dot(a, b, _NT)


def _mm_tn(a, b):
    return _dot(a, b, _TN)


def _split(a):
    hi = a.astype(BF16)
    lo = (a - hi.astype(F32)).astype(BF16)
    return hi, lo


def _dot_precise(a, b, dims):
    dn = (dims, ((), ()))
    ah, al = _split(a)
    bh, bl = _split(b)
    out = lax.dot_general(ah, bh, dn, preferred_element_type=F32)
    out = out + lax.dot_general(ah, bl, dn, preferred_element_type=F32)
    return out + lax.dot_general(al, bh, dn, preferred_element_type=F32)


def _make_diff_mm(dot):
    @jax.custom_vjp
    def nn(a, b):
        return dot(a, b, _NN)

    @jax.custom_vjp
    def nt(a, b):
        return dot(a, b, _NT)

    @jax.custom_vjp
    def tn(a, b):
        return dot(a, b, _TN)

    nn.defvjp(lambda a, b: (dot(a, b, _NN), (a, b)), lambda r, g: (nt(g, r[1]), tn(r[0], g)))
    nt.defvjp(lambda a, b: (dot(a, b, _NT), (a, b)), lambda r, g: (nn(g, r[1]), tn(g, r[0])))
    tn.defvjp(lambda a, b: (dot(a, b, _TN), (a, b)), lambda r, g: (nt(r[1], g), nn(r[0], g)))
    return nn, nt, tn


_d_nn, _d_nt, _d_tn = _make_diff_mm(_dot)
_p_nn, _p_nt, _p_tn = _make_diff_mm(_dot_precise)


def _softplus_raw(x):
    return jnp.maximum(x, 0.0) + jnp.log(1.0 + jnp.exp(-jnp.abs(x)))


@jax.custom_vjp
def _softplus(x):
    return _softplus_raw(x)


_softplus.defvjp(lambda x: (_softplus_raw(x), x), lambda x, g: (g * jax.nn.sigmoid(x),))


def _rms(x, gain, eps):
    return x * lax.rsqrt(jnp.mean(x * x, axis=-1, keepdims=True) + eps) * gain


def _silu(x):
    return x * jax.nn.sigmoid(x)


def _shift_rows_raw(x, k, down):
    n = x.shape[0]
    row = lax.broadcasted_iota(jnp.int32, x.shape, 0)
    if down:
        return jnp.where(row >= k, pltpu.roll(x, k, 0), 0.0)
    return jnp.where(row < n - k, pltpu.roll(x, n - k, 0), 0.0)


@functools.partial(jax.custom_vjp, nondiff_argnums=(1,))
def _shift_down(x, k):
    return _shift_rows_raw(x, k, True)


_shift_down.defvjp(lambda x, k: (_shift_rows_raw(x, k, True), None),
                   lambda k, _, g: (_shift_rows_raw(g, k, False),))


def _layer_spec(layer, block, index_map, single=False):
    full_map = lambda *a: (layer,) + tuple(index_map(*a))
    if single:
        return pl.BlockSpec((None,) + block, full_map, pipeline_mode=pl.Buffered(1))
    return pl.BlockSpec((None,) + block, full_map)


def _ffn_fwd(x, gain, w_in, w_out, layer):
    t, d = x.shape
    f = w_out.shape[1]
    fc = _pick(f, (256, 128))
    nj = f // fc
    rt = _pick(t, (512, 256, 128))

    def body(x_ref, g_ref, wg_ref, wu_ref, wo_ref, o_ref, hs_ref):
        @pl.when(pl.program_id(0) == 0)
        def _():
            for r in range(t // rt):
                rows = pl.ds(r * rt, rt)
                xr = x_ref[rows, :]
                hs_ref[rows, :] = _rms(xr, g_ref[...], RMS_EPS).astype(BF16)
                o_ref[rows, :] = xr

        for r in range(t // rt):
            rows = pl.ds(r * rt, rt)
            h = hs_ref[rows, :]
            a = _mm_nn(h, wg_ref[...])
            b = _mm_nn(h, wu_ref[...])
            o_ref[rows, :] += FFN_HALF * _mm_nn(_silu(a) * b, wo_ref[...])

    return pl.pallas_call(
        body, name="ffn_fwd", grid=(nj,),
        in_specs=[_const_spec((t, d), True), _const_spec((1, d)),
                  _layer_spec(layer, (d, fc), lambda j: (0, j)), _layer_spec(layer, (d, fc), lambda j: (0, nj + j)),
                  _layer_spec(layer, (fc, d), lambda j: (j, 0))],
        out_specs=_const_spec((t, d)),
        out_shape=jax.ShapeDtypeStruct((t, d), F32),
        scratch_shapes=[pltpu.VMEM((t, d), BF16)],
        compiler_params=_params(dimension_semantics=("arbitrary",)),
    )(x, gain, w_in, w_in, w_out)


def _ffn_bwd(x, gain, dy, w_in, w_out, layer):
    t, d = x.shape
    f = w_out.shape[1]
    fc = _pick(f, (256, 128))
    nj = f // fc
    rt = _pick(t, (512, 256, 128))
    nr = t // rt

    def body(x_ref, g_ref, dy_ref, wg_ref, wu_ref, wo_ref, dx_ref, dg_ref, dwg_ref, dwu_ref, dwo_ref, hs_ref):
        j = pl.program_id(0)

        @pl.when(j == 0)
        def _():
            for r in range(nr):
                rows = pl.ds(r * rt, rt)
                hs_ref[rows, :] = _rms(x_ref[rows, :], g_ref[...], RMS_EPS).astype(BF16)
                dx_ref[rows, :] = jnp.zeros((rt, d), F32)

        for r in range(nr):
            rows = pl.ds(r * rt, rt)
            h = hs_ref[rows, :]
            dy2 = (FFN_HALF * dy_ref[rows, :]).astype(BF16)
            a = _mm_nn(h, wg_ref[...])
            b = _mm_nn(h, wu_ref[...])
            sig = jax.nn.sigmoid(a)
            sa = a * sig
            ds = _mm_nt(dy2, wo_ref[...])
            da = ds * b * (sig * (1.0 + a * (1.0 - sig)))
            db = ds * sa
            dx_ref[rows, :] += _mm_nt(da, wg_ref[...]) + _mm_nt(db, wu_ref[...])
            dwo_c = _mm_tn(sa * b, dy2)
            dwg_c = _mm_tn(h, da)
            dwu_c = _mm_tn(h, db)
            if r == 0:
                dwo_ref[...] = dwo_c
                dwg_ref[...] = dwg_c
                dwu_ref[...] = dwu_c
            else:
                dwo_ref[...] += dwo_c
                dwg_ref[...] += dwg_c
                dwu_ref[...] += dwu_c

        @pl.when(j == nj - 1)
        def _():
            for r in range(nr):
                rows = pl.ds(r * rt, rt)
                _, vjp = jax.vjp(lambda xx, gg: _rms(xx, gg, RMS_EPS), x_ref[rows, :], g_ref[...])
                dxn, dgr = vjp(dx_ref[rows, :])
                dx_ref[rows, :] = dy_ref[rows, :] + dxn
                if r == 0:
                    dg_ref[...] = dgr
                else:
                    dg_ref[...] += dgr

    return pl.pallas_call(
        body, name="ffn_bwd", grid=(nj,),
        in_specs=[_const_spec((t, d), True), _const_spec((1, d)), _const_spec((t, d), True),
                  _layer_spec(layer, (d, fc), lambda j: (0, j)), _layer_spec(layer, (d, fc), lambda j: (0, nj + j)),
                  _layer_spec(layer, (fc, d), lambda j: (j, 0))],
        out_specs=[_const_spec((t, d)), _const_spec((1, d)),
                   pl.BlockSpec((d, fc), lambda j: (0, j)), pl.BlockSpec((d, fc), lambda j: (0, j)),
                   pl.BlockSpec((fc, d), lambda j: (j, 0))],
        out_shape=[jax.ShapeDtypeStruct((t, d), F32), jax.ShapeDtypeStruct((1, d), F32),
                   jax.ShapeDtypeStruct((d, f), F32), jax.ShapeDtypeStruct((d, f), F32),
                   jax.ShapeDtypeStruct((f, d), F32)],
        scratch_shapes=[pltpu.VMEM((t, d), BF16)],
        compiler_params=_params(dimension_semantics=("arbitrary",)),
    )(x, gain, dy, w_in, w_in, w_out)


def _proj_fwd(x, gain, w, wba, layer):
    t, d = x.shape
    n = w.shape[2]
    nc = _pick(n, (512, 256, 128))
    rt = _pick(t, (512, 256, 128))

    def body(x_ref, g_ref, w_ref, wba_ref, p_ref, ba_ref, hs_ref):
        @pl.when(pl.program_id(0) == 0)
        def _():
            for r in range(t // rt):
                rows = pl.ds(r * rt, rt)
                h = _rms(x_ref[rows, :], g_ref[...], RMS_EPS).astype(BF16)
                hs_ref[rows, :] = h
                ba_ref[rows, :] = _mm_nn(h, wba_ref[...])

        for r in range(t // rt):
            rows = pl.ds(r * rt, rt)
            p_ref[rows, :] = _mm_nn(hs_ref[rows, :], w_ref[...])

    return pl.pallas_call(
        body, name="proj_fwd", grid=(n // nc,),
        in_specs=[_const_spec((t, d), True), _const_spec((1, d)),
                  _layer_spec(layer, (d, nc), lambda j: (0, j)), _layer_spec(layer, (d, LANES), lambda j: (0, 0))],
        out_specs=[pl.BlockSpec((t, nc), lambda j: (0, j)), _const_spec((t, LANES))],
        out_shape=[jax.ShapeDtypeStruct((t, n), F32), jax.ShapeDtypeStruct((t, LANES), F32)],
        scratch_shapes=[pltpu.VMEM((t, d), BF16)],
        compiler_params=_params(dimension_semantics=("arbitrary",)),
    )(x, gain, w, wba)


def _proj_bwd(x, gain, dres, dp, dba, w, wba, layer):
    t, d = x.shape
    n = w.shape[2]
    nc = _pick(n, (512, 256, 128))
    nj = n // nc
    rt = _pick(t, (512, 256, 128))
    nr = t // rt

    def body(x_ref, g_ref, dres_ref, dp_ref, dba_ref, w_ref, wba_ref, dx_ref, dg_ref, dw_ref, dwba_ref, hs_ref):
        j = pl.program_id(0)

        @pl.when(j == 0)
        def _():
            for r in range(nr):
                rows = pl.ds(r * rt, rt)
                h = _rms(x_ref[rows, :], g_ref[...], RMS_EPS).astype(BF16)
                hs_ref[rows, :] = h
                g = dba_ref[rows, :]
                dx_ref[rows, :] = _mm_nt(g, wba_ref[...])
                if r == 0:
                    dwba_ref[...] = _mm_tn(h, g)
                else:
                    dwba_ref[...] += _mm_tn(h, g)

        for r in range(nr):
            rows = pl.ds(r * rt, rt)
            g = dp_ref[rows, :]
            dx_ref[rows, :] += _mm_nt(g, w_ref[...])
            if r == 0:
                dw_ref[...] = _mm_tn(hs_ref[rows, :], g)
            else:
                dw_ref[...] += _mm_tn(hs_ref[rows, :], g)

        @pl.when(j == nj - 1)
        def _():
            for r in range(nr):
                rows = pl.ds(r * rt, rt)
                _, vjp = jax.vjp(lambda xx, gg: _rms(xx, gg, RMS_EPS), x_ref[rows, :], g_ref[...])
                dxn, dgr = vjp(dx_ref[rows, :])
                dx_ref[rows, :] = dres_ref[rows, :] + dxn
                if r == 0:
                    dg_ref[...] = dgr
                else:
                    dg_ref[...] += dgr

    return pl.pallas_call(
        body, name="proj_bwd", grid=(nj,),
        in_specs=[_const_spec((t, d), True), _const_spec((1, d)), _const_spec((t, d), True),
                  pl.BlockSpec((t, nc), lambda j: (0, j)), _const_spec((t, LANES)),
                  _layer_spec(layer, (d, nc), lambda j: (0, j)), _layer_spec(layer, (d, LANES), lambda j: (0, 0))],
        out_specs=[_const_spec((t, d)), _const_spec((1, d)),
                   pl.BlockSpec((d, nc), lambda j: (0, j)), _const_spec((d, LANES))],
        out_shape=[jax.ShapeDtypeStruct((t, d), F32), jax.ShapeDtypeStruct((1, d), F32),
                   jax.ShapeDtypeStruct((d, n), F32), jax.ShapeDtypeStruct((d, LANES), F32)],
        scratch_shapes=[pltpu.VMEM((t, d), BF16)],
        compiler_params=_params(dimension_semantics=("arbitrary",)),
    )(x, gain, dres, dp, dba, w, wba)


def _conv_act(x, w0, w1, w2, w3, is_qk):
    y = w3 * x + w2 * _shift_down(x, 1) + w1 * _shift_down(x, 2) + w0 * _shift_down(x, 3)
    y = _silu(y)
    inv = lax.rsqrt(jnp.sum(y * y, axis=-1, keepdims=True) + L2_EPS)
    return y * (is_qk * inv + (1.0 - is_qk))


def _taps(w_ref):
    return tuple(w_ref[i:i + 1, :] for i in range(DN_CONV))


def _dn_prep_fwd(proj, conv_w, n_heads):
    t = proj.shape[0]
    nb = 3 * n_heads

    def body(x_ref, w_ref, o_ref):
        is_qk = jnp.where(pl.program_id(0) < 2 * n_heads, 1.0, 0.0).astype(F32)
        o_ref[...] = _conv_act(x_ref[...], *_taps(w_ref), is_qk)

    return pl.pallas_call(
        body, name="dn_prep_fwd", grid=(nb,),
        in_specs=[pl.BlockSpec((t, HEAD_DIM), lambda i: (0, i)), pl.BlockSpec((CONV_ROWS, HEAD_DIM), lambda i: (0, i))],
        out_specs=pl.BlockSpec((t, HEAD_DIM), lambda i: (0, i)),
        out_shape=jax.ShapeDtypeStruct((t, nb * HEAD_DIM), F32),
        compiler_params=_params(dimension_semantics=("arbitrary",)),
    )(proj, conv_w)


def _dn_prep_bwd(proj, conv_w, dact, n_heads):
    t = proj.shape[0]
    nb = 3 * n_heads

    def body(x_ref, w_ref, g_ref, dx_ref, dw_ref):
        is_qk = jnp.where(pl.program_id(0) < 2 * n_heads, 1.0, 0.0).astype(F32)
        _, vjp = jax.vjp(lambda x, a, b, c, e: _conv_act(x, a, b, c, e, is_qk), x_ref[...], *_taps(w_ref))
        dx, d0, d1, d2, d3 = vjp(g_ref[...])
        dx_ref[...] = dx.astype(BF16)
        dw_ref[...] = jnp.concatenate([d0, d1, d2, d3, jnp.zeros((CONV_ROWS - DN_CONV, HEAD_DIM), F32)], axis=0)

    return pl.pallas_call(
        body, name="dn_prep_bwd", grid=(nb,),
        in_specs=[pl.BlockSpec((t, HEAD_DIM), lambda i: (0, i)), pl.BlockSpec((CONV_ROWS, HEAD_DIM), lambda i: (0, i)),
                  pl.BlockSpec((t, HEAD_DIM), lambda i: (0, i))],
        out_specs=[pl.BlockSpec((t, HEAD_DIM), lambda i: (0, i)), pl.BlockSpec((CONV_ROWS, HEAD_DIM), lambda i: (0, i))],
        out_shape=[jax.ShapeDtypeStruct((t, nb * HEAD_DIM), BF16), jax.ShapeDtypeStruct((CONV_ROWS, nb * HEAD_DIM), F32)],
        compiler_params=_params(dimension_semantics=("arbitrary",)),
    )(proj, conv_w, dact)


def _unit_lower_inverses(lmats, c):
    r = lax.broadcasted_iota(jnp.int32, (c, c), 0)
    q = lax.broadcasted_iota(jnp.int32, (c, c), 1)
    eye = jnp.where(r == q, 1.0, 0.0)
    ps = [eye - l for l in lmats]
    ms = [_p_nn(l, l) for l in lmats]
    n = 2
    while True:
        ps = [p + _p_nn(p, m) for p, m in zip(ps, ms)]
        if 2 * n >= c:
            return ps
        ms = [_p_nn(m, m) for m in ms]
        n *= 2


def _delta_heads(qs, ks, vs, bg, alog, dtb, states):
    n_heads = len(qs)
    heads = range(n_heads)
    c = qs[0].shape[0]
    lane = lax.broadcasted_iota(jnp.int32, (c, LANES), 1)
    r = lax.broadcasted_iota(jnp.int32, (c, c), 0)
    s = lax.broadcasted_iota(jnp.int32, (c, c), 1)
    beta_all = jax.nn.sigmoid(bg)
    g_all = -jnp.exp(alog) * _softplus(bg + dtb)
    beta = [jnp.sum(jnp.where(lane == h, beta_all, 0.0), axis=1, keepdims=True) for h in heads]
    g = [jnp.sum(jnp.where(lane == n_heads + h, g_all, 0.0), axis=1, keepdims=True) for h in heads]
    g_row = [jnp.sum(jnp.where(r == s, g[h], 0.0), axis=0, keepdims=True) for h in heads]
    gc = [jnp.sum(jnp.where(s <= r, g_row[h], 0.0), axis=1, keepdims=True) for h in heads]
    gr = [jnp.sum(jnp.where(r <= s, g[h], 0.0), axis=0, keepdims=True) for h in heads]
    g_last = [jnp.sum(g[h], axis=0, keepdims=True) for h in heads]
    decay = [jnp.exp(jnp.where(r >= s, gc[h] - gr[h], NEG_BIG)) for h in heads]
    q_scaled = [qs[h] * (HEAD_DIM ** -0.5) for h in heads]
    k_beta = [ks[h] * beta[h] for h in heads]
    lmat = [jnp.where(r > s, _d_nt(k_beta[h], ks[h]) * decay[h], 0.0) for h in heads]
    attn = [_d_nt(q_scaled[h], ks[h]) * decay[h] for h in heads]
    tinv = _unit_lower_inverses(lmat, c)
    u = [_p_nn(tinv[h], vs[h] * beta[h]) for h in heads]
    w = [_p_nn(tinv[h], k_beta[h] * jnp.exp(gc[h])) for h in heads]
    v_new = [u[h] - _d_nn(w[h], states[h]) for h in heads]
    o_state = [_d_nn(q_scaled[h] * jnp.exp(gc[h]), states[h]) for h in heads]
    o = [o_state[h] + _d_nn(attn[h], v_new[h]) for h in heads]
    kv = [_d_tn(ks[h] * jnp.exp(g_last[h] - gc[h]), v_new[h]) for h in heads]
    new_states = [states[h] * jnp.exp(g_last[h]) + kv[h] for h in heads]
    return tuple(o), tuple(new_states)


def _delta_fwd(act, ba, alog, dtb, n_heads):
    t = act.shape[0]
    d = n_heads * HEAD_DIM
    c = DN_CHUNK
    nc = t // c

    def body(q_ref, k_ref, v_ref, bg_ref, al_ref, dt_ref, o_ref, snap_ref, st_ref):
        @pl.when(pl.program_id(0) == 0)
        def _():
            st_ref[...] = jnp.zeros(st_ref.shape, F32)

        snap_ref[0] = st_ref[...]
        cols = [slice(h * HEAD_DIM, (h + 1) * HEAD_DIM) for h in range(n_heads)]
        os, new_states = _delta_heads([q_ref[:, sl] for sl in cols], [k_ref[:, sl] for sl in cols],
                                      [v_ref[:, sl] for sl in cols], bg_ref[...], al_ref[...], dt_ref[...],
                                      [st_ref[h] for h in range(n_heads)])
        for h, sl in enumerate(cols):
            o_ref[:, sl] = os[h]
            st_ref[h] = new_states[h]

    return pl.pallas_call(
        body, name="delta_fwd", grid=(nc,),
        in_specs=[pl.BlockSpec((c, d), lambda i: (i, 0)), pl.BlockSpec((c, d), lambda i: (i, 1)),
                  pl.BlockSpec((c, d), lambda i: (i, 2)), pl.BlockSpec((c, LANES), lambda i: (i, 0)),
                  _const_spec((1, LANES)), _const_spec((1, LANES))],
        out_specs=[pl.BlockSpec((c, d), lambda i: (i, 0)),
                   pl.BlockSpec((1, n_heads, HEAD_DIM, HEAD_DIM), lambda i: (i, 0, 0, 0))],
        out_shape=[jax.ShapeDtypeStruct((t, d), F32), jax.ShapeDtypeStruct((nc, n_heads, HEAD_DIM, HEAD_DIM), F32)],
        scratch_shapes=[pltpu.VMEM((n_heads, HEAD_DIM, HEAD_DIM), F32)],
        compiler_params=_params(dimension_semantics=("arbitrary",)),
    )(act, act, act, ba, alog, dtb)


def _delta_bwd(act, ba, alog, dtb, snaps, do, n_heads):
    t = act.shape[0]
    d = n_heads * HEAD_DIM
    c = DN_CHUNK
    nc = t // c

    def body(q_ref, k_ref, v_ref, bg_ref, al_ref, dt_ref, snap_ref, do_ref,
             dact_ref, dbg_ref, dal_ref, ddt_ref, ds_ref):
        @pl.when(pl.program_id(0) == 0)
        def _():
            ds_ref[...] = jnp.zeros(ds_ref.shape, F32)
            dal_ref[...] = jnp.zeros((1, LANES), F32)
            ddt_ref[...] = jnp.zeros((1, LANES), F32)

        heads = range(n_heads)
        cols = [slice(h * HEAD_DIM, (h + 1) * HEAD_DIM) for h in heads]
        _, vjp = jax.vjp(_delta_heads, tuple(q_ref[:, sl] for sl in cols), tuple(k_ref[:, sl] for sl in cols),
                         tuple(v_ref[:, sl] for sl in cols), bg_ref[...], al_ref[...], dt_ref[...],
                         tuple(snap_ref[0, h] for h in heads))
        dq, dk, dv, dbg, dal, ddt, dst = vjp((tuple(do_ref[:, sl] for sl in cols), tuple(ds_ref[h] for h in heads)))
        for h, sl in enumerate(cols):
            dact_ref[:, sl] = dq[h]
            dact_ref[:, d + h * HEAD_DIM:d + (h + 1) * HEAD_DIM] = dk[h]
            dact_ref[:, 2 * d + h * HEAD_DIM:2 * d + (h + 1) * HEAD_DIM] = dv[h]
            ds_ref[h] = dst[h]
        dal_ref[...] += dal
        ddt_ref[...] += ddt
        dbg_ref[...] = dbg.astype(BF16)

    rev = lambda i: nc - 1 - i
    return pl.pallas_call(
        body, name="delta_bwd", grid=(nc,),
        in_specs=[pl.BlockSpec((c, d), lambda i: (rev(i), 0)), pl.BlockSpec((c, d), lambda i: (rev(i), 1)),
                  pl.BlockSpec((c, d), lambda i: (rev(i), 2)), pl.BlockSpec((c, LANES), lambda i: (rev(i), 0)),
                  _const_spec((1, LANES)), _const_spec((1, LANES)),
                  pl.BlockSpec((1, n_heads, HEAD_DIM, HEAD_DIM), lambda i: (rev(i), 0, 0, 0)),
                  pl.BlockSpec((c, d), lambda i: (rev(i), 0))],
        out_specs=[pl.BlockSpec((c, 3 * d), lambda i: (rev(i), 0)), pl.BlockSpec((c, LANES), lambda i: (rev(i), 0)),
                   _const_spec((1, LANES)), _const_spec((1, LANES))],
        out_shape=[jax.ShapeDtypeStruct((t, 3 * d), F32), jax.ShapeDtypeStruct((t, LANES), BF16),
                   jax.ShapeDtypeStruct((1, LANES), F32), jax.ShapeDtypeStruct((1, LANES), F32)],
        scratch_shapes=[pltpu.VMEM((n_heads, HEAD_DIM, HEAD_DIM), F32)],
        compiler_params=_params(dimension_semantics=("arbitrary",)),
    )(act, act, act, ba, alog, dtb, snaps, do)


def _head_norm2(a, b, ga, gb):
    return _rms(a, ga, RMS_EPS), _rms(b, gb, RMS_EPS)


def _sb_prep_fwd(proj, qn, kn, n_heads):
    t = proj.shape[0]
    d = n_heads * HEAD_DIM
    tm = _pick(t, (256, 128))

    def body(q_ref, k_ref, qn_ref, kn_ref, qo_ref, ko_ref):
        for h in range(n_heads):
            sl = slice(h * HEAD_DIM, (h + 1) * HEAD_DIM)
            qo_ref[:, sl], ko_ref[:, sl] = _head_norm2(q_ref[:, sl], k_ref[:, sl], qn_ref[...], kn_ref[...])

    return pl.pallas_call(
        body, name="sb_prep_fwd", grid=(t // tm,),
        in_specs=[pl.BlockSpec((tm, d), lambda i: (i, 4)), pl.BlockSpec((tm, d), lambda i: (i, 5)),
                  _const_spec((1, HEAD_DIM)), _const_spec((1, HEAD_DIM))],
        out_specs=[pl.BlockSpec((tm, d), lambda i: (i, 0)), pl.BlockSpec((tm, d), lambda i: (i, 0))],
        out_shape=[jax.ShapeDtypeStruct((t, d), F32), jax.ShapeDtypeStruct((t, d), F32)],
        compiler_params=_params(dimension_semantics=("arbitrary",)),
    )(proj, proj, qn, kn)


def _sb_prep_bwd(proj, qn, kn, dq, dk, n_heads):
    t = proj.shape[0]
    d = n_heads * HEAD_DIM
    tm = _pick(t, (256, 128))

    def body(q_ref, k_ref, qn_ref, kn_ref, dq_ref, dk_ref, dqo_ref, dko_ref, dqn_ref, dkn_ref):
        @pl.when(pl.program_id(0) == 0)
        def _():
            dqn_ref[...] = jnp.zeros((1, HEAD_DIM), F32)
            dkn_ref[...] = jnp.zeros((1, HEAD_DIM), F32)

        for h in range(n_heads):
            sl = slice(h * HEAD_DIM, (h + 1) * HEAD_DIM)
            _, vjp = jax.vjp(_head_norm2, q_ref[:, sl], k_ref[:, sl], qn_ref[...], kn_ref[...])
            da, db, dga, dgb = vjp((dq_ref[:, sl], dk_ref[:, sl]))
            dqo_ref[:, sl] = da.astype(BF16)
            dko_ref[:, sl] = db.astype(BF16)
            dqn_ref[...] += dga
            dkn_ref[...] += dgb

    return pl.pallas_call(
        body, name="sb_prep_bwd", grid=(t // tm,),
        in_specs=[pl.BlockSpec((tm, d), lambda i: (i, 4)), pl.BlockSpec((tm, d), lambda i: (i, 5)),
                  _const_spec((1, HEAD_DIM)), _const_spec((1, HEAD_DIM)),
                  pl.BlockSpec((tm, d), lambda i: (i, 0)), pl.BlockSpec((tm, d), lambda i: (i, 0))],
        out_specs=[pl.BlockSpec((tm, d), lambda i: (i, 0)), pl.BlockSpec((tm, d), lambda i: (i, 0)),
                   _const_spec((1, HEAD_DIM)), _const_spec((1, HEAD_DIM))],
        out_shape=[jax.ShapeDtypeStruct((t, d), BF16), jax.ShapeDtypeStruct((t, d), BF16),
                   jax.ShapeDtypeStruct((1, HEAD_DIM), F32), jax.ShapeDtypeStruct((1, HEAD_DIM), F32)],
        compiler_params=_params(dimension_semantics=("arbitrary",)),
    )(proj, proj, qn, kn, dq, dk)


def _cumsum_mm(x, tri):
    hi, lo = _split(x)
    return (lax.dot_general(hi, tri, (_NN, ((), ())), preferred_element_type=F32)
            + lax.dot_general(lo, tri, (_NN, ((), ())), preferred_element_type=F32))


def _sb_valid(i, j):
    row = lax.broadcasted_iota(jnp.int32, (SB_BLOCK, SB_BLOCK), 0)
    col = lax.broadcasted_iota(jnp.int32, (SB_BLOCK, SB_BLOCK), 1)
    return (col + j * SB_BLOCK) < (row + i * SB_BLOCK)


def _sb_attn_fwd(qb, kb, proj, n_heads):
    t = qb.shape[0]
    d = n_heads * HEAD_DIM
    nq = t // SB_BLOCK
    hb = _pick(n_heads, (4, 2, 1))
    wide = hb * HEAD_DIM
    v_col0 = 6 * n_heads // hb
    scale = HEAD_DIM ** -0.5

    def body(q_ref, k_ref, v_ref, o_ref, lt_ref):
        i = pl.program_id(1)
        row = lax.broadcasted_iota(jnp.int32, (SB_BLOCK, SB_BLOCK), 0)
        col = lax.broadcasted_iota(jnp.int32, (SB_BLOCK, SB_BLOCK), 1)
        after = jnp.where(row > col, 1.0, 0.0).astype(BF16)
        heads = [slice(h * HEAD_DIM, (h + 1) * HEAD_DIM) for h in range(hb)]
        every = range(hb)
        qs = [q_ref[:, sl].astype(BF16) for sl in heads]

        def step(jj, carry):
            j = i - jj
            rows = pl.ds(pl.multiple_of(j * SB_BLOCK, SB_BLOCK), SB_BLOCK)
            valid = _sb_valid(i, j)
            z = [_mm_nt(qs[h], k_ref[rows, heads[h]]) * scale for h in every]
            sp = [_softplus_raw(z[h]) for h in every]
            lm = [jnp.where(valid, -sp[h], 0.0) for h in every]
            tail = [_cumsum_mm(lm[h], after) for h in every]
            w = [jnp.where(valid, jnp.exp(z[h] - sp[h] + carry[h][1] + tail[h]), 0.0) for h in every]
            pv = [_mm_nn(w[h], v_ref[rows, heads[h]]) for h in every]
            return tuple((carry[h][0] + pv[h], carry[h][1] + jnp.sum(lm[h], axis=1, keepdims=True)) for h in every)

        init = tuple((jnp.zeros((SB_BLOCK, HEAD_DIM), F32), jnp.zeros((SB_BLOCK, 1), F32)) for _ in heads)
        res = lax.fori_loop(0, i + 1, step, init)
        for h, sl in enumerate(heads):
            o_ref[:, sl] = res[h][0]
            lt_ref[:, sl] = jnp.broadcast_to(res[h][1], (SB_BLOCK, HEAD_DIM))

    return pl.pallas_call(
        body, name="sb_attn_fwd", grid=(n_heads // hb, nq),
        in_specs=[pl.BlockSpec((SB_BLOCK, wide), lambda g, i: (i, g)),
                  pl.BlockSpec((t, wide), lambda g, i: (0, g)),
                  pl.BlockSpec((t, wide), lambda g, i: (0, v_col0 + g))],
        out_specs=[pl.BlockSpec((SB_BLOCK, wide), lambda g, i: (i, g)),
                   pl.BlockSpec((SB_BLOCK, wide), lambda g, i: (i, g))],
        out_shape=[jax.ShapeDtypeStruct((t, d), F32), jax.ShapeDtypeStruct((t, d), F32)],
        compiler_params=_params(dimension_semantics=("arbitrary", "arbitrary")),
    )(qb, kb, proj)


def _sb_attn_bwd(qb, kb, proj, ltot, do, n_heads):
    t = qb.shape[0]
    d = n_heads * HEAD_DIM
    nq = t // SB_BLOCK
    hb = _pick(n_heads, (4, 2, 1))
    wide = hb * HEAD_DIM
    v_col0 = 6 * n_heads // hb
    scale = HEAD_DIM ** -0.5

    def body(q_ref, k_ref, v_ref, lt_ref, do_ref, dq_ref, dk_ref, dv_ref):
        i = pl.program_id(1)

        @pl.when(i == 0)
        def _():
            dk_ref[...] = jnp.zeros((t, wide), F32)
            dv_ref[...] = jnp.zeros((t, wide), F32)

        row = lax.broadcasted_iota(jnp.int32, (SB_BLOCK, SB_BLOCK), 0)
        col = lax.broadcasted_iota(jnp.int32, (SB_BLOCK, SB_BLOCK), 1)
        upto = jnp.where(row <= col, 1.0, 0.0).astype(BF16)
        before = jnp.where(row < col, 1.0, 0.0).astype(BF16)
        heads = [slice(h * HEAD_DIM, (h + 1) * HEAD_DIM) for h in range(hb)]
        every = range(hb)
        qs = [q_ref[:, sl].astype(BF16) for sl in heads]
        dos = [do_ref[:, sl].astype(BF16) for sl in heads]
        totals = [jnp.max(lt_ref[:, sl], axis=1, keepdims=True) for sl in heads]

        def step(j, carry):
            rows = pl.ds(pl.multiple_of(j * SB_BLOCK, SB_BLOCK), SB_BLOCK)
            valid = _sb_valid(i, j)
            kj = [k_ref[rows, heads[h]].astype(BF16) for h in every]
            vj = [v_ref[rows, heads[h]].astype(BF16) for h in every]
            z = [_mm_nt(qs[h], kj[h]) * scale for h in every]
            dw = [_mm_nt(dos[h], vj[h]) for h in every]
            sp = [_softplus_raw(z[h]) for h in every]
            lm = [jnp.where(valid, -sp[h], 0.0) for h in every]
            head = [_cumsum_mm(lm[h], upto) for h in every]
            w = [jnp.where(valid, jnp.exp(z[h] - sp[h] + totals[h] - (carry[h][1] + head[h])), 0.0) for h in every]
            e = [w[h] * dw[h] for h in every]
            e_pre = [carry[h][2] + _cumsum_mm(e[h], before) for h in every]
            sig = [jnp.exp(z[h] - sp[h]) for h in every]
            dz = [jnp.where(valid, e[h] * (1.0 - sig[h]) - e_pre[h] * sig[h], 0.0) * scale for h in every]
            for h in every:
                dv_ref[rows, heads[h]] += _mm_tn(w[h], dos[h])
            for h in every:
                dk_ref[rows, heads[h]] += _mm_tn(dz[h], qs[h])
            dq = [_mm_nn(dz[h], kj[h]) for h in every]
            return tuple((carry[h][0] + dq[h], carry[h][1] + jnp.sum(lm[h], axis=1, keepdims=True),
                          carry[h][2] + jnp.sum(e[h], axis=1, keepdims=True)) for h in every)

        zero_col = jnp.zeros((SB_BLOCK, 1), F32)
        init = tuple((jnp.zeros((SB_BLOCK, HEAD_DIM), F32), zero_col, zero_col) for _ in heads)
        res = lax.fori_loop(0, i + 1, step, init)
        for h, sl in enumerate(heads):
            dq_ref[:, sl] = res[h][0]

    return pl.pallas_call(
        body, name="sb_attn_bwd", grid=(n_heads // hb, nq),
        in_specs=[pl.BlockSpec((SB_BLOCK, wide), lambda g, i: (i, g)),
                  pl.BlockSpec((t, wide), lambda g, i: (0, g)),
                  pl.BlockSpec((t, wide), lambda g, i: (0, v_col0 + g)),
                  pl.BlockSpec((SB_BLOCK, wide), lambda g, i: (i, g)),
                  pl.BlockSpec((SB_BLOCK, wide), lambda g, i: (i, g))],
        out_specs=[pl.BlockSpec((SB_BLOCK, wide), lambda g, i: (i, g)),
                   pl.BlockSpec((t, wide), lambda g, i: (0, g)),
                   pl.BlockSpec((t, wide), lambda g, i: (0, g))],
        out_shape=[jax.ShapeDtypeStruct((t, d), F32), jax.ShapeDtypeStruct((t, d), F32),
                   jax.ShapeDtypeStruct((t, d), F32)],
        compiler_params=_params(dimension_semantics=("arbitrary", "arbitrary")),
    )(qb, kb, proj, ltot, do)


def _gated_norm(oa, z, gn):
    return _rms(oa, gn, RMS_EPS) * _silu(z)


def _merge_gates(ya, yb, ga, gb):
    return jax.nn.sigmoid(ga) * ya + jax.nn.sigmoid(gb) * yb


def _merge_fwd(x1, oa, proj, ob, gn, wa, wb, wo, layer, n_heads):
    t, d = x1.shape
    tm = _pick(t, (256, 128))
    square = _layer_spec(layer, (d, d), lambda i: (0, 0), single=True)

    def body(x_ref, oa_ref, z_ref, ob_ref, ga_ref, gb_ref, gn_ref, wa_ref, wb_ref, wo_ref, o_ref, na_ref):
        for h in range(n_heads):
            sl = slice(h * HEAD_DIM, (h + 1) * HEAD_DIM)
            na_ref[:, sl] = _gated_norm(oa_ref[:, sl], z_ref[:, sl], gn_ref[...]).astype(BF16)
        m = _merge_gates(_mm_nn(na_ref[...], wa_ref[...]), _mm_nn(ob_ref[...], wb_ref[...]), ga_ref[...], gb_ref[...])
        o_ref[...] = x_ref[...] + _mm_nn(m, wo_ref[...])

    tile = lambda k: pl.BlockSpec((tm, d), lambda i: (i, k))
    return pl.pallas_call(
        body, name="merge_fwd", grid=(t // tm,),
        in_specs=[tile(0), tile(0), tile(3), tile(0), tile(7), tile(8), _const_spec((1, HEAD_DIM)),
                  square, square, square],
        out_specs=tile(0),
        out_shape=jax.ShapeDtypeStruct((t, d), F32),
        scratch_shapes=[pltpu.VMEM((tm, d), BF16)],
        compiler_params=_params(dimension_semantics=("arbitrary",)),
    )(x1, oa, proj, ob, proj, proj, gn, wa, wb, wo)


def _merge_bwd(oa, proj, ob, dy, gn, wa, wb, wo, layer, n_heads):
    t, d = oa.shape
    tm = _pick(t, (256, 128))
    nt = t // tm
    square = _layer_spec(layer, (d, d), lambda i: (0, 0), single=True)

    def body(oa_ref, z_ref, ob_ref, ga_ref, gb_ref, dy_ref, gn_ref, wa_ref, wb_ref, wo_ref,
             doa_ref, dz_ref, dob_ref, dga_ref, dgb_ref, dgn_ref, dwa_hbm, dwb_hbm, dwo_hbm,
             na_ref, dna_ref, dwa_ref, dwb_ref, dwo_ref):
        i = pl.program_id(0)

        @pl.when(i == 0)
        def _():
            dgn_ref[...] = jnp.zeros((1, HEAD_DIM), F32)
            dwa_ref[...] = jnp.zeros((d, d), F32)
            dwb_ref[...] = jnp.zeros((d, d), F32)
            dwo_ref[...] = jnp.zeros((d, d), F32)

        for h in range(n_heads):
            sl = slice(h * HEAD_DIM, (h + 1) * HEAD_DIM)
            na_ref[:, sl] = _gated_norm(oa_ref[:, sl], z_ref[:, sl], gn_ref[...]).astype(BF16)
        dy = dy_ref[...].astype(BF16)
        ob = ob_ref[...].astype(BF16)
        ya = _mm_nn(na_ref[...], wa_ref[...])
        yb = _mm_nn(ob, wb_ref[...])
        m, vjp = jax.vjp(_merge_gates, ya, yb, ga_ref[...], gb_ref[...])
        dwo_ref[...] += _mm_tn(m, dy)
        dya, dyb, dga, dgb = vjp(_mm_nt(dy, wo_ref[...]))
        dga_ref[...] = dga.astype(BF16)
        dgb_ref[...] = dgb.astype(BF16)
        dwa_ref[...] += _mm_tn(na_ref[...], dya)
        dwb_ref[...] += _mm_tn(ob, dyb)
        dob_ref[...] = _mm_nt(dyb, wb_ref[...])
        dna_ref[...] = _mm_nt(dya, wa_ref[...])
        for h in range(n_heads):
            sl = slice(h * HEAD_DIM, (h + 1) * HEAD_DIM)
            _, vjp_h = jax.vjp(_gated_norm, oa_ref[:, sl], z_ref[:, sl], gn_ref[...])
            doa, dz, dgn = vjp_h(dna_ref[:, sl])
            doa_ref[:, sl] = doa
            dz_ref[:, sl] = dz.astype(BF16)
            dgn_ref[...] += dgn

        @pl.when(i == nt - 1)
        def _():
            pltpu.sync_copy(dwa_ref, dwa_hbm)
            pltpu.sync_copy(dwb_ref, dwb_hbm)
            pltpu.sync_copy(dwo_ref, dwo_hbm)

    tile = lambda k: pl.BlockSpec((tm, d), lambda i: (i, k))
    any_spec = pl.BlockSpec(memory_space=pl.ANY)
    return pl.pallas_call(
        body, name="merge_bwd", grid=(nt,),
        in_specs=[tile(0), tile(3), tile(0), tile(7), tile(8), tile(0), _const_spec((1, HEAD_DIM)),
                  square, square, square],
        out_specs=[tile(0), tile(0), tile(0), tile(0), tile(0), _const_spec((1, HEAD_DIM)),
                   any_spec, any_spec, any_spec],
        out_shape=[jax.ShapeDtypeStruct((t, d), F32), jax.ShapeDtypeStruct((t, d), BF16),
                   jax.ShapeDtypeStruct((t, d), F32), jax.ShapeDtypeStruct((t, d), BF16),
                   jax.ShapeDtypeStruct((t, d), BF16), jax.ShapeDtypeStruct((1, HEAD_DIM), F32),
                   jax.ShapeDtypeStruct((d, d), F32), jax.ShapeDtypeStruct((d, d), F32),
                   jax.ShapeDtypeStruct((d, d), F32)],
        scratch_shapes=[pltpu.VMEM((tm, d), BF16), pltpu.VMEM((tm, d), F32),
                        pltpu.VMEM((d, d), F32), pltpu.VMEM((d, d), F32), pltpu.VMEM((d, d), F32)],
        compiler_params=_params(dimension_semantics=("arbitrary",)),
    )(oa, proj, ob, proj, proj, dy, gn, wa, wb, wo)


def _loss_head(y, target):
    t, d = y.shape
    tm = _pick(t, (256, 128))

    def body(y_ref, t_ref, dy_ref, loss_ref):
        @pl.when(pl.program_id(0) == 0)
        def _():
            loss_ref[...] = jnp.zeros((8, LANES), F32)

        err = y_ref[...] - t_ref[...]
        dy_ref[...] = err * (1.0 / d)
        per_token = jnp.sum(err * err, axis=1, keepdims=True) * (1.0 / d)
        loss_ref[...] += 0.5 * jnp.sum(per_token, axis=0, keepdims=True)

    return pl.pallas_call(
        body, name="loss_head", grid=(t // tm,),
        in_specs=[pl.BlockSpec((tm, d), lambda i: (i, 0)), pl.BlockSpec((tm, d), lambda i: (i, 0))],
        out_specs=[pl.BlockSpec((tm, d), lambda i: (i, 0)), _const_spec((8, LANES))],
        out_shape=[jax.ShapeDtypeStruct((t, d), F32), jax.ShapeDtypeStruct((8, LANES), F32)],
        compiler_params=_params(dimension_semantics=("arbitrary",)),
    )(y, target)


def _adamw(w, g, m, v):
    rows, cols = w.shape
    tr = rows
    for cand in (512, 256, 128, 64, 32, 16, 8):
        if rows % cand == 0 and cand * cols * 4 <= 2 * 1024 * 1024:
            tr = cand
            break

    def body(w_ref, g_ref, m_ref, v_ref, d_ref, mo_ref, vo_ref):
        g = g_ref[...]
        m2 = ADAM_B1 * m_ref[...] + (1.0 - ADAM_B1) * g
        v2 = ADAM_B2 * v_ref[...] + (1.0 - ADAM_B2) * (g * g)
        m_hat = m2 / (1.0 - ADAM_B1 ** ADAM_STEP)
        v_hat = v2 / (1.0 - ADAM_B2 ** ADAM_STEP)
        d_ref[...] = -ADAM_LR * (m_hat / (jnp.sqrt(v_hat) + ADAM_EPS) + ADAM_WD * w_ref[...])
        mo_ref[...] = m2
        vo_ref[...] = v2

    spec = pl.BlockSpec((tr, cols), lambda i: (i, 0))
    shape = jax.ShapeDtypeStruct((rows, cols), F32)
    return pl.pallas_call(
        body, name="adamw", grid=(rows // tr,), in_specs=[spec] * 4, out_specs=[spec] * 3,
        out_shape=[shape] * 3, compiler_params=_params(dimension_semantics=("arbitrary",)),
    )(w, g, m, v)


def _place():
    return lax.axis_index("x"), lax.axis_index("y"), lax.axis_index("c")


def _other_chips(x, y):
    return [(1 - x, y), (x, 1 - y), (1 - x, 1 - y)]


def _tile_rows(rows, cols, itemsize, cap=1536 * 1024):
    best = None
    for cand in range(16, rows + 1, 16):
        if rows % cand == 0 and cand * cols * itemsize <= cap:
            best = cand
    return best if best is not None else rows


def _allgather_layer(shards, layer, collective_id):
    n = len(shards)

    def body(*refs):
        srcs, outs = refs[:n], refs[n:2 * n]
        send_sems, recv_sems = refs[2 * n:]
        x, y, c = _place()
        me, sibling = (x, y, c), (x, y, 1 - c)
        chips = _other_chips(x, y)
        barrier = pltpu.get_barrier_semaphore()
        for peer in [(*chip, c) for chip in chips] + [sibling]:
            pl.semaphore_signal(barrier, inc=1, device_id=peer, device_id_type=MESH)
        pl.semaphore_wait(barrier, N_CHIPS)

        def half(w, which):
            rows = shards[w].shape[1] // 2
            return pl.ds(which * rows, rows)

        def copy(w, k, shard, which, to, from_src=False):
            part = half(w, which)
            return pltpu.make_async_remote_copy(
                src_ref=srcs[w].at[layer, part] if from_src else outs[w].at[shard, part],
                dst_ref=outs[w].at[shard, part], send_sem=send_sems.at[6 * w + k], recv_sem=recv_sems.at[6 * w + k],
                device_id=to, device_id_type=MESH)

        first = [copy(w, j, 2 * x + y, c, (*chip, c), from_src=True) for j, chip in enumerate(chips) for w in range(n)]
        for cp in first:
            cp.start()
        passed = []
        for j, (cx, cy) in enumerate(chips):
            for w in range(n):
                copy(w, j, 2 * cx + cy, c, me).wait_recv()
                cp = copy(w, 3 + j, 2 * cx + cy, c, sibling)
                cp.start()
                passed.append(cp)
        for j, (cx, cy) in enumerate(chips):
            for w in range(n):
                copy(w, 3 + j, 2 * cx + cy, 1 - c, me).wait_recv()
        for cp in first + passed:
            cp.wait_send()

    return pl.kernel(
        body, name=f"allgather_layer{layer}",
        out_type=[jax.ShapeDtypeStruct((N_CHIPS,) + s.shape[1:], s.dtype) for s in shards],
        mesh=plsc.ScalarSubcoreMesh(axis_name="sequencer", num_cores=1),
        scratch_types=[pltpu.SemaphoreType.DMA((6 * n,)), pltpu.SemaphoreType.DMA((6 * n,))],
        compiler_params=pltpu.CompilerParams(collective_id=collective_id),
    )(*shards)


def _swap_halves(grads):
    n = len(grads)
    half = grads[0].shape[1] // 2

    def body(*refs):
        gs, gots = refs[:n], refs[n:2 * n]
        send_sems, recv_sems = refs[2 * n:]
        x, y, c = _place()
        theirs = pl.ds((1 - c) * half, half)
        copies = [pltpu.make_async_remote_copy(src_ref=gs[w].at[:, theirs], dst_ref=gots[w], send_sem=send_sems.at[w],
                                               recv_sem=recv_sems.at[w], device_id=(x, y, 1 - c), device_id_type=MESH)
                  for w in range(n)]
        for cp in copies:
            cp.start()
        for cp in copies:
            cp.wait()

    hbm = pl.BlockSpec(memory_space=pl.ANY)
    return pl.pallas_call(
        body, name="swap_halves", in_specs=[hbm] * n, out_specs=[hbm] * n,
        out_shape=[jax.ShapeDtypeStruct((g.shape[0], half) + g.shape[2:], g.dtype) for g in grads],
        scratch_shapes=[pltpu.SemaphoreType.DMA((n,)), pltpu.SemaphoreType.DMA((n,))],
    )(*grads)


def _add_half(grad, got, c_idx):
    n, depth, rows, cols = grad.shape
    half = depth // 2
    tr = _tile_rows(rows, cols, 2)

    def body(c_ref, a_ref, b_ref, o_ref):
        o_ref[...] = (a_ref[...].astype(F32) + b_ref[...].astype(F32)).astype(o_ref.dtype)

    return pl.pallas_call(
        body, name="add_half",
        grid_spec=pltpu.PrefetchScalarGridSpec(
            num_scalar_prefetch=1, grid=(n, half, rows // tr),
            in_specs=[pl.BlockSpec((1, 1, tr, cols), lambda s, l, r, c_ref: (s, c_ref[0] * half + l, r, 0)),
                      pl.BlockSpec((1, 1, tr, cols), lambda s, l, r, c_ref: (s, l, r, 0))],
            out_specs=pl.BlockSpec((1, 1, tr, cols), lambda s, l, r, c_ref: (s, l, r, 0))),
        out_shape=jax.ShapeDtypeStruct((n, half, rows, cols), grad.dtype),
        compiler_params=_params(dimension_semantics=("arbitrary", "arbitrary", "arbitrary")),
    )(c_idx, grad, got)


def _scatter_partials(parts):
    n = len(parts)

    def body(*refs):
        ps, gots = refs[:n], refs[n:2 * n]
        send_sems, recv_sems = refs[2 * n:]
        x, y, c = _place()
        copies = [pltpu.make_async_remote_copy(src_ref=ps[w].at[2 * cx + cy], dst_ref=gots[w].at[j],
                                               send_sem=send_sems.at[3 * w + j], recv_sem=recv_sems.at[3 * w + j],
                                               device_id=(cx, cy, c), device_id_type=MESH)
                  for j, (cx, cy) in enumerate(_other_chips(x, y)) for w in range(n)]
        for cp in copies:
            cp.start()
        for cp in copies:
            cp.wait()

    hbm = pl.BlockSpec(memory_space=pl.ANY)
    return pl.pallas_call(
        body, name="scatter_partials", in_specs=[hbm] * n, out_specs=[hbm] * n,
        out_shape=[jax.ShapeDtypeStruct((N_CHIPS - 1,) + p.shape[1:], p.dtype) for p in parts],
        scratch_shapes=[pltpu.SemaphoreType.DMA((3 * n,)), pltpu.SemaphoreType.DMA((3 * n,))],
    )(*parts)


def _sum_partials(part, got, s_idx, c_idx):
    n, half, rows, cols = part.shape
    tr = _tile_rows(rows, cols, 2, cap=1024 * 1024)

    def body(s_ref, c_ref, a_ref, b_ref, o_ref):
        acc = a_ref[0, 0].astype(F32)
        for j in range(n - 1):
            acc = acc + b_ref[j, 0].astype(F32)
        o_ref[0] = acc

    return pl.pallas_call(
        body, name="sum_partials",
        grid_spec=pltpu.PrefetchScalarGridSpec(
            num_scalar_prefetch=2, grid=(half, rows // tr),
            in_specs=[pl.BlockSpec((1, 1, tr, cols), lambda l, r, s_ref, c_ref: (s_ref[0], l, r, 0)),
                      pl.BlockSpec((n - 1, 1, tr, cols), lambda l, r, s_ref, c_ref: (0, l, r, 0))],
            out_specs=pl.BlockSpec((1, tr, cols), lambda l, r, s_ref, c_ref: (c_ref[0] * half + l, r, 0))),
        out_shape=jax.ShapeDtypeStruct((2 * half, rows, cols), F32),
        compiler_params=_params(dimension_semantics=("arbitrary", "arbitrary")),
    )(s_idx, c_idx, part, got)


def _join_halves(bufs):
    n = len(bufs)
    half = bufs[0].shape[0] // 2

    def body(*refs):
        outs = refs[n:2 * n]
        send_sems, recv_sems = refs[2 * n:]
        x, y, c = _place()
        my_half = pl.ds(c * half, half)
        copies = [pltpu.make_async_remote_copy(src_ref=outs[w].at[my_half], dst_ref=outs[w].at[my_half],
                                               send_sem=send_sems.at[w], recv_sem=recv_sems.at[w],
                                               device_id=(x, y, 1 - c), device_id_type=MESH)
                  for w in range(n)]
        for cp in copies:
            cp.start()
        for cp in copies:
            cp.wait()

    hbm = pl.BlockSpec(memory_space=pl.ANY)
    return pl.pallas_call(
        body, name="join_halves", in_specs=[hbm] * n, out_specs=[hbm] * n,
        out_shape=[jax.ShapeDtypeStruct(b.shape, b.dtype) for b in bufs],
        input_output_aliases={w: w for w in range(n)},
        scratch_shapes=[pltpu.SemaphoreType.DMA((n,)), pltpu.SemaphoreType.DMA((n,))],
    )(*bufs)


def _allreduce_small(v, name):
    rows = v.shape[0]

    def body(v_ref, o_ref, gath, send_sems, recv_sems):
        x, y, c = _place()
        idx = 4 * x + 2 * y + c
        gath[0] = v_ref[...]
        copies = []
        for r in range(1, N_DEV):
            peer = (1 - x if r & 4 else x, 1 - y if r & 2 else y, 1 - c if r & 1 else c)
            cp = pltpu.make_async_remote_copy(src_ref=v_ref, dst_ref=gath.at[r], send_sem=send_sems.at[r - 1],
                                              recv_sem=recv_sems.at[r - 1], device_id=peer, device_id_type=MESH)
            cp.start()
            copies.append(cp)
        for cp in copies:
            cp.wait()
        acc = gath[idx]
        for a in range(1, N_DEV):
            acc = acc + gath[lax.bitwise_xor(idx, a)]
        o_ref[...] = acc

    vmem = pl.BlockSpec(memory_space=pltpu.VMEM)
    return pl.pallas_call(
        body, name=name, in_specs=[vmem], out_specs=vmem,
        out_shape=jax.ShapeDtypeStruct((rows, LANES), F32),
        scratch_shapes=[pltpu.VMEM((N_DEV, rows, LANES), F32), pltpu.SemaphoreType.DMA((N_DEV - 1,)),
                        pltpu.SemaphoreType.DMA((N_DEV - 1,))],
    )(v)


def _join_shards(name, gathered):
    return jnp.concatenate([gathered[s] for s in range(N_CHIPS)], axis=1 if name in COL_SHARDED else 0)[None]


def _split_shards(name, per_layer):
    def shard(g, s):
        if name in COL_SHARDED:
            width = g.shape[1] // N_CHIPS
            return g[:, s * width:(s + 1) * width]
        height = g.shape[0] // N_CHIPS
        return g[s * height:(s + 1) * height, :]

    return jnp.stack([jnp.stack([shard(g, s).astype(BF16) for g in per_layer]) for s in range(N_CHIPS)])


def _pad_small(flat):
    n = flat.shape[0]
    block = 8 * LANES
    padded = -(-n // block) * block
    return jnp.pad(flat, (0, padded - n)).reshape(padded // LANES, LANES)


def kernel(x, ffn1_norm, ffn1_w_in, ffn1_w_out, mix_norm, w_in, dn_conv_w, dn_a_log, dn_dt_bias, dn_out_norm, sb_q_norm, sb_k_norm, w_branch_a, w_branch_b, w_out, ffn2_norm, ffn2_w_in, ffn2_w_out, loss_target, m_ffn1_norm, m_ffn1_w_in, m_ffn1_w_out, m_mix_norm, m_w_in, m_dn_conv_w, m_dn_a_log, m_dn_dt_bias, m_dn_out_norm, m_sb_q_norm, m_sb_k_norm, m_w_branch_a, m_w_branch_b, m_w_out, m_ffn2_norm, m_ffn2_w_in, m_ffn2_w_out, v_ffn1_norm, v_ffn1_w_in, v_ffn1_w_out, v_mix_norm, v_w_in, v_dn_conv_w, v_dn_a_log, v_dn_dt_bias, v_dn_out_norm, v_sb_q_norm, v_sb_k_norm, v_w_branch_a, v_w_branch_b, v_w_out, v_ffn2_norm, v_ffn2_w_in, v_ffn2_w_out):
    w = dict(ffn1_norm=ffn1_norm, ffn1_w_in=ffn1_w_in, ffn1_w_out=ffn1_w_out, mix_norm=mix_norm, w_in=w_in,
             dn_conv_w=dn_conv_w, dn_a_log=dn_a_log, dn_dt_bias=dn_dt_bias, dn_out_norm=dn_out_norm,
             sb_q_norm=sb_q_norm, sb_k_norm=sb_k_norm, w_branch_a=w_branch_a, w_branch_b=w_branch_b, w_out=w_out,
             ffn2_norm=ffn2_norm, ffn2_w_in=ffn2_w_in, ffn2_w_out=ffn2_w_out)
    mom = dict(ffn1_norm=m_ffn1_norm, ffn1_w_in=m_ffn1_w_in, ffn1_w_out=m_ffn1_w_out, mix_norm=m_mix_norm, w_in=m_w_in,
               dn_conv_w=m_dn_conv_w, dn_a_log=m_dn_a_log, dn_dt_bias=m_dn_dt_bias, dn_out_norm=m_dn_out_norm,
               sb_q_norm=m_sb_q_norm, sb_k_norm=m_sb_k_norm, w_branch_a=m_w_branch_a, w_branch_b=m_w_branch_b,
               w_out=m_w_out, ffn2_norm=m_ffn2_norm, ffn2_w_in=m_ffn2_w_in, ffn2_w_out=m_ffn2_w_out)
    var = dict(ffn1_norm=v_ffn1_norm, ffn1_w_in=v_ffn1_w_in, ffn1_w_out=v_ffn1_w_out, mix_norm=v_mix_norm, w_in=v_w_in,
               dn_conv_w=v_dn_conv_w, dn_a_log=v_dn_a_log, dn_dt_bias=v_dn_dt_bias, dn_out_norm=v_dn_out_norm,
               sb_q_norm=v_sb_q_norm, sb_k_norm=v_sb_k_norm, w_branch_a=v_w_branch_a, w_branch_b=v_w_branch_b,
               w_out=v_w_out, ffn2_norm=v_ffn2_norm, ffn2_w_in=v_ffn2_w_in, ffn2_w_out=v_ffn2_w_out)

    _, t, d = x.shape
    depth = ffn1_norm.shape[0]
    n_heads = d // HEAD_DIM
    conv_cols = dn_conv_w.shape[2]
    assert d % HEAD_DIM == 0 and t % SB_BLOCK == 0 and 2 * n_heads <= LANES and depth % 2 == 0
    assert w_in.shape[2] * N_CHIPS == 9 * d + 2 * n_heads and conv_cols * N_CHIPS == 3 * d

    x_idx, y_idx, c_idx = _place()
    shard = 2 * x_idx + y_idx
    c_arr = jnp.reshape(c_idx, (1,)).astype(jnp.int32)
    s_arr = jnp.reshape(shard, (1,)).astype(jnp.int32)

    mine = [w[n].astype(BF16) for n in BIG]
    cut = 4 * d
    layers = []
    for l in range(depth):
        gathered = _allgather_layer(mine, l, collective_id=l)
        gathered = [lax.dynamic_update_slice(g, m[l][None], (shard, 0, 0)) for g, m in zip(gathered, mine)]
        full = {n: _join_shards(n, g) for n, g in zip(BIG, gathered)}
        full["w_main"] = jnp.concatenate([full["w_in"][..., :cut], full["w_in"][..., cut + 2 * n_heads:]], axis=-1)
        full["w_ba"] = jnp.pad(full["w_in"][..., cut:cut + 2 * n_heads], ((0, 0), (0, 0), (0, LANES - 2 * n_heads)))
        layers.append(full)

    conv_place = lax.dynamic_update_slice(jnp.zeros((depth, DN_CONV, 3 * d), F32), dn_conv_w, (0, 0, shard * conv_cols))
    conv_rows = _pad_small(conv_place.reshape(-1))
    conv_full = (0.5 * _allreduce_small(conv_rows, "allgather_conv")).reshape(-1)[:depth * DN_CONV * 3 * d]
    conv_full = jnp.pad(conv_full.reshape(depth, DN_CONV, 3 * d), ((0, 0), (0, CONV_ROWS - DN_CONV), (0, 0)))

    def head_row(vals):
        return jnp.pad(vals, (n_heads, LANES - 2 * n_heads)).reshape(1, LANES)

    saved = []
    cur = x[0]
    for l in range(depth):
        x0 = cur
        full = layers[l]
        x1 = _ffn_fwd(x0, ffn1_norm[l][None], full["ffn1_w_in"], full["ffn1_w_out"], 0)
        proj, ba = _proj_fwd(x1, mix_norm[l][None], full["w_main"], full["w_ba"], 0)
        act = _dn_prep_fwd(proj, conv_full[l], n_heads)
        alog, dtb = head_row(dn_a_log[l]), head_row(dn_dt_bias[l])
        oa, snaps = _delta_fwd(act, ba, alog, dtb, n_heads)
        qb, kb = _sb_prep_fwd(proj, sb_q_norm[l][None], sb_k_norm[l][None], n_heads)
        ob, ltot = _sb_attn_fwd(qb, kb, proj, n_heads)
        x2 = _merge_fwd(x1, oa, proj, ob, dn_out_norm[l][None], full["w_branch_a"], full["w_branch_b"],
                        full["w_out"], 0, n_heads)
        cur = _ffn_fwd(x2, ffn2_norm[l][None], full["ffn2_w_in"], full["ffn2_w_out"], 0)
        saved.append((x0, x1, proj, ba, act, alog, dtb, oa, snaps, qb, kb, ob, ltot, x2))

    dcur, loss_part = _loss_head(cur, loss_target[0])

    grads = {n: [None] * depth for n in WEIGHTS}
    for l in reversed(range(depth)):
        x0, x1, proj, ba, act, alog, dtb, oa, snaps, qb, kb, ob, ltot, x2 = saved[l]
        full = layers[l]
        dx2, dg, dwg, dwu, dwo = _ffn_bwd(x2, ffn2_norm[l][None], dcur, full["ffn2_w_in"], full["ffn2_w_out"], 0)
        grads["ffn2_norm"][l] = dg[0]
        grads["ffn2_w_in"][l] = jnp.concatenate([dwg, dwu], axis=1)
        grads["ffn2_w_out"][l] = dwo
        doa, dz, dob, dga, dgb, dgn, dwa, dwb, dwout = _merge_bwd(
            oa, proj, ob, dx2, dn_out_norm[l][None], full["w_branch_a"], full["w_branch_b"], full["w_out"], 0,
            n_heads)
        grads["dn_out_norm"][l] = dgn[0]
        grads["w_branch_a"][l], grads["w_branch_b"][l], grads["w_out"][l] = dwa, dwb, dwout
        dqb, dkb, dvb = _sb_attn_bwd(qb, kb, proj, ltot, dob, n_heads)
        dsq, dsk, dqn, dkn = _sb_prep_bwd(proj, sb_q_norm[l][None], sb_k_norm[l][None], dqb, dkb, n_heads)
        grads["sb_q_norm"][l], grads["sb_k_norm"][l] = dqn[0], dkn[0]
        dact, dba, dal, ddt = _delta_bwd(act, ba, alog, dtb, snaps, doa, n_heads)
        grads["dn_a_log"][l] = dal[0, n_heads:2 * n_heads]
        grads["dn_dt_bias"][l] = ddt[0, n_heads:2 * n_heads]
        dqkv, dconv = _dn_prep_bwd(proj, conv_full[l], dact, n_heads)
        grads["dn_conv_w"][l] = dconv[:DN_CONV]
        dproj = jnp.concatenate([dqkv, dz, dsq, dsk, dvb.astype(BF16), dga, dgb], axis=1)
        dx1, dg, dwm, dwba = _proj_bwd(x1, mix_norm[l][None], dx2, dproj, dba, full["w_main"], full["w_ba"], 0)
        grads["mix_norm"][l] = dg[0]
        grads["w_in"][l] = jnp.concatenate([dwm[:, :cut], dwba[:, :2 * n_heads], dwm[:, cut:]], axis=1)
        dcur, dg, dwg, dwu, dwo = _ffn_bwd(x0, ffn1_norm[l][None], dx1, full["ffn1_w_in"], full["ffn1_w_out"], 0)
        grads["ffn1_norm"][l] = dg[0]
        grads["ffn1_w_in"][l] = jnp.concatenate([dwg, dwu], axis=1)
        grads["ffn1_w_out"][l] = dwo

    g_major = [_split_shards(n, grads[n]) for n in BIG]
    parts = [_add_half(g, got, c_arr) for g, got in zip(g_major, _swap_halves(g_major))]
    halves = [_sum_partials(p, got, s_arr, c_arr) for p, got in zip(parts, _scatter_partials(parts))]
    final = dict(zip(BIG, _join_halves(halves)))
    grads = {n: jnp.stack(grads[n]) for n in SMALL + ("dn_conv_w",)}

    small_names = SMALL + ("dn_conv_w",)
    small_sizes = [int(np.prod(grads[n].shape)) for n in small_names]
    small_off = np.concatenate([[0], np.cumsum(small_sizes)])
    small = jnp.concatenate([grads[n].reshape(-1) for n in small_names] + [loss_part[0, :1]])
    small_sum = _allreduce_small(_pad_small(small), "allreduce_small").reshape(-1)
    for i, n in enumerate(small_names):
        final[n] = small_sum[small_off[i]:small_off[i + 1]].reshape(grads[n].shape)
    final["dn_conv_w"] = lax.dynamic_slice(final["dn_conv_w"], (0, 0, shard * conv_cols), (depth, DN_CONV, conv_cols))
    loss = small_sum[small_off[-1]]

    deltas, new_m, new_v = {}, {}, {}
    for n in WEIGHTS:
        shape = w[n].shape
        flat = (-1, shape[-1])
        dl, m2, v2 = _adamw(w[n].reshape(flat), final[n].reshape(flat), mom[n].reshape(flat), var[n].reshape(flat))
        deltas[n], new_m[n], new_v[n] = dl.reshape(shape), m2.reshape(shape), v2.reshape(shape)

    grad_x = dcur[None]
    return (loss, grad_x, *[final[n] for n in WEIGHTS], *[deltas[n] for n in WEIGHTS],
            *[new_m[n] for n in WEIGHTS], *[new_v[n] for n in WEIGHTS])
```

```python
import functools

import jax
import jax.numpy as jnp
import numpy as np
from jax import lax
from jax.experimental import pallas as pl
from jax.experimental.pallas import tpu as pltpu
from jax.experimental.pallas import tpu_sc as plsc

F32 = jnp.float32
BF16 = jnp.bfloat16

LANES = 128
HEAD_DIM = 128
DN_CHUNK = 64
DN_CONV = 4
CONV_ROWS = 8
SB_BLOCK = 128
FFN_HALF = 0.5
RMS_EPS = 1e-6
L2_EPS = 1e-6
NEG_BIG = -1e30
ADAM_LR = 0.001
ADAM_B1 = 0.9
ADAM_B2 = 0.999
ADAM_EPS = 1e-08
ADAM_WD = 0.01
ADAM_STEP = 10
VMEM_LIMIT = 56 * 1024 * 1024
N_CHIPS = 4
N_DEV = 8
MESH = pl.DeviceIdType.MESH

BIG = ("ffn1_w_in", "ffn1_w_out", "w_in", "w_branch_a", "w_branch_b", "w_out", "ffn2_w_in", "ffn2_w_out")
COL_SHARDED = ("ffn1_w_in", "w_in", "ffn2_w_in")
SMALL = ("ffn1_norm", "mix_norm", "dn_a_log", "dn_dt_bias", "dn_out_norm", "sb_q_norm", "sb_k_norm", "ffn2_norm")
WEIGHTS = ("ffn1_norm", "ffn1_w_in", "ffn1_w_out", "mix_norm", "w_in", "dn_conv_w", "dn_a_log", "dn_dt_bias",
           "dn_out_norm", "sb_q_norm", "sb_k_norm", "w_branch_a", "w_branch_b", "w_out", "ffn2_norm", "ffn2_w_in",
           "ffn2_w_out")


def _params(**kw):
    return pltpu.CompilerParams(vmem_limit_bytes=VMEM_LIMIT, **kw)


def _pick(n, options):
    for o in options:
        if n % o == 0:
            return o
    return n


def _const_spec(shape, single=False):
    nd = len(shape)
    if single:
        return pl.BlockSpec(shape, lambda *_: (0,) * nd, pipeline_mode=pl.Buffered(1))
    return pl.BlockSpec(shape, lambda *_: (0,) * nd)


_NN = ((1,), (0,))
_NT = ((1,), (1,))
_TN = ((0,), (0,))


def _dot(a, b, dims):
    return lax.dot_general(a.astype(BF16), b.astype(BF16), (dims, ((), ())), preferred_element_type=F32)


def _mm_nn(a, b):
    return _dot(a, b, _NN)


def _mm_nt(a, b):
    return _dot(a, b, _NT)


def _mm_tn(a, b):
    return _dot(a, b, _TN)


def _split(a):
    hi = a.astype(BF16)
    lo = (a - hi.astype(F32)).astype(BF16)
    return hi, lo


def _dot_precise(a, b, dims):
    dn = (dims, ((), ()))
    ah, al = _split(a)
    bh, bl = _split(b)
    out = lax.dot_general(ah, bh, dn, preferred_element_type=F32)
    out = out + lax.dot_general(ah, bl, dn, preferred_element_type=F32)
    return out + lax.dot_general(al, bh, dn, preferred_element_type=F32)


def _make_diff_mm(dot):
    @jax.custom_vjp
    def nn(a, b):
        return dot(a, b, _NN)

    @jax.custom_vjp
    def nt(a, b):
        return dot(a, b, _NT)

    @jax.custom_vjp
    def tn(a, b):
        return dot(a, b, _TN)

    nn.defvjp(lambda a, b: (dot(a, b, _NN), (a, b)), lambda r, g: (nt(g, r[1]), tn(r[0], g)))
    nt.defvjp(lambda a, b: (dot(a, b, _NT), (a, b)), lambda r, g: (nn(g, r[1]), tn(g, r[0])))
    tn.defvjp(lambda a, b: (dot(a, b, _TN), (a, b)), lambda r, g: (nt(r[1], g), nn(r[0], g)))
    return nn, nt, tn


_d_nn, _d_nt, _d_tn = _make_diff_mm(_dot)
_p_nn, _p_nt, _p_tn = _make_diff_mm(_dot_precise)


def _softplus_raw(x):
    return jnp.maximum(x, 0.0) + jnp.log(1.0 + jnp.exp(-jnp.abs(x)))


@jax.custom_vjp
def _softplus(x):
    return _softplus_raw(x)


_softplus.defvjp(lambda x: (_softplus_raw(x), x), lambda x, g: (g * jax.nn.sigmoid(x),))


def _rms(x, gain, eps):
    return x * lax.rsqrt(jnp.mean(x * x, axis=-1, keepdims=True) + eps) * gain


def _silu(x):
    return x * jax.nn.sigmoid(x)


def _shift_rows_raw(x, k, down):
    n = x.shape[0]
    row = lax.broadcasted_iota(jnp.int32, x.shape, 0)
    if down:
        return jnp.where(row >= k, pltpu.roll(x, k, 0), 0.0)
    return jnp.where(row < n - k, pltpu.roll(x, n - k, 0), 0.0)


@functools.partial(jax.custom_vjp, nondiff_argnums=(1,))
def _shift_down(x, k):
    return _shift_rows_raw(x, k, True)


_shift_down.defvjp(lambda x, k: (_shift_rows_raw(x, k, True), None),
                   lambda k, _, g: (_shift_rows_raw(g, k, False),))


def _layer_spec(layer, block, index_map, single=False):
    full_map = lambda *a: (layer,) + tuple(index_map(*a))
    if single:
        return pl.BlockSpec((None,) + block, full_map, pipeline_mode=pl.Buffered(1))
    return pl.BlockSpec((None,) + block, full_map)


def _ffn_fwd(x, gain, w_in, w_out, layer):
    t, d = x.shape
    f = w_out.shape[1]
    fc = _pick(f, (256, 128))
    nj = f // fc
    rt = _pick(t, (512, 256, 128))

    def body(x_ref, g_ref, wg_ref, wu_ref, wo_ref, o_ref, hs_ref):
        @pl.when(pl.program_id(0) == 0)
        def _():
            for r in range(t // rt):
                rows = pl.ds(r * rt, rt)
                xr = x_ref[rows, :]
                hs_ref[rows, :] = _rms(xr, g_ref[...], RMS_EPS).astype(BF16)
                o_ref[rows, :] = xr

        for r in range(t // rt):
            rows = pl.ds(r * rt, rt)
            h = hs_ref[rows, :]
            a = _mm_nn(h, wg_ref[...])
            b = _mm_nn(h, wu_ref[...])
            o_ref[rows, :] += FFN_HALF * _mm_nn(_silu(a) * b, wo_ref[...])

    return pl.pallas_call(
        body, name="ffn_fwd", grid=(nj,),
        in_specs=[_const_spec((t, d), True), _const_spec((1, d)),
                  _layer_spec(layer, (d, fc), lambda j: (0, j)), _layer_spec(layer, (d, fc), lambda j: (0, nj + j)),
                  _layer_spec(layer, (fc, d), lambda j: (j, 0))],
        out_specs=_const_spec((t, d)),
        out_shape=jax.ShapeDtypeStruct((t, d), F32),
        scratch_shapes=[pltpu.VMEM((t, d), BF16)],
        compiler_params=_params(dimension_semantics=("arbitrary",)),
    )(x, gain, w_in, w_in, w_out)


def _ffn_bwd(x, gain, dy, w_in, w_out, layer):
    t, d = x.shape
    f = w_out.shape[1]
    fc = _pick(f, (256, 128))
    nj = f // fc
    rt = _pick(t, (512, 256, 128))
    nr = t // rt

    def body(x_ref, g_ref, dy_ref, wg_ref, wu_ref, wo_ref, dx_ref, dg_ref, dwg_ref, dwu_ref, dwo_ref, hs_ref):
        j = pl.program_id(0)

        @pl.when(j == 0)
        def _():
            for r in range(nr):
                rows = pl.ds(r * rt, rt)
                hs_ref[rows, :] = _rms(x_ref[rows, :], g_ref[...], RMS_EPS).astype(BF16)
                dx_ref[rows, :] = jnp.zeros((rt, d), F32)

        for r in range(nr):
            rows = pl.ds(r * rt, rt)
            h = hs_ref[rows, :]
            dy2 = (FFN_HALF * dy_ref[rows, :]).astype(BF16)
            a = _mm_nn(h, wg_ref[...])
            b = _mm_nn(h, wu_ref[...])
            sig = jax.nn.sigmoid(a)
            sa = a * sig
            ds = _mm_nt(dy2, wo_ref[...])
            da = ds * b * (sig * (1.0 + a * (1.0 - sig)))
            db = ds * sa
            dx_ref[rows, :] += _mm_nt(da, wg_ref[...]) + _mm_nt(db, wu_ref[...])
            dwo_c = _mm_tn(sa * b, dy2)
            dwg_c = _mm_tn(h, da)
            dwu_c = _mm_tn(h, db)
            if r == 0:
                dwo_ref[...] = dwo_c
                dwg_ref[...] = dwg_c
                dwu_ref[...] = dwu_c
            else:
                dwo_ref[...] += dwo_c
                dwg_ref[...] += dwg_c
                dwu_ref[...] += dwu_c

        @pl.when(j == nj - 1)
        def _():
            for r in range(nr):
                rows = pl.ds(r * rt, rt)
                _, vjp = jax.vjp(lambda xx, gg: _rms(xx, gg, RMS_EPS), x_ref[rows, :], g_ref[...])
                dxn, dgr = vjp(dx_ref[rows, :])
                dx_ref[rows, :] = dy_ref[rows, :] + dxn
                if r == 0:
                    dg_ref[...] = dgr
                else:
                    dg_ref[...] += dgr

    return pl.pallas_call(
        body, name="ffn_bwd", grid=(nj,),
        in_specs=[_const_spec((t, d), True), _const_spec((1, d)), _const_spec((t, d), True),
                  _layer_spec(layer, (d, fc), lambda j: (0, j)), _layer_spec(layer, (d, fc), lambda j: (0, nj + j)),
                  _layer_spec(layer, (fc, d), lambda j: (j, 0))],
        out_specs=[_const_spec((t, d)), _const_spec((1, d)),
                   pl.BlockSpec((d, fc), lambda j: (0, j)), pl.BlockSpec((d, fc), lambda j: (0, j)),
                   pl.BlockSpec((fc, d), lambda j: (j, 0))],
        out_shape=[jax.ShapeDtypeStruct((t, d), F32), jax.ShapeDtypeStruct((1, d), F32),
                   jax.ShapeDtypeStruct((d, f), F32), jax.ShapeDtypeStruct((d, f), F32),
                   jax.ShapeDtypeStruct((f, d), F32)],
        scratch_shapes=[pltpu.VMEM((t, d), BF16)],
        compiler_params=_params(dimension_semantics=("arbitrary",)),
    )(x, gain, dy, w_in, w_in, w_out)


def _proj_fwd(x, gain, w, wba, layer):
    t, d = x.shape
    n = w.shape[2]
    nc = _pick(n, (512, 256, 128))
    rt = _pick(t, (512, 256, 128))

    def body(x_ref, g_ref, w_ref, wba_ref, p_ref, ba_ref, hs_ref):
        @pl.when(pl.program_id(0) == 0)
        def _():
            for r in range(t // rt):
                rows = pl.ds(r * rt, rt)
                h = _rms(x_ref[rows, :], g_ref[...], RMS_EPS).astype(BF16)
                hs_ref[rows, :] = h
                ba_ref[rows, :] = _mm_nn(h, wba_ref[...])

        for r in range(t // rt):
            rows = pl.ds(r * rt, rt)
            p_ref[rows, :] = _mm_nn(hs_ref[rows, :], w_ref[...])

    return pl.pallas_call(
        body, name="proj_fwd", grid=(n // nc,),
        in_specs=[_const_spec((t, d), True), _const_spec((1, d)),
                  _layer_spec(layer, (d, nc), lambda j: (0, j)), _layer_spec(layer, (d, LANES), lambda j: (0, 0))],
        out_specs=[pl.BlockSpec((t, nc), lambda j: (0, j)), _const_spec((t, LANES))],
        out_shape=[jax.ShapeDtypeStruct((t, n), F32), jax.ShapeDtypeStruct((t, LANES), F32)],
        scratch_shapes=[pltpu.VMEM((t, d), BF16)],
        compiler_params=_params(dimension_semantics=("arbitrary",)),
    )(x, gain, w, wba)


def _proj_bwd(x, gain, dres, dp, dba, w, wba, layer):
    t, d = x.shape
    n = w.shape[2]
    nc = _pick(n, (512, 256, 128))
    nj = n // nc
    rt = _pick(t, (512, 256, 128))
    nr = t // rt

    def body(x_ref, g_ref, dres_ref, dp_ref, dba_ref, w_ref, wba_ref, dx_ref, dg_ref, dw_ref, dwba_ref, hs_ref):
        j = pl.program_id(0)

        @pl.when(j == 0)
        def _():
            for r in range(nr):
                rows = pl.ds(r * rt, rt)
                h = _rms(x_ref[rows, :], g_ref[...], RMS_EPS).astype(BF16)
                hs_ref[rows, :] = h
                g = dba_ref[rows, :]
                dx_ref[rows, :] = _mm_nt(g, wba_ref[...])
                if r == 0:
                    dwba_ref[...] = _mm_tn(h, g)
                else:
                    dwba_ref[...] += _mm_tn(h, g)

        for r in range(nr):
            rows = pl.ds(r * rt, rt)
            g = dp_ref[rows, :]
            dx_ref[rows, :] += _mm_nt(g, w_ref[...])
            if r == 0:
                dw_ref[...] = _mm_tn(hs_ref[rows, :], g)
            else:
                dw_ref[...] += _mm_tn(hs_ref[rows, :], g)

        @pl.when(j == nj - 1)
        def _():
            for r in range(nr):
                rows = pl.ds(r * rt, rt)
                _, vjp = jax.vjp(lambda xx, gg: _rms(xx, gg, RMS_EPS), x_ref[rows, :], g_ref[...])
                dxn, dgr = vjp(dx_ref[rows, :])
                dx_ref[rows, :] = dres_ref[rows, :] + dxn
                if r == 0:
                    dg_ref[...] = dgr
                else:
                    dg_ref[...] += dgr

    return pl.pallas_call(
        body, name="proj_bwd", grid=(nj,),
        in_specs=[_const_spec((t, d), True), _const_spec((1, d)), _const_spec((t, d), True),
                  pl.BlockSpec((t, nc), lambda j: (0, j)), _const_spec((t, LANES)),
                  _layer_spec(layer, (d, nc), lambda j: (0, j)), _layer_spec(layer, (d, LANES), lambda j: (0, 0))],
        out_specs=[_const_spec((t, d)), _const_spec((1, d)),
                   pl.BlockSpec((d, nc), lambda j: (0, j)), _const_spec((d, LANES))],
        out_shape=[jax.ShapeDtypeStruct((t, d), F32), jax.ShapeDtypeStruct((1, d), F32),
                   jax.ShapeDtypeStruct((d, n), F32), jax.ShapeDtypeStruct((d, LANES), F32)],
        scratch_shapes=[pltpu.VMEM((t, d), BF16)],
        compiler_params=_params(dimension_semantics=("arbitrary",)),
    )(x, gain, dres, dp, dba, w, wba)


def _conv_act(x, w0, w1, w2, w3, is_qk):
    y = w3 * x + w2 * _shift_down(x, 1) + w1 * _shift_down(x, 2) + w0 * _shift_down(x, 3)
    y = _silu(y)
    inv = lax.rsqrt(jnp.sum(y * y, axis=-1, keepdims=True) + L2_EPS)
    return y * (is_qk * inv + (1.0 - is_qk))


def _taps(w_ref):
    return tuple(w_ref[i:i + 1, :] for i in range(DN_CONV))


def _dn_prep_fwd(proj, conv_w, n_heads):
    t = proj.shape[0]
    nb = 3 * n_heads

    def body(x_ref, w_ref, o_ref):
        is_qk = jnp.where(pl.program_id(0) < 2 * n_heads, 1.0, 0.0).astype(F32)
        o_ref[...] = _conv_act(x_ref[...], *_taps(w_ref), is_qk)

    return pl.pallas_call(
        body, name="dn_prep_fwd", grid=(nb,),
        in_specs=[pl.BlockSpec((t, HEAD_DIM), lambda i: (0, i)), pl.BlockSpec((CONV_ROWS, HEAD_DIM), lambda i: (0, i))],
        out_specs=pl.BlockSpec((t, HEAD_DIM), lambda i: (0, i)),
        out_shape=jax.ShapeDtypeStruct((t, nb * HEAD_DIM), F32),
        compiler_params=_params(dimension_semantics=("arbitrary",)),
    )(proj, conv_w)


def _dn_prep_bwd(proj, conv_w, dact, n_heads):
    t = proj.shape[0]
    nb = 3 * n_heads

    def body(x_ref, w_ref, g_ref, dx_ref, dw_ref):
        is_qk = jnp.where(pl.program_id(0) < 2 * n_heads, 1.0, 0.0).astype(F32)
        _, vjp = jax.vjp(lambda x, a, b, c, e: _conv_act(x, a, b, c, e, is_qk), x_ref[...], *_taps(w_ref))
        dx, d0, d1, d2, d3 = vjp(g_ref[...])
        dx_ref[...] = dx.astype(BF16)
        dw_ref[...] = jnp.concatenate([d0, d1, d2, d3, jnp.zeros((CONV_ROWS - DN_CONV, HEAD_DIM), F32)], axis=0)

    return pl.pallas_call(
        body, name="dn_prep_bwd", grid=(nb,),
        in_specs=[pl.BlockSpec((t, HEAD_DIM), lambda i: (0, i)), pl.BlockSpec((CONV_ROWS, HEAD_DIM), lambda i: (0, i)),
                  pl.BlockSpec((t, HEAD_DIM), lambda i: (0, i))],
        out_specs=[pl.BlockSpec((t, HEAD_DIM), lambda i: (0, i)), pl.BlockSpec((CONV_ROWS, HEAD_DIM), lambda i: (0, i))],
        out_shape=[jax.ShapeDtypeStruct((t, nb * HEAD_DIM), BF16), jax.ShapeDtypeStruct((CONV_ROWS, nb * HEAD_DIM), F32)],
        compiler_params=_params(dimension_semantics=("arbitrary",)),
    )(proj, conv_w, dact)


def _unit_lower_inverses(lmats, c):
    r = lax.broadcasted_iota(jnp.int32, (c, c), 0)
    q = lax.broadcasted_iota(jnp.int32, (c, c), 1)
    eye = jnp.where(r == q, 1.0, 0.0)
    ps = [eye - l for l in lmats]
    ms = [_p_nn(l, l) for l in lmats]
    n = 2
    while True:
        ps = [p + _p_nn(p, m) for p, m in zip(ps, ms)]
        if 2 * n >= c:
            return ps
        ms = [_p_nn(m, m) for m in ms]
        n *= 2


def _delta_heads(qs, ks, vs, bg, alog, dtb, states):
    n_heads = len(qs)
    heads = range(n_heads)
    c = qs[0].shape[0]
    lane = lax.broadcasted_iota(jnp.int32, (c, LANES), 1)
    r = lax.broadcasted_iota(jnp.int32, (c, c), 0)
    s = lax.broadcasted_iota(jnp.int32, (c, c), 1)
    beta_all = jax.nn.sigmoid(bg)
    g_all = -jnp.exp(alog) * _softplus(bg + dtb)
    beta = [jnp.sum(jnp.where(lane == h, beta_all, 0.0), axis=1, keepdims=True) for h in heads]
    g = [jnp.sum(jnp.where(lane == n_heads + h, g_all, 0.0), axis=1, keepdims=True) for h in heads]
    g_row = [jnp.sum(jnp.where(r == s, g[h], 0.0), axis=0, keepdims=True) for h in heads]
    gc = [jnp.sum(jnp.where(s <= r, g_row[h], 0.0), axis=1, keepdims=True) for h in heads]
    gr = [jnp.sum(jnp.where(r <= s, g[h], 0.0), axis=0, keepdims=True) for h in heads]
    g_last = [jnp.sum(g[h], axis=0, keepdims=True) for h in heads]
    decay = [jnp.exp(jnp.where(r >= s, gc[h] - gr[h], NEG_BIG)) for h in heads]
    q_scaled = [qs[h] * (HEAD_DIM ** -0.5) for h in heads]
    k_beta = [ks[h] * beta[h] for h in heads]
    lmat = [jnp.where(r > s, _d_nt(k_beta[h], ks[h]) * decay[h], 0.0) for h in heads]
    attn = [_d_nt(q_scaled[h], ks[h]) * decay[h] for h in heads]
    tinv = _unit_lower_inverses(lmat, c)
    u = [_p_nn(tinv[h], vs[h] * beta[h]) for h in heads]
    w = [_p_nn(tinv[h], k_beta[h] * jnp.exp(gc[h])) for h in heads]
    v_new = [u[h] - _d_nn(w[h], states[h]) for h in heads]
    o_state = [_d_nn(q_scaled[h] * jnp.exp(gc[h]), states[h]) for h in heads]
    o = [o_state[h] + _d_nn(attn[h], v_new[h]) for h in heads]
    kv = [_d_tn(ks[h] * jnp.exp(g_last[h] - gc[h]), v_new[h]) for h in heads]
    new_states = [states[h] * jnp.exp(g_last[h]) + kv[h] for h in heads]
    return tuple(o), tuple(new_states)


def _delta_fwd(act, ba, alog, dtb, n_heads):
    t = act.shape[0]
    d = n_heads * HEAD_DIM
    c = DN_CHUNK
    nc = t // c

    def body(q_ref, k_ref, v_ref, bg_ref, al_ref, dt_ref, o_ref, snap_ref, st_ref):
        @pl.when(pl.program_id(0) == 0)
        def _():
            st_ref[...] = jnp.zeros(st_ref.shape, F32)

        snap_ref[0] = st_ref[...]
        cols = [slice(h * HEAD_DIM, (h + 1) * HEAD_DIM) for h in range(n_heads)]
        os, new_states = _delta_heads([q_ref[:, sl] for sl in cols], [k_ref[:, sl] for sl in cols],
                                      [v_ref[:, sl] for sl in cols], bg_ref[...], al_ref[...], dt_ref[...],
                                      [st_ref[h] for h in range(n_heads)])
        for h, sl in enumerate(cols):
            o_ref[:, sl] = os[h]
            st_ref[h] = new_states[h]

    return pl.pallas_call(
        body, name="delta_fwd", grid=(nc,),
        in_specs=[pl.BlockSpec((c, d), lambda i: (i, 0)), pl.BlockSpec((c, d), lambda i: (i, 1)),
                  pl.BlockSpec((c, d), lambda i: (i, 2)), pl.BlockSpec((c, LANES), lambda i: (i, 0)),
                  _const_spec((1, LANES)), _const_spec((1, LANES))],
        out_specs=[pl.BlockSpec((c, d), lambda i: (i, 0)),
                   pl.BlockSpec((1, n_heads, HEAD_DIM, HEAD_DIM), lambda i: (i, 0, 0, 0))],
        out_shape=[jax.ShapeDtypeStruct((t, d), F32), jax.ShapeDtypeStruct((nc, n_heads, HEAD_DIM, HEAD_DIM), F32)],
        scratch_shapes=[pltpu.VMEM((n_heads, HEAD_DIM, HEAD_DIM), F32)],
        compiler_params=_params(dimension_semantics=("arbitrary",)),
    )(act, act, act, ba, alog, dtb)


def _delta_bwd(act, ba, alog, dtb, snaps, do, n_heads):
    t = act.shape[0]
    d = n_heads * HEAD_DIM
    c = DN_CHUNK
    nc = t // c

    def body(q_ref, k_ref, v_ref, bg_ref, al_ref, dt_ref, snap_ref, do_ref,
             dact_ref, dbg_ref, dal_ref, ddt_ref, ds_ref):
        @pl.when(pl.program_id(0) == 0)
        def _():
            ds_ref[...] = jnp.zeros(ds_ref.shape, F32)
            dal_ref[...] = jnp.zeros((1, LANES), F32)
            ddt_ref[...] = jnp.zeros((1, LANES), F32)

        heads = range(n_heads)
        cols = [slice(h * HEAD_DIM, (h + 1) * HEAD_DIM) for h in heads]
        _, vjp = jax.vjp(_delta_heads, tuple(q_ref[:, sl] for sl in cols), tuple(k_ref[:, sl] for sl in cols),
                         tuple(v_ref[:, sl] for sl in cols), bg_ref[...], al_ref[...], dt_ref[...],
                         tuple(snap_ref[0, h] for h in heads))
        dq, dk, dv, dbg, dal, ddt, dst = vjp((tuple(do_ref[:, sl] for sl in cols), tuple(ds_ref[h] for h in heads)))
        for h, sl in enumerate(cols):
            dact_ref[:, sl] = dq[h]
            dact_ref[:, d + h * HEAD_DIM:d + (h + 1) * HEAD_DIM] = dk[h]
            dact_ref[:, 2 * d + h * HEAD_DIM:2 * d + (h + 1) * HEAD_DIM] = dv[h]
            ds_ref[h] = dst[h]
        dal_ref[...] += dal
        ddt_ref[...] += ddt
        dbg_ref[...] = dbg.astype(BF16)

    rev = lambda i: nc - 1 - i
    return pl.pallas_call(
        body, name="delta_bwd", grid=(nc,),
        in_specs=[pl.BlockSpec((c, d), lambda i: (rev(i), 0)), pl.BlockSpec((c, d), lambda i: (rev(i), 1)),
                  pl.BlockSpec((c, d), lambda i: (rev(i), 2)), pl.BlockSpec((c, LANES), lambda i: (rev(i), 0)),
                  _const_spec((1, LANES)), _const_spec((1, LANES)),
                  pl.BlockSpec((1, n_heads, HEAD_DIM, HEAD_DIM), lambda i: (rev(i), 0, 0, 0)),
                  pl.BlockSpec((c, d), lambda i: (rev(i), 0))],
        out_specs=[pl.BlockSpec((c, 3 * d), lambda i: (rev(i), 0)), pl.BlockSpec((c, LANES), lambda i: (rev(i), 0)),
                   _const_spec((1, LANES)), _const_spec((1, LANES))],
        out_shape=[jax.ShapeDtypeStruct((t, 3 * d), F32), jax.ShapeDtypeStruct((t, LANES), BF16),
                   jax.ShapeDtypeStruct((1, LANES), F32), jax.ShapeDtypeStruct((1, LANES), F32)],
        scratch_shapes=[pltpu.VMEM((n_heads, HEAD_DIM, HEAD_DIM), F32)],
        compiler_params=_params(dimension_semantics=("arbitrary",)),
    )(act, act, act, ba, alog, dtb, snaps, do)


def _head_norm2(a, b, ga, gb):
    return _rms(a, ga, RMS_EPS), _rms(b, gb, RMS_EPS)


def _sb_prep_fwd(proj, qn, kn, n_heads):
    t = proj.shape[0]
    d = n_heads * HEAD_DIM
    tm = _pick(t, (256, 128))

    def body(q_ref, k_ref, qn_ref, kn_ref, qo_ref, ko_ref):
        for h in range(n_heads):
            sl = slice(h * HEAD_DIM, (h + 1) * HEAD_DIM)
            qo_ref[:, sl], ko_ref[:, sl] = _head_norm2(q_ref[:, sl], k_ref[:, sl], qn_ref[...], kn_ref[...])

    return pl.pallas_call(
        body, name="sb_prep_fwd", grid=(t // tm,),
        in_specs=[pl.BlockSpec((tm, d), lambda i: (i, 4)), pl.BlockSpec((tm, d), lambda i: (i, 5)),
                  _const_spec((1, HEAD_DIM)), _const_spec((1, HEAD_DIM))],
        out_specs=[pl.BlockSpec((tm, d), lambda i: (i, 0)), pl.BlockSpec((tm, d), lambda i: (i, 0))],
        out_shape=[jax.ShapeDtypeStruct((t, d), F32), jax.ShapeDtypeStruct((t, d), F32)],
        compiler_params=_params(dimension_semantics=("arbitrary",)),
    )(proj, proj, qn, kn)


def _sb_prep_bwd(proj, qn, kn, dq, dk, n_heads):
    t = proj.shape[0]
    d = n_heads * HEAD_DIM
    tm = _pick(t, (256, 128))

    def body(q_ref, k_ref, qn_ref, kn_ref, dq_ref, dk_ref, dqo_ref, dko_ref, dqn_ref, dkn_ref):
        @pl.when(pl.program_id(0) == 0)
        def _():
            dqn_ref[...] = jnp.zeros((1, HEAD_DIM), F32)
            dkn_ref[...] = jnp.zeros((1, HEAD_DIM), F32)

        for h in range(n_heads):
            sl = slice(h * HEAD_DIM, (h + 1) * HEAD_DIM)
            _, vjp = jax.vjp(_head_norm2, q_ref[:, sl], k_ref[:, sl], qn_ref[...], kn_ref[...])
            da, db, dga, dgb = vjp((dq_ref[:, sl], dk_ref[:, sl]))
            dqo_ref[:, sl] = da.astype(BF16)
            dko_ref[:, sl] = db.astype(BF16)
            dqn_ref[...] += dga
            dkn_ref[...] += dgb

    return pl.pallas_call(
        body, name="sb_prep_bwd", grid=(t // tm,),
        in_specs=[pl.BlockSpec((tm, d), lambda i: (i, 4)), pl.BlockSpec((tm, d), lambda i: (i, 5)),
                  _const_spec((1, HEAD_DIM)), _const_spec((1, HEAD_DIM)),
                  pl.BlockSpec((tm, d), lambda i: (i, 0)), pl.BlockSpec((tm, d), lambda i: (i, 0))],
        out_specs=[pl.BlockSpec((tm, d), lambda i: (i, 0)), pl.BlockSpec((tm, d), lambda i: (i, 0)),
                   _const_spec((1, HEAD_DIM)), _const_spec((1, HEAD_DIM))],
        out_shape=[jax.ShapeDtypeStruct((t, d), BF16), jax.ShapeDtypeStruct((t, d), BF16),
                   jax.ShapeDtypeStruct((1, HEAD_DIM), F32), jax.ShapeDtypeStruct((1, HEAD_DIM), F32)],
        compiler_params=_params(dimension_semantics=("arbitrary",)),
    )(proj, proj, qn, kn, dq, dk)


def _cumsum_mm(x, tri):
    hi, lo = _split(x)
    return (lax.dot_general(hi, tri, (_NN, ((), ())), preferred_element_type=F32)
            + lax.dot_general(lo, tri, (_NN, ((), ())), preferred_element_type=F32))


def _sb_valid(i, j):
    row = lax.broadcasted_iota(jnp.int32, (SB_BLOCK, SB_BLOCK), 0)
    col = lax.broadcasted_iota(jnp.int32, (SB_BLOCK, SB_BLOCK), 1)
    return (col + j * SB_BLOCK) < (row + i * SB_BLOCK)


def _sb_attn_fwd(qb, kb, proj, n_heads):
    t = qb.shape[0]
    d = n_heads * HEAD_DIM
    nq = t // SB_BLOCK
    hb = _pick(n_heads, (4, 2, 1))
    wide = hb * HEAD_DIM
    v_col0 = 6 * n_heads // hb
    scale = HEAD_DIM ** -0.5

    def body(q_ref, k_ref, v_ref, o_ref, lt_ref):
        i = pl.program_id(1)
        row = lax.broadcasted_iota(jnp.int32, (SB_BLOCK, SB_BLOCK), 0)
        col = lax.broadcasted_iota(jnp.int32, (SB_BLOCK, SB_BLOCK), 1)
        after = jnp.where(row > col, 1.0, 0.0).astype(BF16)
        heads = [slice(h * HEAD_DIM, (h + 1) * HEAD_DIM) for h in range(hb)]
        every = range(hb)
        qs = [q_ref[:, sl].astype(BF16) for sl in heads]

        def step(jj, carry):
            j = i - jj
            rows = pl.ds(pl.multiple_of(j * SB_BLOCK, SB_BLOCK), SB_BLOCK)
            valid = _sb_valid(i, j)
            z = [_mm_nt(qs[h], k_ref[rows, heads[h]]) * scale for h in every]
            sp = [_softplus_raw(z[h]) for h in every]
            lm = [jnp.where(valid, -sp[h], 0.0) for h in every]
            tail = [_cumsum_mm(lm[h], after) for h in every]
            w = [jnp.where(valid, jnp.exp(z[h] - sp[h] + carry[h][1] + tail[h]), 0.0) for h in every]
            pv = [_mm_nn(w[h], v_ref[rows, heads[h]]) for h in every]
            return tuple((carry[h][0] + pv[h], carry[h][1] + jnp.sum(lm[h], axis=1, keepdims=True)) for h in every)

        init = tuple((jnp.zeros((SB_BLOCK, HEAD_DIM), F32), jnp.zeros((SB_BLOCK, 1), F32)) for _ in heads)
        res = lax.fori_loop(0, i + 1, step, init)
        for h, sl in enumerate(heads):
            o_ref[:, sl] = res[h][0]
            lt_ref[:, sl] = jnp.broadcast_to(res[h][1], (SB_BLOCK, HEAD_DIM))

    return pl.pallas_call(
        body, name="sb_attn_fwd", grid=(n_heads // hb, nq),
        in_specs=[pl.BlockSpec((SB_BLOCK, wide), lambda g, i: (i, g)),
                  pl.BlockSpec((t, wide), lambda g, i: (0, g)),
                  pl.BlockSpec((t, wide), lambda g, i: (0, v_col0 + g))],
        out_specs=[pl.BlockSpec((SB_BLOCK, wide), lambda g, i: (i, g)),
                   pl.BlockSpec((SB_BLOCK, wide), lambda g, i: (i, g))],
        out_shape=[jax.ShapeDtypeStruct((t, d), F32), jax.ShapeDtypeStruct((t, d), F32)],
        compiler_params=_params(dimension_semantics=("arbitrary", "arbitrary")),
    )(qb, kb, proj)


def _sb_attn_bwd(qb, kb, proj, ltot, do, n_heads):
    t = qb.shape[0]
    d = n_heads * HEAD_DIM
    nq = t // SB_BLOCK
    hb = _pick(n_heads, (4, 2, 1))
    wide = hb * HEAD_DIM
    v_col0 = 6 * n_heads // hb
    scale = HEAD_DIM ** -0.5

    def body(q_ref, k_ref, v_ref, lt_ref, do_ref, dq_ref, dk_ref, dv_ref):
        i = pl.program_id(1)

        @pl.when(i == 0)
        def _():
            dk_ref[...] = jnp.zeros((t, wide), F32)
            dv_ref[...] = jnp.zeros((t, wide), F32)

        row = lax.broadcasted_iota(jnp.int32, (SB_BLOCK, SB_BLOCK), 0)
        col = lax.broadcasted_iota(jnp.int32, (SB_BLOCK, SB_BLOCK), 1)
        upto = jnp.where(row <= col, 1.0, 0.0).astype(BF16)
        before = jnp.where(row < col, 1.0, 0.0).astype(BF16)
        heads = [slice(h * HEAD_DIM, (h + 1) * HEAD_DIM) for h in range(hb)]
        every = range(hb)
        qs = [q_ref[:, sl].astype(BF16) for sl in heads]
        dos = [do_ref[:, sl].astype(BF16) for sl in heads]
        totals = [jnp.max(lt_ref[:, sl], axis=1, keepdims=True) for sl in heads]

        def step(j, carry):
            rows = pl.ds(pl.multiple_of(j * SB_BLOCK, SB_BLOCK), SB_BLOCK)
            valid = _sb_valid(i, j)
            kj = [k_ref[rows, heads[h]].astype(BF16) for h in every]
            vj = [v_ref[rows, heads[h]].astype(BF16) for h in every]
            z = [_mm_nt(qs[h], kj[h]) * scale for h in every]
            dw = [_mm_nt(dos[h], vj[h]) for h in every]
            sp = [_softplus_raw(z[h]) for h in every]
            lm = [jnp.where(valid, -sp[h], 0.0) for h in every]
            head = [_cumsum_mm(lm[h], upto) for h in every]
            w = [jnp.where(valid, jnp.exp(z[h] - sp[h] + totals[h] - (carry[h][1] + head[h])), 0.0) for h in every]
            e = [w[h] * dw[h] for h in every]
            e_pre = [carry[h][2] + _cumsum_mm(e[h], before) for h in every]
            sig = [jnp.exp(z[h] - sp[h]) for h in every]
            dz = [jnp.where(valid, e[h] * (1.0 - sig[h]) - e_pre[h] * sig[h], 0.0) * scale for h in every]
            for h in every:
                dv_ref[rows, heads[h]] += _mm_tn(w[h], dos[h])
            for h in every:
                dk_ref[rows, heads[h]] += _mm_tn(dz[h], qs[h])
            dq = [_mm_nn(dz[h], kj[h]) for h in every]
            return tuple((carry[h][0] + dq[h], carry[h][1] + jnp.sum(lm[h], axis=1, keepdims=True),
                          carry[h][2] + jnp.sum(e[h], axis=1, keepdims=True)) for h in every)

        zero_col = jnp.zeros((SB_BLOCK, 1), F32)
        init = tuple((jnp.zeros((SB_BLOCK, HEAD_DIM), F32), zero_col, zero_col) for _ in heads)
        res = lax.fori_loop(0, i + 1, step, init)
        for h, sl in enumerate(heads):
            dq_ref[:, sl] = res[h][0]

    return pl.pallas_call(
        body, name="sb_attn_bwd", grid=(n_heads // hb, nq),
        in_specs=[pl.BlockSpec((SB_BLOCK, wide), lambda g, i: (i, g)),
                  pl.BlockSpec((t, wide), lambda g, i: (0, g)),
                  pl.BlockSpec((t, wide), lambda g, i: (0, v_col0 + g)),
                  pl.BlockSpec((SB_BLOCK, wide), lambda g, i: (i, g)),
                  pl.BlockSpec((SB_BLOCK, wide), lambda g, i: (i, g))],
        out_specs=[pl.BlockSpec((SB_BLOCK, wide), lambda g, i: (i, g)),
                   pl.BlockSpec((t, wide), lambda g, i: (0, g)),
                   pl.BlockSpec((t, wide), lambda g, i: (0, g))],
        out_shape=[jax.ShapeDtypeStruct((t, d), F32), jax.ShapeDtypeStruct((t, d), F32),
                   jax.ShapeDtypeStruct((t, d), F32)],
        compiler_params=_params(dimension_semantics=("arbitrary", "arbitrary")),
    )(qb, kb, proj, ltot, do)


def _gated_norm(oa, z, gn):
    return _rms(oa, gn, RMS_EPS) * _silu(z)


def _merge_gates(ya, yb, ga, gb):
    return jax.nn.sigmoid(ga) * ya + jax.nn.sigmoid(gb) * yb


def _merge_fwd(x1, oa, proj, ob, gn, wa, wb, wo, layer, n_heads):
    t, d = x1.shape
    tm = _pick(t, (256, 128))
    square = _layer_spec(layer, (d, d), lambda i: (0, 0), single=True)

    def body(x_ref, oa_ref, z_ref, ob_ref, ga_ref, gb_ref, gn_ref, wa_ref, wb_ref, wo_ref, o_ref, na_ref):
        for h in range(n_heads):
            sl = slice(h * HEAD_DIM, (h + 1) * HEAD_DIM)
            na_ref[:, sl] = _gated_norm(oa_ref[:, sl], z_ref[:, sl], gn_ref[...]).astype(BF16)
        m = _merge_gates(_mm_nn(na_ref[...], wa_ref[...]), _mm_nn(ob_ref[...], wb_ref[...]), ga_ref[...], gb_ref[...])
        o_ref[...] = x_ref[...] + _mm_nn(m, wo_ref[...])

    tile = lambda k: pl.BlockSpec((tm, d), lambda i: (i, k))
    return pl.pallas_call(
        body, name="merge_fwd", grid=(t // tm,),
        in_specs=[tile(0), tile(0), tile(3), tile(0), tile(7), tile(8), _const_spec((1, HEAD_DIM)),
                  square, square, square],
        out_specs=tile(0),
        out_shape=jax.ShapeDtypeStruct((t, d), F32),
        scratch_shapes=[pltpu.VMEM((tm, d), BF16)],
        compiler_params=_params(dimension_semantics=("arbitrary",)),
    )(x1, oa, proj, ob, proj, proj, gn, wa, wb, wo)


def _merge_bwd(oa, proj, ob, dy, gn, wa, wb, wo, layer, n_heads):
    t, d = oa.shape
    tm = _pick(t, (256, 128))
    nt = t // tm
    square = _layer_spec(layer, (d, d), lambda i: (0, 0), single=True)

    def body(oa_ref, z_ref, ob_ref, ga_ref, gb_ref, dy_ref, gn_ref, wa_ref, wb_ref, wo_ref,
             doa_ref, dz_ref, dob_ref, dga_ref, dgb_ref, dgn_ref, dwa_hbm, dwb_hbm, dwo_hbm,
             na_ref, dna_ref, dwa_ref, dwb_ref, dwo_ref):
        i = pl.program_id(0)

        @pl.when(i == 0)
        def _():
            dgn_ref[...] = jnp.zeros((1, HEAD_DIM), F32)
            dwa_ref[...] = jnp.zeros((d, d), F32)
            dwb_ref[...] = jnp.zeros((d, d), F32)
            dwo_ref[...] = jnp.zeros((d, d), F32)

        for h in range(n_heads):
            sl = slice(h * HEAD_DIM, (h + 1) * HEAD_DIM)
            na_ref[:, sl] = _gated_norm(oa_ref[:, sl], z_ref[:, sl], gn_ref[...]).astype(BF16)
        dy = dy_ref[...].astype(BF16)
        ob = ob_ref[...].astype(BF16)
        ya = _mm_nn(na_ref[...], wa_ref[...])
        yb = _mm_nn(ob, wb_ref[...])
        m, vjp = jax.vjp(_merge_gates, ya, yb, ga_ref[...], gb_ref[...])
        dwo_ref[...] += _mm_tn(m, dy)
        dya, dyb, dga, dgb = vjp(_mm_nt(dy, wo_ref[...]))
        dga_ref[...] = dga.astype(BF16)
        dgb_ref[...] = dgb.astype(BF16)
        dwa_ref[...] += _mm_tn(na_ref[...], dya)
        dwb_ref[...] += _mm_tn(ob, dyb)
        dob_ref[...] = _mm_nt(dyb, wb_ref[...])
        dna_ref[...] = _mm_nt(dya, wa_ref[...])
        for h in range(n_heads):
            sl = slice(h * HEAD_DIM, (h + 1) * HEAD_DIM)
            _, vjp_h = jax.vjp(_gated_norm, oa_ref[:, sl], z_ref[:, sl], gn_ref[...])
            doa, dz, dgn = vjp_h(dna_ref[:, sl])
            doa_ref[:, sl] = doa
            dz_ref[:, sl] = dz.astype(BF16)
            dgn_ref[...] += dgn

        @pl.when(i == nt - 1)
        def _():
            pltpu.sync_copy(dwa_ref, dwa_hbm)
            pltpu.sync_copy(dwb_ref, dwb_hbm)
            pltpu.sync_copy(dwo_ref, dwo_hbm)

    tile = lambda k: pl.BlockSpec((tm, d), lambda i: (i, k))
    any_spec = pl.BlockSpec(memory_space=pl.ANY)
    return pl.pallas_call(
        body, name="merge_bwd", grid=(nt,),
        in_specs=[tile(0), tile(3), tile(0), tile(7), tile(8), tile(0), _const_spec((1, HEAD_DIM)),
                  square, square, square],
        out_specs=[tile(0), tile(0), tile(0), tile(0), tile(0), _const_spec((1, HEAD_DIM)),
                   any_spec, any_spec, any_spec],
        out_shape=[jax.ShapeDtypeStruct((t, d), F32), jax.ShapeDtypeStruct((t, d), BF16),
                   jax.ShapeDtypeStruct((t, d), F32), jax.ShapeDtypeStruct((t, d), BF16),
                   jax.ShapeDtypeStruct((t, d), BF16), jax.ShapeDtypeStruct((1, HEAD_DIM), F32),
                   jax.ShapeDtypeStruct((d, d), F32), jax.ShapeDtypeStruct((d, d), F32),
                   jax.ShapeDtypeStruct((d, d), F32)],
        scratch_shapes=[pltpu.VMEM((tm, d), BF16), pltpu.VMEM((tm, d), F32),
                        pltpu.VMEM((d, d), F32), pltpu.VMEM((d, d), F32), pltpu.VMEM((d, d), F32)],
        compiler_params=_params(dimension_semantics=("arbitrary",)),
    )(oa, proj, ob, proj, proj, dy, gn, wa, wb, wo)


def _loss_head(y, target):
    t, d = y.shape
    tm = _pick(t, (256, 128))

    def body(y_ref, t_ref, dy_ref, loss_ref):
        @pl.when(pl.program_id(0) == 0)
        def _():
            loss_ref[...] = jnp.zeros((8, LANES), F32)

        err = y_ref[...] - t_ref[...]
        dy_ref[...] = err * (1.0 / d)
        per_token = jnp.sum(err * err, axis=1, keepdims=True) * (1.0 / d)
        loss_ref[...] += 0.5 * jnp.sum(per_token, axis=0, keepdims=True)

    return pl.pallas_call(
        body, name="loss_head", grid=(t // tm,),
        in_specs=[pl.BlockSpec((tm, d), lambda i: (i, 0)), pl.BlockSpec((tm, d), lambda i: (i, 0))],
        out_specs=[pl.BlockSpec((tm, d), lambda i: (i, 0)), _const_spec((8, LANES))],
        out_shape=[jax.ShapeDtypeStruct((t, d), F32), jax.ShapeDtypeStruct((8, LANES), F32)],
        compiler_params=_params(dimension_semantics=("arbitrary",)),
    )(y, target)


def _adamw(w, g, m, v):
    rows, cols = w.shape
    tr = rows
    for cand in (512, 256, 128, 64, 32, 16, 8):
        if rows % cand == 0 and cand * cols * 4 <= 2 * 1024 * 1024:
            tr = cand
            break

    def body(w_ref, g_ref, m_ref, v_ref, d_ref, mo_ref, vo_ref):
        g = g_ref[...]
        m2 = ADAM_B1 * m_ref[...] + (1.0 - ADAM_B1) * g
        v2 = ADAM_B2 * v_ref[...] + (1.0 - ADAM_B2) * (g * g)
        m_hat = m2 / (1.0 - ADAM_B1 ** ADAM_STEP)
        v_hat = v2 / (1.0 - ADAM_B2 ** ADAM_STEP)
        d_ref[...] = -ADAM_LR * (m_hat / (jnp.sqrt(v_hat) + ADAM_EPS) + ADAM_WD * w_ref[...])
        mo_ref[...] = m2
        vo_ref[...] = v2

    spec = pl.BlockSpec((tr, cols), lambda i: (i, 0))
    shape = jax.ShapeDtypeStruct((rows, cols), F32)
    return pl.pallas_call(
        body, name="adamw", grid=(rows // tr,), in_specs=[spec] * 4, out_specs=[spec] * 3,
        out_shape=[shape] * 3, compiler_params=_params(dimension_semantics=("arbitrary",)),
    )(w, g, m, v)


def _place():
    return lax.axis_index("x"), lax.axis_index("y"), lax.axis_index("c")


def _other_chips(x, y):
    return [(1 - x, y), (x, 1 - y), (1 - x, 1 - y)]


def _tile_rows(rows, cols, itemsize, cap=1536 * 1024):
    best = None
    for cand in range(16, rows + 1, 16):
        if rows % cand == 0 and cand * cols * itemsize <= cap:
            best = cand
    return best if best is not None else rows


def _allgather_layer(shards, layer, collective_id):
    n = len(shards)

    def body(*refs):
        srcs, outs = refs[:n], refs[n:2 * n]
        send_sems, recv_sems = refs[2 * n:]
        x, y, c = _place()
        me, sibling = (x, y, c), (x, y, 1 - c)
        chips = _other_chips(x, y)
        barrier = pltpu.get_barrier_semaphore()
        for peer in [(*chip, c) for chip in chips] + [sibling]:
            pl.semaphore_signal(barrier, inc=1, device_id=peer, device_id_type=MESH)
        pl.semaphore_wait(barrier, N_CHIPS)

        def half(w, which):
            rows = shards[w].shape[1] // 2
            return pl.ds(which * rows, rows)

        def copy(w, k, shard, which, to, from_src=False):
            part = half(w, which)
            return pltpu.make_async_remote_copy(
                src_ref=srcs[w].at[layer, part] if from_src else outs[w].at[shard, part],
                dst_ref=outs[w].at[shard, part], send_sem=send_sems.at[6 * w + k], recv_sem=recv_sems.at[6 * w + k],
                device_id=to, device_id_type=MESH)

        first = [copy(w, j, 2 * x + y, c, (*chip, c), from_src=True) for j, chip in enumerate(chips) for w in range(n)]
        for cp in first:
            cp.start()
        passed = []
        for j, (cx, cy) in enumerate(chips):
            for w in range(n):
                copy(w, j, 2 * cx + cy, c, me).wait_recv()
                cp = copy(w, 3 + j, 2 * cx + cy, c, sibling)
                cp.start()
                passed.append(cp)
        for j, (cx, cy) in enumerate(chips):
            for w in range(n):
                copy(w, 3 + j, 2 * cx + cy, 1 - c, me).wait_recv()
        for cp in first + passed:
            cp.wait_send()

    return pl.kernel(
        body, name=f"allgather_layer{layer}",
        out_type=[jax.ShapeDtypeStruct((N_CHIPS,) + s.shape[1:], s.dtype) for s in shards],
        mesh=plsc.ScalarSubcoreMesh(axis_name="sequencer", num_cores=1),
        scratch_types=[pltpu.SemaphoreType.DMA((6 * n,)), pltpu.SemaphoreType.DMA((6 * n,))],
        compiler_params=pltpu.CompilerParams(collective_id=collective_id),
    )(*shards)


def _swap_halves(grads):
    n = len(grads)

    def body(*refs):
        gs, gots = refs[:n], refs[n:2 * n]
        send_sems, recv_sems = refs[2 * n:]
        x, y, c = _place()
        copies = []
        for w in range(n):
            half = grads[w].shape[1] // 2
            copies.append(pltpu.make_async_remote_copy(
                src_ref=gs[w].at[:, pl.ds((1 - c) * half, half)], dst_ref=gots[w], send_sem=send_sems.at[w],
                recv_sem=recv_sems.at[w], device_id=(x, y, 1 - c), device_id_type=MESH))
        for cp in copies:
            cp.start()
        for cp in copies:
            cp.wait()

    hbm = pl.BlockSpec(memory_space=pl.ANY)
    return pl.pallas_call(
        body, name="swap_halves", in_specs=[hbm] * n, out_specs=[hbm] * n,
        out_shape=[jax.ShapeDtypeStruct((g.shape[0], g.shape[1] // 2, g.shape[2]), g.dtype) for g in grads],
        scratch_shapes=[pltpu.SemaphoreType.DMA((n,)), pltpu.SemaphoreType.DMA((n,))],
    )(*grads)


def _add_half(grad, got, c_idx):
    n, rows, cols = grad.shape
    half = rows // 2
    tr = _tile_rows(half, cols, 2)
    nb = half // tr

    def body(c_ref, a_ref, b_ref, o_ref):
        o_ref[...] = (a_ref[...].astype(F32) + b_ref[...].astype(F32)).astype(o_ref.dtype)

    return pl.pallas_call(
        body, name="add_half",
        grid_spec=pltpu.PrefetchScalarGridSpec(
            num_scalar_prefetch=1, grid=(n, nb),
            in_specs=[pl.BlockSpec((1, tr, cols), lambda s, r, c_ref: (s, c_ref[0] * nb + r, 0)),
                      pl.BlockSpec((1, tr, cols), lambda s, r, c_ref: (s, r, 0))],
            out_specs=pl.BlockSpec((1, tr, cols), lambda s, r, c_ref: (s, r, 0))),
        out_shape=jax.ShapeDtypeStruct((n, half, cols), grad.dtype),
        compiler_params=_params(dimension_semantics=("arbitrary", "arbitrary")),
    )(c_idx, grad, got)


def _scatter_partials(parts, layer, collective_id):
    n = len(parts)

    def body(*refs):
        ps, gots = refs[:n], refs[n:2 * n]
        send_sems, recv_sems = refs[2 * n:]
        x, y, c = _place()
        chips = _other_chips(x, y)
        barrier = pltpu.get_barrier_semaphore()
        for chip in chips:
            pl.semaphore_signal(barrier, inc=1, device_id=(*chip, c), device_id_type=MESH)
        pl.semaphore_wait(barrier, N_CHIPS - 1)
        copies = [pltpu.make_async_remote_copy(src_ref=ps[w].at[2 * cx + cy], dst_ref=gots[w].at[j],
                                               send_sem=send_sems.at[3 * w + j], recv_sem=recv_sems.at[3 * w + j],
                                               device_id=(cx, cy, c), device_id_type=MESH)
                  for j, (cx, cy) in enumerate(chips) for w in range(n)]
        for cp in copies:
            cp.start()
        for cp in copies:
            cp.wait()

    return pl.kernel(
        body, name=f"scatter_partials{layer}",
        out_type=[jax.ShapeDtypeStruct((N_CHIPS - 1,) + p.shape[1:], p.dtype) for p in parts],
        mesh=plsc.ScalarSubcoreMesh(axis_name="sequencer", num_cores=1),
        scratch_types=[pltpu.SemaphoreType.DMA((3 * n,)), pltpu.SemaphoreType.DMA((3 * n,))],
        compiler_params=pltpu.CompilerParams(collective_id=collective_id),
    )(*parts)


def _sum_partials(part, got, s_idx, c_idx):
    n, half, cols = part.shape
    tr = _tile_rows(half, cols, 2, cap=1024 * 1024)
    nb = half // tr

    def body(s_ref, c_ref, a_ref, b_ref, o_ref):
        acc = a_ref[0].astype(F32)
        for j in range(n - 1):
            acc = acc + b_ref[j].astype(F32)
        o_ref[...] = acc

    return pl.pallas_call(
        body, name="sum_partials",
        grid_spec=pltpu.PrefetchScalarGridSpec(
            num_scalar_prefetch=2, grid=(nb,),
            in_specs=[pl.BlockSpec((1, tr, cols), lambda r, s_ref, c_ref: (s_ref[0], r, 0)),
                      pl.BlockSpec((n - 1, tr, cols), lambda r, s_ref, c_ref: (0, r, 0))],
            out_specs=pl.BlockSpec((tr, cols), lambda r, s_ref, c_ref: (c_ref[0] * nb + r, 0))),
        out_shape=jax.ShapeDtypeStruct((2 * half, cols), F32),
        compiler_params=_params(dimension_semantics=("arbitrary",)),
    )(s_idx, c_idx, part, got)


def _join_halves(bufs):
    n = len(bufs)

    def body(*refs):
        outs = refs[n:2 * n]
        send_sems, recv_sems = refs[2 * n:]
        x, y, c = _place()
        copies = []
        for w in range(n):
            half = bufs[w].shape[0] // 2
            mine = outs[w].at[pl.ds(c * half, half)]
            copies.append(pltpu.make_async_remote_copy(src_ref=mine, dst_ref=mine, send_sem=send_sems.at[w],
                                                       recv_sem=recv_sems.at[w], device_id=(x, y, 1 - c),
                                                       device_id_type=MESH))
        for cp in copies:
            cp.start()
        for cp in copies:
            cp.wait()

    hbm = pl.BlockSpec(memory_space=pl.ANY)
    return pl.pallas_call(
        body, name="join_halves", in_specs=[hbm] * n, out_specs=[hbm] * n,
        out_shape=[jax.ShapeDtypeStruct(b.shape, b.dtype) for b in bufs],
        input_output_aliases={w: w for w in range(n)},
        scratch_shapes=[pltpu.SemaphoreType.DMA((n,)), pltpu.SemaphoreType.DMA((n,))],
    )(*bufs)


def _allreduce_small(v, name):
    rows = v.shape[0]

    def body(v_ref, o_ref, gath, send_sems, recv_sems):
        x, y, c = _place()
        idx = 4 * x + 2 * y + c
        gath[0] = v_ref[...]
        copies = []
        for r in range(1, N_DEV):
            peer = (1 - x if r & 4 else x, 1 - y if r & 2 else y, 1 - c if r & 1 else c)
            cp = pltpu.make_async_remote_copy(src_ref=v_ref, dst_ref=gath.at[r], send_sem=send_sems.at[r - 1],
                                              recv_sem=recv_sems.at[r - 1], device_id=peer, device_id_type=MESH)
            cp.start()
            copies.append(cp)
        for cp in copies:
            cp.wait()
        acc = gath[idx]
        for a in range(1, N_DEV):
            acc = acc + gath[lax.bitwise_xor(idx, a)]
        o_ref[...] = acc

    vmem = pl.BlockSpec(memory_space=pltpu.VMEM)
    return pl.pallas_call(
        body, name=name, in_specs=[vmem], out_specs=vmem,
        out_shape=jax.ShapeDtypeStruct((rows, LANES), F32),
        scratch_shapes=[pltpu.VMEM((N_DEV, rows, LANES), F32), pltpu.SemaphoreType.DMA((N_DEV - 1,)),
                        pltpu.SemaphoreType.DMA((N_DEV - 1,))],
    )(v)


def _join_shards(name, gathered):
    return jnp.concatenate([gathered[s] for s in range(N_CHIPS)], axis=1 if name in COL_SHARDED else 0)[None]


def _split_shards(name, g):
    def shard(s):
        if name in COL_SHARDED:
            width = g.shape[1] // N_CHIPS
            return g[:, s * width:(s + 1) * width]
        height = g.shape[0] // N_CHIPS
        return g[s * height:(s + 1) * height, :]

    return jnp.stack([shard(s).astype(BF16) for s in range(N_CHIPS)])


def _pad_small(flat):
    n = flat.shape[0]
    block = 8 * LANES
    padded = -(-n // block) * block
    return jnp.pad(flat, (0, padded - n)).reshape(padded // LANES, LANES)


def kernel(x, ffn1_norm, ffn1_w_in, ffn1_w_out, mix_norm, w_in, dn_conv_w, dn_a_log, dn_dt_bias, dn_out_norm, sb_q_norm, sb_k_norm, w_branch_a, w_branch_b, w_out, ffn2_norm, ffn2_w_in, ffn2_w_out, loss_target, m_ffn1_norm, m_ffn1_w_in, m_ffn1_w_out, m_mix_norm, m_w_in, m_dn_conv_w, m_dn_a_log, m_dn_dt_bias, m_dn_out_norm, m_sb_q_norm, m_sb_k_norm, m_w_branch_a, m_w_branch_b, m_w_out, m_ffn2_norm, m_ffn2_w_in, m_ffn2_w_out, v_ffn1_norm, v_ffn1_w_in, v_ffn1_w_out, v_mix_norm, v_w_in, v_dn_conv_w, v_dn_a_log, v_dn_dt_bias, v_dn_out_norm, v_sb_q_norm, v_sb_k_norm, v_w_branch_a, v_w_branch_b, v_w_out, v_ffn2_norm, v_ffn2_w_in, v_ffn2_w_out):
    w = dict(ffn1_norm=ffn1_norm, ffn1_w_in=ffn1_w_in, ffn1_w_out=ffn1_w_out, mix_norm=mix_norm, w_in=w_in,
             dn_conv_w=dn_conv_w, dn_a_log=dn_a_log, dn_dt_bias=dn_dt_bias, dn_out_norm=dn_out_norm,
             sb_q_norm=sb_q_norm, sb_k_norm=sb_k_norm, w_branch_a=w_branch_a, w_branch_b=w_branch_b, w_out=w_out,
             ffn2_norm=ffn2_norm, ffn2_w_in=ffn2_w_in, ffn2_w_out=ffn2_w_out)
    mom = dict(ffn1_norm=m_ffn1_norm, ffn1_w_in=m_ffn1_w_in, ffn1_w_out=m_ffn1_w_out, mix_norm=m_mix_norm, w_in=m_w_in,
               dn_conv_w=m_dn_conv_w, dn_a_log=m_dn_a_log, dn_dt_bias=m_dn_dt_bias, dn_out_norm=m_dn_out_norm,
               sb_q_norm=m_sb_q_norm, sb_k_norm=m_sb_k_norm, w_branch_a=m_w_branch_a, w_branch_b=m_w_branch_b,
               w_out=m_w_out, ffn2_norm=m_ffn2_norm, ffn2_w_in=m_ffn2_w_in, ffn2_w_out=m_ffn2_w_out)
    var = dict(ffn1_norm=v_ffn1_norm, ffn1_w_in=v_ffn1_w_in, ffn1_w_out=v_ffn1_w_out, mix_norm=v_mix_norm, w_in=v_w_in,
               dn_conv_w=v_dn_conv_w, dn_a_log=v_dn_a_log, dn_dt_bias=v_dn_dt_bias, dn_out_norm=v_dn_out_norm,
               sb_q_norm=v_sb_q_norm, sb_k_norm=v_sb_k_norm, w_branch_a=v_w_branch_a, w_branch_b=v_w_branch_b,
               w_out=v_w_out, ffn2_norm=v_ffn2_norm, ffn2_w_in=v_ffn2_w_in, ffn2_w_out=v_ffn2_w_out)

    _, t, d = x.shape
    depth = ffn1_norm.shape[0]
    n_heads = d // HEAD_DIM
    conv_cols = dn_conv_w.shape[2]
    assert d % HEAD_DIM == 0 and t % SB_BLOCK == 0 and 2 * n_heads <= LANES and depth % 2 == 0
    assert w_in.shape[2] * N_CHIPS == 9 * d + 2 * n_heads and conv_cols * N_CHIPS == 3 * d

    x_idx, y_idx, c_idx = _place()
    shard = 2 * x_idx + y_idx
    c_arr = jnp.reshape(c_idx, (1,)).astype(jnp.int32)
    s_arr = jnp.reshape(shard, (1,)).astype(jnp.int32)

    mine = [w[n].astype(BF16) for n in BIG]
    cut = 4 * d
    arriving = [_allgather_layer(mine, l, collective_id=l) for l in range(depth)]

    def layer_weights(l, after):
        gathered, after = lax.optimization_barrier((arriving[l], after))
        gathered = [lax.dynamic_update_slice(g, m[l][None], (shard, 0, 0)) for g, m in zip(gathered, mine)]
        full = {n: _join_shards(n, g) for n, g in zip(BIG, gathered)}
        full["w_main"] = jnp.concatenate([full["w_in"][..., :cut], full["w_in"][..., cut + 2 * n_heads:]], axis=-1)
        full["w_ba"] = jnp.pad(full["w_in"][..., cut:cut + 2 * n_heads], ((0, 0), (0, 0), (0, LANES - 2 * n_heads)))
        return full, after

    conv_place = lax.dynamic_update_slice(jnp.zeros((depth, DN_CONV, 3 * d), F32), dn_conv_w, (0, 0, shard * conv_cols))
    conv_rows = _pad_small(conv_place.reshape(-1))
    conv_full = (0.5 * _allreduce_small(conv_rows, "allgather_conv")).reshape(-1)[:depth * DN_CONV * 3 * d]
    conv_full = jnp.pad(conv_full.reshape(depth, DN_CONV, 3 * d), ((0, 0), (0, CONV_ROWS - DN_CONV), (0, 0)))

    def head_row(vals):
        return jnp.pad(vals, (n_heads, LANES - 2 * n_heads)).reshape(1, LANES)

    saved, layers = [], []
    cur = x[0]
    for l in range(depth):
        full, x0 = layer_weights(l, cur)
        layers.append(full)
        x1 = _ffn_fwd(x0, ffn1_norm[l][None], full["ffn1_w_in"], full["ffn1_w_out"], 0)
        proj, ba = _proj_fwd(x1, mix_norm[l][None], full["w_main"], full["w_ba"], 0)
        act = _dn_prep_fwd(proj, conv_full[l], n_heads)
        alog, dtb = head_row(dn_a_log[l]), head_row(dn_dt_bias[l])
        oa, snaps = _delta_fwd(act, ba, alog, dtb, n_heads)
        qb, kb = _sb_prep_fwd(proj, sb_q_norm[l][None], sb_k_norm[l][None], n_heads)
        ob, ltot = _sb_attn_fwd(qb, kb, proj, n_heads)
        x2 = _merge_fwd(x1, oa, proj, ob, dn_out_norm[l][None], full["w_branch_a"], full["w_branch_b"],
                        full["w_out"], 0, n_heads)
        cur = _ffn_fwd(x2, ffn2_norm[l][None], full["ffn2_w_in"], full["ffn2_w_out"], 0)
        saved.append((x0, x1, proj, ba, act, alog, dtb, oa, snaps, qb, kb, ob, ltot, x2))

    dcur, loss_part = _loss_head(cur, loss_target[0])

    grads = {n: [None] * depth for n in WEIGHTS}
    reduced = {n: [None] * depth for n in BIG}
    pending = None

    def finish_reduce(l, parts, arrived, after):
        arrived, after = lax.optimization_barrier((arrived, after))
        halves = [_sum_partials(p, got, s_arr, c_arr) for p, got in zip(parts, arrived)]
        for n, g in zip(BIG, _join_halves(halves)):
            reduced[n][l] = g
        return after

    for l in reversed(range(depth)):
        x0, x1, proj, ba, act, alog, dtb, oa, snaps, qb, kb, ob, ltot, x2 = saved[l]
        full = layers[l]
        dx2, dg, dwg, dwu, dwo = _ffn_bwd(x2, ffn2_norm[l][None], dcur, full["ffn2_w_in"], full["ffn2_w_out"], 0)
        grads["ffn2_norm"][l] = dg[0]
        grads["ffn2_w_in"][l] = jnp.concatenate([dwg, dwu], axis=1)
        grads["ffn2_w_out"][l] = dwo
        doa, dz, dob, dga, dgb, dgn, dwa, dwb, dwout = _merge_bwd(
            oa, proj, ob, dx2, dn_out_norm[l][None], full["w_branch_a"], full["w_branch_b"], full["w_out"], 0,
            n_heads)
        grads["dn_out_norm"][l] = dgn[0]
        grads["w_branch_a"][l], grads["w_branch_b"][l], grads["w_out"][l] = dwa, dwb, dwout
        dqb, dkb, dvb = _sb_attn_bwd(qb, kb, proj, ltot, dob, n_heads)
        dsq, dsk, dqn, dkn = _sb_prep_bwd(proj, sb_q_norm[l][None], sb_k_norm[l][None], dqb, dkb, n_heads)
        grads["sb_q_norm"][l], grads["sb_k_norm"][l] = dqn[0], dkn[0]
        dact, dba, dal, ddt = _delta_bwd(act, ba, alog, dtb, snaps, doa, n_heads)
        grads["dn_a_log"][l] = dal[0, n_heads:2 * n_heads]
        grads["dn_dt_bias"][l] = ddt[0, n_heads:2 * n_heads]
        dqkv, dconv = _dn_prep_bwd(proj, conv_full[l], dact, n_heads)
        grads["dn_conv_w"][l] = dconv[:DN_CONV]
        dproj = jnp.concatenate([dqkv, dz, dsq, dsk, dvb.astype(BF16), dga, dgb], axis=1)
        dx1, dg, dwm, dwba = _proj_bwd(x1, mix_norm[l][None], dx2, dproj, dba, full["w_main"], full["w_ba"], 0)
        grads["mix_norm"][l] = dg[0]
        grads["w_in"][l] = jnp.concatenate([dwm[:, :cut], dwba[:, :2 * n_heads], dwm[:, cut:]], axis=1)
        dcur, dg, dwg, dwu, dwo = _ffn_bwd(x0, ffn1_norm[l][None], dx1, full["ffn1_w_in"], full["ffn1_w_out"], 0)
        grads["ffn1_norm"][l] = dg[0]
        grads["ffn1_w_in"][l] = jnp.concatenate([dwg, dwu], axis=1)
        grads["ffn1_w_out"][l] = dwo

        if pending is not None:
            dcur = finish_reduce(*pending, dcur)
        g_major = [_split_shards(n, grads[n][l]) for n in BIG]
        parts = [_add_half(g, got, c_arr) for g, got in zip(g_major, _swap_halves(g_major))]
        pending = (l, parts, _scatter_partials(parts, l, collective_id=depth + l))
    dcur = finish_reduce(*pending, dcur)
    final = {n: jnp.stack(reduced[n]) for n in BIG}
    grads = {n: jnp.stack(grads[n]) for n in SMALL + ("dn_conv_w",)}

    small_names = SMALL + ("dn_conv_w",)
    small_sizes = [int(np.prod(grads[n].shape)) for n in small_names]
    small_off = np.concatenate([[0], np.cumsum(small_sizes)])
    small = jnp.concatenate([grads[n].reshape(-1) for n in small_names] + [loss_part[0, :1]])
    small_sum = _allreduce_small(_pad_small(small), "allreduce_small").reshape(-1)
    for i, n in enumerate(small_names):
        final[n] = small_sum[small_off[i]:small_off[i + 1]].reshape(grads[n].shape)
    final["dn_conv_w"] = lax.dynamic_slice(final["dn_conv_w"], (0, 0, shard * conv_cols), (depth, DN_CONV, conv_cols))
    loss = small_sum[small_off[-1]]

    deltas, new_m, new_v = {}, {}, {}
    for n in WEIGHTS:
        shape = w[n].shape
        flat = (-1, shape[-1])
        dl, m2, v2 = _adamw(w[n].reshape(flat), final[n].reshape(flat), mom[n].reshape(flat), var[n].reshape(flat))
        deltas[n], new_m[n], new_v[n] = dl.reshape(shape), m2.reshape(shape), v2.reshape(shape)

    grad_x = dcur[None]
    return (loss, grad_x, *[final[n] for n in WEIGHTS], *[deltas[n] for n in WEIGHTS],
            *[new_m[n] for n in WEIGHTS], *[new_v[n] for n in WEIGHTS])
```

```python
import functools

import jax
import jax.numpy as jnp
import numpy as np
from jax import lax
from jax.experimental import pallas as pl
from jax.experimental.pallas import tpu as pltpu
from jax.experimental.pallas import tpu_sc as plsc

F32 = jnp.float32
BF16 = jnp.bfloat16

LANES = 128
HEAD_DIM = 128
DN_CHUNK = 64
DN_CONV = 4
CONV_ROWS = 8
SB_BLOCK = 128
FFN_HALF = 0.5
RMS_EPS = 1e-6
L2_EPS = 1e-6
NEG_BIG = -1e30
ADAM_LR = 0.001
ADAM_B1 = 0.9
ADAM_B2 = 0.999
ADAM_EPS = 1e-08
ADAM_WD = 0.01
ADAM_STEP = 10
VMEM_LIMIT = 56 * 1024 * 1024
N_CHIPS = 4
N_DEV = 8
MESH = pl.DeviceIdType.MESH

BIG = ("ffn1_w_in", "ffn1_w_out", "w_in", "w_branch_a", "w_branch_b", "w_out", "ffn2_w_in", "ffn2_w_out")
COL_SHARDED = ("ffn1_w_in", "w_in", "ffn2_w_in")
SMALL = ("ffn1_norm", "mix_norm", "dn_a_log", "dn_dt_bias", "dn_out_norm", "sb_q_norm", "sb_k_norm", "ffn2_norm")
WEIGHTS = ("ffn1_norm", "ffn1_w_in", "ffn1_w_out", "mix_norm", "w_in", "dn_conv_w", "dn_a_log", "dn_dt_bias",
           "dn_out_norm", "sb_q_norm", "sb_k_norm", "w_branch_a", "w_branch_b", "w_out", "ffn2_norm", "ffn2_w_in",
           "ffn2_w_out")


def _params(**kw):
    return pltpu.CompilerParams(vmem_limit_bytes=VMEM_LIMIT, **kw)


def _pick(n, options):
    for o in options:
        if n % o == 0:
            return o
    return n


def _const_spec(shape, single=False):
    nd = len(shape)
    if single:
        return pl.BlockSpec(shape, lambda *_: (0,) * nd, pipeline_mode=pl.Buffered(1))
    return pl.BlockSpec(shape, lambda *_: (0,) * nd)


_NN = ((1,), (0,))
_NT = ((1,), (1,))
_TN = ((0,), (0,))


def _dot(a, b, dims):
    return lax.dot_general(a.astype(BF16), b.astype(BF16), (dims, ((), ())), preferred_element_type=F32)


def _mm_nn(a, b):
    return _dot(a, b, _NN)


def _mm_nt(a, b):
    return _dot(a, b, _NT)


def _mm_tn(a, b):
    return _dot(a, b, _TN)


def _split(a):
    hi = a.astype(BF16)
    lo = (a - hi.astype(F32)).astype(BF16)
    return hi, lo


def _dot_precise(a, b, dims):
    dn = (dims, ((), ()))
    ah, al = _split(a)
    bh, bl = _split(b)
    out = lax.dot_general(ah, bh, dn, preferred_element_type=F32)
    out = out + lax.dot_general(ah, bl, dn, preferred_element_type=F32)
    return out + lax.dot_general(al, bh, dn, preferred_element_type=F32)


def _make_diff_mm(dot):
    @jax.custom_vjp
    def nn(a, b):
        return dot(a, b, _NN)

    @jax.custom_vjp
    def nt(a, b):
        return dot(a, b, _NT)

    @jax.custom_vjp
    def tn(a, b):
        return dot(a, b, _TN)

    nn.defvjp(lambda a, b: (dot(a, b, _NN), (a, b)), lambda r, g: (nt(g, r[1]), tn(r[0], g)))
    nt.defvjp(lambda a, b: (dot(a, b, _NT), (a, b)), lambda r, g: (nn(g, r[1]), tn(g, r[0])))
    tn.defvjp(lambda a, b: (dot(a, b, _TN), (a, b)), lambda r, g: (nt(r[1], g), nn(r[0], g)))
    return nn, nt, tn


_d_nn, _d_nt, _d_tn = _make_diff_mm(_dot)
_p_nn, _p_nt, _p_tn = _make_diff_mm(_dot_precise)


def _softplus_raw(x):
    return jnp.maximum(x, 0.0) + jnp.log(1.0 + jnp.exp(-jnp.abs(x)))


@jax.custom_vjp
def _softplus(x):
    return _softplus_raw(x)


_softplus.defvjp(lambda x: (_softplus_raw(x), x), lambda x, g: (g * jax.nn.sigmoid(x),))


def _rms(x, gain, eps):
    return x * lax.rsqrt(jnp.mean(x * x, axis=-1, keepdims=True) + eps) * gain


def _silu(x):
    return x * jax.nn.sigmoid(x)


def _shift_rows_raw(x, k, down):
    n = x.shape[0]
    row = lax.broadcasted_iota(jnp.int32, x.shape, 0)
    if down:
        return jnp.where(row >= k, pltpu.roll(x, k, 0), 0.0)
    return jnp.where(row < n - k, pltpu.roll(x, n - k, 0), 0.0)


@functools.partial(jax.custom_vjp, nondiff_argnums=(1,))
def _shift_down(x, k):
    return _shift_rows_raw(x, k, True)


_shift_down.defvjp(lambda x, k: (_shift_rows_raw(x, k, True), None),
                   lambda k, _, g: (_shift_rows_raw(g, k, False),))


def _layer_spec(layer, block, index_map, single=False):
    full_map = lambda *a: (layer,) + tuple(index_map(*a))
    if single:
        return pl.BlockSpec((None,) + block, full_map, pipeline_mode=pl.Buffered(1))
    return pl.BlockSpec((None,) + block, full_map)


def _ffn_fwd(x, gain, w_in, w_out, layer):
    t, d = x.shape
    f = w_out.shape[1]
    fc = _pick(f, (256, 128))
    nj = f // fc
    rt = _pick(t, (512, 256, 128))

    def body(x_ref, g_ref, wg_ref, wu_ref, wo_ref, o_ref, hs_ref):
        @pl.when(pl.program_id(0) == 0)
        def _():
            for r in range(t // rt):
                rows = pl.ds(r * rt, rt)
                xr = x_ref[rows, :]
                hs_ref[rows, :] = _rms(xr, g_ref[...], RMS_EPS).astype(BF16)
                o_ref[rows, :] = xr

        for r in range(t // rt):
            rows = pl.ds(r * rt, rt)
            h = hs_ref[rows, :]
            a = _mm_nn(h, wg_ref[...])
            b = _mm_nn(h, wu_ref[...])
            o_ref[rows, :] += FFN_HALF * _mm_nn(_silu(a) * b, wo_ref[...])

    return pl.pallas_call(
        body, name="ffn_fwd", grid=(nj,),
        in_specs=[_const_spec((t, d), True), _const_spec((1, d)),
                  _layer_spec(layer, (d, fc), lambda j: (0, j)), _layer_spec(layer, (d, fc), lambda j: (0, nj + j)),
                  _layer_spec(layer, (fc, d), lambda j: (j, 0))],
        out_specs=_const_spec((t, d)),
        out_shape=jax.ShapeDtypeStruct((t, d), F32),
        scratch_shapes=[pltpu.VMEM((t, d), BF16)],
        compiler_params=_params(dimension_semantics=("arbitrary",)),
    )(x, gain, w_in, w_in, w_out)


def _ffn_bwd(x, gain, dy, w_in, w_out, layer):
    t, d = x.shape
    f = w_out.shape[1]
    fc = _pick(f, (256, 128))
    nj = f // fc
    rt = _pick(t, (512, 256, 128))
    nr = t // rt

    def body(x_ref, g_ref, dy_ref, wg_ref, wu_ref, wo_ref, dx_ref, dg_ref, dwi_ref, dwo_ref,
             hs_ref, dwg_acc, dwu_acc, dwo_acc):
        j = pl.program_id(0)

        @pl.when(j == 0)
        def _():
            for r in range(nr):
                rows = pl.ds(r * rt, rt)
                hs_ref[rows, :] = _rms(x_ref[rows, :], g_ref[...], RMS_EPS).astype(BF16)
                dx_ref[rows, :] = jnp.zeros((rt, d), F32)

        for r in range(nr):
            rows = pl.ds(r * rt, rt)
            h = hs_ref[rows, :]
            dy2 = (FFN_HALF * dy_ref[rows, :]).astype(BF16)
            a = _mm_nn(h, wg_ref[...])
            b = _mm_nn(h, wu_ref[...])
            sig = jax.nn.sigmoid(a)
            sa = a * sig
            ds = _mm_nt(dy2, wo_ref[...])
            da = ds * b * (sig * (1.0 + a * (1.0 - sig)))
            db = ds * sa
            dx_ref[rows, :] += _mm_nt(da, wg_ref[...]) + _mm_nt(db, wu_ref[...])
            dwo_c = _mm_tn(sa * b, dy2)
            dwg_c = _mm_tn(h, da)
            dwu_c = _mm_tn(h, db)
            if r == 0:
                dwo_acc[...] = dwo_c
                dwg_acc[...] = dwg_c
                dwu_acc[...] = dwu_c
            else:
                dwo_acc[...] += dwo_c
                dwg_acc[...] += dwg_c
                dwu_acc[...] += dwu_c
        dwi_ref[0] = dwg_acc[...].astype(BF16)
        dwi_ref[1] = dwu_acc[...].astype(BF16)
        dwo_ref[...] = dwo_acc[...].astype(BF16)

        @pl.when(j == nj - 1)
        def _():
            for r in range(nr):
                rows = pl.ds(r * rt, rt)
                _, vjp = jax.vjp(lambda xx, gg: _rms(xx, gg, RMS_EPS), x_ref[rows, :], g_ref[...])
                dxn, dgr = vjp(dx_ref[rows, :])
                dx_ref[rows, :] = dy_ref[rows, :] + dxn
                if r == 0:
                    dg_ref[...] = dgr
                else:
                    dg_ref[...] += dgr

    return pl.pallas_call(
        body, name="ffn_bwd", grid=(nj,),
        in_specs=[_const_spec((t, d), True), _const_spec((1, d)), _const_spec((t, d), True),
                  _layer_spec(layer, (d, fc), lambda j: (0, j)), _layer_spec(layer, (d, fc), lambda j: (0, nj + j)),
                  _layer_spec(layer, (fc, d), lambda j: (j, 0))],
        out_specs=[_const_spec((t, d)), _const_spec((1, d)),
                   pl.BlockSpec((2, d, fc), lambda j: (0, 0, j)), pl.BlockSpec((fc, d), lambda j: (j, 0))],
        out_shape=[jax.ShapeDtypeStruct((t, d), F32), jax.ShapeDtypeStruct((1, d), F32),
                   jax.ShapeDtypeStruct((2, d, f), BF16), jax.ShapeDtypeStruct((f, d), BF16)],
        scratch_shapes=[pltpu.VMEM((t, d), BF16), pltpu.VMEM((d, fc), F32), pltpu.VMEM((d, fc), F32),
                        pltpu.VMEM((fc, d), F32)],
        compiler_params=_params(dimension_semantics=("arbitrary",)),
    )(x, gain, dy, w_in, w_in, w_out)


def _proj_fwd(x, gain, w, wba, layer):
    t, d = x.shape
    n = w.shape[2]
    nc = _pick(n, (512, 256, 128))
    rt = _pick(t, (512, 256, 128))

    def body(x_ref, g_ref, w_ref, wba_ref, p_ref, ba_ref, hs_ref):
        @pl.when(pl.program_id(0) == 0)
        def _():
            for r in range(t // rt):
                rows = pl.ds(r * rt, rt)
                h = _rms(x_ref[rows, :], g_ref[...], RMS_EPS).astype(BF16)
                hs_ref[rows, :] = h
                ba_ref[rows, :] = _mm_nn(h, wba_ref[...])

        for r in range(t // rt):
            rows = pl.ds(r * rt, rt)
            p_ref[rows, :] = _mm_nn(hs_ref[rows, :], w_ref[...])

    return pl.pallas_call(
        body, name="proj_fwd", grid=(n // nc,),
        in_specs=[_const_spec((t, d), True), _const_spec((1, d)),
                  _layer_spec(layer, (d, nc), lambda j: (0, j)), _layer_spec(layer, (d, LANES), lambda j: (0, 0))],
        out_specs=[pl.BlockSpec((t, nc), lambda j: (0, j)), _const_spec((t, LANES))],
        out_shape=[jax.ShapeDtypeStruct((t, n), F32), jax.ShapeDtypeStruct((t, LANES), F32)],
        scratch_shapes=[pltpu.VMEM((t, d), BF16)],
        compiler_params=_params(dimension_semantics=("arbitrary",)),
    )(x, gain, w, wba)


def _proj_bwd(x, gain, dres, dp, dba, w, wba, layer):
    t, d = x.shape
    n = w.shape[2]
    nc = _pick(n, (512, 256, 128))
    nj = n // nc
    rt = _pick(t, (512, 256, 128))
    nr = t // rt

    def body(x_ref, g_ref, dres_ref, dp_ref, dba_ref, w_ref, wba_ref, dx_ref, dg_ref, dw_ref, dwba_ref, hs_ref, dw_acc):
        j = pl.program_id(0)

        @pl.when(j == 0)
        def _():
            for r in range(nr):
                rows = pl.ds(r * rt, rt)
                h = _rms(x_ref[rows, :], g_ref[...], RMS_EPS).astype(BF16)
                hs_ref[rows, :] = h
                g = dba_ref[rows, :]
                dx_ref[rows, :] = _mm_nt(g, wba_ref[...])
                if r == 0:
                    dwba_ref[...] = _mm_tn(h, g)
                else:
                    dwba_ref[...] += _mm_tn(h, g)

        for r in range(nr):
            rows = pl.ds(r * rt, rt)
            g = dp_ref[rows, :]
            dx_ref[rows, :] += _mm_nt(g, w_ref[...])
            if r == 0:
                dw_acc[...] = _mm_tn(hs_ref[rows, :], g)
            else:
                dw_acc[...] += _mm_tn(hs_ref[rows, :], g)
        dw_ref[...] = dw_acc[...].astype(BF16)

        @pl.when(j == nj - 1)
        def _():
            for r in range(nr):
                rows = pl.ds(r * rt, rt)
                _, vjp = jax.vjp(lambda xx, gg: _rms(xx, gg, RMS_EPS), x_ref[rows, :], g_ref[...])
                dxn, dgr = vjp(dx_ref[rows, :])
                dx_ref[rows, :] = dres_ref[rows, :] + dxn
                if r == 0:
                    dg_ref[...] = dgr
                else:
                    dg_ref[...] += dgr

    return pl.pallas_call(
        body, name="proj_bwd", grid=(nj,),
        in_specs=[_const_spec((t, d), True), _const_spec((1, d)), _const_spec((t, d), True),
                  pl.BlockSpec((t, nc), lambda j: (0, j)), _const_spec((t, LANES)),
                  _layer_spec(layer, (d, nc), lambda j: (0, j)), _layer_spec(layer, (d, LANES), lambda j: (0, 0))],
        out_specs=[_const_spec((t, d)), _const_spec((1, d)),
                   pl.BlockSpec((d, nc), lambda j: (0, j)), _const_spec((d, LANES))],
        out_shape=[jax.ShapeDtypeStruct((t, d), F32), jax.ShapeDtypeStruct((1, d), F32),
                   jax.ShapeDtypeStruct((d, n), BF16), jax.ShapeDtypeStruct((d, LANES), F32)],
        scratch_shapes=[pltpu.VMEM((t, d), BF16), pltpu.VMEM((d, nc), F32)],
        compiler_params=_params(dimension_semantics=("arbitrary",)),
    )(x, gain, dres, dp, dba, w, wba)


def _conv_act(x, w0, w1, w2, w3, is_qk):
    y = w3 * x + w2 * _shift_down(x, 1) + w1 * _shift_down(x, 2) + w0 * _shift_down(x, 3)
    y = _silu(y)
    inv = lax.rsqrt(jnp.sum(y * y, axis=-1, keepdims=True) + L2_EPS)
    return y * (is_qk * inv + (1.0 - is_qk))


def _taps(w_ref):
    return tuple(w_ref[i:i + 1, :] for i in range(DN_CONV))


def _dn_prep_fwd(proj, conv_w, n_heads):
    t = proj.shape[0]
    nb = 3 * n_heads

    def body(x_ref, w_ref, o_ref):
        is_qk = jnp.where(pl.program_id(0) < 2 * n_heads, 1.0, 0.0).astype(F32)
        o_ref[...] = _conv_act(x_ref[...], *_taps(w_ref), is_qk)

    return pl.pallas_call(
        body, name="dn_prep_fwd", grid=(nb,),
        in_specs=[pl.BlockSpec((t, HEAD_DIM), lambda i: (0, i)), pl.BlockSpec((CONV_ROWS, HEAD_DIM), lambda i: (0, i))],
        out_specs=pl.BlockSpec((t, HEAD_DIM), lambda i: (0, i)),
        out_shape=jax.ShapeDtypeStruct((t, nb * HEAD_DIM), F32),
        compiler_params=_params(dimension_semantics=("arbitrary",)),
    )(proj, conv_w)


def _dn_prep_bwd(proj, conv_w, dact, n_heads):
    t = proj.shape[0]
    nb = 3 * n_heads

    def body(x_ref, w_ref, g_ref, dx_ref, dw_ref):
        is_qk = jnp.where(pl.program_id(0) < 2 * n_heads, 1.0, 0.0).astype(F32)
        _, vjp = jax.vjp(lambda x, a, b, c, e: _conv_act(x, a, b, c, e, is_qk), x_ref[...], *_taps(w_ref))
        dx, d0, d1, d2, d3 = vjp(g_ref[...])
        dx_ref[...] = dx.astype(BF16)
        dw_ref[...] = jnp.concatenate([d0, d1, d2, d3, jnp.zeros((CONV_ROWS - DN_CONV, HEAD_DIM), F32)], axis=0)

    return pl.pallas_call(
        body, name="dn_prep_bwd", grid=(nb,),
        in_specs=[pl.BlockSpec((t, HEAD_DIM), lambda i: (0, i)), pl.BlockSpec((CONV_ROWS, HEAD_DIM), lambda i: (0, i)),
                  pl.BlockSpec((t, HEAD_DIM), lambda i: (0, i))],
        out_specs=[pl.BlockSpec((t, HEAD_DIM), lambda i: (0, i)), pl.BlockSpec((CONV_ROWS, HEAD_DIM), lambda i: (0, i))],
        out_shape=[jax.ShapeDtypeStruct((t, nb * HEAD_DIM), BF16), jax.ShapeDtypeStruct((CONV_ROWS, nb * HEAD_DIM), F32)],
        compiler_params=_params(dimension_semantics=("arbitrary",)),
    )(proj, conv_w, dact)


def _unit_lower_inverses(lmats, c):
    r = lax.broadcasted_iota(jnp.int32, (c, c), 0)
    q = lax.broadcasted_iota(jnp.int32, (c, c), 1)
    eye = jnp.where(r == q, 1.0, 0.0)
    ps = [eye - l for l in lmats]
    ms = [_p_nn(l, l) for l in lmats]
    n = 2
    while True:
        ps = [p + _p_nn(p, m) for p, m in zip(ps, ms)]
        if 2 * n >= c:
            return ps
        ms = [_p_nn(m, m) for m in ms]
        n *= 2


def _delta_heads(qs, ks, vs, bg, alog, dtb, states):
    n_heads = len(qs)
    heads = range(n_heads)
    c = qs[0].shape[0]
    lane = lax.broadcasted_iota(jnp.int32, (c, LANES), 1)
    r = lax.broadcasted_iota(jnp.int32, (c, c), 0)
    s = lax.broadcasted_iota(jnp.int32, (c, c), 1)
    beta_all = jax.nn.sigmoid(bg)
    g_all = -jnp.exp(alog) * _softplus(bg + dtb)
    beta = [jnp.sum(jnp.where(lane == h, beta_all, 0.0), axis=1, keepdims=True) for h in heads]
    g = [jnp.sum(jnp.where(lane == n_heads + h, g_all, 0.0), axis=1, keepdims=True) for h in heads]
    g_row = [jnp.sum(jnp.where(r == s, g[h], 0.0), axis=0, keepdims=True) for h in heads]
    gc = [jnp.sum(jnp.where(s <= r, g_row[h], 0.0), axis=1, keepdims=True) for h in heads]
    gr = [jnp.sum(jnp.where(r <= s, g[h], 0.0), axis=0, keepdims=True) for h in heads]
    g_last = [jnp.sum(g[h], axis=0, keepdims=True) for h in heads]
    decay = [jnp.exp(jnp.where(r >= s, gc[h] - gr[h], NEG_BIG)) for h in heads]
    q_scaled = [qs[h] * (HEAD_DIM ** -0.5) for h in heads]
    k_beta = [ks[h] * beta[h] for h in heads]
    lmat = [jnp.where(r > s, _d_nt(k_beta[h], ks[h]) * decay[h], 0.0) for h in heads]
    attn = [_d_nt(q_scaled[h], ks[h]) * decay[h] for h in heads]
    tinv = _unit_lower_inverses(lmat, c)
    u = [_p_nn(tinv[h], vs[h] * beta[h]) for h in heads]
    w = [_p_nn(tinv[h], k_beta[h] * jnp.exp(gc[h])) for h in heads]
    v_new = [u[h] - _d_nn(w[h], states[h]) for h in heads]
    o_state = [_d_nn(q_scaled[h] * jnp.exp(gc[h]), states[h]) for h in heads]
    o = [o_state[h] + _d_nn(attn[h], v_new[h]) for h in heads]
    kv = [_d_tn(ks[h] * jnp.exp(g_last[h] - gc[h]), v_new[h]) for h in heads]
    new_states = [states[h] * jnp.exp(g_last[h]) + kv[h] for h in heads]
    return tuple(o), tuple(new_states)


def _delta_fwd(act, ba, alog, dtb, n_heads):
    t = act.shape[0]
    d = n_heads * HEAD_DIM
    c = DN_CHUNK
    nc = t // c

    def body(q_ref, k_ref, v_ref, bg_ref, al_ref, dt_ref, o_ref, snap_ref, st_ref):
        @pl.when(pl.program_id(0) == 0)
        def _():
            st_ref[...] = jnp.zeros(st_ref.shape, F32)

        snap_ref[0] = st_ref[...]
        cols = [slice(h * HEAD_DIM, (h + 1) * HEAD_DIM) for h in range(n_heads)]
        os, new_states = _delta_heads([q_ref[:, sl] for sl in cols], [k_ref[:, sl] for sl in cols],
                                      [v_ref[:, sl] for sl in cols], bg_ref[...], al_ref[...], dt_ref[...],
                                      [st_ref[h] for h in range(n_heads)])
        for h, sl in enumerate(cols):
            o_ref[:, sl] = os[h]
            st_ref[h] = new_states[h]

    return pl.pallas_call(
        body, name="delta_fwd", grid=(nc,),
        in_specs=[pl.BlockSpec((c, d), lambda i: (i, 0)), pl.BlockSpec((c, d), lambda i: (i, 1)),
                  pl.BlockSpec((c, d), lambda i: (i, 2)), pl.BlockSpec((c, LANES), lambda i: (i, 0)),
                  _const_spec((1, LANES)), _const_spec((1, LANES))],
        out_specs=[pl.BlockSpec((c, d), lambda i: (i, 0)),
                   pl.BlockSpec((1, n_heads, HEAD_DIM, HEAD_DIM), lambda i: (i, 0, 0, 0))],
        out_shape=[jax.ShapeDtypeStruct((t, d), F32), jax.ShapeDtypeStruct((nc, n_heads, HEAD_DIM, HEAD_DIM), F32)],
        scratch_shapes=[pltpu.VMEM((n_heads, HEAD_DIM, HEAD_DIM), F32)],
        compiler_params=_params(dimension_semantics=("arbitrary",)),
    )(act, act, act, ba, alog, dtb)


def _delta_bwd(act, ba, alog, dtb, snaps, do, n_heads):
    t = act.shape[0]
    d = n_heads * HEAD_DIM
    c = DN_CHUNK
    nc = t // c

    def body(q_ref, k_ref, v_ref, bg_ref, al_ref, dt_ref, snap_ref, do_ref,
             dact_ref, dbg_ref, dal_ref, ddt_ref, ds_ref):
        @pl.when(pl.program_id(0) == 0)
        def _():
            ds_ref[...] = jnp.zeros(ds_ref.shape, F32)
            dal_ref[...] = jnp.zeros((1, LANES), F32)
            ddt_ref[...] = jnp.zeros((1, LANES), F32)

        heads = range(n_heads)
        cols = [slice(h * HEAD_DIM, (h + 1) * HEAD_DIM) for h in heads]
        _, vjp = jax.vjp(_delta_heads, tuple(q_ref[:, sl] for sl in cols), tuple(k_ref[:, sl] for sl in cols),
                         tuple(v_ref[:, sl] for sl in cols), bg_ref[...], al_ref[...], dt_ref[...],
                         tuple(snap_ref[0, h] for h in heads))
        dq, dk, dv, dbg, dal, ddt, dst = vjp((tuple(do_ref[:, sl] for sl in cols), tuple(ds_ref[h] for h in heads)))
        for h, sl in enumerate(cols):
            dact_ref[:, sl] = dq[h]
            dact_ref[:, d + h * HEAD_DIM:d + (h + 1) * HEAD_DIM] = dk[h]
            dact_ref[:, 2 * d + h * HEAD_DIM:2 * d + (h + 1) * HEAD_DIM] = dv[h]
            ds_ref[h] = dst[h]
        dal_ref[...] += dal
        ddt_ref[...] += ddt
        dbg_ref[...] = dbg.astype(BF16)

    rev = lambda i: nc - 1 - i
    return pl.pallas_call(
        body, name="delta_bwd", grid=(nc,),
        in_specs=[pl.BlockSpec((c, d), lambda i: (rev(i), 0)), pl.BlockSpec((c, d), lambda i: (rev(i), 1)),
                  pl.BlockSpec((c, d), lambda i: (rev(i), 2)), pl.BlockSpec((c, LANES), lambda i: (rev(i), 0)),
                  _const_spec((1, LANES)), _const_spec((1, LANES)),
                  pl.BlockSpec((1, n_heads, HEAD_DIM, HEAD_DIM), lambda i: (rev(i), 0, 0, 0)),
                  pl.BlockSpec((c, d), lambda i: (rev(i), 0))],
        out_specs=[pl.BlockSpec((c, 3 * d), lambda i: (rev(i), 0)), pl.BlockSpec((c, LANES), lambda i: (rev(i), 0)),
                   _const_spec((1, LANES)), _const_spec((1, LANES))],
        out_shape=[jax.ShapeDtypeStruct((t, 3 * d), F32), jax.ShapeDtypeStruct((t, LANES), BF16),
                   jax.ShapeDtypeStruct((1, LANES), F32), jax.ShapeDtypeStruct((1, LANES), F32)],
        scratch_shapes=[pltpu.VMEM((n_heads, HEAD_DIM, HEAD_DIM), F32)],
        compiler_params=_params(dimension_semantics=("arbitrary",)),
    )(act, act, act, ba, alog, dtb, snaps, do)


def _head_norm2(a, b, ga, gb):
    return _rms(a, ga, RMS_EPS), _rms(b, gb, RMS_EPS)


def _sb_prep_fwd(proj, qn, kn, n_heads):
    t = proj.shape[0]
    d = n_heads * HEAD_DIM
    tm = _pick(t, (256, 128))

    def body(q_ref, k_ref, qn_ref, kn_ref, qo_ref, ko_ref):
        for h in range(n_heads):
            sl = slice(h * HEAD_DIM, (h + 1) * HEAD_DIM)
            qo_ref[:, sl], ko_ref[:, sl] = _head_norm2(q_ref[:, sl], k_ref[:, sl], qn_ref[...], kn_ref[...])

    return pl.pallas_call(
        body, name="sb_prep_fwd", grid=(t // tm,),
        in_specs=[pl.BlockSpec((tm, d), lambda i: (i, 4)), pl.BlockSpec((tm, d), lambda i: (i, 5)),
                  _const_spec((1, HEAD_DIM)), _const_spec((1, HEAD_DIM))],
        out_specs=[pl.BlockSpec((tm, d), lambda i: (i, 0)), pl.BlockSpec((tm, d), lambda i: (i, 0))],
        out_shape=[jax.ShapeDtypeStruct((t, d), F32), jax.ShapeDtypeStruct((t, d), F32)],
        compiler_params=_params(dimension_semantics=("arbitrary",)),
    )(proj, proj, qn, kn)


def _sb_prep_bwd(proj, qn, kn, dq, dk, n_heads):
    t = proj.shape[0]
    d = n_heads * HEAD_DIM
    tm = _pick(t, (256, 128))

    def body(q_ref, k_ref, qn_ref, kn_ref, dq_ref, dk_ref, dqo_ref, dko_ref, dqn_ref, dkn_ref):
        @pl.when(pl.program_id(0) == 0)
        def _():
            dqn_ref[...] = jnp.zeros((1, HEAD_DIM), F32)
            dkn_ref[...] = jnp.zeros((1, HEAD_DIM), F32)

        for h in range(n_heads):
            sl = slice(h * HEAD_DIM, (h + 1) * HEAD_DIM)
            _, vjp = jax.vjp(_head_norm2, q_ref[:, sl], k_ref[:, sl], qn_ref[...], kn_ref[...])
            da, db, dga, dgb = vjp((dq_ref[:, sl], dk_ref[:, sl]))
            dqo_ref[:, sl] = da.astype(BF16)
            dko_ref[:, sl] = db.astype(BF16)
            dqn_ref[...] += dga
            dkn_ref[...] += dgb

    return pl.pallas_call(
        body, name="sb_prep_bwd", grid=(t // tm,),
        in_specs=[pl.BlockSpec((tm, d), lambda i: (i, 4)), pl.BlockSpec((tm, d), lambda i: (i, 5)),
                  _const_spec((1, HEAD_DIM)), _const_spec((1, HEAD_DIM)),
                  pl.BlockSpec((tm, d), lambda i: (i, 0)), pl.BlockSpec((tm, d), lambda i: (i, 0))],
        out_specs=[pl.BlockSpec((tm, d), lambda i: (i, 0)), pl.BlockSpec((tm, d), lambda i: (i, 0)),
                   _const_spec((1, HEAD_DIM)), _const_spec((1, HEAD_DIM))],
        out_shape=[jax.ShapeDtypeStruct((t, d), BF16), jax.ShapeDtypeStruct((t, d), BF16),
                   jax.ShapeDtypeStruct((1, HEAD_DIM), F32), jax.ShapeDtypeStruct((1, HEAD_DIM), F32)],
        compiler_params=_params(dimension_semantics=("arbitrary",)),
    )(proj, proj, qn, kn, dq, dk)


def _cumsum_mm(x, tri):
    hi, lo = _split(x)
    return (lax.dot_general(hi, tri, (_NN, ((), ())), preferred_element_type=F32)
            + lax.dot_general(lo, tri, (_NN, ((), ())), preferred_element_type=F32))


def _sb_valid(i, j):
    row = lax.broadcasted_iota(jnp.int32, (SB_BLOCK, SB_BLOCK), 0)
    col = lax.broadcasted_iota(jnp.int32, (SB_BLOCK, SB_BLOCK), 1)
    return (col + j * SB_BLOCK) < (row + i * SB_BLOCK)


def _sb_attn_fwd(qb, kb, proj, n_heads):
    t = qb.shape[0]
    d = n_heads * HEAD_DIM
    nq = t // SB_BLOCK
    hb = _pick(n_heads, (4, 2, 1))
    wide = hb * HEAD_DIM
    v_col0 = 6 * n_heads // hb
    scale = HEAD_DIM ** -0.5

    def body(q_ref, k_ref, v_ref, o_ref, lt_ref):
        i = pl.program_id(1)
        row = lax.broadcasted_iota(jnp.int32, (SB_BLOCK, SB_BLOCK), 0)
        col = lax.broadcasted_iota(jnp.int32, (SB_BLOCK, SB_BLOCK), 1)
        after = jnp.where(row > col, 1.0, 0.0).astype(BF16)
        heads = [slice(h * HEAD_DIM, (h + 1) * HEAD_DIM) for h in range(hb)]
        every = range(hb)
        qs = [q_ref[:, sl].astype(BF16) for sl in heads]

        def step(jj, carry):
            j = i - jj
            rows = pl.ds(pl.multiple_of(j * SB_BLOCK, SB_BLOCK), SB_BLOCK)
            valid = _sb_valid(i, j)
            z = [_mm_nt(qs[h], k_ref[rows, heads[h]]) * scale for h in every]
            sp = [_softplus_raw(z[h]) for h in every]
            lm = [jnp.where(valid, -sp[h], 0.0) for h in every]
            tail = [_cumsum_mm(lm[h], after) for h in every]
            w = [jnp.where(valid, jnp.exp(z[h] - sp[h] + carry[h][1] + tail[h]), 0.0) for h in every]
            pv = [_mm_nn(w[h], v_ref[rows, heads[h]]) for h in every]
            return tuple((carry[h][0] + pv[h], carry[h][1] + jnp.sum(lm[h], axis=1, keepdims=True)) for h in every)

        init = tuple((jnp.zeros((SB_BLOCK, HEAD_DIM), F32), jnp.zeros((SB_BLOCK, 1), F32)) for _ in heads)
        res = lax.fori_loop(0, i + 1, step, init)
        for h, sl in enumerate(heads):
            o_ref[:, sl] = res[h][0]
            lt_ref[:, sl] = jnp.broadcast_to(res[h][1], (SB_BLOCK, HEAD_DIM))

    return pl.pallas_call(
        body, name="sb_attn_fwd", grid=(n_heads // hb, nq),
        in_specs=[pl.BlockSpec((SB_BLOCK, wide), lambda g, i: (i, g)),
                  pl.BlockSpec((t, wide), lambda g, i: (0, g)),
                  pl.BlockSpec((t, wide), lambda g, i: (0, v_col0 + g))],
        out_specs=[pl.BlockSpec((SB_BLOCK, wide), lambda g, i: (i, g)),
                   pl.BlockSpec((SB_BLOCK, wide), lambda g, i: (i, g))],
        out_shape=[jax.ShapeDtypeStruct((t, d), F32), jax.ShapeDtypeStruct((t, d), F32)],
        compiler_params=_params(dimension_semantics=("arbitrary", "arbitrary")),
    )(qb, kb, proj)


def _sb_attn_bwd(qb, kb, proj, ltot, do, n_heads):
    t = qb.shape[0]
    d = n_heads * HEAD_DIM
    nq = t // SB_BLOCK
    hb = _pick(n_heads, (4, 2, 1))
    wide = hb * HEAD_DIM
    v_col0 = 6 * n_heads // hb
    scale = HEAD_DIM ** -0.5

    def body(q_ref, k_ref, v_ref, lt_ref, do_ref, dq_ref, dk_ref, dv_ref):
        i = pl.program_id(1)

        @pl.when(i == 0)
        def _():
            dk_ref[...] = jnp.zeros((t, wide), F32)
            dv_ref[...] = jnp.zeros((t, wide), F32)

        row = lax.broadcasted_iota(jnp.int32, (SB_BLOCK, SB_BLOCK), 0)
        col = lax.broadcasted_iota(jnp.int32, (SB_BLOCK, SB_BLOCK), 1)
        upto = jnp.where(row <= col, 1.0, 0.0).astype(BF16)
        before = jnp.where(row < col, 1.0, 0.0).astype(BF16)
        heads = [slice(h * HEAD_DIM, (h + 1) * HEAD_DIM) for h in range(hb)]
        every = range(hb)
        qs = [q_ref[:, sl].astype(BF16) for sl in heads]
        dos = [do_ref[:, sl].astype(BF16) for sl in heads]
        totals = [jnp.max(lt_ref[:, sl], axis=1, keepdims=True) for sl in heads]

        def step(j, carry):
            rows = pl.ds(pl.multiple_of(j * SB_BLOCK, SB_BLOCK), SB_BLOCK)
            valid = _sb_valid(i, j)
            kj = [k_ref[rows, heads[h]].astype(BF16) for h in every]
            vj = [v_ref[rows, heads[h]].astype(BF16) for h in every]
            z = [_mm_nt(qs[h], kj[h]) * scale for h in every]
            dw = [_mm_nt(dos[h], vj[h]) for h in every]
            sp = [_softplus_raw(z[h]) for h in every]
            lm = [jnp.where(valid, -sp[h], 0.0) for h in every]
            head = [_cumsum_mm(lm[h], upto) for h in every]
            w = [jnp.where(valid, jnp.exp(z[h] - sp[h] + totals[h] - (carry[h][1] + head[h])), 0.0) for h in every]
            e = [w[h] * dw[h] for h in every]
            e_pre = [carry[h][2] + _mm_nn(e[h], before) for h in every]
            sig = [jnp.exp(z[h] - sp[h]) for h in every]
            dz = [jnp.where(valid, e[h] * (1.0 - sig[h]) - e_pre[h] * sig[h], 0.0) * scale for h in every]
            for h in every:
                dv_ref[rows, heads[h]] += _mm_tn(w[h], dos[h])
            for h in every:
                dk_ref[rows, heads[h]] += _mm_tn(dz[h], qs[h])
            dq = [_mm_nn(dz[h], kj[h]) for h in every]
            return tuple((carry[h][0] + dq[h], carry[h][1] + jnp.sum(lm[h], axis=1, keepdims=True),
                          carry[h][2] + jnp.sum(e[h], axis=1, keepdims=True)) for h in every)

        zero_col = jnp.zeros((SB_BLOCK, 1), F32)
        init = tuple((jnp.zeros((SB_BLOCK, HEAD_DIM), F32), zero_col, zero_col) for _ in heads)
        res = lax.fori_loop(0, i + 1, step, init)
        for h, sl in enumerate(heads):
            dq_ref[:, sl] = res[h][0]

    return pl.pallas_call(
        body, name="sb_attn_bwd", grid=(n_heads // hb, nq),
        in_specs=[pl.BlockSpec((SB_BLOCK, wide), lambda g, i: (i, g)),
                  pl.BlockSpec((t, wide), lambda g, i: (0, g)),
                  pl.BlockSpec((t, wide), lambda g, i: (0, v_col0 + g)),
                  pl.BlockSpec((SB_BLOCK, wide), lambda g, i: (i, g)),
                  pl.BlockSpec((SB_BLOCK, wide), lambda g, i: (i, g))],
        out_specs=[pl.BlockSpec((SB_BLOCK, wide), lambda g, i: (i, g)),
                   pl.BlockSpec((t, wide), lambda g, i: (0, g)),
                   pl.BlockSpec((t, wide), lambda g, i: (0, g))],
        out_shape=[jax.ShapeDtypeStruct((t, d), F32), jax.ShapeDtypeStruct((t, d), F32),
                   jax.ShapeDtypeStruct((t, d), F32)],
        compiler_params=_params(dimension_semantics=("arbitrary", "arbitrary")),
    )(qb, kb, proj, ltot, do)


def _gated_norm(oa, z, gn):
    return _rms(oa, gn, RMS_EPS) * _silu(z)


def _merge_gates(ya, yb, ga, gb):
    return jax.nn.sigmoid(ga) * ya + jax.nn.sigmoid(gb) * yb


def _merge_fwd(x1, oa, proj, ob, gn, wa, wb, wo, layer, n_heads):
    t, d = x1.shape
    tm = _pick(t, (256, 128))
    square = _layer_spec(layer, (d, d), lambda i: (0, 0), single=True)

    def body(x_ref, oa_ref, z_ref, ob_ref, ga_ref, gb_ref, gn_ref, wa_ref, wb_ref, wo_ref, o_ref, na_ref):
        for h in range(n_heads):
            sl = slice(h * HEAD_DIM, (h + 1) * HEAD_DIM)
            na_ref[:, sl] = _gated_norm(oa_ref[:, sl], z_ref[:, sl], gn_ref[...]).astype(BF16)
        m = _merge_gates(_mm_nn(na_ref[...], wa_ref[...]), _mm_nn(ob_ref[...], wb_ref[...]), ga_ref[...], gb_ref[...])
        o_ref[...] = x_ref[...] + _mm_nn(m, wo_ref[...])

    tile = lambda k: pl.BlockSpec((tm, d), lambda i: (i, k))
    return pl.pallas_call(
        body, name="merge_fwd", grid=(t // tm,),
        in_specs=[tile(0), tile(0), tile(3), tile(0), tile(7), tile(8), _const_spec((1, HEAD_DIM)),
                  square, square, square],
        out_specs=tile(0),
        out_shape=jax.ShapeDtypeStruct((t, d), F32),
        scratch_shapes=[pltpu.VMEM((tm, d), BF16)],
        compiler_params=_params(dimension_semantics=("arbitrary",)),
    )(x1, oa, proj, ob, proj, proj, gn, wa, wb, wo)


def _merge_bwd(oa, proj, ob, dy, gn, wa, wb, wo, layer, n_heads):
    t, d = oa.shape
    tm = _pick(t, (256, 128))
    nt = t // tm
    square = _layer_spec(layer, (d, d), lambda i: (0, 0), single=True)

    def body(oa_ref, z_ref, ob_ref, ga_ref, gb_ref, dy_ref, gn_ref, wa_ref, wb_ref, wo_ref,
             doa_ref, dz_ref, dob_ref, dga_ref, dgb_ref, dgn_ref, dwa_hbm, dwb_hbm, dwo_hbm,
             na_ref, dna_ref, dwa_ref, dwb_ref, dwo_ref, stage_ref):
        i = pl.program_id(0)

        @pl.when(i == 0)
        def _():
            dgn_ref[...] = jnp.zeros((1, HEAD_DIM), F32)
            dwa_ref[...] = jnp.zeros((d, d), F32)
            dwb_ref[...] = jnp.zeros((d, d), F32)
            dwo_ref[...] = jnp.zeros((d, d), F32)

        for h in range(n_heads):
            sl = slice(h * HEAD_DIM, (h + 1) * HEAD_DIM)
            na_ref[:, sl] = _gated_norm(oa_ref[:, sl], z_ref[:, sl], gn_ref[...]).astype(BF16)
        dy = dy_ref[...].astype(BF16)
        ob = ob_ref[...].astype(BF16)
        ya = _mm_nn(na_ref[...], wa_ref[...])
        yb = _mm_nn(ob, wb_ref[...])
        m, vjp = jax.vjp(_merge_gates, ya, yb, ga_ref[...], gb_ref[...])
        dwo_ref[...] += _mm_tn(m, dy)
        dya, dyb, dga, dgb = vjp(_mm_nt(dy, wo_ref[...]))
        dga_ref[...] = dga.astype(BF16)
        dgb_ref[...] = dgb.astype(BF16)
        dwa_ref[...] += _mm_tn(na_ref[...], dya)
        dwb_ref[...] += _mm_tn(ob, dyb)
        dob_ref[...] = _mm_nt(dyb, wb_ref[...])
        dna_ref[...] = _mm_nt(dya, wa_ref[...])
        for h in range(n_heads):
            sl = slice(h * HEAD_DIM, (h + 1) * HEAD_DIM)
            _, vjp_h = jax.vjp(_gated_norm, oa_ref[:, sl], z_ref[:, sl], gn_ref[...])
            doa, dz, dgn = vjp_h(dna_ref[:, sl])
            doa_ref[:, sl] = doa
            dz_ref[:, sl] = dz.astype(BF16)
            dgn_ref[...] += dgn

        @pl.when(i == nt - 1)
        def _():
            for acc, out in ((dwa_ref, dwa_hbm), (dwb_ref, dwb_hbm), (dwo_ref, dwo_hbm)):
                stage_ref[...] = acc[...].astype(BF16)
                pltpu.sync_copy(stage_ref, out)

    tile = lambda k: pl.BlockSpec((tm, d), lambda i: (i, k))
    any_spec = pl.BlockSpec(memory_space=pl.ANY)
    return pl.pallas_call(
        body, name="merge_bwd", grid=(nt,),
        in_specs=[tile(0), tile(3), tile(0), tile(7), tile(8), tile(0), _const_spec((1, HEAD_DIM)),
                  square, square, square],
        out_specs=[tile(0), tile(0), tile(0), tile(0), tile(0), _const_spec((1, HEAD_DIM)),
                   any_spec, any_spec, any_spec],
        out_shape=[jax.ShapeDtypeStruct((t, d), F32), jax.ShapeDtypeStruct((t, d), BF16),
                   jax.ShapeDtypeStruct((t, d), F32), jax.ShapeDtypeStruct((t, d), BF16),
                   jax.ShapeDtypeStruct((t, d), BF16), jax.ShapeDtypeStruct((1, HEAD_DIM), F32),
                   jax.ShapeDtypeStruct((d, d), BF16), jax.ShapeDtypeStruct((d, d), BF16),
                   jax.ShapeDtypeStruct((d, d), BF16)],
        scratch_shapes=[pltpu.VMEM((tm, d), BF16), pltpu.VMEM((tm, d), F32),
                        pltpu.VMEM((d, d), F32), pltpu.VMEM((d, d), F32), pltpu.VMEM((d, d), F32),
                        pltpu.VMEM((d, d), BF16)],
        compiler_params=_params(dimension_semantics=("arbitrary",)),
    )(oa, proj, ob, proj, proj, dy, gn, wa, wb, wo)


def _loss_head(y, target):
    t, d = y.shape
    tm = _pick(t, (256, 128))

    def body(y_ref, t_ref, dy_ref, loss_ref):
        @pl.when(pl.program_id(0) == 0)
        def _():
            loss_ref[...] = jnp.zeros((8, LANES), F32)

        err = y_ref[...] - t_ref[...]
        dy_ref[...] = err * (1.0 / d)
        per_token = jnp.sum(err * err, axis=1, keepdims=True) * (1.0 / d)
        loss_ref[...] += 0.5 * jnp.sum(per_token, axis=0, keepdims=True)

    return pl.pallas_call(
        body, name="loss_head", grid=(t // tm,),
        in_specs=[pl.BlockSpec((tm, d), lambda i: (i, 0)), pl.BlockSpec((tm, d), lambda i: (i, 0))],
        out_specs=[pl.BlockSpec((tm, d), lambda i: (i, 0)), _const_spec((8, LANES))],
        out_shape=[jax.ShapeDtypeStruct((t, d), F32), jax.ShapeDtypeStruct((8, LANES), F32)],
        compiler_params=_params(dimension_semantics=("arbitrary",)),
    )(y, target)


def _adamw(w, g, m, v):
    rows, cols = w.shape
    tr = rows
    for cand in (512, 256, 128, 64, 32, 16, 8):
        if rows % cand == 0 and cand * cols * 4 <= 2 * 1024 * 1024:
            tr = cand
            break

    def body(w_ref, g_ref, m_ref, v_ref, d_ref, mo_ref, vo_ref):
        g = g_ref[...]
        m2 = ADAM_B1 * m_ref[...] + (1.0 - ADAM_B1) * g
        v2 = ADAM_B2 * v_ref[...] + (1.0 - ADAM_B2) * (g * g)
        m_hat = m2 / (1.0 - ADAM_B1 ** ADAM_STEP)
        v_hat = v2 / (1.0 - ADAM_B2 ** ADAM_STEP)
        d_ref[...] = -ADAM_LR * (m_hat / (jnp.sqrt(v_hat) + ADAM_EPS) + ADAM_WD * w_ref[...])
        mo_ref[...] = m2
        vo_ref[...] = v2

    spec = pl.BlockSpec((tr, cols), lambda i: (i, 0))
    shape = jax.ShapeDtypeStruct((rows, cols), F32)
    return pl.pallas_call(
        body, name="adamw", grid=(rows // tr,), in_specs=[spec] * 4, out_specs=[spec] * 3,
        out_shape=[shape] * 3, compiler_params=_params(dimension_semantics=("arbitrary",)),
    )(w, g, m, v)


def _place():
    return lax.axis_index("x"), lax.axis_index("y"), lax.axis_index("c")


def _other_chips(x, y):
    return [(1 - x, y), (x, 1 - y), (1 - x, 1 - y)]


def _tile_rows(rows, cols, itemsize, cap=1536 * 1024):
    best = None
    for cand in range(16, rows + 1, 16):
        if rows % cand == 0 and cand * cols * itemsize <= cap:
            best = cand
    return best if best is not None else rows


def _allgather_layer(shards, layer, collective_id):
    n = len(shards)

    def body(*refs):
        srcs, outs = refs[:n], refs[n:2 * n]
        send_sems, recv_sems = refs[2 * n:]
        x, y, c = _place()
        me, sibling = (x, y, c), (x, y, 1 - c)
        chips = _other_chips(x, y)
        barrier = pltpu.get_barrier_semaphore()
        for peer in [(*chip, c) for chip in chips] + [sibling]:
            pl.semaphore_signal(barrier, inc=1, device_id=peer, device_id_type=MESH)
        pl.semaphore_wait(barrier, N_CHIPS)

        def half(w, which):
            rows = shards[w].shape[1] // 2
            return pl.ds(which * rows, rows)

        def copy(w, k, shard, which, to, from_src=False):
            part = half(w, which)
            return pltpu.make_async_remote_copy(
                src_ref=srcs[w].at[layer, part] if from_src else outs[w].at[shard, part],
                dst_ref=outs[w].at[shard, part], send_sem=send_sems.at[6 * w + k], recv_sem=recv_sems.at[6 * w + k],
                device_id=to, device_id_type=MESH)

        first = [copy(w, j, 2 * x + y, c, (*chip, c), from_src=True) for j, chip in enumerate(chips) for w in range(n)]
        for cp in first:
            cp.start()
        passed = []
        for j, (cx, cy) in enumerate(chips):
            for w in range(n):
                copy(w, j, 2 * cx + cy, c, me).wait_recv()
                cp = copy(w, 3 + j, 2 * cx + cy, c, sibling)
                cp.start()
                passed.append(cp)
        for j, (cx, cy) in enumerate(chips):
            for w in range(n):
                copy(w, 3 + j, 2 * cx + cy, 1 - c, me).wait_recv()
        for cp in first + passed:
            cp.wait_send()

    return pl.kernel(
        body, name=f"allgather_layer{layer}_id{collective_id}",
        out_type=[jax.ShapeDtypeStruct((N_CHIPS,) + s.shape[1:], s.dtype) for s in shards],
        mesh=plsc.ScalarSubcoreMesh(axis_name="sequencer", num_cores=1),
        scratch_types=[pltpu.SemaphoreType.DMA((6 * n,)), pltpu.SemaphoreType.DMA((6 * n,))],
        compiler_params=pltpu.CompilerParams(collective_id=collective_id),
    )(*shards)


def _swap_halves(grads):
    n = len(grads)

    def body(*refs):
        gs, gots = refs[:n], refs[n:2 * n]
        send_sems, recv_sems = refs[2 * n:]
        x, y, c = _place()
        copies = []
        for w in range(n):
            half = grads[w].shape[1] // 2
            copies.append(pltpu.make_async_remote_copy(
                src_ref=gs[w].at[:, pl.ds((1 - c) * half, half)], dst_ref=gots[w], send_sem=send_sems.at[w],
                recv_sem=recv_sems.at[w], device_id=(x, y, 1 - c), device_id_type=MESH))
        for cp in copies:
            cp.start()
        for cp in copies:
            cp.wait()

    hbm = pl.BlockSpec(memory_space=pl.ANY)
    return pl.pallas_call(
        body, name="swap_halves", in_specs=[hbm] * n, out_specs=[hbm] * n,
        out_shape=[jax.ShapeDtypeStruct((g.shape[0], g.shape[1] // 2, g.shape[2]), g.dtype) for g in grads],
        scratch_shapes=[pltpu.SemaphoreType.DMA((n,)), pltpu.SemaphoreType.DMA((n,))],
    )(*grads)


def _add_half(grad, got, c_idx):
    n, rows, all_cols = grad.shape
    side = N_CHIPS // n
    cols = all_cols // side
    half = rows // 2
    tr = _tile_rows(half, cols, 2)
    nb = half // tr

    def body(c_ref, a_ref, b_ref, o_ref):
        o_ref[...] = (a_ref[...].astype(F32) + b_ref[...].astype(F32)).astype(o_ref.dtype)

    return pl.pallas_call(
        body, name="add_half",
        grid_spec=pltpu.PrefetchScalarGridSpec(
            num_scalar_prefetch=1, grid=(N_CHIPS, nb),
            in_specs=[pl.BlockSpec((1, tr, cols), lambda s, r, c_ref: (s // side, c_ref[0] * nb + r, s % side)),
                      pl.BlockSpec((1, tr, cols), lambda s, r, c_ref: (s // side, r, s % side))],
            out_specs=pl.BlockSpec((1, tr, cols), lambda s, r, c_ref: (s, r, 0))),
        out_shape=jax.ShapeDtypeStruct((N_CHIPS, half, cols), grad.dtype),
        compiler_params=_params(dimension_semantics=("arbitrary", "arbitrary")),
    )(c_idx, grad, got)


def _scatter_partials(parts, layer, collective_id):
    n = len(parts)

    def body(*refs):
        ps, gots = refs[:n], refs[n:2 * n]
        send_sems, recv_sems = refs[2 * n:]
        x, y, c = _place()
        chips = _other_chips(x, y)
        barrier = pltpu.get_barrier_semaphore()
        for chip in chips:
            pl.semaphore_signal(barrier, inc=1, device_id=(*chip, c), device_id_type=MESH)
        pl.semaphore_wait(barrier, N_CHIPS - 1)
        copies = [pltpu.make_async_remote_copy(src_ref=ps[w].at[2 * cx + cy], dst_ref=gots[w].at[j],
                                               send_sem=send_sems.at[3 * w + j], recv_sem=recv_sems.at[3 * w + j],
                                               device_id=(cx, cy, c), device_id_type=MESH)
                  for j, (cx, cy) in enumerate(chips) for w in range(n)]
        for cp in copies:
            cp.start()
        for cp in copies:
            cp.wait()

    return pl.kernel(
        body, name=f"scatter_partials{layer}",
        out_type=[jax.ShapeDtypeStruct((N_CHIPS - 1,) + p.shape[1:], p.dtype) for p in parts],
        mesh=plsc.ScalarSubcoreMesh(axis_name="sequencer", num_cores=1),
        scratch_types=[pltpu.SemaphoreType.DMA((3 * n,)), pltpu.SemaphoreType.DMA((3 * n,))],
        compiler_params=pltpu.CompilerParams(collective_id=collective_id),
    )(*parts)


def _sum_partials(part, got, s_idx, c_idx):
    n, half, cols = part.shape
    tr = _tile_rows(half, cols, 2, cap=1024 * 1024)
    nb = half // tr

    def body(s_ref, c_ref, a_ref, b_ref, o_ref):
        acc = a_ref[0].astype(F32)
        for j in range(n - 1):
            acc = acc + b_ref[j].astype(F32)
        o_ref[...] = acc

    return pl.pallas_call(
        body, name="sum_partials",
        grid_spec=pltpu.PrefetchScalarGridSpec(
            num_scalar_prefetch=2, grid=(nb,),
            in_specs=[pl.BlockSpec((1, tr, cols), lambda r, s_ref, c_ref: (s_ref[0], r, 0)),
                      pl.BlockSpec((n - 1, tr, cols), lambda r, s_ref, c_ref: (0, r, 0))],
            out_specs=pl.BlockSpec((tr, cols), lambda r, s_ref, c_ref: (c_ref[0] * nb + r, 0))),
        out_shape=jax.ShapeDtypeStruct((2 * half, cols), F32),
        compiler_params=_params(dimension_semantics=("arbitrary",)),
    )(s_idx, c_idx, part, got)


def _join_halves(bufs):
    n = len(bufs)

    def body(*refs):
        outs = refs[n:2 * n]
        send_sems, recv_sems = refs[2 * n:]
        x, y, c = _place()
        copies = []
        for w in range(n):
            half = bufs[w].shape[0] // 2
            mine = outs[w].at[pl.ds(c * half, half)]
            copies.append(pltpu.make_async_remote_copy(src_ref=mine, dst_ref=mine, send_sem=send_sems.at[w],
                                                       recv_sem=recv_sems.at[w], device_id=(x, y, 1 - c),
                                                       device_id_type=MESH))
        for cp in copies:
            cp.start()
        for cp in copies:
            cp.wait()

    hbm = pl.BlockSpec(memory_space=pl.ANY)
    return pl.pallas_call(
        body, name="join_halves", in_specs=[hbm] * n, out_specs=[hbm] * n,
        out_shape=[jax.ShapeDtypeStruct(b.shape, b.dtype) for b in bufs],
        input_output_aliases={w: w for w in range(n)},
        scratch_shapes=[pltpu.SemaphoreType.DMA((n,)), pltpu.SemaphoreType.DMA((n,))],
    )(*bufs)


def _allreduce_small(v, name):
    rows = v.shape[0]

    def body(v_ref, o_ref, gath, send_sems, recv_sems):
        x, y, c = _place()
        idx = 4 * x + 2 * y + c
        gath[0] = v_ref[...]
        copies = []
        for r in range(1, N_DEV):
            peer = (1 - x if r & 4 else x, 1 - y if r & 2 else y, 1 - c if r & 1 else c)
            cp = pltpu.make_async_remote_copy(src_ref=v_ref, dst_ref=gath.at[r], send_sem=send_sems.at[r - 1],
                                              recv_sem=recv_sems.at[r - 1], device_id=peer, device_id_type=MESH)
            cp.start()
            copies.append(cp)
        for cp in copies:
            cp.wait()
        acc = gath[idx]
        for a in range(1, N_DEV):
            acc = acc + gath[lax.bitwise_xor(idx, a)]
        o_ref[...] = acc

    vmem = pl.BlockSpec(memory_space=pltpu.VMEM)
    return pl.pallas_call(
        body, name=name, in_specs=[vmem], out_specs=vmem,
        out_shape=jax.ShapeDtypeStruct((rows, LANES), F32),
        scratch_shapes=[pltpu.VMEM((N_DEV, rows, LANES), F32), pltpu.SemaphoreType.DMA((N_DEV - 1,)),
                        pltpu.SemaphoreType.DMA((N_DEV - 1,))],
    )(v)


def _join_shards(name, gathered):
    return jnp.concatenate([gathered[s] for s in range(N_CHIPS)], axis=1 if name in COL_SHARDED else 0)[None]


def _row_shards(g):
    return g.reshape(N_CHIPS, g.shape[0] // N_CHIPS, g.shape[1])


def _col_shards(g):
    width = g.shape[1] // N_CHIPS
    return jnp.stack([g[:, s * width:(s + 1) * width] for s in range(N_CHIPS)])


def _pad_small(flat):
    n = flat.shape[0]
    block = 8 * LANES
    padded = -(-n // block) * block
    return jnp.pad(flat, (0, padded - n)).reshape(padded // LANES, LANES)


def kernel(x, ffn1_norm, ffn1_w_in, ffn1_w_out, mix_norm, w_in, dn_conv_w, dn_a_log, dn_dt_bias, dn_out_norm, sb_q_norm, sb_k_norm, w_branch_a, w_branch_b, w_out, ffn2_norm, ffn2_w_in, ffn2_w_out, loss_target, m_ffn1_norm, m_ffn1_w_in, m_ffn1_w_out, m_mix_norm, m_w_in, m_dn_conv_w, m_dn_a_log, m_dn_dt_bias, m_dn_out_norm, m_sb_q_norm, m_sb_k_norm, m_w_branch_a, m_w_branch_b, m_w_out, m_ffn2_norm, m_ffn2_w_in, m_ffn2_w_out, v_ffn1_norm, v_ffn1_w_in, v_ffn1_w_out, v_mix_norm, v_w_in, v_dn_conv_w, v_dn_a_log, v_dn_dt_bias, v_dn_out_norm, v_sb_q_norm, v_sb_k_norm, v_w_branch_a, v_w_branch_b, v_w_out, v_ffn2_norm, v_ffn2_w_in, v_ffn2_w_out):
    w = dict(ffn1_norm=ffn1_norm, ffn1_w_in=ffn1_w_in, ffn1_w_out=ffn1_w_out, mix_norm=mix_norm, w_in=w_in,
             dn_conv_w=dn_conv_w, dn_a_log=dn_a_log, dn_dt_bias=dn_dt_bias, dn_out_norm=dn_out_norm,
             sb_q_norm=sb_q_norm, sb_k_norm=sb_k_norm, w_branch_a=w_branch_a, w_branch_b=w_branch_b, w_out=w_out,
             ffn2_norm=ffn2_norm, ffn2_w_in=ffn2_w_in, ffn2_w_out=ffn2_w_out)
    mom = dict(ffn1_norm=m_ffn1_norm, ffn1_w_in=m_ffn1_w_in, ffn1_w_out=m_ffn1_w_out, mix_norm=m_mix_norm, w_in=m_w_in,
               dn_conv_w=m_dn_conv_w, dn_a_log=m_dn_a_log, dn_dt_bias=m_dn_dt_bias, dn_out_norm=m_dn_out_norm,
               sb_q_norm=m_sb_q_norm, sb_k_norm=m_sb_k_norm, w_branch_a=m_w_branch_a, w_branch_b=m_w_branch_b,
               w_out=m_w_out, ffn2_norm=m_ffn2_norm, ffn2_w_in=m_ffn2_w_in, ffn2_w_out=m_ffn2_w_out)
    var = dict(ffn1_norm=v_ffn1_norm, ffn1_w_in=v_ffn1_w_in, ffn1_w_out=v_ffn1_w_out, mix_norm=v_mix_norm, w_in=v_w_in,
               dn_conv_w=v_dn_conv_w, dn_a_log=v_dn_a_log, dn_dt_bias=v_dn_dt_bias, dn_out_norm=v_dn_out_norm,
               sb_q_norm=v_sb_q_norm, sb_k_norm=v_sb_k_norm, w_branch_a=v_w_branch_a, w_branch_b=v_w_branch_b,
               w_out=v_w_out, ffn2_norm=v_ffn2_norm, ffn2_w_in=v_ffn2_w_in, ffn2_w_out=v_ffn2_w_out)

    _, t, d = x.shape
    depth = ffn1_norm.shape[0]
    n_heads = d // HEAD_DIM
    conv_cols = dn_conv_w.shape[2]
    assert d % HEAD_DIM == 0 and t % SB_BLOCK == 0 and 2 * n_heads <= LANES and depth % 2 == 0
    assert w_in.shape[2] * N_CHIPS == 9 * d + 2 * n_heads and conv_cols * N_CHIPS == 3 * d

    x_idx, y_idx, c_idx = _place()
    shard = 2 * x_idx + y_idx
    c_arr = jnp.reshape(c_idx, (1,)).astype(jnp.int32)
    s_arr = jnp.reshape(shard, (1,)).astype(jnp.int32)

    mine = {n: w[n].astype(BF16) for n in BIG}
    cut = 4 * d
    first = ("ffn1_w_in", "ffn1_w_out")
    rest = tuple(n for n in BIG if n not in first)

    def gather(names, l, collective_id):
        return dict(zip(names, _allgather_layer([mine[n] for n in names], l, collective_id)))

    arriving = [{**gather(first, 0, 0), **gather(rest, 0, depth)}] + [gather(BIG, l, l) for l in range(1, depth)]

    def layer_weights(l, names, after):
        gathered, after = lax.optimization_barrier(({n: arriving[l][n] for n in names}, after))
        full = {}
        for n in names:
            g = lax.dynamic_update_slice(gathered[n], mine[n][l][None], (shard, 0, 0))
            full[n] = _join_shards(n, g)
        if "w_in" in names:
            w_all = full["w_in"]
            full["w_main"] = jnp.concatenate([w_all[..., :cut], w_all[..., cut + 2 * n_heads:]], axis=-1)
            full["w_ba"] = jnp.pad(w_all[..., cut:cut + 2 * n_heads], ((0, 0), (0, 0), (0, LANES - 2 * n_heads)))
        return full, after

    conv_place = lax.dynamic_update_slice(jnp.zeros((depth, DN_CONV, 3 * d), F32), dn_conv_w, (0, 0, shard * conv_cols))
    conv_rows = _pad_small(conv_place.reshape(-1))
    conv_full = (0.5 * _allreduce_small(conv_rows, "allgather_conv")).reshape(-1)[:depth * DN_CONV * 3 * d]
    conv_full = jnp.pad(conv_full.reshape(depth, DN_CONV, 3 * d), ((0, 0), (0, CONV_ROWS - DN_CONV), (0, 0)))

    def head_row(vals):
        return jnp.pad(vals, (n_heads, LANES - 2 * n_heads)).reshape(1, LANES)

    saved, layers = [], []
    cur = x[0]
    for l in range(depth):
        full, x0 = layer_weights(l, first, cur)
        x1 = _ffn_fwd(x0, ffn1_norm[l][None], full["ffn1_w_in"], full["ffn1_w_out"], 0)
        later, x1 = layer_weights(l, rest, x1)
        full.update(later)
        layers.append(full)
        proj, ba = _proj_fwd(x1, mix_norm[l][None], full["w_main"], full["w_ba"], 0)
        act = _dn_prep_fwd(proj, conv_full[l], n_heads)
        alog, dtb = head_row(dn_a_log[l]), head_row(dn_dt_bias[l])
        oa, snaps = _delta_fwd(act, ba, alog, dtb, n_heads)
        qb, kb = _sb_prep_fwd(proj, sb_q_norm[l][None], sb_k_norm[l][None], n_heads)
        ob, ltot = _sb_attn_fwd(qb, kb, proj, n_heads)
        x2 = _merge_fwd(x1, oa, proj, ob, dn_out_norm[l][None], full["w_branch_a"], full["w_branch_b"],
                        full["w_out"], 0, n_heads)
        cur = _ffn_fwd(x2, ffn2_norm[l][None], full["ffn2_w_in"], full["ffn2_w_out"], 0)
        saved.append((x0, x1, proj, ba, act, alog, dtb, oa, snaps, qb, kb, ob, ltot, x2))

    dcur, loss_part = _loss_head(cur, loss_target[0])

    grads = {n: [None] * depth for n in WEIGHTS}
    reduced = {n: [None] * depth for n in BIG}
    pending = None

    def finish_reduce(l, parts, arrived, after):
        arrived, after = lax.optimization_barrier((arrived, after))
        halves = [_sum_partials(p, got, s_arr, c_arr) for p, got in zip(parts, arrived)]
        for n, g in zip(BIG, _join_halves(halves)):
            reduced[n][l] = g
        return after

    for l in reversed(range(depth)):
        x0, x1, proj, ba, act, alog, dtb, oa, snaps, qb, kb, ob, ltot, x2 = saved[l]
        full = layers[l]
        dx2, dg, dwi, dwo = _ffn_bwd(x2, ffn2_norm[l][None], dcur, full["ffn2_w_in"], full["ffn2_w_out"], 0)
        grads["ffn2_norm"][l] = dg[0]
        grads["ffn2_w_in"][l] = dwi
        grads["ffn2_w_out"][l] = _row_shards(dwo)
        doa, dz, dob, dga, dgb, dgn, dwa, dwb, dwout = _merge_bwd(
            oa, proj, ob, dx2, dn_out_norm[l][None], full["w_branch_a"], full["w_branch_b"], full["w_out"], 0,
            n_heads)
        grads["dn_out_norm"][l] = dgn[0]
        grads["w_branch_a"][l], grads["w_branch_b"][l] = _row_shards(dwa), _row_shards(dwb)
        grads["w_out"][l] = _row_shards(dwout)
        dqb, dkb, dvb = _sb_attn_bwd(qb, kb, proj, ltot, dob, n_heads)
        dsq, dsk, dqn, dkn = _sb_prep_bwd(proj, sb_q_norm[l][None], sb_k_norm[l][None], dqb, dkb, n_heads)
        grads["sb_q_norm"][l], grads["sb_k_norm"][l] = dqn[0], dkn[0]
        dact, dba, dal, ddt = _delta_bwd(act, ba, alog, dtb, snaps, doa, n_heads)
        grads["dn_a_log"][l] = dal[0, n_heads:2 * n_heads]
        grads["dn_dt_bias"][l] = ddt[0, n_heads:2 * n_heads]
        dqkv, dconv = _dn_prep_bwd(proj, conv_full[l], dact, n_heads)
        grads["dn_conv_w"][l] = dconv[:DN_CONV]
        dproj = jnp.concatenate([dqkv, dz, dsq, dsk, dvb.astype(BF16), dga, dgb], axis=1)
        dx1, dg, dwm, dwba = _proj_bwd(x1, mix_norm[l][None], dx2, dproj, dba, full["w_main"], full["w_ba"], 0)
        grads["mix_norm"][l] = dg[0]
        grads["w_in"][l] = _col_shards(
            jnp.concatenate([dwm[:, :cut], dwba[:, :2 * n_heads].astype(BF16), dwm[:, cut:]], axis=1))
        dcur, dg, dwi, dwo = _ffn_bwd(x0, ffn1_norm[l][None], dx1, full["ffn1_w_in"], full["ffn1_w_out"], 0)
        grads["ffn1_norm"][l] = dg[0]
        grads["ffn1_w_in"][l] = dwi
        grads["ffn1_w_out"][l] = _row_shards(dwo)

        if pending is not None:
            dcur = finish_reduce(*pending, dcur)
        g_major = [grads[n][l] for n in BIG]
        parts = [_add_half(g, got, c_arr) for g, got in zip(g_major, _swap_halves(g_major))]
        pending = (l, parts, _scatter_partials(parts, l, collective_id=depth + 1 + l))
    dcur = finish_reduce(*pending, dcur)
    final = {n: jnp.stack(reduced[n]) for n in BIG}
    grads = {n: jnp.stack(grads[n]) for n in SMALL + ("dn_conv_w",)}

    small_names = SMALL + ("dn_conv_w",)
    small_sizes = [int(np.prod(grads[n].shape)) for n in small_names]
    small_off = np.concatenate([[0], np.cumsum(small_sizes)])
    small = jnp.concatenate([grads[n].reshape(-1) for n in small_names] + [loss_part[0, :1]])
    small_sum = _allreduce_small(_pad_small(small), "allreduce_small").reshape(-1)
    for i, n in enumerate(small_names):
        final[n] = small_sum[small_off[i]:small_off[i + 1]].reshape(grads[n].shape)
    final["dn_conv_w"] = lax.dynamic_slice(final["dn_conv_w"], (0, 0, shard * conv_cols), (depth, DN_CONV, conv_cols))
    loss = small_sum[small_off[-1]]

    deltas, new_m, new_v = {}, {}, {}
    for n in WEIGHTS:
        shape = w[n].shape
        flat = (-1, shape[-1])
        dl, m2, v2 = _adamw(w[n].reshape(flat), final[n].reshape(flat), mom[n].reshape(flat), var[n].reshape(flat))
        deltas[n], new_m[n], new_v[n] = dl.reshape(shape), m2.reshape(shape), v2.reshape(shape)

    grad_x = dcur[None]
    return (loss, grad_x, *[final[n] for n in WEIGHTS], *[deltas[n] for n in WEIGHTS],
            *[new_m[n] for n in WEIGHTS], *[new_v[n] for n in WEIGHTS])
```

```python
import functools

import jax
import jax.numpy as jnp
import numpy as np
from jax import lax
from jax.experimental import pallas as pl
from jax.experimental.pallas import tpu as pltpu
from jax.experimental.pallas import tpu_sc as plsc

F32 = jnp.float32
BF16 = jnp.bfloat16

LANES = 128
HEAD_DIM = 128
DN_CHUNK = 64
DN_CONV = 4
CONV_ROWS = 8
SB_BLOCK = 128
FFN_HALF = 0.5
RMS_EPS = 1e-6
L2_EPS = 1e-6
NEG_BIG = -1e30
ADAM_LR = 0.001
ADAM_B1 = 0.9
ADAM_B2 = 0.999
ADAM_EPS = 1e-08
ADAM_WD = 0.01
ADAM_STEP = 10
VMEM_LIMIT = 56 * 1024 * 1024
N_CHIPS = 4
N_DEV = 8
MESH = pl.DeviceIdType.MESH

BIG = ("ffn1_w_in", "ffn1_w_out", "w_in", "w_branch_a", "w_branch_b", "w_out", "ffn2_w_in", "ffn2_w_out")
COL_SHARDED = ("ffn1_w_in", "w_in", "ffn2_w_in")
SMALL = ("ffn1_norm", "mix_norm", "dn_a_log", "dn_dt_bias", "dn_out_norm", "sb_q_norm", "sb_k_norm", "ffn2_norm")
WEIGHTS = ("ffn1_norm", "ffn1_w_in", "ffn1_w_out", "mix_norm", "w_in", "dn_conv_w", "dn_a_log", "dn_dt_bias",
           "dn_out_norm", "sb_q_norm", "sb_k_norm", "w_branch_a", "w_branch_b", "w_out", "ffn2_norm", "ffn2_w_in",
           "ffn2_w_out")


def _params(**kw):
    return pltpu.CompilerParams(vmem_limit_bytes=VMEM_LIMIT, **kw)


def _pick(n, options):
    for o in options:
        if n % o == 0:
            return o
    return n


def _const_spec(shape, single=False):
    nd = len(shape)
    if single:
        return pl.BlockSpec(shape, lambda *_: (0,) * nd, pipeline_mode=pl.Buffered(1))
    return pl.BlockSpec(shape, lambda *_: (0,) * nd)


_NN = ((1,), (0,))
_NT = ((1,), (1,))
_TN = ((0,), (0,))


def _dot(a, b, dims):
    return lax.dot_general(a.astype(BF16), b.astype(BF16), (dims, ((), ())), preferred_element_type=F32)


def _mm_nn(a, b):
    return _dot(a, b, _NN)


def _mm_nt(a, b):
    return _dot(a, b, _NT)


def _mm_tn(a, b):
    return _dot(a, b, _TN)


def _split(a):
    hi = a.astype(BF16)
    lo = (a - hi.astype(F32)).astype(BF16)
    return hi, lo


def _dot_precise(a, b, dims):
    dn = (dims, ((), ()))
    ah, al = _split(a)
    bh, bl = _split(b)
    out = lax.dot_general(ah, bh, dn, preferred_element_type=F32)
    out = out + lax.dot_general(ah, bl, dn, preferred_element_type=F32)
    return out + lax.dot_general(al, bh, dn, preferred_element_type=F32)


def _make_diff_mm(dot):
    @jax.custom_vjp
    def nn(a, b):
        return dot(a, b, _NN)

    @jax.custom_vjp
    def nt(a, b):
        return dot(a, b, _NT)

    @jax.custom_vjp
    def tn(a, b):
        return dot(a, b, _TN)

    nn.defvjp(lambda a, b: (dot(a, b, _NN), (a, b)), lambda r, g: (nt(g, r[1]), tn(r[0], g)))
    nt.defvjp(lambda a, b: (dot(a, b, _NT), (a, b)), lambda r, g: (nn(g, r[1]), tn(g, r[0])))
    tn.defvjp(lambda a, b: (dot(a, b, _TN), (a, b)), lambda r, g: (nt(r[1], g), nn(r[0], g)))
    return nn, nt, tn


_d_nn, _d_nt, _d_tn = _make_diff_mm(_dot)
_p_nn, _p_nt, _p_tn = _make_diff_mm(_dot_precise)


def _softplus_raw(x):
    return jnp.maximum(x, 0.0) + jnp.log(1.0 + jnp.exp(-jnp.abs(x)))


@jax.custom_vjp
def _softplus(x):
    return _softplus_raw(x)


_softplus.defvjp(lambda x: (_softplus_raw(x), x), lambda x, g: (g * jax.nn.sigmoid(x),))


def _rms(x, gain, eps):
    return x * lax.rsqrt(jnp.mean(x * x, axis=-1, keepdims=True) + eps) * gain


def _silu(x):
    return x * jax.nn.sigmoid(x)


def _shift_rows_raw(x, k, down):
    n = x.shape[0]
    row = lax.broadcasted_iota(jnp.int32, x.shape, 0)
    if down:
        return jnp.where(row >= k, pltpu.roll(x, k, 0), 0.0)
    return jnp.where(row < n - k, pltpu.roll(x, n - k, 0), 0.0)


@functools.partial(jax.custom_vjp, nondiff_argnums=(1,))
def _shift_down(x, k):
    return _shift_rows_raw(x, k, True)


_shift_down.defvjp(lambda x, k: (_shift_rows_raw(x, k, True), None),
                   lambda k, _, g: (_shift_rows_raw(g, k, False),))


def _layer_spec(layer, block, index_map, single=False):
    full_map = lambda *a: (layer,) + tuple(index_map(*a))
    if single:
        return pl.BlockSpec((None,) + block, full_map, pipeline_mode=pl.Buffered(1))
    return pl.BlockSpec((None,) + block, full_map)


def _ffn_fwd(x, gain, w_in, w_out, layer):
    t, d = x.shape
    f = w_out.shape[1]
    fc = _pick(f, (256, 128))
    nj = f // fc
    rt = _pick(t, (512, 256, 128))

    def body(x_ref, g_ref, wg_ref, wu_ref, wo_ref, o_ref, hs_ref):
        @pl.when(pl.program_id(0) == 0)
        def _():
            for r in range(t // rt):
                rows = pl.ds(r * rt, rt)
                xr = x_ref[rows, :]
                hs_ref[rows, :] = _rms(xr, g_ref[...], RMS_EPS).astype(BF16)
                o_ref[rows, :] = xr

        for r in range(t // rt):
            rows = pl.ds(r * rt, rt)
            h = hs_ref[rows, :]
            a = _mm_nn(h, wg_ref[...])
            b = _mm_nn(h, wu_ref[...])
            o_ref[rows, :] += FFN_HALF * _mm_nn(_silu(a) * b, wo_ref[...])

    return pl.pallas_call(
        body, name="ffn_fwd", grid=(nj,),
        in_specs=[_const_spec((t, d), True), _const_spec((1, d)),
                  _layer_spec(layer, (d, fc), lambda j: (0, j)), _layer_spec(layer, (d, fc), lambda j: (0, nj + j)),
                  _layer_spec(layer, (fc, d), lambda j: (j, 0))],
        out_specs=_const_spec((t, d)),
        out_shape=jax.ShapeDtypeStruct((t, d), F32),
        scratch_shapes=[pltpu.VMEM((t, d), BF16)],
        compiler_params=_params(dimension_semantics=("arbitrary",)),
    )(x, gain, w_in, w_in, w_out)


def _ffn_bwd(x, gain, dy, w_in, w_out, layer):
    t, d = x.shape
    f = w_out.shape[1]
    fc = _pick(f, (256, 128))
    nj = f // fc
    rt = _pick(t, (512, 256, 128))
    nr = t // rt

    def body(x_ref, g_ref, dy_ref, wg_ref, wu_ref, wo_ref, dx_ref, dg_ref, dwi_ref, dwo_ref,
             hs_ref, dwg_acc, dwu_acc, dwo_acc):
        j = pl.program_id(0)

        @pl.when(j == 0)
        def _():
            for r in range(nr):
                rows = pl.ds(r * rt, rt)
                hs_ref[rows, :] = _rms(x_ref[rows, :], g_ref[...], RMS_EPS).astype(BF16)
                dx_ref[rows, :] = jnp.zeros((rt, d), F32)

        for r in range(nr):
            rows = pl.ds(r * rt, rt)
            h = hs_ref[rows, :]
            dy2 = (FFN_HALF * dy_ref[rows, :]).astype(BF16)
            a = _mm_nn(h, wg_ref[...])
            b = _mm_nn(h, wu_ref[...])
            sig = jax.nn.sigmoid(a)
            sa = a * sig
            ds = _mm_nt(dy2, wo_ref[...])
            da = ds * b * (sig * (1.0 + a * (1.0 - sig)))
            db = ds * sa
            dx_ref[rows, :] += _mm_nt(da, wg_ref[...]) + _mm_nt(db, wu_ref[...])
            dwo_c = _mm_tn(sa * b, dy2)
            dwg_c = _mm_tn(h, da)
            dwu_c = _mm_tn(h, db)
            if r == 0:
                dwo_acc[...] = dwo_c
                dwg_acc[...] = dwg_c
                dwu_acc[...] = dwu_c
            else:
                dwo_acc[...] += dwo_c
                dwg_acc[...] += dwg_c
                dwu_acc[...] += dwu_c
        dwi_ref[0] = dwg_acc[...].astype(BF16)
        dwi_ref[1] = dwu_acc[...].astype(BF16)
        dwo_ref[...] = dwo_acc[...].astype(BF16)

        @pl.when(j == nj - 1)
        def _():
            for r in range(nr):
                rows = pl.ds(r * rt, rt)
                _, vjp = jax.vjp(lambda xx, gg: _rms(xx, gg, RMS_EPS), x_ref[rows, :], g_ref[...])
                dxn, dgr = vjp(dx_ref[rows, :])
                dx_ref[rows, :] = dy_ref[rows, :] + dxn
                if r == 0:
                    dg_ref[...] = dgr
                else:
                    dg_ref[...] += dgr

    return pl.pallas_call(
        body, name="ffn_bwd", grid=(nj,),
        in_specs=[_const_spec((t, d), True), _const_spec((1, d)), _const_spec((t, d), True),
                  _layer_spec(layer, (d, fc), lambda j: (0, j)), _layer_spec(layer, (d, fc), lambda j: (0, nj + j)),
                  _layer_spec(layer, (fc, d), lambda j: (j, 0))],
        out_specs=[_const_spec((t, d)), _const_spec((1, d)),
                   pl.BlockSpec((2, d, fc), lambda j: (0, 0, j)), pl.BlockSpec((fc, d), lambda j: (j, 0))],
        out_shape=[jax.ShapeDtypeStruct((t, d), F32), jax.ShapeDtypeStruct((1, d), F32),
                   jax.ShapeDtypeStruct((2, d, f), BF16), jax.ShapeDtypeStruct((f, d), BF16)],
        scratch_shapes=[pltpu.VMEM((t, d), BF16), pltpu.VMEM((d, fc), F32), pltpu.VMEM((d, fc), F32),
                        pltpu.VMEM((fc, d), F32)],
        compiler_params=_params(dimension_semantics=("arbitrary",)),
    )(x, gain, dy, w_in, w_in, w_out)


def _proj_fwd(x, gain, w, wba, layer):
    t, d = x.shape
    n = w.shape[2]
    nc = _pick(n, (512, 256, 128))
    rt = _pick(t, (512, 256, 128))

    def body(x_ref, g_ref, w_ref, wba_ref, p_ref, ba_ref, hs_ref):
        @pl.when(pl.program_id(0) == 0)
        def _():
            for r in range(t // rt):
                rows = pl.ds(r * rt, rt)
                h = _rms(x_ref[rows, :], g_ref[...], RMS_EPS).astype(BF16)
                hs_ref[rows, :] = h
                ba_ref[rows, :] = _mm_nn(h, wba_ref[...])

        for r in range(t // rt):
            rows = pl.ds(r * rt, rt)
            p_ref[rows, :] = _mm_nn(hs_ref[rows, :], w_ref[...])

    return pl.pallas_call(
        body, name="proj_fwd", grid=(n // nc,),
        in_specs=[_const_spec((t, d), True), _const_spec((1, d)),
                  _layer_spec(layer, (d, nc), lambda j: (0, j)), _layer_spec(layer, (d, LANES), lambda j: (0, 0))],
        out_specs=[pl.BlockSpec((t, nc), lambda j: (0, j)), _const_spec((t, LANES))],
        out_shape=[jax.ShapeDtypeStruct((t, n), F32), jax.ShapeDtypeStruct((t, LANES), F32)],
        scratch_shapes=[pltpu.VMEM((t, d), BF16)],
        compiler_params=_params(dimension_semantics=("arbitrary",)),
    )(x, gain, w, wba)


def _proj_bwd(x, gain, dres, dp, dba, w, wba, layer):
    t, d = x.shape
    n = w.shape[2]
    nc = _pick(n, (512, 256, 128))
    nj = n // nc
    rt = _pick(t, (512, 256, 128))
    nr = t // rt

    def body(x_ref, g_ref, dres_ref, dp_ref, dba_ref, w_ref, wba_ref, dx_ref, dg_ref, dw_ref, dwba_ref, hs_ref, dw_acc):
        j = pl.program_id(0)

        @pl.when(j == 0)
        def _():
            for r in range(nr):
                rows = pl.ds(r * rt, rt)
                h = _rms(x_ref[rows, :], g_ref[...], RMS_EPS).astype(BF16)
                hs_ref[rows, :] = h
                g = dba_ref[rows, :]
                dx_ref[rows, :] = _mm_nt(g, wba_ref[...])
                if r == 0:
                    dwba_ref[...] = _mm_tn(h, g)
                else:
                    dwba_ref[...] += _mm_tn(h, g)

        for r in range(nr):
            rows = pl.ds(r * rt, rt)
            g = dp_ref[rows, :]
            dx_ref[rows, :] += _mm_nt(g, w_ref[...])
            if r == 0:
                dw_acc[...] = _mm_tn(hs_ref[rows, :], g)
            else:
                dw_acc[...] += _mm_tn(hs_ref[rows, :], g)
        dw_ref[...] = dw_acc[...].astype(BF16)

        @pl.when(j == nj - 1)
        def _():
            for r in range(nr):
                rows = pl.ds(r * rt, rt)
                _, vjp = jax.vjp(lambda xx, gg: _rms(xx, gg, RMS_EPS), x_ref[rows, :], g_ref[...])
                dxn, dgr = vjp(dx_ref[rows, :])
                dx_ref[rows, :] = dres_ref[rows, :] + dxn
                if r == 0:
                    dg_ref[...] = dgr
                else:
                    dg_ref[...] += dgr

    return pl.pallas_call(
        body, name="proj_bwd", grid=(nj,),
        in_specs=[_const_spec((t, d), True), _const_spec((1, d)), _const_spec((t, d), True),
                  pl.BlockSpec((t, nc), lambda j: (0, j)), _const_spec((t, LANES)),
                  _layer_spec(layer, (d, nc), lambda j: (0, j)), _layer_spec(layer, (d, LANES), lambda j: (0, 0))],
        out_specs=[_const_spec((t, d)), _const_spec((1, d)),
                   pl.BlockSpec((d, nc), lambda j: (0, j)), _const_spec((d, LANES))],
        out_shape=[jax.ShapeDtypeStruct((t, d), F32), jax.ShapeDtypeStruct((1, d), F32),
                   jax.ShapeDtypeStruct((d, n), BF16), jax.ShapeDtypeStruct((d, LANES), F32)],
        scratch_shapes=[pltpu.VMEM((t, d), BF16), pltpu.VMEM((d, nc), F32)],
        compiler_params=_params(dimension_semantics=("arbitrary",)),
    )(x, gain, dres, dp, dba, w, wba)


def _conv_act(x, w0, w1, w2, w3, is_qk):
    y = w3 * x + w2 * _shift_down(x, 1) + w1 * _shift_down(x, 2) + w0 * _shift_down(x, 3)
    y = _silu(y)
    inv = lax.rsqrt(jnp.sum(y * y, axis=-1, keepdims=True) + L2_EPS)
    return y * (is_qk * inv + (1.0 - is_qk))


def _taps(w_ref):
    return tuple(w_ref[i:i + 1, :] for i in range(DN_CONV))


def _dn_prep_fwd(proj, conv_w, n_heads):
    t = proj.shape[0]
    nb = 3 * n_heads

    def body(x_ref, w_ref, o_ref):
        is_qk = jnp.where(pl.program_id(0) < 2 * n_heads, 1.0, 0.0).astype(F32)
        o_ref[...] = _conv_act(x_ref[...], *_taps(w_ref), is_qk)

    return pl.pallas_call(
        body, name="dn_prep_fwd", grid=(nb,),
        in_specs=[pl.BlockSpec((t, HEAD_DIM), lambda i: (0, i)), pl.BlockSpec((CONV_ROWS, HEAD_DIM), lambda i: (0, i))],
        out_specs=pl.BlockSpec((t, HEAD_DIM), lambda i: (0, i)),
        out_shape=jax.ShapeDtypeStruct((t, nb * HEAD_DIM), F32),
        compiler_params=_params(dimension_semantics=("arbitrary",)),
    )(proj, conv_w)


def _dn_prep_bwd(proj, conv_w, dact, n_heads):
    t = proj.shape[0]
    nb = 3 * n_heads

    def body(x_ref, w_ref, g_ref, dx_ref, dw_ref):
        is_qk = jnp.where(pl.program_id(0) < 2 * n_heads, 1.0, 0.0).astype(F32)
        _, vjp = jax.vjp(lambda x, a, b, c, e: _conv_act(x, a, b, c, e, is_qk), x_ref[...], *_taps(w_ref))
        dx, d0, d1, d2, d3 = vjp(g_ref[...])
        dx_ref[...] = dx.astype(BF16)
        dw_ref[...] = jnp.concatenate([d0, d1, d2, d3, jnp.zeros((CONV_ROWS - DN_CONV, HEAD_DIM), F32)], axis=0)

    return pl.pallas_call(
        body, name="dn_prep_bwd", grid=(nb,),
        in_specs=[pl.BlockSpec((t, HEAD_DIM), lambda i: (0, i)), pl.BlockSpec((CONV_ROWS, HEAD_DIM), lambda i: (0, i)),
                  pl.BlockSpec((t, HEAD_DIM), lambda i: (0, i))],
        out_specs=[pl.BlockSpec((t, HEAD_DIM), lambda i: (0, i)), pl.BlockSpec((CONV_ROWS, HEAD_DIM), lambda i: (0, i))],
        out_shape=[jax.ShapeDtypeStruct((t, nb * HEAD_DIM), BF16), jax.ShapeDtypeStruct((CONV_ROWS, nb * HEAD_DIM), F32)],
        compiler_params=_params(dimension_semantics=("arbitrary",)),
    )(proj, conv_w, dact)


def _unit_lower_inverses(lmats, c):
    r = lax.broadcasted_iota(jnp.int32, (c, c), 0)
    q = lax.broadcasted_iota(jnp.int32, (c, c), 1)
    eye = jnp.where(r == q, 1.0, 0.0)
    ps = [eye - l for l in lmats]
    ms = [_p_nn(l, l) for l in lmats]
    n = 2
    while True:
        ps = [p + _p_nn(p, m) for p, m in zip(ps, ms)]
        if 2 * n >= c:
            return ps
        ms = [_p_nn(m, m) for m in ms]
        n *= 2


def _delta_heads(qs, ks, vs, bg, alog, dtb, states):
    n_heads = len(qs)
    heads = range(n_heads)
    c = qs[0].shape[0]
    lane = lax.broadcasted_iota(jnp.int32, (c, LANES), 1)
    r = lax.broadcasted_iota(jnp.int32, (c, c), 0)
    s = lax.broadcasted_iota(jnp.int32, (c, c), 1)
    beta_all = jax.nn.sigmoid(bg)
    g_all = -jnp.exp(alog) * _softplus(bg + dtb)
    beta = [jnp.sum(jnp.where(lane == h, beta_all, 0.0), axis=1, keepdims=True) for h in heads]
    g = [jnp.sum(jnp.where(lane == n_heads + h, g_all, 0.0), axis=1, keepdims=True) for h in heads]
    g_row = [jnp.sum(jnp.where(r == s, g[h], 0.0), axis=0, keepdims=True) for h in heads]
    gc = [jnp.sum(jnp.where(s <= r, g_row[h], 0.0), axis=1, keepdims=True) for h in heads]
    gr = [jnp.sum(jnp.where(r <= s, g[h], 0.0), axis=0, keepdims=True) for h in heads]
    g_last = [jnp.sum(g[h], axis=0, keepdims=True) for h in heads]
    decay = [jnp.exp(jnp.where(r >= s, gc[h] - gr[h], NEG_BIG)) for h in heads]
    q_scaled = [qs[h] * (HEAD_DIM ** -0.5) for h in heads]
    k_beta = [ks[h] * beta[h] for h in heads]
    lmat = [jnp.where(r > s, _d_nt(k_beta[h], ks[h]) * decay[h], 0.0) for h in heads]
    attn = [_d_nt(q_scaled[h], ks[h]) * decay[h] for h in heads]
    tinv = _unit_lower_inverses(lmat, c)
    u = [_p_nn(tinv[h], vs[h] * beta[h]) for h in heads]
    w = [_p_nn(tinv[h], k_beta[h] * jnp.exp(gc[h])) for h in heads]
    v_new = [u[h] - _d_nn(w[h], states[h]) for h in heads]
    o_state = [_d_nn(q_scaled[h] * jnp.exp(gc[h]), states[h]) for h in heads]
    o = [o_state[h] + _d_nn(attn[h], v_new[h]) for h in heads]
    kv = [_d_tn(ks[h] * jnp.exp(g_last[h] - gc[h]), v_new[h]) for h in heads]
    new_states = [states[h] * jnp.exp(g_last[h]) + kv[h] for h in heads]
    return tuple(o), tuple(new_states)


def _delta_fwd(act, ba, alog, dtb, n_heads):
    t = act.shape[0]
    d = n_heads * HEAD_DIM
    c = DN_CHUNK
    nc = t // c

    def body(q_ref, k_ref, v_ref, bg_ref, al_ref, dt_ref, o_ref, snap_ref, st_ref):
        @pl.when(pl.program_id(0) == 0)
        def _():
            st_ref[...] = jnp.zeros(st_ref.shape, F32)

        snap_ref[0] = st_ref[...]
        cols = [slice(h * HEAD_DIM, (h + 1) * HEAD_DIM) for h in range(n_heads)]
        os, new_states = _delta_heads([q_ref[:, sl] for sl in cols], [k_ref[:, sl] for sl in cols],
                                      [v_ref[:, sl] for sl in cols], bg_ref[...], al_ref[...], dt_ref[...],
                                      [st_ref[h] for h in range(n_heads)])
        for h, sl in enumerate(cols):
            o_ref[:, sl] = os[h]
            st_ref[h] = new_states[h]

    return pl.pallas_call(
        body, name="delta_fwd", grid=(nc,),
        in_specs=[pl.BlockSpec((c, d), lambda i: (i, 0)), pl.BlockSpec((c, d), lambda i: (i, 1)),
                  pl.BlockSpec((c, d), lambda i: (i, 2)), pl.BlockSpec((c, LANES), lambda i: (i, 0)),
                  _const_spec((1, LANES)), _const_spec((1, LANES))],
        out_specs=[pl.BlockSpec((c, d), lambda i: (i, 0)),
                   pl.BlockSpec((1, n_heads, HEAD_DIM, HEAD_DIM), lambda i: (i, 0, 0, 0))],
        out_shape=[jax.ShapeDtypeStruct((t, d), F32), jax.ShapeDtypeStruct((nc, n_heads, HEAD_DIM, HEAD_DIM), F32)],
        scratch_shapes=[pltpu.VMEM((n_heads, HEAD_DIM, HEAD_DIM), F32)],
        compiler_params=_params(dimension_semantics=("arbitrary",)),
    )(act, act, act, ba, alog, dtb)


def _delta_bwd(act, ba, alog, dtb, snaps, do, n_heads):
    t = act.shape[0]
    d = n_heads * HEAD_DIM
    c = DN_CHUNK
    nc = t // c

    def body(q_ref, k_ref, v_ref, bg_ref, al_ref, dt_ref, snap_ref, do_ref,
             dact_ref, dbg_ref, dal_ref, ddt_ref, ds_ref):
        @pl.when(pl.program_id(0) == 0)
        def _():
            ds_ref[...] = jnp.zeros(ds_ref.shape, F32)
            dal_ref[...] = jnp.zeros((1, LANES), F32)
            ddt_ref[...] = jnp.zeros((1, LANES), F32)

        heads = range(n_heads)
        cols = [slice(h * HEAD_DIM, (h + 1) * HEAD_DIM) for h in heads]
        _, vjp = jax.vjp(_delta_heads, tuple(q_ref[:, sl] for sl in cols), tuple(k_ref[:, sl] for sl in cols),
                         tuple(v_ref[:, sl] for sl in cols), bg_ref[...], al_ref[...], dt_ref[...],
                         tuple(snap_ref[0, h] for h in heads))
        dq, dk, dv, dbg, dal, ddt, dst = vjp((tuple(do_ref[:, sl] for sl in cols), tuple(ds_ref[h] for h in heads)))
        for h, sl in enumerate(cols):
            dact_ref[:, sl] = dq[h]
            dact_ref[:, d + h * HEAD_DIM:d + (h + 1) * HEAD_DIM] = dk[h]
            dact_ref[:, 2 * d + h * HEAD_DIM:2 * d + (h + 1) * HEAD_DIM] = dv[h]
            ds_ref[h] = dst[h]
        dal_ref[...] += dal
        ddt_ref[...] += ddt
        dbg_ref[...] = dbg.astype(BF16)

    rev = lambda i: nc - 1 - i
    return pl.pallas_call(
        body, name="delta_bwd", grid=(nc,),
        in_specs=[pl.BlockSpec((c, d), lambda i: (rev(i), 0)), pl.BlockSpec((c, d), lambda i: (rev(i), 1)),
                  pl.BlockSpec((c, d), lambda i: (rev(i), 2)), pl.BlockSpec((c, LANES), lambda i: (rev(i), 0)),
                  _const_spec((1, LANES)), _const_spec((1, LANES)),
                  pl.BlockSpec((1, n_heads, HEAD_DIM, HEAD_DIM), lambda i: (rev(i), 0, 0, 0)),
                  pl.BlockSpec((c, d), lambda i: (rev(i), 0))],
        out_specs=[pl.BlockSpec((c, 3 * d), lambda i: (rev(i), 0)), pl.BlockSpec((c, LANES), lambda i: (rev(i), 0)),
                   _const_spec((1, LANES)), _const_spec((1, LANES))],
        out_shape=[jax.ShapeDtypeStruct((t, 3 * d), F32), jax.ShapeDtypeStruct((t, LANES), BF16),
                   jax.ShapeDtypeStruct((1, LANES), F32), jax.ShapeDtypeStruct((1, LANES), F32)],
        scratch_shapes=[pltpu.VMEM((n_heads, HEAD_DIM, HEAD_DIM), F32)],
        compiler_params=_params(dimension_semantics=("arbitrary",)),
    )(act, act, act, ba, alog, dtb, snaps, do)


def _head_norm2(a, b, ga, gb):
    return _rms(a, ga, RMS_EPS), _rms(b, gb, RMS_EPS)


def _sb_prep_fwd(proj, qn, kn, n_heads):
    t = proj.shape[0]
    d = n_heads * HEAD_DIM
    tm = _pick(t, (256, 128))

    def body(q_ref, k_ref, qn_ref, kn_ref, qo_ref, ko_ref):
        for h in range(n_heads):
            sl = slice(h * HEAD_DIM, (h + 1) * HEAD_DIM)
            qo_ref[:, sl], ko_ref[:, sl] = _head_norm2(q_ref[:, sl], k_ref[:, sl], qn_ref[...], kn_ref[...])

    return pl.pallas_call(
        body, name="sb_prep_fwd", grid=(t // tm,),
        in_specs=[pl.BlockSpec((tm, d), lambda i: (i, 4)), pl.BlockSpec((tm, d), lambda i: (i, 5)),
                  _const_spec((1, HEAD_DIM)), _const_spec((1, HEAD_DIM))],
        out_specs=[pl.BlockSpec((tm, d), lambda i: (i, 0)), pl.BlockSpec((tm, d), lambda i: (i, 0))],
        out_shape=[jax.ShapeDtypeStruct((t, d), F32), jax.ShapeDtypeStruct((t, d), F32)],
        compiler_params=_params(dimension_semantics=("arbitrary",)),
    )(proj, proj, qn, kn)


def _sb_prep_bwd(proj, qn, kn, dq, dk, n_heads):
    t = proj.shape[0]
    d = n_heads * HEAD_DIM
    tm = _pick(t, (256, 128))

    def body(q_ref, k_ref, qn_ref, kn_ref, dq_ref, dk_ref, dqo_ref, dko_ref, dqn_ref, dkn_ref):
        @pl.when(pl.program_id(0) == 0)
        def _():
            dqn_ref[...] = jnp.zeros((1, HEAD_DIM), F32)
            dkn_ref[...] = jnp.zeros((1, HEAD_DIM), F32)

        for h in range(n_heads):
            sl = slice(h * HEAD_DIM, (h + 1) * HEAD_DIM)
            _, vjp = jax.vjp(_head_norm2, q_ref[:, sl], k_ref[:, sl], qn_ref[...], kn_ref[...])
            da, db, dga, dgb = vjp((dq_ref[:, sl], dk_ref[:, sl]))
            dqo_ref[:, sl] = da.astype(BF16)
            dko_ref[:, sl] = db.astype(BF16)
            dqn_ref[...] += dga
            dkn_ref[...] += dgb

    return pl.pallas_call(
        body, name="sb_prep_bwd", grid=(t // tm,),
        in_specs=[pl.BlockSpec((tm, d), lambda i: (i, 4)), pl.BlockSpec((tm, d), lambda i: (i, 5)),
                  _const_spec((1, HEAD_DIM)), _const_spec((1, HEAD_DIM)),
                  pl.BlockSpec((tm, d), lambda i: (i, 0)), pl.BlockSpec((tm, d), lambda i: (i, 0))],
        out_specs=[pl.BlockSpec((tm, d), lambda i: (i, 0)), pl.BlockSpec((tm, d), lambda i: (i, 0)),
                   _const_spec((1, HEAD_DIM)), _const_spec((1, HEAD_DIM))],
        out_shape=[jax.ShapeDtypeStruct((t, d), BF16), jax.ShapeDtypeStruct((t, d), BF16),
                   jax.ShapeDtypeStruct((1, HEAD_DIM), F32), jax.ShapeDtypeStruct((1, HEAD_DIM), F32)],
        compiler_params=_params(dimension_semantics=("arbitrary",)),
    )(proj, proj, qn, kn, dq, dk)


def _cumsum_mm(x, tri):
    hi, lo = _split(x)
    return (lax.dot_general(hi, tri, (_NN, ((), ())), preferred_element_type=F32)
            + lax.dot_general(lo, tri, (_NN, ((), ())), preferred_element_type=F32))


def _sb_valid(i, j):
    row = lax.broadcasted_iota(jnp.int32, (SB_BLOCK, SB_BLOCK), 0)
    col = lax.broadcasted_iota(jnp.int32, (SB_BLOCK, SB_BLOCK), 1)
    return (col + j * SB_BLOCK) < (row + i * SB_BLOCK)


def _sb_attn_fwd(qb, kb, proj, n_heads):
    t = qb.shape[0]
    d = n_heads * HEAD_DIM
    nq = t // SB_BLOCK
    hb = _pick(n_heads, (4, 2, 1))
    wide = hb * HEAD_DIM
    v_col0 = 6 * n_heads // hb
    scale = HEAD_DIM ** -0.5

    def body(q_ref, k_ref, v_ref, o_ref, lt_ref):
        i = pl.program_id(1)
        row = lax.broadcasted_iota(jnp.int32, (SB_BLOCK, SB_BLOCK), 0)
        col = lax.broadcasted_iota(jnp.int32, (SB_BLOCK, SB_BLOCK), 1)
        after = jnp.where(row > col, 1.0, 0.0).astype(BF16)
        heads = [slice(h * HEAD_DIM, (h + 1) * HEAD_DIM) for h in range(hb)]
        every = range(hb)
        qs = [q_ref[:, sl].astype(BF16) for sl in heads]

        def step(m, carry):
            j0 = i - 2 * m
            js = (j0, jnp.maximum(j0 - 1, 0))
            valid = (_sb_valid(i, js[0]), jnp.logical_and(_sb_valid(i, js[1]), j0 >= 1))
            rows = [pl.ds(pl.multiple_of(j * SB_BLOCK, SB_BLOCK), SB_BLOCK) for j in js]
            units = [(h, b) for b in range(2) for h in every]
            z = {u: _mm_nt(qs[u[0]], k_ref[rows[u[1]], heads[u[0]]]) * scale for u in units}
            sp = {u: _softplus_raw(z[u]) for u in units}
            lm = {u: jnp.where(valid[u[1]], -sp[u], 0.0) for u in units}
            tail = {u: _cumsum_mm(lm[u], after) for u in units}
            later = {(h, 0): carry[h][1] for h in every}
            later.update({(h, 1): carry[h][1] + jnp.sum(lm[h, 0], axis=1, keepdims=True) for h in every})
            w = {u: jnp.where(valid[u[1]], jnp.exp(z[u] - sp[u] + later[u] + tail[u]), 0.0) for u in units}
            pv = {u: _mm_nn(w[u], v_ref[rows[u[1]], heads[u[0]]]) for u in units}
            return tuple((carry[h][0] + pv[h, 0] + pv[h, 1], later[h, 1] + jnp.sum(lm[h, 1], axis=1, keepdims=True))
                         for h in every)

        init = tuple((jnp.zeros((SB_BLOCK, HEAD_DIM), F32), jnp.zeros((SB_BLOCK, 1), F32)) for _ in heads)
        res = lax.fori_loop(0, (i + 2) // 2, step, init)
        for h, sl in enumerate(heads):
            o_ref[:, sl] = res[h][0]
            lt_ref[:, sl] = jnp.broadcast_to(res[h][1], (SB_BLOCK, HEAD_DIM))

    return pl.pallas_call(
        body, name="sb_attn_fwd", grid=(n_heads // hb, nq),
        in_specs=[pl.BlockSpec((SB_BLOCK, wide), lambda g, i: (i, g)),
                  pl.BlockSpec((t, wide), lambda g, i: (0, g)),
                  pl.BlockSpec((t, wide), lambda g, i: (0, v_col0 + g))],
        out_specs=[pl.BlockSpec((SB_BLOCK, wide), lambda g, i: (i, g)),
                   pl.BlockSpec((SB_BLOCK, wide), lambda g, i: (i, g))],
        out_shape=[jax.ShapeDtypeStruct((t, d), F32), jax.ShapeDtypeStruct((t, d), F32)],
        compiler_params=_params(dimension_semantics=("arbitrary", "arbitrary")),
    )(qb, kb, proj)


def _sb_attn_bwd(qb, kb, proj, ltot, do, n_heads):
    t = qb.shape[0]
    d = n_heads * HEAD_DIM
    nq = t // SB_BLOCK
    hb = _pick(n_heads, (4, 2, 1))
    wide = hb * HEAD_DIM
    v_col0 = 6 * n_heads // hb
    scale = HEAD_DIM ** -0.5

    def body(q_ref, k_ref, v_ref, lt_ref, do_ref, dq_ref, dk_ref, dv_ref):
        i = pl.program_id(1)

        @pl.when(i == 0)
        def _():
            dk_ref[...] = jnp.zeros((t, wide), F32)
            dv_ref[...] = jnp.zeros((t, wide), F32)

        row = lax.broadcasted_iota(jnp.int32, (SB_BLOCK, SB_BLOCK), 0)
        col = lax.broadcasted_iota(jnp.int32, (SB_BLOCK, SB_BLOCK), 1)
        upto = jnp.where(row <= col, 1.0, 0.0).astype(BF16)
        before = jnp.where(row < col, 1.0, 0.0).astype(BF16)
        heads = [slice(h * HEAD_DIM, (h + 1) * HEAD_DIM) for h in range(hb)]
        every = range(hb)
        qs = [q_ref[:, sl].astype(BF16) for sl in heads]
        dos = [do_ref[:, sl].astype(BF16) for sl in heads]
        totals = [jnp.max(lt_ref[:, sl], axis=1, keepdims=True) for sl in heads]

        def step(m, carry):
            js = (2 * m, jnp.minimum(2 * m + 1, nq - 1))
            valid = (_sb_valid(i, js[0]), jnp.logical_and(_sb_valid(i, js[1]), 2 * m + 1 <= i))
            rows = [pl.ds(pl.multiple_of(j * SB_BLOCK, SB_BLOCK), SB_BLOCK) for j in js]
            units = [(h, b) for b in range(2) for h in every]
            kj = {u: k_ref[rows[u[1]], heads[u[0]]].astype(BF16) for u in units}
            vj = {u: v_ref[rows[u[1]], heads[u[0]]].astype(BF16) for u in units}
            z = {u: _mm_nt(qs[u[0]], kj[u]) * scale for u in units}
            dw = {u: _mm_nt(dos[u[0]], vj[u]) for u in units}
            sp = {u: _softplus_raw(z[u]) for u in units}
            lm = {u: jnp.where(valid[u[1]], -sp[u], 0.0) for u in units}
            head = {u: _cumsum_mm(lm[u], upto) for u in units}
            lm_before = {(h, 0): carry[h][1] for h in every}
            lm_before.update({(h, 1): carry[h][1] + jnp.sum(lm[h, 0], axis=1, keepdims=True) for h in every})
            w = {u: jnp.where(valid[u[1]], jnp.exp(z[u] - sp[u] + totals[u[0]] - (lm_before[u] + head[u])), 0.0)
                 for u in units}
            e = {u: w[u] * dw[u] for u in units}
            e_local = {u: _mm_nn(e[u], before) for u in units}
            e_before = {(h, 0): carry[h][2] for h in every}
            e_before.update({(h, 1): carry[h][2] + jnp.sum(e[h, 0], axis=1, keepdims=True) for h in every})
            sig = {u: jnp.exp(z[u] - sp[u]) for u in units}
            dz = {u: jnp.where(valid[u[1]], e[u] * (1.0 - sig[u]) - (e_before[u] + e_local[u]) * sig[u], 0.0) * scale
                  for u in units}
            for h, b in units:
                dv_ref[rows[b], heads[h]] += _mm_tn(w[h, b], dos[h])
            for h, b in units:
                dk_ref[rows[b], heads[h]] += _mm_tn(dz[h, b], qs[h])
            dq = {u: _mm_nn(dz[u], kj[u]) for u in units}
            return tuple((carry[h][0] + dq[h, 0] + dq[h, 1],
                          lm_before[h, 1] + jnp.sum(lm[h, 1], axis=1, keepdims=True),
                          e_before[h, 1] + jnp.sum(e[h, 1], axis=1, keepdims=True)) for h in every)

        zero_col = jnp.zeros((SB_BLOCK, 1), F32)
        init = tuple((jnp.zeros((SB_BLOCK, HEAD_DIM), F32), zero_col, zero_col) for _ in heads)
        res = lax.fori_loop(0, (i + 2) // 2, step, init)
        for h, sl in enumerate(heads):
            dq_ref[:, sl] = res[h][0]

    return pl.pallas_call(
        body, name="sb_attn_bwd", grid=(n_heads // hb, nq),
        in_specs=[pl.BlockSpec((SB_BLOCK, wide), lambda g, i: (i, g)),
                  pl.BlockSpec((t, wide), lambda g, i: (0, g)),
                  pl.BlockSpec((t, wide), lambda g, i: (0, v_col0 + g)),
                  pl.BlockSpec((SB_BLOCK, wide), lambda g, i: (i, g)),
                  pl.BlockSpec((SB_BLOCK, wide), lambda g, i: (i, g))],
        out_specs=[pl.BlockSpec((SB_BLOCK, wide), lambda g, i: (i, g)),
                   pl.BlockSpec((t, wide), lambda g, i: (0, g)),
                   pl.BlockSpec((t, wide), lambda g, i: (0, g))],
        out_shape=[jax.ShapeDtypeStruct((t, d), F32), jax.ShapeDtypeStruct((t, d), F32),
                   jax.ShapeDtypeStruct((t, d), F32)],
        compiler_params=_params(dimension_semantics=("arbitrary", "arbitrary")),
    )(qb, kb, proj, ltot, do)


def _gated_norm(oa, z, gn):
    return _rms(oa, gn, RMS_EPS) * _silu(z)


def _merge_gates(ya, yb, ga, gb):
    return jax.nn.sigmoid(ga) * ya + jax.nn.sigmoid(gb) * yb


def _merge_fwd(x1, oa, proj, ob, gn, wa, wb, wo, layer, n_heads):
    t, d = x1.shape
    tm = _pick(t, (256, 128))
    square = _layer_spec(layer, (d, d), lambda i: (0, 0), single=True)

    def body(x_ref, oa_ref, z_ref, ob_ref, ga_ref, gb_ref, gn_ref, wa_ref, wb_ref, wo_ref, o_ref, na_ref):
        for h in range(n_heads):
            sl = slice(h * HEAD_DIM, (h + 1) * HEAD_DIM)
            na_ref[:, sl] = _gated_norm(oa_ref[:, sl], z_ref[:, sl], gn_ref[...]).astype(BF16)
        m = _merge_gates(_mm_nn(na_ref[...], wa_ref[...]), _mm_nn(ob_ref[...], wb_ref[...]), ga_ref[...], gb_ref[...])
        o_ref[...] = x_ref[...] + _mm_nn(m, wo_ref[...])

    tile = lambda k: pl.BlockSpec((tm, d), lambda i: (i, k))
    return pl.pallas_call(
        body, name="merge_fwd", grid=(t // tm,),
        in_specs=[tile(0), tile(0), tile(3), tile(0), tile(7), tile(8), _const_spec((1, HEAD_DIM)),
                  square, square, square],
        out_specs=tile(0),
        out_shape=jax.ShapeDtypeStruct((t, d), F32),
        scratch_shapes=[pltpu.VMEM((tm, d), BF16)],
        compiler_params=_params(dimension_semantics=("arbitrary",)),
    )(x1, oa, proj, ob, proj, proj, gn, wa, wb, wo)


def _merge_bwd(oa, proj, ob, dy, gn, wa, wb, wo, layer, n_heads):
    t, d = oa.shape
    tm = _pick(t, (256, 128))
    nt = t // tm
    square = _layer_spec(layer, (d, d), lambda i: (0, 0), single=True)

    def body(oa_ref, z_ref, ob_ref, ga_ref, gb_ref, dy_ref, gn_ref, wa_ref, wb_ref, wo_ref,
             doa_ref, dz_ref, dob_ref, dga_ref, dgb_ref, dgn_ref, dwa_hbm, dwb_hbm, dwo_hbm,
             na_ref, dna_ref, dwa_ref, dwb_ref, dwo_ref, stage_ref):
        i = pl.program_id(0)

        @pl.when(i == 0)
        def _():
            dgn_ref[...] = jnp.zeros((1, HEAD_DIM), F32)
            dwa_ref[...] = jnp.zeros((d, d), F32)
            dwb_ref[...] = jnp.zeros((d, d), F32)
            dwo_ref[...] = jnp.zeros((d, d), F32)

        for h in range(n_heads):
            sl = slice(h * HEAD_DIM, (h + 1) * HEAD_DIM)
            na_ref[:, sl] = _gated_norm(oa_ref[:, sl], z_ref[:, sl], gn_ref[...]).astype(BF16)
        dy = dy_ref[...].astype(BF16)
        ob = ob_ref[...].astype(BF16)
        ya = _mm_nn(na_ref[...], wa_ref[...])
        yb = _mm_nn(ob, wb_ref[...])
        m, vjp = jax.vjp(_merge_gates, ya, yb, ga_ref[...], gb_ref[...])
        dwo_ref[...] += _mm_tn(m, dy)
        dya, dyb, dga, dgb = vjp(_mm_nt(dy, wo_ref[...]))
        dga_ref[...] = dga.astype(BF16)
        dgb_ref[...] = dgb.astype(BF16)
        dwa_ref[...] += _mm_tn(na_ref[...], dya)
        dwb_ref[...] += _mm_tn(ob, dyb)
        dob_ref[...] = _mm_nt(dyb, wb_ref[...])
        dna_ref[...] = _mm_nt(dya, wa_ref[...])
        for h in range(n_heads):
            sl = slice(h * HEAD_DIM, (h + 1) * HEAD_DIM)
            _, vjp_h = jax.vjp(_gated_norm, oa_ref[:, sl], z_ref[:, sl], gn_ref[...])
            doa, dz, dgn = vjp_h(dna_ref[:, sl])
            doa_ref[:, sl] = doa
            dz_ref[:, sl] = dz.astype(BF16)
            dgn_ref[...] += dgn

        @pl.when(i == nt - 1)
        def _():
            for acc, out in ((dwa_ref, dwa_hbm), (dwb_ref, dwb_hbm), (dwo_ref, dwo_hbm)):
                stage_ref[...] = acc[...].astype(BF16)
                pltpu.sync_copy(stage_ref, out)

    tile = lambda k: pl.BlockSpec((tm, d), lambda i: (i, k))
    any_spec = pl.BlockSpec(memory_space=pl.ANY)
    return pl.pallas_call(
        body, name="merge_bwd", grid=(nt,),
        in_specs=[tile(0), tile(3), tile(0), tile(7), tile(8), tile(0), _const_spec((1, HEAD_DIM)),
                  square, square, square],
        out_specs=[tile(0), tile(0), tile(0), tile(0), tile(0), _const_spec((1, HEAD_DIM)),
                   any_spec, any_spec, any_spec],
        out_shape=[jax.ShapeDtypeStruct((t, d), F32), jax.ShapeDtypeStruct((t, d), BF16),
                   jax.ShapeDtypeStruct((t, d), F32), jax.ShapeDtypeStruct((t, d), BF16),
                   jax.ShapeDtypeStruct((t, d), BF16), jax.ShapeDtypeStruct((1, HEAD_DIM), F32),
                   jax.ShapeDtypeStruct((d, d), BF16), jax.ShapeDtypeStruct((d, d), BF16),
                   jax.ShapeDtypeStruct((d, d), BF16)],
        scratch_shapes=[pltpu.VMEM((tm, d), BF16), pltpu.VMEM((tm, d), F32),
                        pltpu.VMEM((d, d), F32), pltpu.VMEM((d, d), F32), pltpu.VMEM((d, d), F32),
                        pltpu.VMEM((d, d), BF16)],
        compiler_params=_params(dimension_semantics=("arbitrary",)),
    )(oa, proj, ob, proj, proj, dy, gn, wa, wb, wo)


def _loss_head(y, target):
    t, d = y.shape
    tm = _pick(t, (256, 128))

    def body(y_ref, t_ref, dy_ref, loss_ref):
        @pl.when(pl.program_id(0) == 0)
        def _():
            loss_ref[...] = jnp.zeros((8, LANES), F32)

        err = y_ref[...] - t_ref[...]
        dy_ref[...] = err * (1.0 / d)
        per_token = jnp.sum(err * err, axis=1, keepdims=True) * (1.0 / d)
        loss_ref[...] += 0.5 * jnp.sum(per_token, axis=0, keepdims=True)

    return pl.pallas_call(
        body, name="loss_head", grid=(t // tm,),
        in_specs=[pl.BlockSpec((tm, d), lambda i: (i, 0)), pl.BlockSpec((tm, d), lambda i: (i, 0))],
        out_specs=[pl.BlockSpec((tm, d), lambda i: (i, 0)), _const_spec((8, LANES))],
        out_shape=[jax.ShapeDtypeStruct((t, d), F32), jax.ShapeDtypeStruct((8, LANES), F32)],
        compiler_params=_params(dimension_semantics=("arbitrary",)),
    )(y, target)


def _adamw(w, g, m, v):
    rows, cols = w.shape
    tr = rows
    for cand in (512, 256, 128, 64, 32, 16, 8):
        if rows % cand == 0 and cand * cols * 4 <= 2 * 1024 * 1024:
            tr = cand
            break

    def body(w_ref, g_ref, m_ref, v_ref, d_ref, mo_ref, vo_ref):
        g = g_ref[...]
        m2 = ADAM_B1 * m_ref[...] + (1.0 - ADAM_B1) * g
        v2 = ADAM_B2 * v_ref[...] + (1.0 - ADAM_B2) * (g * g)
        m_hat = m2 / (1.0 - ADAM_B1 ** ADAM_STEP)
        v_hat = v2 / (1.0 - ADAM_B2 ** ADAM_STEP)
        d_ref[...] = -ADAM_LR * (m_hat / (jnp.sqrt(v_hat) + ADAM_EPS) + ADAM_WD * w_ref[...])
        mo_ref[...] = m2
        vo_ref[...] = v2

    spec = pl.BlockSpec((tr, cols), lambda i: (i, 0))
    shape = jax.ShapeDtypeStruct((rows, cols), F32)
    return pl.pallas_call(
        body, name="adamw", grid=(rows // tr,), in_specs=[spec] * 4, out_specs=[spec] * 3,
        out_shape=[shape] * 3, compiler_params=_params(dimension_semantics=("arbitrary",)),
    )(w, g, m, v)


def _place():
    return lax.axis_index("x"), lax.axis_index("y"), lax.axis_index("c")


def _other_chips(x, y):
    return [(1 - x, y), (x, 1 - y), (1 - x, 1 - y)]


def _tile_rows(rows, cols, itemsize, cap=1536 * 1024):
    best = None
    for cand in range(16, rows + 1, 16):
        if rows % cand == 0 and cand * cols * itemsize <= cap:
            best = cand
    return best if best is not None else rows


def _allgather_layer(shards, layer, collective_id):
    n = len(shards)

    def body(*refs):
        srcs, outs = refs[:n], refs[n:2 * n]
        send_sems, recv_sems = refs[2 * n:]
        x, y, c = _place()
        me, sibling = (x, y, c), (x, y, 1 - c)
        chips = _other_chips(x, y)
        barrier = pltpu.get_barrier_semaphore()
        for peer in [(*chip, c) for chip in chips] + [sibling]:
            pl.semaphore_signal(barrier, inc=1, device_id=peer, device_id_type=MESH)
        pl.semaphore_wait(barrier, N_CHIPS)

        def half(w, which):
            rows = shards[w].shape[1] // 2
            return pl.ds(which * rows, rows)

        def copy(w, k, shard, which, to, from_src=False):
            part = half(w, which)
            return pltpu.make_async_remote_copy(
                src_ref=srcs[w].at[layer, part] if from_src else outs[w].at[shard, part],
                dst_ref=outs[w].at[shard, part], send_sem=send_sems.at[6 * w + k], recv_sem=recv_sems.at[6 * w + k],
                device_id=to, device_id_type=MESH)

        first = [copy(w, j, 2 * x + y, c, (*chip, c), from_src=True) for j, chip in enumerate(chips) for w in range(n)]
        for cp in first:
            cp.start()
        passed = []
        for j, (cx, cy) in enumerate(chips):
            for w in range(n):
                copy(w, j, 2 * cx + cy, c, me).wait_recv()
                cp = copy(w, 3 + j, 2 * cx + cy, c, sibling)
                cp.start()
                passed.append(cp)
        for j, (cx, cy) in enumerate(chips):
            for w in range(n):
                copy(w, 3 + j, 2 * cx + cy, 1 - c, me).wait_recv()
        for cp in first + passed:
            cp.wait_send()

    return pl.kernel(
        body, name=f"allgather_layer{layer}_id{collective_id}",
        out_type=[jax.ShapeDtypeStruct((N_CHIPS,) + s.shape[1:], s.dtype) for s in shards],
        mesh=plsc.ScalarSubcoreMesh(axis_name="sequencer", num_cores=1),
        scratch_types=[pltpu.SemaphoreType.DMA((6 * n,)), pltpu.SemaphoreType.DMA((6 * n,))],
        compiler_params=pltpu.CompilerParams(collective_id=collective_id),
    )(*shards)


def _swap_halves(grads):
    n = len(grads)

    def body(*refs):
        gs, gots = refs[:n], refs[n:2 * n]
        send_sems, recv_sems = refs[2 * n:]
        x, y, c = _place()
        copies = []
        for w in range(n):
            half = grads[w].shape[1] // 2
            copies.append(pltpu.make_async_remote_copy(
                src_ref=gs[w].at[:, pl.ds((1 - c) * half, half)], dst_ref=gots[w], send_sem=send_sems.at[w],
                recv_sem=recv_sems.at[w], device_id=(x, y, 1 - c), device_id_type=MESH))
        for cp in copies:
            cp.start()
        for cp in copies:
            cp.wait()

    hbm = pl.BlockSpec(memory_space=pl.ANY)
    return pl.pallas_call(
        body, name="swap_halves", in_specs=[hbm] * n, out_specs=[hbm] * n,
        out_shape=[jax.ShapeDtypeStruct((g.shape[0], g.shape[1] // 2, g.shape[2]), g.dtype) for g in grads],
        scratch_shapes=[pltpu.SemaphoreType.DMA((n,)), pltpu.SemaphoreType.DMA((n,))],
    )(*grads)


def _add_half(grad, got, c_idx):
    n, rows, all_cols = grad.shape
    side = N_CHIPS // n
    cols = all_cols // side
    half = rows // 2
    tr = _tile_rows(half, cols, 2)
    nb = half // tr

    def body(c_ref, a_ref, b_ref, o_ref):
        o_ref[...] = (a_ref[...].astype(F32) + b_ref[...].astype(F32)).astype(o_ref.dtype)

    return pl.pallas_call(
        body, name="add_half",
        grid_spec=pltpu.PrefetchScalarGridSpec(
            num_scalar_prefetch=1, grid=(N_CHIPS, nb),
            in_specs=[pl.BlockSpec((1, tr, cols), lambda s, r, c_ref: (s // side, c_ref[0] * nb + r, s % side)),
                      pl.BlockSpec((1, tr, cols), lambda s, r, c_ref: (s // side, r, s % side))],
            out_specs=pl.BlockSpec((1, tr, cols), lambda s, r, c_ref: (s, r, 0))),
        out_shape=jax.ShapeDtypeStruct((N_CHIPS, half, cols), grad.dtype),
        compiler_params=_params(dimension_semantics=("arbitrary", "arbitrary")),
    )(c_idx, grad, got)


def _scatter_partials(parts, layer, collective_id):
    n = len(parts)

    def body(*refs):
        ps, gots = refs[:n], refs[n:2 * n]
        send_sems, recv_sems = refs[2 * n:]
        x, y, c = _place()
        chips = _other_chips(x, y)
        barrier = pltpu.get_barrier_semaphore()
        for chip in chips:
            pl.semaphore_signal(barrier, inc=1, device_id=(*chip, c), device_id_type=MESH)
        pl.semaphore_wait(barrier, N_CHIPS - 1)
        copies = [pltpu.make_async_remote_copy(src_ref=ps[w].at[2 * cx + cy], dst_ref=gots[w].at[j],
                                               send_sem=send_sems.at[3 * w + j], recv_sem=recv_sems.at[3 * w + j],
                                               device_id=(cx, cy, c), device_id_type=MESH)
                  for j, (cx, cy) in enumerate(chips) for w in range(n)]
        for cp in copies:
            cp.start()
        for cp in copies:
            cp.wait()

    return pl.kernel(
        body, name=f"scatter_partials{layer}",
        out_type=[jax.ShapeDtypeStruct((N_CHIPS - 1,) + p.shape[1:], p.dtype) for p in parts],
        mesh=plsc.ScalarSubcoreMesh(axis_name="sequencer", num_cores=1),
        scratch_types=[pltpu.SemaphoreType.DMA((3 * n,)), pltpu.SemaphoreType.DMA((3 * n,))],
        compiler_params=pltpu.CompilerParams(collective_id=collective_id),
    )(*parts)


def _sum_partials(part, got, s_idx, c_idx):
    n, half, cols = part.shape
    tr = _tile_rows(half, cols, 2, cap=1024 * 1024)
    nb = half // tr

    def body(s_ref, c_ref, a_ref, b_ref, o_ref):
        acc = a_ref[0].astype(F32)
        for j in range(n - 1):
            acc = acc + b_ref[j].astype(F32)
        o_ref[...] = acc

    return pl.pallas_call(
        body, name="sum_partials",
        grid_spec=pltpu.PrefetchScalarGridSpec(
            num_scalar_prefetch=2, grid=(nb,),
            in_specs=[pl.BlockSpec((1, tr, cols), lambda r, s_ref, c_ref: (s_ref[0], r, 0)),
                      pl.BlockSpec((n - 1, tr, cols), lambda r, s_ref, c_ref: (0, r, 0))],
            out_specs=pl.BlockSpec((tr, cols), lambda r, s_ref, c_ref: (c_ref[0] * nb + r, 0))),
        out_shape=jax.ShapeDtypeStruct((2 * half, cols), F32),
        compiler_params=_params(dimension_semantics=("arbitrary",)),
    )(s_idx, c_idx, part, got)


def _join_halves(bufs):
    n = len(bufs)

    def body(*refs):
        outs = refs[n:2 * n]
        send_sems, recv_sems = refs[2 * n:]
        x, y, c = _place()
        copies = []
        for w in range(n):
            half = bufs[w].shape[0] // 2
            mine = outs[w].at[pl.ds(c * half, half)]
            copies.append(pltpu.make_async_remote_copy(src_ref=mine, dst_ref=mine, send_sem=send_sems.at[w],
                                                       recv_sem=recv_sems.at[w], device_id=(x, y, 1 - c),
                                                       device_id_type=MESH))
        for cp in copies:
            cp.start()
        for cp in copies:
            cp.wait()

    hbm = pl.BlockSpec(memory_space=pl.ANY)
    return pl.pallas_call(
        body, name="join_halves", in_specs=[hbm] * n, out_specs=[hbm] * n,
        out_shape=[jax.ShapeDtypeStruct(b.shape, b.dtype) for b in bufs],
        input_output_aliases={w: w for w in range(n)},
        scratch_shapes=[pltpu.SemaphoreType.DMA((n,)), pltpu.SemaphoreType.DMA((n,))],
    )(*bufs)


def _allreduce_small(v, name):
    rows = v.shape[0]

    def body(v_ref, o_ref, gath, send_sems, recv_sems):
        x, y, c = _place()
        idx = 4 * x + 2 * y + c
        gath[0] = v_ref[...]
        copies = []
        for r in range(1, N_DEV):
            peer = (1 - x if r & 4 else x, 1 - y if r & 2 else y, 1 - c if r & 1 else c)
            cp = pltpu.make_async_remote_copy(src_ref=v_ref, dst_ref=gath.at[r], send_sem=send_sems.at[r - 1],
                                              recv_sem=recv_sems.at[r - 1], device_id=peer, device_id_type=MESH)
            cp.start()
            copies.append(cp)
        for cp in copies:
            cp.wait()
        acc = gath[idx]
        for a in range(1, N_DEV):
            acc = acc + gath[lax.bitwise_xor(idx, a)]
        o_ref[...] = acc

    vmem = pl.BlockSpec(memory_space=pltpu.VMEM)
    return pl.pallas_call(
        body, name=name, in_specs=[vmem], out_specs=vmem,
        out_shape=jax.ShapeDtypeStruct((rows, LANES), F32),
        scratch_shapes=[pltpu.VMEM((N_DEV, rows, LANES), F32), pltpu.SemaphoreType.DMA((N_DEV - 1,)),
                        pltpu.SemaphoreType.DMA((N_DEV - 1,))],
    )(v)


def _join_shards(name, gathered):
    if name in COL_SHARDED:
        return jnp.concatenate([gathered[s] for s in range(N_CHIPS)], axis=1)[None]
    return gathered.reshape(1, N_CHIPS * gathered.shape[1], gathered.shape[2])


def _row_shards(g):
    return g.reshape(N_CHIPS, g.shape[0] // N_CHIPS, g.shape[1])


def _col_shards(g):
    width = g.shape[1] // N_CHIPS
    return jnp.stack([g[:, s * width:(s + 1) * width] for s in range(N_CHIPS)])


def _pad_small(flat):
    n = flat.shape[0]
    block = 8 * LANES
    padded = -(-n // block) * block
    return jnp.pad(flat, (0, padded - n)).reshape(padded // LANES, LANES)


def kernel(x, ffn1_norm, ffn1_w_in, ffn1_w_out, mix_norm, w_in, dn_conv_w, dn_a_log, dn_dt_bias, dn_out_norm, sb_q_norm, sb_k_norm, w_branch_a, w_branch_b, w_out, ffn2_norm, ffn2_w_in, ffn2_w_out, loss_target, m_ffn1_norm, m_ffn1_w_in, m_ffn1_w_out, m_mix_norm, m_w_in, m_dn_conv_w, m_dn_a_log, m_dn_dt_bias, m_dn_out_norm, m_sb_q_norm, m_sb_k_norm, m_w_branch_a, m_w_branch_b, m_w_out, m_ffn2_norm, m_ffn2_w_in, m_ffn2_w_out, v_ffn1_norm, v_ffn1_w_in, v_ffn1_w_out, v_mix_norm, v_w_in, v_dn_conv_w, v_dn_a_log, v_dn_dt_bias, v_dn_out_norm, v_sb_q_norm, v_sb_k_norm, v_w_branch_a, v_w_branch_b, v_w_out, v_ffn2_norm, v_ffn2_w_in, v_ffn2_w_out):
    w = dict(ffn1_norm=ffn1_norm, ffn1_w_in=ffn1_w_in, ffn1_w_out=ffn1_w_out, mix_norm=mix_norm, w_in=w_in,
             dn_conv_w=dn_conv_w, dn_a_log=dn_a_log, dn_dt_bias=dn_dt_bias, dn_out_norm=dn_out_norm,
             sb_q_norm=sb_q_norm, sb_k_norm=sb_k_norm, w_branch_a=w_branch_a, w_branch_b=w_branch_b, w_out=w_out,
             ffn2_norm=ffn2_norm, ffn2_w_in=ffn2_w_in, ffn2_w_out=ffn2_w_out)
    mom = dict(ffn1_norm=m_ffn1_norm, ffn1_w_in=m_ffn1_w_in, ffn1_w_out=m_ffn1_w_out, mix_norm=m_mix_norm, w_in=m_w_in,
               dn_conv_w=m_dn_conv_w, dn_a_log=m_dn_a_log, dn_dt_bias=m_dn_dt_bias, dn_out_norm=m_dn_out_norm,
               sb_q_norm=m_sb_q_norm, sb_k_norm=m_sb_k_norm, w_branch_a=m_w_branch_a, w_branch_b=m_w_branch_b,
               w_out=m_w_out, ffn2_norm=m_ffn2_norm, ffn2_w_in=m_ffn2_w_in, ffn2_w_out=m_ffn2_w_out)
    var = dict(ffn1_norm=v_ffn1_norm, ffn1_w_in=v_ffn1_w_in, ffn1_w_out=v_ffn1_w_out, mix_norm=v_mix_norm, w_in=v_w_in,
               dn_conv_w=v_dn_conv_w, dn_a_log=v_dn_a_log, dn_dt_bias=v_dn_dt_bias, dn_out_norm=v_dn_out_norm,
               sb_q_norm=v_sb_q_norm, sb_k_norm=v_sb_k_norm, w_branch_a=v_w_branch_a, w_branch_b=v_w_branch_b,
               w_out=v_w_out, ffn2_norm=v_ffn2_norm, ffn2_w_in=v_ffn2_w_in, ffn2_w_out=v_ffn2_w_out)

    _, t, d = x.shape
    depth = ffn1_norm.shape[0]
    n_heads = d // HEAD_DIM
    conv_cols = dn_conv_w.shape[2]
    assert d % HEAD_DIM == 0 and t % SB_BLOCK == 0 and 2 * n_heads <= LANES and depth % 2 == 0
    assert w_in.shape[2] * N_CHIPS == 9 * d + 2 * n_heads and conv_cols * N_CHIPS == 3 * d

    x_idx, y_idx, c_idx = _place()
    shard = 2 * x_idx + y_idx
    c_arr = jnp.reshape(c_idx, (1,)).astype(jnp.int32)
    s_arr = jnp.reshape(shard, (1,)).astype(jnp.int32)

    mine = {n: w[n].astype(BF16) for n in BIG}
    cut = 4 * d
    first = ("ffn1_w_in", "ffn1_w_out")
    rest = tuple(n for n in BIG if n not in first)

    def gather(names, l, collective_id):
        return dict(zip(names, _allgather_layer([mine[n] for n in names], l, collective_id)))

    arriving = [{**gather(first, 0, 0), **gather(rest, 0, depth)}] + [gather(BIG, l, l) for l in range(1, depth)]

    def layer_weights(l, names, after):
        gathered, after = lax.optimization_barrier(({n: arriving[l][n] for n in names}, after))
        full = {}
        for n in names:
            g = lax.dynamic_update_slice(gathered[n], mine[n][l][None], (shard, 0, 0))
            full[n] = _join_shards(n, g)
        if "w_in" in names:
            w_all = full["w_in"]
            full["w_main"] = jnp.concatenate([w_all[..., :cut], w_all[..., cut + 2 * n_heads:]], axis=-1)
            full["w_ba"] = jnp.pad(w_all[..., cut:cut + 2 * n_heads], ((0, 0), (0, 0), (0, LANES - 2 * n_heads)))
        return full, after

    conv_place = lax.dynamic_update_slice(jnp.zeros((depth, DN_CONV, 3 * d), F32), dn_conv_w, (0, 0, shard * conv_cols))
    conv_rows = _pad_small(conv_place.reshape(-1))
    conv_full = (0.5 * _allreduce_small(conv_rows, "allgather_conv")).reshape(-1)[:depth * DN_CONV * 3 * d]
    conv_full = jnp.pad(conv_full.reshape(depth, DN_CONV, 3 * d), ((0, 0), (0, CONV_ROWS - DN_CONV), (0, 0)))

    def head_row(vals):
        return jnp.pad(vals, (n_heads, LANES - 2 * n_heads)).reshape(1, LANES)

    saved, layers = [], []
    cur = x[0]
    for l in range(depth):
        full, x0 = layer_weights(l, first, cur)
        x1 = _ffn_fwd(x0, ffn1_norm[l][None], full["ffn1_w_in"], full["ffn1_w_out"], 0)
        later, x1 = layer_weights(l, rest, x1)
        full.update(later)
        layers.append(full)
        proj, ba = _proj_fwd(x1, mix_norm[l][None], full["w_main"], full["w_ba"], 0)
        act = _dn_prep_fwd(proj, conv_full[l], n_heads)
        alog, dtb = head_row(dn_a_log[l]), head_row(dn_dt_bias[l])
        oa, snaps = _delta_fwd(act, ba, alog, dtb, n_heads)
        qb, kb = _sb_prep_fwd(proj, sb_q_norm[l][None], sb_k_norm[l][None], n_heads)
        ob, ltot = _sb_attn_fwd(qb, kb, proj, n_heads)
        x2 = _merge_fwd(x1, oa, proj, ob, dn_out_norm[l][None], full["w_branch_a"], full["w_branch_b"],
                        full["w_out"], 0, n_heads)
        cur = _ffn_fwd(x2, ffn2_norm[l][None], full["ffn2_w_in"], full["ffn2_w_out"], 0)
        saved.append((x0, x1, proj, ba, act, alog, dtb, oa, snaps, qb, kb, ob, ltot, x2))

    dcur, loss_part = _loss_head(cur, loss_target[0])

    grads = {n: [None] * depth for n in WEIGHTS}
    reduced = {n: [None] * depth for n in BIG}
    pending = None

    def finish_reduce(l, parts, arrived, after):
        arrived, after = lax.optimization_barrier((arrived, after))
        halves = [_sum_partials(p, got, s_arr, c_arr) for p, got in zip(parts, arrived)]
        for n, g in zip(BIG, _join_halves(halves)):
            reduced[n][l] = g
        return after

    for l in reversed(range(depth)):
        x0, x1, proj, ba, act, alog, dtb, oa, snaps, qb, kb, ob, ltot, x2 = saved[l]
        full = layers[l]
        dx2, dg, dwi, dwo = _ffn_bwd(x2, ffn2_norm[l][None], dcur, full["ffn2_w_in"], full["ffn2_w_out"], 0)
        grads["ffn2_norm"][l] = dg[0]
        grads["ffn2_w_in"][l] = dwi
        grads["ffn2_w_out"][l] = _row_shards(dwo)
        doa, dz, dob, dga, dgb, dgn, dwa, dwb, dwout = _merge_bwd(
            oa, proj, ob, dx2, dn_out_norm[l][None], full["w_branch_a"], full["w_branch_b"], full["w_out"], 0,
            n_heads)
        grads["dn_out_norm"][l] = dgn[0]
        grads["w_branch_a"][l], grads["w_branch_b"][l] = _row_shards(dwa), _row_shards(dwb)
        grads["w_out"][l] = _row_shards(dwout)
        dqb, dkb, dvb = _sb_attn_bwd(qb, kb, proj, ltot, dob, n_heads)
        dsq, dsk, dqn, dkn = _sb_prep_bwd(proj, sb_q_norm[l][None], sb_k_norm[l][None], dqb, dkb, n_heads)
        grads["sb_q_norm"][l], grads["sb_k_norm"][l] = dqn[0], dkn[0]
        dact, dba, dal, ddt = _delta_bwd(act, ba, alog, dtb, snaps, doa, n_heads)
        grads["dn_a_log"][l] = dal[0, n_heads:2 * n_heads]
        grads["dn_dt_bias"][l] = ddt[0, n_heads:2 * n_heads]
        dqkv, dconv = _dn_prep_bwd(proj, conv_full[l], dact, n_heads)
        grads["dn_conv_w"][l] = dconv[:DN_CONV]
        dproj = jnp.concatenate([dqkv, dz, dsq, dsk, dvb.astype(BF16), dga, dgb], axis=1)
        dx1, dg, dwm, dwba = _proj_bwd(x1, mix_norm[l][None], dx2, dproj, dba, full["w_main"], full["w_ba"], 0)
        grads["mix_norm"][l] = dg[0]
        grads["w_in"][l] = _col_shards(
            jnp.concatenate([dwm[:, :cut], dwba[:, :2 * n_heads].astype(BF16), dwm[:, cut:]], axis=1))
        dcur, dg, dwi, dwo = _ffn_bwd(x0, ffn1_norm[l][None], dx1, full["ffn1_w_in"], full["ffn1_w_out"], 0)
        grads["ffn1_norm"][l] = dg[0]
        grads["ffn1_w_in"][l] = dwi
        grads["ffn1_w_out"][l] = _row_shards(dwo)

        if pending is not None:
            dcur = finish_reduce(*pending, dcur)
        g_major = [grads[n][l] for n in BIG]
        parts = [_add_half(g, got, c_arr) for g, got in zip(g_major, _swap_halves(g_major))]
        pending = (l, parts, _scatter_partials(parts, l, collective_id=depth + 1 + l))
    dcur = finish_reduce(*pending, dcur)
    final = {n: jnp.stack(reduced[n]) for n in BIG}
    grads = {n: jnp.stack(grads[n]) for n in SMALL + ("dn_conv_w",)}

    small_names = SMALL + ("dn_conv_w",)
    small_sizes = [int(np.prod(grads[n].shape)) for n in small_names]
    small_off = np.concatenate([[0], np.cumsum(small_sizes)])
    small = jnp.concatenate([grads[n].reshape(-1) for n in small_names] + [loss_part[0, :1]])
    small_sum = _allreduce_small(_pad_small(small), "allreduce_small").reshape(-1)
    for i, n in enumerate(small_names):
        final[n] = small_sum[small_off[i]:small_off[i + 1]].reshape(grads[n].shape)
    final["dn_conv_w"] = lax.dynamic_slice(final["dn_conv_w"], (0, 0, shard * conv_cols), (depth, DN_CONV, conv_cols))
    loss = small_sum[small_off[-1]]

    deltas, new_m, new_v = {}, {}, {}
    for n in WEIGHTS:
        shape = w[n].shape
        flat = (-1, shape[-1])
        dl, m2, v2 = _adamw(w[n].reshape(flat), final[n].reshape(flat), mom[n].reshape(flat), var[n].reshape(flat))
        deltas[n], new_m[n], new_v[n] = dl.reshape(shape), m2.reshape(shape), v2.reshape(shape)

    grad_x = dcur[None]
    return (loss, grad_x, *[final[n] for n in WEIGHTS], *[deltas[n] for n in WEIGHTS],
            *[new_m[n] for n in WEIGHTS], *[new_v[n] for n in WEIGHTS])
```

```python
import functools

import jax
import jax.numpy as jnp
import numpy as np
from jax import lax
from jax.experimental import pallas as pl
from jax.experimental.pallas import tpu as pltpu
from jax.experimental.pallas import tpu_sc as plsc

F32 = jnp.float32
BF16 = jnp.bfloat16

LANES = 128
HEAD_DIM = 128
DN_CHUNK = 64
DN_CONV = 4
CONV_ROWS = 8
SB_BLOCK = 128
FFN_HALF = 0.5
RMS_EPS = 1e-6
L2_EPS = 1e-6
NEG_BIG = -1e30
ADAM_LR = 0.001
ADAM_B1 = 0.9
ADAM_B2 = 0.999
ADAM_EPS = 1e-08
ADAM_WD = 0.01
ADAM_STEP = 10
VMEM_LIMIT = 56 * 1024 * 1024
N_CHIPS = 4
N_DEV = 8
MESH = pl.DeviceIdType.MESH

BIG = ("ffn1_w_in", "ffn1_w_out", "w_in", "w_branch_a", "w_branch_b", "w_out", "ffn2_w_in", "ffn2_w_out")
COL_SHARDED = ("ffn1_w_in", "w_in", "ffn2_w_in")
SMALL = ("ffn1_norm", "mix_norm", "dn_a_log", "dn_dt_bias", "dn_out_norm", "sb_q_norm", "sb_k_norm", "ffn2_norm")
WEIGHTS = ("ffn1_norm", "ffn1_w_in", "ffn1_w_out", "mix_norm", "w_in", "dn_conv_w", "dn_a_log", "dn_dt_bias",
           "dn_out_norm", "sb_q_norm", "sb_k_norm", "w_branch_a", "w_branch_b", "w_out", "ffn2_norm", "ffn2_w_in",
           "ffn2_w_out")


def _params(**kw):
    return pltpu.CompilerParams(vmem_limit_bytes=VMEM_LIMIT, **kw)


def _pick(n, options):
    for o in options:
        if n % o == 0:
            return o
    return n


def _const_spec(shape, single=False):
    nd = len(shape)
    if single:
        return pl.BlockSpec(shape, lambda *_: (0,) * nd, pipeline_mode=pl.Buffered(1))
    return pl.BlockSpec(shape, lambda *_: (0,) * nd)


_NN = ((1,), (0,))
_NT = ((1,), (1,))
_TN = ((0,), (0,))


def _dot(a, b, dims):
    return lax.dot_general(a.astype(BF16), b.astype(BF16), (dims, ((), ())), preferred_element_type=F32)


def _mm_nn(a, b):
    return _dot(a, b, _NN)


def _mm_nt(a, b):
    return _dot(a, b, _NT)


def _mm_tn(a, b):
    return _dot(a, b, _TN)


def _split(a):
    hi = a.astype(BF16)
    lo = (a - hi.astype(F32)).astype(BF16)
    return hi, lo


def _dot_precise(a, b, dims):
    dn = (dims, ((), ()))
    ah, al = _split(a)
    bh, bl = _split(b)
    out = lax.dot_general(ah, bh, dn, preferred_element_type=F32)
    out = out + lax.dot_general(ah, bl, dn, preferred_element_type=F32)
    return out + lax.dot_general(al, bh, dn, preferred_element_type=F32)


def _make_diff_mm(dot):
    @jax.custom_vjp
    def nn(a, b):
        return dot(a, b, _NN)

    @jax.custom_vjp
    def nt(a, b):
        return dot(a, b, _NT)

    @jax.custom_vjp
    def tn(a, b):
        return dot(a, b, _TN)

    nn.defvjp(lambda a, b: (dot(a, b, _NN), (a, b)), lambda r, g: (nt(g, r[1]), tn(r[0], g)))
    nt.defvjp(lambda a, b: (dot(a, b, _NT), (a, b)), lambda r, g: (nn(g, r[1]), tn(g, r[0])))
    tn.defvjp(lambda a, b: (dot(a, b, _TN), (a, b)), lambda r, g: (nt(r[1], g), nn(r[0], g)))
    return nn, nt, tn


_d_nn, _d_nt, _d_tn = _make_diff_mm(_dot)
_p_nn, _p_nt, _p_tn = _make_diff_mm(_dot_precise)


def _softplus_raw(x):
    return jnp.maximum(x, 0.0) + jnp.log(1.0 + jnp.exp(-jnp.abs(x)))


@jax.custom_vjp
def _softplus(x):
    return _softplus_raw(x)


_softplus.defvjp(lambda x: (_softplus_raw(x), x), lambda x, g: (g * jax.nn.sigmoid(x),))


def _rms(x, gain, eps):
    return x * lax.rsqrt(jnp.mean(x * x, axis=-1, keepdims=True) + eps) * gain


def _silu(x):
    return x * jax.nn.sigmoid(x)


def _shift_rows_raw(x, k, down):
    n = x.shape[0]
    row = lax.broadcasted_iota(jnp.int32, x.shape, 0)
    if down:
        return jnp.where(row >= k, pltpu.roll(x, k, 0), 0.0)
    return jnp.where(row < n - k, pltpu.roll(x, n - k, 0), 0.0)


@functools.partial(jax.custom_vjp, nondiff_argnums=(1,))
def _shift_down(x, k):
    return _shift_rows_raw(x, k, True)


_shift_down.defvjp(lambda x, k: (_shift_rows_raw(x, k, True), None),
                   lambda k, _, g: (_shift_rows_raw(g, k, False),))


def _layer_spec(layer, block, index_map, single=False):
    full_map = lambda *a: (layer,) + tuple(index_map(*a))
    if single:
        return pl.BlockSpec((None,) + block, full_map, pipeline_mode=pl.Buffered(1))
    return pl.BlockSpec((None,) + block, full_map)


def _ffn_fwd(x, gain, w_in, w_out, layer):
    t, d = x.shape
    f = w_out.shape[1]
    fc = _pick(f, (256, 128))
    nj = f // fc
    rt = _pick(t, (512, 256, 128))

    def body(x_ref, g_ref, wg_ref, wu_ref, wo_ref, o_ref, hs_ref):
        @pl.when(pl.program_id(0) == 0)
        def _():
            for r in range(t // rt):
                rows = pl.ds(r * rt, rt)
                xr = x_ref[rows, :]
                hs_ref[rows, :] = _rms(xr, g_ref[...], RMS_EPS).astype(BF16)
                o_ref[rows, :] = xr

        for r in range(t // rt):
            rows = pl.ds(r * rt, rt)
            h = hs_ref[rows, :]
            a = _mm_nn(h, wg_ref[...])
            b = _mm_nn(h, wu_ref[...])
            o_ref[rows, :] += FFN_HALF * _mm_nn(_silu(a) * b, wo_ref[...])

    return pl.pallas_call(
        body, name="ffn_fwd", grid=(nj,),
        in_specs=[_const_spec((t, d), True), _const_spec((1, d)),
                  _layer_spec(layer, (d, fc), lambda j: (0, j)), _layer_spec(layer, (d, fc), lambda j: (0, nj + j)),
                  _layer_spec(layer, (fc, d), lambda j: (j, 0))],
        out_specs=_const_spec((t, d)),
        out_shape=jax.ShapeDtypeStruct((t, d), F32),
        scratch_shapes=[pltpu.VMEM((t, d), BF16)],
        compiler_params=_params(dimension_semantics=("arbitrary",)),
    )(x, gain, w_in, w_in, w_out)


def _ffn_bwd(x, gain, dy, w_in, w_out, layer):
    t, d = x.shape
    f = w_out.shape[1]
    fc = _pick(f, (256, 128))
    nj = f // fc
    rt = _pick(t, (512, 256, 128))
    nr = t // rt

    def body(x_ref, g_ref, dy_ref, wg_ref, wu_ref, wo_ref, dx_ref, dg_ref, dwi_ref, dwo_ref,
             hs_ref, dwg_acc, dwu_acc, dwo_acc):
        j = pl.program_id(0)

        @pl.when(j == 0)
        def _():
            for r in range(nr):
                rows = pl.ds(r * rt, rt)
                hs_ref[rows, :] = _rms(x_ref[rows, :], g_ref[...], RMS_EPS).astype(BF16)
                dx_ref[rows, :] = jnp.zeros((rt, d), F32)

        for r in range(nr):
            rows = pl.ds(r * rt, rt)
            h = hs_ref[rows, :]
            dy2 = (FFN_HALF * dy_ref[rows, :]).astype(BF16)
            a = _mm_nn(h, wg_ref[...])
            b = _mm_nn(h, wu_ref[...])
            sig = jax.nn.sigmoid(a)
            sa = a * sig
            ds = _mm_nt(dy2, wo_ref[...])
            da = ds * b * (sig * (1.0 + a * (1.0 - sig)))
            db = ds * sa
            dx_ref[rows, :] += _mm_nt(da, wg_ref[...]) + _mm_nt(db, wu_ref[...])
            dwo_c = _mm_tn(sa * b, dy2)
            dwg_c = _mm_tn(h, da)
            dwu_c = _mm_tn(h, db)
            if r == 0:
                dwo_acc[...] = dwo_c
                dwg_acc[...] = dwg_c
                dwu_acc[...] = dwu_c
            else:
                dwo_acc[...] += dwo_c
                dwg_acc[...] += dwg_c
                dwu_acc[...] += dwu_c
        dwi_ref[0] = dwg_acc[...].astype(BF16)
        dwi_ref[1] = dwu_acc[...].astype(BF16)
        dwo_ref[...] = dwo_acc[...].astype(BF16)

        @pl.when(j == nj - 1)
        def _():
            for r in range(nr):
                rows = pl.ds(r * rt, rt)
                _, vjp = jax.vjp(lambda xx, gg: _rms(xx, gg, RMS_EPS), x_ref[rows, :], g_ref[...])
                dxn, dgr = vjp(dx_ref[rows, :])
                dx_ref[rows, :] = dy_ref[rows, :] + dxn
                if r == 0:
                    dg_ref[...] = dgr
                else:
                    dg_ref[...] += dgr

    return pl.pallas_call(
        body, name="ffn_bwd", grid=(nj,),
        in_specs=[_const_spec((t, d), True), _const_spec((1, d)), _const_spec((t, d), True),
                  _layer_spec(layer, (d, fc), lambda j: (0, j)), _layer_spec(layer, (d, fc), lambda j: (0, nj + j)),
                  _layer_spec(layer, (fc, d), lambda j: (j, 0))],
        out_specs=[_const_spec((t, d)), _const_spec((1, d)),
                   pl.BlockSpec((2, d, fc), lambda j: (0, 0, j)), pl.BlockSpec((fc, d), lambda j: (j, 0))],
        out_shape=[jax.ShapeDtypeStruct((t, d), F32), jax.ShapeDtypeStruct((1, d), F32),
                   jax.ShapeDtypeStruct((2, d, f), BF16), jax.ShapeDtypeStruct((f, d), BF16)],
        scratch_shapes=[pltpu.VMEM((t, d), BF16), pltpu.VMEM((d, fc), F32), pltpu.VMEM((d, fc), F32),
                        pltpu.VMEM((fc, d), F32)],
        compiler_params=_params(dimension_semantics=("arbitrary",)),
    )(x, gain, dy, w_in, w_in, w_out)


def _proj_fwd(x, gain, w, wba, layer):
    t, d = x.shape
    n = w.shape[2]
    nc = _pick(n, (512, 256, 128))
    rt = _pick(t, (512, 256, 128))

    def body(x_ref, g_ref, w_ref, wba_ref, p_ref, ba_ref, hs_ref):
        @pl.when(pl.program_id(0) == 0)
        def _():
            for r in range(t // rt):
                rows = pl.ds(r * rt, rt)
                h = _rms(x_ref[rows, :], g_ref[...], RMS_EPS).astype(BF16)
                hs_ref[rows, :] = h
                ba_ref[rows, :] = _mm_nn(h, wba_ref[...])

        for r in range(t // rt):
            rows = pl.ds(r * rt, rt)
            p_ref[rows, :] = _mm_nn(hs_ref[rows, :], w_ref[...])

    return pl.pallas_call(
        body, name="proj_fwd", grid=(n // nc,),
        in_specs=[_const_spec((t, d), True), _const_spec((1, d)),
                  _layer_spec(layer, (d, nc), lambda j: (0, j)), _layer_spec(layer, (d, LANES), lambda j: (0, 0))],
        out_specs=[pl.BlockSpec((t, nc), lambda j: (0, j)), _const_spec((t, LANES))],
        out_shape=[jax.ShapeDtypeStruct((t, n), F32), jax.ShapeDtypeStruct((t, LANES), F32)],
        scratch_shapes=[pltpu.VMEM((t, d), BF16)],
        compiler_params=_params(dimension_semantics=("arbitrary",)),
    )(x, gain, w, wba)


def _proj_bwd(x, gain, dres, dp, dba, w, wba, layer):
    t, d = x.shape
    n = w.shape[2]
    nc = _pick(n, (512, 256, 128))
    nj = n // nc
    rt = _pick(t, (512, 256, 128))
    nr = t // rt

    def body(x_ref, g_ref, dres_ref, dp_ref, dba_ref, w_ref, wba_ref, dx_ref, dg_ref, dw_ref, dwba_ref, hs_ref, dw_acc):
        j = pl.program_id(0)

        @pl.when(j == 0)
        def _():
            for r in range(nr):
                rows = pl.ds(r * rt, rt)
                h = _rms(x_ref[rows, :], g_ref[...], RMS_EPS).astype(BF16)
                hs_ref[rows, :] = h
                g = dba_ref[rows, :]
                dx_ref[rows, :] = _mm_nt(g, wba_ref[...])
                if r == 0:
                    dwba_ref[...] = _mm_tn(h, g)
                else:
                    dwba_ref[...] += _mm_tn(h, g)

        for r in range(nr):
            rows = pl.ds(r * rt, rt)
            g = dp_ref[rows, :]
            dx_ref[rows, :] += _mm_nt(g, w_ref[...])
            if r == 0:
                dw_acc[...] = _mm_tn(hs_ref[rows, :], g)
            else:
                dw_acc[...] += _mm_tn(hs_ref[rows, :], g)
        dw_ref[...] = dw_acc[...].astype(BF16)

        @pl.when(j == nj - 1)
        def _():
            for r in range(nr):
                rows = pl.ds(r * rt, rt)
                _, vjp = jax.vjp(lambda xx, gg: _rms(xx, gg, RMS_EPS), x_ref[rows, :], g_ref[...])
                dxn, dgr = vjp(dx_ref[rows, :])
                dx_ref[rows, :] = dres_ref[rows, :] + dxn
                if r == 0:
                    dg_ref[...] = dgr
                else:
                    dg_ref[...] += dgr

    return pl.pallas_call(
        body, name="proj_bwd", grid=(nj,),
        in_specs=[_const_spec((t, d), True), _const_spec((1, d)), _const_spec((t, d), True),
                  pl.BlockSpec((t, nc), lambda j: (0, j)), _const_spec((t, LANES)),
                  _layer_spec(layer, (d, nc), lambda j: (0, j)), _layer_spec(layer, (d, LANES), lambda j: (0, 0))],
        out_specs=[_const_spec((t, d)), _const_spec((1, d)),
                   pl.BlockSpec((d, nc), lambda j: (0, j)), _const_spec((d, LANES))],
        out_shape=[jax.ShapeDtypeStruct((t, d), F32), jax.ShapeDtypeStruct((1, d), F32),
                   jax.ShapeDtypeStruct((d, n), BF16), jax.ShapeDtypeStruct((d, LANES), F32)],
        scratch_shapes=[pltpu.VMEM((t, d), BF16), pltpu.VMEM((d, nc), F32)],
        compiler_params=_params(dimension_semantics=("arbitrary",)),
    )(x, gain, dres, dp, dba, w, wba)


def _conv_act(x, w0, w1, w2, w3, is_qk):
    y = w3 * x + w2 * _shift_down(x, 1) + w1 * _shift_down(x, 2) + w0 * _shift_down(x, 3)
    y = _silu(y)
    inv = lax.rsqrt(jnp.sum(y * y, axis=-1, keepdims=True) + L2_EPS)
    return y * (is_qk * inv + (1.0 - is_qk))


def _taps(w_ref):
    return tuple(w_ref[i:i + 1, :] for i in range(DN_CONV))


def _dn_prep_fwd(proj, conv_w, n_heads):
    t = proj.shape[0]
    nb = 3 * n_heads

    def body(x_ref, w_ref, o_ref):
        is_qk = jnp.where(pl.program_id(0) < 2 * n_heads, 1.0, 0.0).astype(F32)
        o_ref[...] = _conv_act(x_ref[...], *_taps(w_ref), is_qk)

    return pl.pallas_call(
        body, name="dn_prep_fwd", grid=(nb,),
        in_specs=[pl.BlockSpec((t, HEAD_DIM), lambda i: (0, i)), pl.BlockSpec((CONV_ROWS, HEAD_DIM), lambda i: (0, i))],
        out_specs=pl.BlockSpec((t, HEAD_DIM), lambda i: (0, i)),
        out_shape=jax.ShapeDtypeStruct((t, nb * HEAD_DIM), F32),
        compiler_params=_params(dimension_semantics=("arbitrary",)),
    )(proj, conv_w)


def _dn_prep_bwd(proj, conv_w, dact, n_heads):
    t = proj.shape[0]
    nb = 3 * n_heads

    def body(x_ref, w_ref, g_ref, dx_ref, dw_ref):
        is_qk = jnp.where(pl.program_id(0) < 2 * n_heads, 1.0, 0.0).astype(F32)
        _, vjp = jax.vjp(lambda x, a, b, c, e: _conv_act(x, a, b, c, e, is_qk), x_ref[...], *_taps(w_ref))
        dx, d0, d1, d2, d3 = vjp(g_ref[...])
        dx_ref[...] = dx.astype(BF16)
        dw_ref[...] = jnp.concatenate([d0, d1, d2, d3, jnp.zeros((CONV_ROWS - DN_CONV, HEAD_DIM), F32)], axis=0)

    return pl.pallas_call(
        body, name="dn_prep_bwd", grid=(nb,),
        in_specs=[pl.BlockSpec((t, HEAD_DIM), lambda i: (0, i)), pl.BlockSpec((CONV_ROWS, HEAD_DIM), lambda i: (0, i)),
                  pl.BlockSpec((t, HEAD_DIM), lambda i: (0, i))],
        out_specs=[pl.BlockSpec((t, HEAD_DIM), lambda i: (0, i)), pl.BlockSpec((CONV_ROWS, HEAD_DIM), lambda i: (0, i))],
        out_shape=[jax.ShapeDtypeStruct((t, nb * HEAD_DIM), BF16), jax.ShapeDtypeStruct((CONV_ROWS, nb * HEAD_DIM), F32)],
        compiler_params=_params(dimension_semantics=("arbitrary",)),
    )(proj, conv_w, dact)


def _unit_lower_inverses(lmats, c):
    r = lax.broadcasted_iota(jnp.int32, (c, c), 0)
    q = lax.broadcasted_iota(jnp.int32, (c, c), 1)
    eye = jnp.where(r == q, 1.0, 0.0)
    ps = [eye - l for l in lmats]
    ms = [_p_nn(l, l) for l in lmats]
    n = 2
    while True:
        ps = [p + _p_nn(p, m) for p, m in zip(ps, ms)]
        if 2 * n >= c:
            return ps
        ms = [_p_nn(m, m) for m in ms]
        n *= 2


def _delta_heads(qs, ks, vs, bg, alog, dtb, states):
    n_heads = len(qs)
    heads = range(n_heads)
    c = qs[0].shape[0]
    lane = lax.broadcasted_iota(jnp.int32, (c, LANES), 1)
    r = lax.broadcasted_iota(jnp.int32, (c, c), 0)
    s = lax.broadcasted_iota(jnp.int32, (c, c), 1)
    beta_all = jax.nn.sigmoid(bg)
    g_all = -jnp.exp(alog) * _softplus(bg + dtb)
    beta = [jnp.sum(jnp.where(lane == h, beta_all, 0.0), axis=1, keepdims=True) for h in heads]
    g = [jnp.sum(jnp.where(lane == n_heads + h, g_all, 0.0), axis=1, keepdims=True) for h in heads]
    g_row = [jnp.sum(jnp.where(r == s, g[h], 0.0), axis=0, keepdims=True) for h in heads]
    gc = [jnp.sum(jnp.where(s <= r, g_row[h], 0.0), axis=1, keepdims=True) for h in heads]
    gr = [jnp.sum(jnp.where(r <= s, g[h], 0.0), axis=0, keepdims=True) for h in heads]
    g_last = [jnp.sum(g[h], axis=0, keepdims=True) for h in heads]
    decay = [jnp.exp(jnp.where(r >= s, gc[h] - gr[h], NEG_BIG)) for h in heads]
    q_scaled = [qs[h] * (HEAD_DIM ** -0.5) for h in heads]
    k_beta = [ks[h] * beta[h] for h in heads]
    lmat = [jnp.where(r > s, _d_nt(k_beta[h], ks[h]) * decay[h], 0.0) for h in heads]
    attn = [_d_nt(q_scaled[h], ks[h]) * decay[h] for h in heads]
    tinv = _unit_lower_inverses(lmat, c)
    u = [_p_nn(tinv[h], vs[h] * beta[h]) for h in heads]
    w = [_p_nn(tinv[h], k_beta[h] * jnp.exp(gc[h])) for h in heads]
    v_new = [u[h] - _d_nn(w[h], states[h]) for h in heads]
    o_state = [_d_nn(q_scaled[h] * jnp.exp(gc[h]), states[h]) for h in heads]
    o = [o_state[h] + _d_nn(attn[h], v_new[h]) for h in heads]
    kv = [_d_tn(ks[h] * jnp.exp(g_last[h] - gc[h]), v_new[h]) for h in heads]
    new_states = [states[h] * jnp.exp(g_last[h]) + kv[h] for h in heads]
    return tuple(o), tuple(new_states)


def _delta_fwd(act, ba, alog, dtb, n_heads):
    t = act.shape[0]
    d = n_heads * HEAD_DIM
    c = DN_CHUNK
    nc = t // c

    def body(q_ref, k_ref, v_ref, bg_ref, al_ref, dt_ref, o_ref, snap_ref, st_ref):
        @pl.when(pl.program_id(0) == 0)
        def _():
            st_ref[...] = jnp.zeros(st_ref.shape, F32)

        snap_ref[0] = st_ref[...]
        cols = [slice(h * HEAD_DIM, (h + 1) * HEAD_DIM) for h in range(n_heads)]
        os, new_states = _delta_heads([q_ref[:, sl] for sl in cols], [k_ref[:, sl] for sl in cols],
                                      [v_ref[:, sl] for sl in cols], bg_ref[...], al_ref[...], dt_ref[...],
                                      [st_ref[h] for h in range(n_heads)])
        for h, sl in enumerate(cols):
            o_ref[:, sl] = os[h]
            st_ref[h] = new_states[h]

    return pl.pallas_call(
        body, name="delta_fwd", grid=(nc,),
        in_specs=[pl.BlockSpec((c, d), lambda i: (i, 0)), pl.BlockSpec((c, d), lambda i: (i, 1)),
                  pl.BlockSpec((c, d), lambda i: (i, 2)), pl.BlockSpec((c, LANES), lambda i: (i, 0)),
                  _const_spec((1, LANES)), _const_spec((1, LANES))],
        out_specs=[pl.BlockSpec((c, d), lambda i: (i, 0)),
                   pl.BlockSpec((1, n_heads, HEAD_DIM, HEAD_DIM), lambda i: (i, 0, 0, 0))],
        out_shape=[jax.ShapeDtypeStruct((t, d), F32), jax.ShapeDtypeStruct((nc, n_heads, HEAD_DIM, HEAD_DIM), F32)],
        scratch_shapes=[pltpu.VMEM((n_heads, HEAD_DIM, HEAD_DIM), F32)],
        compiler_params=_params(dimension_semantics=("arbitrary",)),
    )(act, act, act, ba, alog, dtb)


def _delta_bwd(act, ba, alog, dtb, snaps, do, n_heads):
    t = act.shape[0]
    d = n_heads * HEAD_DIM
    c = DN_CHUNK
    nc = t // c

    def body(q_ref, k_ref, v_ref, bg_ref, al_ref, dt_ref, snap_ref, do_ref,
             dact_ref, dbg_ref, dal_ref, ddt_ref, ds_ref):
        @pl.when(pl.program_id(0) == 0)
        def _():
            ds_ref[...] = jnp.zeros(ds_ref.shape, F32)
            dal_ref[...] = jnp.zeros((1, LANES), F32)
            ddt_ref[...] = jnp.zeros((1, LANES), F32)

        heads = range(n_heads)
        cols = [slice(h * HEAD_DIM, (h + 1) * HEAD_DIM) for h in heads]
        _, vjp = jax.vjp(_delta_heads, tuple(q_ref[:, sl] for sl in cols), tuple(k_ref[:, sl] for sl in cols),
                         tuple(v_ref[:, sl] for sl in cols), bg_ref[...], al_ref[...], dt_ref[...],
                         tuple(snap_ref[0, h] for h in heads))
        dq, dk, dv, dbg, dal, ddt, dst = vjp((tuple(do_ref[:, sl] for sl in cols), tuple(ds_ref[h] for h in heads)))
        for h, sl in enumerate(cols):
            dact_ref[:, sl] = dq[h]
            dact_ref[:, d + h * HEAD_DIM:d + (h + 1) * HEAD_DIM] = dk[h]
            dact_ref[:, 2 * d + h * HEAD_DIM:2 * d + (h + 1) * HEAD_DIM] = dv[h]
            ds_ref[h] = dst[h]
        dal_ref[...] += dal
        ddt_ref[...] += ddt
        dbg_ref[...] = dbg.astype(BF16)

    rev = lambda i: nc - 1 - i
    return pl.pallas_call(
        body, name="delta_bwd", grid=(nc,),
        in_specs=[pl.BlockSpec((c, d), lambda i: (rev(i), 0)), pl.BlockSpec((c, d), lambda i: (rev(i), 1)),
                  pl.BlockSpec((c, d), lambda i: (rev(i), 2)), pl.BlockSpec((c, LANES), lambda i: (rev(i), 0)),
                  _const_spec((1, LANES)), _const_spec((1, LANES)),
                  pl.BlockSpec((1, n_heads, HEAD_DIM, HEAD_DIM), lambda i: (rev(i), 0, 0, 0)),
                  pl.BlockSpec((c, d), lambda i: (rev(i), 0))],
        out_specs=[pl.BlockSpec((c, 3 * d), lambda i: (rev(i), 0)), pl.BlockSpec((c, LANES), lambda i: (rev(i), 0)),
                   _const_spec((1, LANES)), _const_spec((1, LANES))],
        out_shape=[jax.ShapeDtypeStruct((t, 3 * d), F32), jax.ShapeDtypeStruct((t, LANES), BF16),
                   jax.ShapeDtypeStruct((1, LANES), F32), jax.ShapeDtypeStruct((1, LANES), F32)],
        scratch_shapes=[pltpu.VMEM((n_heads, HEAD_DIM, HEAD_DIM), F32)],
        compiler_params=_params(dimension_semantics=("arbitrary",)),
    )(act, act, act, ba, alog, dtb, snaps, do)


def _head_norm2(a, b, ga, gb):
    return _rms(a, ga, RMS_EPS), _rms(b, gb, RMS_EPS)


def _sb_prep_fwd(proj, qn, kn, n_heads):
    t = proj.shape[0]
    d = n_heads * HEAD_DIM
    tm = _pick(t, (256, 128))

    def body(q_ref, k_ref, qn_ref, kn_ref, qo_ref, ko_ref):
        for h in range(n_heads):
            sl = slice(h * HEAD_DIM, (h + 1) * HEAD_DIM)
            qo_ref[:, sl], ko_ref[:, sl] = _head_norm2(q_ref[:, sl], k_ref[:, sl], qn_ref[...], kn_ref[...])

    return pl.pallas_call(
        body, name="sb_prep_fwd", grid=(t // tm,),
        in_specs=[pl.BlockSpec((tm, d), lambda i: (i, 4)), pl.BlockSpec((tm, d), lambda i: (i, 5)),
                  _const_spec((1, HEAD_DIM)), _const_spec((1, HEAD_DIM))],
        out_specs=[pl.BlockSpec((tm, d), lambda i: (i, 0)), pl.BlockSpec((tm, d), lambda i: (i, 0))],
        out_shape=[jax.ShapeDtypeStruct((t, d), F32), jax.ShapeDtypeStruct((t, d), F32)],
        compiler_params=_params(dimension_semantics=("arbitrary",)),
    )(proj, proj, qn, kn)


def _sb_prep_bwd(proj, qn, kn, dq, dk, n_heads):
    t = proj.shape[0]
    d = n_heads * HEAD_DIM
    tm = _pick(t, (256, 128))

    def body(q_ref, k_ref, qn_ref, kn_ref, dq_ref, dk_ref, dqo_ref, dko_ref, dqn_ref, dkn_ref):
        @pl.when(pl.program_id(0) == 0)
        def _():
            dqn_ref[...] = jnp.zeros((1, HEAD_DIM), F32)
            dkn_ref[...] = jnp.zeros((1, HEAD_DIM), F32)

        for h in range(n_heads):
            sl = slice(h * HEAD_DIM, (h + 1) * HEAD_DIM)
            _, vjp = jax.vjp(_head_norm2, q_ref[:, sl], k_ref[:, sl], qn_ref[...], kn_ref[...])
            da, db, dga, dgb = vjp((dq_ref[:, sl], dk_ref[:, sl]))
            dqo_ref[:, sl] = da.astype(BF16)
            dko_ref[:, sl] = db.astype(BF16)
            dqn_ref[...] += dga
            dkn_ref[...] += dgb

    return pl.pallas_call(
        body, name="sb_prep_bwd", grid=(t // tm,),
        in_specs=[pl.BlockSpec((tm, d), lambda i: (i, 4)), pl.BlockSpec((tm, d), lambda i: (i, 5)),
                  _const_spec((1, HEAD_DIM)), _const_spec((1, HEAD_DIM)),
                  pl.BlockSpec((tm, d), lambda i: (i, 0)), pl.BlockSpec((tm, d), lambda i: (i, 0))],
        out_specs=[pl.BlockSpec((tm, d), lambda i: (i, 0)), pl.BlockSpec((tm, d), lambda i: (i, 0)),
                   _const_spec((1, HEAD_DIM)), _const_spec((1, HEAD_DIM))],
        out_shape=[jax.ShapeDtypeStruct((t, d), BF16), jax.ShapeDtypeStruct((t, d), BF16),
                   jax.ShapeDtypeStruct((1, HEAD_DIM), F32), jax.ShapeDtypeStruct((1, HEAD_DIM), F32)],
        compiler_params=_params(dimension_semantics=("arbitrary",)),
    )(proj, proj, qn, kn, dq, dk)


def _cumsum_mm(x, tri):
    hi, lo = _split(x)
    return (lax.dot_general(hi, tri, (_NN, ((), ())), preferred_element_type=F32)
            + lax.dot_general(lo, tri, (_NN, ((), ())), preferred_element_type=F32))


def _sb_valid(i, j):
    row = lax.broadcasted_iota(jnp.int32, (SB_BLOCK, SB_BLOCK), 0)
    col = lax.broadcasted_iota(jnp.int32, (SB_BLOCK, SB_BLOCK), 1)
    return (col + j * SB_BLOCK) < (row + i * SB_BLOCK)


def _sb_attn_fwd(qb, kb, proj, n_heads):
    t = qb.shape[0]
    d = n_heads * HEAD_DIM
    nq = t // SB_BLOCK
    hb = _pick(n_heads, (4, 2, 1))
    wide = hb * HEAD_DIM
    v_col0 = 6 * n_heads // hb
    scale = HEAD_DIM ** -0.5

    def body(q_ref, k_ref, v_ref, o_ref, lt_ref):
        i = pl.program_id(1)
        row = lax.broadcasted_iota(jnp.int32, (SB_BLOCK, SB_BLOCK), 0)
        col = lax.broadcasted_iota(jnp.int32, (SB_BLOCK, SB_BLOCK), 1)
        after = jnp.where(row > col, 1.0, 0.0).astype(BF16)
        heads = [slice(h * HEAD_DIM, (h + 1) * HEAD_DIM) for h in range(hb)]
        every = range(hb)
        qs = [q_ref[:, sl].astype(BF16) for sl in heads]

        def step(m, carry):
            j0 = i - 2 * m
            js = (j0, jnp.maximum(j0 - 1, 0))
            valid = (_sb_valid(i, js[0]), jnp.logical_and(_sb_valid(i, js[1]), j0 >= 1))
            rows = [pl.ds(pl.multiple_of(j * SB_BLOCK, SB_BLOCK), SB_BLOCK) for j in js]
            units = [(h, b) for b in range(2) for h in every]
            z = {u: _mm_nt(qs[u[0]], k_ref[rows[u[1]], heads[u[0]]]) * scale for u in units}
            sp = {u: _softplus_raw(z[u]) for u in units}
            lm = {u: jnp.where(valid[u[1]], -sp[u], 0.0) for u in units}
            tail = {u: _cumsum_mm(lm[u], after) for u in units}
            later = {(h, 0): carry[h][1] for h in every}
            later.update({(h, 1): carry[h][1] + jnp.sum(lm[h, 0], axis=1, keepdims=True) for h in every})
            w = {u: jnp.where(valid[u[1]], jnp.exp(z[u] - sp[u] + later[u] + tail[u]), 0.0) for u in units}
            pv = {u: _mm_nn(w[u], v_ref[rows[u[1]], heads[u[0]]]) for u in units}
            return tuple((carry[h][0] + pv[h, 0] + pv[h, 1], later[h, 1] + jnp.sum(lm[h, 1], axis=1, keepdims=True))
                         for h in every)

        init = tuple((jnp.zeros((SB_BLOCK, HEAD_DIM), F32), jnp.zeros((SB_BLOCK, 1), F32)) for _ in heads)
        res = lax.fori_loop(0, (i + 2) // 2, step, init)
        for h, sl in enumerate(heads):
            o_ref[:, sl] = res[h][0]
            lt_ref[:, sl] = jnp.broadcast_to(res[h][1], (SB_BLOCK, HEAD_DIM))

    return pl.pallas_call(
        body, name="sb_attn_fwd", grid=(n_heads // hb, nq),
        in_specs=[pl.BlockSpec((SB_BLOCK, wide), lambda g, i: (i, g)),
                  pl.BlockSpec((t, wide), lambda g, i: (0, g)),
                  pl.BlockSpec((t, wide), lambda g, i: (0, v_col0 + g))],
        out_specs=[pl.BlockSpec((SB_BLOCK, wide), lambda g, i: (i, g)),
                   pl.BlockSpec((SB_BLOCK, wide), lambda g, i: (i, g))],
        out_shape=[jax.ShapeDtypeStruct((t, d), F32), jax.ShapeDtypeStruct((t, d), F32)],
        compiler_params=_params(dimension_semantics=("arbitrary", "arbitrary")),
    )(qb, kb, proj)


def _sb_attn_bwd(qb, kb, proj, ltot, do, n_heads):
    t = qb.shape[0]
    d = n_heads * HEAD_DIM
    nq = t // SB_BLOCK
    hb = _pick(n_heads, (4, 2, 1))
    wide = hb * HEAD_DIM
    v_col0 = 6 * n_heads // hb
    scale = HEAD_DIM ** -0.5

    def body(q_ref, k_ref, v_ref, lt_ref, do_ref, dq_ref, dk_ref, dv_ref):
        i = pl.program_id(1)

        @pl.when(i == 0)
        def _():
            dk_ref[...] = jnp.zeros((t, wide), F32)
            dv_ref[...] = jnp.zeros((t, wide), F32)

        row = lax.broadcasted_iota(jnp.int32, (SB_BLOCK, SB_BLOCK), 0)
        col = lax.broadcasted_iota(jnp.int32, (SB_BLOCK, SB_BLOCK), 1)
        upto = jnp.where(row <= col, 1.0, 0.0).astype(BF16)
        before = jnp.where(row < col, 1.0, 0.0).astype(BF16)
        heads = [slice(h * HEAD_DIM, (h + 1) * HEAD_DIM) for h in range(hb)]
        every = range(hb)
        qs = [q_ref[:, sl].astype(BF16) for sl in heads]
        dos = [do_ref[:, sl].astype(BF16) for sl in heads]
        totals = [jnp.max(lt_ref[:, sl], axis=1, keepdims=True) for sl in heads]

        def step(m, carry):
            js = (2 * m, jnp.minimum(2 * m + 1, nq - 1))
            valid = (_sb_valid(i, js[0]), jnp.logical_and(_sb_valid(i, js[1]), 2 * m + 1 <= i))
            rows = [pl.ds(pl.multiple_of(j * SB_BLOCK, SB_BLOCK), SB_BLOCK) for j in js]
            units = [(h, b) for b in range(2) for h in every]
            kj = {u: k_ref[rows[u[1]], heads[u[0]]].astype(BF16) for u in units}
            vj = {u: v_ref[rows[u[1]], heads[u[0]]].astype(BF16) for u in units}
            z = {u: _mm_nt(qs[u[0]], kj[u]) * scale for u in units}
            dw = {u: _mm_nt(dos[u[0]], vj[u]) for u in units}
            sp = {u: _softplus_raw(z[u]) for u in units}
            lm = {u: jnp.where(valid[u[1]], -sp[u], 0.0) for u in units}
            head = {u: _cumsum_mm(lm[u], upto) for u in units}
            lm_before = {(h, 0): carry[h][1] for h in every}
            lm_before.update({(h, 1): carry[h][1] + jnp.sum(lm[h, 0], axis=1, keepdims=True) for h in every})
            w = {u: jnp.where(valid[u[1]], jnp.exp(z[u] - sp[u] + totals[u[0]] - (lm_before[u] + head[u])), 0.0)
                 for u in units}
            e = {u: w[u] * dw[u] for u in units}
            e_local = {u: _mm_nn(e[u], before) for u in units}
            e_before = {(h, 0): carry[h][2] for h in every}
            e_before.update({(h, 1): carry[h][2] + jnp.sum(e[h, 0], axis=1, keepdims=True) for h in every})
            sig = {u: jnp.exp(z[u] - sp[u]) for u in units}
            dz = {u: jnp.where(valid[u[1]], e[u] * (1.0 - sig[u]) - (e_before[u] + e_local[u]) * sig[u], 0.0) * scale
                  for u in units}
            for h, b in units:
                dv_ref[rows[b], heads[h]] += _mm_tn(w[h, b], dos[h])
            for h, b in units:
                dk_ref[rows[b], heads[h]] += _mm_tn(dz[h, b], qs[h])
            dq = {u: _mm_nn(dz[u], kj[u]) for u in units}
            return tuple((carry[h][0] + dq[h, 0] + dq[h, 1],
                          lm_before[h, 1] + jnp.sum(lm[h, 1], axis=1, keepdims=True),
                          e_before[h, 1] + jnp.sum(e[h, 1], axis=1, keepdims=True)) for h in every)

        zero_col = jnp.zeros((SB_BLOCK, 1), F32)
        init = tuple((jnp.zeros((SB_BLOCK, HEAD_DIM), F32), zero_col, zero_col) for _ in heads)
        res = lax.fori_loop(0, (i + 2) // 2, step, init)
        for h, sl in enumerate(heads):
            dq_ref[:, sl] = res[h][0]

    return pl.pallas_call(
        body, name="sb_attn_bwd", grid=(n_heads // hb, nq),
        in_specs=[pl.BlockSpec((SB_BLOCK, wide), lambda g, i: (i, g)),
                  pl.BlockSpec((t, wide), lambda g, i: (0, g)),
                  pl.BlockSpec((t, wide), lambda g, i: (0, v_col0 + g)),
                  pl.BlockSpec((SB_BLOCK, wide), lambda g, i: (i, g)),
                  pl.BlockSpec((SB_BLOCK, wide), lambda g, i: (i, g))],
        out_specs=[pl.BlockSpec((SB_BLOCK, wide), lambda g, i: (i, g)),
                   pl.BlockSpec((t, wide), lambda g, i: (0, g)),
                   pl.BlockSpec((t, wide), lambda g, i: (0, g))],
        out_shape=[jax.ShapeDtypeStruct((t, d), F32), jax.ShapeDtypeStruct((t, d), F32),
                   jax.ShapeDtypeStruct((t, d), F32)],
        compiler_params=_params(dimension_semantics=("arbitrary", "arbitrary")),
    )(qb, kb, proj, ltot, do)


def _gated_norm(oa, z, gn):
    return _rms(oa, gn, RMS_EPS) * _silu(z)


def _merge_gates(ya, yb, ga, gb):
    return jax.nn.sigmoid(ga) * ya + jax.nn.sigmoid(gb) * yb


def _merge_fwd(x1, oa, proj, ob, gn, wa, wb, wo, layer, n_heads):
    t, d = x1.shape
    tm = _pick(t, (256, 128))
    square = _layer_spec(layer, (d, d), lambda i: (0, 0), single=True)

    def body(x_ref, oa_ref, z_ref, ob_ref, ga_ref, gb_ref, gn_ref, wa_ref, wb_ref, wo_ref, o_ref, na_ref):
        for h in range(n_heads):
            sl = slice(h * HEAD_DIM, (h + 1) * HEAD_DIM)
            na_ref[:, sl] = _gated_norm(oa_ref[:, sl], z_ref[:, sl], gn_ref[...]).astype(BF16)
        m = _merge_gates(_mm_nn(na_ref[...], wa_ref[...]), _mm_nn(ob_ref[...], wb_ref[...]), ga_ref[...], gb_ref[...])
        o_ref[...] = x_ref[...] + _mm_nn(m, wo_ref[...])

    tile = lambda k: pl.BlockSpec((tm, d), lambda i: (i, k))
    return pl.pallas_call(
        body, name="merge_fwd", grid=(t // tm,),
        in_specs=[tile(0), tile(0), tile(3), tile(0), tile(7), tile(8), _const_spec((1, HEAD_DIM)),
                  square, square, square],
        out_specs=tile(0),
        out_shape=jax.ShapeDtypeStruct((t, d), F32),
        scratch_shapes=[pltpu.VMEM((tm, d), BF16)],
        compiler_params=_params(dimension_semantics=("arbitrary",)),
    )(x1, oa, proj, ob, proj, proj, gn, wa, wb, wo)


def _merge_bwd(oa, proj, ob, dy, gn, wa, wb, wo, layer, n_heads):
    t, d = oa.shape
    tm = _pick(t, (256, 128))
    nt = t // tm
    square = _layer_spec(layer, (d, d), lambda i: (0, 0), single=True)

    def body(oa_ref, z_ref, ob_ref, ga_ref, gb_ref, dy_ref, gn_ref, wa_ref, wb_ref, wo_ref,
             doa_ref, dz_ref, dob_ref, dga_ref, dgb_ref, dgn_ref, dwa_hbm, dwb_hbm, dwo_hbm,
             na_ref, dna_ref, dwa_ref, dwb_ref, dwo_ref, stage_ref):
        i = pl.program_id(0)

        @pl.when(i == 0)
        def _():
            dgn_ref[...] = jnp.zeros((1, HEAD_DIM), F32)
            dwa_ref[...] = jnp.zeros((d, d), F32)
            dwb_ref[...] = jnp.zeros((d, d), F32)
            dwo_ref[...] = jnp.zeros((d, d), F32)

        for h in range(n_heads):
            sl = slice(h * HEAD_DIM, (h + 1) * HEAD_DIM)
            na_ref[:, sl] = _gated_norm(oa_ref[:, sl], z_ref[:, sl], gn_ref[...]).astype(BF16)
        dy = dy_ref[...].astype(BF16)
        ob = ob_ref[...].astype(BF16)
        ya = _mm_nn(na_ref[...], wa_ref[...])
        yb = _mm_nn(ob, wb_ref[...])
        m, vjp = jax.vjp(_merge_gates, ya, yb, ga_ref[...], gb_ref[...])
        dwo_ref[...] += _mm_tn(m, dy)
        dya, dyb, dga, dgb = vjp(_mm_nt(dy, wo_ref[...]))
        dga_ref[...] = dga.astype(BF16)
        dgb_ref[...] = dgb.astype(BF16)
        dwa_ref[...] += _mm_tn(na_ref[...], dya)
        dwb_ref[...] += _mm_tn(ob, dyb)
        dob_ref[...] = _mm_nt(dyb, wb_ref[...])
        dna_ref[...] = _mm_nt(dya, wa_ref[...])
        for h in range(n_heads):
            sl = slice(h * HEAD_DIM, (h + 1) * HEAD_DIM)
            _, vjp_h = jax.vjp(_gated_norm, oa_ref[:, sl], z_ref[:, sl], gn_ref[...])
            doa, dz, dgn = vjp_h(dna_ref[:, sl])
            doa_ref[:, sl] = doa
            dz_ref[:, sl] = dz.astype(BF16)
            dgn_ref[...] += dgn

        @pl.when(i == nt - 1)
        def _():
            for acc, out in ((dwa_ref, dwa_hbm), (dwb_ref, dwb_hbm), (dwo_ref, dwo_hbm)):
                stage_ref[...] = acc[...].astype(BF16)
                pltpu.sync_copy(stage_ref, out)

    tile = lambda k: pl.BlockSpec((tm, d), lambda i: (i, k))
    any_spec = pl.BlockSpec(memory_space=pl.ANY)
    return pl.pallas_call(
        body, name="merge_bwd", grid=(nt,),
        in_specs=[tile(0), tile(3), tile(0), tile(7), tile(8), tile(0), _const_spec((1, HEAD_DIM)),
                  square, square, square],
        out_specs=[tile(0), tile(0), tile(0), tile(0), tile(0), _const_spec((1, HEAD_DIM)),
                   any_spec, any_spec, any_spec],
        out_shape=[jax.ShapeDtypeStruct((t, d), F32), jax.ShapeDtypeStruct((t, d), BF16),
                   jax.ShapeDtypeStruct((t, d), F32), jax.ShapeDtypeStruct((t, d), BF16),
                   jax.ShapeDtypeStruct((t, d), BF16), jax.ShapeDtypeStruct((1, HEAD_DIM), F32),
                   jax.ShapeDtypeStruct((d, d), BF16), jax.ShapeDtypeStruct((d, d), BF16),
                   jax.ShapeDtypeStruct((d, d), BF16)],
        scratch_shapes=[pltpu.VMEM((tm, d), BF16), pltpu.VMEM((tm, d), F32),
                        pltpu.VMEM((d, d), F32), pltpu.VMEM((d, d), F32), pltpu.VMEM((d, d), F32),
                        pltpu.VMEM((d, d), BF16)],
        compiler_params=_params(dimension_semantics=("arbitrary",)),
    )(oa, proj, ob, proj, proj, dy, gn, wa, wb, wo)


def _loss_head(y, target):
    t, d = y.shape
    tm = _pick(t, (256, 128))

    def body(y_ref, t_ref, dy_ref, loss_ref):
        @pl.when(pl.program_id(0) == 0)
        def _():
            loss_ref[...] = jnp.zeros((8, LANES), F32)

        err = y_ref[...] - t_ref[...]
        dy_ref[...] = err * (1.0 / d)
        per_token = jnp.sum(err * err, axis=1, keepdims=True) * (1.0 / d)
        loss_ref[...] += 0.5 * jnp.sum(per_token, axis=0, keepdims=True)

    return pl.pallas_call(
        body, name="loss_head", grid=(t // tm,),
        in_specs=[pl.BlockSpec((tm, d), lambda i: (i, 0)), pl.BlockSpec((tm, d), lambda i: (i, 0))],
        out_specs=[pl.BlockSpec((tm, d), lambda i: (i, 0)), _const_spec((8, LANES))],
        out_shape=[jax.ShapeDtypeStruct((t, d), F32), jax.ShapeDtypeStruct((8, LANES), F32)],
        compiler_params=_params(dimension_semantics=("arbitrary",)),
    )(y, target)


def _adamw(w, g, m, v):
    rows, cols = w.shape
    tr = rows
    for cand in (512, 256, 128, 64, 32, 16, 8):
        if rows % cand == 0 and cand * cols * 4 <= 2 * 1024 * 1024:
            tr = cand
            break

    def body(w_ref, g_ref, m_ref, v_ref, d_ref, mo_ref, vo_ref):
        g = g_ref[...]
        m2 = ADAM_B1 * m_ref[...] + (1.0 - ADAM_B1) * g
        v2 = ADAM_B2 * v_ref[...] + (1.0 - ADAM_B2) * (g * g)
        m_hat = m2 / (1.0 - ADAM_B1 ** ADAM_STEP)
        v_hat = v2 / (1.0 - ADAM_B2 ** ADAM_STEP)
        d_ref[...] = -ADAM_LR * (m_hat / (jnp.sqrt(v_hat) + ADAM_EPS) + ADAM_WD * w_ref[...])
        mo_ref[...] = m2
        vo_ref[...] = v2

    spec = pl.BlockSpec((tr, cols), lambda i: (i, 0))
    shape = jax.ShapeDtypeStruct((rows, cols), F32)
    return pl.pallas_call(
        body, name="adamw", grid=(rows // tr,), in_specs=[spec] * 4, out_specs=[spec] * 3,
        out_shape=[shape] * 3, compiler_params=_params(dimension_semantics=("arbitrary",)),
    )(w, g, m, v)


def _place():
    return lax.axis_index("x"), lax.axis_index("y"), lax.axis_index("c")


def _other_chips(x, y):
    return [(1 - x, y), (x, 1 - y), (1 - x, 1 - y)]


def _tile_rows(rows, cols, itemsize, cap=1536 * 1024):
    best = None
    for cand in range(16, rows + 1, 16):
        if rows % cand == 0 and cand * cols * itemsize <= cap:
            best = cand
    return best if best is not None else rows


def _allgather_layer(shards, layer, collective_id):
    n = len(shards)

    def body(*refs):
        srcs, outs = refs[:n], refs[n:2 * n]
        send_sems, recv_sems = refs[2 * n:]
        x, y, c = _place()
        me, sibling = (x, y, c), (x, y, 1 - c)
        chips = _other_chips(x, y)
        barrier = pltpu.get_barrier_semaphore()
        for peer in [(*chip, c) for chip in chips] + [sibling]:
            pl.semaphore_signal(barrier, inc=1, device_id=peer, device_id_type=MESH)
        pl.semaphore_wait(barrier, N_CHIPS)

        def half(w, which):
            rows = shards[w].shape[1] // 2
            return pl.ds(which * rows, rows)

        def copy(w, k, shard, which, to, from_src=False):
            part = half(w, which)
            return pltpu.make_async_remote_copy(
                src_ref=srcs[w].at[layer, part] if from_src else outs[w].at[shard, part],
                dst_ref=outs[w].at[shard, part], send_sem=send_sems.at[6 * w + k], recv_sem=recv_sems.at[6 * w + k],
                device_id=to, device_id_type=MESH)

        first = [copy(w, j, 2 * x + y, c, (*chip, c), from_src=True) for j, chip in enumerate(chips) for w in range(n)]
        for cp in first:
            cp.start()
        passed = []
        for j, (cx, cy) in enumerate(chips):
            for w in range(n):
                copy(w, j, 2 * cx + cy, c, me).wait_recv()
                cp = copy(w, 3 + j, 2 * cx + cy, c, sibling)
                cp.start()
                passed.append(cp)
        for j, (cx, cy) in enumerate(chips):
            for w in range(n):
                copy(w, 3 + j, 2 * cx + cy, 1 - c, me).wait_recv()
        for cp in first + passed:
            cp.wait_send()

    return pl.kernel(
        body, name=f"allgather_layer{layer}_id{collective_id}",
        out_type=[jax.ShapeDtypeStruct((N_CHIPS,) + s.shape[1:], s.dtype) for s in shards],
        mesh=plsc.ScalarSubcoreMesh(axis_name="sequencer", num_cores=1),
        scratch_types=[pltpu.SemaphoreType.DMA((6 * n,)), pltpu.SemaphoreType.DMA((6 * n,))],
        compiler_params=pltpu.CompilerParams(collective_id=collective_id),
    )(*shards)


def _swap_halves(grads):
    n = len(grads)

    def body(*refs):
        gs, gots = refs[:n], refs[n:2 * n]
        send_sems, recv_sems = refs[2 * n:]
        x, y, c = _place()
        copies = []
        for w in range(n):
            half = grads[w].shape[1] // 2
            copies.append(pltpu.make_async_remote_copy(
                src_ref=gs[w].at[:, pl.ds((1 - c) * half, half)], dst_ref=gots[w], send_sem=send_sems.at[w],
                recv_sem=recv_sems.at[w], device_id=(x, y, 1 - c), device_id_type=MESH))
        for cp in copies:
            cp.start()
        for cp in copies:
            cp.wait()

    hbm = pl.BlockSpec(memory_space=pl.ANY)
    return pl.pallas_call(
        body, name="swap_halves", in_specs=[hbm] * n, out_specs=[hbm] * n,
        out_shape=[jax.ShapeDtypeStruct((g.shape[0], g.shape[1] // 2, g.shape[2]), g.dtype) for g in grads],
        scratch_shapes=[pltpu.SemaphoreType.DMA((n,)), pltpu.SemaphoreType.DMA((n,))],
    )(*grads)


def _add_half(grad, got, c_idx):
    n, rows, all_cols = grad.shape
    side = N_CHIPS // n
    cols = all_cols // side
    half = rows // 2
    tr = _tile_rows(half, cols, 2)
    nb = half // tr

    def body(c_ref, a_ref, b_ref, o_ref):
        o_ref[...] = (a_ref[...].astype(F32) + b_ref[...].astype(F32)).astype(o_ref.dtype)

    return pl.pallas_call(
        body, name="add_half",
        grid_spec=pltpu.PrefetchScalarGridSpec(
            num_scalar_prefetch=1, grid=(N_CHIPS, nb),
            in_specs=[pl.BlockSpec((1, tr, cols), lambda s, r, c_ref: (s // side, c_ref[0] * nb + r, s % side)),
                      pl.BlockSpec((1, tr, cols), lambda s, r, c_ref: (s // side, r, s % side))],
            out_specs=pl.BlockSpec((1, tr, cols), lambda s, r, c_ref: (s, r, 0))),
        out_shape=jax.ShapeDtypeStruct((N_CHIPS, half, cols), grad.dtype),
        compiler_params=_params(dimension_semantics=("arbitrary", "arbitrary")),
    )(c_idx, grad, got)


def _scatter_partials(parts, layer, collective_id):
    n = len(parts)

    def body(*refs):
        ps, gots = refs[:n], refs[n:2 * n]
        send_sems, recv_sems = refs[2 * n:]
        x, y, c = _place()
        chips = _other_chips(x, y)
        barrier = pltpu.get_barrier_semaphore()
        for chip in chips:
            pl.semaphore_signal(barrier, inc=1, device_id=(*chip, c), device_id_type=MESH)
        pl.semaphore_wait(barrier, N_CHIPS - 1)
        copies = [pltpu.make_async_remote_copy(src_ref=ps[w].at[2 * cx + cy], dst_ref=gots[w].at[j],
                                               send_sem=send_sems.at[3 * w + j], recv_sem=recv_sems.at[3 * w + j],
                                               device_id=(cx, cy, c), device_id_type=MESH)
                  for j, (cx, cy) in enumerate(chips) for w in range(n)]
        for cp in copies:
            cp.start()
        for cp in copies:
            cp.wait()

    return pl.kernel(
        body, name=f"scatter_partials{layer}",
        out_type=[jax.ShapeDtypeStruct((N_CHIPS - 1,) + p.shape[1:], p.dtype) for p in parts],
        mesh=plsc.ScalarSubcoreMesh(axis_name="sequencer", num_cores=1),
        scratch_types=[pltpu.SemaphoreType.DMA((3 * n,)), pltpu.SemaphoreType.DMA((3 * n,))],
        compiler_params=pltpu.CompilerParams(collective_id=collective_id),
    )(*parts)


def _sum_partials(part, got, s_idx, c_idx, layer, depth, stacked=None):
    n, half, cols = part.shape
    tr = _tile_rows(half, cols, 2, cap=1024 * 1024)
    nb = half // tr

    def body(s_ref, c_ref, a_ref, b_ref, *rest):
        o_ref = rest[-1]
        acc = a_ref[0].astype(F32)
        for j in range(n - 1):
            acc = acc + b_ref[j].astype(F32)
        o_ref[...] = acc

    in_specs = [pl.BlockSpec((1, tr, cols), lambda r, s_ref, c_ref: (s_ref[0], r, 0)),
                pl.BlockSpec((n - 1, tr, cols), lambda r, s_ref, c_ref: (0, r, 0))]
    operands = [s_idx, c_idx, part, got]
    aliases = {}
    if stacked is not None:
        in_specs.append(pl.BlockSpec(memory_space=pl.ANY))
        operands.append(stacked)
        aliases = {len(operands) - 1: 0}
    return pl.pallas_call(
        body, name="sum_partials",
        grid_spec=pltpu.PrefetchScalarGridSpec(
            num_scalar_prefetch=2, grid=(nb,), in_specs=in_specs,
            out_specs=pl.BlockSpec((None, tr, cols), lambda r, s_ref, c_ref: (layer, c_ref[0] * nb + r, 0))),
        out_shape=jax.ShapeDtypeStruct((depth, 2 * half, cols), F32),
        input_output_aliases=aliases,
        compiler_params=_params(dimension_semantics=("arbitrary",)),
    )(*operands)


def _join_halves(bufs, layer):
    n = len(bufs)

    def body(*refs):
        outs = refs[n:2 * n]
        send_sems, recv_sems = refs[2 * n:]
        x, y, c = _place()
        copies = []
        for w in range(n):
            half = bufs[w].shape[1] // 2
            mine = outs[w].at[layer, pl.ds(c * half, half)]
            copies.append(pltpu.make_async_remote_copy(src_ref=mine, dst_ref=mine, send_sem=send_sems.at[w],
                                                       recv_sem=recv_sems.at[w], device_id=(x, y, 1 - c),
                                                       device_id_type=MESH))
        for cp in copies:
            cp.start()
        for cp in copies:
            cp.wait()

    hbm = pl.BlockSpec(memory_space=pl.ANY)
    return pl.pallas_call(
        body, name="join_halves", in_specs=[hbm] * n, out_specs=[hbm] * n,
        out_shape=[jax.ShapeDtypeStruct(b.shape, b.dtype) for b in bufs],
        input_output_aliases={w: w for w in range(n)},
        scratch_shapes=[pltpu.SemaphoreType.DMA((n,)), pltpu.SemaphoreType.DMA((n,))],
    )(*bufs)


def _allreduce_small(v, name):
    rows = v.shape[0]

    def body(v_ref, o_ref, gath, send_sems, recv_sems):
        x, y, c = _place()
        idx = 4 * x + 2 * y + c
        gath[0] = v_ref[...]
        copies = []
        for r in range(1, N_DEV):
            peer = (1 - x if r & 4 else x, 1 - y if r & 2 else y, 1 - c if r & 1 else c)
            cp = pltpu.make_async_remote_copy(src_ref=v_ref, dst_ref=gath.at[r], send_sem=send_sems.at[r - 1],
                                              recv_sem=recv_sems.at[r - 1], device_id=peer, device_id_type=MESH)
            cp.start()
            copies.append(cp)
        for cp in copies:
            cp.wait()
        acc = gath[idx]
        for a in range(1, N_DEV):
            acc = acc + gath[lax.bitwise_xor(idx, a)]
        o_ref[...] = acc

    vmem = pl.BlockSpec(memory_space=pltpu.VMEM)
    return pl.pallas_call(
        body, name=name, in_specs=[vmem], out_specs=vmem,
        out_shape=jax.ShapeDtypeStruct((rows, LANES), F32),
        scratch_shapes=[pltpu.VMEM((N_DEV, rows, LANES), F32), pltpu.SemaphoreType.DMA((N_DEV - 1,)),
                        pltpu.SemaphoreType.DMA((N_DEV - 1,))],
    )(v)


def _join_shards(name, gathered):
    if name in COL_SHARDED:
        return jnp.concatenate([gathered[s] for s in range(N_CHIPS)], axis=1)[None]
    return gathered.reshape(1, N_CHIPS * gathered.shape[1], gathered.shape[2])


def _row_shards(g):
    return g.reshape(N_CHIPS, g.shape[0] // N_CHIPS, g.shape[1])


def _mixer_runs(width, cut, n_small):
    runs = []
    for s in range(N_CHIPS):
        lo, hi = s * width, (s + 1) * width
        spans = ((True, lo, min(hi, cut)), (False, max(lo, cut), min(hi, cut + n_small)),
                 (True, max(lo, cut + n_small), hi))
        runs.append([(is_main, a - lo, b - lo) for is_main, a, b in spans if a < b])
    return runs


def _split_mixer_weight(gathered, runs):
    main = [gathered[s][:, a:b] for s, parts in enumerate(runs) for is_main, a, b in parts if is_main]
    small = [gathered[s][:, a:b] for s, parts in enumerate(runs) for is_main, a, b in parts if not is_main]
    small = small[0] if len(small) == 1 else jnp.concatenate(small, axis=1)
    return jnp.concatenate(main, axis=1)[None], jnp.pad(small, ((0, 0), (0, LANES - small.shape[1])))[None]


def _join_mixer_grad(d_main, d_small, runs):
    shards, m, k = [], 0, 0
    for parts in runs:
        cols = []
        for is_main, a, b in parts:
            if is_main:
                cols.append(d_main[:, m:m + b - a])
                m += b - a
            else:
                cols.append(d_small[:, k:k + b - a].astype(d_main.dtype))
                k += b - a
        shards.append(cols[0] if len(cols) == 1 else jnp.concatenate(cols, axis=1))
    return jnp.stack(shards)


def _pad_small(flat):
    n = flat.shape[0]
    block = 8 * LANES
    padded = -(-n // block) * block
    return jnp.pad(flat, (0, padded - n)).reshape(padded // LANES, LANES)


def kernel(x, ffn1_norm, ffn1_w_in, ffn1_w_out, mix_norm, w_in, dn_conv_w, dn_a_log, dn_dt_bias, dn_out_norm, sb_q_norm, sb_k_norm, w_branch_a, w_branch_b, w_out, ffn2_norm, ffn2_w_in, ffn2_w_out, loss_target, m_ffn1_norm, m_ffn1_w_in, m_ffn1_w_out, m_mix_norm, m_w_in, m_dn_conv_w, m_dn_a_log, m_dn_dt_bias, m_dn_out_norm, m_sb_q_norm, m_sb_k_norm, m_w_branch_a, m_w_branch_b, m_w_out, m_ffn2_norm, m_ffn2_w_in, m_ffn2_w_out, v_ffn1_norm, v_ffn1_w_in, v_ffn1_w_out, v_mix_norm, v_w_in, v_dn_conv_w, v_dn_a_log, v_dn_dt_bias, v_dn_out_norm, v_sb_q_norm, v_sb_k_norm, v_w_branch_a, v_w_branch_b, v_w_out, v_ffn2_norm, v_ffn2_w_in, v_ffn2_w_out):
    w = dict(ffn1_norm=ffn1_norm, ffn1_w_in=ffn1_w_in, ffn1_w_out=ffn1_w_out, mix_norm=mix_norm, w_in=w_in,
             dn_conv_w=dn_conv_w, dn_a_log=dn_a_log, dn_dt_bias=dn_dt_bias, dn_out_norm=dn_out_norm,
             sb_q_norm=sb_q_norm, sb_k_norm=sb_k_norm, w_branch_a=w_branch_a, w_branch_b=w_branch_b, w_out=w_out,
             ffn2_norm=ffn2_norm, ffn2_w_in=ffn2_w_in, ffn2_w_out=ffn2_w_out)
    mom = dict(ffn1_norm=m_ffn1_norm, ffn1_w_in=m_ffn1_w_in, ffn1_w_out=m_ffn1_w_out, mix_norm=m_mix_norm, w_in=m_w_in,
               dn_conv_w=m_dn_conv_w, dn_a_log=m_dn_a_log, dn_dt_bias=m_dn_dt_bias, dn_out_norm=m_dn_out_norm,
               sb_q_norm=m_sb_q_norm, sb_k_norm=m_sb_k_norm, w_branch_a=m_w_branch_a, w_branch_b=m_w_branch_b,
               w_out=m_w_out, ffn2_norm=m_ffn2_norm, ffn2_w_in=m_ffn2_w_in, ffn2_w_out=m_ffn2_w_out)
    var = dict(ffn1_norm=v_ffn1_norm, ffn1_w_in=v_ffn1_w_in, ffn1_w_out=v_ffn1_w_out, mix_norm=v_mix_norm, w_in=v_w_in,
               dn_conv_w=v_dn_conv_w, dn_a_log=v_dn_a_log, dn_dt_bias=v_dn_dt_bias, dn_out_norm=v_dn_out_norm,
               sb_q_norm=v_sb_q_norm, sb_k_norm=v_sb_k_norm, w_branch_a=v_w_branch_a, w_branch_b=v_w_branch_b,
               w_out=v_w_out, ffn2_norm=v_ffn2_norm, ffn2_w_in=v_ffn2_w_in, ffn2_w_out=v_ffn2_w_out)

    _, t, d = x.shape
    depth = ffn1_norm.shape[0]
    n_heads = d // HEAD_DIM
    conv_cols = dn_conv_w.shape[2]
    assert d % HEAD_DIM == 0 and t % SB_BLOCK == 0 and 2 * n_heads <= LANES and depth % 2 == 0
    assert w_in.shape[2] * N_CHIPS == 9 * d + 2 * n_heads and conv_cols * N_CHIPS == 3 * d

    x_idx, y_idx, c_idx = _place()
    shard = 2 * x_idx + y_idx
    c_arr = jnp.reshape(c_idx, (1,)).astype(jnp.int32)
    s_arr = jnp.reshape(shard, (1,)).astype(jnp.int32)

    mine = {n: w[n].astype(BF16) for n in BIG}
    runs = _mixer_runs(w_in.shape[2], 4 * d, 2 * n_heads)
    first = ("ffn1_w_in", "ffn1_w_out")
    rest = tuple(n for n in BIG if n not in first)

    def gather(names, l, collective_id):
        return dict(zip(names, _allgather_layer([mine[n] for n in names], l, collective_id)))

    arriving = [{**gather(first, 0, 0), **gather(rest, 0, depth)}] + [gather(BIG, l, l) for l in range(1, depth)]

    def layer_weights(l, names, after):
        gathered, after = lax.optimization_barrier(({n: arriving[l][n] for n in names}, after))
        full = {}
        for n in names:
            g = lax.dynamic_update_slice(gathered[n], mine[n][l][None], (shard, 0, 0))
            if n == "w_in":
                full["w_main"], full["w_ba"] = _split_mixer_weight(g, runs)
            else:
                full[n] = _join_shards(n, g)
        return full, after

    conv_place = lax.dynamic_update_slice(jnp.zeros((depth, DN_CONV, 3 * d), F32), dn_conv_w, (0, 0, shard * conv_cols))
    conv_rows = _pad_small(conv_place.reshape(-1))
    conv_full = (0.5 * _allreduce_small(conv_rows, "allgather_conv")).reshape(-1)[:depth * DN_CONV * 3 * d]
    conv_full = jnp.pad(conv_full.reshape(depth, DN_CONV, 3 * d), ((0, 0), (0, CONV_ROWS - DN_CONV), (0, 0)))

    def head_row(vals):
        return jnp.pad(vals, (n_heads, LANES - 2 * n_heads)).reshape(1, LANES)

    saved, layers = [], []
    cur = x[0]
    for l in range(depth):
        full, x0 = layer_weights(l, first, cur)
        x1 = _ffn_fwd(x0, ffn1_norm[l][None], full["ffn1_w_in"], full["ffn1_w_out"], 0)
        later, x1 = layer_weights(l, rest, x1)
        full.update(later)
        layers.append(full)
        proj, ba = _proj_fwd(x1, mix_norm[l][None], full["w_main"], full["w_ba"], 0)
        act = _dn_prep_fwd(proj, conv_full[l], n_heads)
        alog, dtb = head_row(dn_a_log[l]), head_row(dn_dt_bias[l])
        oa, snaps = _delta_fwd(act, ba, alog, dtb, n_heads)
        qb, kb = _sb_prep_fwd(proj, sb_q_norm[l][None], sb_k_norm[l][None], n_heads)
        ob, ltot = _sb_attn_fwd(qb, kb, proj, n_heads)
        x2 = _merge_fwd(x1, oa, proj, ob, dn_out_norm[l][None], full["w_branch_a"], full["w_branch_b"],
                        full["w_out"], 0, n_heads)
        cur = _ffn_fwd(x2, ffn2_norm[l][None], full["ffn2_w_in"], full["ffn2_w_out"], 0)
        saved.append((x0, x1, proj, ba, act, alog, dtb, oa, snaps, qb, kb, ob, ltot, x2))

    dcur, loss_part = _loss_head(cur, loss_target[0])

    grads = {n: [None] * depth for n in WEIGHTS}
    reduced = [None] * len(BIG)
    pending = None

    def finish_reduce(l, parts, arrived, after):
        arrived, after = lax.optimization_barrier((arrived, after))
        halves = [_sum_partials(p, got, s_arr, c_arr, l, depth, stacked=buf)
                  for p, got, buf in zip(parts, arrived, reduced)]
        reduced[:] = _join_halves(halves, l)
        return after

    for l in reversed(range(depth)):
        x0, x1, proj, ba, act, alog, dtb, oa, snaps, qb, kb, ob, ltot, x2 = saved[l]
        full = layers[l]
        dx2, dg, dwi, dwo = _ffn_bwd(x2, ffn2_norm[l][None], dcur, full["ffn2_w_in"], full["ffn2_w_out"], 0)
        grads["ffn2_norm"][l] = dg[0]
        grads["ffn2_w_in"][l] = dwi
        grads["ffn2_w_out"][l] = _row_shards(dwo)
        doa, dz, dob, dga, dgb, dgn, dwa, dwb, dwout = _merge_bwd(
            oa, proj, ob, dx2, dn_out_norm[l][None], full["w_branch_a"], full["w_branch_b"], full["w_out"], 0,
            n_heads)
        grads["dn_out_norm"][l] = dgn[0]
        grads["w_branch_a"][l], grads["w_branch_b"][l] = _row_shards(dwa), _row_shards(dwb)
        grads["w_out"][l] = _row_shards(dwout)
        dqb, dkb, dvb = _sb_attn_bwd(qb, kb, proj, ltot, dob, n_heads)
        dsq, dsk, dqn, dkn = _sb_prep_bwd(proj, sb_q_norm[l][None], sb_k_norm[l][None], dqb, dkb, n_heads)
        grads["sb_q_norm"][l], grads["sb_k_norm"][l] = dqn[0], dkn[0]
        dact, dba, dal, ddt = _delta_bwd(act, ba, alog, dtb, snaps, doa, n_heads)
        grads["dn_a_log"][l] = dal[0, n_heads:2 * n_heads]
        grads["dn_dt_bias"][l] = ddt[0, n_heads:2 * n_heads]
        dqkv, dconv = _dn_prep_bwd(proj, conv_full[l], dact, n_heads)
        grads["dn_conv_w"][l] = dconv[:DN_CONV]
        dproj = jnp.concatenate([dqkv, dz, dsq, dsk, dvb.astype(BF16), dga, dgb], axis=1)
        dx1, dg, dwm, dwba = _proj_bwd(x1, mix_norm[l][None], dx2, dproj, dba, full["w_main"], full["w_ba"], 0)
        grads["mix_norm"][l] = dg[0]
        grads["w_in"][l] = _join_mixer_grad(dwm, dwba, runs)
        dcur, dg, dwi, dwo = _ffn_bwd(x0, ffn1_norm[l][None], dx1, full["ffn1_w_in"], full["ffn1_w_out"], 0)
        grads["ffn1_norm"][l] = dg[0]
        grads["ffn1_w_in"][l] = dwi
        grads["ffn1_w_out"][l] = _row_shards(dwo)

        if pending is not None:
            dcur = finish_reduce(*pending, dcur)
        g_major = [grads[n][l] for n in BIG]
        parts = [_add_half(g, got, c_arr) for g, got in zip(g_major, _swap_halves(g_major))]
        pending = (l, parts, _scatter_partials(parts, l, collective_id=depth + 1 + l))
    dcur = finish_reduce(*pending, dcur)
    final = dict(zip(BIG, reduced))
    grads = {n: jnp.stack(grads[n]) for n in SMALL + ("dn_conv_w",)}

    small_names = SMALL + ("dn_conv_w",)
    small_sizes = [int(np.prod(grads[n].shape)) for n in small_names]
    small_off = np.concatenate([[0], np.cumsum(small_sizes)])
    small = jnp.concatenate([grads[n].reshape(-1) for n in small_names] + [loss_part[0, :1]])
    small_sum = _allreduce_small(_pad_small(small), "allreduce_small").reshape(-1)
    for i, n in enumerate(small_names):
        final[n] = small_sum[small_off[i]:small_off[i + 1]].reshape(grads[n].shape)
    final["dn_conv_w"] = lax.dynamic_slice(final["dn_conv_w"], (0, 0, shard * conv_cols), (depth, DN_CONV, conv_cols))
    loss = small_sum[small_off[-1]]

    deltas, new_m, new_v = {}, {}, {}
    for n in WEIGHTS:
        shape = w[n].shape
        flat = (-1, shape[-1])
        dl, m2, v2 = _adamw(w[n].reshape(flat), final[n].reshape(flat), mom[n].reshape(flat), var[n].reshape(flat))
        deltas[n], new_m[n], new_v[n] = dl.reshape(shape), m2.reshape(shape), v2.reshape(shape)

    grad_x = dcur[None]
    return (loss, grad_x, *[final[n] for n in WEIGHTS], *[deltas[n] for n in WEIGHTS],
            *[new_m[n] for n in WEIGHTS], *[new_v[n] for n in WEIGHTS])
```

```python
import functools

import jax
import jax.numpy as jnp
import numpy as np
from jax import lax
from jax.experimental import pallas as pl
from jax.experimental.pallas import tpu as pltpu
from jax.experimental.pallas import tpu_sc as plsc

F32 = jnp.float32
BF16 = jnp.bfloat16

LANES = 128
HEAD_DIM = 128
DN_CHUNK = 64
DN_CONV = 4
CONV_ROWS = 8
SB_BLOCK = 128
FFN_HALF = 0.5
RMS_EPS = 1e-6
L2_EPS = 1e-6
NEG_BIG = -1e30
ADAM_LR = 0.001
ADAM_B1 = 0.9
ADAM_B2 = 0.999
ADAM_EPS = 1e-08
ADAM_WD = 0.01
ADAM_STEP = 10
VMEM_LIMIT = 56 * 1024 * 1024
N_CHIPS = 4
N_DEV = 8
MESH = pl.DeviceIdType.MESH

BIG = ("ffn1_w_in", "ffn1_w_out", "w_in", "w_branch_a", "w_branch_b", "w_out", "ffn2_w_in", "ffn2_w_out")
COL_SHARDED = ("ffn1_w_in", "w_in", "ffn2_w_in")
SMALL = ("ffn1_norm", "mix_norm", "dn_a_log", "dn_dt_bias", "dn_out_norm", "sb_q_norm", "sb_k_norm", "ffn2_norm")
WEIGHTS = ("ffn1_norm", "ffn1_w_in", "ffn1_w_out", "mix_norm", "w_in", "dn_conv_w", "dn_a_log", "dn_dt_bias",
           "dn_out_norm", "sb_q_norm", "sb_k_norm", "w_branch_a", "w_branch_b", "w_out", "ffn2_norm", "ffn2_w_in",
           "ffn2_w_out")


def _params(**kw):
    return pltpu.CompilerParams(vmem_limit_bytes=VMEM_LIMIT, **kw)


def _pick(n, options):
    for o in options:
        if n % o == 0:
            return o
    return n


def _const_spec(shape, single=False):
    nd = len(shape)
    if single:
        return pl.BlockSpec(shape, lambda *_: (0,) * nd, pipeline_mode=pl.Buffered(1))
    return pl.BlockSpec(shape, lambda *_: (0,) * nd)


_NN = ((1,), (0,))
_NT = ((1,), (1,))
_TN = ((0,), (0,))


def _dot(a, b, dims):
    return lax.dot_general(a.astype(BF16), b.astype(BF16), (dims, ((), ())), preferred_element_type=F32)


def _mm_nn(a, b):
    return _dot(a, b, _NN)


def _mm_nt(a, b):
    return _dot(a, b, _NT)


def _mm_tn(a, b):
    return _dot(a, b, _TN)


def _split(a):
    hi = a.astype(BF16)
    lo = (a - hi.astype(F32)).astype(BF16)
    return hi, lo


def _dot_precise(a, b, dims):
    dn = (dims, ((), ()))
    ah, al = _split(a)
    bh, bl = _split(b)
    out = lax.dot_general(ah, bh, dn, preferred_element_type=F32)
    out = out + lax.dot_general(ah, bl, dn, preferred_element_type=F32)
    return out + lax.dot_general(al, bh, dn, preferred_element_type=F32)


def _make_diff_mm(dot):
    @jax.custom_vjp
    def nn(a, b):
        return dot(a, b, _NN)

    @jax.custom_vjp
    def nt(a, b):
        return dot(a, b, _NT)

    @jax.custom_vjp
    def tn(a, b):
        return dot(a, b, _TN)

    nn.defvjp(lambda a, b: (dot(a, b, _NN), (a, b)), lambda r, g: (nt(g, r[1]), tn(r[0], g)))
    nt.defvjp(lambda a, b: (dot(a, b, _NT), (a, b)), lambda r, g: (nn(g, r[1]), tn(g, r[0])))
    tn.defvjp(lambda a, b: (dot(a, b, _TN), (a, b)), lambda r, g: (nt(r[1], g), nn(r[0], g)))
    return nn, nt, tn


_d_nn, _d_nt, _d_tn = _make_diff_mm(_dot)
_p_nn, _p_nt, _p_tn = _make_diff_mm(_dot_precise)


def _softplus_raw(x):
    return jnp.maximum(x, 0.0) + jnp.log(1.0 + jnp.exp(-jnp.abs(x)))


@jax.custom_vjp
def _softplus(x):
    return _softplus_raw(x)


_softplus.defvjp(lambda x: (_softplus_raw(x), x), lambda x, g: (g * jax.nn.sigmoid(x),))


def _rms(x, gain, eps):
    return x * lax.rsqrt(jnp.mean(x * x, axis=-1, keepdims=True) + eps) * gain


def _silu(x):
    return x * jax.nn.sigmoid(x)


def _shift_rows_raw(x, k, down):
    n = x.shape[0]
    row = lax.broadcasted_iota(jnp.int32, x.shape, 0)
    if down:
        return jnp.where(row >= k, pltpu.roll(x, k, 0), 0.0)
    return jnp.where(row < n - k, pltpu.roll(x, n - k, 0), 0.0)


@functools.partial(jax.custom_vjp, nondiff_argnums=(1,))
def _shift_down(x, k):
    return _shift_rows_raw(x, k, True)


_shift_down.defvjp(lambda x, k: (_shift_rows_raw(x, k, True), None),
                   lambda k, _, g: (_shift_rows_raw(g, k, False),))


def _layer_spec(layer, block, index_map, single=False):
    full_map = lambda *a: (layer,) + tuple(index_map(*a))
    if single:
        return pl.BlockSpec((None,) + block, full_map, pipeline_mode=pl.Buffered(1))
    return pl.BlockSpec((None,) + block, full_map)


def _ffn_fwd(x, gain, w_in, w_out, layer):
    t, d = x.shape
    f = w_out.shape[1]
    fc = _pick(f, (256, 128))
    nj = f // fc
    rt = _pick(t, (512, 256, 128))

    def body(x_ref, g_ref, wg_ref, wu_ref, wo_ref, o_ref, hs_ref):
        @pl.when(pl.program_id(0) == 0)
        def _():
            for r in range(t // rt):
                rows = pl.ds(r * rt, rt)
                xr = x_ref[rows, :]
                hs_ref[rows, :] = _rms(xr, g_ref[...], RMS_EPS).astype(BF16)
                o_ref[rows, :] = xr

        for r in range(t // rt):
            rows = pl.ds(r * rt, rt)
            h = hs_ref[rows, :]
            a = _mm_nn(h, wg_ref[...])
            b = _mm_nn(h, wu_ref[...])
            o_ref[rows, :] += FFN_HALF * _mm_nn(_silu(a) * b, wo_ref[...])

    return pl.pallas_call(
        body, name="ffn_fwd", grid=(nj,),
        in_specs=[_const_spec((t, d), True), _const_spec((1, d)),
                  _layer_spec(layer, (d, fc), lambda j: (0, j)), _layer_spec(layer, (d, fc), lambda j: (0, nj + j)),
                  _layer_spec(layer, (fc, d), lambda j: (j, 0))],
        out_specs=_const_spec((t, d)),
        out_shape=jax.ShapeDtypeStruct((t, d), F32),
        scratch_shapes=[pltpu.VMEM((t, d), BF16)],
        compiler_params=_params(dimension_semantics=("arbitrary",)),
    )(x, gain, w_in, w_in, w_out)


def _ffn_bwd(x, gain, dy, w_in, w_out, layer):
    t, d = x.shape
    f = w_out.shape[1]
    fc = _pick(f, (256, 128))
    nj = f // fc
    rt = _pick(t, (512, 256, 128))
    nr = t // rt

    def body(x_ref, g_ref, dy_ref, wg_ref, wu_ref, wo_ref, dx_ref, dg_ref, dwi_ref, dwo_ref,
             hs_ref, dwg_acc, dwu_acc, dwo_acc):
        j = pl.program_id(0)

        @pl.when(j == 0)
        def _():
            for r in range(nr):
                rows = pl.ds(r * rt, rt)
                hs_ref[rows, :] = _rms(x_ref[rows, :], g_ref[...], RMS_EPS).astype(BF16)
                dx_ref[rows, :] = jnp.zeros((rt, d), F32)

        for r in range(nr):
            rows = pl.ds(r * rt, rt)
            h = hs_ref[rows, :]
            dy2 = (FFN_HALF * dy_ref[rows, :]).astype(BF16)
            a = _mm_nn(h, wg_ref[...])
            b = _mm_nn(h, wu_ref[...])
            sig = jax.nn.sigmoid(a)
            sa = a * sig
            ds = _mm_nt(dy2, wo_ref[...])
            da = ds * b * (sig * (1.0 + a * (1.0 - sig)))
            db = ds * sa
            dx_ref[rows, :] += _mm_nt(da, wg_ref[...]) + _mm_nt(db, wu_ref[...])
            dwo_c = _mm_tn(sa * b, dy2)
            dwg_c = _mm_tn(h, da)
            dwu_c = _mm_tn(h, db)
            if r == 0:
                dwo_acc[...] = dwo_c
                dwg_acc[...] = dwg_c
                dwu_acc[...] = dwu_c
            else:
                dwo_acc[...] += dwo_c
                dwg_acc[...] += dwg_c
                dwu_acc[...] += dwu_c
        dwi_ref[0] = dwg_acc[...].astype(BF16)
        dwi_ref[1] = dwu_acc[...].astype(BF16)
        dwo_ref[...] = dwo_acc[...].astype(BF16)

        @pl.when(j == nj - 1)
        def _():
            for r in range(nr):
                rows = pl.ds(r * rt, rt)
                _, vjp = jax.vjp(lambda xx, gg: _rms(xx, gg, RMS_EPS), x_ref[rows, :], g_ref[...])
                dxn, dgr = vjp(dx_ref[rows, :])
                dx_ref[rows, :] = dy_ref[rows, :] + dxn
                if r == 0:
                    dg_ref[...] = dgr
                else:
                    dg_ref[...] += dgr

    return pl.pallas_call(
        body, name="ffn_bwd", grid=(nj,),
        in_specs=[_const_spec((t, d), True), _const_spec((1, d)), _const_spec((t, d), True),
                  _layer_spec(layer, (d, fc), lambda j: (0, j)), _layer_spec(layer, (d, fc), lambda j: (0, nj + j)),
                  _layer_spec(layer, (fc, d), lambda j: (j, 0))],
        out_specs=[_const_spec((t, d)), _const_spec((1, d)),
                   pl.BlockSpec((2, d, fc), lambda j: (0, 0, j)), pl.BlockSpec((fc, d), lambda j: (j, 0))],
        out_shape=[jax.ShapeDtypeStruct((t, d), F32), jax.ShapeDtypeStruct((1, d), F32),
                   jax.ShapeDtypeStruct((2, d, f), BF16), jax.ShapeDtypeStruct((f, d), BF16)],
        scratch_shapes=[pltpu.VMEM((t, d), BF16), pltpu.VMEM((d, fc), F32), pltpu.VMEM((d, fc), F32),
                        pltpu.VMEM((fc, d), F32)],
        compiler_params=_params(dimension_semantics=("arbitrary",)),
    )(x, gain, dy, w_in, w_in, w_out)


def _proj_fwd(x, gain, w, wba, layer):
    t, d = x.shape
    n = w.shape[2]
    nc = _pick(n, (512, 256, 128))
    rt = _pick(t, (512, 256, 128))

    def body(x_ref, g_ref, w_ref, wba_ref, p_ref, ba_ref, hs_ref):
        @pl.when(pl.program_id(0) == 0)
        def _():
            for r in range(t // rt):
                rows = pl.ds(r * rt, rt)
                h = _rms(x_ref[rows, :], g_ref[...], RMS_EPS).astype(BF16)
                hs_ref[rows, :] = h
                ba_ref[rows, :] = _mm_nn(h, wba_ref[...])

        for r in range(t // rt):
            rows = pl.ds(r * rt, rt)
            p_ref[rows, :] = _mm_nn(hs_ref[rows, :], w_ref[...])

    return pl.pallas_call(
        body, name="proj_fwd", grid=(n // nc,),
        in_specs=[_const_spec((t, d), True), _const_spec((1, d)),
                  _layer_spec(layer, (d, nc), lambda j: (0, j)), _layer_spec(layer, (d, LANES), lambda j: (0, 0))],
        out_specs=[pl.BlockSpec((t, nc), lambda j: (0, j)), _const_spec((t, LANES))],
        out_shape=[jax.ShapeDtypeStruct((t, n), F32), jax.ShapeDtypeStruct((t, LANES), F32)],
        scratch_shapes=[pltpu.VMEM((t, d), BF16)],
        compiler_params=_params(dimension_semantics=("arbitrary",)),
    )(x, gain, w, wba)


def _proj_bwd(x, gain, dres, dp, dba, w, wba, layer):
    t, d = x.shape
    n = w.shape[2]
    nc = _pick(n, (512, 256, 128))
    nj = n // nc
    rt = _pick(t, (512, 256, 128))
    nr = t // rt

    def body(x_ref, g_ref, dres_ref, dp_ref, dba_ref, w_ref, wba_ref, dx_ref, dg_ref, dw_ref, dwba_ref, hs_ref, dw_acc):
        j = pl.program_id(0)

        @pl.when(j == 0)
        def _():
            for r in range(nr):
                rows = pl.ds(r * rt, rt)
                h = _rms(x_ref[rows, :], g_ref[...], RMS_EPS).astype(BF16)
                hs_ref[rows, :] = h
                g = dba_ref[rows, :]
                dx_ref[rows, :] = _mm_nt(g, wba_ref[...])
                if r == 0:
                    dwba_ref[...] = _mm_tn(h, g)
                else:
                    dwba_ref[...] += _mm_tn(h, g)

        for r in range(nr):
            rows = pl.ds(r * rt, rt)
            g = dp_ref[rows, :]
            dx_ref[rows, :] += _mm_nt(g, w_ref[...])
            if r == 0:
                dw_acc[...] = _mm_tn(hs_ref[rows, :], g)
            else:
                dw_acc[...] += _mm_tn(hs_ref[rows, :], g)
        dw_ref[...] = dw_acc[...].astype(BF16)

        @pl.when(j == nj - 1)
        def _():
            for r in range(nr):
                rows = pl.ds(r * rt, rt)
                _, vjp = jax.vjp(lambda xx, gg: _rms(xx, gg, RMS_EPS), x_ref[rows, :], g_ref[...])
                dxn, dgr = vjp(dx_ref[rows, :])
                dx_ref[rows, :] = dres_ref[rows, :] + dxn
                if r == 0:
                    dg_ref[...] = dgr
                else:
                    dg_ref[...] += dgr

    return pl.pallas_call(
        body, name="proj_bwd", grid=(nj,),
        in_specs=[_const_spec((t, d), True), _const_spec((1, d)), _const_spec((t, d), True),
                  pl.BlockSpec((t, nc), lambda j: (0, j)), _const_spec((t, LANES)),
                  _layer_spec(layer, (d, nc), lambda j: (0, j)), _layer_spec(layer, (d, LANES), lambda j: (0, 0))],
        out_specs=[_const_spec((t, d)), _const_spec((1, d)),
                   pl.BlockSpec((d, nc), lambda j: (0, j)), _const_spec((d, LANES))],
        out_shape=[jax.ShapeDtypeStruct((t, d), F32), jax.ShapeDtypeStruct((1, d), F32),
                   jax.ShapeDtypeStruct((d, n), BF16), jax.ShapeDtypeStruct((d, LANES), F32)],
        scratch_shapes=[pltpu.VMEM((t, d), BF16), pltpu.VMEM((d, nc), F32)],
        compiler_params=_params(dimension_semantics=("arbitrary",)),
    )(x, gain, dres, dp, dba, w, wba)


def _conv_act(x, w0, w1, w2, w3, is_qk):
    y = w3 * x + w2 * _shift_down(x, 1) + w1 * _shift_down(x, 2) + w0 * _shift_down(x, 3)
    y = _silu(y)
    inv = lax.rsqrt(jnp.sum(y * y, axis=-1, keepdims=True) + L2_EPS)
    return y * (is_qk * inv + (1.0 - is_qk))


def _taps(w_ref):
    return tuple(w_ref[i:i + 1, :] for i in range(DN_CONV))


def _dn_prep_fwd(proj, conv_w, n_heads):
    t = proj.shape[0]
    nb = 3 * n_heads

    def body(x_ref, w_ref, o_ref):
        is_qk = jnp.where(pl.program_id(0) < 2 * n_heads, 1.0, 0.0).astype(F32)
        o_ref[...] = _conv_act(x_ref[...], *_taps(w_ref), is_qk)

    return pl.pallas_call(
        body, name="dn_prep_fwd", grid=(nb,),
        in_specs=[pl.BlockSpec((t, HEAD_DIM), lambda i: (0, i)), pl.BlockSpec((CONV_ROWS, HEAD_DIM), lambda i: (0, i))],
        out_specs=pl.BlockSpec((t, HEAD_DIM), lambda i: (0, i)),
        out_shape=jax.ShapeDtypeStruct((t, nb * HEAD_DIM), F32),
        compiler_params=_params(dimension_semantics=("arbitrary",)),
    )(proj, conv_w)


def _dn_prep_bwd(proj, conv_w, dact, n_heads):
    t = proj.shape[0]
    nb = 3 * n_heads

    def body(x_ref, w_ref, g_ref, dx_ref, dw_ref):
        is_qk = jnp.where(pl.program_id(0) < 2 * n_heads, 1.0, 0.0).astype(F32)
        _, vjp = jax.vjp(lambda x, a, b, c, e: _conv_act(x, a, b, c, e, is_qk), x_ref[...], *_taps(w_ref))
        dx, d0, d1, d2, d3 = vjp(g_ref[...])
        dx_ref[...] = dx.astype(BF16)
        dw_ref[...] = jnp.concatenate([d0, d1, d2, d3, jnp.zeros((CONV_ROWS - DN_CONV, HEAD_DIM), F32)], axis=0)

    return pl.pallas_call(
        body, name="dn_prep_bwd", grid=(nb,),
        in_specs=[pl.BlockSpec((t, HEAD_DIM), lambda i: (0, i)), pl.BlockSpec((CONV_ROWS, HEAD_DIM), lambda i: (0, i)),
                  pl.BlockSpec((t, HEAD_DIM), lambda i: (0, i))],
        out_specs=[pl.BlockSpec((t, HEAD_DIM), lambda i: (0, i)), pl.BlockSpec((CONV_ROWS, HEAD_DIM), lambda i: (0, i))],
        out_shape=[jax.ShapeDtypeStruct((t, nb * HEAD_DIM), BF16), jax.ShapeDtypeStruct((CONV_ROWS, nb * HEAD_DIM), F32)],
        compiler_params=_params(dimension_semantics=("arbitrary",)),
    )(proj, conv_w, dact)


def _unit_lower_inverses(lmats, c):
    r = lax.broadcasted_iota(jnp.int32, (c, c), 0)
    q = lax.broadcasted_iota(jnp.int32, (c, c), 1)
    eye = jnp.where(r == q, 1.0, 0.0)
    ps = [eye - l for l in lmats]
    ms = [_p_nn(l, l) for l in lmats]
    n = 2
    while True:
        ps = [p + _p_nn(p, m) for p, m in zip(ps, ms)]
        if 2 * n >= c:
            return ps
        ms = [_p_nn(m, m) for m in ms]
        n *= 2


def _delta_heads(qs, ks, vs, bg, alog, dtb, states):
    n_heads = len(qs)
    heads = range(n_heads)
    c = qs[0].shape[0]
    lane = lax.broadcasted_iota(jnp.int32, (c, LANES), 1)
    r = lax.broadcasted_iota(jnp.int32, (c, c), 0)
    s = lax.broadcasted_iota(jnp.int32, (c, c), 1)
    beta_all = jax.nn.sigmoid(bg)
    g_all = -jnp.exp(alog) * _softplus(bg + dtb)
    beta = [jnp.sum(jnp.where(lane == h, beta_all, 0.0), axis=1, keepdims=True) for h in heads]
    g = [jnp.sum(jnp.where(lane == n_heads + h, g_all, 0.0), axis=1, keepdims=True) for h in heads]
    g_row = [jnp.sum(jnp.where(r == s, g[h], 0.0), axis=0, keepdims=True) for h in heads]
    gc = [jnp.sum(jnp.where(s <= r, g_row[h], 0.0), axis=1, keepdims=True) for h in heads]
    gr = [jnp.sum(jnp.where(r <= s, g[h], 0.0), axis=0, keepdims=True) for h in heads]
    g_last = [jnp.sum(g[h], axis=0, keepdims=True) for h in heads]
    decay = [jnp.exp(jnp.where(r >= s, gc[h] - gr[h], NEG_BIG)) for h in heads]
    q_scaled = [qs[h] * (HEAD_DIM ** -0.5) for h in heads]
    k_beta = [ks[h] * beta[h] for h in heads]
    lmat = [jnp.where(r > s, _d_nt(k_beta[h], ks[h]) * decay[h], 0.0) for h in heads]
    attn = [_d_nt(q_scaled[h], ks[h]) * decay[h] for h in heads]
    tinv = _unit_lower_inverses(lmat, c)
    u = [_p_nn(tinv[h], vs[h] * beta[h]) for h in heads]
    w = [_p_nn(tinv[h], k_beta[h] * jnp.exp(gc[h])) for h in heads]
    v_new = [u[h] - _d_nn(w[h], states[h]) for h in heads]
    o_state = [_d_nn(q_scaled[h] * jnp.exp(gc[h]), states[h]) for h in heads]
    o = [o_state[h] + _d_nn(attn[h], v_new[h]) for h in heads]
    kv = [_d_tn(ks[h] * jnp.exp(g_last[h] - gc[h]), v_new[h]) for h in heads]
    new_states = [states[h] * jnp.exp(g_last[h]) + kv[h] for h in heads]
    return tuple(o), tuple(new_states)


def _delta_fwd(act, ba, alog, dtb, n_heads):
    t = act.shape[0]
    d = n_heads * HEAD_DIM
    c = DN_CHUNK
    nc = t // c

    def body(q_ref, k_ref, v_ref, bg_ref, al_ref, dt_ref, o_ref, snap_ref, st_ref):
        @pl.when(pl.program_id(0) == 0)
        def _():
            st_ref[...] = jnp.zeros(st_ref.shape, F32)

        snap_ref[0] = st_ref[...]
        cols = [slice(h * HEAD_DIM, (h + 1) * HEAD_DIM) for h in range(n_heads)]
        os, new_states = _delta_heads([q_ref[:, sl] for sl in cols], [k_ref[:, sl] for sl in cols],
                                      [v_ref[:, sl] for sl in cols], bg_ref[...], al_ref[...], dt_ref[...],
                                      [st_ref[h] for h in range(n_heads)])
        for h, sl in enumerate(cols):
            o_ref[:, sl] = os[h]
            st_ref[h] = new_states[h]

    return pl.pallas_call(
        body, name="delta_fwd", grid=(nc,),
        in_specs=[pl.BlockSpec((c, d), lambda i: (i, 0)), pl.BlockSpec((c, d), lambda i: (i, 1)),
                  pl.BlockSpec((c, d), lambda i: (i, 2)), pl.BlockSpec((c, LANES), lambda i: (i, 0)),
                  _const_spec((1, LANES)), _const_spec((1, LANES))],
        out_specs=[pl.BlockSpec((c, d), lambda i: (i, 0)),
                   pl.BlockSpec((1, n_heads, HEAD_DIM, HEAD_DIM), lambda i: (i, 0, 0, 0))],
        out_shape=[jax.ShapeDtypeStruct((t, d), F32), jax.ShapeDtypeStruct((nc, n_heads, HEAD_DIM, HEAD_DIM), F32)],
        scratch_shapes=[pltpu.VMEM((n_heads, HEAD_DIM, HEAD_DIM), F32)],
        compiler_params=_params(dimension_semantics=("arbitrary",)),
    )(act, act, act, ba, alog, dtb)


def _delta_bwd(act, ba, alog, dtb, snaps, do, n_heads):
    t = act.shape[0]
    d = n_heads * HEAD_DIM
    c = DN_CHUNK
    nc = t // c

    def body(q_ref, k_ref, v_ref, bg_ref, al_ref, dt_ref, snap_ref, do_ref,
             dact_ref, dbg_ref, dal_ref, ddt_ref, ds_ref):
        @pl.when(pl.program_id(0) == 0)
        def _():
            ds_ref[...] = jnp.zeros(ds_ref.shape, F32)
            dal_ref[...] = jnp.zeros((1, LANES), F32)
            ddt_ref[...] = jnp.zeros((1, LANES), F32)

        heads = range(n_heads)
        cols = [slice(h * HEAD_DIM, (h + 1) * HEAD_DIM) for h in heads]
        _, vjp = jax.vjp(_delta_heads, tuple(q_ref[:, sl] for sl in cols), tuple(k_ref[:, sl] for sl in cols),
                         tuple(v_ref[:, sl] for sl in cols), bg_ref[...], al_ref[...], dt_ref[...],
                         tuple(snap_ref[0, h] for h in heads))
        dq, dk, dv, dbg, dal, ddt, dst = vjp((tuple(do_ref[:, sl] for sl in cols), tuple(ds_ref[h] for h in heads)))
        for h, sl in enumerate(cols):
            dact_ref[:, sl] = dq[h]
            dact_ref[:, d + h * HEAD_DIM:d + (h + 1) * HEAD_DIM] = dk[h]
            dact_ref[:, 2 * d + h * HEAD_DIM:2 * d + (h + 1) * HEAD_DIM] = dv[h]
            ds_ref[h] = dst[h]
        dal_ref[...] += dal
        ddt_ref[...] += ddt
        dbg_ref[...] = dbg.astype(BF16)

    rev = lambda i: nc - 1 - i
    return pl.pallas_call(
        body, name="delta_bwd", grid=(nc,),
        in_specs=[pl.BlockSpec((c, d), lambda i: (rev(i), 0)), pl.BlockSpec((c, d), lambda i: (rev(i), 1)),
                  pl.BlockSpec((c, d), lambda i: (rev(i), 2)), pl.BlockSpec((c, LANES), lambda i: (rev(i), 0)),
                  _const_spec((1, LANES)), _const_spec((1, LANES)),
                  pl.BlockSpec((1, n_heads, HEAD_DIM, HEAD_DIM), lambda i: (rev(i), 0, 0, 0)),
                  pl.BlockSpec((c, d), lambda i: (rev(i), 0))],
        out_specs=[pl.BlockSpec((c, 3 * d), lambda i: (rev(i), 0)), pl.BlockSpec((c, LANES), lambda i: (rev(i), 0)),
                   _const_spec((1, LANES)), _const_spec((1, LANES))],
        out_shape=[jax.ShapeDtypeStruct((t, 3 * d), F32), jax.ShapeDtypeStruct((t, LANES), BF16),
                   jax.ShapeDtypeStruct((1, LANES), F32), jax.ShapeDtypeStruct((1, LANES), F32)],
        scratch_shapes=[pltpu.VMEM((n_heads, HEAD_DIM, HEAD_DIM), F32)],
        compiler_params=_params(dimension_semantics=("arbitrary",)),
    )(act, act, act, ba, alog, dtb, snaps, do)


def _head_norm2(a, b, ga, gb):
    return _rms(a, ga, RMS_EPS), _rms(b, gb, RMS_EPS)


def _sb_prep_fwd(proj, qn, kn, n_heads):
    t = proj.shape[0]
    d = n_heads * HEAD_DIM
    tm = _pick(t, (256, 128))

    def body(q_ref, k_ref, qn_ref, kn_ref, qo_ref, ko_ref):
        for h in range(n_heads):
            sl = slice(h * HEAD_DIM, (h + 1) * HEAD_DIM)
            qo_ref[:, sl], ko_ref[:, sl] = _head_norm2(q_ref[:, sl], k_ref[:, sl], qn_ref[...], kn_ref[...])

    return pl.pallas_call(
        body, name="sb_prep_fwd", grid=(t // tm,),
        in_specs=[pl.BlockSpec((tm, d), lambda i: (i, 4)), pl.BlockSpec((tm, d), lambda i: (i, 5)),
                  _const_spec((1, HEAD_DIM)), _const_spec((1, HEAD_DIM))],
        out_specs=[pl.BlockSpec((tm, d), lambda i: (i, 0)), pl.BlockSpec((tm, d), lambda i: (i, 0))],
        out_shape=[jax.ShapeDtypeStruct((t, d), F32), jax.ShapeDtypeStruct((t, d), F32)],
        compiler_params=_params(dimension_semantics=("arbitrary",)),
    )(proj, proj, qn, kn)


def _sb_prep_bwd(proj, qn, kn, dq, dk, n_heads):
    t = proj.shape[0]
    d = n_heads * HEAD_DIM
    tm = _pick(t, (256, 128))

    def body(q_ref, k_ref, qn_ref, kn_ref, dq_ref, dk_ref, dqo_ref, dko_ref, dqn_ref, dkn_ref):
        @pl.when(pl.program_id(0) == 0)
        def _():
            dqn_ref[...] = jnp.zeros((1, HEAD_DIM), F32)
            dkn_ref[...] = jnp.zeros((1, HEAD_DIM), F32)

        for h in range(n_heads):
            sl = slice(h * HEAD_DIM, (h + 1) * HEAD_DIM)
            _, vjp = jax.vjp(_head_norm2, q_ref[:, sl], k_ref[:, sl], qn_ref[...], kn_ref[...])
            da, db, dga, dgb = vjp((dq_ref[:, sl], dk_ref[:, sl]))
            dqo_ref[:, sl] = da.astype(BF16)
            dko_ref[:, sl] = db.astype(BF16)
            dqn_ref[...] += dga
            dkn_ref[...] += dgb

    return pl.pallas_call(
        body, name="sb_prep_bwd", grid=(t // tm,),
        in_specs=[pl.BlockSpec((tm, d), lambda i: (i, 4)), pl.BlockSpec((tm, d), lambda i: (i, 5)),
                  _const_spec((1, HEAD_DIM)), _const_spec((1, HEAD_DIM)),
                  pl.BlockSpec((tm, d), lambda i: (i, 0)), pl.BlockSpec((tm, d), lambda i: (i, 0))],
        out_specs=[pl.BlockSpec((tm, d), lambda i: (i, 0)), pl.BlockSpec((tm, d), lambda i: (i, 0)),
                   _const_spec((1, HEAD_DIM)), _const_spec((1, HEAD_DIM))],
        out_shape=[jax.ShapeDtypeStruct((t, d), BF16), jax.ShapeDtypeStruct((t, d), BF16),
                   jax.ShapeDtypeStruct((1, HEAD_DIM), F32), jax.ShapeDtypeStruct((1, HEAD_DIM), F32)],
        compiler_params=_params(dimension_semantics=("arbitrary",)),
    )(proj, proj, qn, kn, dq, dk)


def _cumsum_mm(x, tri):
    hi, lo = _split(x)
    return (lax.dot_general(hi, tri, (_NN, ((), ())), preferred_element_type=F32)
            + lax.dot_general(lo, tri, (_NN, ((), ())), preferred_element_type=F32))


def _sb_valid(i, j):
    row = lax.broadcasted_iota(jnp.int32, (SB_BLOCK, SB_BLOCK), 0)
    col = lax.broadcasted_iota(jnp.int32, (SB_BLOCK, SB_BLOCK), 1)
    return (col + j * SB_BLOCK) < (row + i * SB_BLOCK)


def _sb_attn_fwd(qb, kb, proj, n_heads):
    t = qb.shape[0]
    d = n_heads * HEAD_DIM
    nq = t // SB_BLOCK
    hb = _pick(n_heads, (4, 2, 1))
    wide = hb * HEAD_DIM
    v_col0 = 6 * n_heads // hb
    scale = HEAD_DIM ** -0.5

    def body(q_ref, k_ref, v_ref, o_ref, lt_ref):
        i = pl.program_id(1)
        row = lax.broadcasted_iota(jnp.int32, (SB_BLOCK, SB_BLOCK), 0)
        col = lax.broadcasted_iota(jnp.int32, (SB_BLOCK, SB_BLOCK), 1)
        after = jnp.where(row > col, 1.0, 0.0).astype(BF16)
        heads = [slice(h * HEAD_DIM, (h + 1) * HEAD_DIM) for h in range(hb)]
        every = range(hb)
        qs = [q_ref[:, sl].astype(BF16) for sl in heads]

        def step(m, carry):
            j0 = i - 2 * m
            js = (j0, jnp.maximum(j0 - 1, 0))
            valid = (_sb_valid(i, js[0]), jnp.logical_and(_sb_valid(i, js[1]), j0 >= 1))
            rows = [pl.ds(pl.multiple_of(j * SB_BLOCK, SB_BLOCK), SB_BLOCK) for j in js]
            units = [(h, b) for b in range(2) for h in every]
            z = {u: _mm_nt(qs[u[0]], k_ref[rows[u[1]], heads[u[0]]]) * scale for u in units}
            sp = {u: _softplus_raw(z[u]) for u in units}
            lm = {u: jnp.where(valid[u[1]], -sp[u], 0.0) for u in units}
            tail = {u: _cumsum_mm(lm[u], after) for u in units}
            later = {(h, 0): carry[h][1] for h in every}
            later.update({(h, 1): carry[h][1] + jnp.sum(lm[h, 0], axis=1, keepdims=True) for h in every})
            w = {u: jnp.where(valid[u[1]], jnp.exp(z[u] - sp[u] + later[u] + tail[u]), 0.0) for u in units}
            pv = {u: _mm_nn(w[u], v_ref[rows[u[1]], heads[u[0]]]) for u in units}
            return tuple((carry[h][0] + pv[h, 0] + pv[h, 1], later[h, 1] + jnp.sum(lm[h, 1], axis=1, keepdims=True))
                         for h in every)

        init = tuple((jnp.zeros((SB_BLOCK, HEAD_DIM), F32), jnp.zeros((SB_BLOCK, 1), F32)) for _ in heads)
        res = lax.fori_loop(0, (i + 2) // 2, step, init)
        for h, sl in enumerate(heads):
            o_ref[:, sl] = res[h][0]
            lt_ref[:, sl] = jnp.broadcast_to(res[h][1], (SB_BLOCK, HEAD_DIM))

    return pl.pallas_call(
        body, name="sb_attn_fwd", grid=(n_heads // hb, nq),
        in_specs=[pl.BlockSpec((SB_BLOCK, wide), lambda g, i: (i, g)),
                  pl.BlockSpec((t, wide), lambda g, i: (0, g)),
                  pl.BlockSpec((t, wide), lambda g, i: (0, v_col0 + g))],
        out_specs=[pl.BlockSpec((SB_BLOCK, wide), lambda g, i: (i, g)),
                   pl.BlockSpec((SB_BLOCK, wide), lambda g, i: (i, g))],
        out_shape=[jax.ShapeDtypeStruct((t, d), F32), jax.ShapeDtypeStruct((t, d), F32)],
        compiler_params=_params(dimension_semantics=("arbitrary", "arbitrary")),
    )(qb, kb, proj)


def _sb_attn_bwd(qb, kb, proj, ltot, do, n_heads):
    t = qb.shape[0]
    d = n_heads * HEAD_DIM
    nq = t // SB_BLOCK
    hb = _pick(n_heads, (4, 2, 1))
    wide = hb * HEAD_DIM
    v_col0 = 6 * n_heads // hb
    scale = HEAD_DIM ** -0.5

    def body(q_ref, k_ref, v_ref, lt_ref, do_ref, dq_ref, dk_ref, dv_ref):
        i = pl.program_id(1)

        @pl.when(i == 0)
        def _():
            dk_ref[...] = jnp.zeros((t, wide), F32)
            dv_ref[...] = jnp.zeros((t, wide), F32)

        row = lax.broadcasted_iota(jnp.int32, (SB_BLOCK, SB_BLOCK), 0)
        col = lax.broadcasted_iota(jnp.int32, (SB_BLOCK, SB_BLOCK), 1)
        upto = jnp.where(row <= col, 1.0, 0.0).astype(BF16)
        before = jnp.where(row < col, 1.0, 0.0).astype(BF16)
        heads = [slice(h * HEAD_DIM, (h + 1) * HEAD_DIM) for h in range(hb)]
        every = range(hb)
        qs = [q_ref[:, sl].astype(BF16) for sl in heads]
        dos = [do_ref[:, sl].astype(BF16) for sl in heads]
        totals = [jnp.max(lt_ref[:, sl], axis=1, keepdims=True) for sl in heads]

        def step(m, carry):
            js = (2 * m, jnp.minimum(2 * m + 1, nq - 1))
            valid = (_sb_valid(i, js[0]), jnp.logical_and(_sb_valid(i, js[1]), 2 * m + 1 <= i))
            rows = [pl.ds(pl.multiple_of(j * SB_BLOCK, SB_BLOCK), SB_BLOCK) for j in js]
            units = [(h, b) for b in range(2) for h in every]
            kj = {u: k_ref[rows[u[1]], heads[u[0]]].astype(BF16) for u in units}
            vj = {u: v_ref[rows[u[1]], heads[u[0]]].astype(BF16) for u in units}
            z = {u: _mm_nt(qs[u[0]], kj[u]) * scale for u in units}
            dw = {u: _mm_nt(dos[u[0]], vj[u]) for u in units}
            sp = {u: _softplus_raw(z[u]) for u in units}
            lm = {u: jnp.where(valid[u[1]], -sp[u], 0.0) for u in units}
            head = {u: _cumsum_mm(lm[u], upto) for u in units}
            lm_before = {(h, 0): carry[h][1] for h in every}
            lm_before.update({(h, 1): carry[h][1] + jnp.sum(lm[h, 0], axis=1, keepdims=True) for h in every})
            w = {u: jnp.where(valid[u[1]], jnp.exp(z[u] - sp[u] + totals[u[0]] - (lm_before[u] + head[u])), 0.0)
                 for u in units}
            e = {u: w[u] * dw[u] for u in units}
            e_local = {u: _mm_nn(e[u], before) for u in units}
            e_before = {(h, 0): carry[h][2] for h in every}
            e_before.update({(h, 1): carry[h][2] + jnp.sum(e[h, 0], axis=1, keepdims=True) for h in every})
            sig = {u: jnp.exp(z[u] - sp[u]) for u in units}
            dz = {u: jnp.where(valid[u[1]], e[u] * (1.0 - sig[u]) - (e_before[u] + e_local[u]) * sig[u], 0.0) * scale
                  for u in units}
            for h, b in units:
                dv_ref[rows[b], heads[h]] += _mm_tn(w[h, b], dos[h])
            for h, b in units:
                dk_ref[rows[b], heads[h]] += _mm_tn(dz[h, b], qs[h])
            dq = {u: _mm_nn(dz[u], kj[u]) for u in units}
            return tuple((carry[h][0] + dq[h, 0] + dq[h, 1],
                          lm_before[h, 1] + jnp.sum(lm[h, 1], axis=1, keepdims=True),
                          e_before[h, 1] + jnp.sum(e[h, 1], axis=1, keepdims=True)) for h in every)

        zero_col = jnp.zeros((SB_BLOCK, 1), F32)
        init = tuple((jnp.zeros((SB_BLOCK, HEAD_DIM), F32), zero_col, zero_col) for _ in heads)
        res = lax.fori_loop(0, (i + 2) // 2, step, init)
        for h, sl in enumerate(heads):
            dq_ref[:, sl] = res[h][0]

    return pl.pallas_call(
        body, name="sb_attn_bwd", grid=(n_heads // hb, nq),
        in_specs=[pl.BlockSpec((SB_BLOCK, wide), lambda g, i: (i, g)),
                  pl.BlockSpec((t, wide), lambda g, i: (0, g)),
                  pl.BlockSpec((t, wide), lambda g, i: (0, v_col0 + g)),
                  pl.BlockSpec((SB_BLOCK, wide), lambda g, i: (i, g)),
                  pl.BlockSpec((SB_BLOCK, wide), lambda g, i: (i, g))],
        out_specs=[pl.BlockSpec((SB_BLOCK, wide), lambda g, i: (i, g)),
                   pl.BlockSpec((t, wide), lambda g, i: (0, g)),
                   pl.BlockSpec((t, wide), lambda g, i: (0, g))],
        out_shape=[jax.ShapeDtypeStruct((t, d), F32), jax.ShapeDtypeStruct((t, d), F32),
                   jax.ShapeDtypeStruct((t, d), F32)],
        compiler_params=_params(dimension_semantics=("arbitrary", "arbitrary")),
    )(qb, kb, proj, ltot, do)


def _gated_norm(oa, z, gn):
    return _rms(oa, gn, RMS_EPS) * _silu(z)


def _merge_gates(ya, yb, ga, gb):
    return jax.nn.sigmoid(ga) * ya + jax.nn.sigmoid(gb) * yb


def _merge_fwd(x1, oa, proj, ob, gn, wa, wb, wo, layer, n_heads):
    t, d = x1.shape
    tm = _pick(t, (256, 128))
    square = _layer_spec(layer, (d, d), lambda i: (0, 0), single=True)

    def body(x_ref, oa_ref, z_ref, ob_ref, ga_ref, gb_ref, gn_ref, wa_ref, wb_ref, wo_ref, o_ref, na_ref):
        for h in range(n_heads):
            sl = slice(h * HEAD_DIM, (h + 1) * HEAD_DIM)
            na_ref[:, sl] = _gated_norm(oa_ref[:, sl], z_ref[:, sl], gn_ref[...]).astype(BF16)
        m = _merge_gates(_mm_nn(na_ref[...], wa_ref[...]), _mm_nn(ob_ref[...], wb_ref[...]), ga_ref[...], gb_ref[...])
        o_ref[...] = x_ref[...] + _mm_nn(m, wo_ref[...])

    tile = lambda k: pl.BlockSpec((tm, d), lambda i: (i, k))
    return pl.pallas_call(
        body, name="merge_fwd", grid=(t // tm,),
        in_specs=[tile(0), tile(0), tile(3), tile(0), tile(7), tile(8), _const_spec((1, HEAD_DIM)),
                  square, square, square],
        out_specs=tile(0),
        out_shape=jax.ShapeDtypeStruct((t, d), F32),
        scratch_shapes=[pltpu.VMEM((tm, d), BF16)],
        compiler_params=_params(dimension_semantics=("arbitrary",)),
    )(x1, oa, proj, ob, proj, proj, gn, wa, wb, wo)


def _merge_bwd(oa, proj, ob, dy, gn, wa, wb, wo, layer, n_heads):
    t, d = oa.shape
    tm = _pick(t, (256, 128))
    nt = t // tm
    square = _layer_spec(layer, (d, d), lambda i: (0, 0), single=True)

    def body(oa_ref, z_ref, ob_ref, ga_ref, gb_ref, dy_ref, gn_ref, wa_ref, wb_ref, wo_ref,
             doa_ref, dz_ref, dob_ref, dga_ref, dgb_ref, dgn_ref, dwa_hbm, dwb_hbm, dwo_hbm,
             na_ref, dna_ref, dwa_ref, dwb_ref, dwo_ref, stage_ref):
        i = pl.program_id(0)

        @pl.when(i == 0)
        def _():
            dgn_ref[...] = jnp.zeros((1, HEAD_DIM), F32)
            dwa_ref[...] = jnp.zeros((d, d), F32)
            dwb_ref[...] = jnp.zeros((d, d), F32)
            dwo_ref[...] = jnp.zeros((d, d), F32)

        for h in range(n_heads):
            sl = slice(h * HEAD_DIM, (h + 1) * HEAD_DIM)
            na_ref[:, sl] = _gated_norm(oa_ref[:, sl], z_ref[:, sl], gn_ref[...]).astype(BF16)
        dy = dy_ref[...].astype(BF16)
        ob = ob_ref[...].astype(BF16)
        ya = _mm_nn(na_ref[...], wa_ref[...])
        yb = _mm_nn(ob, wb_ref[...])
        m, vjp = jax.vjp(_merge_gates, ya, yb, ga_ref[...], gb_ref[...])
        dwo_ref[...] += _mm_tn(m, dy)
        dya, dyb, dga, dgb = vjp(_mm_nt(dy, wo_ref[...]))
        dga_ref[...] = dga.astype(BF16)
        dgb_ref[...] = dgb.astype(BF16)
        dwa_ref[...] += _mm_tn(na_ref[...], dya)
        dwb_ref[...] += _mm_tn(ob, dyb)
        dob_ref[...] = _mm_nt(dyb, wb_ref[...])
        dna_ref[...] = _mm_nt(dya, wa_ref[...])
        for h in range(n_heads):
            sl = slice(h * HEAD_DIM, (h + 1) * HEAD_DIM)
            _, vjp_h = jax.vjp(_gated_norm, oa_ref[:, sl], z_ref[:, sl], gn_ref[...])
            doa, dz, dgn = vjp_h(dna_ref[:, sl])
            doa_ref[:, sl] = doa
            dz_ref[:, sl] = dz.astype(BF16)
            dgn_ref[...] += dgn

        @pl.when(i == nt - 1)
        def _():
            for acc, out in ((dwa_ref, dwa_hbm), (dwb_ref, dwb_hbm), (dwo_ref, dwo_hbm)):
                stage_ref[...] = acc[...].astype(BF16)
                pltpu.sync_copy(stage_ref, out)

    tile = lambda k: pl.BlockSpec((tm, d), lambda i: (i, k))
    any_spec = pl.BlockSpec(memory_space=pl.ANY)
    return pl.pallas_call(
        body, name="merge_bwd", grid=(nt,),
        in_specs=[tile(0), tile(3), tile(0), tile(7), tile(8), tile(0), _const_spec((1, HEAD_DIM)),
                  square, square, square],
        out_specs=[tile(0), tile(0), tile(0), tile(0), tile(0), _const_spec((1, HEAD_DIM)),
                   any_spec, any_spec, any_spec],
        out_shape=[jax.ShapeDtypeStruct((t, d), F32), jax.ShapeDtypeStruct((t, d), BF16),
                   jax.ShapeDtypeStruct((t, d), F32), jax.ShapeDtypeStruct((t, d), BF16),
                   jax.ShapeDtypeStruct((t, d), BF16), jax.ShapeDtypeStruct((1, HEAD_DIM), F32),
                   jax.ShapeDtypeStruct((d, d), BF16), jax.ShapeDtypeStruct((d, d), BF16),
                   jax.ShapeDtypeStruct((d, d), BF16)],
        scratch_shapes=[pltpu.VMEM((tm, d), BF16), pltpu.VMEM((tm, d), F32),
                        pltpu.VMEM((d, d), F32), pltpu.VMEM((d, d), F32), pltpu.VMEM((d, d), F32),
                        pltpu.VMEM((d, d), BF16)],
        compiler_params=_params(dimension_semantics=("arbitrary",)),
    )(oa, proj, ob, proj, proj, dy, gn, wa, wb, wo)


def _loss_head(y, target):
    t, d = y.shape
    tm = _pick(t, (256, 128))

    def body(y_ref, t_ref, dy_ref, loss_ref):
        @pl.when(pl.program_id(0) == 0)
        def _():
            loss_ref[...] = jnp.zeros((8, LANES), F32)

        err = y_ref[...] - t_ref[...]
        dy_ref[...] = err * (1.0 / d)
        per_token = jnp.sum(err * err, axis=1, keepdims=True) * (1.0 / d)
        loss_ref[...] += 0.5 * jnp.sum(per_token, axis=0, keepdims=True)

    return pl.pallas_call(
        body, name="loss_head", grid=(t // tm,),
        in_specs=[pl.BlockSpec((tm, d), lambda i: (i, 0)), pl.BlockSpec((tm, d), lambda i: (i, 0))],
        out_specs=[pl.BlockSpec((tm, d), lambda i: (i, 0)), _const_spec((8, LANES))],
        out_shape=[jax.ShapeDtypeStruct((t, d), F32), jax.ShapeDtypeStruct((8, LANES), F32)],
        compiler_params=_params(dimension_semantics=("arbitrary",)),
    )(y, target)


def _adamw(w, g, m, v):
    rows, cols = w.shape
    tr = rows
    for cand in (512, 256, 128, 64, 32, 16, 8):
        if rows % cand == 0 and cand * cols * 4 <= 2 * 1024 * 1024:
            tr = cand
            break

    def body(w_ref, g_ref, m_ref, v_ref, d_ref, mo_ref, vo_ref):
        g = g_ref[...]
        m2 = ADAM_B1 * m_ref[...] + (1.0 - ADAM_B1) * g
        v2 = ADAM_B2 * v_ref[...] + (1.0 - ADAM_B2) * (g * g)
        m_hat = m2 / (1.0 - ADAM_B1 ** ADAM_STEP)
        v_hat = v2 / (1.0 - ADAM_B2 ** ADAM_STEP)
        d_ref[...] = -ADAM_LR * (m_hat / (jnp.sqrt(v_hat) + ADAM_EPS) + ADAM_WD * w_ref[...])
        mo_ref[...] = m2
        vo_ref[...] = v2

    spec = pl.BlockSpec((tr, cols), lambda i: (i, 0))
    shape = jax.ShapeDtypeStruct((rows, cols), F32)
    return pl.pallas_call(
        body, name="adamw", grid=(rows // tr,), in_specs=[spec] * 4, out_specs=[spec] * 3,
        out_shape=[shape] * 3, compiler_params=_params(dimension_semantics=("arbitrary",)),
    )(w, g, m, v)


def _place():
    return lax.axis_index("x"), lax.axis_index("y"), lax.axis_index("c")


def _other_chips(x, y):
    return [(1 - x, y), (x, 1 - y), (1 - x, 1 - y)]


def _tile_rows(rows, cols, itemsize, cap=1536 * 1024):
    best = None
    for cand in range(16, rows + 1, 16):
        if rows % cand == 0 and cand * cols * itemsize <= cap:
            best = cand
    return best if best is not None else rows


def _allgather_layer(shards, layer, collective_id):
    n = len(shards)

    def body(*refs):
        srcs, outs = refs[:n], refs[n:2 * n]
        send_sems, recv_sems = refs[2 * n:]
        x, y, c = _place()
        me, sibling = (x, y, c), (x, y, 1 - c)
        chips = _other_chips(x, y)
        barrier = pltpu.get_barrier_semaphore()
        for peer in [(*chip, c) for chip in chips] + [sibling]:
            pl.semaphore_signal(barrier, inc=1, device_id=peer, device_id_type=MESH)
        pl.semaphore_wait(barrier, N_CHIPS)

        def half(w, which):
            rows = shards[w].shape[1] // 2
            return pl.ds(which * rows, rows)

        def copy(w, k, shard, which, to, from_src=False):
            part = half(w, which)
            return pltpu.make_async_remote_copy(
                src_ref=srcs[w].at[layer, part] if from_src else outs[w].at[shard, part],
                dst_ref=outs[w].at[shard, part], send_sem=send_sems.at[6 * w + k], recv_sem=recv_sems.at[6 * w + k],
                device_id=to, device_id_type=MESH)

        first = [copy(w, j, 2 * x + y, c, (*chip, c), from_src=True) for j, chip in enumerate(chips) for w in range(n)]
        for cp in first:
            cp.start()
        passed = []
        for j, (cx, cy) in enumerate(chips):
            for w in range(n):
                copy(w, j, 2 * cx + cy, c, me).wait_recv()
                cp = copy(w, 3 + j, 2 * cx + cy, c, sibling)
                cp.start()
                passed.append(cp)
        for j, (cx, cy) in enumerate(chips):
            for w in range(n):
                copy(w, 3 + j, 2 * cx + cy, 1 - c, me).wait_recv()
        for cp in first + passed:
            cp.wait_send()

    return pl.kernel(
        body, name=f"allgather_layer{layer}_id{collective_id}",
        out_type=[jax.ShapeDtypeStruct((N_CHIPS,) + s.shape[1:], s.dtype) for s in shards],
        mesh=plsc.ScalarSubcoreMesh(axis_name="sequencer", num_cores=1),
        scratch_types=[pltpu.SemaphoreType.DMA((6 * n,)), pltpu.SemaphoreType.DMA((6 * n,))],
        compiler_params=pltpu.CompilerParams(collective_id=collective_id),
    )(*shards)


def _swap_halves(grads):
    n = len(grads)

    def body(*refs):
        gs, gots = refs[:n], refs[n:2 * n]
        send_sems, recv_sems = refs[2 * n:]
        x, y, c = _place()
        copies = []
        for w in range(n):
            half = grads[w].shape[1] // 2
            copies.append(pltpu.make_async_remote_copy(
                src_ref=gs[w].at[:, pl.ds((1 - c) * half, half)], dst_ref=gots[w], send_sem=send_sems.at[w],
                recv_sem=recv_sems.at[w], device_id=(x, y, 1 - c), device_id_type=MESH))
        for cp in copies:
            cp.start()
        for cp in copies:
            cp.wait()

    hbm = pl.BlockSpec(memory_space=pl.ANY)
    return pl.pallas_call(
        body, name="swap_halves", in_specs=[hbm] * n, out_specs=[hbm] * n,
        out_shape=[jax.ShapeDtypeStruct((g.shape[0], g.shape[1] // 2, g.shape[2]), g.dtype) for g in grads],
        scratch_shapes=[pltpu.SemaphoreType.DMA((n,)), pltpu.SemaphoreType.DMA((n,))],
    )(*grads)


def _add_half(grad, got, c_idx):
    n, rows, all_cols = grad.shape
    side = N_CHIPS // n
    cols = all_cols // side
    half = rows // 2
    tr = _tile_rows(half, cols, 2)
    nb = half // tr

    def body(c_ref, a_ref, b_ref, o_ref):
        o_ref[...] = (a_ref[...].astype(F32) + b_ref[...].astype(F32)).astype(o_ref.dtype)

    return pl.pallas_call(
        body, name="add_half",
        grid_spec=pltpu.PrefetchScalarGridSpec(
            num_scalar_prefetch=1, grid=(N_CHIPS, nb),
            in_specs=[pl.BlockSpec((1, tr, cols), lambda s, r, c_ref: (s // side, c_ref[0] * nb + r, s % side)),
                      pl.BlockSpec((1, tr, cols), lambda s, r, c_ref: (s // side, r, s % side))],
            out_specs=pl.BlockSpec((1, tr, cols), lambda s, r, c_ref: (s, r, 0))),
        out_shape=jax.ShapeDtypeStruct((N_CHIPS, half, cols), grad.dtype),
        compiler_params=_params(dimension_semantics=("arbitrary", "arbitrary")),
    )(c_idx, grad, got)


def _scatter_partials(parts, layer, collective_id):
    n = len(parts)

    def body(*refs):
        ps, gots = refs[:n], refs[n:2 * n]
        send_sems, recv_sems = refs[2 * n:]
        x, y, c = _place()
        chips = _other_chips(x, y)
        barrier = pltpu.get_barrier_semaphore()
        for chip in chips:
            pl.semaphore_signal(barrier, inc=1, device_id=(*chip, c), device_id_type=MESH)
        pl.semaphore_wait(barrier, N_CHIPS - 1)
        copies = [pltpu.make_async_remote_copy(src_ref=ps[w].at[2 * cx + cy], dst_ref=gots[w].at[j],
                                               send_sem=send_sems.at[3 * w + j], recv_sem=recv_sems.at[3 * w + j],
                                               device_id=(cx, cy, c), device_id_type=MESH)
                  for j, (cx, cy) in enumerate(chips) for w in range(n)]
        for cp in copies:
            cp.start()
        for cp in copies:
            cp.wait()

    return pl.kernel(
        body, name=f"scatter_partials{layer}_id{collective_id}",
        out_type=[jax.ShapeDtypeStruct((N_CHIPS - 1,) + p.shape[1:], p.dtype) for p in parts],
        mesh=plsc.ScalarSubcoreMesh(axis_name="sequencer", num_cores=1),
        scratch_types=[pltpu.SemaphoreType.DMA((3 * n,)), pltpu.SemaphoreType.DMA((3 * n,))],
        compiler_params=pltpu.CompilerParams(collective_id=collective_id),
    )(*parts)


def _sum_partials(part, got, s_idx, c_idx, layer, depth, stacked=None):
    n, half, cols = part.shape
    tr = _tile_rows(half, cols, 2, cap=1024 * 1024)
    nb = half // tr

    def body(s_ref, c_ref, a_ref, b_ref, *rest):
        o_ref = rest[-1]
        acc = a_ref[0].astype(F32)
        for j in range(n - 1):
            acc = acc + b_ref[j].astype(F32)
        o_ref[...] = acc

    in_specs = [pl.BlockSpec((1, tr, cols), lambda r, s_ref, c_ref: (s_ref[0], r, 0)),
                pl.BlockSpec((n - 1, tr, cols), lambda r, s_ref, c_ref: (0, r, 0))]
    operands = [s_idx, c_idx, part, got]
    aliases = {}
    if stacked is not None:
        in_specs.append(pl.BlockSpec(memory_space=pl.ANY))
        operands.append(stacked)
        aliases = {len(operands) - 1: 0}
    return pl.pallas_call(
        body, name="sum_partials",
        grid_spec=pltpu.PrefetchScalarGridSpec(
            num_scalar_prefetch=2, grid=(nb,), in_specs=in_specs,
            out_specs=pl.BlockSpec((None, tr, cols), lambda r, s_ref, c_ref: (layer, c_ref[0] * nb + r, 0))),
        out_shape=jax.ShapeDtypeStruct((depth, 2 * half, cols), F32),
        input_output_aliases=aliases,
        compiler_params=_params(dimension_semantics=("arbitrary",)),
    )(*operands)


def _join_halves(bufs, layer):
    n = len(bufs)

    def body(*refs):
        outs = refs[n:2 * n]
        send_sems, recv_sems = refs[2 * n:]
        x, y, c = _place()
        copies = []
        for w in range(n):
            half = bufs[w].shape[1] // 2
            mine = outs[w].at[layer, pl.ds(c * half, half)]
            copies.append(pltpu.make_async_remote_copy(src_ref=mine, dst_ref=mine, send_sem=send_sems.at[w],
                                                       recv_sem=recv_sems.at[w], device_id=(x, y, 1 - c),
                                                       device_id_type=MESH))
        for cp in copies:
            cp.start()
        for cp in copies:
            cp.wait()

    hbm = pl.BlockSpec(memory_space=pl.ANY)
    return pl.pallas_call(
        body, name="join_halves", in_specs=[hbm] * n, out_specs=[hbm] * n,
        out_shape=[jax.ShapeDtypeStruct(b.shape, b.dtype) for b in bufs],
        input_output_aliases={w: w for w in range(n)},
        scratch_shapes=[pltpu.SemaphoreType.DMA((n,)), pltpu.SemaphoreType.DMA((n,))],
    )(*bufs)


def _allreduce_small(v, name):
    rows = v.shape[0]

    def body(v_ref, o_ref, gath, send_sems, recv_sems):
        x, y, c = _place()
        idx = 4 * x + 2 * y + c
        gath[0] = v_ref[...]
        copies = []
        for r in range(1, N_DEV):
            peer = (1 - x if r & 4 else x, 1 - y if r & 2 else y, 1 - c if r & 1 else c)
            cp = pltpu.make_async_remote_copy(src_ref=v_ref, dst_ref=gath.at[r], send_sem=send_sems.at[r - 1],
                                              recv_sem=recv_sems.at[r - 1], device_id=peer, device_id_type=MESH)
            cp.start()
            copies.append(cp)
        for cp in copies:
            cp.wait()
        acc = gath[idx]
        for a in range(1, N_DEV):
            acc = acc + gath[lax.bitwise_xor(idx, a)]
        o_ref[...] = acc

    vmem = pl.BlockSpec(memory_space=pltpu.VMEM)
    return pl.pallas_call(
        body, name=name, in_specs=[vmem], out_specs=vmem,
        out_shape=jax.ShapeDtypeStruct((rows, LANES), F32),
        scratch_shapes=[pltpu.VMEM((N_DEV, rows, LANES), F32), pltpu.SemaphoreType.DMA((N_DEV - 1,)),
                        pltpu.SemaphoreType.DMA((N_DEV - 1,))],
    )(v)


def _join_shards(name, gathered):
    if name in COL_SHARDED:
        return jnp.concatenate([gathered[s] for s in range(N_CHIPS)], axis=1)[None]
    return gathered.reshape(1, N_CHIPS * gathered.shape[1], gathered.shape[2])


def _row_shards(g):
    return g.reshape(N_CHIPS, g.shape[0] // N_CHIPS, g.shape[1])


def _mixer_runs(width, cut, n_small):
    runs = []
    for s in range(N_CHIPS):
        lo, hi = s * width, (s + 1) * width
        spans = ((True, lo, min(hi, cut)), (False, max(lo, cut), min(hi, cut + n_small)),
                 (True, max(lo, cut + n_small), hi))
        runs.append([(is_main, a - lo, b - lo) for is_main, a, b in spans if a < b])
    return runs


def _split_mixer_weight(gathered, runs):
    main = [gathered[s][:, a:b] for s, parts in enumerate(runs) for is_main, a, b in parts if is_main]
    small = [gathered[s][:, a:b] for s, parts in enumerate(runs) for is_main, a, b in parts if not is_main]
    small = small[0] if len(small) == 1 else jnp.concatenate(small, axis=1)
    return jnp.concatenate(main, axis=1)[None], jnp.pad(small, ((0, 0), (0, LANES - small.shape[1])))[None]


def _join_mixer_grad(d_main, d_small, runs):
    shards, m, k = [], 0, 0
    for parts in runs:
        cols = []
        for is_main, a, b in parts:
            if is_main:
                cols.append(d_main[:, m:m + b - a])
                m += b - a
            else:
                cols.append(d_small[:, k:k + b - a].astype(d_main.dtype))
                k += b - a
        shards.append(cols[0] if len(cols) == 1 else jnp.concatenate(cols, axis=1))
    return jnp.stack(shards)


def _pad_small(flat):
    n = flat.shape[0]
    block = 8 * LANES
    padded = -(-n // block) * block
    return jnp.pad(flat, (0, padded - n)).reshape(padded // LANES, LANES)


def kernel(x, ffn1_norm, ffn1_w_in, ffn1_w_out, mix_norm, w_in, dn_conv_w, dn_a_log, dn_dt_bias, dn_out_norm, sb_q_norm, sb_k_norm, w_branch_a, w_branch_b, w_out, ffn2_norm, ffn2_w_in, ffn2_w_out, loss_target, m_ffn1_norm, m_ffn1_w_in, m_ffn1_w_out, m_mix_norm, m_w_in, m_dn_conv_w, m_dn_a_log, m_dn_dt_bias, m_dn_out_norm, m_sb_q_norm, m_sb_k_norm, m_w_branch_a, m_w_branch_b, m_w_out, m_ffn2_norm, m_ffn2_w_in, m_ffn2_w_out, v_ffn1_norm, v_ffn1_w_in, v_ffn1_w_out, v_mix_norm, v_w_in, v_dn_conv_w, v_dn_a_log, v_dn_dt_bias, v_dn_out_norm, v_sb_q_norm, v_sb_k_norm, v_w_branch_a, v_w_branch_b, v_w_out, v_ffn2_norm, v_ffn2_w_in, v_ffn2_w_out):
    w = dict(ffn1_norm=ffn1_norm, ffn1_w_in=ffn1_w_in, ffn1_w_out=ffn1_w_out, mix_norm=mix_norm, w_in=w_in,
             dn_conv_w=dn_conv_w, dn_a_log=dn_a_log, dn_dt_bias=dn_dt_bias, dn_out_norm=dn_out_norm,
             sb_q_norm=sb_q_norm, sb_k_norm=sb_k_norm, w_branch_a=w_branch_a, w_branch_b=w_branch_b, w_out=w_out,
             ffn2_norm=ffn2_norm, ffn2_w_in=ffn2_w_in, ffn2_w_out=ffn2_w_out)
    mom = dict(ffn1_norm=m_ffn1_norm, ffn1_w_in=m_ffn1_w_in, ffn1_w_out=m_ffn1_w_out, mix_norm=m_mix_norm, w_in=m_w_in,
               dn_conv_w=m_dn_conv_w, dn_a_log=m_dn_a_log, dn_dt_bias=m_dn_dt_bias, dn_out_norm=m_dn_out_norm,
               sb_q_norm=m_sb_q_norm, sb_k_norm=m_sb_k_norm, w_branch_a=m_w_branch_a, w_branch_b=m_w_branch_b,
               w_out=m_w_out, ffn2_norm=m_ffn2_norm, ffn2_w_in=m_ffn2_w_in, ffn2_w_out=m_ffn2_w_out)
    var = dict(ffn1_norm=v_ffn1_norm, ffn1_w_in=v_ffn1_w_in, ffn1_w_out=v_ffn1_w_out, mix_norm=v_mix_norm, w_in=v_w_in,
               dn_conv_w=v_dn_conv_w, dn_a_log=v_dn_a_log, dn_dt_bias=v_dn_dt_bias, dn_out_norm=v_dn_out_norm,
               sb_q_norm=v_sb_q_norm, sb_k_norm=v_sb_k_norm, w_branch_a=v_w_branch_a, w_branch_b=v_w_branch_b,
               w_out=v_w_out, ffn2_norm=v_ffn2_norm, ffn2_w_in=v_ffn2_w_in, ffn2_w_out=v_ffn2_w_out)

    _, t, d = x.shape
    depth = ffn1_norm.shape[0]
    n_heads = d // HEAD_DIM
    conv_cols = dn_conv_w.shape[2]
    assert d % HEAD_DIM == 0 and t % SB_BLOCK == 0 and 2 * n_heads <= LANES and depth % 2 == 0
    assert w_in.shape[2] * N_CHIPS == 9 * d + 2 * n_heads and conv_cols * N_CHIPS == 3 * d

    x_idx, y_idx, c_idx = _place()
    shard = 2 * x_idx + y_idx
    c_arr = jnp.reshape(c_idx, (1,)).astype(jnp.int32)
    s_arr = jnp.reshape(shard, (1,)).astype(jnp.int32)

    mine = {n: w[n].astype(BF16) for n in BIG}
    runs = _mixer_runs(w_in.shape[2], 4 * d, 2 * n_heads)
    first = ("ffn1_w_in", "ffn1_w_out")
    rest = tuple(n for n in BIG if n not in first)

    def gather(names, l, collective_id):
        return dict(zip(names, _allgather_layer([mine[n] for n in names], l, collective_id)))

    arriving = [{**gather(first, 0, 0), **gather(rest, 0, depth)}] + [gather(BIG, l, l) for l in range(1, depth)]

    def layer_weights(l, names, after):
        gathered, after = lax.optimization_barrier(({n: arriving[l][n] for n in names}, after))
        full = {}
        for n in names:
            g = lax.dynamic_update_slice(gathered[n], mine[n][l][None], (shard, 0, 0))
            if n == "w_in":
                full["w_main"], full["w_ba"] = _split_mixer_weight(g, runs)
            else:
                full[n] = _join_shards(n, g)
        return full, after

    conv_place = lax.dynamic_update_slice(jnp.zeros((depth, DN_CONV, 3 * d), F32), dn_conv_w, (0, 0, shard * conv_cols))
    conv_rows = _pad_small(conv_place.reshape(-1))
    conv_full = (0.5 * _allreduce_small(conv_rows, "allgather_conv")).reshape(-1)[:depth * DN_CONV * 3 * d]
    conv_full = jnp.pad(conv_full.reshape(depth, DN_CONV, 3 * d), ((0, 0), (0, CONV_ROWS - DN_CONV), (0, 0)))

    def head_row(vals):
        return jnp.pad(vals, (n_heads, LANES - 2 * n_heads)).reshape(1, LANES)

    saved, layers = [], []
    cur = x[0]
    for l in range(depth):
        full, x0 = layer_weights(l, first, cur)
        x1 = _ffn_fwd(x0, ffn1_norm[l][None], full["ffn1_w_in"], full["ffn1_w_out"], 0)
        later, x1 = layer_weights(l, rest, x1)
        full.update(later)
        layers.append(full)
        proj, ba = _proj_fwd(x1, mix_norm[l][None], full["w_main"], full["w_ba"], 0)
        act = _dn_prep_fwd(proj, conv_full[l], n_heads)
        alog, dtb = head_row(dn_a_log[l]), head_row(dn_dt_bias[l])
        oa, snaps = _delta_fwd(act, ba, alog, dtb, n_heads)
        qb, kb = _sb_prep_fwd(proj, sb_q_norm[l][None], sb_k_norm[l][None], n_heads)
        ob, ltot = _sb_attn_fwd(qb, kb, proj, n_heads)
        x2 = _merge_fwd(x1, oa, proj, ob, dn_out_norm[l][None], full["w_branch_a"], full["w_branch_b"],
                        full["w_out"], 0, n_heads)
        cur = _ffn_fwd(x2, ffn2_norm[l][None], full["ffn2_w_in"], full["ffn2_w_out"], 0)
        saved.append((x0, x1, proj, ba, act, alog, dtb, oa, snaps, qb, kb, ob, ltot, x2))

    dcur, loss_part = _loss_head(cur, loss_target[0])

    grads = {n: [None] * depth for n in WEIGHTS}
    reduced = {n: None for n in BIG}
    in_flight = []

    def start_reduce(l, names, collective_id):
        g_major = [grads[n][l] for n in names]
        parts = [_add_half(g, got, c_arr) for g, got in zip(g_major, _swap_halves(g_major))]
        return l, names, parts, _scatter_partials(parts, l, collective_id)

    def finish_reduce(started, after):
        for l, names, parts, arrived in started:
            arrived, after = lax.optimization_barrier((arrived, after))
            halves = [_sum_partials(p, got, s_arr, c_arr, l, depth, stacked=reduced[n])
                      for n, p, got in zip(names, parts, arrived)]
            reduced.update(zip(names, _join_halves(halves, l)))
        return after

    for l in reversed(range(depth)):
        x0, x1, proj, ba, act, alog, dtb, oa, snaps, qb, kb, ob, ltot, x2 = saved[l]
        full = layers[l]
        dx2, dg, dwi, dwo = _ffn_bwd(x2, ffn2_norm[l][None], dcur, full["ffn2_w_in"], full["ffn2_w_out"], 0)
        grads["ffn2_norm"][l] = dg[0]
        grads["ffn2_w_in"][l] = dwi
        grads["ffn2_w_out"][l] = _row_shards(dwo)
        doa, dz, dob, dga, dgb, dgn, dwa, dwb, dwout = _merge_bwd(
            oa, proj, ob, dx2, dn_out_norm[l][None], full["w_branch_a"], full["w_branch_b"], full["w_out"], 0,
            n_heads)
        grads["dn_out_norm"][l] = dgn[0]
        grads["w_branch_a"][l], grads["w_branch_b"][l] = _row_shards(dwa), _row_shards(dwb)
        grads["w_out"][l] = _row_shards(dwout)
        dqb, dkb, dvb = _sb_attn_bwd(qb, kb, proj, ltot, dob, n_heads)
        dsq, dsk, dqn, dkn = _sb_prep_bwd(proj, sb_q_norm[l][None], sb_k_norm[l][None], dqb, dkb, n_heads)
        grads["sb_q_norm"][l], grads["sb_k_norm"][l] = dqn[0], dkn[0]
        dact, dba, dal, ddt = _delta_bwd(act, ba, alog, dtb, snaps, doa, n_heads)
        grads["dn_a_log"][l] = dal[0, n_heads:2 * n_heads]
        grads["dn_dt_bias"][l] = ddt[0, n_heads:2 * n_heads]
        dqkv, dconv = _dn_prep_bwd(proj, conv_full[l], dact, n_heads)
        grads["dn_conv_w"][l] = dconv[:DN_CONV]
        dproj = jnp.concatenate([dqkv, dz, dsq, dsk, dvb.astype(BF16), dga, dgb], axis=1)
        dx1, dg, dwm, dwba = _proj_bwd(x1, mix_norm[l][None], dx2, dproj, dba, full["w_main"], full["w_ba"], 0)
        grads["mix_norm"][l] = dg[0]
        grads["w_in"][l] = _join_mixer_grad(dwm, dwba, runs)
        early = start_reduce(l, rest, collective_id=depth + 1 + l)
        dcur, dg, dwi, dwo = _ffn_bwd(x0, ffn1_norm[l][None], dx1, full["ffn1_w_in"], full["ffn1_w_out"], 0)
        grads["ffn1_norm"][l] = dg[0]
        grads["ffn1_w_in"][l] = dwi
        grads["ffn1_w_out"][l] = _row_shards(dwo)

        dcur = finish_reduce(in_flight, dcur)
        in_flight = [early, start_reduce(l, first, collective_id=2 * depth + 1 + l)]
    dcur = finish_reduce(in_flight, dcur)
    final = reduced
    grads = {n: jnp.stack(grads[n]) for n in SMALL + ("dn_conv_w",)}

    small_names = SMALL + ("dn_conv_w",)
    small_sizes = [int(np.prod(grads[n].shape)) for n in small_names]
    small_off = np.concatenate([[0], np.cumsum(small_sizes)])
    small = jnp.concatenate([grads[n].reshape(-1) for n in small_names] + [loss_part[0, :1]])
    small_sum = _allreduce_small(_pad_small(small), "allreduce_small").reshape(-1)
    for i, n in enumerate(small_names):
        final[n] = small_sum[small_off[i]:small_off[i + 1]].reshape(grads[n].shape)
    final["dn_conv_w"] = lax.dynamic_slice(final["dn_conv_w"], (0, 0, shard * conv_cols), (depth, DN_CONV, conv_cols))
    loss = small_sum[small_off[-1]]

    deltas, new_m, new_v = {}, {}, {}
    for n in WEIGHTS:
        shape = w[n].shape
        flat = (-1, shape[-1])
        dl, m2, v2 = _adamw(w[n].reshape(flat), final[n].reshape(flat), mom[n].reshape(flat), var[n].reshape(flat))
        deltas[n], new_m[n], new_v[n] = dl.reshape(shape), m2.reshape(shape), v2.reshape(shape)

    grad_x = dcur[None]
    return (loss, grad_x, *[final[n] for n in WEIGHTS], *[deltas[n] for n in WEIGHTS],
            *[new_m[n] for n in WEIGHTS], *[new_v[n] for n in WEIGHTS])
```

```python
import functools

import jax
import jax.numpy as jnp
import numpy as np
from jax import lax
from jax.experimental import pallas as pl
from jax.experimental.pallas import tpu as pltpu
from jax.experimental.pallas import tpu_sc as plsc

F32 = jnp.float32
BF16 = jnp.bfloat16

LANES = 128
HEAD_DIM = 128
DN_CHUNK = 64
DN_CONV = 4
CONV_ROWS = 8
SB_BLOCK = 128
FFN_HALF = 0.5
RMS_EPS = 1e-6
L2_EPS = 1e-6
NEG_BIG = -1e30
ADAM_LR = 0.001
ADAM_B1 = 0.9
ADAM_B2 = 0.999
ADAM_EPS = 1e-08
ADAM_WD = 0.01
ADAM_STEP = 10
VMEM_LIMIT = 56 * 1024 * 1024
N_CHIPS = 4
N_DEV = 8
MESH = pl.DeviceIdType.MESH

BIG = ("ffn1_w_in", "ffn1_w_out", "w_in", "w_branch_a", "w_branch_b", "w_out", "ffn2_w_in", "ffn2_w_out")
COL_SHARDED = ("ffn1_w_in", "w_in", "ffn2_w_in")
SMALL = ("ffn1_norm", "mix_norm", "dn_a_log", "dn_dt_bias", "dn_out_norm", "sb_q_norm", "sb_k_norm", "ffn2_norm")
WEIGHTS = ("ffn1_norm", "ffn1_w_in", "ffn1_w_out", "mix_norm", "w_in", "dn_conv_w", "dn_a_log", "dn_dt_bias",
           "dn_out_norm", "sb_q_norm", "sb_k_norm", "w_branch_a", "w_branch_b", "w_out", "ffn2_norm", "ffn2_w_in",
           "ffn2_w_out")


def _params(**kw):
    return pltpu.CompilerParams(vmem_limit_bytes=VMEM_LIMIT, **kw)


def _pick(n, options):
    for o in options:
        if n % o == 0:
            return o
    return n


def _const_spec(shape, single=False):
    nd = len(shape)
    if single:
        return pl.BlockSpec(shape, lambda *_: (0,) * nd, pipeline_mode=pl.Buffered(1))
    return pl.BlockSpec(shape, lambda *_: (0,) * nd)


_NN = ((1,), (0,))
_NT = ((1,), (1,))
_TN = ((0,), (0,))


def _dot(a, b, dims):
    return lax.dot_general(a.astype(BF16), b.astype(BF16), (dims, ((), ())), preferred_element_type=F32)


def _mm_nn(a, b):
    return _dot(a, b, _NN)


def _mm_nt(a, b):
    return _dot(a, b, _NT)


def _mm_tn(a, b):
    return _dot(a, b, _TN)


def _split(a):
    hi = a.astype(BF16)
    lo = (a - hi.astype(F32)).astype(BF16)
    return hi, lo


def _dot_precise(a, b, dims):
    dn = (dims, ((), ()))
    ah, al = _split(a)
    bh, bl = _split(b)
    out = lax.dot_general(ah, bh, dn, preferred_element_type=F32)
    out = out + lax.dot_general(ah, bl, dn, preferred_element_type=F32)
    return out + lax.dot_general(al, bh, dn, preferred_element_type=F32)


def _make_diff_mm(dot):
    @jax.custom_vjp
    def nn(a, b):
        return dot(a, b, _NN)

    @jax.custom_vjp
    def nt(a, b):
        return dot(a, b, _NT)

    @jax.custom_vjp
    def tn(a, b):
        return dot(a, b, _TN)

    nn.defvjp(lambda a, b: (dot(a, b, _NN), (a, b)), lambda r, g: (nt(g, r[1]), tn(r[0], g)))
    nt.defvjp(lambda a, b: (dot(a, b, _NT), (a, b)), lambda r, g: (nn(g, r[1]), tn(g, r[0])))
    tn.defvjp(lambda a, b: (dot(a, b, _TN), (a, b)), lambda r, g: (nt(r[1], g), nn(r[0], g)))
    return nn, nt, tn


_d_nn, _d_nt, _d_tn = _make_diff_mm(_dot)
_p_nn, _p_nt, _p_tn = _make_diff_mm(_dot_precise)


def _softplus_raw(x):
    return jnp.maximum(x, 0.0) + jnp.log(1.0 + jnp.exp(-jnp.abs(x)))


@jax.custom_vjp
def _softplus(x):
    return _softplus_raw(x)


_softplus.defvjp(lambda x: (_softplus_raw(x), x), lambda x, g: (g * jax.nn.sigmoid(x),))


def _rms(x, gain, eps):
    return x * lax.rsqrt(jnp.mean(x * x, axis=-1, keepdims=True) + eps) * gain


def _silu(x):
    return x * jax.nn.sigmoid(x)


def _shift_rows_raw(x, k, down):
    n = x.shape[0]
    row = lax.broadcasted_iota(jnp.int32, x.shape, 0)
    if down:
        return jnp.where(row >= k, pltpu.roll(x, k, 0), 0.0)
    return jnp.where(row < n - k, pltpu.roll(x, n - k, 0), 0.0)


@functools.partial(jax.custom_vjp, nondiff_argnums=(1,))
def _shift_down(x, k):
    return _shift_rows_raw(x, k, True)


_shift_down.defvjp(lambda x, k: (_shift_rows_raw(x, k, True), None),
                   lambda k, _, g: (_shift_rows_raw(g, k, False),))


def _layer_spec(layer, block, index_map, single=False):
    full_map = lambda *a: (layer,) + tuple(index_map(*a))
    if single:
        return pl.BlockSpec((None,) + block, full_map, pipeline_mode=pl.Buffered(1))
    return pl.BlockSpec((None,) + block, full_map)


def _ffn_fwd(x, gain, w_in, w_out, layer):
    t, d = x.shape
    f = w_out.shape[1]
    fc = _pick(f, (256, 128))
    nj = f // fc
    rt = _pick(t, (512, 256, 128))

    def body(x_ref, g_ref, wg_ref, wu_ref, wo_ref, o_ref, hs_ref):
        @pl.when(pl.program_id(0) == 0)
        def _():
            for r in range(t // rt):
                rows = pl.ds(r * rt, rt)
                xr = x_ref[rows, :]
                hs_ref[rows, :] = _rms(xr, g_ref[...], RMS_EPS).astype(BF16)
                o_ref[rows, :] = xr

        for r in range(t // rt):
            rows = pl.ds(r * rt, rt)
            h = hs_ref[rows, :]
            a = _mm_nn(h, wg_ref[...])
            b = _mm_nn(h, wu_ref[...])
            o_ref[rows, :] += FFN_HALF * _mm_nn(_silu(a) * b, wo_ref[...])

    return pl.pallas_call(
        body, name="ffn_fwd", grid=(nj,),
        in_specs=[_const_spec((t, d), True), _const_spec((1, d)),
                  _layer_spec(layer, (d, fc), lambda j: (0, j)), _layer_spec(layer, (d, fc), lambda j: (0, nj + j)),
                  _layer_spec(layer, (fc, d), lambda j: (j, 0))],
        out_specs=_const_spec((t, d)),
        out_shape=jax.ShapeDtypeStruct((t, d), F32),
        scratch_shapes=[pltpu.VMEM((t, d), BF16)],
        compiler_params=_params(dimension_semantics=("arbitrary",)),
    )(x, gain, w_in, w_in, w_out)


def _ffn_bwd(x, gain, dy, w_in, w_out, layer):
    t, d = x.shape
    f = w_out.shape[1]
    fc = _pick(f, (256, 128))
    nj = f // fc
    rt = _pick(t, (512, 256, 128))
    nr = t // rt

    def body(x_ref, g_ref, dy_ref, wg_ref, wu_ref, wo_ref, dx_ref, dg_ref, dwi_ref, dwo_ref,
             hs_ref, dwg_acc, dwu_acc, dwo_acc):
        j = pl.program_id(0)

        @pl.when(j == 0)
        def _():
            for r in range(nr):
                rows = pl.ds(r * rt, rt)
                hs_ref[rows, :] = _rms(x_ref[rows, :], g_ref[...], RMS_EPS).astype(BF16)
                dx_ref[rows, :] = jnp.zeros((rt, d), F32)

        for r in range(nr):
            rows = pl.ds(r * rt, rt)
            h = hs_ref[rows, :]
            dy2 = (FFN_HALF * dy_ref[rows, :]).astype(BF16)
            a = _mm_nn(h, wg_ref[...])
            b = _mm_nn(h, wu_ref[...])
            sig = jax.nn.sigmoid(a)
            sa = a * sig
            ds = _mm_nt(dy2, wo_ref[...])
            da = ds * b * (sig * (1.0 + a * (1.0 - sig)))
            db = ds * sa
            dx_ref[rows, :] += _mm_nt(da, wg_ref[...]) + _mm_nt(db, wu_ref[...])
            dwo_c = _mm_tn(sa * b, dy2)
            dwg_c = _mm_tn(h, da)
            dwu_c = _mm_tn(h, db)
            if r == 0:
                dwo_acc[...] = dwo_c
                dwg_acc[...] = dwg_c
                dwu_acc[...] = dwu_c
            else:
                dwo_acc[...] += dwo_c
                dwg_acc[...] += dwg_c
                dwu_acc[...] += dwu_c
        dwi_ref[0] = dwg_acc[...].astype(BF16)
        dwi_ref[1] = dwu_acc[...].astype(BF16)
        dwo_ref[...] = dwo_acc[...].astype(BF16)

        @pl.when(j == nj - 1)
        def _():
            for r in range(nr):
                rows = pl.ds(r * rt, rt)
                _, vjp = jax.vjp(lambda xx, gg: _rms(xx, gg, RMS_EPS), x_ref[rows, :], g_ref[...])
                dxn, dgr = vjp(dx_ref[rows, :])
                dx_ref[rows, :] = dy_ref[rows, :] + dxn
                if r == 0:
                    dg_ref[...] = dgr
                else:
                    dg_ref[...] += dgr

    return pl.pallas_call(
        body, name="ffn_bwd", grid=(nj,),
        in_specs=[_const_spec((t, d), True), _const_spec((1, d)), _const_spec((t, d), True),
                  _layer_spec(layer, (d, fc), lambda j: (0, j)), _layer_spec(layer, (d, fc), lambda j: (0, nj + j)),
                  _layer_spec(layer, (fc, d), lambda j: (j, 0))],
        out_specs=[_const_spec((t, d)), _const_spec((1, d)),
                   pl.BlockSpec((2, d, fc), lambda j: (0, 0, j)), pl.BlockSpec((fc, d), lambda j: (j, 0))],
        out_shape=[jax.ShapeDtypeStruct((t, d), F32), jax.ShapeDtypeStruct((1, d), F32),
                   jax.ShapeDtypeStruct((2, d, f), BF16), jax.ShapeDtypeStruct((f, d), BF16)],
        scratch_shapes=[pltpu.VMEM((t, d), BF16), pltpu.VMEM((d, fc), F32), pltpu.VMEM((d, fc), F32),
                        pltpu.VMEM((fc, d), F32)],
        compiler_params=_params(dimension_semantics=("arbitrary",)),
    )(x, gain, dy, w_in, w_in, w_out)


def _proj_fwd(x, gain, w, wba, layer):
    t, d = x.shape
    n = w.shape[2]
    nc = _pick(n, (512, 256, 128))
    rt = _pick(t, (512, 256, 128))

    def body(x_ref, g_ref, w_ref, wba_ref, p_ref, ba_ref, hs_ref):
        @pl.when(pl.program_id(0) == 0)
        def _():
            for r in range(t // rt):
                rows = pl.ds(r * rt, rt)
                h = _rms(x_ref[rows, :], g_ref[...], RMS_EPS).astype(BF16)
                hs_ref[rows, :] = h
                ba_ref[rows, :] = _mm_nn(h, wba_ref[...])

        for r in range(t // rt):
            rows = pl.ds(r * rt, rt)
            p_ref[rows, :] = _mm_nn(hs_ref[rows, :], w_ref[...])

    return pl.pallas_call(
        body, name="proj_fwd", grid=(n // nc,),
        in_specs=[_const_spec((t, d), True), _const_spec((1, d)),
                  _layer_spec(layer, (d, nc), lambda j: (0, j)), _layer_spec(layer, (d, LANES), lambda j: (0, 0))],
        out_specs=[pl.BlockSpec((t, nc), lambda j: (0, j)), _const_spec((t, LANES))],
        out_shape=[jax.ShapeDtypeStruct((t, n), F32), jax.ShapeDtypeStruct((t, LANES), F32)],
        scratch_shapes=[pltpu.VMEM((t, d), BF16)],
        compiler_params=_params(dimension_semantics=("arbitrary",)),
    )(x, gain, w, wba)


def _proj_bwd(x, gain, dres, dp, dba, w, wba, layer):
    t, d = x.shape
    n = w.shape[2]
    nc = _pick(n, (512, 256, 128))
    nj = n // nc
    rt = _pick(t, (512, 256, 128))
    nr = t // rt

    def body(x_ref, g_ref, dres_ref, dp_ref, dba_ref, w_ref, wba_ref, dx_ref, dg_ref, dw_ref, dwba_ref, hs_ref, dw_acc):
        j = pl.program_id(0)

        @pl.when(j == 0)
        def _():
            for r in range(nr):
                rows = pl.ds(r * rt, rt)
                h = _rms(x_ref[rows, :], g_ref[...], RMS_EPS).astype(BF16)
                hs_ref[rows, :] = h
                g = dba_ref[rows, :]
                dx_ref[rows, :] = _mm_nt(g, wba_ref[...])
                if r == 0:
                    dwba_ref[...] = _mm_tn(h, g)
                else:
                    dwba_ref[...] += _mm_tn(h, g)

        for r in range(nr):
            rows = pl.ds(r * rt, rt)
            g = dp_ref[rows, :]
            dx_ref[rows, :] += _mm_nt(g, w_ref[...])
            if r == 0:
                dw_acc[...] = _mm_tn(hs_ref[rows, :], g)
            else:
                dw_acc[...] += _mm_tn(hs_ref[rows, :], g)
        dw_ref[...] = dw_acc[...].astype(BF16)

        @pl.when(j == nj - 1)
        def _():
            for r in range(nr):
                rows = pl.ds(r * rt, rt)
                _, vjp = jax.vjp(lambda xx, gg: _rms(xx, gg, RMS_EPS), x_ref[rows, :], g_ref[...])
                dxn, dgr = vjp(dx_ref[rows, :])
                dx_ref[rows, :] = dres_ref[rows, :] + dxn
                if r == 0:
                    dg_ref[...] = dgr
                else:
                    dg_ref[...] += dgr

    return pl.pallas_call(
        body, name="proj_bwd", grid=(nj,),
        in_specs=[_const_spec((t, d), True), _const_spec((1, d)), _const_spec((t, d), True),
                  pl.BlockSpec((t, nc), lambda j: (0, j)), _const_spec((t, LANES)),
                  _layer_spec(layer, (d, nc), lambda j: (0, j)), _layer_spec(layer, (d, LANES), lambda j: (0, 0))],
        out_specs=[_const_spec((t, d)), _const_spec((1, d)),
                   pl.BlockSpec((d, nc), lambda j: (0, j)), _const_spec((d, LANES))],
        out_shape=[jax.ShapeDtypeStruct((t, d), F32), jax.ShapeDtypeStruct((1, d), F32),
                   jax.ShapeDtypeStruct((d, n), BF16), jax.ShapeDtypeStruct((d, LANES), F32)],
        scratch_shapes=[pltpu.VMEM((t, d), BF16), pltpu.VMEM((d, nc), F32)],
        compiler_params=_params(dimension_semantics=("arbitrary",)),
    )(x, gain, dres, dp, dba, w, wba)


def _conv_act(x, w0, w1, w2, w3, is_qk):
    y = w3 * x + w2 * _shift_down(x, 1) + w1 * _shift_down(x, 2) + w0 * _shift_down(x, 3)
    y = _silu(y)
    inv = lax.rsqrt(jnp.sum(y * y, axis=-1, keepdims=True) + L2_EPS)
    return y * (is_qk * inv + (1.0 - is_qk))


def _taps(w_ref):
    return tuple(w_ref[i:i + 1, :] for i in range(DN_CONV))


def _dn_prep_fwd(proj, conv_w, n_heads):
    t = proj.shape[0]
    nb = 3 * n_heads

    def body(x_ref, w_ref, o_ref):
        is_qk = jnp.where(pl.program_id(0) < 2 * n_heads, 1.0, 0.0).astype(F32)
        o_ref[...] = _conv_act(x_ref[...], *_taps(w_ref), is_qk)

    return pl.pallas_call(
        body, name="dn_prep_fwd", grid=(nb,),
        in_specs=[pl.BlockSpec((t, HEAD_DIM), lambda i: (0, i)), pl.BlockSpec((CONV_ROWS, HEAD_DIM), lambda i: (0, i))],
        out_specs=pl.BlockSpec((t, HEAD_DIM), lambda i: (0, i)),
        out_shape=jax.ShapeDtypeStruct((t, nb * HEAD_DIM), F32),
        compiler_params=_params(dimension_semantics=("arbitrary",)),
    )(proj, conv_w)


def _dn_prep_bwd(proj, conv_w, dact, n_heads):
    t = proj.shape[0]
    nb = 3 * n_heads

    def body(x_ref, w_ref, g_ref, dx_ref, dw_ref):
        is_qk = jnp.where(pl.program_id(0) < 2 * n_heads, 1.0, 0.0).astype(F32)
        _, vjp = jax.vjp(lambda x, a, b, c, e: _conv_act(x, a, b, c, e, is_qk), x_ref[...], *_taps(w_ref))
        dx, d0, d1, d2, d3 = vjp(g_ref[...])
        dx_ref[...] = dx.astype(BF16)
        dw_ref[...] = jnp.concatenate([d0, d1, d2, d3, jnp.zeros((CONV_ROWS - DN_CONV, HEAD_DIM), F32)], axis=0)

    return pl.pallas_call(
        body, name="dn_prep_bwd", grid=(nb,),
        in_specs=[pl.BlockSpec((t, HEAD_DIM), lambda i: (0, i)), pl.BlockSpec((CONV_ROWS, HEAD_DIM), lambda i: (0, i)),
                  pl.BlockSpec((t, HEAD_DIM), lambda i: (0, i))],
        out_specs=[pl.BlockSpec((t, HEAD_DIM), lambda i: (0, i)), pl.BlockSpec((CONV_ROWS, HEAD_DIM), lambda i: (0, i))],
        out_shape=[jax.ShapeDtypeStruct((t, nb * HEAD_DIM), BF16), jax.ShapeDtypeStruct((CONV_ROWS, nb * HEAD_DIM), F32)],
        compiler_params=_params(dimension_semantics=("arbitrary",)),
    )(proj, conv_w, dact)


def _unit_lower_inverses(lmats, c):
    r = lax.broadcasted_iota(jnp.int32, (c, c), 0)
    q = lax.broadcasted_iota(jnp.int32, (c, c), 1)
    eye = jnp.where(r == q, 1.0, 0.0)
    ps = [eye - l for l in lmats]
    ms = [_p_nn(l, l) for l in lmats]
    n = 2
    while True:
        ps = [p + _p_nn(p, m) for p, m in zip(ps, ms)]
        if 2 * n >= c:
            return ps
        ms = [_p_nn(m, m) for m in ms]
        n *= 2


def _delta_heads(qs, ks, vs, bg, alog, dtb, states):
    n_heads = len(qs)
    heads = range(n_heads)
    c = qs[0].shape[0]
    lane = lax.broadcasted_iota(jnp.int32, (c, LANES), 1)
    r = lax.broadcasted_iota(jnp.int32, (c, c), 0)
    s = lax.broadcasted_iota(jnp.int32, (c, c), 1)
    beta_all = jax.nn.sigmoid(bg)
    g_all = -jnp.exp(alog) * _softplus(bg + dtb)
    beta = [jnp.sum(jnp.where(lane == h, beta_all, 0.0), axis=1, keepdims=True) for h in heads]
    g = [jnp.sum(jnp.where(lane == n_heads + h, g_all, 0.0), axis=1, keepdims=True) for h in heads]
    g_row = [jnp.sum(jnp.where(r == s, g[h], 0.0), axis=0, keepdims=True) for h in heads]
    gc = [jnp.sum(jnp.where(s <= r, g_row[h], 0.0), axis=1, keepdims=True) for h in heads]
    gr = [jnp.sum(jnp.where(r <= s, g[h], 0.0), axis=0, keepdims=True) for h in heads]
    g_last = [jnp.sum(g[h], axis=0, keepdims=True) for h in heads]
    decay = [jnp.exp(jnp.where(r >= s, gc[h] - gr[h], NEG_BIG)) for h in heads]
    q_scaled = [qs[h] * (HEAD_DIM ** -0.5) for h in heads]
    k_beta = [ks[h] * beta[h] for h in heads]
    lmat = [jnp.where(r > s, _d_nt(k_beta[h], ks[h]) * decay[h], 0.0) for h in heads]
    attn = [_d_nt(q_scaled[h], ks[h]) * decay[h] for h in heads]
    tinv = _unit_lower_inverses(lmat, c)
    u = [_p_nn(tinv[h], vs[h] * beta[h]) for h in heads]
    w = [_p_nn(tinv[h], k_beta[h] * jnp.exp(gc[h])) for h in heads]
    v_new = [u[h] - _d_nn(w[h], states[h]) for h in heads]
    o_state = [_d_nn(q_scaled[h] * jnp.exp(gc[h]), states[h]) for h in heads]
    o = [o_state[h] + _d_nn(attn[h], v_new[h]) for h in heads]
    kv = [_d_tn(ks[h] * jnp.exp(g_last[h] - gc[h]), v_new[h]) for h in heads]
    new_states = [states[h] * jnp.exp(g_last[h]) + kv[h] for h in heads]
    return tuple(o), tuple(new_states)


def _delta_fwd(act, ba, alog, dtb, n_heads):
    t = act.shape[0]
    d = n_heads * HEAD_DIM
    c = DN_CHUNK
    nc = t // c

    def body(q_ref, k_ref, v_ref, bg_ref, al_ref, dt_ref, o_ref, snap_ref, st_ref):
        @pl.when(pl.program_id(0) == 0)
        def _():
            st_ref[...] = jnp.zeros(st_ref.shape, F32)

        snap_ref[0] = st_ref[...]
        cols = [slice(h * HEAD_DIM, (h + 1) * HEAD_DIM) for h in range(n_heads)]
        os, new_states = _delta_heads([q_ref[:, sl] for sl in cols], [k_ref[:, sl] for sl in cols],
                                      [v_ref[:, sl] for sl in cols], bg_ref[...], al_ref[...], dt_ref[...],
                                      [st_ref[h] for h in range(n_heads)])
        for h, sl in enumerate(cols):
            o_ref[:, sl] = os[h]
            st_ref[h] = new_states[h]

    return pl.pallas_call(
        body, name="delta_fwd", grid=(nc,),
        in_specs=[pl.BlockSpec((c, d), lambda i: (i, 0)), pl.BlockSpec((c, d), lambda i: (i, 1)),
                  pl.BlockSpec((c, d), lambda i: (i, 2)), pl.BlockSpec((c, LANES), lambda i: (i, 0)),
                  _const_spec((1, LANES)), _const_spec((1, LANES))],
        out_specs=[pl.BlockSpec((c, d), lambda i: (i, 0)),
                   pl.BlockSpec((1, n_heads, HEAD_DIM, HEAD_DIM), lambda i: (i, 0, 0, 0))],
        out_shape=[jax.ShapeDtypeStruct((t, d), F32), jax.ShapeDtypeStruct((nc, n_heads, HEAD_DIM, HEAD_DIM), F32)],
        scratch_shapes=[pltpu.VMEM((n_heads, HEAD_DIM, HEAD_DIM), F32)],
        compiler_params=_params(dimension_semantics=("arbitrary",)),
    )(act, act, act, ba, alog, dtb)


def _delta_bwd(act, ba, alog, dtb, snaps, do, n_heads):
    t = act.shape[0]
    d = n_heads * HEAD_DIM
    c = DN_CHUNK
    nc = t // c

    def body(q_ref, k_ref, v_ref, bg_ref, al_ref, dt_ref, snap_ref, do_ref,
             dact_ref, dbg_ref, dal_ref, ddt_ref, ds_ref):
        @pl.when(pl.program_id(0) == 0)
        def _():
            ds_ref[...] = jnp.zeros(ds_ref.shape, F32)
            dal_ref[...] = jnp.zeros((1, LANES), F32)
            ddt_ref[...] = jnp.zeros((1, LANES), F32)

        heads = range(n_heads)
        cols = [slice(h * HEAD_DIM, (h + 1) * HEAD_DIM) for h in heads]
        _, vjp = jax.vjp(_delta_heads, tuple(q_ref[:, sl] for sl in cols), tuple(k_ref[:, sl] for sl in cols),
                         tuple(v_ref[:, sl] for sl in cols), bg_ref[...], al_ref[...], dt_ref[...],
                         tuple(snap_ref[0, h] for h in heads))
        dq, dk, dv, dbg, dal, ddt, dst = vjp((tuple(do_ref[:, sl] for sl in cols), tuple(ds_ref[h] for h in heads)))
        for h, sl in enumerate(cols):
            dact_ref[:, sl] = dq[h]
            dact_ref[:, d + h * HEAD_DIM:d + (h + 1) * HEAD_DIM] = dk[h]
            dact_ref[:, 2 * d + h * HEAD_DIM:2 * d + (h + 1) * HEAD_DIM] = dv[h]
            ds_ref[h] = dst[h]
        dal_ref[...] += dal
        ddt_ref[...] += ddt
        dbg_ref[...] = dbg.astype(BF16)

    rev = lambda i: nc - 1 - i
    return pl.pallas_call(
        body, name="delta_bwd", grid=(nc,),
        in_specs=[pl.BlockSpec((c, d), lambda i: (rev(i), 0)), pl.BlockSpec((c, d), lambda i: (rev(i), 1)),
                  pl.BlockSpec((c, d), lambda i: (rev(i), 2)), pl.BlockSpec((c, LANES), lambda i: (rev(i), 0)),
                  _const_spec((1, LANES)), _const_spec((1, LANES)),
                  pl.BlockSpec((1, n_heads, HEAD_DIM, HEAD_DIM), lambda i: (rev(i), 0, 0, 0)),
                  pl.BlockSpec((c, d), lambda i: (rev(i), 0))],
        out_specs=[pl.BlockSpec((c, 3 * d), lambda i: (rev(i), 0)), pl.BlockSpec((c, LANES), lambda i: (rev(i), 0)),
                   _const_spec((1, LANES)), _const_spec((1, LANES))],
        out_shape=[jax.ShapeDtypeStruct((t, 3 * d), F32), jax.ShapeDtypeStruct((t, LANES), BF16),
                   jax.ShapeDtypeStruct((1, LANES), F32), jax.ShapeDtypeStruct((1, LANES), F32)],
        scratch_shapes=[pltpu.VMEM((n_heads, HEAD_DIM, HEAD_DIM), F32)],
        compiler_params=_params(dimension_semantics=("arbitrary",)),
    )(act, act, act, ba, alog, dtb, snaps, do)


def _head_norm2(a, b, ga, gb):
    return _rms(a, ga, RMS_EPS), _rms(b, gb, RMS_EPS)


def _sb_prep_fwd(proj, qn, kn, n_heads):
    t = proj.shape[0]
    d = n_heads * HEAD_DIM
    tm = _pick(t, (256, 128))

    def body(q_ref, k_ref, qn_ref, kn_ref, qo_ref, ko_ref):
        for h in range(n_heads):
            sl = slice(h * HEAD_DIM, (h + 1) * HEAD_DIM)
            qo_ref[:, sl], ko_ref[:, sl] = _head_norm2(q_ref[:, sl], k_ref[:, sl], qn_ref[...], kn_ref[...])

    return pl.pallas_call(
        body, name="sb_prep_fwd", grid=(t // tm,),
        in_specs=[pl.BlockSpec((tm, d), lambda i: (i, 4)), pl.BlockSpec((tm, d), lambda i: (i, 5)),
                  _const_spec((1, HEAD_DIM)), _const_spec((1, HEAD_DIM))],
        out_specs=[pl.BlockSpec((tm, d), lambda i: (i, 0)), pl.BlockSpec((tm, d), lambda i: (i, 0))],
        out_shape=[jax.ShapeDtypeStruct((t, d), F32), jax.ShapeDtypeStruct((t, d), F32)],
        compiler_params=_params(dimension_semantics=("arbitrary",)),
    )(proj, proj, qn, kn)


def _sb_prep_bwd(proj, qn, kn, dq, dk, n_heads):
    t = proj.shape[0]
    d = n_heads * HEAD_DIM
    tm = _pick(t, (256, 128))

    def body(q_ref, k_ref, qn_ref, kn_ref, dq_ref, dk_ref, dqo_ref, dko_ref, dqn_ref, dkn_ref):
        @pl.when(pl.program_id(0) == 0)
        def _():
            dqn_ref[...] = jnp.zeros((1, HEAD_DIM), F32)
            dkn_ref[...] = jnp.zeros((1, HEAD_DIM), F32)

        for h in range(n_heads):
            sl = slice(h * HEAD_DIM, (h + 1) * HEAD_DIM)
            _, vjp = jax.vjp(_head_norm2, q_ref[:, sl], k_ref[:, sl], qn_ref[...], kn_ref[...])
            da, db, dga, dgb = vjp((dq_ref[:, sl], dk_ref[:, sl]))
            dqo_ref[:, sl] = da.astype(BF16)
            dko_ref[:, sl] = db.astype(BF16)
            dqn_ref[...] += dga
            dkn_ref[...] += dgb

    return pl.pallas_call(
        body, name="sb_prep_bwd", grid=(t // tm,),
        in_specs=[pl.BlockSpec((tm, d), lambda i: (i, 4)), pl.BlockSpec((tm, d), lambda i: (i, 5)),
                  _const_spec((1, HEAD_DIM)), _const_spec((1, HEAD_DIM)),
                  pl.BlockSpec((tm, d), lambda i: (i, 0)), pl.BlockSpec((tm, d), lambda i: (i, 0))],
        out_specs=[pl.BlockSpec((tm, d), lambda i: (i, 0)), pl.BlockSpec((tm, d), lambda i: (i, 0)),
                   _const_spec((1, HEAD_DIM)), _const_spec((1, HEAD_DIM))],
        out_shape=[jax.ShapeDtypeStruct((t, d), BF16), jax.ShapeDtypeStruct((t, d), BF16),
                   jax.ShapeDtypeStruct((1, HEAD_DIM), F32), jax.ShapeDtypeStruct((1, HEAD_DIM), F32)],
        compiler_params=_params(dimension_semantics=("arbitrary",)),
    )(proj, proj, qn, kn, dq, dk)


def _cumsum_mm(x, tri):
    hi, lo = _split(x)
    return (lax.dot_general(hi, tri, (_NN, ((), ())), preferred_element_type=F32)
            + lax.dot_general(lo, tri, (_NN, ((), ())), preferred_element_type=F32))


def _sb_valid(i, j):
    row = lax.broadcasted_iota(jnp.int32, (SB_BLOCK, SB_BLOCK), 0)
    col = lax.broadcasted_iota(jnp.int32, (SB_BLOCK, SB_BLOCK), 1)
    return (col + j * SB_BLOCK) < (row + i * SB_BLOCK)


def _sb_attn_fwd(qb, kb, proj, n_heads):
    t = qb.shape[0]
    d = n_heads * HEAD_DIM
    nq = t // SB_BLOCK
    hb = _pick(n_heads, (4, 2, 1))
    wide = hb * HEAD_DIM
    v_col0 = 6 * n_heads // hb
    scale = HEAD_DIM ** -0.5

    def body(q_ref, k_ref, v_ref, o_ref, lt_ref):
        i = pl.program_id(1)
        row = lax.broadcasted_iota(jnp.int32, (SB_BLOCK, SB_BLOCK), 0)
        col = lax.broadcasted_iota(jnp.int32, (SB_BLOCK, SB_BLOCK), 1)
        after = jnp.where(row > col, 1.0, 0.0).astype(BF16)
        heads = [slice(h * HEAD_DIM, (h + 1) * HEAD_DIM) for h in range(hb)]
        every = range(hb)
        qs = [q_ref[:, sl].astype(BF16) for sl in heads]

        def step(m, carry):
            j0 = i - 2 * m
            js = (j0, jnp.maximum(j0 - 1, 0))
            valid = (_sb_valid(i, js[0]), jnp.logical_and(_sb_valid(i, js[1]), j0 >= 1))
            rows = [pl.ds(pl.multiple_of(j * SB_BLOCK, SB_BLOCK), SB_BLOCK) for j in js]
            units = [(h, b) for b in range(2) for h in every]
            z = {u: _mm_nt(qs[u[0]], k_ref[rows[u[1]], heads[u[0]]]) * scale for u in units}
            sp = {u: _softplus_raw(z[u]) for u in units}
            lm = {u: jnp.where(valid[u[1]], -sp[u], 0.0) for u in units}
            tail = {u: _cumsum_mm(lm[u], after) for u in units}
            later = {(h, 0): carry[h][1] for h in every}
            later.update({(h, 1): carry[h][1] + jnp.sum(lm[h, 0], axis=1, keepdims=True) for h in every})
            w = {u: jnp.where(valid[u[1]], jnp.exp(z[u] - sp[u] + later[u] + tail[u]), 0.0) for u in units}
            pv = {u: _mm_nn(w[u], v_ref[rows[u[1]], heads[u[0]]]) for u in units}
            return tuple((carry[h][0] + pv[h, 0] + pv[h, 1], later[h, 1] + jnp.sum(lm[h, 1], axis=1, keepdims=True))
                         for h in every)

        init = tuple((jnp.zeros((SB_BLOCK, HEAD_DIM), F32), jnp.zeros((SB_BLOCK, 1), F32)) for _ in heads)
        res = lax.fori_loop(0, (i + 2) // 2, step, init)
        for h, sl in enumerate(heads):
            o_ref[:, sl] = res[h][0]
            lt_ref[:, sl] = jnp.broadcast_to(res[h][1], (SB_BLOCK, HEAD_DIM))

    return pl.pallas_call(
        body, name="sb_attn_fwd", grid=(n_heads // hb, nq),
        in_specs=[pl.BlockSpec((SB_BLOCK, wide), lambda g, i: (i, g)),
                  pl.BlockSpec((t, wide), lambda g, i: (0, g)),
                  pl.BlockSpec((t, wide), lambda g, i: (0, v_col0 + g))],
        out_specs=[pl.BlockSpec((SB_BLOCK, wide), lambda g, i: (i, g)),
                   pl.BlockSpec((SB_BLOCK, wide), lambda g, i: (i, g))],
        out_shape=[jax.ShapeDtypeStruct((t, d), F32), jax.ShapeDtypeStruct((t, d), F32)],
        compiler_params=_params(dimension_semantics=("arbitrary", "arbitrary")),
    )(qb, kb, proj)


def _sb_attn_bwd(qb, kb, proj, ltot, do, n_heads):
    t = qb.shape[0]
    d = n_heads * HEAD_DIM
    nq = t // SB_BLOCK
    hb = _pick(n_heads, (4, 2, 1))
    wide = hb * HEAD_DIM
    v_col0 = 6 * n_heads // hb
    scale = HEAD_DIM ** -0.5

    def body(q_ref, k_ref, v_ref, lt_ref, do_ref, dq_ref, dk_ref, dv_ref):
        i = pl.program_id(1)

        @pl.when(i == 0)
        def _():
            dk_ref[...] = jnp.zeros((t, wide), F32)
            dv_ref[...] = jnp.zeros((t, wide), F32)

        row = lax.broadcasted_iota(jnp.int32, (SB_BLOCK, SB_BLOCK), 0)
        col = lax.broadcasted_iota(jnp.int32, (SB_BLOCK, SB_BLOCK), 1)
        upto = jnp.where(row <= col, 1.0, 0.0).astype(BF16)
        before = jnp.where(row < col, 1.0, 0.0).astype(BF16)
        heads = [slice(h * HEAD_DIM, (h + 1) * HEAD_DIM) for h in range(hb)]
        every = range(hb)
        qs = [q_ref[:, sl].astype(BF16) for sl in heads]
        dos = [do_ref[:, sl].astype(BF16) for sl in heads]
        totals = [jnp.max(lt_ref[:, sl], axis=1, keepdims=True) for sl in heads]

        def step(m, carry):
            js = (2 * m, jnp.minimum(2 * m + 1, nq - 1))
            valid = (_sb_valid(i, js[0]), jnp.logical_and(_sb_valid(i, js[1]), 2 * m + 1 <= i))
            rows = [pl.ds(pl.multiple_of(j * SB_BLOCK, SB_BLOCK), SB_BLOCK) for j in js]
            units = [(h, b) for b in range(2) for h in every]
            kj = {u: k_ref[rows[u[1]], heads[u[0]]].astype(BF16) for u in units}
            vj = {u: v_ref[rows[u[1]], heads[u[0]]].astype(BF16) for u in units}
            z = {u: _mm_nt(qs[u[0]], kj[u]) * scale for u in units}
            dw = {u: _mm_nt(dos[u[0]], vj[u]) for u in units}
            sp = {u: _softplus_raw(z[u]) for u in units}
            lm = {u: jnp.where(valid[u[1]], -sp[u], 0.0) for u in units}
            head = {u: _cumsum_mm(lm[u], upto) for u in units}
            lm_before = {(h, 0): carry[h][1] for h in every}
            lm_before.update({(h, 1): carry[h][1] + jnp.sum(lm[h, 0], axis=1, keepdims=True) for h in every})
            w = {u: jnp.where(valid[u[1]], jnp.exp(z[u] - sp[u] + totals[u[0]] - (lm_before[u] + head[u])), 0.0)
                 for u in units}
            e = {u: w[u] * dw[u] for u in units}
            e_local = {u: _mm_nn(e[u], before) for u in units}
            e_before = {(h, 0): carry[h][2] for h in every}
            e_before.update({(h, 1): carry[h][2] + jnp.sum(e[h, 0], axis=1, keepdims=True) for h in every})
            sig = {u: jnp.exp(z[u] - sp[u]) for u in units}
            dz = {u: jnp.where(valid[u[1]], e[u] * (1.0 - sig[u]) - (e_before[u] + e_local[u]) * sig[u], 0.0) * scale
                  for u in units}
            for h, b in units:
                dv_ref[rows[b], heads[h]] += _mm_tn(w[h, b], dos[h])
            for h, b in units:
                dk_ref[rows[b], heads[h]] += _mm_tn(dz[h, b], qs[h])
            dq = {u: _mm_nn(dz[u], kj[u]) for u in units}
            return tuple((carry[h][0] + dq[h, 0] + dq[h, 1],
                          lm_before[h, 1] + jnp.sum(lm[h, 1], axis=1, keepdims=True),
                          e_before[h, 1] + jnp.sum(e[h, 1], axis=1, keepdims=True)) for h in every)

        zero_col = jnp.zeros((SB_BLOCK, 1), F32)
        init = tuple((jnp.zeros((SB_BLOCK, HEAD_DIM), F32), zero_col, zero_col) for _ in heads)
        res = lax.fori_loop(0, (i + 2) // 2, step, init)
        for h, sl in enumerate(heads):
            dq_ref[:, sl] = res[h][0]

    return pl.pallas_call(
        body, name="sb_attn_bwd", grid=(n_heads // hb, nq),
        in_specs=[pl.BlockSpec((SB_BLOCK, wide), lambda g, i: (i, g)),
                  pl.BlockSpec((t, wide), lambda g, i: (0, g)),
                  pl.BlockSpec((t, wide), lambda g, i: (0, v_col0 + g)),
                  pl.BlockSpec((SB_BLOCK, wide), lambda g, i: (i, g)),
                  pl.BlockSpec((SB_BLOCK, wide), lambda g, i: (i, g))],
        out_specs=[pl.BlockSpec((SB_BLOCK, wide), lambda g, i: (i, g)),
                   pl.BlockSpec((t, wide), lambda g, i: (0, g)),
                   pl.BlockSpec((t, wide), lambda g, i: (0, g))],
        out_shape=[jax.ShapeDtypeStruct((t, d), F32), jax.ShapeDtypeStruct((t, d), F32),
                   jax.ShapeDtypeStruct((t, d), F32)],
        compiler_params=_params(dimension_semantics=("arbitrary", "arbitrary")),
    )(qb, kb, proj, ltot, do)


def _gated_norm(oa, z, gn):
    return _rms(oa, gn, RMS_EPS) * _silu(z)


def _merge_gates(ya, yb, ga, gb):
    return jax.nn.sigmoid(ga) * ya + jax.nn.sigmoid(gb) * yb


def _merge_fwd(x1, oa, proj, ob, gn, wa, wb, wo, layer, n_heads):
    t, d = x1.shape
    tm = _pick(t, (256, 128))
    square = _layer_spec(layer, (d, d), lambda i: (0, 0), single=True)

    def body(x_ref, oa_ref, z_ref, ob_ref, ga_ref, gb_ref, gn_ref, wa_ref, wb_ref, wo_ref, o_ref, na_ref):
        for h in range(n_heads):
            sl = slice(h * HEAD_DIM, (h + 1) * HEAD_DIM)
            na_ref[:, sl] = _gated_norm(oa_ref[:, sl], z_ref[:, sl], gn_ref[...]).astype(BF16)
        m = _merge_gates(_mm_nn(na_ref[...], wa_ref[...]), _mm_nn(ob_ref[...], wb_ref[...]), ga_ref[...], gb_ref[...])
        o_ref[...] = x_ref[...] + _mm_nn(m, wo_ref[...])

    tile = lambda k: pl.BlockSpec((tm, d), lambda i: (i, k))
    return pl.pallas_call(
        body, name="merge_fwd", grid=(t // tm,),
        in_specs=[tile(0), tile(0), tile(3), tile(0), tile(7), tile(8), _const_spec((1, HEAD_DIM)),
                  square, square, square],
        out_specs=tile(0),
        out_shape=jax.ShapeDtypeStruct((t, d), F32),
        scratch_shapes=[pltpu.VMEM((tm, d), BF16)],
        compiler_params=_params(dimension_semantics=("arbitrary",)),
    )(x1, oa, proj, ob, proj, proj, gn, wa, wb, wo)


def _merge_bwd(oa, proj, ob, dy, gn, wa, wb, wo, layer, n_heads):
    t, d = oa.shape
    tm = _pick(t, (256, 128))
    nt = t // tm
    square = _layer_spec(layer, (d, d), lambda i: (0, 0), single=True)

    def body(oa_ref, z_ref, ob_ref, ga_ref, gb_ref, dy_ref, gn_ref, wa_ref, wb_ref, wo_ref,
             doa_ref, dz_ref, dob_ref, dga_ref, dgb_ref, dgn_ref, dwa_hbm, dwb_hbm, dwo_hbm,
             na_ref, dna_ref, dwa_ref, dwb_ref, dwo_ref, stage_ref):
        i = pl.program_id(0)

        @pl.when(i == 0)
        def _():
            dgn_ref[...] = jnp.zeros((1, HEAD_DIM), F32)
            dwa_ref[...] = jnp.zeros((d, d), F32)
            dwb_ref[...] = jnp.zeros((d, d), F32)
            dwo_ref[...] = jnp.zeros((d, d), F32)

        for h in range(n_heads):
            sl = slice(h * HEAD_DIM, (h + 1) * HEAD_DIM)
            na_ref[:, sl] = _gated_norm(oa_ref[:, sl], z_ref[:, sl], gn_ref[...]).astype(BF16)
        dy = dy_ref[...].astype(BF16)
        ob = ob_ref[...].astype(BF16)
        ya = _mm_nn(na_ref[...], wa_ref[...])
        yb = _mm_nn(ob, wb_ref[...])
        m, vjp = jax.vjp(_merge_gates, ya, yb, ga_ref[...], gb_ref[...])
        dwo_ref[...] += _mm_tn(m, dy)
        dya, dyb, dga, dgb = vjp(_mm_nt(dy, wo_ref[...]))
        dga_ref[...] = dga.astype(BF16)
        dgb_ref[...] = dgb.astype(BF16)
        dwa_ref[...] += _mm_tn(na_ref[...], dya)
        dwb_ref[...] += _mm_tn(ob, dyb)
        dob_ref[...] = _mm_nt(dyb, wb_ref[...])
        dna_ref[...] = _mm_nt(dya, wa_ref[...])
        for h in range(n_heads):
            sl = slice(h * HEAD_DIM, (h + 1) * HEAD_DIM)
            _, vjp_h = jax.vjp(_gated_norm, oa_ref[:, sl], z_ref[:, sl], gn_ref[...])
            doa, dz, dgn = vjp_h(dna_ref[:, sl])
            doa_ref[:, sl] = doa
            dz_ref[:, sl] = dz.astype(BF16)
            dgn_ref[...] += dgn

        @pl.when(i == nt - 1)
        def _():
            for acc, out in ((dwa_ref, dwa_hbm), (dwb_ref, dwb_hbm), (dwo_ref, dwo_hbm)):
                stage_ref[...] = acc[...].astype(BF16)
                pltpu.sync_copy(stage_ref, out)

    tile = lambda k: pl.BlockSpec((tm, d), lambda i: (i, k))
    any_spec = pl.BlockSpec(memory_space=pl.ANY)
    return pl.pallas_call(
        body, name="merge_bwd", grid=(nt,),
        in_specs=[tile(0), tile(3), tile(0), tile(7), tile(8), tile(0), _const_spec((1, HEAD_DIM)),
                  square, square, square],
        out_specs=[tile(0), tile(0), tile(0), tile(0), tile(0), _const_spec((1, HEAD_DIM)),
                   any_spec, any_spec, any_spec],
        out_shape=[jax.ShapeDtypeStruct((t, d), F32), jax.ShapeDtypeStruct((t, d), BF16),
                   jax.ShapeDtypeStruct((t, d), F32), jax.ShapeDtypeStruct((t, d), BF16),
                   jax.ShapeDtypeStruct((t, d), BF16), jax.ShapeDtypeStruct((1, HEAD_DIM), F32),
                   jax.ShapeDtypeStruct((d, d), BF16), jax.ShapeDtypeStruct((d, d), BF16),
                   jax.ShapeDtypeStruct((d, d), BF16)],
        scratch_shapes=[pltpu.VMEM((tm, d), BF16), pltpu.VMEM((tm, d), F32),
                        pltpu.VMEM((d, d), F32), pltpu.VMEM((d, d), F32), pltpu.VMEM((d, d), F32),
                        pltpu.VMEM((d, d), BF16)],
        compiler_params=_params(dimension_semantics=("arbitrary",)),
    )(oa, proj, ob, proj, proj, dy, gn, wa, wb, wo)


def _loss_head(y, target):
    t, d = y.shape
    tm = _pick(t, (256, 128))

    def body(y_ref, t_ref, dy_ref, loss_ref):
        @pl.when(pl.program_id(0) == 0)
        def _():
            loss_ref[...] = jnp.zeros((8, LANES), F32)

        err = y_ref[...] - t_ref[...]
        dy_ref[...] = err * (1.0 / d)
        per_token = jnp.sum(err * err, axis=1, keepdims=True) * (1.0 / d)
        loss_ref[...] += 0.5 * jnp.sum(per_token, axis=0, keepdims=True)

    return pl.pallas_call(
        body, name="loss_head", grid=(t // tm,),
        in_specs=[pl.BlockSpec((tm, d), lambda i: (i, 0)), pl.BlockSpec((tm, d), lambda i: (i, 0))],
        out_specs=[pl.BlockSpec((tm, d), lambda i: (i, 0)), _const_spec((8, LANES))],
        out_shape=[jax.ShapeDtypeStruct((t, d), F32), jax.ShapeDtypeStruct((8, LANES), F32)],
        compiler_params=_params(dimension_semantics=("arbitrary",)),
    )(y, target)


def _adamw(w, g, m, v):
    rows, cols = w.shape
    tr = rows
    for cand in (512, 256, 128, 64, 32, 16, 8):
        if rows % cand == 0 and cand * cols * 4 <= 2 * 1024 * 1024:
            tr = cand
            break

    def body(w_ref, g_ref, m_ref, v_ref, d_ref, mo_ref, vo_ref):
        g = g_ref[...]
        m2 = ADAM_B1 * m_ref[...] + (1.0 - ADAM_B1) * g
        v2 = ADAM_B2 * v_ref[...] + (1.0 - ADAM_B2) * (g * g)
        m_hat = m2 / (1.0 - ADAM_B1 ** ADAM_STEP)
        v_hat = v2 / (1.0 - ADAM_B2 ** ADAM_STEP)
        d_ref[...] = -ADAM_LR * (m_hat / (jnp.sqrt(v_hat) + ADAM_EPS) + ADAM_WD * w_ref[...])
        mo_ref[...] = m2
        vo_ref[...] = v2

    spec = pl.BlockSpec((tr, cols), lambda i: (i, 0))
    shape = jax.ShapeDtypeStruct((rows, cols), F32)
    return pl.pallas_call(
        body, name="adamw", grid=(rows // tr,), in_specs=[spec] * 4, out_specs=[spec] * 3,
        out_shape=[shape] * 3, compiler_params=_params(dimension_semantics=("arbitrary",)),
    )(w, g, m, v)


def _place():
    return lax.axis_index("x"), lax.axis_index("y"), lax.axis_index("c")


def _other_chips(x, y):
    return [(1 - x, y), (x, 1 - y), (1 - x, 1 - y)]


def _tile_rows(rows, cols, itemsize, cap=1536 * 1024):
    best = None
    for cand in range(16, rows + 1, 16):
        if rows % cand == 0 and cand * cols * itemsize <= cap:
            best = cand
    return best if best is not None else rows


def _allgather_layer(shards, layer, collective_id):
    n = len(shards)

    def body(*refs):
        srcs, outs = refs[:n], refs[n:2 * n]
        send_sems, recv_sems = refs[2 * n:]
        x, y, c = _place()
        me, sibling = (x, y, c), (x, y, 1 - c)
        chips = _other_chips(x, y)
        barrier = pltpu.get_barrier_semaphore()
        for peer in [(*chip, c) for chip in chips] + [sibling]:
            pl.semaphore_signal(barrier, inc=1, device_id=peer, device_id_type=MESH)
        pl.semaphore_wait(barrier, N_CHIPS)

        def half(w, which):
            rows = shards[w].shape[1] // 2
            return pl.ds(which * rows, rows)

        def copy(w, k, shard, which, to, from_src=False):
            part = half(w, which)
            return pltpu.make_async_remote_copy(
                src_ref=srcs[w].at[layer, part] if from_src else outs[w].at[shard, part],
                dst_ref=outs[w].at[shard, part], send_sem=send_sems.at[6 * w + k], recv_sem=recv_sems.at[6 * w + k],
                device_id=to, device_id_type=MESH)

        first = [copy(w, j, 2 * x + y, c, (*chip, c), from_src=True) for j, chip in enumerate(chips) for w in range(n)]
        for cp in first:
            cp.start()
        passed = []
        for j, (cx, cy) in enumerate(chips):
            for w in range(n):
                copy(w, j, 2 * cx + cy, c, me).wait_recv()
                cp = copy(w, 3 + j, 2 * cx + cy, c, sibling)
                cp.start()
                passed.append(cp)
        for j, (cx, cy) in enumerate(chips):
            for w in range(n):
                copy(w, 3 + j, 2 * cx + cy, 1 - c, me).wait_recv()
        for cp in first + passed:
            cp.wait_send()

    return pl.kernel(
        body, name=f"allgather_layer{layer}_id{collective_id}",
        out_type=[jax.ShapeDtypeStruct((N_CHIPS,) + s.shape[1:], s.dtype) for s in shards],
        mesh=plsc.ScalarSubcoreMesh(axis_name="sequencer", num_cores=1),
        scratch_types=[pltpu.SemaphoreType.DMA((6 * n,)), pltpu.SemaphoreType.DMA((6 * n,))],
        compiler_params=pltpu.CompilerParams(collective_id=collective_id),
    )(*shards)


def _swap_halves(grads):
    n = len(grads)

    def body(*refs):
        gs, gots = refs[:n], refs[n:2 * n]
        send_sems, recv_sems = refs[2 * n:]
        x, y, c = _place()
        copies = []
        for w in range(n):
            half = grads[w].shape[1] // 2
            copies.append(pltpu.make_async_remote_copy(
                src_ref=gs[w].at[:, pl.ds((1 - c) * half, half)], dst_ref=gots[w], send_sem=send_sems.at[w],
                recv_sem=recv_sems.at[w], device_id=(x, y, 1 - c), device_id_type=MESH))
        for cp in copies:
            cp.start()
        for cp in copies:
            cp.wait()

    hbm = pl.BlockSpec(memory_space=pl.ANY)
    return pl.pallas_call(
        body, name="swap_halves", in_specs=[hbm] * n, out_specs=[hbm] * n,
        out_shape=[jax.ShapeDtypeStruct((g.shape[0], g.shape[1] // 2, g.shape[2]), g.dtype) for g in grads],
        scratch_shapes=[pltpu.SemaphoreType.DMA((n,)), pltpu.SemaphoreType.DMA((n,))],
    )(*grads)


def _add_half(grad, got, c_idx):
    n, rows, all_cols = grad.shape
    side = N_CHIPS // n
    cols = all_cols // side
    half = rows // 2
    tr = _tile_rows(half, cols, 2)
    nb = half // tr

    def body(c_ref, a_ref, b_ref, o_ref):
        o_ref[...] = (a_ref[...].astype(F32) + b_ref[...].astype(F32)).astype(o_ref.dtype)

    return pl.pallas_call(
        body, name="add_half",
        grid_spec=pltpu.PrefetchScalarGridSpec(
            num_scalar_prefetch=1, grid=(N_CHIPS, nb),
            in_specs=[pl.BlockSpec((1, tr, cols), lambda s, r, c_ref: (s // side, c_ref[0] * nb + r, s % side)),
                      pl.BlockSpec((1, tr, cols), lambda s, r, c_ref: (s // side, r, s % side))],
            out_specs=pl.BlockSpec((1, tr, cols), lambda s, r, c_ref: (s, r, 0))),
        out_shape=jax.ShapeDtypeStruct((N_CHIPS, half, cols), grad.dtype),
        compiler_params=_params(dimension_semantics=("arbitrary", "arbitrary")),
    )(c_idx, grad, got)


def _scatter_partials(parts, layer, collective_id):
    n = len(parts)

    def body(*refs):
        ps, gots = refs[:n], refs[n:2 * n]
        send_sems, recv_sems = refs[2 * n:]
        x, y, c = _place()
        chips = _other_chips(x, y)
        barrier = pltpu.get_barrier_semaphore()
        for chip in chips:
            pl.semaphore_signal(barrier, inc=1, device_id=(*chip, c), device_id_type=MESH)
        pl.semaphore_wait(barrier, N_CHIPS - 1)
        copies = [pltpu.make_async_remote_copy(src_ref=ps[w].at[2 * cx + cy], dst_ref=gots[w].at[j],
                                               send_sem=send_sems.at[3 * w + j], recv_sem=recv_sems.at[3 * w + j],
                                               device_id=(cx, cy, c), device_id_type=MESH)
                  for j, (cx, cy) in enumerate(chips) for w in range(n)]
        for cp in copies:
            cp.start()
        for cp in copies:
            cp.wait()

    return pl.kernel(
        body, name=f"scatter_partials{layer}_id{collective_id}",
        out_type=[jax.ShapeDtypeStruct((N_CHIPS - 1,) + p.shape[1:], p.dtype) for p in parts],
        mesh=plsc.ScalarSubcoreMesh(axis_name="sequencer", num_cores=1),
        scratch_types=[pltpu.SemaphoreType.DMA((3 * n,)), pltpu.SemaphoreType.DMA((3 * n,))],
        compiler_params=pltpu.CompilerParams(collective_id=collective_id),
    )(*parts)


def _sum_partials(part, got, s_idx, c_idx, layer, depth, stacked=None):
    n, half, cols = part.shape
    tr = _tile_rows(half, cols, 2, cap=1024 * 1024)
    nb = half // tr

    def body(s_ref, c_ref, a_ref, b_ref, *rest):
        o_ref = rest[-1]
        acc = a_ref[0].astype(F32)
        for j in range(n - 1):
            acc = acc + b_ref[j].astype(F32)
        o_ref[...] = acc

    in_specs = [pl.BlockSpec((1, tr, cols), lambda r, s_ref, c_ref: (s_ref[0], r, 0)),
                pl.BlockSpec((n - 1, tr, cols), lambda r, s_ref, c_ref: (0, r, 0))]
    operands = [s_idx, c_idx, part, got]
    aliases = {}
    if stacked is not None:
        in_specs.append(pl.BlockSpec(memory_space=pl.ANY))
        operands.append(stacked)
        aliases = {len(operands) - 1: 0}
    return pl.pallas_call(
        body, name="sum_partials",
        grid_spec=pltpu.PrefetchScalarGridSpec(
            num_scalar_prefetch=2, grid=(nb,), in_specs=in_specs,
            out_specs=pl.BlockSpec((None, tr, cols), lambda r, s_ref, c_ref: (layer, c_ref[0] * nb + r, 0))),
        out_shape=jax.ShapeDtypeStruct((depth, 2 * half, cols), F32),
        input_output_aliases=aliases,
        compiler_params=_params(dimension_semantics=("arbitrary",)),
    )(*operands)


def _join_halves(bufs, layer):
    n = len(bufs)

    def body(*refs):
        outs = refs[n:2 * n]
        send_sems, recv_sems = refs[2 * n:]
        x, y, c = _place()
        copies = []
        for w in range(n):
            half = bufs[w].shape[1] // 2
            mine = outs[w].at[layer, pl.ds(c * half, half)]
            copies.append(pltpu.make_async_remote_copy(src_ref=mine, dst_ref=mine, send_sem=send_sems.at[w],
                                                       recv_sem=recv_sems.at[w], device_id=(x, y, 1 - c),
                                                       device_id_type=MESH))
        for cp in copies:
            cp.start()
        for cp in copies:
            cp.wait()

    hbm = pl.BlockSpec(memory_space=pl.ANY)
    return pl.pallas_call(
        body, name="join_halves", in_specs=[hbm] * n, out_specs=[hbm] * n,
        out_shape=[jax.ShapeDtypeStruct(b.shape, b.dtype) for b in bufs],
        input_output_aliases={w: w for w in range(n)},
        scratch_shapes=[pltpu.SemaphoreType.DMA((n,)), pltpu.SemaphoreType.DMA((n,))],
    )(*bufs)


def _allreduce_small(v, name):
    rows = v.shape[0]

    def body(v_ref, o_ref, gath, send_sems, recv_sems):
        x, y, c = _place()
        idx = 4 * x + 2 * y + c
        gath[0] = v_ref[...]
        copies = []
        for r in range(1, N_DEV):
            peer = (1 - x if r & 4 else x, 1 - y if r & 2 else y, 1 - c if r & 1 else c)
            cp = pltpu.make_async_remote_copy(src_ref=v_ref, dst_ref=gath.at[r], send_sem=send_sems.at[r - 1],
                                              recv_sem=recv_sems.at[r - 1], device_id=peer, device_id_type=MESH)
            cp.start()
            copies.append(cp)
        for cp in copies:
            cp.wait()
        acc = gath[idx]
        for a in range(1, N_DEV):
            acc = acc + gath[lax.bitwise_xor(idx, a)]
        o_ref[...] = acc

    vmem = pl.BlockSpec(memory_space=pltpu.VMEM)
    return pl.pallas_call(
        body, name=name, in_specs=[vmem], out_specs=vmem,
        out_shape=jax.ShapeDtypeStruct((rows, LANES), F32),
        scratch_shapes=[pltpu.VMEM((N_DEV, rows, LANES), F32), pltpu.SemaphoreType.DMA((N_DEV - 1,)),
                        pltpu.SemaphoreType.DMA((N_DEV - 1,))],
    )(v)


def _join_shards(name, gathered):
    if name in COL_SHARDED:
        return jnp.concatenate([gathered[s] for s in range(N_CHIPS)], axis=1)[None]
    return gathered.reshape(1, N_CHIPS * gathered.shape[1], gathered.shape[2])


def _row_shards(g):
    return g.reshape(N_CHIPS, g.shape[0] // N_CHIPS, g.shape[1])


def _mixer_runs(width, cut, n_small):
    runs = []
    for s in range(N_CHIPS):
        lo, hi = s * width, (s + 1) * width
        spans = ((True, lo, min(hi, cut)), (False, max(lo, cut), min(hi, cut + n_small)),
                 (True, max(lo, cut + n_small), hi))
        runs.append([(is_main, a - lo, b - lo) for is_main, a, b in spans if a < b])
    return runs


def _split_mixer_weight(gathered, runs):
    main = [gathered[s][:, a:b] for s, parts in enumerate(runs) for is_main, a, b in parts if is_main]
    small = [gathered[s][:, a:b] for s, parts in enumerate(runs) for is_main, a, b in parts if not is_main]
    small = small[0] if len(small) == 1 else jnp.concatenate(small, axis=1)
    return jnp.concatenate(main, axis=1)[None], jnp.pad(small, ((0, 0), (0, LANES - small.shape[1])))[None]


def _join_mixer_grad(d_main, d_small, runs):
    shards, m, k = [], 0, 0
    for parts in runs:
        cols = []
        for is_main, a, b in parts:
            if is_main:
                cols.append(d_main[:, m:m + b - a])
                m += b - a
            else:
                cols.append(d_small[:, k:k + b - a].astype(d_main.dtype))
                k += b - a
        shards.append(cols[0] if len(cols) == 1 else jnp.concatenate(cols, axis=1))
    return jnp.stack(shards)


def _pad_small(flat):
    n = flat.shape[0]
    block = 8 * LANES
    padded = -(-n // block) * block
    return jnp.pad(flat, (0, padded - n)).reshape(padded // LANES, LANES)


def kernel(x, ffn1_norm, ffn1_w_in, ffn1_w_out, mix_norm, w_in, dn_conv_w, dn_a_log, dn_dt_bias, dn_out_norm, sb_q_norm, sb_k_norm, w_branch_a, w_branch_b, w_out, ffn2_norm, ffn2_w_in, ffn2_w_out, loss_target, m_ffn1_norm, m_ffn1_w_in, m_ffn1_w_out, m_mix_norm, m_w_in, m_dn_conv_w, m_dn_a_log, m_dn_dt_bias, m_dn_out_norm, m_sb_q_norm, m_sb_k_norm, m_w_branch_a, m_w_branch_b, m_w_out, m_ffn2_norm, m_ffn2_w_in, m_ffn2_w_out, v_ffn1_norm, v_ffn1_w_in, v_ffn1_w_out, v_mix_norm, v_w_in, v_dn_conv_w, v_dn_a_log, v_dn_dt_bias, v_dn_out_norm, v_sb_q_norm, v_sb_k_norm, v_w_branch_a, v_w_branch_b, v_w_out, v_ffn2_norm, v_ffn2_w_in, v_ffn2_w_out):
    w = dict(ffn1_norm=ffn1_norm, ffn1_w_in=ffn1_w_in, ffn1_w_out=ffn1_w_out, mix_norm=mix_norm, w_in=w_in,
             dn_conv_w=dn_conv_w, dn_a_log=dn_a_log, dn_dt_bias=dn_dt_bias, dn_out_norm=dn_out_norm,
             sb_q_norm=sb_q_norm, sb_k_norm=sb_k_norm, w_branch_a=w_branch_a, w_branch_b=w_branch_b, w_out=w_out,
             ffn2_norm=ffn2_norm, ffn2_w_in=ffn2_w_in, ffn2_w_out=ffn2_w_out)
    mom = dict(ffn1_norm=m_ffn1_norm, ffn1_w_in=m_ffn1_w_in, ffn1_w_out=m_ffn1_w_out, mix_norm=m_mix_norm, w_in=m_w_in,
               dn_conv_w=m_dn_conv_w, dn_a_log=m_dn_a_log, dn_dt_bias=m_dn_dt_bias, dn_out_norm=m_dn_out_norm,
               sb_q_norm=m_sb_q_norm, sb_k_norm=m_sb_k_norm, w_branch_a=m_w_branch_a, w_branch_b=m_w_branch_b,
               w_out=m_w_out, ffn2_norm=m_ffn2_norm, ffn2_w_in=m_ffn2_w_in, ffn2_w_out=m_ffn2_w_out)
    var = dict(ffn1_norm=v_ffn1_norm, ffn1_w_in=v_ffn1_w_in, ffn1_w_out=v_ffn1_w_out, mix_norm=v_mix_norm, w_in=v_w_in,
               dn_conv_w=v_dn_conv_w, dn_a_log=v_dn_a_log, dn_dt_bias=v_dn_dt_bias, dn_out_norm=v_dn_out_norm,
               sb_q_norm=v_sb_q_norm, sb_k_norm=v_sb_k_norm, w_branch_a=v_w_branch_a, w_branch_b=v_w_branch_b,
               w_out=v_w_out, ffn2_norm=v_ffn2_norm, ffn2_w_in=v_ffn2_w_in, ffn2_w_out=v_ffn2_w_out)

    _, t, d = x.shape
    depth = ffn1_norm.shape[0]
    n_heads = d // HEAD_DIM
    conv_cols = dn_conv_w.shape[2]
    assert d % HEAD_DIM == 0 and t % SB_BLOCK == 0 and 2 * n_heads <= LANES and depth % 2 == 0
    assert w_in.shape[2] * N_CHIPS == 9 * d + 2 * n_heads and conv_cols * N_CHIPS == 3 * d

    x_idx, y_idx, c_idx = _place()
    shard = 2 * x_idx + y_idx
    c_arr = jnp.reshape(c_idx, (1,)).astype(jnp.int32)
    s_arr = jnp.reshape(shard, (1,)).astype(jnp.int32)

    mine = {n: w[n].astype(BF16) for n in BIG}
    runs = _mixer_runs(w_in.shape[2], 4 * d, 2 * n_heads)
    groups = (("ffn1_w_in", "ffn1_w_out"), ("w_in",), ("w_branch_a", "w_branch_b", "w_out"),
              ("ffn2_w_in", "ffn2_w_out"))
    first = groups[0]
    rest = tuple(n for g in groups[1:] for n in g)
    n_gathers = depth + len(groups) - 1

    def gather(names, l, collective_id):
        return dict(zip(names, _allgather_layer([mine[n] for n in names], l, collective_id)))

    arriving = [{}]
    for k, names in enumerate(groups):
        arriving[0].update(gather(names, 0, 0 if k == 0 else depth + k - 1))
    arriving += [gather(BIG, l, l) for l in range(1, depth)]

    def layer_weights(l, names, after):
        gathered, after = lax.optimization_barrier(({n: arriving[l][n] for n in names}, after))
        full = {}
        for n in names:
            g = lax.dynamic_update_slice(gathered[n], mine[n][l][None], (shard, 0, 0))
            if n == "w_in":
                full["w_main"], full["w_ba"] = _split_mixer_weight(g, runs)
            else:
                full[n] = _join_shards(n, g)
        return full, after

    conv_place = lax.dynamic_update_slice(jnp.zeros((depth, DN_CONV, 3 * d), F32), dn_conv_w, (0, 0, shard * conv_cols))
    conv_rows = _pad_small(conv_place.reshape(-1))
    conv_full = (0.5 * _allreduce_small(conv_rows, "allgather_conv")).reshape(-1)[:depth * DN_CONV * 3 * d]
    conv_full = jnp.pad(conv_full.reshape(depth, DN_CONV, 3 * d), ((0, 0), (0, CONV_ROWS - DN_CONV), (0, 0)))

    def head_row(vals):
        return jnp.pad(vals, (n_heads, LANES - 2 * n_heads)).reshape(1, LANES)

    saved, layers = [], []
    cur = x[0]
    for l in range(depth):
        full, x0 = layer_weights(l, groups[0], cur)
        layers.append(full)
        x1 = _ffn_fwd(x0, ffn1_norm[l][None], full["ffn1_w_in"], full["ffn1_w_out"], 0)
        later, x1 = layer_weights(l, groups[1], x1)
        full.update(later)
        proj, ba = _proj_fwd(x1, mix_norm[l][None], full["w_main"], full["w_ba"], 0)
        act = _dn_prep_fwd(proj, conv_full[l], n_heads)
        alog, dtb = head_row(dn_a_log[l]), head_row(dn_dt_bias[l])
        oa, snaps = _delta_fwd(act, ba, alog, dtb, n_heads)
        qb, kb = _sb_prep_fwd(proj, sb_q_norm[l][None], sb_k_norm[l][None], n_heads)
        ob, ltot = _sb_attn_fwd(qb, kb, proj, n_heads)
        later, (oa, ob) = layer_weights(l, groups[2], (oa, ob))
        full.update(later)
        x2 = _merge_fwd(x1, oa, proj, ob, dn_out_norm[l][None], full["w_branch_a"], full["w_branch_b"],
                        full["w_out"], 0, n_heads)
        later, x2 = layer_weights(l, groups[3], x2)
        full.update(later)
        cur = _ffn_fwd(x2, ffn2_norm[l][None], full["ffn2_w_in"], full["ffn2_w_out"], 0)
        saved.append((x0, x1, proj, ba, act, alog, dtb, oa, snaps, qb, kb, ob, ltot, x2))

    dcur, loss_part = _loss_head(cur, loss_target[0])

    grads = {n: [None] * depth for n in WEIGHTS}
    reduced = {n: None for n in BIG}
    in_flight = []

    def start_reduce(l, names, collective_id):
        g_major = [grads[n][l] for n in names]
        parts = [_add_half(g, got, c_arr) for g, got in zip(g_major, _swap_halves(g_major))]
        return l, names, parts, _scatter_partials(parts, l, collective_id)

    def finish_reduce(started, after):
        for l, names, parts, arrived in started:
            arrived, after = lax.optimization_barrier((arrived, after))
            halves = [_sum_partials(p, got, s_arr, c_arr, l, depth, stacked=reduced[n])
                      for n, p, got in zip(names, parts, arrived)]
            reduced.update(zip(names, _join_halves(halves, l)))
        return after

    for l in reversed(range(depth)):
        x0, x1, proj, ba, act, alog, dtb, oa, snaps, qb, kb, ob, ltot, x2 = saved[l]
        full = layers[l]
        dx2, dg, dwi, dwo = _ffn_bwd(x2, ffn2_norm[l][None], dcur, full["ffn2_w_in"], full["ffn2_w_out"], 0)
        grads["ffn2_norm"][l] = dg[0]
        grads["ffn2_w_in"][l] = dwi
        grads["ffn2_w_out"][l] = _row_shards(dwo)
        doa, dz, dob, dga, dgb, dgn, dwa, dwb, dwout = _merge_bwd(
            oa, proj, ob, dx2, dn_out_norm[l][None], full["w_branch_a"], full["w_branch_b"], full["w_out"], 0,
            n_heads)
        grads["dn_out_norm"][l] = dgn[0]
        grads["w_branch_a"][l], grads["w_branch_b"][l] = _row_shards(dwa), _row_shards(dwb)
        grads["w_out"][l] = _row_shards(dwout)
        dqb, dkb, dvb = _sb_attn_bwd(qb, kb, proj, ltot, dob, n_heads)
        dsq, dsk, dqn, dkn = _sb_prep_bwd(proj, sb_q_norm[l][None], sb_k_norm[l][None], dqb, dkb, n_heads)
        grads["sb_q_norm"][l], grads["sb_k_norm"][l] = dqn[0], dkn[0]
        dact, dba, dal, ddt = _delta_bwd(act, ba, alog, dtb, snaps, doa, n_heads)
        grads["dn_a_log"][l] = dal[0, n_heads:2 * n_heads]
        grads["dn_dt_bias"][l] = ddt[0, n_heads:2 * n_heads]
        dqkv, dconv = _dn_prep_bwd(proj, conv_full[l], dact, n_heads)
        grads["dn_conv_w"][l] = dconv[:DN_CONV]
        dproj = jnp.concatenate([dqkv, dz, dsq, dsk, dvb.astype(BF16), dga, dgb], axis=1)
        dx1, dg, dwm, dwba = _proj_bwd(x1, mix_norm[l][None], dx2, dproj, dba, full["w_main"], full["w_ba"], 0)
        grads["mix_norm"][l] = dg[0]
        grads["w_in"][l] = _join_mixer_grad(dwm, dwba, runs)
        early = start_reduce(l, rest, collective_id=n_gathers + l)
        dcur, dg, dwi, dwo = _ffn_bwd(x0, ffn1_norm[l][None], dx1, full["ffn1_w_in"], full["ffn1_w_out"], 0)
        grads["ffn1_norm"][l] = dg[0]
        grads["ffn1_w_in"][l] = dwi
        grads["ffn1_w_out"][l] = _row_shards(dwo)

        dcur = finish_reduce(in_flight, dcur)
        in_flight = [early, start_reduce(l, first, collective_id=n_gathers + depth + l)]
    dcur = finish_reduce(in_flight, dcur)
    final = reduced
    grads = {n: jnp.stack(grads[n]) for n in SMALL + ("dn_conv_w",)}

    small_names = SMALL + ("dn_conv_w",)
    small_sizes = [int(np.prod(grads[n].shape)) for n in small_names]
    small_off = np.concatenate([[0], np.cumsum(small_sizes)])
    small = jnp.concatenate([grads[n].reshape(-1) for n in small_names] + [loss_part[0, :1]])
    small_sum = _allreduce_small(_pad_small(small), "allreduce_small").reshape(-1)
    for i, n in enumerate(small_names):
        final[n] = small_sum[small_off[i]:small_off[i + 1]].reshape(grads[n].shape)
    final["dn_conv_w"] = lax.dynamic_slice(final["dn_conv_w"], (0, 0, shard * conv_cols), (depth, DN_CONV, conv_cols))
    loss = small_sum[small_off[-1]]

    deltas, new_m, new_v = {}, {}, {}
    for n in WEIGHTS:
        shape = w[n].shape
        flat = (-1, shape[-1])
        dl, m2, v2 = _adamw(w[n].reshape(flat), final[n].reshape(flat), mom[n].reshape(flat), var[n].reshape(flat))
        deltas[n], new_m[n], new_v[n] = dl.reshape(shape), m2.reshape(shape), v2.reshape(shape)

    grad_x = dcur[None]
    return (loss, grad_x, *[final[n] for n in WEIGHTS], *[deltas[n] for n in WEIGHTS],
            *[new_m[n] for n in WEIGHTS], *[new_v[n] for n in WEIGHTS])
```

```python
import functools

import jax
import jax.numpy as jnp
import numpy as np
from jax import lax
from jax.experimental import pallas as pl
from jax.experimental.pallas import tpu as pltpu
from jax.experimental.pallas import tpu_sc as plsc

F32 = jnp.float32
BF16 = jnp.bfloat16

LANES = 128
HEAD_DIM = 128
DN_CHUNK = 64
DN_CONV = 4
CONV_ROWS = 8
SB_BLOCK = 128
FFN_HALF = 0.5
RMS_EPS = 1e-6
L2_EPS = 1e-6
NEG_BIG = -1e30
ADAM_LR = 0.001
ADAM_B1 = 0.9
ADAM_B2 = 0.999
ADAM_EPS = 1e-08
ADAM_WD = 0.01
ADAM_STEP = 10
VMEM_LIMIT = 56 * 1024 * 1024
N_CHIPS = 4
N_DEV = 8
MESH = pl.DeviceIdType.MESH

BIG = ("ffn1_w_in", "ffn1_w_out", "w_in", "w_branch_a", "w_branch_b", "w_out", "ffn2_w_in", "ffn2_w_out")
COL_SHARDED = ("ffn1_w_in", "w_in", "ffn2_w_in")
SMALL = ("ffn1_norm", "mix_norm", "dn_a_log", "dn_dt_bias", "dn_out_norm", "sb_q_norm", "sb_k_norm", "ffn2_norm")
WEIGHTS = ("ffn1_norm", "ffn1_w_in", "ffn1_w_out", "mix_norm", "w_in", "dn_conv_w", "dn_a_log", "dn_dt_bias",
           "dn_out_norm", "sb_q_norm", "sb_k_norm", "w_branch_a", "w_branch_b", "w_out", "ffn2_norm", "ffn2_w_in",
           "ffn2_w_out")


def _params(**kw):
    return pltpu.CompilerParams(vmem_limit_bytes=VMEM_LIMIT, **kw)


def _pick(n, options):
    for o in options:
        if n % o == 0:
            return o
    return n


def _const_spec(shape, single=False):
    nd = len(shape)
    if single:
        return pl.BlockSpec(shape, lambda *_: (0,) * nd, pipeline_mode=pl.Buffered(1))
    return pl.BlockSpec(shape, lambda *_: (0,) * nd)


_NN = ((1,), (0,))
_NT = ((1,), (1,))
_TN = ((0,), (0,))


def _dot(a, b, dims):
    return lax.dot_general(a.astype(BF16), b.astype(BF16), (dims, ((), ())), preferred_element_type=F32)


def _mm_nn(a, b):
    return _dot(a, b, _NN)


def _mm_nt(a, b):
    return _dot(a, b, _NT)


def _mm_tn(a, b):
    return _dot(a, b, _TN)


def _split(a):
    hi = a.astype(BF16)
    lo = (a - hi.astype(F32)).astype(BF16)
    return hi, lo


def _dot_precise(a, b, dims):
    dn = (dims, ((), ()))
    ah, al = _split(a)
    bh, bl = _split(b)
    out = lax.dot_general(ah, bh, dn, preferred_element_type=F32)
    out = out + lax.dot_general(ah, bl, dn, preferred_element_type=F32)
    return out + lax.dot_general(al, bh, dn, preferred_element_type=F32)


def _make_diff_mm(dot):
    @jax.custom_vjp
    def nn(a, b):
        return dot(a, b, _NN)

    @jax.custom_vjp
    def nt(a, b):
        return dot(a, b, _NT)

    @jax.custom_vjp
    def tn(a, b):
        return dot(a, b, _TN)

    nn.defvjp(lambda a, b: (dot(a, b, _NN), (a, b)), lambda r, g: (nt(g, r[1]), tn(r[0], g)))
    nt.defvjp(lambda a, b: (dot(a, b, _NT), (a, b)), lambda r, g: (nn(g, r[1]), tn(g, r[0])))
    tn.defvjp(lambda a, b: (dot(a, b, _TN), (a, b)), lambda r, g: (nt(r[1], g), nn(r[0], g)))
    return nn, nt, tn


_d_nn, _d_nt, _d_tn = _make_diff_mm(_dot)
_p_nn, _p_nt, _p_tn = _make_diff_mm(_dot_precise)


@jax.custom_vjp
def _halves(x):
    n = x.shape[0] // 2
    return x[:n], x[n:]


_halves.defvjp(lambda x: (_halves(x), None), lambda _, g: (jnp.concatenate(g, axis=0),))


@jax.custom_vjp
def _col_halves(x):
    n = x.shape[1] // 2
    return x[:, :n], x[:, n:]


_col_halves.defvjp(lambda x: (_col_halves(x), None), lambda _, g: (jnp.concatenate(g, axis=1),))


def _softplus_raw(x):
    return jnp.maximum(x, 0.0) + jnp.log(1.0 + jnp.exp(-jnp.abs(x)))


@jax.custom_vjp
def _softplus(x):
    return _softplus_raw(x)


_softplus.defvjp(lambda x: (_softplus_raw(x), x), lambda x, g: (g * jax.nn.sigmoid(x),))


def _rms(x, gain, eps):
    return x * lax.rsqrt(jnp.mean(x * x, axis=-1, keepdims=True) + eps) * gain


def _silu(x):
    return x * jax.nn.sigmoid(x)


def _shift_rows_raw(x, k, down):
    n = x.shape[0]
    row = lax.broadcasted_iota(jnp.int32, x.shape, 0)
    if down:
        return jnp.where(row >= k, pltpu.roll(x, k, 0), 0.0)
    return jnp.where(row < n - k, pltpu.roll(x, n - k, 0), 0.0)


@functools.partial(jax.custom_vjp, nondiff_argnums=(1,))
def _shift_down(x, k):
    return _shift_rows_raw(x, k, True)


_shift_down.defvjp(lambda x, k: (_shift_rows_raw(x, k, True), None),
                   lambda k, _, g: (_shift_rows_raw(g, k, False),))


def _layer_spec(layer, block, index_map, single=False):
    full_map = lambda *a: (layer,) + tuple(index_map(*a))
    if single:
        return pl.BlockSpec((None,) + block, full_map, pipeline_mode=pl.Buffered(1))
    return pl.BlockSpec((None,) + block, full_map)


def _ffn_fwd(x, gain, w_in, w_out, layer):
    t, d = x.shape
    f = w_out.shape[1]
    fc = _pick(f, (256, 128))
    nj = f // fc
    rt = _pick(t, (512, 256, 128))

    def body(x_ref, g_ref, wg_ref, wu_ref, wo_ref, o_ref, hs_ref):
        @pl.when(pl.program_id(0) == 0)
        def _():
            for r in range(t // rt):
                rows = pl.ds(r * rt, rt)
                xr = x_ref[rows, :]
                hs_ref[rows, :] = _rms(xr, g_ref[...], RMS_EPS).astype(BF16)
                o_ref[rows, :] = xr

        for r in range(t // rt):
            rows = pl.ds(r * rt, rt)
            h = hs_ref[rows, :]
            a = _mm_nn(h, wg_ref[...])
            b = _mm_nn(h, wu_ref[...])
            o_ref[rows, :] += FFN_HALF * _mm_nn(_silu(a) * b, wo_ref[...])

    return pl.pallas_call(
        body, name="ffn_fwd", grid=(nj,),
        in_specs=[_const_spec((t, d), True), _const_spec((1, d)),
                  _layer_spec(layer, (d, fc), lambda j: (0, j)), _layer_spec(layer, (d, fc), lambda j: (0, nj + j)),
                  _layer_spec(layer, (fc, d), lambda j: (j, 0))],
        out_specs=_const_spec((t, d)),
        out_shape=jax.ShapeDtypeStruct((t, d), F32),
        scratch_shapes=[pltpu.VMEM((t, d), BF16)],
        compiler_params=_params(dimension_semantics=("arbitrary",)),
    )(x, gain, w_in, w_in, w_out)


def _ffn_bwd(x, gain, dy, w_in, w_out, layer):
    t, d = x.shape
    f = w_out.shape[1]
    fc = _pick(f, (256, 128))
    nj = f // fc
    rt = _pick(t, (512, 256, 128))
    nr = t // rt

    def body(x_ref, g_ref, dy_ref, wg_ref, wu_ref, wo_ref, dx_ref, dg_ref, dwi_ref, dwo_ref,
             hs_ref, dwg_acc, dwu_acc, dwo_acc):
        j = pl.program_id(0)

        @pl.when(j == 0)
        def _():
            for r in range(nr):
                rows = pl.ds(r * rt, rt)
                hs_ref[rows, :] = _rms(x_ref[rows, :], g_ref[...], RMS_EPS).astype(BF16)
                dx_ref[rows, :] = jnp.zeros((rt, d), F32)

        for r in range(nr):
            rows = pl.ds(r * rt, rt)
            h = hs_ref[rows, :]
            dy2 = (FFN_HALF * dy_ref[rows, :]).astype(BF16)
            a = _mm_nn(h, wg_ref[...])
            b = _mm_nn(h, wu_ref[...])
            sig = jax.nn.sigmoid(a)
            sa = a * sig
            ds = _mm_nt(dy2, wo_ref[...])
            da = ds * b * (sig * (1.0 + a * (1.0 - sig)))
            db = ds * sa
            dx_ref[rows, :] += _mm_nt(da, wg_ref[...]) + _mm_nt(db, wu_ref[...])
            dwo_c = _mm_tn(sa * b, dy2)
            dwg_c = _mm_tn(h, da)
            dwu_c = _mm_tn(h, db)
            if r == 0:
                dwo_acc[...] = dwo_c
                dwg_acc[...] = dwg_c
                dwu_acc[...] = dwu_c
            else:
                dwo_acc[...] += dwo_c
                dwg_acc[...] += dwg_c
                dwu_acc[...] += dwu_c
        dwi_ref[0] = dwg_acc[...].astype(BF16)
        dwi_ref[1] = dwu_acc[...].astype(BF16)
        dwo_ref[...] = dwo_acc[...].astype(BF16)

        @pl.when(j == nj - 1)
        def _():
            for r in range(nr):
                rows = pl.ds(r * rt, rt)
                _, vjp = jax.vjp(lambda xx, gg: _rms(xx, gg, RMS_EPS), x_ref[rows, :], g_ref[...])
                dxn, dgr = vjp(dx_ref[rows, :])
                dx_ref[rows, :] = dy_ref[rows, :] + dxn
                if r == 0:
                    dg_ref[...] = dgr
                else:
                    dg_ref[...] += dgr

    return pl.pallas_call(
        body, name="ffn_bwd", grid=(nj,),
        in_specs=[_const_spec((t, d), True), _const_spec((1, d)), _const_spec((t, d), True),
                  _layer_spec(layer, (d, fc), lambda j: (0, j)), _layer_spec(layer, (d, fc), lambda j: (0, nj + j)),
                  _layer_spec(layer, (fc, d), lambda j: (j, 0))],
        out_specs=[_const_spec((t, d)), _const_spec((1, d)),
                   pl.BlockSpec((2, d, fc), lambda j: (0, 0, j)), pl.BlockSpec((fc, d), lambda j: (j, 0))],
        out_shape=[jax.ShapeDtypeStruct((t, d), F32), jax.ShapeDtypeStruct((1, d), F32),
                   jax.ShapeDtypeStruct((2, d, f), BF16), jax.ShapeDtypeStruct((f, d), BF16)],
        scratch_shapes=[pltpu.VMEM((t, d), BF16), pltpu.VMEM((d, fc), F32), pltpu.VMEM((d, fc), F32),
                        pltpu.VMEM((fc, d), F32)],
        compiler_params=_params(dimension_semantics=("arbitrary",)),
    )(x, gain, dy, w_in, w_in, w_out)


def _proj_fwd(x, gain, w, wba, layer):
    t, d = x.shape
    n = w.shape[2]
    nc = _pick(n, (512, 256, 128))
    rt = _pick(t, (512, 256, 128))

    def body(x_ref, g_ref, w_ref, wba_ref, p_ref, ba_ref, hs_ref):
        @pl.when(pl.program_id(0) == 0)
        def _():
            for r in range(t // rt):
                rows = pl.ds(r * rt, rt)
                h = _rms(x_ref[rows, :], g_ref[...], RMS_EPS).astype(BF16)
                hs_ref[rows, :] = h
                ba_ref[rows, :] = _mm_nn(h, wba_ref[...])

        for r in range(t // rt):
            rows = pl.ds(r * rt, rt)
            p_ref[rows, :] = _mm_nn(hs_ref[rows, :], w_ref[...])

    return pl.pallas_call(
        body, name="proj_fwd", grid=(n // nc,),
        in_specs=[_const_spec((t, d), True), _const_spec((1, d)),
                  _layer_spec(layer, (d, nc), lambda j: (0, j)), _layer_spec(layer, (d, LANES), lambda j: (0, 0))],
        out_specs=[pl.BlockSpec((t, nc), lambda j: (0, j)), _const_spec((t, LANES))],
        out_shape=[jax.ShapeDtypeStruct((t, n), F32), jax.ShapeDtypeStruct((t, LANES), F32)],
        scratch_shapes=[pltpu.VMEM((t, d), BF16)],
        compiler_params=_params(dimension_semantics=("arbitrary",)),
    )(x, gain, w, wba)


def _proj_bwd(x, gain, dres, dp, dba, w, wba, layer):
    t, d = x.shape
    n = w.shape[2]
    nc = _pick(n, (512, 256, 128))
    nj = n // nc
    rt = _pick(t, (512, 256, 128))
    nr = t // rt

    def body(x_ref, g_ref, dres_ref, dp_ref, dba_ref, w_ref, wba_ref, dx_ref, dg_ref, dw_ref, dwba_ref, hs_ref, dw_acc):
        j = pl.program_id(0)

        @pl.when(j == 0)
        def _():
            for r in range(nr):
                rows = pl.ds(r * rt, rt)
                h = _rms(x_ref[rows, :], g_ref[...], RMS_EPS).astype(BF16)
                hs_ref[rows, :] = h
                g = dba_ref[rows, :]
                dx_ref[rows, :] = _mm_nt(g, wba_ref[...])
                if r == 0:
                    dwba_ref[...] = _mm_tn(h, g)
                else:
                    dwba_ref[...] += _mm_tn(h, g)

        for r in range(nr):
            rows = pl.ds(r * rt, rt)
            g = dp_ref[rows, :]
            dx_ref[rows, :] += _mm_nt(g, w_ref[...])
            if r == 0:
                dw_acc[...] = _mm_tn(hs_ref[rows, :], g)
            else:
                dw_acc[...] += _mm_tn(hs_ref[rows, :], g)
        dw_ref[...] = dw_acc[...].astype(BF16)

        @pl.when(j == nj - 1)
        def _():
            for r in range(nr):
                rows = pl.ds(r * rt, rt)
                _, vjp = jax.vjp(lambda xx, gg: _rms(xx, gg, RMS_EPS), x_ref[rows, :], g_ref[...])
                dxn, dgr = vjp(dx_ref[rows, :])
                dx_ref[rows, :] = dres_ref[rows, :] + dxn
                if r == 0:
                    dg_ref[...] = dgr
                else:
                    dg_ref[...] += dgr

    return pl.pallas_call(
        body, name="proj_bwd", grid=(nj,),
        in_specs=[_const_spec((t, d), True), _const_spec((1, d)), _const_spec((t, d), True),
                  pl.BlockSpec((t, nc), lambda j: (0, j)), _const_spec((t, LANES)),
                  _layer_spec(layer, (d, nc), lambda j: (0, j)), _layer_spec(layer, (d, LANES), lambda j: (0, 0))],
        out_specs=[_const_spec((t, d)), _const_spec((1, d)),
                   pl.BlockSpec((d, nc), lambda j: (0, j)), _const_spec((d, LANES))],
        out_shape=[jax.ShapeDtypeStruct((t, d), F32), jax.ShapeDtypeStruct((1, d), F32),
                   jax.ShapeDtypeStruct((d, n), BF16), jax.ShapeDtypeStruct((d, LANES), F32)],
        scratch_shapes=[pltpu.VMEM((t, d), BF16), pltpu.VMEM((d, nc), F32)],
        compiler_params=_params(dimension_semantics=("arbitrary",)),
    )(x, gain, dres, dp, dba, w, wba)


def _conv_act(x, w0, w1, w2, w3, is_qk):
    y = w3 * x + w2 * _shift_down(x, 1) + w1 * _shift_down(x, 2) + w0 * _shift_down(x, 3)
    y = _silu(y)
    inv = lax.rsqrt(jnp.sum(y * y, axis=-1, keepdims=True) + L2_EPS)
    return y * (is_qk * inv + (1.0 - is_qk))


def _taps(w_ref):
    return tuple(w_ref[i:i + 1, :] for i in range(DN_CONV))


def _dn_prep_fwd(proj, conv_w, n_heads):
    t = proj.shape[0]
    nb = 3 * n_heads

    def body(x_ref, w_ref, o_ref):
        is_qk = jnp.where(pl.program_id(0) < 2 * n_heads, 1.0, 0.0).astype(F32)
        o_ref[...] = _conv_act(x_ref[...], *_taps(w_ref), is_qk)

    return pl.pallas_call(
        body, name="dn_prep_fwd", grid=(nb,),
        in_specs=[pl.BlockSpec((t, HEAD_DIM), lambda i: (0, i)), pl.BlockSpec((CONV_ROWS, HEAD_DIM), lambda i: (0, i))],
        out_specs=pl.BlockSpec((t, HEAD_DIM), lambda i: (0, i)),
        out_shape=jax.ShapeDtypeStruct((t, nb * HEAD_DIM), F32),
        compiler_params=_params(dimension_semantics=("arbitrary",)),
    )(proj, conv_w)


def _dn_prep_bwd(proj, conv_w, dact, n_heads):
    t = proj.shape[0]
    nb = 3 * n_heads

    def body(x_ref, w_ref, g_ref, dx_ref, dw_ref):
        is_qk = jnp.where(pl.program_id(0) < 2 * n_heads, 1.0, 0.0).astype(F32)
        _, vjp = jax.vjp(lambda x, a, b, c, e: _conv_act(x, a, b, c, e, is_qk), x_ref[...], *_taps(w_ref))
        dx, d0, d1, d2, d3 = vjp(g_ref[...])
        dx_ref[...] = dx.astype(BF16)
        dw_ref[...] = jnp.concatenate([d0, d1, d2, d3, jnp.zeros((CONV_ROWS - DN_CONV, HEAD_DIM), F32)], axis=0)

    return pl.pallas_call(
        body, name="dn_prep_bwd", grid=(nb,),
        in_specs=[pl.BlockSpec((t, HEAD_DIM), lambda i: (0, i)), pl.BlockSpec((CONV_ROWS, HEAD_DIM), lambda i: (0, i)),
                  pl.BlockSpec((t, HEAD_DIM), lambda i: (0, i))],
        out_specs=[pl.BlockSpec((t, HEAD_DIM), lambda i: (0, i)), pl.BlockSpec((CONV_ROWS, HEAD_DIM), lambda i: (0, i))],
        out_shape=[jax.ShapeDtypeStruct((t, nb * HEAD_DIM), BF16), jax.ShapeDtypeStruct((CONV_ROWS, nb * HEAD_DIM), F32)],
        compiler_params=_params(dimension_semantics=("arbitrary",)),
    )(proj, conv_w, dact)


def _unit_lower_inverses(lmats, c):
    r = lax.broadcasted_iota(jnp.int32, (c, c), 0)
    q = lax.broadcasted_iota(jnp.int32, (c, c), 1)
    eye = jnp.where(r == q, 1.0, 0.0)
    ps = [eye - l for l in lmats]
    ms = [_p_nn(l, l) for l in lmats]
    n = 2
    while 2 * n < c:
        both = [_halves(_p_nn(jnp.concatenate([p, m], axis=0), m)) for p, m in zip(ps, ms)]
        ps = [p + pm for p, (pm, _) in zip(ps, both)]
        ms = [mm for _, mm in both]
        n *= 2
    return [p + _p_nn(p, m) for p, m in zip(ps, ms)]


def _delta_heads(qs, ks, vs, bg, alog, dtb, states):
    n_heads = len(qs)
    heads = range(n_heads)
    c = qs[0].shape[0]
    lane = lax.broadcasted_iota(jnp.int32, (c, LANES), 1)
    r = lax.broadcasted_iota(jnp.int32, (c, c), 0)
    s = lax.broadcasted_iota(jnp.int32, (c, c), 1)
    beta_all = jax.nn.sigmoid(bg)
    g_all = -jnp.exp(alog) * _softplus(bg + dtb)
    beta = [jnp.sum(jnp.where(lane == h, beta_all, 0.0), axis=1, keepdims=True) for h in heads]
    g = [jnp.sum(jnp.where(lane == n_heads + h, g_all, 0.0), axis=1, keepdims=True) for h in heads]
    g_row = [jnp.sum(jnp.where(r == s, g[h], 0.0), axis=0, keepdims=True) for h in heads]
    gc = [jnp.sum(jnp.where(s <= r, g_row[h], 0.0), axis=1, keepdims=True) for h in heads]
    gr = [jnp.sum(jnp.where(r <= s, g[h], 0.0), axis=0, keepdims=True) for h in heads]
    g_last = [jnp.sum(g[h], axis=0, keepdims=True) for h in heads]
    decay = [jnp.exp(jnp.where(r >= s, gc[h] - gr[h], NEG_BIG)) for h in heads]
    q_scaled = [qs[h] * (HEAD_DIM ** -0.5) for h in heads]
    k_beta = [ks[h] * beta[h] for h in heads]
    kk_qk = [_halves(_d_nt(jnp.concatenate([k_beta[h], q_scaled[h]], axis=0), ks[h])) for h in heads]
    lmat = [jnp.where(r > s, kk_qk[h][0] * decay[h], 0.0) for h in heads]
    attn = [kk_qk[h][1] * decay[h] for h in heads]
    tinv = _unit_lower_inverses(lmat, c)
    u_w = [_col_halves(_p_nn(tinv[h], jnp.concatenate([vs[h] * beta[h], k_beta[h] * jnp.exp(gc[h])], axis=1)))
           for h in heads]
    u = [u_w[h][0] for h in heads]
    w = [u_w[h][1] for h in heads]
    ws_qs = [_halves(_d_nn(jnp.concatenate([w[h], q_scaled[h] * jnp.exp(gc[h])], axis=0), states[h])) for h in heads]
    v_new = [u[h] - ws_qs[h][0] for h in heads]
    o = [ws_qs[h][1] + _d_nn(attn[h], v_new[h]) for h in heads]
    kv = [_d_tn(ks[h] * jnp.exp(g_last[h] - gc[h]), v_new[h]) for h in heads]
    new_states = [states[h] * jnp.exp(g_last[h]) + kv[h] for h in heads]
    return tuple(o), tuple(new_states)


def _delta_fwd(act, ba, alog, dtb, n_heads):
    t = act.shape[0]
    d = n_heads * HEAD_DIM
    c = DN_CHUNK
    nc = t // c

    def body(q_ref, k_ref, v_ref, bg_ref, al_ref, dt_ref, o_ref, snap_ref, st_ref):
        @pl.when(pl.program_id(0) == 0)
        def _():
            st_ref[...] = jnp.zeros(st_ref.shape, F32)

        snap_ref[0] = st_ref[...]
        cols = [slice(h * HEAD_DIM, (h + 1) * HEAD_DIM) for h in range(n_heads)]
        os, new_states = _delta_heads([q_ref[:, sl] for sl in cols], [k_ref[:, sl] for sl in cols],
                                      [v_ref[:, sl] for sl in cols], bg_ref[...], al_ref[...], dt_ref[...],
                                      [st_ref[h] for h in range(n_heads)])
        for h, sl in enumerate(cols):
            o_ref[:, sl] = os[h]
            st_ref[h] = new_states[h]

    return pl.pallas_call(
        body, name="delta_fwd", grid=(nc,),
        in_specs=[pl.BlockSpec((c, d), lambda i: (i, 0)), pl.BlockSpec((c, d), lambda i: (i, 1)),
                  pl.BlockSpec((c, d), lambda i: (i, 2)), pl.BlockSpec((c, LANES), lambda i: (i, 0)),
                  _const_spec((1, LANES)), _const_spec((1, LANES))],
        out_specs=[pl.BlockSpec((c, d), lambda i: (i, 0)),
                   pl.BlockSpec((1, n_heads, HEAD_DIM, HEAD_DIM), lambda i: (i, 0, 0, 0))],
        out_shape=[jax.ShapeDtypeStruct((t, d), F32), jax.ShapeDtypeStruct((nc, n_heads, HEAD_DIM, HEAD_DIM), F32)],
        scratch_shapes=[pltpu.VMEM((n_heads, HEAD_DIM, HEAD_DIM), F32)],
        compiler_params=_params(dimension_semantics=("arbitrary",)),
    )(act, act, act, ba, alog, dtb)


def _delta_bwd(act, ba, alog, dtb, snaps, do, n_heads):
    t = act.shape[0]
    d = n_heads * HEAD_DIM
    c = DN_CHUNK
    nc = t // c

    def body(q_ref, k_ref, v_ref, bg_ref, al_ref, dt_ref, snap_ref, do_ref,
             dact_ref, dbg_ref, dal_ref, ddt_ref, ds_ref):
        @pl.when(pl.program_id(0) == 0)
        def _():
            ds_ref[...] = jnp.zeros(ds_ref.shape, F32)
            dal_ref[...] = jnp.zeros((1, LANES), F32)
            ddt_ref[...] = jnp.zeros((1, LANES), F32)

        heads = range(n_heads)
        cols = [slice(h * HEAD_DIM, (h + 1) * HEAD_DIM) for h in heads]
        _, vjp = jax.vjp(_delta_heads, tuple(q_ref[:, sl] for sl in cols), tuple(k_ref[:, sl] for sl in cols),
                         tuple(v_ref[:, sl] for sl in cols), bg_ref[...], al_ref[...], dt_ref[...],
                         tuple(snap_ref[0, h] for h in heads))
        dq, dk, dv, dbg, dal, ddt, dst = vjp((tuple(do_ref[:, sl] for sl in cols), tuple(ds_ref[h] for h in heads)))
        for h, sl in enumerate(cols):
            dact_ref[:, sl] = dq[h]
            dact_ref[:, d + h * HEAD_DIM:d + (h + 1) * HEAD_DIM] = dk[h]
            dact_ref[:, 2 * d + h * HEAD_DIM:2 * d + (h + 1) * HEAD_DIM] = dv[h]
            ds_ref[h] = dst[h]
        dal_ref[...] += dal
        ddt_ref[...] += ddt
        dbg_ref[...] = dbg.astype(BF16)

    rev = lambda i: nc - 1 - i
    return pl.pallas_call(
        body, name="delta_bwd", grid=(nc,),
        in_specs=[pl.BlockSpec((c, d), lambda i: (rev(i), 0)), pl.BlockSpec((c, d), lambda i: (rev(i), 1)),
                  pl.BlockSpec((c, d), lambda i: (rev(i), 2)), pl.BlockSpec((c, LANES), lambda i: (rev(i), 0)),
                  _const_spec((1, LANES)), _const_spec((1, LANES)),
                  pl.BlockSpec((1, n_heads, HEAD_DIM, HEAD_DIM), lambda i: (rev(i), 0, 0, 0)),
                  pl.BlockSpec((c, d), lambda i: (rev(i), 0))],
        out_specs=[pl.BlockSpec((c, 3 * d), lambda i: (rev(i), 0)), pl.BlockSpec((c, LANES), lambda i: (rev(i), 0)),
                   _const_spec((1, LANES)), _const_spec((1, LANES))],
        out_shape=[jax.ShapeDtypeStruct((t, 3 * d), F32), jax.ShapeDtypeStruct((t, LANES), BF16),
                   jax.ShapeDtypeStruct((1, LANES), F32), jax.ShapeDtypeStruct((1, LANES), F32)],
        scratch_shapes=[pltpu.VMEM((n_heads, HEAD_DIM, HEAD_DIM), F32)],
        compiler_params=_params(dimension_semantics=("arbitrary",)),
    )(act, act, act, ba, alog, dtb, snaps, do)


def _head_norm2(a, b, ga, gb):
    return _rms(a, ga, RMS_EPS), _rms(b, gb, RMS_EPS)


def _sb_prep_fwd(proj, qn, kn, n_heads):
    t = proj.shape[0]
    d = n_heads * HEAD_DIM
    tm = _pick(t, (256, 128))

    def body(q_ref, k_ref, qn_ref, kn_ref, qo_ref, ko_ref):
        for h in range(n_heads):
            sl = slice(h * HEAD_DIM, (h + 1) * HEAD_DIM)
            qo_ref[:, sl], ko_ref[:, sl] = _head_norm2(q_ref[:, sl], k_ref[:, sl], qn_ref[...], kn_ref[...])

    return pl.pallas_call(
        body, name="sb_prep_fwd", grid=(t // tm,),
        in_specs=[pl.BlockSpec((tm, d), lambda i: (i, 4)), pl.BlockSpec((tm, d), lambda i: (i, 5)),
                  _const_spec((1, HEAD_DIM)), _const_spec((1, HEAD_DIM))],
        out_specs=[pl.BlockSpec((tm, d), lambda i: (i, 0)), pl.BlockSpec((tm, d), lambda i: (i, 0))],
        out_shape=[jax.ShapeDtypeStruct((t, d), F32), jax.ShapeDtypeStruct((t, d), F32)],
        compiler_params=_params(dimension_semantics=("arbitrary",)),
    )(proj, proj, qn, kn)


def _sb_prep_bwd(proj, qn, kn, dq, dk, n_heads):
    t = proj.shape[0]
    d = n_heads * HEAD_DIM
    tm = _pick(t, (256, 128))

    def body(q_ref, k_ref, qn_ref, kn_ref, dq_ref, dk_ref, dqo_ref, dko_ref, dqn_ref, dkn_ref):
        @pl.when(pl.program_id(0) == 0)
        def _():
            dqn_ref[...] = jnp.zeros((1, HEAD_DIM), F32)
            dkn_ref[...] = jnp.zeros((1, HEAD_DIM), F32)

        for h in range(n_heads):
            sl = slice(h * HEAD_DIM, (h + 1) * HEAD_DIM)
            _, vjp = jax.vjp(_head_norm2, q_ref[:, sl], k_ref[:, sl], qn_ref[...], kn_ref[...])
            da, db, dga, dgb = vjp((dq_ref[:, sl], dk_ref[:, sl]))
            dqo_ref[:, sl] = da.astype(BF16)
            dko_ref[:, sl] = db.astype(BF16)
            dqn_ref[...] += dga
            dkn_ref[...] += dgb

    return pl.pallas_call(
        body, name="sb_prep_bwd", grid=(t // tm,),
        in_specs=[pl.BlockSpec((tm, d), lambda i: (i, 4)), pl.BlockSpec((tm, d), lambda i: (i, 5)),
                  _const_spec((1, HEAD_DIM)), _const_spec((1, HEAD_DIM)),
                  pl.BlockSpec((tm, d), lambda i: (i, 0)), pl.BlockSpec((tm, d), lambda i: (i, 0))],
        out_specs=[pl.BlockSpec((tm, d), lambda i: (i, 0)), pl.BlockSpec((tm, d), lambda i: (i, 0)),
                   _const_spec((1, HEAD_DIM)), _const_spec((1, HEAD_DIM))],
        out_shape=[jax.ShapeDtypeStruct((t, d), BF16), jax.ShapeDtypeStruct((t, d), BF16),
                   jax.ShapeDtypeStruct((1, HEAD_DIM), F32), jax.ShapeDtypeStruct((1, HEAD_DIM), F32)],
        compiler_params=_params(dimension_semantics=("arbitrary",)),
    )(proj, proj, qn, kn, dq, dk)


def _cumsum_mm(x, tri):
    hi, lo = _split(x)
    return (lax.dot_general(hi, tri, (_NN, ((), ())), preferred_element_type=F32)
            + lax.dot_general(lo, tri, (_NN, ((), ())), preferred_element_type=F32))


def _sb_valid(i, j):
    row = lax.broadcasted_iota(jnp.int32, (SB_BLOCK, SB_BLOCK), 0)
    col = lax.broadcasted_iota(jnp.int32, (SB_BLOCK, SB_BLOCK), 1)
    return (col + j * SB_BLOCK) < (row + i * SB_BLOCK)


def _sb_attn_fwd(qb, kb, proj, n_heads):
    t = qb.shape[0]
    d = n_heads * HEAD_DIM
    nq = t // SB_BLOCK
    hb = _pick(n_heads, (4, 2, 1))
    wide = hb * HEAD_DIM
    v_col0 = 6 * n_heads // hb
    scale = HEAD_DIM ** -0.5

    def body(q_ref, k_ref, v_ref, o_ref, lt_ref):
        i = pl.program_id(1)
        row = lax.broadcasted_iota(jnp.int32, (SB_BLOCK, SB_BLOCK), 0)
        col = lax.broadcasted_iota(jnp.int32, (SB_BLOCK, SB_BLOCK), 1)
        after = jnp.where(row > col, 1.0, 0.0).astype(BF16)
        heads = [slice(h * HEAD_DIM, (h + 1) * HEAD_DIM) for h in range(hb)]
        every = range(hb)
        qs = [q_ref[:, sl].astype(BF16) for sl in heads]

        def step(m, carry):
            j0 = i - 2 * m
            js = (j0, jnp.maximum(j0 - 1, 0))
            valid = (_sb_valid(i, js[0]), jnp.logical_and(_sb_valid(i, js[1]), j0 >= 1))
            rows = [pl.ds(pl.multiple_of(j * SB_BLOCK, SB_BLOCK), SB_BLOCK) for j in js]
            units = [(h, b) for b in range(2) for h in every]
            z = {u: _mm_nt(qs[u[0]], k_ref[rows[u[1]], heads[u[0]]]) * scale for u in units}
            sp = {u: _softplus_raw(z[u]) for u in units}
            lm = {u: jnp.where(valid[u[1]], -sp[u], 0.0) for u in units}
            tail = {u: _cumsum_mm(lm[u], after) for u in units}
            later = {(h, 0): carry[h][1] for h in every}
            later.update({(h, 1): carry[h][1] + jnp.sum(lm[h, 0], axis=1, keepdims=True) for h in every})
            w = {u: jnp.where(valid[u[1]], jnp.exp(z[u] - sp[u] + later[u] + tail[u]), 0.0) for u in units}
            pv = {u: _mm_nn(w[u], v_ref[rows[u[1]], heads[u[0]]]) for u in units}
            return tuple((carry[h][0] + pv[h, 0] + pv[h, 1], later[h, 1] + jnp.sum(lm[h, 1], axis=1, keepdims=True))
                         for h in every)

        init = tuple((jnp.zeros((SB_BLOCK, HEAD_DIM), F32), jnp.zeros((SB_BLOCK, 1), F32)) for _ in heads)
        res = lax.fori_loop(0, (i + 2) // 2, step, init)
        for h, sl in enumerate(heads):
            o_ref[:, sl] = res[h][0]
            lt_ref[:, sl] = jnp.broadcast_to(res[h][1], (SB_BLOCK, HEAD_DIM))

    return pl.pallas_call(
        body, name="sb_attn_fwd", grid=(n_heads // hb, nq),
        in_specs=[pl.BlockSpec((SB_BLOCK, wide), lambda g, i: (i, g)),
                  pl.BlockSpec((t, wide), lambda g, i: (0, g)),
                  pl.BlockSpec((t, wide), lambda g, i: (0, v_col0 + g))],
        out_specs=[pl.BlockSpec((SB_BLOCK, wide), lambda g, i: (i, g)),
                   pl.BlockSpec((SB_BLOCK, wide), lambda g, i: (i, g))],
        out_shape=[jax.ShapeDtypeStruct((t, d), F32), jax.ShapeDtypeStruct((t, d), F32)],
        compiler_params=_params(dimension_semantics=("arbitrary", "arbitrary")),
    )(qb, kb, proj)


def _sb_attn_bwd(qb, kb, proj, ltot, do, n_heads):
    t = qb.shape[0]
    d = n_heads * HEAD_DIM
    nq = t // SB_BLOCK
    hb = _pick(n_heads, (4, 2, 1))
    wide = hb * HEAD_DIM
    v_col0 = 6 * n_heads // hb
    scale = HEAD_DIM ** -0.5

    def body(q_ref, k_ref, v_ref, lt_ref, do_ref, dq_ref, dk_ref, dv_ref):
        i = pl.program_id(1)

        @pl.when(i == 0)
        def _():
            dk_ref[...] = jnp.zeros((t, wide), F32)
            dv_ref[...] = jnp.zeros((t, wide), F32)

        row = lax.broadcasted_iota(jnp.int32, (SB_BLOCK, SB_BLOCK), 0)
        col = lax.broadcasted_iota(jnp.int32, (SB_BLOCK, SB_BLOCK), 1)
        upto = jnp.where(row <= col, 1.0, 0.0).astype(BF16)
        before = jnp.where(row < col, 1.0, 0.0).astype(BF16)
        heads = [slice(h * HEAD_DIM, (h + 1) * HEAD_DIM) for h in range(hb)]
        every = range(hb)
        qs = [q_ref[:, sl].astype(BF16) for sl in heads]
        dos = [do_ref[:, sl].astype(BF16) for sl in heads]
        totals = [jnp.max(lt_ref[:, sl], axis=1, keepdims=True) for sl in heads]

        def step(m, carry):
            js = (2 * m, jnp.minimum(2 * m + 1, nq - 1))
            valid = (_sb_valid(i, js[0]), jnp.logical_and(_sb_valid(i, js[1]), 2 * m + 1 <= i))
            rows = [pl.ds(pl.multiple_of(j * SB_BLOCK, SB_BLOCK), SB_BLOCK) for j in js]
            units = [(h, b) for b in range(2) for h in every]
            kj = {u: k_ref[rows[u[1]], heads[u[0]]].astype(BF16) for u in units}
            vj = {u: v_ref[rows[u[1]], heads[u[0]]].astype(BF16) for u in units}
            z = {u: _mm_nt(qs[u[0]], kj[u]) * scale for u in units}
            dw = {u: _mm_nt(dos[u[0]], vj[u]) for u in units}
            sp = {u: _softplus_raw(z[u]) for u in units}
            lm = {u: jnp.where(valid[u[1]], -sp[u], 0.0) for u in units}
            head = {u: _cumsum_mm(lm[u], upto) for u in units}
            lm_before = {(h, 0): carry[h][1] for h in every}
            lm_before.update({(h, 1): carry[h][1] + jnp.sum(lm[h, 0], axis=1, keepdims=True) for h in every})
            w = {u: jnp.where(valid[u[1]], jnp.exp(z[u] - sp[u] + totals[u[0]] - (lm_before[u] + head[u])), 0.0)
                 for u in units}
            e = {u: w[u] * dw[u] for u in units}
            e_local = {u: _mm_nn(e[u], before) for u in units}
            e_before = {(h, 0): carry[h][2] for h in every}
            e_before.update({(h, 1): carry[h][2] + jnp.sum(e[h, 0], axis=1, keepdims=True) for h in every})
            sig = {u: jnp.exp(z[u] - sp[u]) for u in units}
            dz = {u: jnp.where(valid[u[1]], e[u] * (1.0 - sig[u]) - (e_before[u] + e_local[u]) * sig[u], 0.0) * scale
                  for u in units}
            for h, b in units:
                dv_ref[rows[b], heads[h]] += _mm_tn(w[h, b], dos[h])
            for h, b in units:
                dk_ref[rows[b], heads[h]] += _mm_tn(dz[h, b], qs[h])
            dq = {u: _mm_nn(dz[u], kj[u]) for u in units}
            return tuple((carry[h][0] + dq[h, 0] + dq[h, 1],
                          lm_before[h, 1] + jnp.sum(lm[h, 1], axis=1, keepdims=True),
                          e_before[h, 1] + jnp.sum(e[h, 1], axis=1, keepdims=True)) for h in every)

        zero_col = jnp.zeros((SB_BLOCK, 1), F32)
        init = tuple((jnp.zeros((SB_BLOCK, HEAD_DIM), F32), zero_col, zero_col) for _ in heads)
        res = lax.fori_loop(0, (i + 2) // 2, step, init)
        for h, sl in enumerate(heads):
            dq_ref[:, sl] = res[h][0]

    return pl.pallas_call(
        body, name="sb_attn_bwd", grid=(n_heads // hb, nq),
        in_specs=[pl.BlockSpec((SB_BLOCK, wide), lambda g, i: (i, g)),
                  pl.BlockSpec((t, wide), lambda g, i: (0, g)),
                  pl.BlockSpec((t, wide), lambda g, i: (0, v_col0 + g)),
                  pl.BlockSpec((SB_BLOCK, wide), lambda g, i: (i, g)),
                  pl.BlockSpec((SB_BLOCK, wide), lambda g, i: (i, g))],
        out_specs=[pl.BlockSpec((SB_BLOCK, wide), lambda g, i: (i, g)),
                   pl.BlockSpec((t, wide), lambda g, i: (0, g)),
                   pl.BlockSpec((t, wide), lambda g, i: (0, g))],
        out_shape=[jax.ShapeDtypeStruct((t, d), F32), jax.ShapeDtypeStruct((t, d), F32),
                   jax.ShapeDtypeStruct((t, d), F32)],
        compiler_params=_params(dimension_semantics=("arbitrary", "arbitrary")),
    )(qb, kb, proj, ltot, do)


def _gated_norm(oa, z, gn):
    return _rms(oa, gn, RMS_EPS) * _silu(z)


def _merge_gates(ya, yb, ga, gb):
    return jax.nn.sigmoid(ga) * ya + jax.nn.sigmoid(gb) * yb


def _merge_fwd(x1, oa, proj, ob, gn, wa, wb, wo, layer, n_heads):
    t, d = x1.shape
    tm = _pick(t, (256, 128))
    square = _layer_spec(layer, (d, d), lambda i: (0, 0), single=True)

    def body(x_ref, oa_ref, z_ref, ob_ref, ga_ref, gb_ref, gn_ref, wa_ref, wb_ref, wo_ref, o_ref, na_ref):
        for h in range(n_heads):
            sl = slice(h * HEAD_DIM, (h + 1) * HEAD_DIM)
            na_ref[:, sl] = _gated_norm(oa_ref[:, sl], z_ref[:, sl], gn_ref[...]).astype(BF16)
        m = _merge_gates(_mm_nn(na_ref[...], wa_ref[...]), _mm_nn(ob_ref[...], wb_ref[...]), ga_ref[...], gb_ref[...])
        o_ref[...] = x_ref[...] + _mm_nn(m, wo_ref[...])

    tile = lambda k: pl.BlockSpec((tm, d), lambda i: (i, k))
    return pl.pallas_call(
        body, name="merge_fwd", grid=(t // tm,),
        in_specs=[tile(0), tile(0), tile(3), tile(0), tile(7), tile(8), _const_spec((1, HEAD_DIM)),
                  square, square, square],
        out_specs=tile(0),
        out_shape=jax.ShapeDtypeStruct((t, d), F32),
        scratch_shapes=[pltpu.VMEM((tm, d), BF16)],
        compiler_params=_params(dimension_semantics=("arbitrary",)),
    )(x1, oa, proj, ob, proj, proj, gn, wa, wb, wo)


def _merge_bwd(oa, proj, ob, dy, gn, wa, wb, wo, layer, n_heads):
    t, d = oa.shape
    tm = _pick(t, (256, 128))
    nt = t // tm
    square = _layer_spec(layer, (d, d), lambda i: (0, 0), single=True)

    def body(oa_ref, z_ref, ob_ref, ga_ref, gb_ref, dy_ref, gn_ref, wa_ref, wb_ref, wo_ref,
             doa_ref, dz_ref, dob_ref, dga_ref, dgb_ref, dgn_ref, dwa_hbm, dwb_hbm, dwo_hbm,
             na_ref, dna_ref, dwa_ref, dwb_ref, dwo_ref, stage_ref):
        i = pl.program_id(0)

        @pl.when(i == 0)
        def _():
            dgn_ref[...] = jnp.zeros((1, HEAD_DIM), F32)
            dwa_ref[...] = jnp.zeros((d, d), F32)
            dwb_ref[...] = jnp.zeros((d, d), F32)
            dwo_ref[...] = jnp.zeros((d, d), F32)

        for h in range(n_heads):
            sl = slice(h * HEAD_DIM, (h + 1) * HEAD_DIM)
            na_ref[:, sl] = _gated_norm(oa_ref[:, sl], z_ref[:, sl], gn_ref[...]).astype(BF16)
        dy = dy_ref[...].astype(BF16)
        ob = ob_ref[...].astype(BF16)
        ya = _mm_nn(na_ref[...], wa_ref[...])
        yb = _mm_nn(ob, wb_ref[...])
        m, vjp = jax.vjp(_merge_gates, ya, yb, ga_ref[...], gb_ref[...])
        dwo_ref[...] += _mm_tn(m, dy)
        dya, dyb, dga, dgb = vjp(_mm_nt(dy, wo_ref[...]))
        dga_ref[...] = dga.astype(BF16)
        dgb_ref[...] = dgb.astype(BF16)
        dwa_ref[...] += _mm_tn(na_ref[...], dya)
        dwb_ref[...] += _mm_tn(ob, dyb)
        dob_ref[...] = _mm_nt(dyb, wb_ref[...])
        dna_ref[...] = _mm_nt(dya, wa_ref[...])
        for h in range(n_heads):
            sl = slice(h * HEAD_DIM, (h + 1) * HEAD_DIM)
            _, vjp_h = jax.vjp(_gated_norm, oa_ref[:, sl], z_ref[:, sl], gn_ref[...])
            doa, dz, dgn = vjp_h(dna_ref[:, sl])
            doa_ref[:, sl] = doa
            dz_ref[:, sl] = dz.astype(BF16)
            dgn_ref[...] += dgn

        @pl.when(i == nt - 1)
        def _():
            for acc, out in ((dwa_ref, dwa_hbm), (dwb_ref, dwb_hbm), (dwo_ref, dwo_hbm)):
                stage_ref[...] = acc[...].astype(BF16)
                pltpu.sync_copy(stage_ref, out)

    tile = lambda k: pl.BlockSpec((tm, d), lambda i: (i, k))
    any_spec = pl.BlockSpec(memory_space=pl.ANY)
    return pl.pallas_call(
        body, name="merge_bwd", grid=(nt,),
        in_specs=[tile(0), tile(3), tile(0), tile(7), tile(8), tile(0), _const_spec((1, HEAD_DIM)),
                  square, square, square],
        out_specs=[tile(0), tile(0), tile(0), tile(0), tile(0), _const_spec((1, HEAD_DIM)),
                   any_spec, any_spec, any_spec],
        out_shape=[jax.ShapeDtypeStruct((t, d), F32), jax.ShapeDtypeStruct((t, d), BF16),
                   jax.ShapeDtypeStruct((t, d), F32), jax.ShapeDtypeStruct((t, d), BF16),
                   jax.ShapeDtypeStruct((t, d), BF16), jax.ShapeDtypeStruct((1, HEAD_DIM), F32),
                   jax.ShapeDtypeStruct((d, d), BF16), jax.ShapeDtypeStruct((d, d), BF16),
                   jax.ShapeDtypeStruct((d, d), BF16)],
        scratch_shapes=[pltpu.VMEM((tm, d), BF16), pltpu.VMEM((tm, d), F32),
                        pltpu.VMEM((d, d), F32), pltpu.VMEM((d, d), F32), pltpu.VMEM((d, d), F32),
                        pltpu.VMEM((d, d), BF16)],
        compiler_params=_params(dimension_semantics=("arbitrary",)),
    )(oa, proj, ob, proj, proj, dy, gn, wa, wb, wo)


def _loss_head(y, target):
    t, d = y.shape
    tm = _pick(t, (256, 128))

    def body(y_ref, t_ref, dy_ref, loss_ref):
        @pl.when(pl.program_id(0) == 0)
        def _():
            loss_ref[...] = jnp.zeros((8, LANES), F32)

        err = y_ref[...] - t_ref[...]
        dy_ref[...] = err * (1.0 / d)
        per_token = jnp.sum(err * err, axis=1, keepdims=True) * (1.0 / d)
        loss_ref[...] += 0.5 * jnp.sum(per_token, axis=0, keepdims=True)

    return pl.pallas_call(
        body, name="loss_head", grid=(t // tm,),
        in_specs=[pl.BlockSpec((tm, d), lambda i: (i, 0)), pl.BlockSpec((tm, d), lambda i: (i, 0))],
        out_specs=[pl.BlockSpec((tm, d), lambda i: (i, 0)), _const_spec((8, LANES))],
        out_shape=[jax.ShapeDtypeStruct((t, d), F32), jax.ShapeDtypeStruct((8, LANES), F32)],
        compiler_params=_params(dimension_semantics=("arbitrary",)),
    )(y, target)


def _adamw(w, g, m, v):
    rows, cols = w.shape
    tr = rows
    for cand in (512, 256, 128, 64, 32, 16, 8):
        if rows % cand == 0 and cand * cols * 4 <= 2 * 1024 * 1024:
            tr = cand
            break

    def body(w_ref, g_ref, m_ref, v_ref, d_ref, mo_ref, vo_ref):
        g = g_ref[...]
        m2 = ADAM_B1 * m_ref[...] + (1.0 - ADAM_B1) * g
        v2 = ADAM_B2 * v_ref[...] + (1.0 - ADAM_B2) * (g * g)
        m_hat = m2 / (1.0 - ADAM_B1 ** ADAM_STEP)
        v_hat = v2 / (1.0 - ADAM_B2 ** ADAM_STEP)
        d_ref[...] = -ADAM_LR * (m_hat / (jnp.sqrt(v_hat) + ADAM_EPS) + ADAM_WD * w_ref[...])
        mo_ref[...] = m2
        vo_ref[...] = v2

    spec = pl.BlockSpec((tr, cols), lambda i: (i, 0))
    shape = jax.ShapeDtypeStruct((rows, cols), F32)
    return pl.pallas_call(
        body, name="adamw", grid=(rows // tr,), in_specs=[spec] * 4, out_specs=[spec] * 3,
        out_shape=[shape] * 3, compiler_params=_params(dimension_semantics=("arbitrary",)),
    )(w, g, m, v)


def _place():
    return lax.axis_index("x"), lax.axis_index("y"), lax.axis_index("c")


def _other_chips(x, y):
    return [(1 - x, y), (x, 1 - y), (1 - x, 1 - y)]


def _tile_rows(rows, cols, itemsize, cap=1536 * 1024):
    best = None
    for cand in range(16, rows + 1, 16):
        if rows % cand == 0 and cand * cols * itemsize <= cap:
            best = cand
    return best if best is not None else rows


def _allgather_layer(shards, layer, collective_id):
    n = len(shards)

    def body(*refs):
        srcs, outs = refs[:n], refs[n:2 * n]
        send_sems, recv_sems = refs[2 * n:]
        x, y, c = _place()
        me, sibling = (x, y, c), (x, y, 1 - c)
        chips = _other_chips(x, y)
        barrier = pltpu.get_barrier_semaphore()
        for peer in [(*chip, c) for chip in chips] + [sibling]:
            pl.semaphore_signal(barrier, inc=1, device_id=peer, device_id_type=MESH)
        pl.semaphore_wait(barrier, N_CHIPS)

        def half(w, which):
            rows = shards[w].shape[1] // 2
            return pl.ds(which * rows, rows)

        def copy(w, k, shard, which, to, from_src=False):
            part = half(w, which)
            return pltpu.make_async_remote_copy(
                src_ref=srcs[w].at[layer, part] if from_src else outs[w].at[shard, part],
                dst_ref=outs[w].at[shard, part], send_sem=send_sems.at[6 * w + k], recv_sem=recv_sems.at[6 * w + k],
                device_id=to, device_id_type=MESH)

        first = [copy(w, j, 2 * x + y, c, (*chip, c), from_src=True) for j, chip in enumerate(chips) for w in range(n)]
        for cp in first:
            cp.start()
        passed = []
        for j, (cx, cy) in enumerate(chips):
            for w in range(n):
                copy(w, j, 2 * cx + cy, c, me).wait_recv()
                cp = copy(w, 3 + j, 2 * cx + cy, c, sibling)
                cp.start()
                passed.append(cp)
        for j, (cx, cy) in enumerate(chips):
            for w in range(n):
                copy(w, 3 + j, 2 * cx + cy, 1 - c, me).wait_recv()
        for cp in first + passed:
            cp.wait_send()

    return pl.kernel(
        body, name=f"allgather_layer{layer}_id{collective_id}",
        out_type=[jax.ShapeDtypeStruct((N_CHIPS,) + s.shape[1:], s.dtype) for s in shards],
        mesh=plsc.ScalarSubcoreMesh(axis_name="sequencer", num_cores=1),
        scratch_types=[pltpu.SemaphoreType.DMA((6 * n,)), pltpu.SemaphoreType.DMA((6 * n,))],
        compiler_params=pltpu.CompilerParams(collective_id=collective_id),
    )(*shards)


def _swap_halves(grads):
    n = len(grads)

    def body(*refs):
        gs, gots = refs[:n], refs[n:2 * n]
        send_sems, recv_sems = refs[2 * n:]
        x, y, c = _place()
        copies = []
        for w in range(n):
            half = grads[w].shape[1] // 2
            copies.append(pltpu.make_async_remote_copy(
                src_ref=gs[w].at[:, pl.ds((1 - c) * half, half)], dst_ref=gots[w], send_sem=send_sems.at[w],
                recv_sem=recv_sems.at[w], device_id=(x, y, 1 - c), device_id_type=MESH))
        for cp in copies:
            cp.start()
        for cp in copies:
            cp.wait()

    hbm = pl.BlockSpec(memory_space=pl.ANY)
    return pl.pallas_call(
        body, name="swap_halves", in_specs=[hbm] * n, out_specs=[hbm] * n,
        out_shape=[jax.ShapeDtypeStruct((g.shape[0], g.shape[1] // 2, g.shape[2]), g.dtype) for g in grads],
        scratch_shapes=[pltpu.SemaphoreType.DMA((n,)), pltpu.SemaphoreType.DMA((n,))],
    )(*grads)


def _add_half(grad, got, c_idx):
    n, rows, all_cols = grad.shape
    side = N_CHIPS // n
    cols = all_cols // side
    half = rows // 2
    tr = _tile_rows(half, cols, 2)
    nb = half // tr

    def body(c_ref, a_ref, b_ref, o_ref):
        o_ref[...] = (a_ref[...].astype(F32) + b_ref[...].astype(F32)).astype(o_ref.dtype)

    return pl.pallas_call(
        body, name="add_half",
        grid_spec=pltpu.PrefetchScalarGridSpec(
            num_scalar_prefetch=1, grid=(N_CHIPS, nb),
            in_specs=[pl.BlockSpec((1, tr, cols), lambda s, r, c_ref: (s // side, c_ref[0] * nb + r, s % side)),
                      pl.BlockSpec((1, tr, cols), lambda s, r, c_ref: (s // side, r, s % side))],
            out_specs=pl.BlockSpec((1, tr, cols), lambda s, r, c_ref: (s, r, 0))),
        out_shape=jax.ShapeDtypeStruct((N_CHIPS, half, cols), grad.dtype),
        compiler_params=_params(dimension_semantics=("arbitrary", "arbitrary")),
    )(c_idx, grad, got)


def _scatter_partials(parts, layer, collective_id):
    n = len(parts)

    def body(*refs):
        ps, gots = refs[:n], refs[n:2 * n]
        send_sems, recv_sems = refs[2 * n:]
        x, y, c = _place()
        chips = _other_chips(x, y)
        barrier = pltpu.get_barrier_semaphore()
        for chip in chips:
            pl.semaphore_signal(barrier, inc=1, device_id=(*chip, c), device_id_type=MESH)
        pl.semaphore_wait(barrier, N_CHIPS - 1)
        copies = [pltpu.make_async_remote_copy(src_ref=ps[w].at[2 * cx + cy], dst_ref=gots[w].at[j],
                                               send_sem=send_sems.at[3 * w + j], recv_sem=recv_sems.at[3 * w + j],
                                               device_id=(cx, cy, c), device_id_type=MESH)
                  for j, (cx, cy) in enumerate(chips) for w in range(n)]
        for cp in copies:
            cp.start()
        for cp in copies:
            cp.wait()

    return pl.kernel(
        body, name=f"scatter_partials{layer}_id{collective_id}",
        out_type=[jax.ShapeDtypeStruct((N_CHIPS - 1,) + p.shape[1:], p.dtype) for p in parts],
        mesh=plsc.ScalarSubcoreMesh(axis_name="sequencer", num_cores=1),
        scratch_types=[pltpu.SemaphoreType.DMA((3 * n,)), pltpu.SemaphoreType.DMA((3 * n,))],
        compiler_params=pltpu.CompilerParams(collective_id=collective_id),
    )(*parts)


def _sum_partials(part, got, s_idx, c_idx, layer, depth, stacked=None):
    n, half, cols = part.shape
    tr = _tile_rows(half, cols, 2, cap=1024 * 1024)
    nb = half // tr

    def body(s_ref, c_ref, a_ref, b_ref, *rest):
        o_ref = rest[-1]
        acc = a_ref[0].astype(F32)
        for j in range(n - 1):
            acc = acc + b_ref[j].astype(F32)
        o_ref[...] = acc

    in_specs = [pl.BlockSpec((1, tr, cols), lambda r, s_ref, c_ref: (s_ref[0], r, 0)),
                pl.BlockSpec((n - 1, tr, cols), lambda r, s_ref, c_ref: (0, r, 0))]
    operands = [s_idx, c_idx, part, got]
    aliases = {}
    if stacked is not None:
        in_specs.append(pl.BlockSpec(memory_space=pl.ANY))
        operands.append(stacked)
        aliases = {len(operands) - 1: 0}
    return pl.pallas_call(
        body, name="sum_partials",
        grid_spec=pltpu.PrefetchScalarGridSpec(
            num_scalar_prefetch=2, grid=(nb,), in_specs=in_specs,
            out_specs=pl.BlockSpec((None, tr, cols), lambda r, s_ref, c_ref: (layer, c_ref[0] * nb + r, 0))),
        out_shape=jax.ShapeDtypeStruct((depth, 2 * half, cols), F32),
        input_output_aliases=aliases,
        compiler_params=_params(dimension_semantics=("arbitrary",)),
    )(*operands)


def _join_halves(bufs, layer):
    n = len(bufs)

    def body(*refs):
        outs = refs[n:2 * n]
        send_sems, recv_sems = refs[2 * n:]
        x, y, c = _place()
        copies = []
        for w in range(n):
            half = bufs[w].shape[1] // 2
            mine = outs[w].at[layer, pl.ds(c * half, half)]
            copies.append(pltpu.make_async_remote_copy(src_ref=mine, dst_ref=mine, send_sem=send_sems.at[w],
                                                       recv_sem=recv_sems.at[w], device_id=(x, y, 1 - c),
                                                       device_id_type=MESH))
        for cp in copies:
            cp.start()
        for cp in copies:
            cp.wait()

    hbm = pl.BlockSpec(memory_space=pl.ANY)
    return pl.pallas_call(
        body, name="join_halves", in_specs=[hbm] * n, out_specs=[hbm] * n,
        out_shape=[jax.ShapeDtypeStruct(b.shape, b.dtype) for b in bufs],
        input_output_aliases={w: w for w in range(n)},
        scratch_shapes=[pltpu.SemaphoreType.DMA((n,)), pltpu.SemaphoreType.DMA((n,))],
    )(*bufs)


def _allreduce_small(v, name):
    rows = v.shape[0]

    def body(v_ref, o_ref, gath, send_sems, recv_sems):
        x, y, c = _place()
        idx = 4 * x + 2 * y + c
        gath[0] = v_ref[...]
        copies = []
        for r in range(1, N_DEV):
            peer = (1 - x if r & 4 else x, 1 - y if r & 2 else y, 1 - c if r & 1 else c)
            cp = pltpu.make_async_remote_copy(src_ref=v_ref, dst_ref=gath.at[r], send_sem=send_sems.at[r - 1],
                                              recv_sem=recv_sems.at[r - 1], device_id=peer, device_id_type=MESH)
            cp.start()
            copies.append(cp)
        for cp in copies:
            cp.wait()
        acc = gath[idx]
        for a in range(1, N_DEV):
            acc = acc + gath[lax.bitwise_xor(idx, a)]
        o_ref[...] = acc

    vmem = pl.BlockSpec(memory_space=pltpu.VMEM)
    return pl.pallas_call(
        body, name=name, in_specs=[vmem], out_specs=vmem,
        out_shape=jax.ShapeDtypeStruct((rows, LANES), F32),
        scratch_shapes=[pltpu.VMEM((N_DEV, rows, LANES), F32), pltpu.SemaphoreType.DMA((N_DEV - 1,)),
                        pltpu.SemaphoreType.DMA((N_DEV - 1,))],
    )(v)


def _join_shards(name, gathered):
    if name in COL_SHARDED:
        return jnp.concatenate([gathered[s] for s in range(N_CHIPS)], axis=1)[None]
    return gathered.reshape(1, N_CHIPS * gathered.shape[1], gathered.shape[2])


def _row_shards(g):
    return g.reshape(N_CHIPS, g.shape[0] // N_CHIPS, g.shape[1])


def _mixer_runs(width, cut, n_small):
    runs = []
    for s in range(N_CHIPS):
        lo, hi = s * width, (s + 1) * width
        spans = ((True, lo, min(hi, cut)), (False, max(lo, cut), min(hi, cut + n_small)),
                 (True, max(lo, cut + n_small), hi))
        runs.append([(is_main, a - lo, b - lo) for is_main, a, b in spans if a < b])
    return runs


def _split_mixer_weight(gathered, runs):
    main = [gathered[s][:, a:b] for s, parts in enumerate(runs) for is_main, a, b in parts if is_main]
    small = [gathered[s][:, a:b] for s, parts in enumerate(runs) for is_main, a, b in parts if not is_main]
    small = small[0] if len(small) == 1 else jnp.concatenate(small, axis=1)
    return jnp.concatenate(main, axis=1)[None], jnp.pad(small, ((0, 0), (0, LANES - small.shape[1])))[None]


def _join_mixer_grad(d_main, d_small, runs):
    shards, m, k = [], 0, 0
    for parts in runs:
        cols = []
        for is_main, a, b in parts:
            if is_main:
                cols.append(d_main[:, m:m + b - a])
                m += b - a
            else:
                cols.append(d_small[:, k:k + b - a].astype(d_main.dtype))
                k += b - a
        shards.append(cols[0] if len(cols) == 1 else jnp.concatenate(cols, axis=1))
    return jnp.stack(shards)


def _pad_small(flat):
    n = flat.shape[0]
    block = 8 * LANES
    padded = -(-n // block) * block
    return jnp.pad(flat, (0, padded - n)).reshape(padded // LANES, LANES)


def kernel(x, ffn1_norm, ffn1_w_in, ffn1_w_out, mix_norm, w_in, dn_conv_w, dn_a_log, dn_dt_bias, dn_out_norm, sb_q_norm, sb_k_norm, w_branch_a, w_branch_b, w_out, ffn2_norm, ffn2_w_in, ffn2_w_out, loss_target, m_ffn1_norm, m_ffn1_w_in, m_ffn1_w_out, m_mix_norm, m_w_in, m_dn_conv_w, m_dn_a_log, m_dn_dt_bias, m_dn_out_norm, m_sb_q_norm, m_sb_k_norm, m_w_branch_a, m_w_branch_b, m_w_out, m_ffn2_norm, m_ffn2_w_in, m_ffn2_w_out, v_ffn1_norm, v_ffn1_w_in, v_ffn1_w_out, v_mix_norm, v_w_in, v_dn_conv_w, v_dn_a_log, v_dn_dt_bias, v_dn_out_norm, v_sb_q_norm, v_sb_k_norm, v_w_branch_a, v_w_branch_b, v_w_out, v_ffn2_norm, v_ffn2_w_in, v_ffn2_w_out):
    w = dict(ffn1_norm=ffn1_norm, ffn1_w_in=ffn1_w_in, ffn1_w_out=ffn1_w_out, mix_norm=mix_norm, w_in=w_in,
             dn_conv_w=dn_conv_w, dn_a_log=dn_a_log, dn_dt_bias=dn_dt_bias, dn_out_norm=dn_out_norm,
             sb_q_norm=sb_q_norm, sb_k_norm=sb_k_norm, w_branch_a=w_branch_a, w_branch_b=w_branch_b, w_out=w_out,
             ffn2_norm=ffn2_norm, ffn2_w_in=ffn2_w_in, ffn2_w_out=ffn2_w_out)
    mom = dict(ffn1_norm=m_ffn1_norm, ffn1_w_in=m_ffn1_w_in, ffn1_w_out=m_ffn1_w_out, mix_norm=m_mix_norm, w_in=m_w_in,
               dn_conv_w=m_dn_conv_w, dn_a_log=m_dn_a_log, dn_dt_bias=m_dn_dt_bias, dn_out_norm=m_dn_out_norm,
               sb_q_norm=m_sb_q_norm, sb_k_norm=m_sb_k_norm, w_branch_a=m_w_branch_a, w_branch_b=m_w_branch_b,
               w_out=m_w_out, ffn2_norm=m_ffn2_norm, ffn2_w_in=m_ffn2_w_in, ffn2_w_out=m_ffn2_w_out)
    var = dict(ffn1_norm=v_ffn1_norm, ffn1_w_in=v_ffn1_w_in, ffn1_w_out=v_ffn1_w_out, mix_norm=v_mix_norm, w_in=v_w_in,
               dn_conv_w=v_dn_conv_w, dn_a_log=v_dn_a_log, dn_dt_bias=v_dn_dt_bias, dn_out_norm=v_dn_out_norm,
               sb_q_norm=v_sb_q_norm, sb_k_norm=v_sb_k_norm, w_branch_a=v_w_branch_a, w_branch_b=v_w_branch_b,
               w_out=v_w_out, ffn2_norm=v_ffn2_norm, ffn2_w_in=v_ffn2_w_in, ffn2_w_out=v_ffn2_w_out)

    _, t, d = x.shape
    depth = ffn1_norm.shape[0]
    n_heads = d // HEAD_DIM
    conv_cols = dn_conv_w.shape[2]
    assert d % HEAD_DIM == 0 and t % SB_BLOCK == 0 and 2 * n_heads <= LANES and depth % 2 == 0
    assert w_in.shape[2] * N_CHIPS == 9 * d + 2 * n_heads and conv_cols * N_CHIPS == 3 * d

    x_idx, y_idx, c_idx = _place()
    shard = 2 * x_idx + y_idx
    c_arr = jnp.reshape(c_idx, (1,)).astype(jnp.int32)
    s_arr = jnp.reshape(shard, (1,)).astype(jnp.int32)

    mine = {n: w[n].astype(BF16) for n in BIG}
    runs = _mixer_runs(w_in.shape[2], 4 * d, 2 * n_heads)
    groups = (("ffn1_w_in", "ffn1_w_out"), ("w_in",), ("w_branch_a", "w_branch_b", "w_out"),
              ("ffn2_w_in", "ffn2_w_out"))
    first = groups[0]
    rest = tuple(n for g in groups[1:] for n in g)
    n_gathers = depth + len(groups) - 1

    def gather(names, l, collective_id):
        return dict(zip(names, _allgather_layer([mine[n] for n in names], l, collective_id)))

    arriving = [{}]
    for k, names in enumerate(groups):
        arriving[0].update(gather(names, 0, 0 if k == 0 else depth + k - 1))
    arriving += [gather(BIG, l, l) for l in range(1, depth)]

    def layer_weights(l, names, after):
        gathered, after = lax.optimization_barrier(({n: arriving[l][n] for n in names}, after))
        full = {}
        for n in names:
            g = lax.dynamic_update_slice(gathered[n], mine[n][l][None], (shard, 0, 0))
            if n == "w_in":
                full["w_main"], full["w_ba"] = _split_mixer_weight(g, runs)
            else:
                full[n] = _join_shards(n, g)
        return full, after

    conv_place = lax.dynamic_update_slice(jnp.zeros((depth, DN_CONV, 3 * d), F32), dn_conv_w, (0, 0, shard * conv_cols))
    conv_rows = _pad_small(conv_place.reshape(-1))
    conv_full = (0.5 * _allreduce_small(conv_rows, "allgather_conv")).reshape(-1)[:depth * DN_CONV * 3 * d]
    conv_full = jnp.pad(conv_full.reshape(depth, DN_CONV, 3 * d), ((0, 0), (0, CONV_ROWS - DN_CONV), (0, 0)))

    def head_row(vals):
        return jnp.pad(vals, (n_heads, LANES - 2 * n_heads)).reshape(1, LANES)

    saved, layers = [], []
    cur = x[0]
    for l in range(depth):
        full, x0 = layer_weights(l, groups[0], cur)
        layers.append(full)
        x1 = _ffn_fwd(x0, ffn1_norm[l][None], full["ffn1_w_in"], full["ffn1_w_out"], 0)
        later, x1 = layer_weights(l, groups[1], x1)
        full.update(later)
        proj, ba = _proj_fwd(x1, mix_norm[l][None], full["w_main"], full["w_ba"], 0)
        act = _dn_prep_fwd(proj, conv_full[l], n_heads)
        alog, dtb = head_row(dn_a_log[l]), head_row(dn_dt_bias[l])
        oa, snaps = _delta_fwd(act, ba, alog, dtb, n_heads)
        qb, kb = _sb_prep_fwd(proj, sb_q_norm[l][None], sb_k_norm[l][None], n_heads)
        ob, ltot = _sb_attn_fwd(qb, kb, proj, n_heads)
        later, (oa, ob) = layer_weights(l, groups[2], (oa, ob))
        full.update(later)
        x2 = _merge_fwd(x1, oa, proj, ob, dn_out_norm[l][None], full["w_branch_a"], full["w_branch_b"],
                        full["w_out"], 0, n_heads)
        later, x2 = layer_weights(l, groups[3], x2)
        full.update(later)
        cur = _ffn_fwd(x2, ffn2_norm[l][None], full["ffn2_w_in"], full["ffn2_w_out"], 0)
        saved.append((x0, x1, proj, ba, act, alog, dtb, oa, snaps, qb, kb, ob, ltot, x2))

    dcur, loss_part = _loss_head(cur, loss_target[0])

    grads = {n: [None] * depth for n in WEIGHTS}
    reduced = {n: None for n in BIG}
    in_flight = []

    def start_reduce(l, names, collective_id):
        g_major = [grads[n][l] for n in names]
        parts = [_add_half(g, got, c_arr) for g, got in zip(g_major, _swap_halves(g_major))]
        return l, names, parts, _scatter_partials(parts, l, collective_id)

    def finish_reduce(started, after):
        for l, names, parts, arrived in started:
            arrived, after = lax.optimization_barrier((arrived, after))
            halves = [_sum_partials(p, got, s_arr, c_arr, l, depth, stacked=reduced[n])
                      for n, p, got in zip(names, parts, arrived)]
            reduced.update(zip(names, _join_halves(halves, l)))
        return after

    for l in reversed(range(depth)):
        x0, x1, proj, ba, act, alog, dtb, oa, snaps, qb, kb, ob, ltot, x2 = saved[l]
        full = layers[l]
        dx2, dg, dwi, dwo = _ffn_bwd(x2, ffn2_norm[l][None], dcur, full["ffn2_w_in"], full["ffn2_w_out"], 0)
        grads["ffn2_norm"][l] = dg[0]
        grads["ffn2_w_in"][l] = dwi
        grads["ffn2_w_out"][l] = _row_shards(dwo)
        doa, dz, dob, dga, dgb, dgn, dwa, dwb, dwout = _merge_bwd(
            oa, proj, ob, dx2, dn_out_norm[l][None], full["w_branch_a"], full["w_branch_b"], full["w_out"], 0,
            n_heads)
        grads["dn_out_norm"][l] = dgn[0]
        grads["w_branch_a"][l], grads["w_branch_b"][l] = _row_shards(dwa), _row_shards(dwb)
        grads["w_out"][l] = _row_shards(dwout)
        dqb, dkb, dvb = _sb_attn_bwd(qb, kb, proj, ltot, dob, n_heads)
        dsq, dsk, dqn, dkn = _sb_prep_bwd(proj, sb_q_norm[l][None], sb_k_norm[l][None], dqb, dkb, n_heads)
        grads["sb_q_norm"][l], grads["sb_k_norm"][l] = dqn[0], dkn[0]
        dact, dba, dal, ddt = _delta_bwd(act, ba, alog, dtb, snaps, doa, n_heads)
        grads["dn_a_log"][l] = dal[0, n_heads:2 * n_heads]
        grads["dn_dt_bias"][l] = ddt[0, n_heads:2 * n_heads]
        dqkv, dconv = _dn_prep_bwd(proj, conv_full[l], dact, n_heads)
        grads["dn_conv_w"][l] = dconv[:DN_CONV]
        dproj = jnp.concatenate([dqkv, dz, dsq, dsk, dvb.astype(BF16), dga, dgb], axis=1)
        dx1, dg, dwm, dwba = _proj_bwd(x1, mix_norm[l][None], dx2, dproj, dba, full["w_main"], full["w_ba"], 0)
        grads["mix_norm"][l] = dg[0]
        grads["w_in"][l] = _join_mixer_grad(dwm, dwba, runs)
        early = start_reduce(l, rest, collective_id=n_gathers + l)
        dcur, dg, dwi, dwo = _ffn_bwd(x0, ffn1_norm[l][None], dx1, full["ffn1_w_in"], full["ffn1_w_out"], 0)
        grads["ffn1_norm"][l] = dg[0]
        grads["ffn1_w_in"][l] = dwi
        grads["ffn1_w_out"][l] = _row_shards(dwo)

        dcur = finish_reduce(in_flight, dcur)
        in_flight = [early, start_reduce(l, first, collective_id=n_gathers + depth + l)]
    dcur = finish_reduce(in_flight, dcur)
    final = reduced
    grads = {n: jnp.stack(grads[n]) for n in SMALL + ("dn_conv_w",)}

    small_names = SMALL + ("dn_conv_w",)
    small_sizes = [int(np.prod(grads[n].shape)) for n in small_names]
    small_off = np.concatenate([[0], np.cumsum(small_sizes)])
    small = jnp.concatenate([grads[n].reshape(-1) for n in small_names] + [loss_part[0, :1]])
    small_sum = _allreduce_small(_pad_small(small), "allreduce_small").reshape(-1)
    for i, n in enumerate(small_names):
        final[n] = small_sum[small_off[i]:small_off[i + 1]].reshape(grads[n].shape)
    final["dn_conv_w"] = lax.dynamic_slice(final["dn_conv_w"], (0, 0, shard * conv_cols), (depth, DN_CONV, conv_cols))
    loss = small_sum[small_off[-1]]

    deltas, new_m, new_v = {}, {}, {}
    for n in WEIGHTS:
        shape = w[n].shape
        flat = (-1, shape[-1])
        dl, m2, v2 = _adamw(w[n].reshape(flat), final[n].reshape(flat), mom[n].reshape(flat), var[n].reshape(flat))
        deltas[n], new_m[n], new_v[n] = dl.reshape(shape), m2.reshape(shape), v2.reshape(shape)

    grad_x = dcur[None]
    return (loss, grad_x, *[final[n] for n in WEIGHTS], *[deltas[n] for n in WEIGHTS],
            *[new_m[n] for n in WEIGHTS], *[new_v[n] for n in WEIGHTS])
```

```python
import functools

import jax
import jax.numpy as jnp
import numpy as np
from jax import lax
from jax.experimental import pallas as pl
from jax.experimental.pallas import tpu as pltpu
from jax.experimental.pallas import tpu_sc as plsc

F32 = jnp.float32
BF16 = jnp.bfloat16

LANES = 128
HEAD_DIM = 128
DN_CHUNK = 64
DN_CONV = 4
CONV_ROWS = 8
SB_BLOCK = 128
FFN_HALF = 0.5
RMS_EPS = 1e-6
L2_EPS = 1e-6
NEG_BIG = -1e30
ADAM_LR = 0.001
ADAM_B1 = 0.9
ADAM_B2 = 0.999
ADAM_EPS = 1e-08
ADAM_WD = 0.01
ADAM_STEP = 10
VMEM_LIMIT = 56 * 1024 * 1024
N_CHIPS = 4
N_DEV = 8
MESH = pl.DeviceIdType.MESH

BIG = ("ffn1_w_in", "ffn1_w_out", "w_in", "w_branch_a", "w_branch_b", "w_out", "ffn2_w_in", "ffn2_w_out")
COL_SHARDED = ("ffn1_w_in", "w_in", "ffn2_w_in")
SMALL = ("ffn1_norm", "mix_norm", "dn_a_log", "dn_dt_bias", "dn_out_norm", "sb_q_norm", "sb_k_norm", "ffn2_norm")
WEIGHTS = ("ffn1_norm", "ffn1_w_in", "ffn1_w_out", "mix_norm", "w_in", "dn_conv_w", "dn_a_log", "dn_dt_bias",
           "dn_out_norm", "sb_q_norm", "sb_k_norm", "w_branch_a", "w_branch_b", "w_out", "ffn2_norm", "ffn2_w_in",
           "ffn2_w_out")


def _params(**kw):
    return pltpu.CompilerParams(vmem_limit_bytes=VMEM_LIMIT, **kw)


def _pick(n, options):
    for o in options:
        if n % o == 0:
            return o
    return n


def _const_spec(shape, single=False):
    nd = len(shape)
    if single:
        return pl.BlockSpec(shape, lambda *_: (0,) * nd, pipeline_mode=pl.Buffered(1))
    return pl.BlockSpec(shape, lambda *_: (0,) * nd)


_NN = ((1,), (0,))
_NT = ((1,), (1,))
_TN = ((0,), (0,))


def _dot(a, b, dims):
    return lax.dot_general(a.astype(BF16), b.astype(BF16), (dims, ((), ())), preferred_element_type=F32)


def _mm_nn(a, b):
    return _dot(a, b, _NN)


def _mm_nt(a, b):
    return _dot(a, b, _NT)


def _mm_tn(a, b):
    return _dot(a, b, _TN)


def _split(a):
    hi = a.astype(BF16)
    lo = (a - hi.astype(F32)).astype(BF16)
    return hi, lo


def _dot_precise(a, b, dims):
    dn = (dims, ((), ()))
    ah, al = _split(a)
    bh, bl = _split(b)
    out = lax.dot_general(ah, bh, dn, preferred_element_type=F32)
    out = out + lax.dot_general(ah, bl, dn, preferred_element_type=F32)
    return out + lax.dot_general(al, bh, dn, preferred_element_type=F32)


def _make_diff_mm(dot):
    @jax.custom_vjp
    def nn(a, b):
        return dot(a, b, _NN)

    @jax.custom_vjp
    def nt(a, b):
        return dot(a, b, _NT)

    @jax.custom_vjp
    def tn(a, b):
        return dot(a, b, _TN)

    nn.defvjp(lambda a, b: (dot(a, b, _NN), (a, b)), lambda r, g: (nt(g, r[1]), tn(r[0], g)))
    nt.defvjp(lambda a, b: (dot(a, b, _NT), (a, b)), lambda r, g: (nn(g, r[1]), tn(g, r[0])))
    tn.defvjp(lambda a, b: (dot(a, b, _TN), (a, b)), lambda r, g: (nt(r[1], g), nn(r[0], g)))
    return nn, nt, tn


_d_nn, _d_nt, _d_tn = _make_diff_mm(_dot)
_p_nn, _p_nt, _p_tn = _make_diff_mm(_dot_precise)


@jax.custom_vjp
def _halves(x):
    n = x.shape[0] // 2
    return x[:n], x[n:]


_halves.defvjp(lambda x: (_halves(x), None), lambda _, g: (jnp.concatenate(g, axis=0),))


@jax.custom_vjp
def _col_halves(x):
    n = x.shape[1] // 2
    return x[:, :n], x[:, n:]


_col_halves.defvjp(lambda x: (_col_halves(x), None), lambda _, g: (jnp.concatenate(g, axis=1),))


def _softplus_raw(x):
    return jnp.maximum(x, 0.0) + jnp.log(1.0 + jnp.exp(-jnp.abs(x)))


@jax.custom_vjp
def _softplus(x):
    return _softplus_raw(x)


_softplus.defvjp(lambda x: (_softplus_raw(x), x), lambda x, g: (g * jax.nn.sigmoid(x),))


def _rms(x, gain, eps):
    return x * lax.rsqrt(jnp.mean(x * x, axis=-1, keepdims=True) + eps) * gain


def _silu(x):
    return x * jax.nn.sigmoid(x)


def _shift_rows_raw(x, k, down):
    n = x.shape[0]
    row = lax.broadcasted_iota(jnp.int32, x.shape, 0)
    if down:
        return jnp.where(row >= k, pltpu.roll(x, k, 0), 0.0)
    return jnp.where(row < n - k, pltpu.roll(x, n - k, 0), 0.0)


@functools.partial(jax.custom_vjp, nondiff_argnums=(1,))
def _shift_down(x, k):
    return _shift_rows_raw(x, k, True)


_shift_down.defvjp(lambda x, k: (_shift_rows_raw(x, k, True), None),
                   lambda k, _, g: (_shift_rows_raw(g, k, False),))


def _layer_spec(layer, block, index_map, single=False):
    full_map = lambda *a: (layer,) + tuple(index_map(*a))
    if single:
        return pl.BlockSpec((None,) + block, full_map, pipeline_mode=pl.Buffered(1))
    return pl.BlockSpec((None,) + block, full_map)


def _ffn_fwd(x, gain, w_in, w_out, layer):
    t, d = x.shape
    f = w_out.shape[1]
    fc = _pick(f, (256, 128))
    nj = f // fc
    rt = _pick(t, (512, 256, 128))

    def body(x_ref, g_ref, wg_ref, wu_ref, wo_ref, o_ref, hs_ref):
        @pl.when(pl.program_id(0) == 0)
        def _():
            for r in range(t // rt):
                rows = pl.ds(r * rt, rt)
                xr = x_ref[rows, :]
                hs_ref[rows, :] = _rms(xr, g_ref[...], RMS_EPS).astype(BF16)
                o_ref[rows, :] = xr

        for r in range(t // rt):
            rows = pl.ds(r * rt, rt)
            h = hs_ref[rows, :]
            a = _mm_nn(h, wg_ref[...])
            b = _mm_nn(h, wu_ref[...])
            o_ref[rows, :] += FFN_HALF * _mm_nn(_silu(a) * b, wo_ref[...])

    return pl.pallas_call(
        body, name="ffn_fwd", grid=(nj,),
        in_specs=[_const_spec((t, d), True), _const_spec((1, d)),
                  _layer_spec(layer, (d, fc), lambda j: (0, j)), _layer_spec(layer, (d, fc), lambda j: (0, nj + j)),
                  _layer_spec(layer, (fc, d), lambda j: (j, 0))],
        out_specs=[_const_spec((t, d)), _const_spec((t, d))],
        out_shape=[jax.ShapeDtypeStruct((t, d), F32), jax.ShapeDtypeStruct((t, d), BF16)],
        compiler_params=_params(dimension_semantics=("arbitrary",)),
    )(x, gain, w_in, w_in, w_out)


def _ffn_bwd(x, hs, gain, dy, w_in, w_out, layer):
    t, d = x.shape
    f = w_out.shape[1]
    fc = _pick(f, (256, 128))
    nj = f // fc
    rt = _pick(t, (512, 256, 128))
    nr = t // rt

    def body(x_ref, hs_ref, g_ref, dy_ref, wg_ref, wu_ref, wo_ref, dx_ref, dg_ref, dwi_ref, dwo_ref,
             dwg_acc, dwu_acc, dwo_acc):
        j = pl.program_id(0)

        @pl.when(j == 0)
        def _():
            for r in range(nr):
                dx_ref[pl.ds(r * rt, rt), :] = jnp.zeros((rt, d), F32)

        for r in range(nr):
            rows = pl.ds(r * rt, rt)
            h = hs_ref[rows, :]
            dy2 = (FFN_HALF * dy_ref[rows, :]).astype(BF16)
            a = _mm_nn(h, wg_ref[...])
            b = _mm_nn(h, wu_ref[...])
            sig = jax.nn.sigmoid(a)
            sa = a * sig
            ds = _mm_nt(dy2, wo_ref[...])
            da = ds * b * (sig * (1.0 + a * (1.0 - sig)))
            db = ds * sa
            dx_ref[rows, :] += _mm_nt(da, wg_ref[...]) + _mm_nt(db, wu_ref[...])
            dwo_c = _mm_tn(sa * b, dy2)
            dwg_c = _mm_tn(h, da)
            dwu_c = _mm_tn(h, db)
            if r == 0:
                dwo_acc[...] = dwo_c
                dwg_acc[...] = dwg_c
                dwu_acc[...] = dwu_c
            else:
                dwo_acc[...] += dwo_c
                dwg_acc[...] += dwg_c
                dwu_acc[...] += dwu_c
        dwi_ref[0] = dwg_acc[...].astype(BF16)
        dwi_ref[1] = dwu_acc[...].astype(BF16)
        dwo_ref[...] = dwo_acc[...].astype(BF16)

        @pl.when(j == nj - 1)
        def _():
            for r in range(nr):
                rows = pl.ds(r * rt, rt)
                _, vjp = jax.vjp(lambda xx, gg: _rms(xx, gg, RMS_EPS), x_ref[rows, :], g_ref[...])
                dxn, dgr = vjp(dx_ref[rows, :])
                dx_ref[rows, :] = dy_ref[rows, :] + dxn
                if r == 0:
                    dg_ref[...] = dgr
                else:
                    dg_ref[...] += dgr

    return pl.pallas_call(
        body, name="ffn_bwd", grid=(nj,),
        in_specs=[_const_spec((t, d), True), _const_spec((t, d), True), _const_spec((1, d)), _const_spec((t, d), True),
                  _layer_spec(layer, (d, fc), lambda j: (0, j)), _layer_spec(layer, (d, fc), lambda j: (0, nj + j)),
                  _layer_spec(layer, (fc, d), lambda j: (j, 0))],
        out_specs=[_const_spec((t, d)), _const_spec((1, d)),
                   pl.BlockSpec((2, d, fc), lambda j: (0, 0, j)), pl.BlockSpec((fc, d), lambda j: (j, 0))],
        out_shape=[jax.ShapeDtypeStruct((t, d), F32), jax.ShapeDtypeStruct((1, d), F32),
                   jax.ShapeDtypeStruct((2, d, f), BF16), jax.ShapeDtypeStruct((f, d), BF16)],
        scratch_shapes=[pltpu.VMEM((d, fc), F32), pltpu.VMEM((d, fc), F32), pltpu.VMEM((fc, d), F32)],
        compiler_params=_params(dimension_semantics=("arbitrary",)),
    )(x, hs, gain, dy, w_in, w_in, w_out)


def _proj_fwd(x, gain, w, wba, layer):
    t, d = x.shape
    n = w.shape[2]
    nc = _pick(n, (512, 256, 128))
    rt = _pick(t, (512, 256, 128))

    def body(x_ref, g_ref, w_ref, wba_ref, p_ref, ba_ref, hs_ref):
        @pl.when(pl.program_id(0) == 0)
        def _():
            for r in range(t // rt):
                rows = pl.ds(r * rt, rt)
                h = _rms(x_ref[rows, :], g_ref[...], RMS_EPS).astype(BF16)
                hs_ref[rows, :] = h
                ba_ref[rows, :] = _mm_nn(h, wba_ref[...])

        for r in range(t // rt):
            rows = pl.ds(r * rt, rt)
            p_ref[rows, :] = _mm_nn(hs_ref[rows, :], w_ref[...])

    return pl.pallas_call(
        body, name="proj_fwd", grid=(n // nc,),
        in_specs=[_const_spec((t, d), True), _const_spec((1, d)),
                  _layer_spec(layer, (d, nc), lambda j: (0, j)), _layer_spec(layer, (d, LANES), lambda j: (0, 0))],
        out_specs=[pl.BlockSpec((t, nc), lambda j: (0, j)), _const_spec((t, LANES)), _const_spec((t, d))],
        out_shape=[jax.ShapeDtypeStruct((t, n), F32), jax.ShapeDtypeStruct((t, LANES), F32),
                   jax.ShapeDtypeStruct((t, d), BF16)],
        compiler_params=_params(dimension_semantics=("arbitrary",)),
    )(x, gain, w, wba)


def _proj_bwd(x, hs, gain, dres, dp, dba, w, wba, layer):
    t, d = x.shape
    n = w.shape[2]
    nc = _pick(n, (512, 256, 128))
    nj = n // nc
    rt = _pick(t, (512, 256, 128))
    nr = t // rt

    def body(x_ref, hs_ref, g_ref, dres_ref, dp_ref, dba_ref, w_ref, wba_ref, dx_ref, dg_ref, dw_ref, dwba_ref, dw_acc):
        j = pl.program_id(0)

        @pl.when(j == 0)
        def _():
            for r in range(nr):
                rows = pl.ds(r * rt, rt)
                h = hs_ref[rows, :]
                g = dba_ref[rows, :]
                dx_ref[rows, :] = _mm_nt(g, wba_ref[...])
                if r == 0:
                    dwba_ref[...] = _mm_tn(h, g)
                else:
                    dwba_ref[...] += _mm_tn(h, g)

        for r in range(nr):
            rows = pl.ds(r * rt, rt)
            g = dp_ref[rows, :]
            dx_ref[rows, :] += _mm_nt(g, w_ref[...])
            if r == 0:
                dw_acc[...] = _mm_tn(hs_ref[rows, :], g)
            else:
                dw_acc[...] += _mm_tn(hs_ref[rows, :], g)
        dw_ref[...] = dw_acc[...].astype(BF16)

        @pl.when(j == nj - 1)
        def _():
            for r in range(nr):
                rows = pl.ds(r * rt, rt)
                _, vjp = jax.vjp(lambda xx, gg: _rms(xx, gg, RMS_EPS), x_ref[rows, :], g_ref[...])
                dxn, dgr = vjp(dx_ref[rows, :])
                dx_ref[rows, :] = dres_ref[rows, :] + dxn
                if r == 0:
                    dg_ref[...] = dgr
                else:
                    dg_ref[...] += dgr

    return pl.pallas_call(
        body, name="proj_bwd", grid=(nj,),
        in_specs=[_const_spec((t, d), True), _const_spec((t, d), True), _const_spec((1, d)), _const_spec((t, d), True),
                  pl.BlockSpec((t, nc), lambda j: (0, j)), _const_spec((t, LANES)),
                  _layer_spec(layer, (d, nc), lambda j: (0, j)), _layer_spec(layer, (d, LANES), lambda j: (0, 0))],
        out_specs=[_const_spec((t, d)), _const_spec((1, d)),
                   pl.BlockSpec((d, nc), lambda j: (0, j)), _const_spec((d, LANES))],
        out_shape=[jax.ShapeDtypeStruct((t, d), F32), jax.ShapeDtypeStruct((1, d), F32),
                   jax.ShapeDtypeStruct((d, n), BF16), jax.ShapeDtypeStruct((d, LANES), F32)],
        scratch_shapes=[pltpu.VMEM((d, nc), F32)],
        compiler_params=_params(dimension_semantics=("arbitrary",)),
    )(x, hs, gain, dres, dp, dba, w, wba)


def _conv_act(x, w0, w1, w2, w3, is_qk):
    y = w3 * x + w2 * _shift_down(x, 1) + w1 * _shift_down(x, 2) + w0 * _shift_down(x, 3)
    y = _silu(y)
    inv = lax.rsqrt(jnp.sum(y * y, axis=-1, keepdims=True) + L2_EPS)
    return y * (is_qk * inv + (1.0 - is_qk))


def _taps(w_ref):
    return tuple(w_ref[i:i + 1, :] for i in range(DN_CONV))


def _dn_prep_fwd(proj, conv_w, n_heads):
    t = proj.shape[0]
    nb = 3 * n_heads

    def body(x_ref, w_ref, o_ref):
        is_qk = jnp.where(pl.program_id(0) < 2 * n_heads, 1.0, 0.0).astype(F32)
        o_ref[...] = _conv_act(x_ref[...], *_taps(w_ref), is_qk)

    return pl.pallas_call(
        body, name="dn_prep_fwd", grid=(nb,),
        in_specs=[pl.BlockSpec((t, HEAD_DIM), lambda i: (0, i)), pl.BlockSpec((CONV_ROWS, HEAD_DIM), lambda i: (0, i))],
        out_specs=pl.BlockSpec((t, HEAD_DIM), lambda i: (0, i)),
        out_shape=jax.ShapeDtypeStruct((t, nb * HEAD_DIM), F32),
        compiler_params=_params(dimension_semantics=("arbitrary",)),
    )(proj, conv_w)


def _dn_prep_bwd(proj, conv_w, dact, n_heads):
    t = proj.shape[0]
    nb = 3 * n_heads

    def body(x_ref, w_ref, g_ref, dx_ref, dw_ref):
        is_qk = jnp.where(pl.program_id(0) < 2 * n_heads, 1.0, 0.0).astype(F32)
        _, vjp = jax.vjp(lambda x, a, b, c, e: _conv_act(x, a, b, c, e, is_qk), x_ref[...], *_taps(w_ref))
        dx, d0, d1, d2, d3 = vjp(g_ref[...])
        dx_ref[...] = dx.astype(BF16)
        dw_ref[...] = jnp.concatenate([d0, d1, d2, d3, jnp.zeros((CONV_ROWS - DN_CONV, HEAD_DIM), F32)], axis=0)

    return pl.pallas_call(
        body, name="dn_prep_bwd", grid=(nb,),
        in_specs=[pl.BlockSpec((t, HEAD_DIM), lambda i: (0, i)), pl.BlockSpec((CONV_ROWS, HEAD_DIM), lambda i: (0, i)),
                  pl.BlockSpec((t, HEAD_DIM), lambda i: (0, i))],
        out_specs=[pl.BlockSpec((t, HEAD_DIM), lambda i: (0, i)), pl.BlockSpec((CONV_ROWS, HEAD_DIM), lambda i: (0, i))],
        out_shape=[jax.ShapeDtypeStruct((t, nb * HEAD_DIM), BF16), jax.ShapeDtypeStruct((CONV_ROWS, nb * HEAD_DIM), F32)],
        compiler_params=_params(dimension_semantics=("arbitrary",)),
    )(proj, conv_w, dact)


def _unit_lower_inverses(lmats, c):
    r = lax.broadcasted_iota(jnp.int32, (c, c), 0)
    q = lax.broadcasted_iota(jnp.int32, (c, c), 1)
    eye = jnp.where(r == q, 1.0, 0.0)
    ps = [eye - l for l in lmats]
    ms = [_p_nn(l, l) for l in lmats]
    n = 2
    while 2 * n < c:
        both = [_halves(_p_nn(jnp.concatenate([p, m], axis=0), m)) for p, m in zip(ps, ms)]
        ps = [p + pm for p, (pm, _) in zip(ps, both)]
        ms = [mm for _, mm in both]
        n *= 2
    return [p + _p_nn(p, m) for p, m in zip(ps, ms)]


def _delta_heads(qs, ks, vs, bg, alog, dtb, states):
    n_heads = len(qs)
    heads = range(n_heads)
    c = qs[0].shape[0]
    lane = lax.broadcasted_iota(jnp.int32, (c, LANES), 1)
    r = lax.broadcasted_iota(jnp.int32, (c, c), 0)
    s = lax.broadcasted_iota(jnp.int32, (c, c), 1)
    beta_all = jax.nn.sigmoid(bg)
    g_all = -jnp.exp(alog) * _softplus(bg + dtb)
    beta = [jnp.sum(jnp.where(lane == h, beta_all, 0.0), axis=1, keepdims=True) for h in heads]
    g = [jnp.sum(jnp.where(lane == n_heads + h, g_all, 0.0), axis=1, keepdims=True) for h in heads]
    g_row = [jnp.sum(jnp.where(r == s, g[h], 0.0), axis=0, keepdims=True) for h in heads]
    gc = [jnp.sum(jnp.where(s <= r, g_row[h], 0.0), axis=1, keepdims=True) for h in heads]
    gr = [jnp.sum(jnp.where(r <= s, g[h], 0.0), axis=0, keepdims=True) for h in heads]
    g_last = [jnp.sum(g[h], axis=0, keepdims=True) for h in heads]
    decay = [jnp.exp(jnp.where(r >= s, gc[h] - gr[h], NEG_BIG)) for h in heads]
    q_scaled = [qs[h] * (HEAD_DIM ** -0.5) for h in heads]
    k_beta = [ks[h] * beta[h] for h in heads]
    kk_qk = [_halves(_d_nt(jnp.concatenate([k_beta[h], q_scaled[h]], axis=0), ks[h])) for h in heads]
    lmat = [jnp.where(r > s, kk_qk[h][0] * decay[h], 0.0) for h in heads]
    attn = [kk_qk[h][1] * decay[h] for h in heads]
    tinv = _unit_lower_inverses(lmat, c)
    u_w = [_col_halves(_p_nn(tinv[h], jnp.concatenate([vs[h] * beta[h], k_beta[h] * jnp.exp(gc[h])], axis=1)))
           for h in heads]
    u = [u_w[h][0] for h in heads]
    w = [u_w[h][1] for h in heads]
    ws_qs = [_halves(_d_nn(jnp.concatenate([w[h], q_scaled[h] * jnp.exp(gc[h])], axis=0), states[h])) for h in heads]
    v_new = [u[h] - ws_qs[h][0] for h in heads]
    o = [ws_qs[h][1] + _d_nn(attn[h], v_new[h]) for h in heads]
    kv = [_d_tn(ks[h] * jnp.exp(g_last[h] - gc[h]), v_new[h]) for h in heads]
    new_states = [states[h] * jnp.exp(g_last[h]) + kv[h] for h in heads]
    return tuple(o), tuple(new_states)


def _delta_fwd(act, ba, alog, dtb, n_heads):
    t = act.shape[0]
    d = n_heads * HEAD_DIM
    c = DN_CHUNK
    nc = t // c

    def body(q_ref, k_ref, v_ref, bg_ref, al_ref, dt_ref, o_ref, snap_ref, st_ref):
        @pl.when(pl.program_id(0) == 0)
        def _():
            st_ref[...] = jnp.zeros(st_ref.shape, F32)

        snap_ref[0] = st_ref[...]
        cols = [slice(h * HEAD_DIM, (h + 1) * HEAD_DIM) for h in range(n_heads)]
        os, new_states = _delta_heads([q_ref[:, sl] for sl in cols], [k_ref[:, sl] for sl in cols],
                                      [v_ref[:, sl] for sl in cols], bg_ref[...], al_ref[...], dt_ref[...],
                                      [st_ref[h] for h in range(n_heads)])
        for h, sl in enumerate(cols):
            o_ref[:, sl] = os[h]
            st_ref[h] = new_states[h]

    return pl.pallas_call(
        body, name="delta_fwd", grid=(nc,),
        in_specs=[pl.BlockSpec((c, d), lambda i: (i, 0)), pl.BlockSpec((c, d), lambda i: (i, 1)),
                  pl.BlockSpec((c, d), lambda i: (i, 2)), pl.BlockSpec((c, LANES), lambda i: (i, 0)),
                  _const_spec((1, LANES)), _const_spec((1, LANES))],
        out_specs=[pl.BlockSpec((c, d), lambda i: (i, 0)),
                   pl.BlockSpec((1, n_heads, HEAD_DIM, HEAD_DIM), lambda i: (i, 0, 0, 0))],
        out_shape=[jax.ShapeDtypeStruct((t, d), F32), jax.ShapeDtypeStruct((nc, n_heads, HEAD_DIM, HEAD_DIM), F32)],
        scratch_shapes=[pltpu.VMEM((n_heads, HEAD_DIM, HEAD_DIM), F32)],
        compiler_params=_params(dimension_semantics=("arbitrary",)),
    )(act, act, act, ba, alog, dtb)


def _delta_bwd(act, ba, alog, dtb, snaps, do, n_heads):
    t = act.shape[0]
    d = n_heads * HEAD_DIM
    c = DN_CHUNK
    nc = t // c

    def body(q_ref, k_ref, v_ref, bg_ref, al_ref, dt_ref, snap_ref, do_ref,
             dact_ref, dbg_ref, dal_ref, ddt_ref, ds_ref):
        @pl.when(pl.program_id(0) == 0)
        def _():
            ds_ref[...] = jnp.zeros(ds_ref.shape, F32)
            dal_ref[...] = jnp.zeros((1, LANES), F32)
            ddt_ref[...] = jnp.zeros((1, LANES), F32)

        heads = range(n_heads)
        cols = [slice(h * HEAD_DIM, (h + 1) * HEAD_DIM) for h in heads]
        _, vjp = jax.vjp(_delta_heads, tuple(q_ref[:, sl] for sl in cols), tuple(k_ref[:, sl] for sl in cols),
                         tuple(v_ref[:, sl] for sl in cols), bg_ref[...], al_ref[...], dt_ref[...],
                         tuple(snap_ref[0, h] for h in heads))
        dq, dk, dv, dbg, dal, ddt, dst = vjp((tuple(do_ref[:, sl] for sl in cols), tuple(ds_ref[h] for h in heads)))
        for h, sl in enumerate(cols):
            dact_ref[:, sl] = dq[h]
            dact_ref[:, d + h * HEAD_DIM:d + (h + 1) * HEAD_DIM] = dk[h]
            dact_ref[:, 2 * d + h * HEAD_DIM:2 * d + (h + 1) * HEAD_DIM] = dv[h]
            ds_ref[h] = dst[h]
        dal_ref[...] += dal
        ddt_ref[...] += ddt
        dbg_ref[...] = dbg.astype(BF16)

    rev = lambda i: nc - 1 - i
    return pl.pallas_call(
        body, name="delta_bwd", grid=(nc,),
        in_specs=[pl.BlockSpec((c, d), lambda i: (rev(i), 0)), pl.BlockSpec((c, d), lambda i: (rev(i), 1)),
                  pl.BlockSpec((c, d), lambda i: (rev(i), 2)), pl.BlockSpec((c, LANES), lambda i: (rev(i), 0)),
                  _const_spec((1, LANES)), _const_spec((1, LANES)),
                  pl.BlockSpec((1, n_heads, HEAD_DIM, HEAD_DIM), lambda i: (rev(i), 0, 0, 0)),
                  pl.BlockSpec((c, d), lambda i: (rev(i), 0))],
        out_specs=[pl.BlockSpec((c, 3 * d), lambda i: (rev(i), 0)), pl.BlockSpec((c, LANES), lambda i: (rev(i), 0)),
                   _const_spec((1, LANES)), _const_spec((1, LANES))],
        out_shape=[jax.ShapeDtypeStruct((t, 3 * d), F32), jax.ShapeDtypeStruct((t, LANES), BF16),
                   jax.ShapeDtypeStruct((1, LANES), F32), jax.ShapeDtypeStruct((1, LANES), F32)],
        scratch_shapes=[pltpu.VMEM((n_heads, HEAD_DIM, HEAD_DIM), F32)],
        compiler_params=_params(dimension_semantics=("arbitrary",)),
    )(act, act, act, ba, alog, dtb, snaps, do)


def _head_norm2(a, b, ga, gb):
    return _rms(a, ga, RMS_EPS), _rms(b, gb, RMS_EPS)


def _sb_prep_fwd(proj, qn, kn, n_heads):
    t = proj.shape[0]
    d = n_heads * HEAD_DIM
    tm = _pick(t, (256, 128))

    def body(q_ref, k_ref, v_ref, qn_ref, kn_ref, qo_ref, ko_ref, vo_ref):
        for h in range(n_heads):
            sl = slice(h * HEAD_DIM, (h + 1) * HEAD_DIM)
            qh, kh = _head_norm2(q_ref[:, sl], k_ref[:, sl], qn_ref[...], kn_ref[...])
            qo_ref[:, sl] = qh.astype(BF16)
            ko_ref[:, sl] = kh.astype(BF16)
        vo_ref[...] = v_ref[...].astype(BF16)

    tile = lambda k: pl.BlockSpec((tm, d), lambda i: (i, k))
    return pl.pallas_call(
        body, name="sb_prep_fwd", grid=(t // tm,),
        in_specs=[tile(4), tile(5), tile(6), _const_spec((1, HEAD_DIM)), _const_spec((1, HEAD_DIM))],
        out_specs=[tile(0), tile(0), tile(0)],
        out_shape=[jax.ShapeDtypeStruct((t, d), BF16)] * 3,
        compiler_params=_params(dimension_semantics=("arbitrary",)),
    )(proj, proj, proj, qn, kn)


def _sb_prep_bwd(proj, qn, kn, dq, dk, n_heads):
    t = proj.shape[0]
    d = n_heads * HEAD_DIM
    tm = _pick(t, (256, 128))

    def body(q_ref, k_ref, qn_ref, kn_ref, dq_ref, dk_ref, dqo_ref, dko_ref, dqn_ref, dkn_ref):
        @pl.when(pl.program_id(0) == 0)
        def _():
            dqn_ref[...] = jnp.zeros((1, HEAD_DIM), F32)
            dkn_ref[...] = jnp.zeros((1, HEAD_DIM), F32)

        for h in range(n_heads):
            sl = slice(h * HEAD_DIM, (h + 1) * HEAD_DIM)
            _, vjp = jax.vjp(_head_norm2, q_ref[:, sl], k_ref[:, sl], qn_ref[...], kn_ref[...])
            da, db, dga, dgb = vjp((dq_ref[:, sl], dk_ref[:, sl]))
            dqo_ref[:, sl] = da.astype(BF16)
            dko_ref[:, sl] = db.astype(BF16)
            dqn_ref[...] += dga
            dkn_ref[...] += dgb

    return pl.pallas_call(
        body, name="sb_prep_bwd", grid=(t // tm,),
        in_specs=[pl.BlockSpec((tm, d), lambda i: (i, 4)), pl.BlockSpec((tm, d), lambda i: (i, 5)),
                  _const_spec((1, HEAD_DIM)), _const_spec((1, HEAD_DIM)),
                  pl.BlockSpec((tm, d), lambda i: (i, 0)), pl.BlockSpec((tm, d), lambda i: (i, 0))],
        out_specs=[pl.BlockSpec((tm, d), lambda i: (i, 0)), pl.BlockSpec((tm, d), lambda i: (i, 0)),
                   _const_spec((1, HEAD_DIM)), _const_spec((1, HEAD_DIM))],
        out_shape=[jax.ShapeDtypeStruct((t, d), BF16), jax.ShapeDtypeStruct((t, d), BF16),
                   jax.ShapeDtypeStruct((1, HEAD_DIM), F32), jax.ShapeDtypeStruct((1, HEAD_DIM), F32)],
        compiler_params=_params(dimension_semantics=("arbitrary",)),
    )(proj, proj, qn, kn, dq, dk)


def _cumsum_mm(x, tri):
    hi, lo = _split(x)
    return (lax.dot_general(hi, tri, (_NN, ((), ())), preferred_element_type=F32)
            + lax.dot_general(lo, tri, (_NN, ((), ())), preferred_element_type=F32))


def _sb_valid(i, j):
    row = lax.broadcasted_iota(jnp.int32, (SB_BLOCK, SB_BLOCK), 0)
    col = lax.broadcasted_iota(jnp.int32, (SB_BLOCK, SB_BLOCK), 1)
    return (col + j * SB_BLOCK) < (row + i * SB_BLOCK)


def _sb_attn_fwd(qb, kb, vb, n_heads):
    t = qb.shape[0]
    d = n_heads * HEAD_DIM
    nq = t // SB_BLOCK
    hb = _pick(n_heads, (4, 2, 1))
    wide = hb * HEAD_DIM
    scale = HEAD_DIM ** -0.5

    def body(q_ref, k_ref, v_ref, o_ref, lt_ref):
        i = pl.program_id(1)
        row = lax.broadcasted_iota(jnp.int32, (SB_BLOCK, SB_BLOCK), 0)
        col = lax.broadcasted_iota(jnp.int32, (SB_BLOCK, SB_BLOCK), 1)
        after = jnp.where(row > col, 1.0, 0.0).astype(BF16)
        heads = [slice(h * HEAD_DIM, (h + 1) * HEAD_DIM) for h in range(hb)]
        every = range(hb)
        qs = [q_ref[:, sl].astype(BF16) for sl in heads]

        def step(m, carry):
            j0 = i - 2 * m
            js = (j0, jnp.maximum(j0 - 1, 0))
            valid = (_sb_valid(i, js[0]), jnp.logical_and(_sb_valid(i, js[1]), j0 >= 1))
            rows = [pl.ds(pl.multiple_of(j * SB_BLOCK, SB_BLOCK), SB_BLOCK) for j in js]
            units = [(h, b) for b in range(2) for h in every]
            z = {u: _mm_nt(qs[u[0]], k_ref[rows[u[1]], heads[u[0]]]) * scale for u in units}
            sp = {u: _softplus_raw(z[u]) for u in units}
            lm = {u: jnp.where(valid[u[1]], -sp[u], 0.0) for u in units}
            tail = {u: _cumsum_mm(lm[u], after) for u in units}
            later = {(h, 0): carry[h][1] for h in every}
            later.update({(h, 1): carry[h][1] + jnp.sum(lm[h, 0], axis=1, keepdims=True) for h in every})
            w = {u: jnp.where(valid[u[1]], jnp.exp(z[u] - sp[u] + later[u] + tail[u]), 0.0) for u in units}
            pv = {u: _mm_nn(w[u], v_ref[rows[u[1]], heads[u[0]]]) for u in units}
            return tuple((carry[h][0] + pv[h, 0] + pv[h, 1], later[h, 1] + jnp.sum(lm[h, 1], axis=1, keepdims=True))
                         for h in every)

        init = tuple((jnp.zeros((SB_BLOCK, HEAD_DIM), F32), jnp.zeros((SB_BLOCK, 1), F32)) for _ in heads)
        res = lax.fori_loop(0, (i + 2) // 2, step, init)
        for h, sl in enumerate(heads):
            o_ref[:, sl] = res[h][0]
            lt_ref[:, sl] = jnp.broadcast_to(res[h][1], (SB_BLOCK, HEAD_DIM))

    return pl.pallas_call(
        body, name="sb_attn_fwd", grid=(n_heads // hb, nq),
        in_specs=[pl.BlockSpec((SB_BLOCK, wide), lambda g, i: (i, g)),
                  pl.BlockSpec((t, wide), lambda g, i: (0, g)),
                  pl.BlockSpec((t, wide), lambda g, i: (0, g))],
        out_specs=[pl.BlockSpec((SB_BLOCK, wide), lambda g, i: (i, g)),
                   pl.BlockSpec((SB_BLOCK, wide), lambda g, i: (i, g))],
        out_shape=[jax.ShapeDtypeStruct((t, d), F32), jax.ShapeDtypeStruct((t, d), F32)],
        compiler_params=_params(dimension_semantics=("arbitrary", "arbitrary")),
    )(qb, kb, vb)


def _sb_attn_bwd(qb, kb, vb, ltot, do, n_heads):
    t = qb.shape[0]
    d = n_heads * HEAD_DIM
    nq = t // SB_BLOCK
    hb = _pick(n_heads, (4, 2, 1))
    wide = hb * HEAD_DIM
    scale = HEAD_DIM ** -0.5

    def body(q_ref, k_ref, v_ref, lt_ref, do_ref, dq_ref, dk_ref, dv_ref):
        i = pl.program_id(1)

        @pl.when(i == 0)
        def _():
            dk_ref[...] = jnp.zeros((t, wide), F32)
            dv_ref[...] = jnp.zeros((t, wide), F32)

        row = lax.broadcasted_iota(jnp.int32, (SB_BLOCK, SB_BLOCK), 0)
        col = lax.broadcasted_iota(jnp.int32, (SB_BLOCK, SB_BLOCK), 1)
        upto = jnp.where(row <= col, 1.0, 0.0).astype(BF16)
        before = jnp.where(row < col, 1.0, 0.0).astype(BF16)
        heads = [slice(h * HEAD_DIM, (h + 1) * HEAD_DIM) for h in range(hb)]
        every = range(hb)
        qs = [q_ref[:, sl].astype(BF16) for sl in heads]
        dos = [do_ref[:, sl].astype(BF16) for sl in heads]
        totals = [jnp.max(lt_ref[:, sl], axis=1, keepdims=True) for sl in heads]

        def step(m, carry):
            js = (2 * m, jnp.minimum(2 * m + 1, nq - 1))
            valid = (_sb_valid(i, js[0]), jnp.logical_and(_sb_valid(i, js[1]), 2 * m + 1 <= i))
            rows = [pl.ds(pl.multiple_of(j * SB_BLOCK, SB_BLOCK), SB_BLOCK) for j in js]
            units = [(h, b) for b in range(2) for h in every]
            kj = {u: k_ref[rows[u[1]], heads[u[0]]].astype(BF16) for u in units}
            vj = {u: v_ref[rows[u[1]], heads[u[0]]].astype(BF16) for u in units}
            z = {u: _mm_nt(qs[u[0]], kj[u]) * scale for u in units}
            dw = {u: _mm_nt(dos[u[0]], vj[u]) for u in units}
            sp = {u: _softplus_raw(z[u]) for u in units}
            lm = {u: jnp.where(valid[u[1]], -sp[u], 0.0) for u in units}
            head = {u: _cumsum_mm(lm[u], upto) for u in units}
            lm_before = {(h, 0): carry[h][1] for h in every}
            lm_before.update({(h, 1): carry[h][1] + jnp.sum(lm[h, 0], axis=1, keepdims=True) for h in every})
            w = {u: jnp.where(valid[u[1]], jnp.exp(z[u] - sp[u] + totals[u[0]] - (lm_before[u] + head[u])), 0.0)
                 for u in units}
            e = {u: w[u] * dw[u] for u in units}
            e_local = {u: _mm_nn(e[u], before) for u in units}
            e_before = {(h, 0): carry[h][2] for h in every}
            e_before.update({(h, 1): carry[h][2] + jnp.sum(e[h, 0], axis=1, keepdims=True) for h in every})
            sig = {u: jnp.exp(z[u] - sp[u]) for u in units}
            dz = {u: jnp.where(valid[u[1]], e[u] * (1.0 - sig[u]) - (e_before[u] + e_local[u]) * sig[u], 0.0) * scale
                  for u in units}
            for h, b in units:
                dv_ref[rows[b], heads[h]] += _mm_tn(w[h, b], dos[h])
            for h, b in units:
                dk_ref[rows[b], heads[h]] += _mm_tn(dz[h, b], qs[h])
            dq = {u: _mm_nn(dz[u], kj[u]) for u in units}
            return tuple((carry[h][0] + dq[h, 0] + dq[h, 1],
                          lm_before[h, 1] + jnp.sum(lm[h, 1], axis=1, keepdims=True),
                          e_before[h, 1] + jnp.sum(e[h, 1], axis=1, keepdims=True)) for h in every)

        zero_col = jnp.zeros((SB_BLOCK, 1), F32)
        init = tuple((jnp.zeros((SB_BLOCK, HEAD_DIM), F32), zero_col, zero_col) for _ in heads)
        res = lax.fori_loop(0, (i + 2) // 2, step, init)
        for h, sl in enumerate(heads):
            dq_ref[:, sl] = res[h][0]

    return pl.pallas_call(
        body, name="sb_attn_bwd", grid=(n_heads // hb, nq),
        in_specs=[pl.BlockSpec((SB_BLOCK, wide), lambda g, i: (i, g)),
                  pl.BlockSpec((t, wide), lambda g, i: (0, g)),
                  pl.BlockSpec((t, wide), lambda g, i: (0, g)),
                  pl.BlockSpec((SB_BLOCK, wide), lambda g, i: (i, g)),
                  pl.BlockSpec((SB_BLOCK, wide), lambda g, i: (i, g))],
        out_specs=[pl.BlockSpec((SB_BLOCK, wide), lambda g, i: (i, g)),
                   pl.BlockSpec((t, wide), lambda g, i: (0, g)),
                   pl.BlockSpec((t, wide), lambda g, i: (0, g))],
        out_shape=[jax.ShapeDtypeStruct((t, d), F32), jax.ShapeDtypeStruct((t, d), F32),
                   jax.ShapeDtypeStruct((t, d), F32)],
        compiler_params=_params(dimension_semantics=("arbitrary", "arbitrary")),
    )(qb, kb, vb, ltot, do)


def _gated_norm(oa, z, gn):
    return _rms(oa, gn, RMS_EPS) * _silu(z)


def _merge_gates(ya, yb, ga, gb):
    return jax.nn.sigmoid(ga) * ya + jax.nn.sigmoid(gb) * yb


def _merge_fwd(x1, oa, proj, ob, gn, wa, wb, wo, layer, n_heads):
    t, d = x1.shape
    tm = _pick(t, (256, 128))
    square = _layer_spec(layer, (d, d), lambda i: (0, 0), single=True)

    def body(x_ref, oa_ref, z_ref, ob_ref, ga_ref, gb_ref, gn_ref, wa_ref, wb_ref, wo_ref, o_ref, na_ref):
        for h in range(n_heads):
            sl = slice(h * HEAD_DIM, (h + 1) * HEAD_DIM)
            na_ref[:, sl] = _gated_norm(oa_ref[:, sl], z_ref[:, sl], gn_ref[...]).astype(BF16)
        m = _merge_gates(_mm_nn(na_ref[...], wa_ref[...]), _mm_nn(ob_ref[...], wb_ref[...]), ga_ref[...], gb_ref[...])
        o_ref[...] = x_ref[...] + _mm_nn(m, wo_ref[...])

    tile = lambda k: pl.BlockSpec((tm, d), lambda i: (i, k))
    return pl.pallas_call(
        body, name="merge_fwd", grid=(t // tm,),
        in_specs=[tile(0), tile(0), tile(3), tile(0), tile(7), tile(8), _const_spec((1, HEAD_DIM)),
                  square, square, square],
        out_specs=tile(0),
        out_shape=jax.ShapeDtypeStruct((t, d), F32),
        scratch_shapes=[pltpu.VMEM((tm, d), BF16)],
        compiler_params=_params(dimension_semantics=("arbitrary",)),
    )(x1, oa, proj, ob, proj, proj, gn, wa, wb, wo)


def _merge_bwd(oa, proj, ob, dy, gn, wa, wb, wo, layer, n_heads):
    t, d = oa.shape
    tm = _pick(t, (256, 128))
    nt = t // tm
    square = _layer_spec(layer, (d, d), lambda i: (0, 0), single=True)

    def body(oa_ref, z_ref, ob_ref, ga_ref, gb_ref, dy_ref, gn_ref, wa_ref, wb_ref, wo_ref,
             doa_ref, dz_ref, dob_ref, dga_ref, dgb_ref, dgn_ref, dwa_hbm, dwb_hbm, dwo_hbm,
             na_ref, dna_ref, dwa_ref, dwb_ref, dwo_ref, stage_ref):
        i = pl.program_id(0)

        @pl.when(i == 0)
        def _():
            dgn_ref[...] = jnp.zeros((1, HEAD_DIM), F32)
            dwa_ref[...] = jnp.zeros((d, d), F32)
            dwb_ref[...] = jnp.zeros((d, d), F32)
            dwo_ref[...] = jnp.zeros((d, d), F32)

        for h in range(n_heads):
            sl = slice(h * HEAD_DIM, (h + 1) * HEAD_DIM)
            na_ref[:, sl] = _gated_norm(oa_ref[:, sl], z_ref[:, sl], gn_ref[...]).astype(BF16)
        dy = dy_ref[...].astype(BF16)
        ob = ob_ref[...].astype(BF16)
        ya = _mm_nn(na_ref[...], wa_ref[...])
        yb = _mm_nn(ob, wb_ref[...])
        m, vjp = jax.vjp(_merge_gates, ya, yb, ga_ref[...], gb_ref[...])
        dwo_ref[...] += _mm_tn(m, dy)
        dya, dyb, dga, dgb = vjp(_mm_nt(dy, wo_ref[...]))
        dga_ref[...] = dga.astype(BF16)
        dgb_ref[...] = dgb.astype(BF16)
        dwa_ref[...] += _mm_tn(na_ref[...], dya)
        dwb_ref[...] += _mm_tn(ob, dyb)
        dob_ref[...] = _mm_nt(dyb, wb_ref[...])
        dna_ref[...] = _mm_nt(dya, wa_ref[...])
        for h in range(n_heads):
            sl = slice(h * HEAD_DIM, (h + 1) * HEAD_DIM)
            _, vjp_h = jax.vjp(_gated_norm, oa_ref[:, sl], z_ref[:, sl], gn_ref[...])
            doa, dz, dgn = vjp_h(dna_ref[:, sl])
            doa_ref[:, sl] = doa
            dz_ref[:, sl] = dz.astype(BF16)
            dgn_ref[...] += dgn

        @pl.when(i == nt - 1)
        def _():
            for acc, out in ((dwa_ref, dwa_hbm), (dwb_ref, dwb_hbm), (dwo_ref, dwo_hbm)):
                stage_ref[...] = acc[...].astype(BF16)
                pltpu.sync_copy(stage_ref, out)

    tile = lambda k: pl.BlockSpec((tm, d), lambda i: (i, k))
    any_spec = pl.BlockSpec(memory_space=pl.ANY)
    return pl.pallas_call(
        body, name="merge_bwd", grid=(nt,),
        in_specs=[tile(0), tile(3), tile(0), tile(7), tile(8), tile(0), _const_spec((1, HEAD_DIM)),
                  square, square, square],
        out_specs=[tile(0), tile(0), tile(0), tile(0), tile(0), _const_spec((1, HEAD_DIM)),
                   any_spec, any_spec, any_spec],
        out_shape=[jax.ShapeDtypeStruct((t, d), F32), jax.ShapeDtypeStruct((t, d), BF16),
                   jax.ShapeDtypeStruct((t, d), F32), jax.ShapeDtypeStruct((t, d), BF16),
                   jax.ShapeDtypeStruct((t, d), BF16), jax.ShapeDtypeStruct((1, HEAD_DIM), F32),
                   jax.ShapeDtypeStruct((d, d), BF16), jax.ShapeDtypeStruct((d, d), BF16),
                   jax.ShapeDtypeStruct((d, d), BF16)],
        scratch_shapes=[pltpu.VMEM((tm, d), BF16), pltpu.VMEM((tm, d), F32),
                        pltpu.VMEM((d, d), F32), pltpu.VMEM((d, d), F32), pltpu.VMEM((d, d), F32),
                        pltpu.VMEM((d, d), BF16)],
        compiler_params=_params(dimension_semantics=("arbitrary",)),
    )(oa, proj, ob, proj, proj, dy, gn, wa, wb, wo)


def _loss_head(y, target):
    t, d = y.shape
    tm = _pick(t, (256, 128))

    def body(y_ref, t_ref, dy_ref, loss_ref):
        @pl.when(pl.program_id(0) == 0)
        def _():
            loss_ref[...] = jnp.zeros((8, LANES), F32)

        err = y_ref[...] - t_ref[...]
        dy_ref[...] = err * (1.0 / d)
        per_token = jnp.sum(err * err, axis=1, keepdims=True) * (1.0 / d)
        loss_ref[...] += 0.5 * jnp.sum(per_token, axis=0, keepdims=True)

    return pl.pallas_call(
        body, name="loss_head", grid=(t // tm,),
        in_specs=[pl.BlockSpec((tm, d), lambda i: (i, 0)), pl.BlockSpec((tm, d), lambda i: (i, 0))],
        out_specs=[pl.BlockSpec((tm, d), lambda i: (i, 0)), _const_spec((8, LANES))],
        out_shape=[jax.ShapeDtypeStruct((t, d), F32), jax.ShapeDtypeStruct((8, LANES), F32)],
        compiler_params=_params(dimension_semantics=("arbitrary",)),
    )(y, target)


def _adamw(w, g, m, v):
    rows, cols = w.shape
    tr = rows
    for cand in (512, 256, 128, 64, 32, 16, 8):
        if rows % cand == 0 and cand * cols * 4 <= 2 * 1024 * 1024:
            tr = cand
            break

    def body(w_ref, g_ref, m_ref, v_ref, d_ref, mo_ref, vo_ref):
        g = g_ref[...]
        m2 = ADAM_B1 * m_ref[...] + (1.0 - ADAM_B1) * g
        v2 = ADAM_B2 * v_ref[...] + (1.0 - ADAM_B2) * (g * g)
        m_hat = m2 / (1.0 - ADAM_B1 ** ADAM_STEP)
        v_hat = v2 / (1.0 - ADAM_B2 ** ADAM_STEP)
        d_ref[...] = -ADAM_LR * (m_hat / (jnp.sqrt(v_hat) + ADAM_EPS) + ADAM_WD * w_ref[...])
        mo_ref[...] = m2
        vo_ref[...] = v2

    spec = pl.BlockSpec((tr, cols), lambda i: (i, 0))
    shape = jax.ShapeDtypeStruct((rows, cols), F32)
    return pl.pallas_call(
        body, name="adamw", grid=(rows // tr,), in_specs=[spec] * 4, out_specs=[spec] * 3,
        out_shape=[shape] * 3, compiler_params=_params(dimension_semantics=("arbitrary",)),
    )(w, g, m, v)


def _place():
    return lax.axis_index("x"), lax.axis_index("y"), lax.axis_index("c")


def _other_chips(x, y):
    return [(1 - x, y), (x, 1 - y), (1 - x, 1 - y)]


def _tile_rows(rows, cols, itemsize, cap=1536 * 1024):
    best = None
    for cand in range(16, rows + 1, 16):
        if rows % cand == 0 and cand * cols * itemsize <= cap:
            best = cand
    return best if best is not None else rows


def _allgather_layer(shards, layer, collective_id):
    n = len(shards)

    def body(*refs):
        srcs, outs = refs[:n], refs[n:2 * n]
        send_sems, recv_sems = refs[2 * n:]
        x, y, c = _place()
        me, sibling = (x, y, c), (x, y, 1 - c)
        chips = _other_chips(x, y)
        barrier = pltpu.get_barrier_semaphore()
        for peer in [(*chip, c) for chip in chips] + [sibling]:
            pl.semaphore_signal(barrier, inc=1, device_id=peer, device_id_type=MESH)
        pl.semaphore_wait(barrier, N_CHIPS)

        def half(w, which):
            rows = shards[w].shape[1] // 2
            return pl.ds(which * rows, rows)

        def copy(w, k, shard, which, to, from_src=False):
            part = half(w, which)
            return pltpu.make_async_remote_copy(
                src_ref=srcs[w].at[layer, part] if from_src else outs[w].at[shard, part],
                dst_ref=outs[w].at[shard, part], send_sem=send_sems.at[6 * w + k], recv_sem=recv_sems.at[6 * w + k],
                device_id=to, device_id_type=MESH)

        first = [copy(w, j, 2 * x + y, c, (*chip, c), from_src=True) for j, chip in enumerate(chips) for w in range(n)]
        for cp in first:
            cp.start()
        passed = []
        for j, (cx, cy) in enumerate(chips):
            for w in range(n):
                copy(w, j, 2 * cx + cy, c, me).wait_recv()
                cp = copy(w, 3 + j, 2 * cx + cy, c, sibling)
                cp.start()
                passed.append(cp)
        for j, (cx, cy) in enumerate(chips):
            for w in range(n):
                copy(w, 3 + j, 2 * cx + cy, 1 - c, me).wait_recv()
        for cp in first + passed:
            cp.wait_send()

    return pl.kernel(
        body, name=f"allgather_layer{layer}_id{collective_id}",
        out_type=[jax.ShapeDtypeStruct((N_CHIPS,) + s.shape[1:], s.dtype) for s in shards],
        mesh=plsc.ScalarSubcoreMesh(axis_name="sequencer", num_cores=1),
        scratch_types=[pltpu.SemaphoreType.DMA((6 * n,)), pltpu.SemaphoreType.DMA((6 * n,))],
        compiler_params=pltpu.CompilerParams(collective_id=collective_id),
    )(*shards)


def _swap_halves(grads):
    n = len(grads)

    def body(*refs):
        gs, gots = refs[:n], refs[n:2 * n]
        send_sems, recv_sems = refs[2 * n:]
        x, y, c = _place()
        copies = []
        for w in range(n):
            half = grads[w].shape[1] // 2
            copies.append(pltpu.make_async_remote_copy(
                src_ref=gs[w].at[:, pl.ds((1 - c) * half, half)], dst_ref=gots[w], send_sem=send_sems.at[w],
                recv_sem=recv_sems.at[w], device_id=(x, y, 1 - c), device_id_type=MESH))
        for cp in copies:
            cp.start()
        for cp in copies:
            cp.wait()

    hbm = pl.BlockSpec(memory_space=pl.ANY)
    return pl.pallas_call(
        body, name="swap_halves", in_specs=[hbm] * n, out_specs=[hbm] * n,
        out_shape=[jax.ShapeDtypeStruct((g.shape[0], g.shape[1] // 2, g.shape[2]), g.dtype) for g in grads],
        scratch_shapes=[pltpu.SemaphoreType.DMA((n,)), pltpu.SemaphoreType.DMA((n,))],
    )(*grads)


def _add_half(grad, got, c_idx):
    n, rows, all_cols = grad.shape
    side = N_CHIPS // n
    cols = all_cols // side
    half = rows // 2
    tr = _tile_rows(half, cols, 2)
    nb = half // tr

    def body(c_ref, a_ref, b_ref, o_ref):
        o_ref[...] = (a_ref[...].astype(F32) + b_ref[...].astype(F32)).astype(o_ref.dtype)

    return pl.pallas_call(
        body, name="add_half",
        grid_spec=pltpu.PrefetchScalarGridSpec(
            num_scalar_prefetch=1, grid=(N_CHIPS, nb),
            in_specs=[pl.BlockSpec((1, tr, cols), lambda s, r, c_ref: (s // side, c_ref[0] * nb + r, s % side)),
                      pl.BlockSpec((1, tr, cols), lambda s, r, c_ref: (s // side, r, s % side))],
            out_specs=pl.BlockSpec((1, tr, cols), lambda s, r, c_ref: (s, r, 0))),
        out_shape=jax.ShapeDtypeStruct((N_CHIPS, half, cols), grad.dtype),
        compiler_params=_params(dimension_semantics=("arbitrary", "arbitrary")),
    )(c_idx, grad, got)


def _scatter_partials(parts, layer, collective_id):
    n = len(parts)

    def body(*refs):
        ps, gots = refs[:n], refs[n:2 * n]
        send_sems, recv_sems = refs[2 * n:]
        x, y, c = _place()
        chips = _other_chips(x, y)
        barrier = pltpu.get_barrier_semaphore()
        for chip in chips:
            pl.semaphore_signal(barrier, inc=1, device_id=(*chip, c), device_id_type=MESH)
        pl.semaphore_wait(barrier, N_CHIPS - 1)
        copies = [pltpu.make_async_remote_copy(src_ref=ps[w].at[2 * cx + cy], dst_ref=gots[w].at[j],
                                               send_sem=send_sems.at[3 * w + j], recv_sem=recv_sems.at[3 * w + j],
                                               device_id=(cx, cy, c), device_id_type=MESH)
                  for j, (cx, cy) in enumerate(chips) for w in range(n)]
        for cp in copies:
            cp.start()
        for cp in copies:
            cp.wait()

    return pl.kernel(
        body, name=f"scatter_partials{layer}_id{collective_id}",
        out_type=[jax.ShapeDtypeStruct((N_CHIPS - 1,) + p.shape[1:], p.dtype) for p in parts],
        mesh=plsc.ScalarSubcoreMesh(axis_name="sequencer", num_cores=1),
        scratch_types=[pltpu.SemaphoreType.DMA((3 * n,)), pltpu.SemaphoreType.DMA((3 * n,))],
        compiler_params=pltpu.CompilerParams(collective_id=collective_id),
    )(*parts)


def _sum_partials(part, got, s_idx, c_idx, layer, depth, stacked=None):
    n, half, cols = part.shape
    tr = _tile_rows(half, cols, 2, cap=1024 * 1024)
    nb = half // tr

    def body(s_ref, c_ref, a_ref, b_ref, *rest):
        o_ref = rest[-1]
        acc = a_ref[0].astype(F32)
        for j in range(n - 1):
            acc = acc + b_ref[j].astype(F32)
        o_ref[...] = acc

    in_specs = [pl.BlockSpec((1, tr, cols), lambda r, s_ref, c_ref: (s_ref[0], r, 0)),
                pl.BlockSpec((n - 1, tr, cols), lambda r, s_ref, c_ref: (0, r, 0))]
    operands = [s_idx, c_idx, part, got]
    aliases = {}
    if stacked is not None:
        in_specs.append(pl.BlockSpec(memory_space=pl.ANY))
        operands.append(stacked)
        aliases = {len(operands) - 1: 0}
    return pl.pallas_call(
        body, name="sum_partials",
        grid_spec=pltpu.PrefetchScalarGridSpec(
            num_scalar_prefetch=2, grid=(nb,), in_specs=in_specs,
            out_specs=pl.BlockSpec((None, tr, cols), lambda r, s_ref, c_ref: (layer, c_ref[0] * nb + r, 0))),
        out_shape=jax.ShapeDtypeStruct((depth, 2 * half, cols), F32),
        input_output_aliases=aliases,
        compiler_params=_params(dimension_semantics=("arbitrary",)),
    )(*operands)


def _join_halves(bufs, layer):
    n = len(bufs)

    def body(*refs):
        outs = refs[n:2 * n]
        send_sems, recv_sems = refs[2 * n:]
        x, y, c = _place()
        copies = []
        for w in range(n):
            half = bufs[w].shape[1] // 2
            mine = outs[w].at[layer, pl.ds(c * half, half)]
            copies.append(pltpu.make_async_remote_copy(src_ref=mine, dst_ref=mine, send_sem=send_sems.at[w],
                                                       recv_sem=recv_sems.at[w], device_id=(x, y, 1 - c),
                                                       device_id_type=MESH))
        for cp in copies:
            cp.start()
        for cp in copies:
            cp.wait()

    hbm = pl.BlockSpec(memory_space=pl.ANY)
    return pl.pallas_call(
        body, name="join_halves", in_specs=[hbm] * n, out_specs=[hbm] * n,
        out_shape=[jax.ShapeDtypeStruct(b.shape, b.dtype) for b in bufs],
        input_output_aliases={w: w for w in range(n)},
        scratch_shapes=[pltpu.SemaphoreType.DMA((n,)), pltpu.SemaphoreType.DMA((n,))],
    )(*bufs)


def _allreduce_small(v, name):
    rows = v.shape[0]

    def body(v_ref, o_ref, gath, send_sems, recv_sems):
        x, y, c = _place()
        idx = 4 * x + 2 * y + c
        gath[0] = v_ref[...]
        copies = []
        for r in range(1, N_DEV):
            peer = (1 - x if r & 4 else x, 1 - y if r & 2 else y, 1 - c if r & 1 else c)
            cp = pltpu.make_async_remote_copy(src_ref=v_ref, dst_ref=gath.at[r], send_sem=send_sems.at[r - 1],
                                              recv_sem=recv_sems.at[r - 1], device_id=peer, device_id_type=MESH)
            cp.start()
            copies.append(cp)
        for cp in copies:
            cp.wait()
        acc = gath[idx]
        for a in range(1, N_DEV):
            acc = acc + gath[lax.bitwise_xor(idx, a)]
        o_ref[...] = acc

    vmem = pl.BlockSpec(memory_space=pltpu.VMEM)
    return pl.pallas_call(
        body, name=name, in_specs=[vmem], out_specs=vmem,
        out_shape=jax.ShapeDtypeStruct((rows, LANES), F32),
        scratch_shapes=[pltpu.VMEM((N_DEV, rows, LANES), F32), pltpu.SemaphoreType.DMA((N_DEV - 1,)),
                        pltpu.SemaphoreType.DMA((N_DEV - 1,))],
    )(v)


def _join_shards(name, gathered):
    if name in COL_SHARDED:
        return jnp.concatenate([gathered[s] for s in range(N_CHIPS)], axis=1)[None]
    return gathered.reshape(1, N_CHIPS * gathered.shape[1], gathered.shape[2])


def _row_shards(g):
    return g.reshape(N_CHIPS, g.shape[0] // N_CHIPS, g.shape[1])


def _mixer_runs(width, cut, n_small):
    runs = []
    for s in range(N_CHIPS):
        lo, hi = s * width, (s + 1) * width
        spans = ((True, lo, min(hi, cut)), (False, max(lo, cut), min(hi, cut + n_small)),
                 (True, max(lo, cut + n_small), hi))
        runs.append([(is_main, a - lo, b - lo) for is_main, a, b in spans if a < b])
    return runs


def _split_mixer_weight(gathered, runs):
    main = [gathered[s][:, a:b] for s, parts in enumerate(runs) for is_main, a, b in parts if is_main]
    small = [gathered[s][:, a:b] for s, parts in enumerate(runs) for is_main, a, b in parts if not is_main]
    small = small[0] if len(small) == 1 else jnp.concatenate(small, axis=1)
    return jnp.concatenate(main, axis=1)[None], jnp.pad(small, ((0, 0), (0, LANES - small.shape[1])))[None]


def _join_mixer_grad(d_main, d_small, runs):
    shards, m, k = [], 0, 0
    for parts in runs:
        cols = []
        for is_main, a, b in parts:
            if is_main:
                cols.append(d_main[:, m:m + b - a])
                m += b - a
            else:
                cols.append(d_small[:, k:k + b - a].astype(d_main.dtype))
                k += b - a
        shards.append(cols[0] if len(cols) == 1 else jnp.concatenate(cols, axis=1))
    return jnp.stack(shards)


def _pad_small(flat):
    n = flat.shape[0]
    block = 8 * LANES
    padded = -(-n // block) * block
    return jnp.pad(flat, (0, padded - n)).reshape(padded // LANES, LANES)


def kernel(x, ffn1_norm, ffn1_w_in, ffn1_w_out, mix_norm, w_in, dn_conv_w, dn_a_log, dn_dt_bias, dn_out_norm, sb_q_norm, sb_k_norm, w_branch_a, w_branch_b, w_out, ffn2_norm, ffn2_w_in, ffn2_w_out, loss_target, m_ffn1_norm, m_ffn1_w_in, m_ffn1_w_out, m_mix_norm, m_w_in, m_dn_conv_w, m_dn_a_log, m_dn_dt_bias, m_dn_out_norm, m_sb_q_norm, m_sb_k_norm, m_w_branch_a, m_w_branch_b, m_w_out, m_ffn2_norm, m_ffn2_w_in, m_ffn2_w_out, v_ffn1_norm, v_ffn1_w_in, v_ffn1_w_out, v_mix_norm, v_w_in, v_dn_conv_w, v_dn_a_log, v_dn_dt_bias, v_dn_out_norm, v_sb_q_norm, v_sb_k_norm, v_w_branch_a, v_w_branch_b, v_w_out, v_ffn2_norm, v_ffn2_w_in, v_ffn2_w_out):
    w = dict(ffn1_norm=ffn1_norm, ffn1_w_in=ffn1_w_in, ffn1_w_out=ffn1_w_out, mix_norm=mix_norm, w_in=w_in,
             dn_conv_w=dn_conv_w, dn_a_log=dn_a_log, dn_dt_bias=dn_dt_bias, dn_out_norm=dn_out_norm,
             sb_q_norm=sb_q_norm, sb_k_norm=sb_k_norm, w_branch_a=w_branch_a, w_branch_b=w_branch_b, w_out=w_out,
             ffn2_norm=ffn2_norm, ffn2_w_in=ffn2_w_in, ffn2_w_out=ffn2_w_out)
    mom = dict(ffn1_norm=m_ffn1_norm, ffn1_w_in=m_ffn1_w_in, ffn1_w_out=m_ffn1_w_out, mix_norm=m_mix_norm, w_in=m_w_in,
               dn_conv_w=m_dn_conv_w, dn_a_log=m_dn_a_log, dn_dt_bias=m_dn_dt_bias, dn_out_norm=m_dn_out_norm,
               sb_q_norm=m_sb_q_norm, sb_k_norm=m_sb_k_norm, w_branch_a=m_w_branch_a, w_branch_b=m_w_branch_b,
               w_out=m_w_out, ffn2_norm=m_ffn2_norm, ffn2_w_in=m_ffn2_w_in, ffn2_w_out=m_ffn2_w_out)
    var = dict(ffn1_norm=v_ffn1_norm, ffn1_w_in=v_ffn1_w_in, ffn1_w_out=v_ffn1_w_out, mix_norm=v_mix_norm, w_in=v_w_in,
               dn_conv_w=v_dn_conv_w, dn_a_log=v_dn_a_log, dn_dt_bias=v_dn_dt_bias, dn_out_norm=v_dn_out_norm,
               sb_q_norm=v_sb_q_norm, sb_k_norm=v_sb_k_norm, w_branch_a=v_w_branch_a, w_branch_b=v_w_branch_b,
               w_out=v_w_out, ffn2_norm=v_ffn2_norm, ffn2_w_in=v_ffn2_w_in, ffn2_w_out=v_ffn2_w_out)

    _, t, d = x.shape
    depth = ffn1_norm.shape[0]
    n_heads = d // HEAD_DIM
    conv_cols = dn_conv_w.shape[2]
    assert d % HEAD_DIM == 0 and t % SB_BLOCK == 0 and 2 * n_heads <= LANES and depth % 2 == 0
    assert w_in.shape[2] * N_CHIPS == 9 * d + 2 * n_heads and conv_cols * N_CHIPS == 3 * d

    x_idx, y_idx, c_idx = _place()
    shard = 2 * x_idx + y_idx
    c_arr = jnp.reshape(c_idx, (1,)).astype(jnp.int32)
    s_arr = jnp.reshape(shard, (1,)).astype(jnp.int32)

    mine = {n: w[n].astype(BF16) for n in BIG}
    runs = _mixer_runs(w_in.shape[2], 4 * d, 2 * n_heads)
    groups = (("ffn1_w_in", "ffn1_w_out"), ("w_in",), ("w_branch_a", "w_branch_b", "w_out"),
              ("ffn2_w_in", "ffn2_w_out"))
    first = groups[0]
    rest = tuple(n for g in groups[1:] for n in g)
    n_gathers = depth + len(groups) - 1

    def gather(names, l, collective_id):
        return dict(zip(names, _allgather_layer([mine[n] for n in names], l, collective_id)))

    arriving = [{}]
    for k, names in enumerate(groups):
        arriving[0].update(gather(names, 0, 0 if k == 0 else depth + k - 1))
    arriving += [gather(BIG, l, l) for l in range(1, depth)]

    def layer_weights(l, names, after):
        gathered, after = lax.optimization_barrier(({n: arriving[l][n] for n in names}, after))
        full = {}
        for n in names:
            g = lax.dynamic_update_slice(gathered[n], mine[n][l][None], (shard, 0, 0))
            if n == "w_in":
                full["w_main"], full["w_ba"] = _split_mixer_weight(g, runs)
            else:
                full[n] = _join_shards(n, g)
        return full, after

    conv_place = lax.dynamic_update_slice(jnp.zeros((depth, DN_CONV, 3 * d), F32), dn_conv_w, (0, 0, shard * conv_cols))
    conv_rows = _pad_small(conv_place.reshape(-1))
    conv_full = (0.5 * _allreduce_small(conv_rows, "allgather_conv")).reshape(-1)[:depth * DN_CONV * 3 * d]
    conv_full = jnp.pad(conv_full.reshape(depth, DN_CONV, 3 * d), ((0, 0), (0, CONV_ROWS - DN_CONV), (0, 0)))

    def head_row(vals):
        return jnp.pad(vals, (n_heads, LANES - 2 * n_heads)).reshape(1, LANES)

    saved, layers = [], []
    cur = x[0]
    for l in range(depth):
        full, x0 = layer_weights(l, groups[0], cur)
        layers.append(full)
        x1, h1 = _ffn_fwd(x0, ffn1_norm[l][None], full["ffn1_w_in"], full["ffn1_w_out"], 0)
        later, x1 = layer_weights(l, groups[1], x1)
        full.update(later)
        proj, ba, hm = _proj_fwd(x1, mix_norm[l][None], full["w_main"], full["w_ba"], 0)
        act = _dn_prep_fwd(proj, conv_full[l], n_heads)
        alog, dtb = head_row(dn_a_log[l]), head_row(dn_dt_bias[l])
        oa, snaps = _delta_fwd(act, ba, alog, dtb, n_heads)
        qb, kb, vb = _sb_prep_fwd(proj, sb_q_norm[l][None], sb_k_norm[l][None], n_heads)
        ob, ltot = _sb_attn_fwd(qb, kb, vb, n_heads)
        later, (oa, ob) = layer_weights(l, groups[2], (oa, ob))
        full.update(later)
        x2 = _merge_fwd(x1, oa, proj, ob, dn_out_norm[l][None], full["w_branch_a"], full["w_branch_b"],
                        full["w_out"], 0, n_heads)
        later, x2 = layer_weights(l, groups[3], x2)
        full.update(later)
        cur, h2 = _ffn_fwd(x2, ffn2_norm[l][None], full["ffn2_w_in"], full["ffn2_w_out"], 0)
        saved.append((x0, x1, proj, ba, act, alog, dtb, oa, snaps, qb, kb, vb, ob, ltot, x2, h1, hm, h2))

    dcur, loss_part = _loss_head(cur, loss_target[0])

    grads = {n: [None] * depth for n in WEIGHTS}
    reduced = {n: None for n in BIG}
    in_flight = []

    def start_reduce(l, names, collective_id):
        g_major = [grads[n][l] for n in names]
        parts = [_add_half(g, got, c_arr) for g, got in zip(g_major, _swap_halves(g_major))]
        return l, names, parts, _scatter_partials(parts, l, collective_id)

    def finish_reduce(started, after):
        done_names, halves = [], []
        for l, names, parts, arrived in started:
            arrived, after = lax.optimization_barrier((arrived, after))
            halves += [_sum_partials(p, got, s_arr, c_arr, l, depth, stacked=reduced[n])
                       for n, p, got in zip(names, parts, arrived)]
            done_names += names
        if started:
            reduced.update(zip(done_names, _join_halves(halves, started[0][0])))
        return after

    for l in reversed(range(depth)):
        x0, x1, proj, ba, act, alog, dtb, oa, snaps, qb, kb, vb, ob, ltot, x2, h1, hm, h2 = saved[l]
        full = layers[l]
        dx2, dg, dwi, dwo = _ffn_bwd(x2, h2, ffn2_norm[l][None], dcur, full["ffn2_w_in"], full["ffn2_w_out"], 0)
        grads["ffn2_norm"][l] = dg[0]
        grads["ffn2_w_in"][l] = dwi
        grads["ffn2_w_out"][l] = _row_shards(dwo)
        doa, dz, dob, dga, dgb, dgn, dwa, dwb, dwout = _merge_bwd(
            oa, proj, ob, dx2, dn_out_norm[l][None], full["w_branch_a"], full["w_branch_b"], full["w_out"], 0,
            n_heads)
        grads["dn_out_norm"][l] = dgn[0]
        grads["w_branch_a"][l], grads["w_branch_b"][l] = _row_shards(dwa), _row_shards(dwb)
        grads["w_out"][l] = _row_shards(dwout)
        dqb, dkb, dvb = _sb_attn_bwd(qb, kb, vb, ltot, dob, n_heads)
        dsq, dsk, dqn, dkn = _sb_prep_bwd(proj, sb_q_norm[l][None], sb_k_norm[l][None], dqb, dkb, n_heads)
        grads["sb_q_norm"][l], grads["sb_k_norm"][l] = dqn[0], dkn[0]
        dact, dba, dal, ddt = _delta_bwd(act, ba, alog, dtb, snaps, doa, n_heads)
        grads["dn_a_log"][l] = dal[0, n_heads:2 * n_heads]
        grads["dn_dt_bias"][l] = ddt[0, n_heads:2 * n_heads]
        dqkv, dconv = _dn_prep_bwd(proj, conv_full[l], dact, n_heads)
        grads["dn_conv_w"][l] = dconv[:DN_CONV]
        dproj = jnp.concatenate([dqkv, dz, dsq, dsk, dvb.astype(BF16), dga, dgb], axis=1)
        dx1, dg, dwm, dwba = _proj_bwd(x1, hm, mix_norm[l][None], dx2, dproj, dba, full["w_main"], full["w_ba"], 0)
        grads["mix_norm"][l] = dg[0]
        grads["w_in"][l] = _join_mixer_grad(dwm, dwba, runs)
        early = start_reduce(l, rest, collective_id=n_gathers + l)
        dcur, dg, dwi, dwo = _ffn_bwd(x0, h1, ffn1_norm[l][None], dx1, full["ffn1_w_in"], full["ffn1_w_out"], 0)
        grads["ffn1_norm"][l] = dg[0]
        grads["ffn1_w_in"][l] = dwi
        grads["ffn1_w_out"][l] = _row_shards(dwo)

        dcur = finish_reduce(in_flight, dcur)
        in_flight = [early, start_reduce(l, first, collective_id=n_gathers + depth + l)]
    dcur = finish_reduce(in_flight, dcur)
    final = reduced
    grads = {n: jnp.stack(grads[n]) for n in SMALL + ("dn_conv_w",)}

    small_names = SMALL + ("dn_conv_w",)
    small_sizes = [int(np.prod(grads[n].shape)) for n in small_names]
    small_off = np.concatenate([[0], np.cumsum(small_sizes)])
    small = jnp.concatenate([grads[n].reshape(-1) for n in small_names] + [loss_part[0, :1]])
    small_sum = _allreduce_small(_pad_small(small), "allreduce_small").reshape(-1)
    for i, n in enumerate(small_names):
        final[n] = small_sum[small_off[i]:small_off[i + 1]].reshape(grads[n].shape)
    final["dn_conv_w"] = lax.dynamic_slice(final["dn_conv_w"], (0, 0, shard * conv_cols), (depth, DN_CONV, conv_cols))
    loss = small_sum[small_off[-1]]

    deltas, new_m, new_v = {}, {}, {}
    for n in WEIGHTS:
        shape = w[n].shape
        flat = (-1, shape[-1])
        dl, m2, v2 = _adamw(w[n].reshape(flat), final[n].reshape(flat), mom[n].reshape(flat), var[n].reshape(flat))
        deltas[n], new_m[n], new_v[n] = dl.reshape(shape), m2.reshape(shape), v2.reshape(shape)

    grad_x = dcur[None]
    return (loss, grad_x, *[final[n] for n in WEIGHTS], *[deltas[n] for n in WEIGHTS],
            *[new_m[n] for n in WEIGHTS], *[new_v[n] for n in WEIGHTS])
```

```python
import functools

import jax
import jax.numpy as jnp
import numpy as np
from jax import lax
from jax.experimental import pallas as pl
from jax.experimental.pallas import tpu as pltpu
from jax.experimental.pallas import tpu_sc as plsc

F32 = jnp.float32
BF16 = jnp.bfloat16

LANES = 128
HEAD_DIM = 128
DN_CHUNK = 64
DN_CONV = 4
CONV_ROWS = 8
SB_BLOCK = 128
FFN_HALF = 0.5
RMS_EPS = 1e-6
L2_EPS = 1e-6
NEG_BIG = -1e30
ADAM_LR = 0.001
ADAM_B1 = 0.9
ADAM_B2 = 0.999
ADAM_EPS = 1e-08
ADAM_WD = 0.01
ADAM_STEP = 10
VMEM_LIMIT = 56 * 1024 * 1024
N_CHIPS = 4
N_DEV = 8
MESH = pl.DeviceIdType.MESH

BIG = ("ffn1_w_in", "ffn1_w_out", "w_in", "w_branch_a", "w_branch_b", "w_out", "ffn2_w_in", "ffn2_w_out")
COL_SHARDED = ("ffn1_w_in", "w_in", "ffn2_w_in")
SMALL = ("ffn1_norm", "mix_norm", "dn_a_log", "dn_dt_bias", "dn_out_norm", "sb_q_norm", "sb_k_norm", "ffn2_norm")
WEIGHTS = ("ffn1_norm", "ffn1_w_in", "ffn1_w_out", "mix_norm", "w_in", "dn_conv_w", "dn_a_log", "dn_dt_bias",
           "dn_out_norm", "sb_q_norm", "sb_k_norm", "w_branch_a", "w_branch_b", "w_out", "ffn2_norm", "ffn2_w_in",
           "ffn2_w_out")


def _params(**kw):
    return pltpu.CompilerParams(vmem_limit_bytes=VMEM_LIMIT, **kw)


def _pick(n, options):
    for o in options:
        if n % o == 0:
            return o
    return n


def _const_spec(shape, single=False):
    nd = len(shape)
    if single:
        return pl.BlockSpec(shape, lambda *_: (0,) * nd, pipeline_mode=pl.Buffered(1))
    return pl.BlockSpec(shape, lambda *_: (0,) * nd)


_NN = ((1,), (0,))
_NT = ((1,), (1,))
_TN = ((0,), (0,))


def _dot(a, b, dims):
    return lax.dot_general(a.astype(BF16), b.astype(BF16), (dims, ((), ())), preferred_element_type=F32)


def _mm_nn(a, b):
    return _dot(a, b, _NN)


def _mm_nt(a, b):
    return _dot(a, b, _NT)


def _mm_tn(a, b):
    return _dot(a, b, _TN)


def _split(a):
    hi = a.astype(BF16)
    lo = (a - hi.astype(F32)).astype(BF16)
    return hi, lo


def _dot_precise(a, b, dims):
    dn = (dims, ((), ()))
    ah, al = _split(a)
    bh, bl = _split(b)
    out = lax.dot_general(ah, bh, dn, preferred_element_type=F32)
    out = out + lax.dot_general(ah, bl, dn, preferred_element_type=F32)
    return out + lax.dot_general(al, bh, dn, preferred_element_type=F32)


def _make_diff_mm(dot):
    @jax.custom_vjp
    def nn(a, b):
        return dot(a, b, _NN)

    @jax.custom_vjp
    def nt(a, b):
        return dot(a, b, _NT)

    @jax.custom_vjp
    def tn(a, b):
        return dot(a, b, _TN)

    nn.defvjp(lambda a, b: (dot(a, b, _NN), (a, b)), lambda r, g: (nt(g, r[1]), tn(r[0], g)))
    nt.defvjp(lambda a, b: (dot(a, b, _NT), (a, b)), lambda r, g: (nn(g, r[1]), tn(g, r[0])))
    tn.defvjp(lambda a, b: (dot(a, b, _TN), (a, b)), lambda r, g: (nt(r[1], g), nn(r[0], g)))
    return nn, nt, tn


_d_nn, _d_nt, _d_tn = _make_diff_mm(_dot)
_p_nn, _p_nt, _p_tn = _make_diff_mm(_dot_precise)


@jax.custom_vjp
def _halves(x):
    n = x.shape[0] // 2
    return x[:n], x[n:]


_halves.defvjp(lambda x: (_halves(x), None), lambda _, g: (jnp.concatenate(g, axis=0),))


@jax.custom_vjp
def _col_halves(x):
    n = x.shape[1] // 2
    return x[:, :n], x[:, n:]


_col_halves.defvjp(lambda x: (_col_halves(x), None), lambda _, g: (jnp.concatenate(g, axis=1),))


def _softplus_raw(x):
    return jnp.maximum(x, 0.0) + jnp.log(1.0 + jnp.exp(-jnp.abs(x)))


@jax.custom_vjp
def _softplus(x):
    return _softplus_raw(x)


_softplus.defvjp(lambda x: (_softplus_raw(x), x), lambda x, g: (g * jax.nn.sigmoid(x),))


def _rms(x, gain, eps):
    return x * lax.rsqrt(jnp.mean(x * x, axis=-1, keepdims=True) + eps) * gain


def _silu(x):
    return x * jax.nn.sigmoid(x)


def _shift_rows_raw(x, k, down):
    n = x.shape[0]
    row = lax.broadcasted_iota(jnp.int32, x.shape, 0)
    if down:
        return jnp.where(row >= k, pltpu.roll(x, k, 0), 0.0)
    return jnp.where(row < n - k, pltpu.roll(x, n - k, 0), 0.0)


@functools.partial(jax.custom_vjp, nondiff_argnums=(1,))
def _shift_down(x, k):
    return _shift_rows_raw(x, k, True)


_shift_down.defvjp(lambda x, k: (_shift_rows_raw(x, k, True), None),
                   lambda k, _, g: (_shift_rows_raw(g, k, False),))


def _layer_spec(layer, block, index_map, single=False):
    full_map = lambda *a: (layer,) + tuple(index_map(*a))
    if single:
        return pl.BlockSpec((None,) + block, full_map, pipeline_mode=pl.Buffered(1))
    return pl.BlockSpec((None,) + block, full_map)


def _ffn_fwd(x, gain, w_in, w_out, layer):
    t, d = x.shape
    f = w_out.shape[1]
    fc = _pick(f, (256, 128))
    nj = f // fc
    rt = _pick(t, (512, 256, 128))

    def body(x_ref, g_ref, wg_ref, wu_ref, wo_ref, o_ref, hs_ref):
        @pl.when(pl.program_id(0) == 0)
        def _():
            for r in range(t // rt):
                rows = pl.ds(r * rt, rt)
                xr = x_ref[rows, :]
                hs_ref[rows, :] = _rms(xr, g_ref[...], RMS_EPS).astype(BF16)
                o_ref[rows, :] = xr

        for r in range(t // rt):
            rows = pl.ds(r * rt, rt)
            h = hs_ref[rows, :]
            a = _mm_nn(h, wg_ref[...])
            b = _mm_nn(h, wu_ref[...])
            o_ref[rows, :] += FFN_HALF * _mm_nn(_silu(a) * b, wo_ref[...])

    return pl.pallas_call(
        body, name="ffn_fwd", grid=(nj,),
        in_specs=[_const_spec((t, d), True), _const_spec((1, d)),
                  _layer_spec(layer, (d, fc), lambda j: (0, j)), _layer_spec(layer, (d, fc), lambda j: (0, nj + j)),
                  _layer_spec(layer, (fc, d), lambda j: (j, 0))],
        out_specs=[_const_spec((t, d)), _const_spec((t, d))],
        out_shape=[jax.ShapeDtypeStruct((t, d), F32), jax.ShapeDtypeStruct((t, d), BF16)],
        compiler_params=_params(dimension_semantics=("arbitrary",)),
    )(x, gain, w_in, w_in, w_out)


def _ffn_bwd(x, hs, gain, dy, w_in, w_out, layer):
    t, d = x.shape
    f = w_out.shape[1]
    fc = _pick(f, (256, 128))
    nj = f // fc
    rt = _pick(t, (512, 256, 128))
    nr = t // rt

    def body(x_ref, hs_ref, g_ref, dy_ref, wg_ref, wu_ref, wo_ref, dx_ref, dg_ref, dwi_ref, dwo_ref,
             dwg_acc, dwu_acc, dwo_acc):
        j = pl.program_id(0)

        @pl.when(j == 0)
        def _():
            for r in range(nr):
                dx_ref[pl.ds(r * rt, rt), :] = jnp.zeros((rt, d), F32)

        for r in range(nr):
            rows = pl.ds(r * rt, rt)
            h = hs_ref[rows, :]
            dy2 = (FFN_HALF * dy_ref[rows, :]).astype(BF16)
            a = _mm_nn(h, wg_ref[...])
            b = _mm_nn(h, wu_ref[...])
            sig = jax.nn.sigmoid(a)
            sa = a * sig
            ds = _mm_nt(dy2, wo_ref[...])
            da = ds * b * (sig * (1.0 + a * (1.0 - sig)))
            db = ds * sa
            dx_ref[rows, :] += _mm_nt(da, wg_ref[...]) + _mm_nt(db, wu_ref[...])
            dwo_c = _mm_tn(sa * b, dy2)
            dwg_c = _mm_tn(h, da)
            dwu_c = _mm_tn(h, db)
            if r == 0:
                dwo_acc[...] = dwo_c
                dwg_acc[...] = dwg_c
                dwu_acc[...] = dwu_c
            else:
                dwo_acc[...] += dwo_c
                dwg_acc[...] += dwg_c
                dwu_acc[...] += dwu_c
        dwi_ref[0] = dwg_acc[...].astype(BF16)
        dwi_ref[1] = dwu_acc[...].astype(BF16)
        dwo_ref[...] = dwo_acc[...].astype(BF16)

        @pl.when(j == nj - 1)
        def _():
            for r in range(nr):
                rows = pl.ds(r * rt, rt)
                _, vjp = jax.vjp(lambda xx, gg: _rms(xx, gg, RMS_EPS), x_ref[rows, :], g_ref[...])
                dxn, dgr = vjp(dx_ref[rows, :])
                dx_ref[rows, :] = dy_ref[rows, :] + dxn
                if r == 0:
                    dg_ref[...] = dgr
                else:
                    dg_ref[...] += dgr

    return pl.pallas_call(
        body, name="ffn_bwd", grid=(nj,),
        in_specs=[_const_spec((t, d), True), _const_spec((t, d), True), _const_spec((1, d)), _const_spec((t, d), True),
                  _layer_spec(layer, (d, fc), lambda j: (0, j)), _layer_spec(layer, (d, fc), lambda j: (0, nj + j)),
                  _layer_spec(layer, (fc, d), lambda j: (j, 0))],
        out_specs=[_const_spec((t, d)), _const_spec((1, d)),
                   pl.BlockSpec((2, d, fc), lambda j: (0, 0, j)), pl.BlockSpec((fc, d), lambda j: (j, 0))],
        out_shape=[jax.ShapeDtypeStruct((t, d), F32), jax.ShapeDtypeStruct((1, d), F32),
                   jax.ShapeDtypeStruct((2, d, f), BF16), jax.ShapeDtypeStruct((f, d), BF16)],
        scratch_shapes=[pltpu.VMEM((d, fc), F32), pltpu.VMEM((d, fc), F32), pltpu.VMEM((fc, d), F32)],
        compiler_params=_params(dimension_semantics=("arbitrary",)),
    )(x, hs, gain, dy, w_in, w_in, w_out)


def _proj_fwd(x, gain, w, wba, layer):
    t, d = x.shape
    n = w.shape[2]
    nc = _pick(n, (512, 256, 128))
    rt = _pick(t, (512, 256, 128))

    def body(x_ref, g_ref, w_ref, wba_ref, p_ref, ba_ref, hs_ref):
        @pl.when(pl.program_id(0) == 0)
        def _():
            for r in range(t // rt):
                rows = pl.ds(r * rt, rt)
                h = _rms(x_ref[rows, :], g_ref[...], RMS_EPS).astype(BF16)
                hs_ref[rows, :] = h
                ba_ref[rows, :] = _mm_nn(h, wba_ref[...])

        for r in range(t // rt):
            rows = pl.ds(r * rt, rt)
            p_ref[rows, :] = _mm_nn(hs_ref[rows, :], w_ref[...])

    return pl.pallas_call(
        body, name="proj_fwd", grid=(n // nc,),
        in_specs=[_const_spec((t, d), True), _const_spec((1, d)),
                  _layer_spec(layer, (d, nc), lambda j: (0, j)), _layer_spec(layer, (d, LANES), lambda j: (0, 0))],
        out_specs=[pl.BlockSpec((t, nc), lambda j: (0, j)), _const_spec((t, LANES)), _const_spec((t, d))],
        out_shape=[jax.ShapeDtypeStruct((t, n), F32), jax.ShapeDtypeStruct((t, LANES), F32),
                   jax.ShapeDtypeStruct((t, d), BF16)],
        compiler_params=_params(dimension_semantics=("arbitrary",)),
    )(x, gain, w, wba)


def _proj_bwd(x, hs, gain, dres, dp, dba, w, wba, layer):
    t, d = x.shape
    n = w.shape[2]
    nc = _pick(n, (512, 256, 128))
    nj = n // nc
    rt = _pick(t, (512, 256, 128))
    nr = t // rt

    def body(x_ref, hs_ref, g_ref, dres_ref, dp_ref, dba_ref, w_ref, wba_ref, dx_ref, dg_ref, dw_ref, dwba_ref, dw_acc):
        j = pl.program_id(0)

        @pl.when(j == 0)
        def _():
            for r in range(nr):
                rows = pl.ds(r * rt, rt)
                h = hs_ref[rows, :]
                g = dba_ref[rows, :]
                dx_ref[rows, :] = _mm_nt(g, wba_ref[...])
                if r == 0:
                    dwba_ref[...] = _mm_tn(h, g)
                else:
                    dwba_ref[...] += _mm_tn(h, g)

        for r in range(nr):
            rows = pl.ds(r * rt, rt)
            g = dp_ref[rows, :]
            dx_ref[rows, :] += _mm_nt(g, w_ref[...])
            if r == 0:
                dw_acc[...] = _mm_tn(hs_ref[rows, :], g)
            else:
                dw_acc[...] += _mm_tn(hs_ref[rows, :], g)
        dw_ref[...] = dw_acc[...].astype(BF16)

        @pl.when(j == nj - 1)
        def _():
            for r in range(nr):
                rows = pl.ds(r * rt, rt)
                _, vjp = jax.vjp(lambda xx, gg: _rms(xx, gg, RMS_EPS), x_ref[rows, :], g_ref[...])
                dxn, dgr = vjp(dx_ref[rows, :])
                dx_ref[rows, :] = dres_ref[rows, :] + dxn
                if r == 0:
                    dg_ref[...] = dgr
                else:
                    dg_ref[...] += dgr

    return pl.pallas_call(
        body, name="proj_bwd", grid=(nj,),
        in_specs=[_const_spec((t, d), True), _const_spec((t, d), True), _const_spec((1, d)), _const_spec((t, d), True),
                  pl.BlockSpec((t, nc), lambda j: (0, j)), _const_spec((t, LANES)),
                  _layer_spec(layer, (d, nc), lambda j: (0, j)), _layer_spec(layer, (d, LANES), lambda j: (0, 0))],
        out_specs=[_const_spec((t, d)), _const_spec((1, d)),
                   pl.BlockSpec((d, nc), lambda j: (0, j)), _const_spec((d, LANES))],
        out_shape=[jax.ShapeDtypeStruct((t, d), F32), jax.ShapeDtypeStruct((1, d), F32),
                   jax.ShapeDtypeStruct((d, n), BF16), jax.ShapeDtypeStruct((d, LANES), F32)],
        scratch_shapes=[pltpu.VMEM((d, nc), F32)],
        compiler_params=_params(dimension_semantics=("arbitrary",)),
    )(x, hs, gain, dres, dp, dba, w, wba)


def _conv_act(x, w0, w1, w2, w3, is_qk):
    y = w3 * x + w2 * _shift_down(x, 1) + w1 * _shift_down(x, 2) + w0 * _shift_down(x, 3)
    y = _silu(y)
    inv = lax.rsqrt(jnp.sum(y * y, axis=-1, keepdims=True) + L2_EPS)
    return y * (is_qk * inv + (1.0 - is_qk))


def _taps(w_ref):
    return tuple(w_ref[i:i + 1, :] for i in range(DN_CONV))


def _dn_prep_fwd(proj, conv_w, n_heads):
    t = proj.shape[0]
    nb = 3 * n_heads

    def body(x_ref, w_ref, o_ref):
        is_qk = jnp.where(pl.program_id(0) < 2 * n_heads, 1.0, 0.0).astype(F32)
        o_ref[...] = _conv_act(x_ref[...], *_taps(w_ref), is_qk)

    return pl.pallas_call(
        body, name="dn_prep_fwd", grid=(nb,),
        in_specs=[pl.BlockSpec((t, HEAD_DIM), lambda i: (0, i)), pl.BlockSpec((CONV_ROWS, HEAD_DIM), lambda i: (0, i))],
        out_specs=pl.BlockSpec((t, HEAD_DIM), lambda i: (0, i)),
        out_shape=jax.ShapeDtypeStruct((t, nb * HEAD_DIM), F32),
        compiler_params=_params(dimension_semantics=("arbitrary",)),
    )(proj, conv_w)


def _dn_prep_bwd(proj, conv_w, dact, n_heads):
    t = proj.shape[0]
    nb = 3 * n_heads

    def body(x_ref, w_ref, g_ref, dx_ref, dw_ref):
        is_qk = jnp.where(pl.program_id(0) < 2 * n_heads, 1.0, 0.0).astype(F32)
        _, vjp = jax.vjp(lambda x, a, b, c, e: _conv_act(x, a, b, c, e, is_qk), x_ref[...], *_taps(w_ref))
        dx, d0, d1, d2, d3 = vjp(g_ref[...])
        dx_ref[...] = dx.astype(BF16)
        dw_ref[...] = jnp.concatenate([d0, d1, d2, d3, jnp.zeros((CONV_ROWS - DN_CONV, HEAD_DIM), F32)], axis=0)

    return pl.pallas_call(
        body, name="dn_prep_bwd", grid=(nb,),
        in_specs=[pl.BlockSpec((t, HEAD_DIM), lambda i: (0, i)), pl.BlockSpec((CONV_ROWS, HEAD_DIM), lambda i: (0, i)),
                  pl.BlockSpec((t, HEAD_DIM), lambda i: (0, i))],
        out_specs=[pl.BlockSpec((t, HEAD_DIM), lambda i: (0, i)), pl.BlockSpec((CONV_ROWS, HEAD_DIM), lambda i: (0, i))],
        out_shape=[jax.ShapeDtypeStruct((t, nb * HEAD_DIM), BF16), jax.ShapeDtypeStruct((CONV_ROWS, nb * HEAD_DIM), F32)],
        compiler_params=_params(dimension_semantics=("arbitrary",)),
    )(proj, conv_w, dact)


def _unit_lower_inverses(lmats, c):
    r = lax.broadcasted_iota(jnp.int32, (c, c), 0)
    q = lax.broadcasted_iota(jnp.int32, (c, c), 1)
    eye = jnp.where(r == q, 1.0, 0.0)
    ps = [eye - l for l in lmats]
    ms = [_p_nn(l, l) for l in lmats]
    n = 2
    while 2 * n < c:
        both = [_halves(_p_nn(jnp.concatenate([p, m], axis=0), m)) for p, m in zip(ps, ms)]
        ps = [p + pm for p, (pm, _) in zip(ps, both)]
        ms = [mm for _, mm in both]
        n *= 2
    return [p + _p_nn(p, m) for p, m in zip(ps, ms)]


def _delta_heads(qs, ks, vs, bg, alog, dtb, states):
    n_heads = len(qs)
    heads = range(n_heads)
    c = qs[0].shape[0]
    lane = lax.broadcasted_iota(jnp.int32, (c, LANES), 1)
    r = lax.broadcasted_iota(jnp.int32, (c, c), 0)
    s = lax.broadcasted_iota(jnp.int32, (c, c), 1)
    beta_all = jax.nn.sigmoid(bg)
    g_all = -jnp.exp(alog) * _softplus(bg + dtb)
    beta = [jnp.sum(jnp.where(lane == h, beta_all, 0.0), axis=1, keepdims=True) for h in heads]
    g = [jnp.sum(jnp.where(lane == n_heads + h, g_all, 0.0), axis=1, keepdims=True) for h in heads]
    g_row = [jnp.sum(jnp.where(r == s, g[h], 0.0), axis=0, keepdims=True) for h in heads]
    gc = [jnp.sum(jnp.where(s <= r, g_row[h], 0.0), axis=1, keepdims=True) for h in heads]
    gr = [jnp.sum(jnp.where(r <= s, g[h], 0.0), axis=0, keepdims=True) for h in heads]
    g_last = [jnp.sum(g[h], axis=0, keepdims=True) for h in heads]
    decay = [jnp.exp(jnp.where(r >= s, gc[h] - gr[h], NEG_BIG)) for h in heads]
    q_scaled = [qs[h] * (HEAD_DIM ** -0.5) for h in heads]
    k_beta = [ks[h] * beta[h] for h in heads]
    kk_qk = [_halves(_d_nt(jnp.concatenate([k_beta[h], q_scaled[h]], axis=0), ks[h])) for h in heads]
    lmat = [jnp.where(r > s, kk_qk[h][0] * decay[h], 0.0) for h in heads]
    attn = [kk_qk[h][1] * decay[h] for h in heads]
    tinv = _unit_lower_inverses(lmat, c)
    u_w = [_col_halves(_p_nn(tinv[h], jnp.concatenate([vs[h] * beta[h], k_beta[h] * jnp.exp(gc[h])], axis=1)))
           for h in heads]
    u = [u_w[h][0] for h in heads]
    w = [u_w[h][1] for h in heads]
    ws_qs = [_halves(_d_nn(jnp.concatenate([w[h], q_scaled[h] * jnp.exp(gc[h])], axis=0), states[h])) for h in heads]
    v_new = [u[h] - ws_qs[h][0] for h in heads]
    o = [ws_qs[h][1] + _d_nn(attn[h], v_new[h]) for h in heads]
    kv = [_d_tn(ks[h] * jnp.exp(g_last[h] - gc[h]), v_new[h]) for h in heads]
    new_states = [states[h] * jnp.exp(g_last[h]) + kv[h] for h in heads]
    return tuple(o), tuple(new_states)


def _delta_fwd(act, ba, alog, dtb, n_heads):
    t = act.shape[0]
    d = n_heads * HEAD_DIM
    c = DN_CHUNK
    nc = t // c

    def body(q_ref, k_ref, v_ref, bg_ref, al_ref, dt_ref, o_ref, snap_ref, st_ref):
        @pl.when(pl.program_id(0) == 0)
        def _():
            st_ref[...] = jnp.zeros(st_ref.shape, F32)

        snap_ref[0] = st_ref[...]
        cols = [slice(h * HEAD_DIM, (h + 1) * HEAD_DIM) for h in range(n_heads)]
        os, new_states = _delta_heads([q_ref[:, sl] for sl in cols], [k_ref[:, sl] for sl in cols],
                                      [v_ref[:, sl] for sl in cols], bg_ref[...], al_ref[...], dt_ref[...],
                                      [st_ref[h] for h in range(n_heads)])
        for h, sl in enumerate(cols):
            o_ref[:, sl] = os[h]
            st_ref[h] = new_states[h]

    return pl.pallas_call(
        body, name="delta_fwd", grid=(nc,),
        in_specs=[pl.BlockSpec((c, d), lambda i: (i, 0)), pl.BlockSpec((c, d), lambda i: (i, 1)),
                  pl.BlockSpec((c, d), lambda i: (i, 2)), pl.BlockSpec((c, LANES), lambda i: (i, 0)),
                  _const_spec((1, LANES)), _const_spec((1, LANES))],
        out_specs=[pl.BlockSpec((c, d), lambda i: (i, 0)),
                   pl.BlockSpec((1, n_heads, HEAD_DIM, HEAD_DIM), lambda i: (i, 0, 0, 0))],
        out_shape=[jax.ShapeDtypeStruct((t, d), F32), jax.ShapeDtypeStruct((nc, n_heads, HEAD_DIM, HEAD_DIM), F32)],
        scratch_shapes=[pltpu.VMEM((n_heads, HEAD_DIM, HEAD_DIM), F32)],
        compiler_params=_params(dimension_semantics=("arbitrary",)),
    )(act, act, act, ba, alog, dtb)


def _delta_bwd(act, ba, alog, dtb, snaps, do, n_heads):
    t = act.shape[0]
    d = n_heads * HEAD_DIM
    c = DN_CHUNK
    nc = t // c

    def body(q_ref, k_ref, v_ref, bg_ref, al_ref, dt_ref, snap_ref, do_ref,
             dact_ref, dbg_ref, dal_ref, ddt_ref, ds_ref):
        @pl.when(pl.program_id(0) == 0)
        def _():
            ds_ref[...] = jnp.zeros(ds_ref.shape, F32)
            dal_ref[...] = jnp.zeros((1, LANES), F32)
            ddt_ref[...] = jnp.zeros((1, LANES), F32)

        heads = range(n_heads)
        cols = [slice(h * HEAD_DIM, (h + 1) * HEAD_DIM) for h in heads]
        _, vjp = jax.vjp(_delta_heads, tuple(q_ref[:, sl] for sl in cols), tuple(k_ref[:, sl] for sl in cols),
                         tuple(v_ref[:, sl] for sl in cols), bg_ref[...], al_ref[...], dt_ref[...],
                         tuple(snap_ref[0, h] for h in heads))
        dq, dk, dv, dbg, dal, ddt, dst = vjp((tuple(do_ref[:, sl] for sl in cols), tuple(ds_ref[h] for h in heads)))
        for h, sl in enumerate(cols):
            dact_ref[:, sl] = dq[h]
            dact_ref[:, d + h * HEAD_DIM:d + (h + 1) * HEAD_DIM] = dk[h]
            dact_ref[:, 2 * d + h * HEAD_DIM:2 * d + (h + 1) * HEAD_DIM] = dv[h]
            ds_ref[h] = dst[h]
        dal_ref[...] += dal
        ddt_ref[...] += ddt
        dbg_ref[...] = dbg.astype(BF16)

    rev = lambda i: nc - 1 - i
    return pl.pallas_call(
        body, name="delta_bwd", grid=(nc,),
        in_specs=[pl.BlockSpec((c, d), lambda i: (rev(i), 0)), pl.BlockSpec((c, d), lambda i: (rev(i), 1)),
                  pl.BlockSpec((c, d), lambda i: (rev(i), 2)), pl.BlockSpec((c, LANES), lambda i: (rev(i), 0)),
                  _const_spec((1, LANES)), _const_spec((1, LANES)),
                  pl.BlockSpec((1, n_heads, HEAD_DIM, HEAD_DIM), lambda i: (rev(i), 0, 0, 0)),
                  pl.BlockSpec((c, d), lambda i: (rev(i), 0))],
        out_specs=[pl.BlockSpec((c, 3 * d), lambda i: (rev(i), 0)), pl.BlockSpec((c, LANES), lambda i: (rev(i), 0)),
                   _const_spec((1, LANES)), _const_spec((1, LANES))],
        out_shape=[jax.ShapeDtypeStruct((t, 3 * d), F32), jax.ShapeDtypeStruct((t, LANES), BF16),
                   jax.ShapeDtypeStruct((1, LANES), F32), jax.ShapeDtypeStruct((1, LANES), F32)],
        scratch_shapes=[pltpu.VMEM((n_heads, HEAD_DIM, HEAD_DIM), F32)],
        compiler_params=_params(dimension_semantics=("arbitrary",)),
    )(act, act, act, ba, alog, dtb, snaps, do)


def _head_norm2(a, b, ga, gb):
    return _rms(a, ga, RMS_EPS), _rms(b, gb, RMS_EPS)


def _sb_prep_fwd(proj, qn, kn, n_heads):
    t = proj.shape[0]
    d = n_heads * HEAD_DIM
    tm = _pick(t, (256, 128))

    def body(q_ref, k_ref, v_ref, qn_ref, kn_ref, qo_ref, ko_ref, vo_ref):
        for h in range(n_heads):
            sl = slice(h * HEAD_DIM, (h + 1) * HEAD_DIM)
            qh, kh = _head_norm2(q_ref[:, sl], k_ref[:, sl], qn_ref[...], kn_ref[...])
            qo_ref[:, sl] = qh.astype(BF16)
            ko_ref[:, sl] = kh.astype(BF16)
        vo_ref[...] = v_ref[...].astype(BF16)

    tile = lambda k: pl.BlockSpec((tm, d), lambda i: (i, k))
    return pl.pallas_call(
        body, name="sb_prep_fwd", grid=(t // tm,),
        in_specs=[tile(4), tile(5), tile(6), _const_spec((1, HEAD_DIM)), _const_spec((1, HEAD_DIM))],
        out_specs=[tile(0), tile(0), tile(0)],
        out_shape=[jax.ShapeDtypeStruct((t, d), BF16)] * 3,
        compiler_params=_params(dimension_semantics=("arbitrary",)),
    )(proj, proj, proj, qn, kn)


def _sb_prep_bwd(proj, qn, kn, dq, dk, n_heads):
    t = proj.shape[0]
    d = n_heads * HEAD_DIM
    tm = _pick(t, (256, 128))

    def body(q_ref, k_ref, qn_ref, kn_ref, dq_ref, dk_ref, dqo_ref, dko_ref, dqn_ref, dkn_ref):
        @pl.when(pl.program_id(0) == 0)
        def _():
            dqn_ref[...] = jnp.zeros((1, HEAD_DIM), F32)
            dkn_ref[...] = jnp.zeros((1, HEAD_DIM), F32)

        for h in range(n_heads):
            sl = slice(h * HEAD_DIM, (h + 1) * HEAD_DIM)
            _, vjp = jax.vjp(_head_norm2, q_ref[:, sl], k_ref[:, sl], qn_ref[...], kn_ref[...])
            da, db, dga, dgb = vjp((dq_ref[:, sl], dk_ref[:, sl]))
            dqo_ref[:, sl] = da.astype(BF16)
            dko_ref[:, sl] = db.astype(BF16)
            dqn_ref[...] += dga
            dkn_ref[...] += dgb

    return pl.pallas_call(
        body, name="sb_prep_bwd", grid=(t // tm,),
        in_specs=[pl.BlockSpec((tm, d), lambda i: (i, 4)), pl.BlockSpec((tm, d), lambda i: (i, 5)),
                  _const_spec((1, HEAD_DIM)), _const_spec((1, HEAD_DIM)),
                  pl.BlockSpec((tm, d), lambda i: (i, 0)), pl.BlockSpec((tm, d), lambda i: (i, 0))],
        out_specs=[pl.BlockSpec((tm, d), lambda i: (i, 0)), pl.BlockSpec((tm, d), lambda i: (i, 0)),
                   _const_spec((1, HEAD_DIM)), _const_spec((1, HEAD_DIM))],
        out_shape=[jax.ShapeDtypeStruct((t, d), BF16), jax.ShapeDtypeStruct((t, d), BF16),
                   jax.ShapeDtypeStruct((1, HEAD_DIM), F32), jax.ShapeDtypeStruct((1, HEAD_DIM), F32)],
        compiler_params=_params(dimension_semantics=("arbitrary",)),
    )(proj, proj, qn, kn, dq, dk)


def _cumsum_mm(x, tri):
    hi, lo = _split(x)
    return (lax.dot_general(hi, tri, (_NN, ((), ())), preferred_element_type=F32)
            + lax.dot_general(lo, tri, (_NN, ((), ())), preferred_element_type=F32))


def _sb_valid(i, j):
    row = lax.broadcasted_iota(jnp.int32, (SB_BLOCK, SB_BLOCK), 0)
    col = lax.broadcasted_iota(jnp.int32, (SB_BLOCK, SB_BLOCK), 1)
    return (col + j * SB_BLOCK) < (row + i * SB_BLOCK)


def _sb_attn_fwd(qb, kb, vb, n_heads):
    t = qb.shape[0]
    d = n_heads * HEAD_DIM
    nq = t // SB_BLOCK
    hb = _pick(n_heads, (4, 2, 1))
    wide = hb * HEAD_DIM
    scale = HEAD_DIM ** -0.5

    def body(q_ref, k_ref, v_ref, o_ref, lt_ref):
        i = pl.program_id(1)
        row = lax.broadcasted_iota(jnp.int32, (SB_BLOCK, SB_BLOCK), 0)
        col = lax.broadcasted_iota(jnp.int32, (SB_BLOCK, SB_BLOCK), 1)
        after = jnp.where(row > col, 1.0, 0.0).astype(BF16)
        heads = [slice(h * HEAD_DIM, (h + 1) * HEAD_DIM) for h in range(hb)]
        every = range(hb)
        qs = [q_ref[:, sl].astype(BF16) for sl in heads]

        def step(m, carry):
            j0 = i - 2 * m
            js = (j0, jnp.maximum(j0 - 1, 0))
            valid = (_sb_valid(i, js[0]), jnp.logical_and(_sb_valid(i, js[1]), j0 >= 1))
            rows = [pl.ds(pl.multiple_of(j * SB_BLOCK, SB_BLOCK), SB_BLOCK) for j in js]
            units = [(h, b) for b in range(2) for h in every]
            z = {u: _mm_nt(qs[u[0]], k_ref[rows[u[1]], heads[u[0]]]) * scale for u in units}
            sp = {u: _softplus_raw(z[u]) for u in units}
            lm = {u: jnp.where(valid[u[1]], -sp[u], 0.0) for u in units}
            tail = {u: _cumsum_mm(lm[u], after) for u in units}
            later = {(h, 0): carry[h][1] for h in every}
            later.update({(h, 1): carry[h][1] + jnp.sum(lm[h, 0], axis=1, keepdims=True) for h in every})
            w = {u: jnp.where(valid[u[1]], jnp.exp(z[u] - sp[u] + later[u] + tail[u]), 0.0) for u in units}
            pv = {u: _mm_nn(w[u], v_ref[rows[u[1]], heads[u[0]]]) for u in units}
            return tuple((carry[h][0] + pv[h, 0] + pv[h, 1], later[h, 1] + jnp.sum(lm[h, 1], axis=1, keepdims=True))
                         for h in every)

        init = tuple((jnp.zeros((SB_BLOCK, HEAD_DIM), F32), jnp.zeros((SB_BLOCK, 1), F32)) for _ in heads)
        res = lax.fori_loop(0, (i + 2) // 2, step, init)
        for h, sl in enumerate(heads):
            o_ref[:, sl] = res[h][0]
            lt_ref[:, sl] = jnp.broadcast_to(res[h][1], (SB_BLOCK, HEAD_DIM))

    return pl.pallas_call(
        body, name="sb_attn_fwd", grid=(n_heads // hb, nq),
        in_specs=[pl.BlockSpec((SB_BLOCK, wide), lambda g, i: (i, g)),
                  pl.BlockSpec((t, wide), lambda g, i: (0, g)),
                  pl.BlockSpec((t, wide), lambda g, i: (0, g))],
        out_specs=[pl.BlockSpec((SB_BLOCK, wide), lambda g, i: (i, g)),
                   pl.BlockSpec((SB_BLOCK, wide), lambda g, i: (i, g))],
        out_shape=[jax.ShapeDtypeStruct((t, d), F32), jax.ShapeDtypeStruct((t, d), F32)],
        compiler_params=_params(dimension_semantics=("arbitrary", "arbitrary")),
    )(qb, kb, vb)


def _sb_attn_bwd(qb, kb, vb, ltot, do, n_heads):
    t = qb.shape[0]
    d = n_heads * HEAD_DIM
    nq = t // SB_BLOCK
    hb = _pick(n_heads, (4, 2, 1))
    wide = hb * HEAD_DIM
    scale = HEAD_DIM ** -0.5

    def body(q_ref, k_ref, v_ref, lt_ref, do_ref, dq_ref, dk_ref, dv_ref):
        i = pl.program_id(1)

        @pl.when(i == 0)
        def _():
            dk_ref[...] = jnp.zeros((t, wide), F32)
            dv_ref[...] = jnp.zeros((t, wide), F32)

        row = lax.broadcasted_iota(jnp.int32, (SB_BLOCK, SB_BLOCK), 0)
        col = lax.broadcasted_iota(jnp.int32, (SB_BLOCK, SB_BLOCK), 1)
        upto = jnp.where(row <= col, 1.0, 0.0).astype(BF16)
        before = jnp.where(row < col, 1.0, 0.0).astype(BF16)
        heads = [slice(h * HEAD_DIM, (h + 1) * HEAD_DIM) for h in range(hb)]
        every = range(hb)
        qs = [q_ref[:, sl].astype(BF16) for sl in heads]
        dos = [do_ref[:, sl].astype(BF16) for sl in heads]
        totals = [jnp.max(lt_ref[:, sl], axis=1, keepdims=True) for sl in heads]

        def step(m, carry):
            js = (2 * m, jnp.minimum(2 * m + 1, nq - 1))
            valid = (_sb_valid(i, js[0]), jnp.logical_and(_sb_valid(i, js[1]), 2 * m + 1 <= i))
            rows = [pl.ds(pl.multiple_of(j * SB_BLOCK, SB_BLOCK), SB_BLOCK) for j in js]
            units = [(h, b) for b in range(2) for h in every]
            kj = {u: k_ref[rows[u[1]], heads[u[0]]].astype(BF16) for u in units}
            vj = {u: v_ref[rows[u[1]], heads[u[0]]].astype(BF16) for u in units}
            z = {u: _mm_nt(qs[u[0]], kj[u]) * scale for u in units}
            dw = {u: _mm_nt(dos[u[0]], vj[u]) for u in units}
            sp = {u: _softplus_raw(z[u]) for u in units}
            lm = {u: jnp.where(valid[u[1]], -sp[u], 0.0) for u in units}
            head = {u: _cumsum_mm(lm[u], upto) for u in units}
            lm_before = {(h, 0): carry[h][1] for h in every}
            lm_before.update({(h, 1): carry[h][1] + jnp.sum(lm[h, 0], axis=1, keepdims=True) for h in every})
            w = {u: jnp.where(valid[u[1]], jnp.exp(z[u] - sp[u] + totals[u[0]] - (lm_before[u] + head[u])), 0.0)
                 for u in units}
            e = {u: w[u] * dw[u] for u in units}
            e_local = {u: _mm_nn(e[u], before) for u in units}
            e_before = {(h, 0): carry[h][2] for h in every}
            e_before.update({(h, 1): carry[h][2] + jnp.sum(e[h, 0], axis=1, keepdims=True) for h in every})
            sig = {u: jnp.exp(z[u] - sp[u]) for u in units}
            dz = {u: jnp.where(valid[u[1]], e[u] * (1.0 - sig[u]) - (e_before[u] + e_local[u]) * sig[u], 0.0) * scale
                  for u in units}
            for h, b in units:
                dv_ref[rows[b], heads[h]] += _mm_tn(w[h, b], dos[h])
            for h, b in units:
                dk_ref[rows[b], heads[h]] += _mm_tn(dz[h, b], qs[h])
            dq = {u: _mm_nn(dz[u], kj[u]) for u in units}
            return tuple((carry[h][0] + dq[h, 0] + dq[h, 1],
                          lm_before[h, 1] + jnp.sum(lm[h, 1], axis=1, keepdims=True),
                          e_before[h, 1] + jnp.sum(e[h, 1], axis=1, keepdims=True)) for h in every)

        zero_col = jnp.zeros((SB_BLOCK, 1), F32)
        init = tuple((jnp.zeros((SB_BLOCK, HEAD_DIM), F32), zero_col, zero_col) for _ in heads)
        res = lax.fori_loop(0, (i + 2) // 2, step, init)
        for h, sl in enumerate(heads):
            dq_ref[:, sl] = res[h][0]

    return pl.pallas_call(
        body, name="sb_attn_bwd", grid=(n_heads // hb, nq),
        in_specs=[pl.BlockSpec((SB_BLOCK, wide), lambda g, i: (i, g)),
                  pl.BlockSpec((t, wide), lambda g, i: (0, g)),
                  pl.BlockSpec((t, wide), lambda g, i: (0, g)),
                  pl.BlockSpec((SB_BLOCK, wide), lambda g, i: (i, g)),
                  pl.BlockSpec((SB_BLOCK, wide), lambda g, i: (i, g))],
        out_specs=[pl.BlockSpec((SB_BLOCK, wide), lambda g, i: (i, g)),
                   pl.BlockSpec((t, wide), lambda g, i: (0, g)),
                   pl.BlockSpec((t, wide), lambda g, i: (0, g))],
        out_shape=[jax.ShapeDtypeStruct((t, d), F32), jax.ShapeDtypeStruct((t, d), F32),
                   jax.ShapeDtypeStruct((t, d), F32)],
        compiler_params=_params(dimension_semantics=("arbitrary", "arbitrary")),
    )(qb, kb, vb, ltot, do)


def _gated_norm(oa, z, gn):
    return _rms(oa, gn, RMS_EPS) * _silu(z)


def _merge_gates(ya, yb, ga, gb):
    return jax.nn.sigmoid(ga) * ya + jax.nn.sigmoid(gb) * yb


def _merge_fwd(x1, oa, proj, ob, gn, wa, wb, wo, layer, n_heads):
    t, d = x1.shape
    tm = _pick(t, (256, 128))
    square = _layer_spec(layer, (d, d), lambda i: (0, 0), single=True)

    def body(x_ref, oa_ref, z_ref, ob_ref, ga_ref, gb_ref, gn_ref, wa_ref, wb_ref, wo_ref, o_ref, na_ref):
        for h in range(n_heads):
            sl = slice(h * HEAD_DIM, (h + 1) * HEAD_DIM)
            na_ref[:, sl] = _gated_norm(oa_ref[:, sl], z_ref[:, sl], gn_ref[...]).astype(BF16)
        m = _merge_gates(_mm_nn(na_ref[...], wa_ref[...]), _mm_nn(ob_ref[...], wb_ref[...]), ga_ref[...], gb_ref[...])
        o_ref[...] = x_ref[...] + _mm_nn(m, wo_ref[...])

    tile = lambda k: pl.BlockSpec((tm, d), lambda i: (i, k))
    return pl.pallas_call(
        body, name="merge_fwd", grid=(t // tm,),
        in_specs=[tile(0), tile(0), tile(3), tile(0), tile(7), tile(8), _const_spec((1, HEAD_DIM)),
                  square, square, square],
        out_specs=tile(0),
        out_shape=jax.ShapeDtypeStruct((t, d), F32),
        scratch_shapes=[pltpu.VMEM((tm, d), BF16)],
        compiler_params=_params(dimension_semantics=("arbitrary",)),
    )(x1, oa, proj, ob, proj, proj, gn, wa, wb, wo)


def _merge_bwd(oa, proj, ob, dy, gn, wa, wb, wo, layer, n_heads):
    t, d = oa.shape
    tm = _pick(t, (256, 128))
    nt = t // tm
    square = _layer_spec(layer, (d, d), lambda i: (0, 0), single=True)

    def body(oa_ref, z_ref, ob_ref, ga_ref, gb_ref, dy_ref, gn_ref, wa_ref, wb_ref, wo_ref,
             doa_ref, dz_ref, dob_ref, dga_ref, dgb_ref, dgn_ref, dwa_hbm, dwb_hbm, dwo_hbm,
             na_ref, dna_ref, dwa_ref, dwb_ref, dwo_ref, stage_ref):
        i = pl.program_id(0)

        @pl.when(i == 0)
        def _():
            dgn_ref[...] = jnp.zeros((1, HEAD_DIM), F32)
            dwa_ref[...] = jnp.zeros((d, d), F32)
            dwb_ref[...] = jnp.zeros((d, d), F32)
            dwo_ref[...] = jnp.zeros((d, d), F32)

        for h in range(n_heads):
            sl = slice(h * HEAD_DIM, (h + 1) * HEAD_DIM)
            na_ref[:, sl] = _gated_norm(oa_ref[:, sl], z_ref[:, sl], gn_ref[...]).astype(BF16)
        dy = dy_ref[...].astype(BF16)
        ob = ob_ref[...].astype(BF16)
        ya = _mm_nn(na_ref[...], wa_ref[...])
        yb = _mm_nn(ob, wb_ref[...])
        m, vjp = jax.vjp(_merge_gates, ya, yb, ga_ref[...], gb_ref[...])
        dwo_ref[...] += _mm_tn(m, dy)
        dya, dyb, dga, dgb = vjp(_mm_nt(dy, wo_ref[...]))
        dga_ref[...] = dga.astype(BF16)
        dgb_ref[...] = dgb.astype(BF16)
        dwa_ref[...] += _mm_tn(na_ref[...], dya)
        dwb_ref[...] += _mm_tn(ob, dyb)
        dob_ref[...] = _mm_nt(dyb, wb_ref[...])
        dna_ref[...] = _mm_nt(dya, wa_ref[...])
        for h in range(n_heads):
            sl = slice(h * HEAD_DIM, (h + 1) * HEAD_DIM)
            _, vjp_h = jax.vjp(_gated_norm, oa_ref[:, sl], z_ref[:, sl], gn_ref[...])
            doa, dz, dgn = vjp_h(dna_ref[:, sl])
            doa_ref[:, sl] = doa
            dz_ref[:, sl] = dz.astype(BF16)
            dgn_ref[...] += dgn

        @pl.when(i == nt - 1)
        def _():
            for acc, out in ((dwa_ref, dwa_hbm), (dwb_ref, dwb_hbm), (dwo_ref, dwo_hbm)):
                stage_ref[...] = acc[...].astype(BF16)
                pltpu.sync_copy(stage_ref, out)

    tile = lambda k: pl.BlockSpec((tm, d), lambda i: (i, k))
    any_spec = pl.BlockSpec(memory_space=pl.ANY)
    return pl.pallas_call(
        body, name="merge_bwd", grid=(nt,),
        in_specs=[tile(0), tile(3), tile(0), tile(7), tile(8), tile(0), _const_spec((1, HEAD_DIM)),
                  square, square, square],
        out_specs=[tile(0), tile(0), tile(0), tile(0), tile(0), _const_spec((1, HEAD_DIM)),
                   any_spec, any_spec, any_spec],
        out_shape=[jax.ShapeDtypeStruct((t, d), F32), jax.ShapeDtypeStruct((t, d), BF16),
                   jax.ShapeDtypeStruct((t, d), F32), jax.ShapeDtypeStruct((t, d), BF16),
                   jax.ShapeDtypeStruct((t, d), BF16), jax.ShapeDtypeStruct((1, HEAD_DIM), F32),
                   jax.ShapeDtypeStruct((d, d), BF16), jax.ShapeDtypeStruct((d, d), BF16),
                   jax.ShapeDtypeStruct((d, d), BF16)],
        scratch_shapes=[pltpu.VMEM((tm, d), BF16), pltpu.VMEM((tm, d), F32),
                        pltpu.VMEM((d, d), F32), pltpu.VMEM((d, d), F32), pltpu.VMEM((d, d), F32),
                        pltpu.VMEM((d, d), BF16)],
        compiler_params=_params(dimension_semantics=("arbitrary",)),
    )(oa, proj, ob, proj, proj, dy, gn, wa, wb, wo)


def _loss_head(y, target):
    t, d = y.shape
    tm = _pick(t, (256, 128))

    def body(y_ref, t_ref, dy_ref, loss_ref):
        @pl.when(pl.program_id(0) == 0)
        def _():
            loss_ref[...] = jnp.zeros((8, LANES), F32)

        err = y_ref[...] - t_ref[...]
        dy_ref[...] = err * (1.0 / d)
        per_token = jnp.sum(err * err, axis=1, keepdims=True) * (1.0 / d)
        loss_ref[...] += 0.5 * jnp.sum(per_token, axis=0, keepdims=True)

    return pl.pallas_call(
        body, name="loss_head", grid=(t // tm,),
        in_specs=[pl.BlockSpec((tm, d), lambda i: (i, 0)), pl.BlockSpec((tm, d), lambda i: (i, 0))],
        out_specs=[pl.BlockSpec((tm, d), lambda i: (i, 0)), _const_spec((8, LANES))],
        out_shape=[jax.ShapeDtypeStruct((t, d), F32), jax.ShapeDtypeStruct((8, LANES), F32)],
        compiler_params=_params(dimension_semantics=("arbitrary",)),
    )(y, target)


def _adamw(w, g, m, v):
    rows, cols = w.shape
    tr = rows
    for cand in (512, 256, 128, 64, 32, 16, 8):
        if rows % cand == 0 and cand * cols * 4 <= 2 * 1024 * 1024:
            tr = cand
            break

    def body(w_ref, g_ref, m_ref, v_ref, d_ref, mo_ref, vo_ref):
        g = g_ref[...]
        m2 = ADAM_B1 * m_ref[...] + (1.0 - ADAM_B1) * g
        v2 = ADAM_B2 * v_ref[...] + (1.0 - ADAM_B2) * (g * g)
        m_hat = m2 / (1.0 - ADAM_B1 ** ADAM_STEP)
        v_hat = v2 / (1.0 - ADAM_B2 ** ADAM_STEP)
        d_ref[...] = -ADAM_LR * (m_hat / (jnp.sqrt(v_hat) + ADAM_EPS) + ADAM_WD * w_ref[...])
        mo_ref[...] = m2
        vo_ref[...] = v2

    spec = pl.BlockSpec((tr, cols), lambda i: (i, 0))
    shape = jax.ShapeDtypeStruct((rows, cols), F32)
    return pl.pallas_call(
        body, name="adamw", grid=(rows // tr,), in_specs=[spec] * 4, out_specs=[spec] * 3,
        out_shape=[shape] * 3, compiler_params=_params(dimension_semantics=("arbitrary",)),
    )(w, g, m, v)


def _place():
    return lax.axis_index("x"), lax.axis_index("y"), lax.axis_index("c")


def _other_chips(x, y):
    return [(1 - x, y), (x, 1 - y), (1 - x, 1 - y)]


def _tile_rows(rows, cols, itemsize, cap=1536 * 1024):
    best = None
    for cand in range(16, rows + 1, 16):
        if rows % cand == 0 and cand * cols * itemsize <= cap:
            best = cand
    return best if best is not None else rows


def _allgather_layer(shards, layer, collective_id):
    n = len(shards)

    def body(*refs):
        srcs, outs = refs[:n], refs[n:2 * n]
        send_sems, recv_sems = refs[2 * n:]
        x, y, c = _place()
        me, sibling = (x, y, c), (x, y, 1 - c)
        chips = _other_chips(x, y)
        barrier = pltpu.get_barrier_semaphore()
        for peer in [(*chip, c) for chip in chips] + [sibling]:
            pl.semaphore_signal(barrier, inc=1, device_id=peer, device_id_type=MESH)
        pl.semaphore_wait(barrier, N_CHIPS)

        def half(w, which):
            rows = shards[w].shape[1] // 2
            return pl.ds(which * rows, rows)

        def copy(w, k, shard, which, to, from_src=False):
            part = half(w, which)
            return pltpu.make_async_remote_copy(
                src_ref=srcs[w].at[layer, part] if from_src else outs[w].at[shard, part],
                dst_ref=outs[w].at[shard, part], send_sem=send_sems.at[6 * w + k], recv_sem=recv_sems.at[6 * w + k],
                device_id=to, device_id_type=MESH)

        first = [copy(w, j, 2 * x + y, c, (*chip, c), from_src=True) for j, chip in enumerate(chips) for w in range(n)]
        for cp in first:
            cp.start()
        passed = []
        for j, (cx, cy) in enumerate(chips):
            for w in range(n):
                copy(w, j, 2 * cx + cy, c, me).wait_recv()
                cp = copy(w, 3 + j, 2 * cx + cy, c, sibling)
                cp.start()
                passed.append(cp)
        for j, (cx, cy) in enumerate(chips):
            for w in range(n):
                copy(w, 3 + j, 2 * cx + cy, 1 - c, me).wait_recv()
        for cp in first + passed:
            cp.wait_send()

    return pl.kernel(
        body, name=f"allgather_layer{layer}_id{collective_id}",
        out_type=[jax.ShapeDtypeStruct((N_CHIPS,) + s.shape[1:], s.dtype) for s in shards],
        mesh=plsc.ScalarSubcoreMesh(axis_name="sequencer", num_cores=1),
        scratch_types=[pltpu.SemaphoreType.DMA((6 * n,)), pltpu.SemaphoreType.DMA((6 * n,))],
        compiler_params=pltpu.CompilerParams(collective_id=collective_id),
    )(*shards)


def _swap_halves(grads):
    n = len(grads)

    def body(*refs):
        gs, gots = refs[:n], refs[n:2 * n]
        send_sems, recv_sems = refs[2 * n:]
        x, y, c = _place()
        copies = []
        for w in range(n):
            half = grads[w].shape[1] // 2
            copies.append(pltpu.make_async_remote_copy(
                src_ref=gs[w].at[:, pl.ds((1 - c) * half, half)], dst_ref=gots[w], send_sem=send_sems.at[w],
                recv_sem=recv_sems.at[w], device_id=(x, y, 1 - c), device_id_type=MESH))
        for cp in copies:
            cp.start()
        for cp in copies:
            cp.wait()

    hbm = pl.BlockSpec(memory_space=pl.ANY)
    return pl.pallas_call(
        body, name="swap_halves", in_specs=[hbm] * n, out_specs=[hbm] * n,
        out_shape=[jax.ShapeDtypeStruct((g.shape[0], g.shape[1] // 2, g.shape[2]), g.dtype) for g in grads],
        scratch_shapes=[pltpu.SemaphoreType.DMA((n,)), pltpu.SemaphoreType.DMA((n,))],
    )(*grads)


def _swap_halves_async(grads, layer, collective_id):
    n = len(grads)

    def body(*refs):
        gs, gots = refs[:n], refs[n:2 * n]
        send_sems, recv_sems = refs[2 * n:]
        x, y, c = _place()
        sibling = (x, y, 1 - c)
        barrier = pltpu.get_barrier_semaphore()
        pl.semaphore_signal(barrier, inc=1, device_id=sibling, device_id_type=MESH)
        pl.semaphore_wait(barrier, 1)
        copies = []
        for w in range(n):
            half = grads[w].shape[1] // 2
            copies.append(pltpu.make_async_remote_copy(
                src_ref=gs[w].at[:, pl.ds((1 - c) * half, half)], dst_ref=gots[w], send_sem=send_sems.at[w],
                recv_sem=recv_sems.at[w], device_id=sibling, device_id_type=MESH))
        for cp in copies:
            cp.start()
        for cp in copies:
            cp.wait()

    return pl.kernel(
        body, name=f"swap_halves{layer}_id{collective_id}",
        out_type=[jax.ShapeDtypeStruct((g.shape[0], g.shape[1] // 2, g.shape[2]), g.dtype) for g in grads],
        mesh=plsc.ScalarSubcoreMesh(axis_name="sequencer", num_cores=1),
        scratch_types=[pltpu.SemaphoreType.DMA((n,)), pltpu.SemaphoreType.DMA((n,))],
        compiler_params=pltpu.CompilerParams(collective_id=collective_id),
    )(*grads)


def _add_half(grad, got, c_idx):
    n, rows, all_cols = grad.shape
    side = N_CHIPS // n
    cols = all_cols // side
    half = rows // 2
    tr = _tile_rows(half, cols, 2)
    nb = half // tr

    def body(c_ref, a_ref, b_ref, o_ref):
        o_ref[...] = (a_ref[...].astype(F32) + b_ref[...].astype(F32)).astype(o_ref.dtype)

    return pl.pallas_call(
        body, name="add_half",
        grid_spec=pltpu.PrefetchScalarGridSpec(
            num_scalar_prefetch=1, grid=(N_CHIPS, nb),
            in_specs=[pl.BlockSpec((1, tr, cols), lambda s, r, c_ref: (s // side, c_ref[0] * nb + r, s % side)),
                      pl.BlockSpec((1, tr, cols), lambda s, r, c_ref: (s // side, r, s % side))],
            out_specs=pl.BlockSpec((1, tr, cols), lambda s, r, c_ref: (s, r, 0))),
        out_shape=jax.ShapeDtypeStruct((N_CHIPS, half, cols), grad.dtype),
        compiler_params=_params(dimension_semantics=("arbitrary", "arbitrary")),
    )(c_idx, grad, got)


def _scatter_partials(parts, layer, collective_id):
    n = len(parts)

    def body(*refs):
        ps, gots = refs[:n], refs[n:2 * n]
        send_sems, recv_sems = refs[2 * n:]
        x, y, c = _place()
        chips = _other_chips(x, y)
        barrier = pltpu.get_barrier_semaphore()
        for chip in chips:
            pl.semaphore_signal(barrier, inc=1, device_id=(*chip, c), device_id_type=MESH)
        pl.semaphore_wait(barrier, N_CHIPS - 1)
        copies = [pltpu.make_async_remote_copy(src_ref=ps[w].at[2 * cx + cy], dst_ref=gots[w].at[j],
                                               send_sem=send_sems.at[3 * w + j], recv_sem=recv_sems.at[3 * w + j],
                                               device_id=(cx, cy, c), device_id_type=MESH)
                  for j, (cx, cy) in enumerate(chips) for w in range(n)]
        for cp in copies:
            cp.start()
        for cp in copies:
            cp.wait()

    return pl.kernel(
        body, name=f"scatter_partials{layer}_id{collective_id}",
        out_type=[jax.ShapeDtypeStruct((N_CHIPS - 1,) + p.shape[1:], p.dtype) for p in parts],
        mesh=plsc.ScalarSubcoreMesh(axis_name="sequencer", num_cores=1),
        scratch_types=[pltpu.SemaphoreType.DMA((3 * n,)), pltpu.SemaphoreType.DMA((3 * n,))],
        compiler_params=pltpu.CompilerParams(collective_id=collective_id),
    )(*parts)


def _sum_partials(part, got, s_idx, c_idx, layer, depth, stacked=None):
    n, half, cols = part.shape
    tr = _tile_rows(half, cols, 2, cap=1024 * 1024)
    nb = half // tr

    def body(s_ref, c_ref, a_ref, b_ref, *rest):
        o_ref = rest[-1]
        acc = a_ref[0].astype(F32)
        for j in range(n - 1):
            acc = acc + b_ref[j].astype(F32)
        o_ref[...] = acc

    in_specs = [pl.BlockSpec((1, tr, cols), lambda r, s_ref, c_ref: (s_ref[0], r, 0)),
                pl.BlockSpec((n - 1, tr, cols), lambda r, s_ref, c_ref: (0, r, 0))]
    operands = [s_idx, c_idx, part, got]
    aliases = {}
    if stacked is not None:
        in_specs.append(pl.BlockSpec(memory_space=pl.ANY))
        operands.append(stacked)
        aliases = {len(operands) - 1: 0}
    return pl.pallas_call(
        body, name="sum_partials",
        grid_spec=pltpu.PrefetchScalarGridSpec(
            num_scalar_prefetch=2, grid=(nb,), in_specs=in_specs,
            out_specs=pl.BlockSpec((None, tr, cols), lambda r, s_ref, c_ref: (layer, c_ref[0] * nb + r, 0))),
        out_shape=jax.ShapeDtypeStruct((depth, 2 * half, cols), F32),
        input_output_aliases=aliases,
        compiler_params=_params(dimension_semantics=("arbitrary",)),
    )(*operands)


def _join_halves(bufs, layer):
    n = len(bufs)

    def body(*refs):
        outs = refs[n:2 * n]
        send_sems, recv_sems = refs[2 * n:]
        x, y, c = _place()
        copies = []
        for w in range(n):
            half = bufs[w].shape[1] // 2
            mine = outs[w].at[layer, pl.ds(c * half, half)]
            copies.append(pltpu.make_async_remote_copy(src_ref=mine, dst_ref=mine, send_sem=send_sems.at[w],
                                                       recv_sem=recv_sems.at[w], device_id=(x, y, 1 - c),
                                                       device_id_type=MESH))
        for cp in copies:
            cp.start()
        for cp in copies:
            cp.wait()

    hbm = pl.BlockSpec(memory_space=pl.ANY)
    return pl.pallas_call(
        body, name="join_halves", in_specs=[hbm] * n, out_specs=[hbm] * n,
        out_shape=[jax.ShapeDtypeStruct(b.shape, b.dtype) for b in bufs],
        input_output_aliases={w: w for w in range(n)},
        scratch_shapes=[pltpu.SemaphoreType.DMA((n,)), pltpu.SemaphoreType.DMA((n,))],
    )(*bufs)


def _allreduce_small(v, name):
    rows = v.shape[0]

    def body(v_ref, o_ref, gath, send_sems, recv_sems):
        x, y, c = _place()
        idx = 4 * x + 2 * y + c
        gath[0] = v_ref[...]
        copies = []
        for r in range(1, N_DEV):
            peer = (1 - x if r & 4 else x, 1 - y if r & 2 else y, 1 - c if r & 1 else c)
            cp = pltpu.make_async_remote_copy(src_ref=v_ref, dst_ref=gath.at[r], send_sem=send_sems.at[r - 1],
                                              recv_sem=recv_sems.at[r - 1], device_id=peer, device_id_type=MESH)
            cp.start()
            copies.append(cp)
        for cp in copies:
            cp.wait()
        acc = gath[idx]
        for a in range(1, N_DEV):
            acc = acc + gath[lax.bitwise_xor(idx, a)]
        o_ref[...] = acc

    vmem = pl.BlockSpec(memory_space=pltpu.VMEM)
    return pl.pallas_call(
        body, name=name, in_specs=[vmem], out_specs=vmem,
        out_shape=jax.ShapeDtypeStruct((rows, LANES), F32),
        scratch_shapes=[pltpu.VMEM((N_DEV, rows, LANES), F32), pltpu.SemaphoreType.DMA((N_DEV - 1,)),
                        pltpu.SemaphoreType.DMA((N_DEV - 1,))],
    )(v)


def _join_shards(name, gathered):
    if name in COL_SHARDED:
        return jnp.concatenate([gathered[s] for s in range(N_CHIPS)], axis=1)[None]
    return gathered.reshape(1, N_CHIPS * gathered.shape[1], gathered.shape[2])


def _row_shards(g):
    return g.reshape(N_CHIPS, g.shape[0] // N_CHIPS, g.shape[1])


def _mixer_runs(width, cut, n_small):
    runs = []
    for s in range(N_CHIPS):
        lo, hi = s * width, (s + 1) * width
        spans = ((True, lo, min(hi, cut)), (False, max(lo, cut), min(hi, cut + n_small)),
                 (True, max(lo, cut + n_small), hi))
        runs.append([(is_main, a - lo, b - lo) for is_main, a, b in spans if a < b])
    return runs


def _split_mixer_weight(gathered, runs):
    main = [gathered[s][:, a:b] for s, parts in enumerate(runs) for is_main, a, b in parts if is_main]
    small = [gathered[s][:, a:b] for s, parts in enumerate(runs) for is_main, a, b in parts if not is_main]
    small = small[0] if len(small) == 1 else jnp.concatenate(small, axis=1)
    return jnp.concatenate(main, axis=1)[None], jnp.pad(small, ((0, 0), (0, LANES - small.shape[1])))[None]


def _join_mixer_grad(d_main, d_small, runs):
    shards, m, k = [], 0, 0
    for parts in runs:
        cols = []
        for is_main, a, b in parts:
            if is_main:
                cols.append(d_main[:, m:m + b - a])
                m += b - a
            else:
                cols.append(d_small[:, k:k + b - a].astype(d_main.dtype))
                k += b - a
        shards.append(cols[0] if len(cols) == 1 else jnp.concatenate(cols, axis=1))
    return jnp.stack(shards)


def _pad_small(flat):
    n = flat.shape[0]
    block = 8 * LANES
    padded = -(-n // block) * block
    return jnp.pad(flat, (0, padded - n)).reshape(padded // LANES, LANES)


def kernel(x, ffn1_norm, ffn1_w_in, ffn1_w_out, mix_norm, w_in, dn_conv_w, dn_a_log, dn_dt_bias, dn_out_norm, sb_q_norm, sb_k_norm, w_branch_a, w_branch_b, w_out, ffn2_norm, ffn2_w_in, ffn2_w_out, loss_target, m_ffn1_norm, m_ffn1_w_in, m_ffn1_w_out, m_mix_norm, m_w_in, m_dn_conv_w, m_dn_a_log, m_dn_dt_bias, m_dn_out_norm, m_sb_q_norm, m_sb_k_norm, m_w_branch_a, m_w_branch_b, m_w_out, m_ffn2_norm, m_ffn2_w_in, m_ffn2_w_out, v_ffn1_norm, v_ffn1_w_in, v_ffn1_w_out, v_mix_norm, v_w_in, v_dn_conv_w, v_dn_a_log, v_dn_dt_bias, v_dn_out_norm, v_sb_q_norm, v_sb_k_norm, v_w_branch_a, v_w_branch_b, v_w_out, v_ffn2_norm, v_ffn2_w_in, v_ffn2_w_out):
    w = dict(ffn1_norm=ffn1_norm, ffn1_w_in=ffn1_w_in, ffn1_w_out=ffn1_w_out, mix_norm=mix_norm, w_in=w_in,
             dn_conv_w=dn_conv_w, dn_a_log=dn_a_log, dn_dt_bias=dn_dt_bias, dn_out_norm=dn_out_norm,
             sb_q_norm=sb_q_norm, sb_k_norm=sb_k_norm, w_branch_a=w_branch_a, w_branch_b=w_branch_b, w_out=w_out,
             ffn2_norm=ffn2_norm, ffn2_w_in=ffn2_w_in, ffn2_w_out=ffn2_w_out)
    mom = dict(ffn1_norm=m_ffn1_norm, ffn1_w_in=m_ffn1_w_in, ffn1_w_out=m_ffn1_w_out, mix_norm=m_mix_norm, w_in=m_w_in,
               dn_conv_w=m_dn_conv_w, dn_a_log=m_dn_a_log, dn_dt_bias=m_dn_dt_bias, dn_out_norm=m_dn_out_norm,
               sb_q_norm=m_sb_q_norm, sb_k_norm=m_sb_k_norm, w_branch_a=m_w_branch_a, w_branch_b=m_w_branch_b,
               w_out=m_w_out, ffn2_norm=m_ffn2_norm, ffn2_w_in=m_ffn2_w_in, ffn2_w_out=m_ffn2_w_out)
    var = dict(ffn1_norm=v_ffn1_norm, ffn1_w_in=v_ffn1_w_in, ffn1_w_out=v_ffn1_w_out, mix_norm=v_mix_norm, w_in=v_w_in,
               dn_conv_w=v_dn_conv_w, dn_a_log=v_dn_a_log, dn_dt_bias=v_dn_dt_bias, dn_out_norm=v_dn_out_norm,
               sb_q_norm=v_sb_q_norm, sb_k_norm=v_sb_k_norm, w_branch_a=v_w_branch_a, w_branch_b=v_w_branch_b,
               w_out=v_w_out, ffn2_norm=v_ffn2_norm, ffn2_w_in=v_ffn2_w_in, ffn2_w_out=v_ffn2_w_out)

    _, t, d = x.shape
    depth = ffn1_norm.shape[0]
    n_heads = d // HEAD_DIM
    conv_cols = dn_conv_w.shape[2]
    assert d % HEAD_DIM == 0 and t % SB_BLOCK == 0 and 2 * n_heads <= LANES and depth % 2 == 0
    assert w_in.shape[2] * N_CHIPS == 9 * d + 2 * n_heads and conv_cols * N_CHIPS == 3 * d

    x_idx, y_idx, c_idx = _place()
    shard = 2 * x_idx + y_idx
    c_arr = jnp.reshape(c_idx, (1,)).astype(jnp.int32)
    s_arr = jnp.reshape(shard, (1,)).astype(jnp.int32)

    mine = {n: w[n].astype(BF16) for n in BIG}
    runs = _mixer_runs(w_in.shape[2], 4 * d, 2 * n_heads)
    groups = (("ffn1_w_in", "ffn1_w_out"), ("w_in",), ("w_branch_a", "w_branch_b", "w_out"),
              ("ffn2_w_in", "ffn2_w_out"))
    first = groups[0]
    rest = tuple(n for g in groups[1:] for n in g)
    n_gathers = depth + len(groups) - 1

    def gather(names, l, collective_id):
        return dict(zip(names, _allgather_layer([mine[n] for n in names], l, collective_id)))

    arriving = [{}]
    for k, names in enumerate(groups):
        arriving[0].update(gather(names, 0, 0 if k == 0 else depth + k - 1))
    arriving += [gather(BIG, l, l) for l in range(1, depth)]

    def layer_weights(l, names, after):
        gathered, after = lax.optimization_barrier(({n: arriving[l][n] for n in names}, after))
        full = {}
        for n in names:
            g = lax.dynamic_update_slice(gathered[n], mine[n][l][None], (shard, 0, 0))
            if n == "w_in":
                full["w_main"], full["w_ba"] = _split_mixer_weight(g, runs)
            else:
                full[n] = _join_shards(n, g)
        return full, after

    conv_place = lax.dynamic_update_slice(jnp.zeros((depth, DN_CONV, 3 * d), F32), dn_conv_w, (0, 0, shard * conv_cols))
    conv_rows = _pad_small(conv_place.reshape(-1))
    conv_full = (0.5 * _allreduce_small(conv_rows, "allgather_conv")).reshape(-1)[:depth * DN_CONV * 3 * d]
    conv_full = jnp.pad(conv_full.reshape(depth, DN_CONV, 3 * d), ((0, 0), (0, CONV_ROWS - DN_CONV), (0, 0)))

    def head_row(vals):
        return jnp.pad(vals, (n_heads, LANES - 2 * n_heads)).reshape(1, LANES)

    saved, layers = [], []
    cur = x[0]
    for l in range(depth):
        full, x0 = layer_weights(l, groups[0], cur)
        layers.append(full)
        x1, h1 = _ffn_fwd(x0, ffn1_norm[l][None], full["ffn1_w_in"], full["ffn1_w_out"], 0)
        later, x1 = layer_weights(l, groups[1], x1)
        full.update(later)
        proj, ba, hm = _proj_fwd(x1, mix_norm[l][None], full["w_main"], full["w_ba"], 0)
        act = _dn_prep_fwd(proj, conv_full[l], n_heads)
        alog, dtb = head_row(dn_a_log[l]), head_row(dn_dt_bias[l])
        oa, snaps = _delta_fwd(act, ba, alog, dtb, n_heads)
        qb, kb, vb = _sb_prep_fwd(proj, sb_q_norm[l][None], sb_k_norm[l][None], n_heads)
        ob, ltot = _sb_attn_fwd(qb, kb, vb, n_heads)
        later, (oa, ob) = layer_weights(l, groups[2], (oa, ob))
        full.update(later)
        x2 = _merge_fwd(x1, oa, proj, ob, dn_out_norm[l][None], full["w_branch_a"], full["w_branch_b"],
                        full["w_out"], 0, n_heads)
        later, x2 = layer_weights(l, groups[3], x2)
        full.update(later)
        cur, h2 = _ffn_fwd(x2, ffn2_norm[l][None], full["ffn2_w_in"], full["ffn2_w_out"], 0)
        saved.append((x0, x1, proj, ba, act, alog, dtb, oa, snaps, qb, kb, vb, ob, ltot, x2, h1, hm, h2))

    dcur, loss_part = _loss_head(cur, loss_target[0])

    grads = {n: [None] * depth for n in WEIGHTS}
    reduced = {n: None for n in BIG}
    in_flight = []

    def start_reduce(l, names, collective_id, swapped=None):
        g_major = [grads[n][l] for n in names]
        swapped = _swap_halves(g_major) if swapped is None else swapped
        parts = [_add_half(g, got, c_arr) for g, got in zip(g_major, swapped)]
        return l, names, parts, _scatter_partials(parts, l, collective_id)

    def finish_reduce(started, after):
        done_names, halves = [], []
        for l, names, parts, arrived in started:
            arrived, after = lax.optimization_barrier((arrived, after))
            halves += [_sum_partials(p, got, s_arr, c_arr, l, depth, stacked=reduced[n])
                       for n, p, got in zip(names, parts, arrived)]
            done_names += names
        if started:
            reduced.update(zip(done_names, _join_halves(halves, started[0][0])))
        return after

    for l in reversed(range(depth)):
        x0, x1, proj, ba, act, alog, dtb, oa, snaps, qb, kb, vb, ob, ltot, x2, h1, hm, h2 = saved[l]
        full = layers[l]
        dx2, dg, dwi, dwo = _ffn_bwd(x2, h2, ffn2_norm[l][None], dcur, full["ffn2_w_in"], full["ffn2_w_out"], 0)
        grads["ffn2_norm"][l] = dg[0]
        grads["ffn2_w_in"][l] = dwi
        grads["ffn2_w_out"][l] = _row_shards(dwo)
        doa, dz, dob, dga, dgb, dgn, dwa, dwb, dwout = _merge_bwd(
            oa, proj, ob, dx2, dn_out_norm[l][None], full["w_branch_a"], full["w_branch_b"], full["w_out"], 0,
            n_heads)
        grads["dn_out_norm"][l] = dgn[0]
        grads["w_branch_a"][l], grads["w_branch_b"][l] = _row_shards(dwa), _row_shards(dwb)
        grads["w_out"][l] = _row_shards(dwout)
        dqb, dkb, dvb = _sb_attn_bwd(qb, kb, vb, ltot, dob, n_heads)
        dsq, dsk, dqn, dkn = _sb_prep_bwd(proj, sb_q_norm[l][None], sb_k_norm[l][None], dqb, dkb, n_heads)
        grads["sb_q_norm"][l], grads["sb_k_norm"][l] = dqn[0], dkn[0]
        dact, dba, dal, ddt = _delta_bwd(act, ba, alog, dtb, snaps, doa, n_heads)
        grads["dn_a_log"][l] = dal[0, n_heads:2 * n_heads]
        grads["dn_dt_bias"][l] = ddt[0, n_heads:2 * n_heads]
        dqkv, dconv = _dn_prep_bwd(proj, conv_full[l], dact, n_heads)
        grads["dn_conv_w"][l] = dconv[:DN_CONV]
        dproj = jnp.concatenate([dqkv, dz, dsq, dsk, dvb.astype(BF16), dga, dgb], axis=1)
        dx1, dg, dwm, dwba = _proj_bwd(x1, hm, mix_norm[l][None], dx2, dproj, dba, full["w_main"], full["w_ba"], 0)
        grads["mix_norm"][l] = dg[0]
        grads["w_in"][l] = _join_mixer_grad(dwm, dwba, runs)
        if l == 0:
            early = start_reduce(l, rest, collective_id=n_gathers + l)
        else:
            swapping = _swap_halves_async([grads[n][l] for n in rest], l, collective_id=n_gathers + 2 * depth + l)
        dcur, dg, dwi, dwo = _ffn_bwd(x0, h1, ffn1_norm[l][None], dx1, full["ffn1_w_in"], full["ffn1_w_out"], 0)
        grads["ffn1_norm"][l] = dg[0]
        grads["ffn1_w_in"][l] = dwi
        grads["ffn1_w_out"][l] = _row_shards(dwo)
        if l > 0:
            swapping, dcur = lax.optimization_barrier((swapping, dcur))
            early = start_reduce(l, rest, collective_id=n_gathers + l, swapped=swapping)

        dcur = finish_reduce(in_flight, dcur)
        in_flight = [early, start_reduce(l, first, collective_id=n_gathers + depth + l)]
    dcur = finish_reduce(in_flight, dcur)
    final = reduced
    grads = {n: jnp.stack(grads[n]) for n in SMALL + ("dn_conv_w",)}

    small_names = SMALL + ("dn_conv_w",)
    small_sizes = [int(np.prod(grads[n].shape)) for n in small_names]
    small_off = np.concatenate([[0], np.cumsum(small_sizes)])
    small = jnp.concatenate([grads[n].reshape(-1) for n in small_names] + [loss_part[0, :1]])
    small_sum = _allreduce_small(_pad_small(small), "allreduce_small").reshape(-1)
    for i, n in enumerate(small_names):
        final[n] = small_sum[small_off[i]:small_off[i + 1]].reshape(grads[n].shape)
    final["dn_conv_w"] = lax.dynamic_slice(final["dn_conv_w"], (0, 0, shard * conv_cols), (depth, DN_CONV, conv_cols))
    loss = small_sum[small_off[-1]]

    deltas, new_m, new_v = {}, {}, {}
    for n in WEIGHTS:
        shape = w[n].shape
        flat = (-1, shape[-1])
        dl, m2, v2 = _adamw(w[n].reshape(flat), final[n].reshape(flat), mom[n].reshape(flat), var[n].reshape(flat))
        deltas[n], new_m[n], new_v[n] = dl.reshape(shape), m2.reshape(shape), v2.reshape(shape)

    grad_x = dcur[None]
    return (loss, grad_x, *[final[n] for n in WEIGHTS], *[deltas[n] for n in WEIGHTS],
            *[new_m[n] for n in WEIGHTS], *[new_v[n] for n in WEIGHTS])
```

```python
import functools

import jax
import jax.numpy as jnp
import numpy as np
from jax import lax
from jax.experimental import pallas as pl
from jax.experimental.pallas import tpu as pltpu
from jax.experimental.pallas import tpu_sc as plsc

F32 = jnp.float32
BF16 = jnp.bfloat16

LANES = 128
HEAD_DIM = 128
DN_CHUNK = 64
DN_CONV = 4
CONV_ROWS = 8
SB_BLOCK = 128
FFN_HALF = 0.5
RMS_EPS = 1e-6
L2_EPS = 1e-6
NEG_BIG = -1e30
ADAM_LR = 0.001
ADAM_B1 = 0.9
ADAM_B2 = 0.999
ADAM_EPS = 1e-08
ADAM_WD = 0.01
ADAM_STEP = 10
VMEM_LIMIT = 56 * 1024 * 1024
N_CHIPS = 4
N_DEV = 8
MESH = pl.DeviceIdType.MESH

BIG = ("ffn1_w_in", "ffn1_w_out", "w_in", "w_branch_a", "w_branch_b", "w_out", "ffn2_w_in", "ffn2_w_out")
COL_SHARDED = ("ffn1_w_in", "w_in", "ffn2_w_in")
SMALL = ("ffn1_norm", "mix_norm", "dn_a_log", "dn_dt_bias", "dn_out_norm", "sb_q_norm", "sb_k_norm", "ffn2_norm")
WEIGHTS = ("ffn1_norm", "ffn1_w_in", "ffn1_w_out", "mix_norm", "w_in", "dn_conv_w", "dn_a_log", "dn_dt_bias",
           "dn_out_norm", "sb_q_norm", "sb_k_norm", "w_branch_a", "w_branch_b", "w_out", "ffn2_norm", "ffn2_w_in",
           "ffn2_w_out")


def _params(**kw):
    return pltpu.CompilerParams(vmem_limit_bytes=VMEM_LIMIT, **kw)


def _pick(n, options):
    for o in options:
        if n % o == 0:
            return o
    return n


def _const_spec(shape, single=False):
    nd = len(shape)
    if single:
        return pl.BlockSpec(shape, lambda *_: (0,) * nd, pipeline_mode=pl.Buffered(1))
    return pl.BlockSpec(shape, lambda *_: (0,) * nd)


_NN = ((1,), (0,))
_NT = ((1,), (1,))
_TN = ((0,), (0,))


def _dot(a, b, dims):
    return lax.dot_general(a.astype(BF16), b.astype(BF16), (dims, ((), ())), preferred_element_type=F32)


def _mm_nn(a, b):
    return _dot(a, b, _NN)


def _mm_nt(a, b):
    return _dot(a, b, _NT)


def _mm_tn(a, b):
    return _dot(a, b, _TN)


def _split(a):
    hi = a.astype(BF16)
    lo = (a - hi.astype(F32)).astype(BF16)
    return hi, lo


def _dot_precise(a, b, dims):
    dn = (dims, ((), ()))
    ah, al = _split(a)
    bh, bl = _split(b)
    out = lax.dot_general(ah, bh, dn, preferred_element_type=F32)
    out = out + lax.dot_general(ah, bl, dn, preferred_element_type=F32)
    return out + lax.dot_general(al, bh, dn, preferred_element_type=F32)


def _make_diff_mm(dot):
    @jax.custom_vjp
    def nn(a, b):
        return dot(a, b, _NN)

    @jax.custom_vjp
    def nt(a, b):
        return dot(a, b, _NT)

    @jax.custom_vjp
    def tn(a, b):
        return dot(a, b, _TN)

    nn.defvjp(lambda a, b: (dot(a, b, _NN), (a, b)), lambda r, g: (nt(g, r[1]), tn(r[0], g)))
    nt.defvjp(lambda a, b: (dot(a, b, _NT), (a, b)), lambda r, g: (nn(g, r[1]), tn(g, r[0])))
    tn.defvjp(lambda a, b: (dot(a, b, _TN), (a, b)), lambda r, g: (nt(r[1], g), nn(r[0], g)))
    return nn, nt, tn


_d_nn, _d_nt, _d_tn = _make_diff_mm(_dot)
_p_nn, _p_nt, _p_tn = _make_diff_mm(_dot_precise)


@jax.custom_vjp
def _halves(x):
    n = x.shape[0] // 2
    return x[:n], x[n:]


_halves.defvjp(lambda x: (_halves(x), None), lambda _, g: (jnp.concatenate(g, axis=0),))


@jax.custom_vjp
def _col_halves(x):
    n = x.shape[1] // 2
    return x[:, :n], x[:, n:]


_col_halves.defvjp(lambda x: (_col_halves(x), None), lambda _, g: (jnp.concatenate(g, axis=1),))


def _softplus_raw(x):
    return jnp.maximum(x, 0.0) + jnp.log(1.0 + jnp.exp(-jnp.abs(x)))


@jax.custom_vjp
def _softplus(x):
    return _softplus_raw(x)


_softplus.defvjp(lambda x: (_softplus_raw(x), x), lambda x, g: (g * jax.nn.sigmoid(x),))


def _rms(x, gain, eps):
    return x * lax.rsqrt(jnp.mean(x * x, axis=-1, keepdims=True) + eps) * gain


def _silu(x):
    return x * jax.nn.sigmoid(x)


def _shift_rows_raw(x, k, down):
    n = x.shape[0]
    row = lax.broadcasted_iota(jnp.int32, x.shape, 0)
    if down:
        return jnp.where(row >= k, pltpu.roll(x, k, 0), 0.0)
    return jnp.where(row < n - k, pltpu.roll(x, n - k, 0), 0.0)


@functools.partial(jax.custom_vjp, nondiff_argnums=(1,))
def _shift_down(x, k):
    return _shift_rows_raw(x, k, True)


_shift_down.defvjp(lambda x, k: (_shift_rows_raw(x, k, True), None),
                   lambda k, _, g: (_shift_rows_raw(g, k, False),))


def _layer_spec(layer, block, index_map, single=False):
    full_map = lambda *a: (layer,) + tuple(index_map(*a))
    if single:
        return pl.BlockSpec((None,) + block, full_map, pipeline_mode=pl.Buffered(1))
    return pl.BlockSpec((None,) + block, full_map)


def _ffn_fwd(x, gain, w_in, w_out, layer):
    t, d = x.shape
    f = w_out.shape[1]
    fc = _pick(f, (256, 128))
    nj = f // fc
    rt = _pick(t, (512, 256, 128))

    def body(x_ref, g_ref, wg_ref, wu_ref, wo_ref, o_ref, hs_ref):
        @pl.when(pl.program_id(0) == 0)
        def _():
            for r in range(t // rt):
                rows = pl.ds(r * rt, rt)
                xr = x_ref[rows, :]
                hs_ref[rows, :] = _rms(xr, g_ref[...], RMS_EPS).astype(BF16)
                o_ref[rows, :] = xr

        for r in range(t // rt):
            rows = pl.ds(r * rt, rt)
            h = hs_ref[rows, :]
            a = _mm_nn(h, wg_ref[...])
            b = _mm_nn(h, wu_ref[...])
            o_ref[rows, :] += FFN_HALF * _mm_nn(_silu(a) * b, wo_ref[...])

    return pl.pallas_call(
        body, name="ffn_fwd", grid=(nj,),
        in_specs=[_const_spec((t, d), True), _const_spec((1, d)),
                  _layer_spec(layer, (d, fc), lambda j: (0, j)), _layer_spec(layer, (d, fc), lambda j: (0, nj + j)),
                  _layer_spec(layer, (fc, d), lambda j: (j, 0))],
        out_specs=[_const_spec((t, d)), _const_spec((t, d))],
        out_shape=[jax.ShapeDtypeStruct((t, d), F32), jax.ShapeDtypeStruct((t, d), BF16)],
        compiler_params=_params(dimension_semantics=("arbitrary",)),
    )(x, gain, w_in, w_in, w_out)


def _ffn_bwd(x, hs, gain, dy, w_in, w_out, layer):
    t, d = x.shape
    f = w_out.shape[1]
    fc = _pick(f, (256, 128))
    nj = f // fc
    rt = _pick(t, (512, 256, 128))
    nr = t // rt

    def body(x_ref, hs_ref, g_ref, dy_ref, wg_ref, wu_ref, wo_ref, dx_ref, dg_ref, dwi_ref, dwo_ref,
             dwg_acc, dwu_acc, dwo_acc):
        j = pl.program_id(0)

        @pl.when(j == 0)
        def _():
            for r in range(nr):
                dx_ref[pl.ds(r * rt, rt), :] = jnp.zeros((rt, d), F32)

        for r in range(nr):
            rows = pl.ds(r * rt, rt)
            h = hs_ref[rows, :]
            dy2 = (FFN_HALF * dy_ref[rows, :]).astype(BF16)
            a = _mm_nn(h, wg_ref[...])
            b = _mm_nn(h, wu_ref[...])
            sig = jax.nn.sigmoid(a)
            sa = a * sig
            ds = _mm_nt(dy2, wo_ref[...])
            da = ds * b * (sig * (1.0 + a * (1.0 - sig)))
            db = ds * sa
            dx_ref[rows, :] += _mm_nt(da, wg_ref[...]) + _mm_nt(db, wu_ref[...])
            dwo_c = _mm_tn(sa * b, dy2)
            dwg_c = _mm_tn(h, da)
            dwu_c = _mm_tn(h, db)
            if r == 0:
                dwo_acc[...] = dwo_c
                dwg_acc[...] = dwg_c
                dwu_acc[...] = dwu_c
            else:
                dwo_acc[...] += dwo_c
                dwg_acc[...] += dwg_c
                dwu_acc[...] += dwu_c
        dwi_ref[0] = dwg_acc[...].astype(BF16)
        dwi_ref[1] = dwu_acc[...].astype(BF16)
        dwo_ref[...] = dwo_acc[...].astype(BF16)

        @pl.when(j == nj - 1)
        def _():
            for r in range(nr):
                rows = pl.ds(r * rt, rt)
                _, vjp = jax.vjp(lambda xx, gg: _rms(xx, gg, RMS_EPS), x_ref[rows, :], g_ref[...])
                dxn, dgr = vjp(dx_ref[rows, :])
                dx_ref[rows, :] = dy_ref[rows, :] + dxn
                if r == 0:
                    dg_ref[...] = dgr
                else:
                    dg_ref[...] += dgr

    return pl.pallas_call(
        body, name="ffn_bwd", grid=(nj,),
        in_specs=[_const_spec((t, d), True), _const_spec((t, d), True), _const_spec((1, d)), _const_spec((t, d), True),
                  _layer_spec(layer, (d, fc), lambda j: (0, j)), _layer_spec(layer, (d, fc), lambda j: (0, nj + j)),
                  _layer_spec(layer, (fc, d), lambda j: (j, 0))],
        out_specs=[_const_spec((t, d)), _const_spec((1, d)),
                   pl.BlockSpec((2, d, fc), lambda j: (0, 0, j)), pl.BlockSpec((fc, d), lambda j: (j, 0))],
        out_shape=[jax.ShapeDtypeStruct((t, d), F32), jax.ShapeDtypeStruct((1, d), F32),
                   jax.ShapeDtypeStruct((2, d, f), BF16), jax.ShapeDtypeStruct((f, d), BF16)],
        scratch_shapes=[pltpu.VMEM((d, fc), F32), pltpu.VMEM((d, fc), F32), pltpu.VMEM((fc, d), F32)],
        compiler_params=_params(dimension_semantics=("arbitrary",)),
    )(x, hs, gain, dy, w_in, w_in, w_out)


def _proj_fwd(x, gain, w, wba, layer):
    t, d = x.shape
    n = w.shape[2]
    nc = _pick(n, (512, 256, 128))
    rt = _pick(t, (512, 256, 128))

    def body(x_ref, g_ref, w_ref, wba_ref, p_ref, ba_ref, hs_ref):
        @pl.when(pl.program_id(0) == 0)
        def _():
            for r in range(t // rt):
                rows = pl.ds(r * rt, rt)
                h = _rms(x_ref[rows, :], g_ref[...], RMS_EPS).astype(BF16)
                hs_ref[rows, :] = h
                ba_ref[rows, :] = _mm_nn(h, wba_ref[...])

        for r in range(t // rt):
            rows = pl.ds(r * rt, rt)
            p_ref[rows, :] = _mm_nn(hs_ref[rows, :], w_ref[...])

    return pl.pallas_call(
        body, name="proj_fwd", grid=(n // nc,),
        in_specs=[_const_spec((t, d), True), _const_spec((1, d)),
                  _layer_spec(layer, (d, nc), lambda j: (0, j)), _layer_spec(layer, (d, LANES), lambda j: (0, 0))],
        out_specs=[pl.BlockSpec((t, nc), lambda j: (0, j)), _const_spec((t, LANES)), _const_spec((t, d))],
        out_shape=[jax.ShapeDtypeStruct((t, n), F32), jax.ShapeDtypeStruct((t, LANES), F32),
                   jax.ShapeDtypeStruct((t, d), BF16)],
        compiler_params=_params(dimension_semantics=("arbitrary",)),
    )(x, gain, w, wba)


def _proj_bwd(x, hs, gain, dres, dp, dba, w, wba, layer):
    t, d = x.shape
    n = w.shape[2]
    nc = _pick(n, (512, 256, 128))
    nj = n // nc
    rt = _pick(t, (512, 256, 128))
    nr = t // rt

    def body(x_ref, hs_ref, g_ref, dres_ref, dp_ref, dba_ref, w_ref, wba_ref, dx_ref, dg_ref, dw_ref, dwba_ref, dw_acc):
        j = pl.program_id(0)

        @pl.when(j == 0)
        def _():
            for r in range(nr):
                rows = pl.ds(r * rt, rt)
                h = hs_ref[rows, :]
                g = dba_ref[rows, :]
                dx_ref[rows, :] = _mm_nt(g, wba_ref[...])
                if r == 0:
                    dwba_ref[...] = _mm_tn(h, g)
                else:
                    dwba_ref[...] += _mm_tn(h, g)

        for r in range(nr):
            rows = pl.ds(r * rt, rt)
            g = dp_ref[rows, :]
            dx_ref[rows, :] += _mm_nt(g, w_ref[...])
            if r == 0:
                dw_acc[...] = _mm_tn(hs_ref[rows, :], g)
            else:
                dw_acc[...] += _mm_tn(hs_ref[rows, :], g)
        dw_ref[...] = dw_acc[...].astype(BF16)

        @pl.when(j == nj - 1)
        def _():
            for r in range(nr):
                rows = pl.ds(r * rt, rt)
                _, vjp = jax.vjp(lambda xx, gg: _rms(xx, gg, RMS_EPS), x_ref[rows, :], g_ref[...])
                dxn, dgr = vjp(dx_ref[rows, :])
                dx_ref[rows, :] = dres_ref[rows, :] + dxn
                if r == 0:
                    dg_ref[...] = dgr
                else:
                    dg_ref[...] += dgr

    return pl.pallas_call(
        body, name="proj_bwd", grid=(nj,),
        in_specs=[_const_spec((t, d), True), _const_spec((t, d), True), _const_spec((1, d)), _const_spec((t, d), True),
                  pl.BlockSpec((t, nc), lambda j: (0, j)), _const_spec((t, LANES)),
                  _layer_spec(layer, (d, nc), lambda j: (0, j)), _layer_spec(layer, (d, LANES), lambda j: (0, 0))],
        out_specs=[_const_spec((t, d)), _const_spec((1, d)),
                   pl.BlockSpec((d, nc), lambda j: (0, j)), _const_spec((d, LANES))],
        out_shape=[jax.ShapeDtypeStruct((t, d), F32), jax.ShapeDtypeStruct((1, d), F32),
                   jax.ShapeDtypeStruct((d, n), BF16), jax.ShapeDtypeStruct((d, LANES), F32)],
        scratch_shapes=[pltpu.VMEM((d, nc), F32)],
        compiler_params=_params(dimension_semantics=("arbitrary",)),
    )(x, hs, gain, dres, dp, dba, w, wba)


def _conv_act(x, w0, w1, w2, w3, is_qk):
    y = w3 * x + w2 * _shift_down(x, 1) + w1 * _shift_down(x, 2) + w0 * _shift_down(x, 3)
    y = _silu(y)
    inv = lax.rsqrt(jnp.sum(y * y, axis=-1, keepdims=True) + L2_EPS)
    return y * (is_qk * inv + (1.0 - is_qk))


def _taps(w_ref):
    return tuple(w_ref[i:i + 1, :] for i in range(DN_CONV))


def _dn_prep_fwd(proj, conv_w, n_heads):
    t = proj.shape[0]
    nb = 3 * n_heads

    def body(x_ref, w_ref, o_ref):
        is_qk = jnp.where(pl.program_id(0) < 2 * n_heads, 1.0, 0.0).astype(F32)
        o_ref[...] = _conv_act(x_ref[...], *_taps(w_ref), is_qk)

    return pl.pallas_call(
        body, name="dn_prep_fwd", grid=(nb,),
        in_specs=[pl.BlockSpec((t, HEAD_DIM), lambda i: (0, i)), pl.BlockSpec((CONV_ROWS, HEAD_DIM), lambda i: (0, i))],
        out_specs=pl.BlockSpec((t, HEAD_DIM), lambda i: (0, i)),
        out_shape=jax.ShapeDtypeStruct((t, nb * HEAD_DIM), F32),
        compiler_params=_params(dimension_semantics=("arbitrary",)),
    )(proj, conv_w)


def _dn_prep_bwd(proj, conv_w, dact, n_heads):
    t = proj.shape[0]
    nb = 3 * n_heads

    def body(x_ref, w_ref, g_ref, dx_ref, dw_ref):
        is_qk = jnp.where(pl.program_id(0) < 2 * n_heads, 1.0, 0.0).astype(F32)
        _, vjp = jax.vjp(lambda x, a, b, c, e: _conv_act(x, a, b, c, e, is_qk), x_ref[...], *_taps(w_ref))
        dx, d0, d1, d2, d3 = vjp(g_ref[...])
        dx_ref[...] = dx.astype(BF16)
        dw_ref[...] = jnp.concatenate([d0, d1, d2, d3, jnp.zeros((CONV_ROWS - DN_CONV, HEAD_DIM), F32)], axis=0)

    return pl.pallas_call(
        body, name="dn_prep_bwd", grid=(nb,),
        in_specs=[pl.BlockSpec((t, HEAD_DIM), lambda i: (0, i)), pl.BlockSpec((CONV_ROWS, HEAD_DIM), lambda i: (0, i)),
                  pl.BlockSpec((t, HEAD_DIM), lambda i: (0, i))],
        out_specs=[pl.BlockSpec((t, HEAD_DIM), lambda i: (0, i)), pl.BlockSpec((CONV_ROWS, HEAD_DIM), lambda i: (0, i))],
        out_shape=[jax.ShapeDtypeStruct((t, nb * HEAD_DIM), BF16), jax.ShapeDtypeStruct((CONV_ROWS, nb * HEAD_DIM), F32)],
        compiler_params=_params(dimension_semantics=("arbitrary",)),
    )(proj, conv_w, dact)


def _unit_lower_inverses(lmats, c):
    r = lax.broadcasted_iota(jnp.int32, (c, c), 0)
    q = lax.broadcasted_iota(jnp.int32, (c, c), 1)
    eye = jnp.where(r == q, 1.0, 0.0)
    ps = [eye - l for l in lmats]
    ms = [_p_nn(l, l) for l in lmats]
    n = 2
    while 2 * n < c:
        both = [_halves(_p_nn(jnp.concatenate([p, m], axis=0), m)) for p, m in zip(ps, ms)]
        ps = [p + pm for p, (pm, _) in zip(ps, both)]
        ms = [mm for _, mm in both]
        n *= 2
    return [p + _p_nn(p, m) for p, m in zip(ps, ms)]


def _delta_heads(qs, ks, vs, bg, alog, dtb, states):
    n_heads = len(qs)
    heads = range(n_heads)
    c = qs[0].shape[0]
    lane = lax.broadcasted_iota(jnp.int32, (c, LANES), 1)
    r = lax.broadcasted_iota(jnp.int32, (c, c), 0)
    s = lax.broadcasted_iota(jnp.int32, (c, c), 1)
    beta_all = jax.nn.sigmoid(bg)
    g_all = -jnp.exp(alog) * _softplus(bg + dtb)
    beta = [jnp.sum(jnp.where(lane == h, beta_all, 0.0), axis=1, keepdims=True) for h in heads]
    g = [jnp.sum(jnp.where(lane == n_heads + h, g_all, 0.0), axis=1, keepdims=True) for h in heads]
    g_row = [jnp.sum(jnp.where(r == s, g[h], 0.0), axis=0, keepdims=True) for h in heads]
    gc = [jnp.sum(jnp.where(s <= r, g_row[h], 0.0), axis=1, keepdims=True) for h in heads]
    gr = [jnp.sum(jnp.where(r <= s, g[h], 0.0), axis=0, keepdims=True) for h in heads]
    g_last = [jnp.sum(g[h], axis=0, keepdims=True) for h in heads]
    decay = [jnp.exp(jnp.where(r >= s, gc[h] - gr[h], NEG_BIG)) for h in heads]
    q_scaled = [qs[h] * (HEAD_DIM ** -0.5) for h in heads]
    k_beta = [ks[h] * beta[h] for h in heads]
    kk_qk = [_halves(_d_nt(jnp.concatenate([k_beta[h], q_scaled[h]], axis=0), ks[h])) for h in heads]
    lmat = [jnp.where(r > s, kk_qk[h][0] * decay[h], 0.0) for h in heads]
    attn = [kk_qk[h][1] * decay[h] for h in heads]
    tinv = _unit_lower_inverses(lmat, c)
    u_w = [_col_halves(_p_nn(tinv[h], jnp.concatenate([vs[h] * beta[h], k_beta[h] * jnp.exp(gc[h])], axis=1)))
           for h in heads]
    u = [u_w[h][0] for h in heads]
    w = [u_w[h][1] for h in heads]
    ws_qs = [_halves(_d_nn(jnp.concatenate([w[h], q_scaled[h] * jnp.exp(gc[h])], axis=0), states[h])) for h in heads]
    v_new = [u[h] - ws_qs[h][0] for h in heads]
    o = [ws_qs[h][1] + _d_nn(attn[h], v_new[h]) for h in heads]
    kv = [_d_tn(ks[h] * jnp.exp(g_last[h] - gc[h]), v_new[h]) for h in heads]
    new_states = [states[h] * jnp.exp(g_last[h]) + kv[h] for h in heads]
    return tuple(o), tuple(new_states)


def _delta_fwd(act, ba, alog, dtb, n_heads):
    t = act.shape[0]
    d = n_heads * HEAD_DIM
    c = DN_CHUNK
    nc = t // c

    def body(q_ref, k_ref, v_ref, bg_ref, al_ref, dt_ref, o_ref, snap_ref, st_ref):
        @pl.when(pl.program_id(0) == 0)
        def _():
            st_ref[...] = jnp.zeros(st_ref.shape, F32)

        snap_ref[0] = st_ref[...]
        cols = [slice(h * HEAD_DIM, (h + 1) * HEAD_DIM) for h in range(n_heads)]
        os, new_states = _delta_heads([q_ref[:, sl] for sl in cols], [k_ref[:, sl] for sl in cols],
                                      [v_ref[:, sl] for sl in cols], bg_ref[...], al_ref[...], dt_ref[...],
                                      [st_ref[h] for h in range(n_heads)])
        for h, sl in enumerate(cols):
            o_ref[:, sl] = os[h]
            st_ref[h] = new_states[h]

    return pl.pallas_call(
        body, name="delta_fwd", grid=(nc,),
        in_specs=[pl.BlockSpec((c, d), lambda i: (i, 0)), pl.BlockSpec((c, d), lambda i: (i, 1)),
                  pl.BlockSpec((c, d), lambda i: (i, 2)), pl.BlockSpec((c, LANES), lambda i: (i, 0)),
                  _const_spec((1, LANES)), _const_spec((1, LANES))],
        out_specs=[pl.BlockSpec((c, d), lambda i: (i, 0)),
                   pl.BlockSpec((1, n_heads, HEAD_DIM, HEAD_DIM), lambda i: (i, 0, 0, 0))],
        out_shape=[jax.ShapeDtypeStruct((t, d), F32), jax.ShapeDtypeStruct((nc, n_heads, HEAD_DIM, HEAD_DIM), F32)],
        scratch_shapes=[pltpu.VMEM((n_heads, HEAD_DIM, HEAD_DIM), F32)],
        compiler_params=_params(dimension_semantics=("arbitrary",)),
    )(act, act, act, ba, alog, dtb)


def _delta_bwd(act, ba, alog, dtb, snaps, do, n_heads):
    t = act.shape[0]
    d = n_heads * HEAD_DIM
    c = DN_CHUNK
    nc = t // c

    def body(q_ref, k_ref, v_ref, bg_ref, al_ref, dt_ref, snap_ref, do_ref,
             dact_ref, dbg_ref, dal_ref, ddt_ref, ds_ref):
        @pl.when(pl.program_id(0) == 0)
        def _():
            ds_ref[...] = jnp.zeros(ds_ref.shape, F32)
            dal_ref[...] = jnp.zeros((1, LANES), F32)
            ddt_ref[...] = jnp.zeros((1, LANES), F32)

        heads = range(n_heads)
        cols = [slice(h * HEAD_DIM, (h + 1) * HEAD_DIM) for h in heads]
        _, vjp = jax.vjp(_delta_heads, tuple(q_ref[:, sl] for sl in cols), tuple(k_ref[:, sl] for sl in cols),
                         tuple(v_ref[:, sl] for sl in cols), bg_ref[...], al_ref[...], dt_ref[...],
                         tuple(snap_ref[0, h] for h in heads))
        dq, dk, dv, dbg, dal, ddt, dst = vjp((tuple(do_ref[:, sl] for sl in cols), tuple(ds_ref[h] for h in heads)))
        for h, sl in enumerate(cols):
            dact_ref[:, sl] = dq[h]
            dact_ref[:, d + h * HEAD_DIM:d + (h + 1) * HEAD_DIM] = dk[h]
            dact_ref[:, 2 * d + h * HEAD_DIM:2 * d + (h + 1) * HEAD_DIM] = dv[h]
            ds_ref[h] = dst[h]
        dal_ref[...] += dal
        ddt_ref[...] += ddt
        dbg_ref[...] = dbg.astype(BF16)

    rev = lambda i: nc - 1 - i
    return pl.pallas_call(
        body, name="delta_bwd", grid=(nc,),
        in_specs=[pl.BlockSpec((c, d), lambda i: (rev(i), 0)), pl.BlockSpec((c, d), lambda i: (rev(i), 1)),
                  pl.BlockSpec((c, d), lambda i: (rev(i), 2)), pl.BlockSpec((c, LANES), lambda i: (rev(i), 0)),
                  _const_spec((1, LANES)), _const_spec((1, LANES)),
                  pl.BlockSpec((1, n_heads, HEAD_DIM, HEAD_DIM), lambda i: (rev(i), 0, 0, 0)),
                  pl.BlockSpec((c, d), lambda i: (rev(i), 0))],
        out_specs=[pl.BlockSpec((c, 3 * d), lambda i: (rev(i), 0)), pl.BlockSpec((c, LANES), lambda i: (rev(i), 0)),
                   _const_spec((1, LANES)), _const_spec((1, LANES))],
        out_shape=[jax.ShapeDtypeStruct((t, 3 * d), F32), jax.ShapeDtypeStruct((t, LANES), BF16),
                   jax.ShapeDtypeStruct((1, LANES), F32), jax.ShapeDtypeStruct((1, LANES), F32)],
        scratch_shapes=[pltpu.VMEM((n_heads, HEAD_DIM, HEAD_DIM), F32)],
        compiler_params=_params(dimension_semantics=("arbitrary",)),
    )(act, act, act, ba, alog, dtb, snaps, do)


def _head_norm2(a, b, ga, gb):
    return _rms(a, ga, RMS_EPS), _rms(b, gb, RMS_EPS)


def _sb_prep_fwd(proj, qn, kn, n_heads):
    t = proj.shape[0]
    d = n_heads * HEAD_DIM
    tm = _pick(t, (256, 128))

    def body(q_ref, k_ref, v_ref, qn_ref, kn_ref, qo_ref, ko_ref, vo_ref):
        for h in range(n_heads):
            sl = slice(h * HEAD_DIM, (h + 1) * HEAD_DIM)
            qh, kh = _head_norm2(q_ref[:, sl], k_ref[:, sl], qn_ref[...], kn_ref[...])
            qo_ref[:, sl] = qh.astype(BF16)
            ko_ref[:, sl] = kh.astype(BF16)
        vo_ref[...] = v_ref[...].astype(BF16)

    tile = lambda k: pl.BlockSpec((tm, d), lambda i: (i, k))
    return pl.pallas_call(
        body, name="sb_prep_fwd", grid=(t // tm,),
        in_specs=[tile(4), tile(5), tile(6), _const_spec((1, HEAD_DIM)), _const_spec((1, HEAD_DIM))],
        out_specs=[tile(0), tile(0), tile(0)],
        out_shape=[jax.ShapeDtypeStruct((t, d), BF16)] * 3,
        compiler_params=_params(dimension_semantics=("arbitrary",)),
    )(proj, proj, proj, qn, kn)


def _sb_prep_bwd(proj, qn, kn, dq, dk, n_heads):
    t = proj.shape[0]
    d = n_heads * HEAD_DIM
    tm = _pick(t, (256, 128))

    def body(q_ref, k_ref, qn_ref, kn_ref, dq_ref, dk_ref, dqo_ref, dko_ref, dqn_ref, dkn_ref):
        @pl.when(pl.program_id(0) == 0)
        def _():
            dqn_ref[...] = jnp.zeros((1, HEAD_DIM), F32)
            dkn_ref[...] = jnp.zeros((1, HEAD_DIM), F32)

        for h in range(n_heads):
            sl = slice(h * HEAD_DIM, (h + 1) * HEAD_DIM)
            _, vjp = jax.vjp(_head_norm2, q_ref[:, sl], k_ref[:, sl], qn_ref[...], kn_ref[...])
            da, db, dga, dgb = vjp((dq_ref[:, sl], dk_ref[:, sl]))
            dqo_ref[:, sl] = da.astype(BF16)
            dko_ref[:, sl] = db.astype(BF16)
            dqn_ref[...] += dga
            dkn_ref[...] += dgb

    return pl.pallas_call(
        body, name="sb_prep_bwd", grid=(t // tm,),
        in_specs=[pl.BlockSpec((tm, d), lambda i: (i, 4)), pl.BlockSpec((tm, d), lambda i: (i, 5)),
                  _const_spec((1, HEAD_DIM)), _const_spec((1, HEAD_DIM)),
                  pl.BlockSpec((tm, d), lambda i: (i, 0)), pl.BlockSpec((tm, d), lambda i: (i, 0))],
        out_specs=[pl.BlockSpec((tm, d), lambda i: (i, 0)), pl.BlockSpec((tm, d), lambda i: (i, 0)),
                   _const_spec((1, HEAD_DIM)), _const_spec((1, HEAD_DIM))],
        out_shape=[jax.ShapeDtypeStruct((t, d), BF16), jax.ShapeDtypeStruct((t, d), BF16),
                   jax.ShapeDtypeStruct((1, HEAD_DIM), F32), jax.ShapeDtypeStruct((1, HEAD_DIM), F32)],
        compiler_params=_params(dimension_semantics=("arbitrary",)),
    )(proj, proj, qn, kn, dq, dk)


def _cumsum_mm(x, tri):
    hi, lo = _split(x)
    return (lax.dot_general(hi, tri, (_NN, ((), ())), preferred_element_type=F32)
            + lax.dot_general(lo, tri, (_NN, ((), ())), preferred_element_type=F32))


def _sb_valid(i, j):
    row = lax.broadcasted_iota(jnp.int32, (SB_BLOCK, SB_BLOCK), 0)
    col = lax.broadcasted_iota(jnp.int32, (SB_BLOCK, SB_BLOCK), 1)
    return (col + j * SB_BLOCK) < (row + i * SB_BLOCK)


def _sb_attn_fwd(qb, kb, vb, n_heads):
    t = qb.shape[0]
    d = n_heads * HEAD_DIM
    nq = t // SB_BLOCK
    hb = _pick(n_heads, (8, 4, 2, 1))
    wide = hb * HEAD_DIM
    scale = HEAD_DIM ** -0.5

    def body(q_ref, k_ref, v_ref, o_ref, lt_ref):
        i = pl.program_id(1)
        row = lax.broadcasted_iota(jnp.int32, (SB_BLOCK, SB_BLOCK), 0)
        col = lax.broadcasted_iota(jnp.int32, (SB_BLOCK, SB_BLOCK), 1)
        after = jnp.where(row > col, 1.0, 0.0).astype(BF16)
        heads = [slice(h * HEAD_DIM, (h + 1) * HEAD_DIM) for h in range(hb)]
        every = range(hb)
        qs = [q_ref[:, sl].astype(BF16) for sl in heads]

        def step(m, carry):
            j0 = i - 2 * m
            js = (j0, jnp.maximum(j0 - 1, 0))
            valid = (_sb_valid(i, js[0]), jnp.logical_and(_sb_valid(i, js[1]), j0 >= 1))
            rows = [pl.ds(pl.multiple_of(j * SB_BLOCK, SB_BLOCK), SB_BLOCK) for j in js]
            units = [(h, b) for b in range(2) for h in every]
            z = {u: _mm_nt(qs[u[0]], k_ref[rows[u[1]], heads[u[0]]]) * scale for u in units}
            sp = {u: _softplus_raw(z[u]) for u in units}
            lm = {u: jnp.where(valid[u[1]], -sp[u], 0.0) for u in units}
            tail = {u: _cumsum_mm(lm[u], after) for u in units}
            later = {(h, 0): carry[h][1] for h in every}
            later.update({(h, 1): carry[h][1] + jnp.sum(lm[h, 0], axis=1, keepdims=True) for h in every})
            w = {u: jnp.where(valid[u[1]], jnp.exp(z[u] - sp[u] + later[u] + tail[u]), 0.0) for u in units}
            pv = {u: _mm_nn(w[u], v_ref[rows[u[1]], heads[u[0]]]) for u in units}
            return tuple((carry[h][0] + pv[h, 0] + pv[h, 1], later[h, 1] + jnp.sum(lm[h, 1], axis=1, keepdims=True))
                         for h in every)

        init = tuple((jnp.zeros((SB_BLOCK, HEAD_DIM), F32), jnp.zeros((SB_BLOCK, 1), F32)) for _ in heads)
        res = lax.fori_loop(0, (i + 2) // 2, step, init)
        for h, sl in enumerate(heads):
            o_ref[:, sl] = res[h][0]
            lt_ref[:, sl] = jnp.broadcast_to(res[h][1], (SB_BLOCK, HEAD_DIM))

    return pl.pallas_call(
        body, name="sb_attn_fwd", grid=(n_heads // hb, nq),
        in_specs=[pl.BlockSpec((SB_BLOCK, wide), lambda g, i: (i, g)),
                  pl.BlockSpec((t, wide), lambda g, i: (0, g)),
                  pl.BlockSpec((t, wide), lambda g, i: (0, g))],
        out_specs=[pl.BlockSpec((SB_BLOCK, wide), lambda g, i: (i, g)),
                   pl.BlockSpec((SB_BLOCK, wide), lambda g, i: (i, g))],
        out_shape=[jax.ShapeDtypeStruct((t, d), F32), jax.ShapeDtypeStruct((t, d), F32)],
        compiler_params=_params(dimension_semantics=("arbitrary", "arbitrary")),
    )(qb, kb, vb)


def _sb_attn_bwd(qb, kb, vb, ltot, do, n_heads):
    t = qb.shape[0]
    d = n_heads * HEAD_DIM
    nq = t // SB_BLOCK
    hb = _pick(n_heads, (8, 4, 2, 1))
    wide = hb * HEAD_DIM
    scale = HEAD_DIM ** -0.5

    def body(q_ref, k_ref, v_ref, lt_ref, do_ref, dq_ref, dk_ref, dv_ref):
        i = pl.program_id(1)

        @pl.when(i == 0)
        def _():
            dk_ref[...] = jnp.zeros((t, wide), F32)
            dv_ref[...] = jnp.zeros((t, wide), F32)

        row = lax.broadcasted_iota(jnp.int32, (SB_BLOCK, SB_BLOCK), 0)
        col = lax.broadcasted_iota(jnp.int32, (SB_BLOCK, SB_BLOCK), 1)
        upto = jnp.where(row <= col, 1.0, 0.0).astype(BF16)
        before = jnp.where(row < col, 1.0, 0.0).astype(BF16)
        heads = [slice(h * HEAD_DIM, (h + 1) * HEAD_DIM) for h in range(hb)]
        every = range(hb)
        qs = [q_ref[:, sl].astype(BF16) for sl in heads]
        dos = [do_ref[:, sl].astype(BF16) for sl in heads]
        totals = [jnp.max(lt_ref[:, sl], axis=1, keepdims=True) for sl in heads]

        def step(m, carry):
            js = (2 * m, jnp.minimum(2 * m + 1, nq - 1))
            valid = (_sb_valid(i, js[0]), jnp.logical_and(_sb_valid(i, js[1]), 2 * m + 1 <= i))
            rows = [pl.ds(pl.multiple_of(j * SB_BLOCK, SB_BLOCK), SB_BLOCK) for j in js]
            units = [(h, b) for b in range(2) for h in every]
            kj = {u: k_ref[rows[u[1]], heads[u[0]]].astype(BF16) for u in units}
            vj = {u: v_ref[rows[u[1]], heads[u[0]]].astype(BF16) for u in units}
            z = {u: _mm_nt(qs[u[0]], kj[u]) * scale for u in units}
            dw = {u: _mm_nt(dos[u[0]], vj[u]) for u in units}
            sp = {u: _softplus_raw(z[u]) for u in units}
            lm = {u: jnp.where(valid[u[1]], -sp[u], 0.0) for u in units}
            head = {u: _cumsum_mm(lm[u], upto) for u in units}
            lm_before = {(h, 0): carry[h][1] for h in every}
            lm_before.update({(h, 1): carry[h][1] + jnp.sum(lm[h, 0], axis=1, keepdims=True) for h in every})
            w = {u: jnp.where(valid[u[1]], jnp.exp(z[u] - sp[u] + totals[u[0]] - (lm_before[u] + head[u])), 0.0)
                 for u in units}
            e = {u: w[u] * dw[u] for u in units}
            e_local = {u: _mm_nn(e[u], before) for u in units}
            e_before = {(h, 0): carry[h][2] for h in every}
            e_before.update({(h, 1): carry[h][2] + jnp.sum(e[h, 0], axis=1, keepdims=True) for h in every})
            sig = {u: jnp.exp(z[u] - sp[u]) for u in units}
            dz = {u: jnp.where(valid[u[1]], e[u] * (1.0 - sig[u]) - (e_before[u] + e_local[u]) * sig[u], 0.0) * scale
                  for u in units}
            for h, b in units:
                dv_ref[rows[b], heads[h]] += _mm_tn(w[h, b], dos[h])
            for h, b in units:
                dk_ref[rows[b], heads[h]] += _mm_tn(dz[h, b], qs[h])
            dq = {u: _mm_nn(dz[u], kj[u]) for u in units}
            return tuple((carry[h][0] + dq[h, 0] + dq[h, 1],
                          lm_before[h, 1] + jnp.sum(lm[h, 1], axis=1, keepdims=True),
                          e_before[h, 1] + jnp.sum(e[h, 1], axis=1, keepdims=True)) for h in every)

        zero_col = jnp.zeros((SB_BLOCK, 1), F32)
        init = tuple((jnp.zeros((SB_BLOCK, HEAD_DIM), F32), zero_col, zero_col) for _ in heads)
        res = lax.fori_loop(0, (i + 2) // 2, step, init)
        for h, sl in enumerate(heads):
            dq_ref[:, sl] = res[h][0]

    return pl.pallas_call(
        body, name="sb_attn_bwd", grid=(n_heads // hb, nq),
        in_specs=[pl.BlockSpec((SB_BLOCK, wide), lambda g, i: (i, g)),
                  pl.BlockSpec((t, wide), lambda g, i: (0, g)),
                  pl.BlockSpec((t, wide), lambda g, i: (0, g)),
                  pl.BlockSpec((SB_BLOCK, wide), lambda g, i: (i, g)),
                  pl.BlockSpec((SB_BLOCK, wide), lambda g, i: (i, g))],
        out_specs=[pl.BlockSpec((SB_BLOCK, wide), lambda g, i: (i, g)),
                   pl.BlockSpec((t, wide), lambda g, i: (0, g)),
                   pl.BlockSpec((t, wide), lambda g, i: (0, g))],
        out_shape=[jax.ShapeDtypeStruct((t, d), F32), jax.ShapeDtypeStruct((t, d), F32),
                   jax.ShapeDtypeStruct((t, d), F32)],
        compiler_params=_params(dimension_semantics=("arbitrary", "arbitrary")),
    )(qb, kb, vb, ltot, do)


def _gated_norm(oa, z, gn):
    return _rms(oa, gn, RMS_EPS) * _silu(z)


def _merge_gates(ya, yb, ga, gb):
    return jax.nn.sigmoid(ga) * ya + jax.nn.sigmoid(gb) * yb


def _merge_fwd(x1, oa, proj, ob, gn, wa, wb, wo, layer, n_heads):
    t, d = x1.shape
    tm = _pick(t, (256, 128))
    square = _layer_spec(layer, (d, d), lambda i: (0, 0), single=True)

    def body(x_ref, oa_ref, z_ref, ob_ref, ga_ref, gb_ref, gn_ref, wa_ref, wb_ref, wo_ref, o_ref, na_ref):
        for h in range(n_heads):
            sl = slice(h * HEAD_DIM, (h + 1) * HEAD_DIM)
            na_ref[:, sl] = _gated_norm(oa_ref[:, sl], z_ref[:, sl], gn_ref[...]).astype(BF16)
        m = _merge_gates(_mm_nn(na_ref[...], wa_ref[...]), _mm_nn(ob_ref[...], wb_ref[...]), ga_ref[...], gb_ref[...])
        o_ref[...] = x_ref[...] + _mm_nn(m, wo_ref[...])

    tile = lambda k: pl.BlockSpec((tm, d), lambda i: (i, k))
    return pl.pallas_call(
        body, name="merge_fwd", grid=(t // tm,),
        in_specs=[tile(0), tile(0), tile(3), tile(0), tile(7), tile(8), _const_spec((1, HEAD_DIM)),
                  square, square, square],
        out_specs=tile(0),
        out_shape=jax.ShapeDtypeStruct((t, d), F32),
        scratch_shapes=[pltpu.VMEM((tm, d), BF16)],
        compiler_params=_params(dimension_semantics=("arbitrary",)),
    )(x1, oa, proj, ob, proj, proj, gn, wa, wb, wo)


def _merge_bwd(oa, proj, ob, dy, gn, wa, wb, wo, layer, n_heads):
    t, d = oa.shape
    tm = _pick(t, (256, 128))
    nt = t // tm
    square = _layer_spec(layer, (d, d), lambda i: (0, 0), single=True)

    def body(oa_ref, z_ref, ob_ref, ga_ref, gb_ref, dy_ref, gn_ref, wa_ref, wb_ref, wo_ref,
             doa_ref, dz_ref, dob_ref, dga_ref, dgb_ref, dgn_ref, dwa_hbm, dwb_hbm, dwo_hbm,
             na_ref, dna_ref, dwa_ref, dwb_ref, dwo_ref, stage_ref):
        i = pl.program_id(0)

        @pl.when(i == 0)
        def _():
            dgn_ref[...] = jnp.zeros((1, HEAD_DIM), F32)
            dwa_ref[...] = jnp.zeros((d, d), F32)
            dwb_ref[...] = jnp.zeros((d, d), F32)
            dwo_ref[...] = jnp.zeros((d, d), F32)

        for h in range(n_heads):
            sl = slice(h * HEAD_DIM, (h + 1) * HEAD_DIM)
            na_ref[:, sl] = _gated_norm(oa_ref[:, sl], z_ref[:, sl], gn_ref[...]).astype(BF16)
        dy = dy_ref[...].astype(BF16)
        ob = ob_ref[...].astype(BF16)
        ya = _mm_nn(na_ref[...], wa_ref[...])
        yb = _mm_nn(ob, wb_ref[...])
        m, vjp = jax.vjp(_merge_gates, ya, yb, ga_ref[...], gb_ref[...])
        dwo_ref[...] += _mm_tn(m, dy)
        dya, dyb, dga, dgb = vjp(_mm_nt(dy, wo_ref[...]))
        dga_ref[...] = dga.astype(BF16)
        dgb_ref[...] = dgb.astype(BF16)
        dwa_ref[...] += _mm_tn(na_ref[...], dya)
        dwb_ref[...] += _mm_tn(ob, dyb)
        dob_ref[...] = _mm_nt(dyb, wb_ref[...])
        dna_ref[...] = _mm_nt(dya, wa_ref[...])
        for h in range(n_heads):
            sl = slice(h * HEAD_DIM, (h + 1) * HEAD_DIM)
            _, vjp_h = jax.vjp(_gated_norm, oa_ref[:, sl], z_ref[:, sl], gn_ref[...])
            doa, dz, dgn = vjp_h(dna_ref[:, sl])
            doa_ref[:, sl] = doa
            dz_ref[:, sl] = dz.astype(BF16)
            dgn_ref[...] += dgn

        @pl.when(i == nt - 1)
        def _():
            for acc, out in ((dwa_ref, dwa_hbm), (dwb_ref, dwb_hbm), (dwo_ref, dwo_hbm)):
                stage_ref[...] = acc[...].astype(BF16)
                pltpu.sync_copy(stage_ref, out)

    tile = lambda k: pl.BlockSpec((tm, d), lambda i: (i, k))
    any_spec = pl.BlockSpec(memory_space=pl.ANY)
    return pl.pallas_call(
        body, name="merge_bwd", grid=(nt,),
        in_specs=[tile(0), tile(3), tile(0), tile(7), tile(8), tile(0), _const_spec((1, HEAD_DIM)),
                  square, square, square],
        out_specs=[tile(0), tile(0), tile(0), tile(0), tile(0), _const_spec((1, HEAD_DIM)),
                   any_spec, any_spec, any_spec],
        out_shape=[jax.ShapeDtypeStruct((t, d), F32), jax.ShapeDtypeStruct((t, d), BF16),
                   jax.ShapeDtypeStruct((t, d), F32), jax.ShapeDtypeStruct((t, d), BF16),
                   jax.ShapeDtypeStruct((t, d), BF16), jax.ShapeDtypeStruct((1, HEAD_DIM), F32),
                   jax.ShapeDtypeStruct((d, d), BF16), jax.ShapeDtypeStruct((d, d), BF16),
                   jax.ShapeDtypeStruct((d, d), BF16)],
        scratch_shapes=[pltpu.VMEM((tm, d), BF16), pltpu.VMEM((tm, d), F32),
                        pltpu.VMEM((d, d), F32), pltpu.VMEM((d, d), F32), pltpu.VMEM((d, d), F32),
                        pltpu.VMEM((d, d), BF16)],
        compiler_params=_params(dimension_semantics=("arbitrary",)),
    )(oa, proj, ob, proj, proj, dy, gn, wa, wb, wo)


def _loss_head(y, target):
    t, d = y.shape
    tm = _pick(t, (256, 128))

    def body(y_ref, t_ref, dy_ref, loss_ref):
        @pl.when(pl.program_id(0) == 0)
        def _():
            loss_ref[...] = jnp.zeros((8, LANES), F32)

        err = y_ref[...] - t_ref[...]
        dy_ref[...] = err * (1.0 / d)
        per_token = jnp.sum(err * err, axis=1, keepdims=True) * (1.0 / d)
        loss_ref[...] += 0.5 * jnp.sum(per_token, axis=0, keepdims=True)

    return pl.pallas_call(
        body, name="loss_head", grid=(t // tm,),
        in_specs=[pl.BlockSpec((tm, d), lambda i: (i, 0)), pl.BlockSpec((tm, d), lambda i: (i, 0))],
        out_specs=[pl.BlockSpec((tm, d), lambda i: (i, 0)), _const_spec((8, LANES))],
        out_shape=[jax.ShapeDtypeStruct((t, d), F32), jax.ShapeDtypeStruct((8, LANES), F32)],
        compiler_params=_params(dimension_semantics=("arbitrary",)),
    )(y, target)


def _adamw(w, g, m, v):
    rows, cols = w.shape
    tr = rows
    for cand in (512, 256, 128, 64, 32, 16, 8):
        if rows % cand == 0 and cand * cols * 4 <= 2 * 1024 * 1024:
            tr = cand
            break

    def body(w_ref, g_ref, m_ref, v_ref, d_ref, mo_ref, vo_ref):
        g = g_ref[...]
        m2 = ADAM_B1 * m_ref[...] + (1.0 - ADAM_B1) * g
        v2 = ADAM_B2 * v_ref[...] + (1.0 - ADAM_B2) * (g * g)
        m_hat = m2 / (1.0 - ADAM_B1 ** ADAM_STEP)
        v_hat = v2 / (1.0 - ADAM_B2 ** ADAM_STEP)
        d_ref[...] = -ADAM_LR * (m_hat / (jnp.sqrt(v_hat) + ADAM_EPS) + ADAM_WD * w_ref[...])
        mo_ref[...] = m2
        vo_ref[...] = v2

    spec = pl.BlockSpec((tr, cols), lambda i: (i, 0))
    shape = jax.ShapeDtypeStruct((rows, cols), F32)
    return pl.pallas_call(
        body, name="adamw", grid=(rows // tr,), in_specs=[spec] * 4, out_specs=[spec] * 3,
        out_shape=[shape] * 3, compiler_params=_params(dimension_semantics=("arbitrary",)),
    )(w, g, m, v)


def _place():
    return lax.axis_index("x"), lax.axis_index("y"), lax.axis_index("c")


def _other_chips(x, y):
    return [(1 - x, y), (x, 1 - y), (1 - x, 1 - y)]


def _tile_rows(rows, cols, itemsize, cap=1536 * 1024):
    best = None
    for cand in range(16, rows + 1, 16):
        if rows % cand == 0 and cand * cols * itemsize <= cap:
            best = cand
    return best if best is not None else rows


def _allgather_layer(shards, layer, collective_id):
    n = len(shards)

    def body(*refs):
        srcs, outs = refs[:n], refs[n:2 * n]
        send_sems, recv_sems = refs[2 * n:]
        x, y, c = _place()
        me, sibling = (x, y, c), (x, y, 1 - c)
        chips = _other_chips(x, y)
        barrier = pltpu.get_barrier_semaphore()
        for peer in [(*chip, c) for chip in chips] + [sibling]:
            pl.semaphore_signal(barrier, inc=1, device_id=peer, device_id_type=MESH)
        pl.semaphore_wait(barrier, N_CHIPS)

        def half(w, which):
            rows = shards[w].shape[1] // 2
            return pl.ds(which * rows, rows)

        def copy(w, k, shard, which, to, from_src=False):
            part = half(w, which)
            return pltpu.make_async_remote_copy(
                src_ref=srcs[w].at[layer, part] if from_src else outs[w].at[shard, part],
                dst_ref=outs[w].at[shard, part], send_sem=send_sems.at[6 * w + k], recv_sem=recv_sems.at[6 * w + k],
                device_id=to, device_id_type=MESH)

        first = [copy(w, j, 2 * x + y, c, (*chip, c), from_src=True) for j, chip in enumerate(chips) for w in range(n)]
        for cp in first:
            cp.start()
        passed = []
        for j, (cx, cy) in enumerate(chips):
            for w in range(n):
                copy(w, j, 2 * cx + cy, c, me).wait_recv()
                cp = copy(w, 3 + j, 2 * cx + cy, c, sibling)
                cp.start()
                passed.append(cp)
        for j, (cx, cy) in enumerate(chips):
            for w in range(n):
                copy(w, 3 + j, 2 * cx + cy, 1 - c, me).wait_recv()
        for cp in first + passed:
            cp.wait_send()

    return pl.kernel(
        body, name=f"allgather_layer{layer}_id{collective_id}",
        out_type=[jax.ShapeDtypeStruct((N_CHIPS,) + s.shape[1:], s.dtype) for s in shards],
        mesh=plsc.ScalarSubcoreMesh(axis_name="sequencer", num_cores=1),
        scratch_types=[pltpu.SemaphoreType.DMA((6 * n,)), pltpu.SemaphoreType.DMA((6 * n,))],
        compiler_params=pltpu.CompilerParams(collective_id=collective_id),
    )(*shards)


def _swap_halves(grads):
    n = len(grads)

    def body(*refs):
        gs, gots = refs[:n], refs[n:2 * n]
        send_sems, recv_sems = refs[2 * n:]
        x, y, c = _place()
        copies = []
        for w in range(n):
            half = grads[w].shape[1] // 2
            copies.append(pltpu.make_async_remote_copy(
                src_ref=gs[w].at[:, pl.ds((1 - c) * half, half)], dst_ref=gots[w], send_sem=send_sems.at[w],
                recv_sem=recv_sems.at[w], device_id=(x, y, 1 - c), device_id_type=MESH))
        for cp in copies:
            cp.start()
        for cp in copies:
            cp.wait()

    hbm = pl.BlockSpec(memory_space=pl.ANY)
    return pl.pallas_call(
        body, name="swap_halves", in_specs=[hbm] * n, out_specs=[hbm] * n,
        out_shape=[jax.ShapeDtypeStruct((g.shape[0], g.shape[1] // 2, g.shape[2]), g.dtype) for g in grads],
        scratch_shapes=[pltpu.SemaphoreType.DMA((n,)), pltpu.SemaphoreType.DMA((n,))],
    )(*grads)


def _add_half(grad, got, c_idx):
    n, rows, all_cols = grad.shape
    side = N_CHIPS // n
    cols = all_cols // side
    half = rows // 2
    tr = _tile_rows(half, cols, 2)
    nb = half // tr

    def body(c_ref, a_ref, b_ref, o_ref):
        o_ref[...] = (a_ref[...].astype(F32) + b_ref[...].astype(F32)).astype(o_ref.dtype)

    return pl.pallas_call(
        body, name="add_half",
        grid_spec=pltpu.PrefetchScalarGridSpec(
            num_scalar_prefetch=1, grid=(N_CHIPS, nb),
            in_specs=[pl.BlockSpec((1, tr, cols), lambda s, r, c_ref: (s // side, c_ref[0] * nb + r, s % side)),
                      pl.BlockSpec((1, tr, cols), lambda s, r, c_ref: (s // side, r, s % side))],
            out_specs=pl.BlockSpec((1, tr, cols), lambda s, r, c_ref: (s, r, 0))),
        out_shape=jax.ShapeDtypeStruct((N_CHIPS, half, cols), grad.dtype),
        compiler_params=_params(dimension_semantics=("arbitrary", "arbitrary")),
    )(c_idx, grad, got)


def _scatter_partials(parts, layer, collective_id):
    n = len(parts)

    def body(*refs):
        ps, gots = refs[:n], refs[n:2 * n]
        send_sems, recv_sems = refs[2 * n:]
        x, y, c = _place()
        chips = _other_chips(x, y)
        barrier = pltpu.get_barrier_semaphore()
        for chip in chips:
            pl.semaphore_signal(barrier, inc=1, device_id=(*chip, c), device_id_type=MESH)
        pl.semaphore_wait(barrier, N_CHIPS - 1)
        copies = [pltpu.make_async_remote_copy(src_ref=ps[w].at[2 * cx + cy], dst_ref=gots[w].at[j],
                                               send_sem=send_sems.at[3 * w + j], recv_sem=recv_sems.at[3 * w + j],
                                               device_id=(cx, cy, c), device_id_type=MESH)
                  for j, (cx, cy) in enumerate(chips) for w in range(n)]
        for cp in copies:
            cp.start()
        for cp in copies:
            cp.wait()

    return pl.kernel(
        body, name=f"scatter_partials{layer}_id{collective_id}",
        out_type=[jax.ShapeDtypeStruct((N_CHIPS - 1,) + p.shape[1:], p.dtype) for p in parts],
        mesh=plsc.ScalarSubcoreMesh(axis_name="sequencer", num_cores=1),
        scratch_types=[pltpu.SemaphoreType.DMA((3 * n,)), pltpu.SemaphoreType.DMA((3 * n,))],
        compiler_params=pltpu.CompilerParams(collective_id=collective_id),
    )(*parts)


def _sum_partials(part, got, s_idx, c_idx, layer, depth, stacked=None):
    n, half, cols = part.shape
    tr = _tile_rows(half, cols, 2, cap=1024 * 1024)
    nb = half // tr

    def body(s_ref, c_ref, a_ref, b_ref, *rest):
        o_ref = rest[-1]
        acc = a_ref[0].astype(F32)
        for j in range(n - 1):
            acc = acc + b_ref[j].astype(F32)
        o_ref[...] = acc

    in_specs = [pl.BlockSpec((1, tr, cols), lambda r, s_ref, c_ref: (s_ref[0], r, 0)),
                pl.BlockSpec((n - 1, tr, cols), lambda r, s_ref, c_ref: (0, r, 0))]
    operands = [s_idx, c_idx, part, got]
    aliases = {}
    if stacked is not None:
        in_specs.append(pl.BlockSpec(memory_space=pl.ANY))
        operands.append(stacked)
        aliases = {len(operands) - 1: 0}
    return pl.pallas_call(
        body, name="sum_partials",
        grid_spec=pltpu.PrefetchScalarGridSpec(
            num_scalar_prefetch=2, grid=(nb,), in_specs=in_specs,
            out_specs=pl.BlockSpec((None, tr, cols), lambda r, s_ref, c_ref: (layer, c_ref[0] * nb + r, 0))),
        out_shape=jax.ShapeDtypeStruct((depth, 2 * half, cols), F32),
        input_output_aliases=aliases,
        compiler_params=_params(dimension_semantics=("arbitrary",)),
    )(*operands)


def _join_halves(bufs, layer):
    n = len(bufs)

    def body(*refs):
        outs = refs[n:2 * n]
        send_sems, recv_sems = refs[2 * n:]
        x, y, c = _place()
        copies = []
        for w in range(n):
            half = bufs[w].shape[1] // 2
            mine = outs[w].at[layer, pl.ds(c * half, half)]
            copies.append(pltpu.make_async_remote_copy(src_ref=mine, dst_ref=mine, send_sem=send_sems.at[w],
                                                       recv_sem=recv_sems.at[w], device_id=(x, y, 1 - c),
                                                       device_id_type=MESH))
        for cp in copies:
            cp.start()
        for cp in copies:
            cp.wait()

    hbm = pl.BlockSpec(memory_space=pl.ANY)
    return pl.pallas_call(
        body, name="join_halves", in_specs=[hbm] * n, out_specs=[hbm] * n,
        out_shape=[jax.ShapeDtypeStruct(b.shape, b.dtype) for b in bufs],
        input_output_aliases={w: w for w in range(n)},
        scratch_shapes=[pltpu.SemaphoreType.DMA((n,)), pltpu.SemaphoreType.DMA((n,))],
    )(*bufs)


def _allreduce_small(v, name):
    rows = v.shape[0]

    def body(v_ref, o_ref, gath, send_sems, recv_sems):
        x, y, c = _place()
        idx = 4 * x + 2 * y + c
        gath[0] = v_ref[...]
        copies = []
        for r in range(1, N_DEV):
            peer = (1 - x if r & 4 else x, 1 - y if r & 2 else y, 1 - c if r & 1 else c)
            cp = pltpu.make_async_remote_copy(src_ref=v_ref, dst_ref=gath.at[r], send_sem=send_sems.at[r - 1],
                                              recv_sem=recv_sems.at[r - 1], device_id=peer, device_id_type=MESH)
            cp.start()
            copies.append(cp)
        for cp in copies:
            cp.wait()
        acc = gath[idx]
        for a in range(1, N_DEV):
            acc = acc + gath[lax.bitwise_xor(idx, a)]
        o_ref[...] = acc

    vmem = pl.BlockSpec(memory_space=pltpu.VMEM)
    return pl.pallas_call(
        body, name=name, in_specs=[vmem], out_specs=vmem,
        out_shape=jax.ShapeDtypeStruct((rows, LANES), F32),
        scratch_shapes=[pltpu.VMEM((N_DEV, rows, LANES), F32), pltpu.SemaphoreType.DMA((N_DEV - 1,)),
                        pltpu.SemaphoreType.DMA((N_DEV - 1,))],
    )(v)


def _join_shards(name, gathered):
    if name in COL_SHARDED:
        return jnp.concatenate([gathered[s] for s in range(N_CHIPS)], axis=1)[None]
    return gathered.reshape(1, N_CHIPS * gathered.shape[1], gathered.shape[2])


def _row_shards(g):
    return g.reshape(N_CHIPS, g.shape[0] // N_CHIPS, g.shape[1])


def _mixer_runs(width, cut, n_small):
    runs = []
    for s in range(N_CHIPS):
        lo, hi = s * width, (s + 1) * width
        spans = ((True, lo, min(hi, cut)), (False, max(lo, cut), min(hi, cut + n_small)),
                 (True, max(lo, cut + n_small), hi))
        runs.append([(is_main, a - lo, b - lo) for is_main, a, b in spans if a < b])
    return runs


def _split_mixer_weight(gathered, runs):
    main = [gathered[s][:, a:b] for s, parts in enumerate(runs) for is_main, a, b in parts if is_main]
    small = [gathered[s][:, a:b] for s, parts in enumerate(runs) for is_main, a, b in parts if not is_main]
    small = small[0] if len(small) == 1 else jnp.concatenate(small, axis=1)
    return jnp.concatenate(main, axis=1)[None], jnp.pad(small, ((0, 0), (0, LANES - small.shape[1])))[None]


def _join_mixer_grad(d_main, d_small, runs):
    shards, m, k = [], 0, 0
    for parts in runs:
        cols = []
        for is_main, a, b in parts:
            if is_main:
                cols.append(d_main[:, m:m + b - a])
                m += b - a
            else:
                cols.append(d_small[:, k:k + b - a].astype(d_main.dtype))
                k += b - a
        shards.append(cols[0] if len(cols) == 1 else jnp.concatenate(cols, axis=1))
    return jnp.stack(shards)


def _pad_small(flat):
    n = flat.shape[0]
    block = 8 * LANES
    padded = -(-n // block) * block
    return jnp.pad(flat, (0, padded - n)).reshape(padded // LANES, LANES)


def kernel(x, ffn1_norm, ffn1_w_in, ffn1_w_out, mix_norm, w_in, dn_conv_w, dn_a_log, dn_dt_bias, dn_out_norm, sb_q_norm, sb_k_norm, w_branch_a, w_branch_b, w_out, ffn2_norm, ffn2_w_in, ffn2_w_out, loss_target, m_ffn1_norm, m_ffn1_w_in, m_ffn1_w_out, m_mix_norm, m_w_in, m_dn_conv_w, m_dn_a_log, m_dn_dt_bias, m_dn_out_norm, m_sb_q_norm, m_sb_k_norm, m_w_branch_a, m_w_branch_b, m_w_out, m_ffn2_norm, m_ffn2_w_in, m_ffn2_w_out, v_ffn1_norm, v_ffn1_w_in, v_ffn1_w_out, v_mix_norm, v_w_in, v_dn_conv_w, v_dn_a_log, v_dn_dt_bias, v_dn_out_norm, v_sb_q_norm, v_sb_k_norm, v_w_branch_a, v_w_branch_b, v_w_out, v_ffn2_norm, v_ffn2_w_in, v_ffn2_w_out):
    w = dict(ffn1_norm=ffn1_norm, ffn1_w_in=ffn1_w_in, ffn1_w_out=ffn1_w_out, mix_norm=mix_norm, w_in=w_in,
             dn_conv_w=dn_conv_w, dn_a_log=dn_a_log, dn_dt_bias=dn_dt_bias, dn_out_norm=dn_out_norm,
             sb_q_norm=sb_q_norm, sb_k_norm=sb_k_norm, w_branch_a=w_branch_a, w_branch_b=w_branch_b, w_out=w_out,
             ffn2_norm=ffn2_norm, ffn2_w_in=ffn2_w_in, ffn2_w_out=ffn2_w_out)
    mom = dict(ffn1_norm=m_ffn1_norm, ffn1_w_in=m_ffn1_w_in, ffn1_w_out=m_ffn1_w_out, mix_norm=m_mix_norm, w_in=m_w_in,
               dn_conv_w=m_dn_conv_w, dn_a_log=m_dn_a_log, dn_dt_bias=m_dn_dt_bias, dn_out_norm=m_dn_out_norm,
               sb_q_norm=m_sb_q_norm, sb_k_norm=m_sb_k_norm, w_branch_a=m_w_branch_a, w_branch_b=m_w_branch_b,
               w_out=m_w_out, ffn2_norm=m_ffn2_norm, ffn2_w_in=m_ffn2_w_in, ffn2_w_out=m_ffn2_w_out)
    var = dict(ffn1_norm=v_ffn1_norm, ffn1_w_in=v_ffn1_w_in, ffn1_w_out=v_ffn1_w_out, mix_norm=v_mix_norm, w_in=v_w_in,
               dn_conv_w=v_dn_conv_w, dn_a_log=v_dn_a_log, dn_dt_bias=v_dn_dt_bias, dn_out_norm=v_dn_out_norm,
               sb_q_norm=v_sb_q_norm, sb_k_norm=v_sb_k_norm, w_branch_a=v_w_branch_a, w_branch_b=v_w_branch_b,
               w_out=v_w_out, ffn2_norm=v_ffn2_norm, ffn2_w_in=v_ffn2_w_in, ffn2_w_out=v_ffn2_w_out)

    _, t, d = x.shape
    depth = ffn1_norm.shape[0]
    n_heads = d // HEAD_DIM
    conv_cols = dn_conv_w.shape[2]
    assert d % HEAD_DIM == 0 and t % SB_BLOCK == 0 and 2 * n_heads <= LANES and depth % 2 == 0
    assert w_in.shape[2] * N_CHIPS == 9 * d + 2 * n_heads and conv_cols * N_CHIPS == 3 * d

    x_idx, y_idx, c_idx = _place()
    shard = 2 * x_idx + y_idx
    c_arr = jnp.reshape(c_idx, (1,)).astype(jnp.int32)
    s_arr = jnp.reshape(shard, (1,)).astype(jnp.int32)

    mine = {n: w[n].astype(BF16) for n in BIG}
    runs = _mixer_runs(w_in.shape[2], 4 * d, 2 * n_heads)
    groups = (("ffn1_w_in", "ffn1_w_out"), ("w_in",), ("w_branch_a", "w_branch_b", "w_out"),
              ("ffn2_w_in", "ffn2_w_out"))
    first = groups[0]
    rest = tuple(n for g in groups[1:] for n in g)
    n_gathers = depth + len(groups) - 1

    def gather(names, l, collective_id):
        return dict(zip(names, _allgather_layer([mine[n] for n in names], l, collective_id)))

    arriving = [{}]
    for k, names in enumerate(groups):
        arriving[0].update(gather(names, 0, 0 if k == 0 else depth + k - 1))
    arriving += [gather(BIG, l, l) for l in range(1, depth)]

    def layer_weights(l, names, after):
        gathered, after = lax.optimization_barrier(({n: arriving[l][n] for n in names}, after))
        full = {}
        for n in names:
            g = lax.dynamic_update_slice(gathered[n], mine[n][l][None], (shard, 0, 0))
            if n == "w_in":
                full["w_main"], full["w_ba"] = _split_mixer_weight(g, runs)
            else:
                full[n] = _join_shards(n, g)
        return full, after

    conv_place = lax.dynamic_update_slice(jnp.zeros((depth, DN_CONV, 3 * d), F32), dn_conv_w, (0, 0, shard * conv_cols))
    conv_rows = _pad_small(conv_place.reshape(-1))
    conv_full = (0.5 * _allreduce_small(conv_rows, "allgather_conv")).reshape(-1)[:depth * DN_CONV * 3 * d]
    conv_full = jnp.pad(conv_full.reshape(depth, DN_CONV, 3 * d), ((0, 0), (0, CONV_ROWS - DN_CONV), (0, 0)))

    def head_row(vals):
        return jnp.pad(vals, (n_heads, LANES - 2 * n_heads)).reshape(1, LANES)

    saved, layers = [], []
    cur = x[0]
    for l in range(depth):
        full, x0 = layer_weights(l, groups[0], cur)
        layers.append(full)
        x1, h1 = _ffn_fwd(x0, ffn1_norm[l][None], full["ffn1_w_in"], full["ffn1_w_out"], 0)
        later, x1 = layer_weights(l, groups[1], x1)
        full.update(later)
        proj, ba, hm = _proj_fwd(x1, mix_norm[l][None], full["w_main"], full["w_ba"], 0)
        act = _dn_prep_fwd(proj, conv_full[l], n_heads)
        alog, dtb = head_row(dn_a_log[l]), head_row(dn_dt_bias[l])
        oa, snaps = _delta_fwd(act, ba, alog, dtb, n_heads)
        qb, kb, vb = _sb_prep_fwd(proj, sb_q_norm[l][None], sb_k_norm[l][None], n_heads)
        ob, ltot = _sb_attn_fwd(qb, kb, vb, n_heads)
        later, (oa, ob) = layer_weights(l, groups[2], (oa, ob))
        full.update(later)
        x2 = _merge_fwd(x1, oa, proj, ob, dn_out_norm[l][None], full["w_branch_a"], full["w_branch_b"],
                        full["w_out"], 0, n_heads)
        later, x2 = layer_weights(l, groups[3], x2)
        full.update(later)
        cur, h2 = _ffn_fwd(x2, ffn2_norm[l][None], full["ffn2_w_in"], full["ffn2_w_out"], 0)
        saved.append((x0, x1, proj, ba, act, alog, dtb, oa, snaps, qb, kb, vb, ob, ltot, x2, h1, hm, h2))

    dcur, loss_part = _loss_head(cur, loss_target[0])

    grads = {n: [None] * depth for n in WEIGHTS}
    reduced = {n: None for n in BIG}
    in_flight = []

    def start_reduce(l, names, collective_id):
        g_major = [grads[n][l] for n in names]
        parts = [_add_half(g, got, c_arr) for g, got in zip(g_major, _swap_halves(g_major))]
        return l, names, parts, _scatter_partials(parts, l, collective_id)

    def finish_reduce(started, after):
        done_names, halves = [], []
        for l, names, parts, arrived in started:
            arrived, after = lax.optimization_barrier((arrived, after))
            halves += [_sum_partials(p, got, s_arr, c_arr, l, depth, stacked=reduced[n])
                       for n, p, got in zip(names, parts, arrived)]
            done_names += names
        if started:
            reduced.update(zip(done_names, _join_halves(halves, started[0][0])))
        return after

    for l in reversed(range(depth)):
        x0, x1, proj, ba, act, alog, dtb, oa, snaps, qb, kb, vb, ob, ltot, x2, h1, hm, h2 = saved[l]
        full = layers[l]
        dx2, dg, dwi, dwo = _ffn_bwd(x2, h2, ffn2_norm[l][None], dcur, full["ffn2_w_in"], full["ffn2_w_out"], 0)
        grads["ffn2_norm"][l] = dg[0]
        grads["ffn2_w_in"][l] = dwi
        grads["ffn2_w_out"][l] = _row_shards(dwo)
        doa, dz, dob, dga, dgb, dgn, dwa, dwb, dwout = _merge_bwd(
            oa, proj, ob, dx2, dn_out_norm[l][None], full["w_branch_a"], full["w_branch_b"], full["w_out"], 0,
            n_heads)
        grads["dn_out_norm"][l] = dgn[0]
        grads["w_branch_a"][l], grads["w_branch_b"][l] = _row_shards(dwa), _row_shards(dwb)
        grads["w_out"][l] = _row_shards(dwout)
        dqb, dkb, dvb = _sb_attn_bwd(qb, kb, vb, ltot, dob, n_heads)
        dsq, dsk, dqn, dkn = _sb_prep_bwd(proj, sb_q_norm[l][None], sb_k_norm[l][None], dqb, dkb, n_heads)
        grads["sb_q_norm"][l], grads["sb_k_norm"][l] = dqn[0], dkn[0]
        dact, dba, dal, ddt = _delta_bwd(act, ba, alog, dtb, snaps, doa, n_heads)
        grads["dn_a_log"][l] = dal[0, n_heads:2 * n_heads]
        grads["dn_dt_bias"][l] = ddt[0, n_heads:2 * n_heads]
        dqkv, dconv = _dn_prep_bwd(proj, conv_full[l], dact, n_heads)
        grads["dn_conv_w"][l] = dconv[:DN_CONV]
        dproj = jnp.concatenate([dqkv, dz, dsq, dsk, dvb.astype(BF16), dga, dgb], axis=1)
        dx1, dg, dwm, dwba = _proj_bwd(x1, hm, mix_norm[l][None], dx2, dproj, dba, full["w_main"], full["w_ba"], 0)
        grads["mix_norm"][l] = dg[0]
        grads["w_in"][l] = _join_mixer_grad(dwm, dwba, runs)
        early = start_reduce(l, rest, collective_id=n_gathers + l)
        dcur, dg, dwi, dwo = _ffn_bwd(x0, h1, ffn1_norm[l][None], dx1, full["ffn1_w_in"], full["ffn1_w_out"], 0)
        grads["ffn1_norm"][l] = dg[0]
        grads["ffn1_w_in"][l] = dwi
        grads["ffn1_w_out"][l] = _row_shards(dwo)

        dcur = finish_reduce(in_flight, dcur)
        in_flight = [early, start_reduce(l, first, collective_id=n_gathers + depth + l)]
    dcur = finish_reduce(in_flight, dcur)
    final = reduced
    grads = {n: jnp.stack(grads[n]) for n in SMALL + ("dn_conv_w",)}

    small_names = SMALL + ("dn_conv_w",)
    small_sizes = [int(np.prod(grads[n].shape)) for n in small_names]
    small_off = np.concatenate([[0], np.cumsum(small_sizes)])
    small = jnp.concatenate([grads[n].reshape(-1) for n in small_names] + [loss_part[0, :1]])
    small_sum = _allreduce_small(_pad_small(small), "allreduce_small").reshape(-1)
    for i, n in enumerate(small_names):
        final[n] = small_sum[small_off[i]:small_off[i + 1]].reshape(grads[n].shape)
    final["dn_conv_w"] = lax.dynamic_slice(final["dn_conv_w"], (0, 0, shard * conv_cols), (depth, DN_CONV, conv_cols))
    loss = small_sum[small_off[-1]]

    deltas, new_m, new_v = {}, {}, {}
    for n in WEIGHTS:
        shape = w[n].shape
        flat = (-1, shape[-1])
        dl, m2, v2 = _adamw(w[n].reshape(flat), final[n].reshape(flat), mom[n].reshape(flat), var[n].reshape(flat))
        deltas[n], new_m[n], new_v[n] = dl.reshape(shape), m2.reshape(shape), v2.reshape(shape)

    grad_x = dcur[None]
    return (loss, grad_x, *[final[n] for n in WEIGHTS], *[deltas[n] for n in WEIGHTS],
            *[new_m[n] for n in WEIGHTS], *[new_v[n] for n in WEIGHTS])
```

```python
import functools

import jax
import jax.numpy as jnp
import numpy as np
from jax import lax
from jax.experimental import pallas as pl
from jax.experimental.pallas import tpu as pltpu
from jax.experimental.pallas import tpu_sc as plsc

F32 = jnp.float32
BF16 = jnp.bfloat16

LANES = 128
HEAD_DIM = 128
DN_CHUNK = 64
DN_CONV = 4
CONV_ROWS = 8
SB_BLOCK = 128
FFN_HALF = 0.5
RMS_EPS = 1e-6
L2_EPS = 1e-6
NEG_BIG = -1e30
ADAM_LR = 0.001
ADAM_B1 = 0.9
ADAM_B2 = 0.999
ADAM_EPS = 1e-08
ADAM_WD = 0.01
ADAM_STEP = 10
VMEM_LIMIT = 56 * 1024 * 1024
N_CHIPS = 4
N_DEV = 8
MESH = pl.DeviceIdType.MESH

BIG = ("ffn1_w_in", "ffn1_w_out", "w_in", "w_branch_a", "w_branch_b", "w_out", "ffn2_w_in", "ffn2_w_out")
COL_SHARDED = ("ffn1_w_in", "w_in", "ffn2_w_in")
SMALL = ("ffn1_norm", "mix_norm", "dn_a_log", "dn_dt_bias", "dn_out_norm", "sb_q_norm", "sb_k_norm", "ffn2_norm")
WEIGHTS = ("ffn1_norm", "ffn1_w_in", "ffn1_w_out", "mix_norm", "w_in", "dn_conv_w", "dn_a_log", "dn_dt_bias",
           "dn_out_norm", "sb_q_norm", "sb_k_norm", "w_branch_a", "w_branch_b", "w_out", "ffn2_norm", "ffn2_w_in",
           "ffn2_w_out")


def _params(**kw):
    return pltpu.CompilerParams(vmem_limit_bytes=VMEM_LIMIT, **kw)


def _pick(n, options):
    for o in options:
        if n % o == 0:
            return o
    return n


def _const_spec(shape, single=False):
    nd = len(shape)
    if single:
        return pl.BlockSpec(shape, lambda *_: (0,) * nd, pipeline_mode=pl.Buffered(1))
    return pl.BlockSpec(shape, lambda *_: (0,) * nd)


_NN = ((1,), (0,))
_NT = ((1,), (1,))
_TN = ((0,), (0,))


def _dot(a, b, dims):
    return lax.dot_general(a.astype(BF16), b.astype(BF16), (dims, ((), ())), preferred_element_type=F32)


def _mm_nn(a, b):
    return _dot(a, b, _NN)


def _mm_nt(a, b):
    return _dot(a, b, _NT)


def _mm_tn(a, b):
    return _dot(a, b, _TN)


def _split(a):
    hi = a.astype(BF16)
    lo = (a - hi.astype(F32)).astype(BF16)
    return hi, lo


def _dot_precise(a, b, dims):
    dn = (dims, ((), ()))
    ah, al = _split(a)
    bh, bl = _split(b)
    out = lax.dot_general(ah, bh, dn, preferred_element_type=F32)
    out = out + lax.dot_general(ah, bl, dn, preferred_element_type=F32)
    return out + lax.dot_general(al, bh, dn, preferred_element_type=F32)


def _make_diff_mm(dot):
    @jax.custom_vjp
    def nn(a, b):
        return dot(a, b, _NN)

    @jax.custom_vjp
    def nt(a, b):
        return dot(a, b, _NT)

    @jax.custom_vjp
    def tn(a, b):
        return dot(a, b, _TN)

    nn.defvjp(lambda a, b: (dot(a, b, _NN), (a, b)), lambda r, g: (nt(g, r[1]), tn(r[0], g)))
    nt.defvjp(lambda a, b: (dot(a, b, _NT), (a, b)), lambda r, g: (nn(g, r[1]), tn(g, r[0])))
    tn.defvjp(lambda a, b: (dot(a, b, _TN), (a, b)), lambda r, g: (nt(r[1], g), nn(r[0], g)))
    return nn, nt, tn


_d_nn, _d_nt, _d_tn = _make_diff_mm(_dot)
_p_nn, _p_nt, _p_tn = _make_diff_mm(_dot_precise)


@jax.custom_vjp
def _halves(x):
    n = x.shape[0] // 2
    return x[:n], x[n:]


_halves.defvjp(lambda x: (_halves(x), None), lambda _, g: (jnp.concatenate(g, axis=0),))


@jax.custom_vjp
def _col_halves(x):
    n = x.shape[1] // 2
    return x[:, :n], x[:, n:]


_col_halves.defvjp(lambda x: (_col_halves(x), None), lambda _, g: (jnp.concatenate(g, axis=1),))


def _softplus_raw(x):
    return jnp.maximum(x, 0.0) + jnp.log(1.0 + jnp.exp(-jnp.abs(x)))


@jax.custom_vjp
def _softplus(x):
    return _softplus_raw(x)


_softplus.defvjp(lambda x: (_softplus_raw(x), x), lambda x, g: (g * jax.nn.sigmoid(x),))


def _rms(x, gain, eps):
    return x * lax.rsqrt(jnp.mean(x * x, axis=-1, keepdims=True) + eps) * gain


def _silu(x):
    return x * jax.nn.sigmoid(x)


def _shift_rows_raw(x, k, down):
    n = x.shape[0]
    row = lax.broadcasted_iota(jnp.int32, x.shape, 0)
    if down:
        return jnp.where(row >= k, pltpu.roll(x, k, 0), 0.0)
    return jnp.where(row < n - k, pltpu.roll(x, n - k, 0), 0.0)


@functools.partial(jax.custom_vjp, nondiff_argnums=(1,))
def _shift_down(x, k):
    return _shift_rows_raw(x, k, True)


_shift_down.defvjp(lambda x, k: (_shift_rows_raw(x, k, True), None),
                   lambda k, _, g: (_shift_rows_raw(g, k, False),))


def _layer_spec(layer, block, index_map, single=False):
    full_map = lambda *a: (layer,) + tuple(index_map(*a))
    if single:
        return pl.BlockSpec((None,) + block, full_map, pipeline_mode=pl.Buffered(1))
    return pl.BlockSpec((None,) + block, full_map)


def _ffn_fwd(x, gain, w_in, w_out, layer):
    t, d = x.shape
    f = w_out.shape[1]
    fc = _pick(f, (256, 128))
    nj = f // fc
    rt = _pick(t, (512, 256, 128))

    def body(x_ref, g_ref, wg_ref, wu_ref, wo_ref, o_ref, hs_ref):
        @pl.when(pl.program_id(0) == 0)
        def _():
            for r in range(t // rt):
                rows = pl.ds(r * rt, rt)
                xr = x_ref[rows, :]
                hs_ref[rows, :] = _rms(xr, g_ref[...], RMS_EPS).astype(BF16)
                o_ref[rows, :] = xr

        for r in range(t // rt):
            rows = pl.ds(r * rt, rt)
            h = hs_ref[rows, :]
            a = _mm_nn(h, wg_ref[...])
            b = _mm_nn(h, wu_ref[...])
            o_ref[rows, :] += FFN_HALF * _mm_nn(_silu(a) * b, wo_ref[...])

    return pl.pallas_call(
        body, name="ffn_fwd", grid=(nj,),
        in_specs=[_const_spec((t, d), True), _const_spec((1, d)),
                  _layer_spec(layer, (d, fc), lambda j: (0, j)), _layer_spec(layer, (d, fc), lambda j: (0, nj + j)),
                  _layer_spec(layer, (fc, d), lambda j: (j, 0))],
        out_specs=[_const_spec((t, d)), _const_spec((t, d))],
        out_shape=[jax.ShapeDtypeStruct((t, d), F32), jax.ShapeDtypeStruct((t, d), BF16)],
        compiler_params=_params(dimension_semantics=("arbitrary",)),
    )(x, gain, w_in, w_in, w_out)


def _ffn_bwd(x, hs, gain, dy, w_in, w_out, layer):
    t, d = x.shape
    f = w_out.shape[1]
    fc = _pick(f, (256, 128))
    nj = f // fc
    rt = _pick(t, (512, 256, 128))
    nr = t // rt

    def body(x_ref, hs_ref, g_ref, dy_ref, wg_ref, wu_ref, wo_ref, dx_ref, dg_ref, dwi_ref, dwo_ref,
             dwg_acc, dwu_acc, dwo_acc):
        j = pl.program_id(0)

        @pl.when(j == 0)
        def _():
            for r in range(nr):
                dx_ref[pl.ds(r * rt, rt), :] = jnp.zeros((rt, d), F32)

        for r in range(nr):
            rows = pl.ds(r * rt, rt)
            h = hs_ref[rows, :]
            dy2 = (FFN_HALF * dy_ref[rows, :]).astype(BF16)
            a = _mm_nn(h, wg_ref[...])
            b = _mm_nn(h, wu_ref[...])
            sig = jax.nn.sigmoid(a)
            sa = a * sig
            ds = _mm_nt(dy2, wo_ref[...])
            da = ds * b * (sig * (1.0 + a * (1.0 - sig)))
            db = ds * sa
            dx_ref[rows, :] += _mm_nt(da, wg_ref[...]) + _mm_nt(db, wu_ref[...])
            dwo_c = _mm_tn(sa * b, dy2)
            dwg_c = _mm_tn(h, da)
            dwu_c = _mm_tn(h, db)
            if r == 0:
                dwo_acc[...] = dwo_c
                dwg_acc[...] = dwg_c
                dwu_acc[...] = dwu_c
            else:
                dwo_acc[...] += dwo_c
                dwg_acc[...] += dwg_c
                dwu_acc[...] += dwu_c
        dwi_ref[0] = dwg_acc[...].astype(BF16)
        dwi_ref[1] = dwu_acc[...].astype(BF16)
        dwo_ref[...] = dwo_acc[...].astype(BF16)

        @pl.when(j == nj - 1)
        def _():
            for r in range(nr):
                rows = pl.ds(r * rt, rt)
                _, vjp = jax.vjp(lambda xx, gg: _rms(xx, gg, RMS_EPS), x_ref[rows, :], g_ref[...])
                dxn, dgr = vjp(dx_ref[rows, :])
                dx_ref[rows, :] = dy_ref[rows, :] + dxn
                if r == 0:
                    dg_ref[...] = dgr
                else:
                    dg_ref[...] += dgr

    return pl.pallas_call(
        body, name="ffn_bwd", grid=(nj,),
        in_specs=[_const_spec((t, d), True), _const_spec((t, d), True), _const_spec((1, d)), _const_spec((t, d), True),
                  _layer_spec(layer, (d, fc), lambda j: (0, j)), _layer_spec(layer, (d, fc), lambda j: (0, nj + j)),
                  _layer_spec(layer, (fc, d), lambda j: (j, 0))],
        out_specs=[_const_spec((t, d)), _const_spec((1, d)),
                   pl.BlockSpec((2, d, fc), lambda j: (0, 0, j)), pl.BlockSpec((fc, d), lambda j: (j, 0))],
        out_shape=[jax.ShapeDtypeStruct((t, d), F32), jax.ShapeDtypeStruct((1, d), F32),
                   jax.ShapeDtypeStruct((2, d, f), BF16), jax.ShapeDtypeStruct((f, d), BF16)],
        scratch_shapes=[pltpu.VMEM((d, fc), F32), pltpu.VMEM((d, fc), F32), pltpu.VMEM((fc, d), F32)],
        compiler_params=_params(dimension_semantics=("arbitrary",)),
    )(x, hs, gain, dy, w_in, w_in, w_out)


def _proj_fwd(x, gain, w, wba, layer):
    t, d = x.shape
    n = w.shape[2]
    nc = _pick(n, (512, 256, 128))
    rt = _pick(t, (512, 256, 128))

    def body(x_ref, g_ref, w_ref, wba_ref, p_ref, ba_ref, hs_ref):
        @pl.when(pl.program_id(0) == 0)
        def _():
            for r in range(t // rt):
                rows = pl.ds(r * rt, rt)
                h = _rms(x_ref[rows, :], g_ref[...], RMS_EPS).astype(BF16)
                hs_ref[rows, :] = h
                ba_ref[rows, :] = _mm_nn(h, wba_ref[...])

        for r in range(t // rt):
            rows = pl.ds(r * rt, rt)
            p_ref[rows, :] = _mm_nn(hs_ref[rows, :], w_ref[...])

    return pl.pallas_call(
        body, name="proj_fwd", grid=(n // nc,),
        in_specs=[_const_spec((t, d), True), _const_spec((1, d)),
                  _layer_spec(layer, (d, nc), lambda j: (0, j)), _layer_spec(layer, (d, LANES), lambda j: (0, 0))],
        out_specs=[pl.BlockSpec((t, nc), lambda j: (0, j)), _const_spec((t, LANES)), _const_spec((t, d))],
        out_shape=[jax.ShapeDtypeStruct((t, n), F32), jax.ShapeDtypeStruct((t, LANES), F32),
                   jax.ShapeDtypeStruct((t, d), BF16)],
        compiler_params=_params(dimension_semantics=("arbitrary",)),
    )(x, gain, w, wba)


def _proj_bwd(x, hs, gain, dres, dp, dba, w, wba, layer):
    t, d = x.shape
    n = w.shape[2]
    nc = _pick(n, (512, 256, 128))
    nj = n // nc
    rt = _pick(t, (512, 256, 128))
    nr = t // rt

    def body(x_ref, hs_ref, g_ref, dres_ref, dp_ref, dba_ref, w_ref, wba_ref, dx_ref, dg_ref, dw_ref, dwba_ref, dw_acc):
        j = pl.program_id(0)

        @pl.when(j == 0)
        def _():
            for r in range(nr):
                rows = pl.ds(r * rt, rt)
                h = hs_ref[rows, :]
                g = dba_ref[rows, :]
                dx_ref[rows, :] = _mm_nt(g, wba_ref[...])
                if r == 0:
                    dwba_ref[...] = _mm_tn(h, g)
                else:
                    dwba_ref[...] += _mm_tn(h, g)

        for r in range(nr):
            rows = pl.ds(r * rt, rt)
            g = dp_ref[rows, :]
            dx_ref[rows, :] += _mm_nt(g, w_ref[...])
            if r == 0:
                dw_acc[...] = _mm_tn(hs_ref[rows, :], g)
            else:
                dw_acc[...] += _mm_tn(hs_ref[rows, :], g)
        dw_ref[...] = dw_acc[...].astype(BF16)

        @pl.when(j == nj - 1)
        def _():
            for r in range(nr):
                rows = pl.ds(r * rt, rt)
                _, vjp = jax.vjp(lambda xx, gg: _rms(xx, gg, RMS_EPS), x_ref[rows, :], g_ref[...])
                dxn, dgr = vjp(dx_ref[rows, :])
                dx_ref[rows, :] = dres_ref[rows, :] + dxn
                if r == 0:
                    dg_ref[...] = dgr
                else:
                    dg_ref[...] += dgr

    return pl.pallas_call(
        body, name="proj_bwd", grid=(nj,),
        in_specs=[_const_spec((t, d), True), _const_spec((t, d), True), _const_spec((1, d)), _const_spec((t, d), True),
                  pl.BlockSpec((t, nc), lambda j: (0, j)), _const_spec((t, LANES)),
                  _layer_spec(layer, (d, nc), lambda j: (0, j)), _layer_spec(layer, (d, LANES), lambda j: (0, 0))],
        out_specs=[_const_spec((t, d)), _const_spec((1, d)),
                   pl.BlockSpec((d, nc), lambda j: (0, j)), _const_spec((d, LANES))],
        out_shape=[jax.ShapeDtypeStruct((t, d), F32), jax.ShapeDtypeStruct((1, d), F32),
                   jax.ShapeDtypeStruct((d, n), BF16), jax.ShapeDtypeStruct((d, LANES), F32)],
        scratch_shapes=[pltpu.VMEM((d, nc), F32)],
        compiler_params=_params(dimension_semantics=("arbitrary",)),
    )(x, hs, gain, dres, dp, dba, w, wba)


def _conv_act(x, w0, w1, w2, w3, is_qk):
    y = w3 * x + w2 * _shift_down(x, 1) + w1 * _shift_down(x, 2) + w0 * _shift_down(x, 3)
    y = _silu(y)
    inv = lax.rsqrt(jnp.sum(y * y, axis=-1, keepdims=True) + L2_EPS)
    return y * (is_qk * inv + (1.0 - is_qk))


def _taps(w_ref):
    return tuple(w_ref[i:i + 1, :] for i in range(DN_CONV))


def _dn_prep_fwd(proj, conv_w, n_heads):
    t = proj.shape[0]
    nb = 3 * n_heads

    def body(x_ref, w_ref, o_ref):
        is_qk = jnp.where(pl.program_id(0) < 2 * n_heads, 1.0, 0.0).astype(F32)
        o_ref[...] = _conv_act(x_ref[...], *_taps(w_ref), is_qk)

    return pl.pallas_call(
        body, name="dn_prep_fwd", grid=(nb,),
        in_specs=[pl.BlockSpec((t, HEAD_DIM), lambda i: (0, i)), pl.BlockSpec((CONV_ROWS, HEAD_DIM), lambda i: (0, i))],
        out_specs=pl.BlockSpec((t, HEAD_DIM), lambda i: (0, i)),
        out_shape=jax.ShapeDtypeStruct((t, nb * HEAD_DIM), F32),
        compiler_params=_params(dimension_semantics=("arbitrary",)),
    )(proj, conv_w)


def _dn_prep_bwd(proj, conv_w, dact, n_heads):
    t = proj.shape[0]
    nb = 3 * n_heads

    def body(x_ref, w_ref, g_ref, dx_ref, dw_ref):
        is_qk = jnp.where(pl.program_id(0) < 2 * n_heads, 1.0, 0.0).astype(F32)
        _, vjp = jax.vjp(lambda x, a, b, c, e: _conv_act(x, a, b, c, e, is_qk), x_ref[...], *_taps(w_ref))
        dx, d0, d1, d2, d3 = vjp(g_ref[...])
        dx_ref[...] = dx.astype(BF16)
        dw_ref[...] = jnp.concatenate([d0, d1, d2, d3, jnp.zeros((CONV_ROWS - DN_CONV, HEAD_DIM), F32)], axis=0)

    return pl.pallas_call(
        body, name="dn_prep_bwd", grid=(nb,),
        in_specs=[pl.BlockSpec((t, HEAD_DIM), lambda i: (0, i)), pl.BlockSpec((CONV_ROWS, HEAD_DIM), lambda i: (0, i)),
                  pl.BlockSpec((t, HEAD_DIM), lambda i: (0, i))],
        out_specs=[pl.BlockSpec((t, HEAD_DIM), lambda i: (0, i)), pl.BlockSpec((CONV_ROWS, HEAD_DIM), lambda i: (0, i))],
        out_shape=[jax.ShapeDtypeStruct((t, nb * HEAD_DIM), BF16), jax.ShapeDtypeStruct((CONV_ROWS, nb * HEAD_DIM), F32)],
        compiler_params=_params(dimension_semantics=("arbitrary",)),
    )(proj, conv_w, dact)


def _unit_lower_inverses(lmats, c):
    r = lax.broadcasted_iota(jnp.int32, (c, c), 0)
    q = lax.broadcasted_iota(jnp.int32, (c, c), 1)
    eye = jnp.where(r == q, 1.0, 0.0)
    ps = [eye - l for l in lmats]
    ms = [_p_nn(l, l) for l in lmats]
    n = 2
    while 2 * n < c:
        both = [_halves(_p_nn(jnp.concatenate([p, m], axis=0), m)) for p, m in zip(ps, ms)]
        ps = [p + pm for p, (pm, _) in zip(ps, both)]
        ms = [mm for _, mm in both]
        n *= 2
    return [p + _p_nn(p, m) for p, m in zip(ps, ms)]


def _delta_heads(qs, ks, vs, bg, alog, dtb, states):
    n_heads = len(qs)
    heads = range(n_heads)
    c = qs[0].shape[0]
    lane = lax.broadcasted_iota(jnp.int32, (c, LANES), 1)
    r = lax.broadcasted_iota(jnp.int32, (c, c), 0)
    s = lax.broadcasted_iota(jnp.int32, (c, c), 1)
    beta_all = jax.nn.sigmoid(bg)
    g_all = -jnp.exp(alog) * _softplus(bg + dtb)
    beta = [jnp.sum(jnp.where(lane == h, beta_all, 0.0), axis=1, keepdims=True) for h in heads]
    g = [jnp.sum(jnp.where(lane == n_heads + h, g_all, 0.0), axis=1, keepdims=True) for h in heads]
    g_row = [jnp.sum(jnp.where(r == s, g[h], 0.0), axis=0, keepdims=True) for h in heads]
    gc = [jnp.sum(jnp.where(s <= r, g_row[h], 0.0), axis=1, keepdims=True) for h in heads]
    gr = [jnp.sum(jnp.where(r <= s, g[h], 0.0), axis=0, keepdims=True) for h in heads]
    g_last = [jnp.sum(g[h], axis=0, keepdims=True) for h in heads]
    decay = [jnp.exp(jnp.where(r >= s, gc[h] - gr[h], NEG_BIG)) for h in heads]
    q_scaled = [qs[h] * (HEAD_DIM ** -0.5) for h in heads]
    k_beta = [ks[h] * beta[h] for h in heads]
    kk_qk = [_halves(_d_nt(jnp.concatenate([k_beta[h], q_scaled[h]], axis=0), ks[h])) for h in heads]
    lmat = [jnp.where(r > s, kk_qk[h][0] * decay[h], 0.0) for h in heads]
    attn = [kk_qk[h][1] * decay[h] for h in heads]
    tinv = _unit_lower_inverses(lmat, c)
    u_w = [_col_halves(_p_nn(tinv[h], jnp.concatenate([vs[h] * beta[h], k_beta[h] * jnp.exp(gc[h])], axis=1)))
           for h in heads]
    u = [u_w[h][0] for h in heads]
    w = [u_w[h][1] for h in heads]
    ws_qs = [_halves(_d_nn(jnp.concatenate([w[h], q_scaled[h] * jnp.exp(gc[h])], axis=0), states[h])) for h in heads]
    v_new = [u[h] - ws_qs[h][0] for h in heads]
    o = [ws_qs[h][1] + _d_nn(attn[h], v_new[h]) for h in heads]
    kv = [_d_tn(ks[h] * jnp.exp(g_last[h] - gc[h]), v_new[h]) for h in heads]
    new_states = [states[h] * jnp.exp(g_last[h]) + kv[h] for h in heads]
    return tuple(o), tuple(new_states)


def _delta_fwd(act, ba, alog, dtb, n_heads):
    t = act.shape[0]
    d = n_heads * HEAD_DIM
    c = DN_CHUNK
    nc = t // c

    def body(q_ref, k_ref, v_ref, bg_ref, al_ref, dt_ref, o_ref, snap_ref, st_ref):
        @pl.when(pl.program_id(0) == 0)
        def _():
            st_ref[...] = jnp.zeros(st_ref.shape, F32)

        snap_ref[0] = st_ref[...]
        cols = [slice(h * HEAD_DIM, (h + 1) * HEAD_DIM) for h in range(n_heads)]
        os, new_states = _delta_heads([q_ref[:, sl] for sl in cols], [k_ref[:, sl] for sl in cols],
                                      [v_ref[:, sl] for sl in cols], bg_ref[...], al_ref[...], dt_ref[...],
                                      [st_ref[h] for h in range(n_heads)])
        for h, sl in enumerate(cols):
            o_ref[:, sl] = os[h]
            st_ref[h] = new_states[h]

    return pl.pallas_call(
        body, name="delta_fwd", grid=(nc,),
        in_specs=[pl.BlockSpec((c, d), lambda i: (i, 0)), pl.BlockSpec((c, d), lambda i: (i, 1)),
                  pl.BlockSpec((c, d), lambda i: (i, 2)), pl.BlockSpec((c, LANES), lambda i: (i, 0)),
                  _const_spec((1, LANES)), _const_spec((1, LANES))],
        out_specs=[pl.BlockSpec((c, d), lambda i: (i, 0)),
                   pl.BlockSpec((1, n_heads, HEAD_DIM, HEAD_DIM), lambda i: (i, 0, 0, 0))],
        out_shape=[jax.ShapeDtypeStruct((t, d), F32), jax.ShapeDtypeStruct((nc, n_heads, HEAD_DIM, HEAD_DIM), F32)],
        scratch_shapes=[pltpu.VMEM((n_heads, HEAD_DIM, HEAD_DIM), F32)],
        compiler_params=_params(dimension_semantics=("arbitrary",)),
    )(act, act, act, ba, alog, dtb)


def _delta_bwd(act, ba, alog, dtb, snaps, do, n_heads):
    t = act.shape[0]
    d = n_heads * HEAD_DIM
    c = DN_CHUNK
    nc = t // c

    def body(q_ref, k_ref, v_ref, bg_ref, al_ref, dt_ref, snap_ref, do_ref,
             dact_ref, dbg_ref, dal_ref, ddt_ref, ds_ref):
        @pl.when(pl.program_id(0) == 0)
        def _():
            ds_ref[...] = jnp.zeros(ds_ref.shape, F32)
            dal_ref[...] = jnp.zeros((1, LANES), F32)
            ddt_ref[...] = jnp.zeros((1, LANES), F32)

        heads = range(n_heads)
        cols = [slice(h * HEAD_DIM, (h + 1) * HEAD_DIM) for h in heads]
        _, vjp = jax.vjp(_delta_heads, tuple(q_ref[:, sl] for sl in cols), tuple(k_ref[:, sl] for sl in cols),
                         tuple(v_ref[:, sl] for sl in cols), bg_ref[...], al_ref[...], dt_ref[...],
                         tuple(snap_ref[0, h] for h in heads))
        dq, dk, dv, dbg, dal, ddt, dst = vjp((tuple(do_ref[:, sl] for sl in cols), tuple(ds_ref[h] for h in heads)))
        for h, sl in enumerate(cols):
            dact_ref[:, sl] = dq[h]
            dact_ref[:, d + h * HEAD_DIM:d + (h + 1) * HEAD_DIM] = dk[h]
            dact_ref[:, 2 * d + h * HEAD_DIM:2 * d + (h + 1) * HEAD_DIM] = dv[h]
            ds_ref[h] = dst[h]
        dal_ref[...] += dal
        ddt_ref[...] += ddt
        dbg_ref[...] = dbg.astype(BF16)

    rev = lambda i: nc - 1 - i
    return pl.pallas_call(
        body, name="delta_bwd", grid=(nc,),
        in_specs=[pl.BlockSpec((c, d), lambda i: (rev(i), 0)), pl.BlockSpec((c, d), lambda i: (rev(i), 1)),
                  pl.BlockSpec((c, d), lambda i: (rev(i), 2)), pl.BlockSpec((c, LANES), lambda i: (rev(i), 0)),
                  _const_spec((1, LANES)), _const_spec((1, LANES)),
                  pl.BlockSpec((1, n_heads, HEAD_DIM, HEAD_DIM), lambda i: (rev(i), 0, 0, 0)),
                  pl.BlockSpec((c, d), lambda i: (rev(i), 0))],
        out_specs=[pl.BlockSpec((c, 3 * d), lambda i: (rev(i), 0)), pl.BlockSpec((c, LANES), lambda i: (rev(i), 0)),
                   _const_spec((1, LANES)), _const_spec((1, LANES))],
        out_shape=[jax.ShapeDtypeStruct((t, 3 * d), F32), jax.ShapeDtypeStruct((t, LANES), BF16),
                   jax.ShapeDtypeStruct((1, LANES), F32), jax.ShapeDtypeStruct((1, LANES), F32)],
        scratch_shapes=[pltpu.VMEM((n_heads, HEAD_DIM, HEAD_DIM), F32)],
        compiler_params=_params(dimension_semantics=("arbitrary",)),
    )(act, act, act, ba, alog, dtb, snaps, do)


def _head_norm2(a, b, ga, gb):
    return _rms(a, ga, RMS_EPS), _rms(b, gb, RMS_EPS)


def _sb_prep_fwd(proj, qn, kn, n_heads):
    t = proj.shape[0]
    d = n_heads * HEAD_DIM
    tm = _pick(t, (256, 128))

    def body(q_ref, k_ref, v_ref, qn_ref, kn_ref, qo_ref, ko_ref, vo_ref):
        for h in range(n_heads):
            sl = slice(h * HEAD_DIM, (h + 1) * HEAD_DIM)
            qh, kh = _head_norm2(q_ref[:, sl], k_ref[:, sl], qn_ref[...], kn_ref[...])
            qo_ref[:, sl] = qh.astype(BF16)
            ko_ref[:, sl] = kh.astype(BF16)
        vo_ref[...] = v_ref[...].astype(BF16)

    tile = lambda k: pl.BlockSpec((tm, d), lambda i: (i, k))
    return pl.pallas_call(
        body, name="sb_prep_fwd", grid=(t // tm,),
        in_specs=[tile(4), tile(5), tile(6), _const_spec((1, HEAD_DIM)), _const_spec((1, HEAD_DIM))],
        out_specs=[tile(0), tile(0), tile(0)],
        out_shape=[jax.ShapeDtypeStruct((t, d), BF16)] * 3,
        compiler_params=_params(dimension_semantics=("arbitrary",)),
    )(proj, proj, proj, qn, kn)


def _sb_prep_bwd(proj, qn, kn, dq, dk, n_heads):
    t = proj.shape[0]
    d = n_heads * HEAD_DIM
    tm = _pick(t, (256, 128))

    def body(q_ref, k_ref, qn_ref, kn_ref, dq_ref, dk_ref, dqo_ref, dko_ref, dqn_ref, dkn_ref):
        @pl.when(pl.program_id(0) == 0)
        def _():
            dqn_ref[...] = jnp.zeros((1, HEAD_DIM), F32)
            dkn_ref[...] = jnp.zeros((1, HEAD_DIM), F32)

        for h in range(n_heads):
            sl = slice(h * HEAD_DIM, (h + 1) * HEAD_DIM)
            _, vjp = jax.vjp(_head_norm2, q_ref[:, sl], k_ref[:, sl], qn_ref[...], kn_ref[...])
            da, db, dga, dgb = vjp((dq_ref[:, sl], dk_ref[:, sl]))
            dqo_ref[:, sl] = da.astype(BF16)
            dko_ref[:, sl] = db.astype(BF16)
            dqn_ref[...] += dga
            dkn_ref[...] += dgb

    return pl.pallas_call(
        body, name="sb_prep_bwd", grid=(t // tm,),
        in_specs=[pl.BlockSpec((tm, d), lambda i: (i, 4)), pl.BlockSpec((tm, d), lambda i: (i, 5)),
                  _const_spec((1, HEAD_DIM)), _const_spec((1, HEAD_DIM)),
                  pl.BlockSpec((tm, d), lambda i: (i, 0)), pl.BlockSpec((tm, d), lambda i: (i, 0))],
        out_specs=[pl.BlockSpec((tm, d), lambda i: (i, 0)), pl.BlockSpec((tm, d), lambda i: (i, 0)),
                   _const_spec((1, HEAD_DIM)), _const_spec((1, HEAD_DIM))],
        out_shape=[jax.ShapeDtypeStruct((t, d), BF16), jax.ShapeDtypeStruct((t, d), BF16),
                   jax.ShapeDtypeStruct((1, HEAD_DIM), F32), jax.ShapeDtypeStruct((1, HEAD_DIM), F32)],
        compiler_params=_params(dimension_semantics=("arbitrary",)),
    )(proj, proj, qn, kn, dq, dk)


def _cumsum_mm(x, tri):
    hi, lo = _split(x)
    return (lax.dot_general(hi, tri, (_NN, ((), ())), preferred_element_type=F32)
            + lax.dot_general(lo, tri, (_NN, ((), ())), preferred_element_type=F32))


def _sb_valid(i, j):
    row = lax.broadcasted_iota(jnp.int32, (SB_BLOCK, SB_BLOCK), 0)
    col = lax.broadcasted_iota(jnp.int32, (SB_BLOCK, SB_BLOCK), 1)
    return (col + j * SB_BLOCK) < (row + i * SB_BLOCK)


def _sb_attn_fwd(qb, kb, vb, n_heads):
    t = qb.shape[0]
    d = n_heads * HEAD_DIM
    nq = t // SB_BLOCK
    hb = _pick(n_heads, (8, 4, 2, 1))
    wide = hb * HEAD_DIM
    scale = HEAD_DIM ** -0.5

    def body(q_ref, k_ref, v_ref, o_ref, lt_ref):
        i = pl.program_id(1)
        row = lax.broadcasted_iota(jnp.int32, (SB_BLOCK, SB_BLOCK), 0)
        col = lax.broadcasted_iota(jnp.int32, (SB_BLOCK, SB_BLOCK), 1)
        after = jnp.where(row > col, 1.0, 0.0).astype(BF16)
        heads = [slice(h * HEAD_DIM, (h + 1) * HEAD_DIM) for h in range(hb)]
        every = range(hb)

        def step(m, carry):
            j0 = i - 2 * m
            js = (j0, jnp.maximum(j0 - 1, 0))
            valid = (_sb_valid(i, js[0]), jnp.logical_and(_sb_valid(i, js[1]), j0 >= 1))
            rows = [pl.ds(pl.multiple_of(j * SB_BLOCK, SB_BLOCK), SB_BLOCK) for j in js]
            units = [(h, b) for b in range(2) for h in every]
            z = {(h, b): _mm_nt(q_ref[:, heads[h]], k_ref[rows[b], heads[h]]) * scale for h, b in units}
            sp = {u: _softplus_raw(z[u]) for u in units}
            lm = {u: jnp.where(valid[u[1]], -sp[u], 0.0) for u in units}
            tail = {u: _cumsum_mm(lm[u], after) for u in units}
            later = {(h, 0): carry[h][1] for h in every}
            later.update({(h, 1): carry[h][1] + jnp.sum(lm[h, 0], axis=1, keepdims=True) for h in every})
            w = {u: jnp.where(valid[u[1]], jnp.exp(z[u] - sp[u] + later[u] + tail[u]), 0.0) for u in units}
            pv = {u: _mm_nn(w[u], v_ref[rows[u[1]], heads[u[0]]]) for u in units}
            return tuple((carry[h][0] + pv[h, 0] + pv[h, 1], later[h, 1] + jnp.sum(lm[h, 1], axis=1, keepdims=True))
                         for h in every)

        init = tuple((jnp.zeros((SB_BLOCK, HEAD_DIM), F32), jnp.zeros((SB_BLOCK, 1), F32)) for _ in heads)
        res = lax.fori_loop(0, (i + 2) // 2, step, init)
        for h, sl in enumerate(heads):
            o_ref[:, sl] = res[h][0]
            lt_ref[:, sl] = jnp.broadcast_to(res[h][1], (SB_BLOCK, HEAD_DIM))

    return pl.pallas_call(
        body, name="sb_attn_fwd", grid=(n_heads // hb, nq),
        in_specs=[pl.BlockSpec((SB_BLOCK, wide), lambda g, i: (i, g)),
                  pl.BlockSpec((t, wide), lambda g, i: (0, g)),
                  pl.BlockSpec((t, wide), lambda g, i: (0, g))],
        out_specs=[pl.BlockSpec((SB_BLOCK, wide), lambda g, i: (i, g)),
                   pl.BlockSpec((SB_BLOCK, wide), lambda g, i: (i, g))],
        out_shape=[jax.ShapeDtypeStruct((t, d), F32), jax.ShapeDtypeStruct((t, d), F32)],
        compiler_params=_params(dimension_semantics=("arbitrary", "arbitrary")),
    )(qb, kb, vb)


def _sb_attn_bwd(qb, kb, vb, ltot, do, n_heads):
    t = qb.shape[0]
    d = n_heads * HEAD_DIM
    nq = t // SB_BLOCK
    hb = _pick(n_heads, (8, 4, 2, 1))
    wide = hb * HEAD_DIM
    scale = HEAD_DIM ** -0.5

    def body(q_ref, k_ref, v_ref, lt_ref, do_ref, dq_ref, dk_ref, dv_ref, dob_ref):
        i = pl.program_id(1)

        @pl.when(i == 0)
        def _():
            dk_ref[...] = jnp.zeros((t, wide), F32)
            dv_ref[...] = jnp.zeros((t, wide), F32)

        row = lax.broadcasted_iota(jnp.int32, (SB_BLOCK, SB_BLOCK), 0)
        col = lax.broadcasted_iota(jnp.int32, (SB_BLOCK, SB_BLOCK), 1)
        upto = jnp.where(row <= col, 1.0, 0.0).astype(BF16)
        before = jnp.where(row < col, 1.0, 0.0).astype(BF16)
        heads = [slice(h * HEAD_DIM, (h + 1) * HEAD_DIM) for h in range(hb)]
        every = range(hb)
        dob_ref[...] = do_ref[...].astype(BF16)
        totals = [jnp.max(lt_ref[:, sl], axis=1, keepdims=True) for sl in heads]

        def step(m, carry):
            js = (2 * m, jnp.minimum(2 * m + 1, nq - 1))
            valid = (_sb_valid(i, js[0]), jnp.logical_and(_sb_valid(i, js[1]), 2 * m + 1 <= i))
            rows = [pl.ds(pl.multiple_of(j * SB_BLOCK, SB_BLOCK), SB_BLOCK) for j in js]
            units = [(h, b) for b in range(2) for h in every]
            z = {(h, b): _mm_nt(q_ref[:, heads[h]], k_ref[rows[b], heads[h]]) * scale for h, b in units}
            dw = {(h, b): _mm_nt(dob_ref[:, heads[h]], v_ref[rows[b], heads[h]]) for h, b in units}
            sp = {u: _softplus_raw(z[u]) for u in units}
            lm = {u: jnp.where(valid[u[1]], -sp[u], 0.0) for u in units}
            head = {u: _cumsum_mm(lm[u], upto) for u in units}
            lm_before = {(h, 0): carry[h][1] for h in every}
            lm_before.update({(h, 1): carry[h][1] + jnp.sum(lm[h, 0], axis=1, keepdims=True) for h in every})
            w = {u: jnp.where(valid[u[1]], jnp.exp(z[u] - sp[u] + totals[u[0]] - (lm_before[u] + head[u])), 0.0)
                 for u in units}
            e = {u: w[u] * dw[u] for u in units}
            e_local = {u: _mm_nn(e[u], before) for u in units}
            e_before = {(h, 0): carry[h][2] for h in every}
            e_before.update({(h, 1): carry[h][2] + jnp.sum(e[h, 0], axis=1, keepdims=True) for h in every})
            sig = {u: jnp.exp(z[u] - sp[u]) for u in units}
            dz = {u: jnp.where(valid[u[1]], e[u] * (1.0 - sig[u]) - (e_before[u] + e_local[u]) * sig[u], 0.0) * scale
                  for u in units}
            for h, b in units:
                dv_ref[rows[b], heads[h]] += _mm_tn(w[h, b], dob_ref[:, heads[h]])
            for h, b in units:
                dk_ref[rows[b], heads[h]] += _mm_tn(dz[h, b], q_ref[:, heads[h]])
            dq = {(h, b): _mm_nn(dz[h, b], k_ref[rows[b], heads[h]]) for h, b in units}
            return tuple((carry[h][0] + dq[h, 0] + dq[h, 1],
                          lm_before[h, 1] + jnp.sum(lm[h, 1], axis=1, keepdims=True),
                          e_before[h, 1] + jnp.sum(e[h, 1], axis=1, keepdims=True)) for h in every)

        zero_col = jnp.zeros((SB_BLOCK, 1), F32)
        init = tuple((jnp.zeros((SB_BLOCK, HEAD_DIM), F32), zero_col, zero_col) for _ in heads)
        res = lax.fori_loop(0, (i + 2) // 2, step, init)
        for h, sl in enumerate(heads):
            dq_ref[:, sl] = res[h][0]

    return pl.pallas_call(
        body, name="sb_attn_bwd", grid=(n_heads // hb, nq),
        in_specs=[pl.BlockSpec((SB_BLOCK, wide), lambda g, i: (i, g)),
                  pl.BlockSpec((t, wide), lambda g, i: (0, g)),
                  pl.BlockSpec((t, wide), lambda g, i: (0, g)),
                  pl.BlockSpec((SB_BLOCK, wide), lambda g, i: (i, g)),
                  pl.BlockSpec((SB_BLOCK, wide), lambda g, i: (i, g))],
        out_specs=[pl.BlockSpec((SB_BLOCK, wide), lambda g, i: (i, g)),
                   pl.BlockSpec((t, wide), lambda g, i: (0, g)),
                   pl.BlockSpec((t, wide), lambda g, i: (0, g))],
        out_shape=[jax.ShapeDtypeStruct((t, d), F32), jax.ShapeDtypeStruct((t, d), F32),
                   jax.ShapeDtypeStruct((t, d), F32)],
        scratch_shapes=[pltpu.VMEM((SB_BLOCK, wide), BF16)],
        compiler_params=_params(dimension_semantics=("arbitrary", "arbitrary")),
    )(qb, kb, vb, ltot, do)


def _gated_norm(oa, z, gn):
    return _rms(oa, gn, RMS_EPS) * _silu(z)


def _merge_gates(ya, yb, ga, gb):
    return jax.nn.sigmoid(ga) * ya + jax.nn.sigmoid(gb) * yb


def _merge_fwd(x1, oa, proj, ob, gn, wa, wb, wo, layer, n_heads):
    t, d = x1.shape
    tm = _pick(t, (256, 128))
    square = _layer_spec(layer, (d, d), lambda i: (0, 0), single=True)

    def body(x_ref, oa_ref, z_ref, ob_ref, ga_ref, gb_ref, gn_ref, wa_ref, wb_ref, wo_ref, o_ref, na_ref):
        for h in range(n_heads):
            sl = slice(h * HEAD_DIM, (h + 1) * HEAD_DIM)
            na_ref[:, sl] = _gated_norm(oa_ref[:, sl], z_ref[:, sl], gn_ref[...]).astype(BF16)
        m = _merge_gates(_mm_nn(na_ref[...], wa_ref[...]), _mm_nn(ob_ref[...], wb_ref[...]), ga_ref[...], gb_ref[...])
        o_ref[...] = x_ref[...] + _mm_nn(m, wo_ref[...])

    tile = lambda k: pl.BlockSpec((tm, d), lambda i: (i, k))
    return pl.pallas_call(
        body, name="merge_fwd", grid=(t // tm,),
        in_specs=[tile(0), tile(0), tile(3), tile(0), tile(7), tile(8), _const_spec((1, HEAD_DIM)),
                  square, square, square],
        out_specs=tile(0),
        out_shape=jax.ShapeDtypeStruct((t, d), F32),
        scratch_shapes=[pltpu.VMEM((tm, d), BF16)],
        compiler_params=_params(dimension_semantics=("arbitrary",)),
    )(x1, oa, proj, ob, proj, proj, gn, wa, wb, wo)


def _merge_bwd(oa, proj, ob, dy, gn, wa, wb, wo, layer, n_heads):
    t, d = oa.shape
    tm = _pick(t, (256, 128))
    nt = t // tm
    square = _layer_spec(layer, (d, d), lambda i: (0, 0), single=True)

    def body(oa_ref, z_ref, ob_ref, ga_ref, gb_ref, dy_ref, gn_ref, wa_ref, wb_ref, wo_ref,
             doa_ref, dz_ref, dob_ref, dga_ref, dgb_ref, dgn_ref, dwa_hbm, dwb_hbm, dwo_hbm,
             na_ref, dna_ref, dwa_ref, dwb_ref, dwo_ref, stage_ref):
        i = pl.program_id(0)

        @pl.when(i == 0)
        def _():
            dgn_ref[...] = jnp.zeros((1, HEAD_DIM), F32)
            dwa_ref[...] = jnp.zeros((d, d), F32)
            dwb_ref[...] = jnp.zeros((d, d), F32)
            dwo_ref[...] = jnp.zeros((d, d), F32)

        for h in range(n_heads):
            sl = slice(h * HEAD_DIM, (h + 1) * HEAD_DIM)
            na_ref[:, sl] = _gated_norm(oa_ref[:, sl], z_ref[:, sl], gn_ref[...]).astype(BF16)
        dy = dy_ref[...].astype(BF16)
        ob = ob_ref[...].astype(BF16)
        ya = _mm_nn(na_ref[...], wa_ref[...])
        yb = _mm_nn(ob, wb_ref[...])
        m, vjp = jax.vjp(_merge_gates, ya, yb, ga_ref[...], gb_ref[...])
        dwo_ref[...] += _mm_tn(m, dy)
        dya, dyb, dga, dgb = vjp(_mm_nt(dy, wo_ref[...]))
        dga_ref[...] = dga.astype(BF16)
        dgb_ref[...] = dgb.astype(BF16)
        dwa_ref[...] += _mm_tn(na_ref[...], dya)
        dwb_ref[...] += _mm_tn(ob, dyb)
        dob_ref[...] = _mm_nt(dyb, wb_ref[...])
        dna_ref[...] = _mm_nt(dya, wa_ref[...])
        for h in range(n_heads):
            sl = slice(h * HEAD_DIM, (h + 1) * HEAD_DIM)
            _, vjp_h = jax.vjp(_gated_norm, oa_ref[:, sl], z_ref[:, sl], gn_ref[...])
            doa, dz, dgn = vjp_h(dna_ref[:, sl])
            doa_ref[:, sl] = doa
            dz_ref[:, sl] = dz.astype(BF16)
            dgn_ref[...] += dgn

        @pl.when(i == nt - 1)
        def _():
            for acc, out in ((dwa_ref, dwa_hbm), (dwb_ref, dwb_hbm), (dwo_ref, dwo_hbm)):
                stage_ref[...] = acc[...].astype(BF16)
                pltpu.sync_copy(stage_ref, out)

    tile = lambda k: pl.BlockSpec((tm, d), lambda i: (i, k))
    any_spec = pl.BlockSpec(memory_space=pl.ANY)
    return pl.pallas_call(
        body, name="merge_bwd", grid=(nt,),
        in_specs=[tile(0), tile(3), tile(0), tile(7), tile(8), tile(0), _const_spec((1, HEAD_DIM)),
                  square, square, square],
        out_specs=[tile(0), tile(0), tile(0), tile(0), tile(0), _const_spec((1, HEAD_DIM)),
                   any_spec, any_spec, any_spec],
        out_shape=[jax.ShapeDtypeStruct((t, d), F32), jax.ShapeDtypeStruct((t, d), BF16),
                   jax.ShapeDtypeStruct((t, d), F32), jax.ShapeDtypeStruct((t, d), BF16),
                   jax.ShapeDtypeStruct((t, d), BF16), jax.ShapeDtypeStruct((1, HEAD_DIM), F32),
                   jax.ShapeDtypeStruct((d, d), BF16), jax.ShapeDtypeStruct((d, d), BF16),
                   jax.ShapeDtypeStruct((d, d), BF16)],
        scratch_shapes=[pltpu.VMEM((tm, d), BF16), pltpu.VMEM((tm, d), F32),
                        pltpu.VMEM((d, d), F32), pltpu.VMEM((d, d), F32), pltpu.VMEM((d, d), F32),
                        pltpu.VMEM((d, d), BF16)],
        compiler_params=_params(dimension_semantics=("arbitrary",)),
    )(oa, proj, ob, proj, proj, dy, gn, wa, wb, wo)


def _loss_head(y, target):
    t, d = y.shape
    tm = _pick(t, (256, 128))

    def body(y_ref, t_ref, dy_ref, loss_ref):
        @pl.when(pl.program_id(0) == 0)
        def _():
            loss_ref[...] = jnp.zeros((8, LANES), F32)

        err = y_ref[...] - t_ref[...]
        dy_ref[...] = err * (1.0 / d)
        per_token = jnp.sum(err * err, axis=1, keepdims=True) * (1.0 / d)
        loss_ref[...] += 0.5 * jnp.sum(per_token, axis=0, keepdims=True)

    return pl.pallas_call(
        body, name="loss_head", grid=(t // tm,),
        in_specs=[pl.BlockSpec((tm, d), lambda i: (i, 0)), pl.BlockSpec((tm, d), lambda i: (i, 0))],
        out_specs=[pl.BlockSpec((tm, d), lambda i: (i, 0)), _const_spec((8, LANES))],
        out_shape=[jax.ShapeDtypeStruct((t, d), F32), jax.ShapeDtypeStruct((8, LANES), F32)],
        compiler_params=_params(dimension_semantics=("arbitrary",)),
    )(y, target)


def _adamw(w, g, m, v):
    rows, cols = w.shape
    tr = rows
    for cand in (512, 256, 128, 64, 32, 16, 8):
        if rows % cand == 0 and cand * cols * 4 <= 2 * 1024 * 1024:
            tr = cand
            break

    def body(w_ref, g_ref, m_ref, v_ref, d_ref, mo_ref, vo_ref):
        g = g_ref[...]
        m2 = ADAM_B1 * m_ref[...] + (1.0 - ADAM_B1) * g
        v2 = ADAM_B2 * v_ref[...] + (1.0 - ADAM_B2) * (g * g)
        m_hat = m2 / (1.0 - ADAM_B1 ** ADAM_STEP)
        v_hat = v2 / (1.0 - ADAM_B2 ** ADAM_STEP)
        d_ref[...] = -ADAM_LR * (m_hat / (jnp.sqrt(v_hat) + ADAM_EPS) + ADAM_WD * w_ref[...])
        mo_ref[...] = m2
        vo_ref[...] = v2

    spec = pl.BlockSpec((tr, cols), lambda i: (i, 0))
    shape = jax.ShapeDtypeStruct((rows, cols), F32)
    return pl.pallas_call(
        body, name="adamw", grid=(rows // tr,), in_specs=[spec] * 4, out_specs=[spec] * 3,
        out_shape=[shape] * 3, compiler_params=_params(dimension_semantics=("arbitrary",)),
    )(w, g, m, v)


def _place():
    return lax.axis_index("x"), lax.axis_index("y"), lax.axis_index("c")


def _other_chips(x, y):
    return [(1 - x, y), (x, 1 - y), (1 - x, 1 - y)]


def _tile_rows(rows, cols, itemsize, cap=1536 * 1024):
    best = None
    for cand in range(16, rows + 1, 16):
        if rows % cand == 0 and cand * cols * itemsize <= cap:
            best = cand
    return best if best is not None else rows


def _allgather_layer(shards, layer, collective_id):
    n = len(shards)

    def body(*refs):
        srcs, outs = refs[:n], refs[n:2 * n]
        send_sems, recv_sems = refs[2 * n:]
        x, y, c = _place()
        me, sibling = (x, y, c), (x, y, 1 - c)
        chips = _other_chips(x, y)
        barrier = pltpu.get_barrier_semaphore()
        for peer in [(*chip, c) for chip in chips] + [sibling]:
            pl.semaphore_signal(barrier, inc=1, device_id=peer, device_id_type=MESH)
        pl.semaphore_wait(barrier, N_CHIPS)

        def half(w, which):
            rows = shards[w].shape[1] // 2
            return pl.ds(which * rows, rows)

        def copy(w, k, shard, which, to, from_src=False):
            part = half(w, which)
            return pltpu.make_async_remote_copy(
                src_ref=srcs[w].at[layer, part] if from_src else outs[w].at[shard, part],
                dst_ref=outs[w].at[shard, part], send_sem=send_sems.at[6 * w + k], recv_sem=recv_sems.at[6 * w + k],
                device_id=to, device_id_type=MESH)

        first = [copy(w, j, 2 * x + y, c, (*chip, c), from_src=True) for j, chip in enumerate(chips) for w in range(n)]
        for cp in first:
            cp.start()
        passed = []
        for j, (cx, cy) in enumerate(chips):
            for w in range(n):
                copy(w, j, 2 * cx + cy, c, me).wait_recv()
                cp = copy(w, 3 + j, 2 * cx + cy, c, sibling)
                cp.start()
                passed.append(cp)
        for j, (cx, cy) in enumerate(chips):
            for w in range(n):
                copy(w, 3 + j, 2 * cx + cy, 1 - c, me).wait_recv()
        for cp in first + passed:
            cp.wait_send()

    return pl.kernel(
        body, name=f"allgather_layer{layer}_id{collective_id}",
        out_type=[jax.ShapeDtypeStruct((N_CHIPS,) + s.shape[1:], s.dtype) for s in shards],
        mesh=plsc.ScalarSubcoreMesh(axis_name="sequencer", num_cores=1),
        scratch_types=[pltpu.SemaphoreType.DMA((6 * n,)), pltpu.SemaphoreType.DMA((6 * n,))],
        compiler_params=pltpu.CompilerParams(collective_id=collective_id),
    )(*shards)


def _swap_halves(grads):
    n = len(grads)

    def body(*refs):
        gs, gots = refs[:n], refs[n:2 * n]
        send_sems, recv_sems = refs[2 * n:]
        x, y, c = _place()
        copies = []
        for w in range(n):
            half = grads[w].shape[1] // 2
            copies.append(pltpu.make_async_remote_copy(
                src_ref=gs[w].at[:, pl.ds((1 - c) * half, half)], dst_ref=gots[w], send_sem=send_sems.at[w],
                recv_sem=recv_sems.at[w], device_id=(x, y, 1 - c), device_id_type=MESH))
        for cp in copies:
            cp.start()
        for cp in copies:
            cp.wait()

    hbm = pl.BlockSpec(memory_space=pl.ANY)
    return pl.pallas_call(
        body, name="swap_halves", in_specs=[hbm] * n, out_specs=[hbm] * n,
        out_shape=[jax.ShapeDtypeStruct((g.shape[0], g.shape[1] // 2, g.shape[2]), g.dtype) for g in grads],
        scratch_shapes=[pltpu.SemaphoreType.DMA((n,)), pltpu.SemaphoreType.DMA((n,))],
    )(*grads)


def _add_half(grad, got, c_idx):
    n, rows, all_cols = grad.shape
    side = N_CHIPS // n
    cols = all_cols // side
    half = rows // 2
    tr = _tile_rows(half, cols, 2)
    nb = half // tr

    def body(c_ref, a_ref, b_ref, o_ref):
        o_ref[...] = (a_ref[...].astype(F32) + b_ref[...].astype(F32)).astype(o_ref.dtype)

    return pl.pallas_call(
        body, name="add_half",
        grid_spec=pltpu.PrefetchScalarGridSpec(
            num_scalar_prefetch=1, grid=(N_CHIPS, nb),
            in_specs=[pl.BlockSpec((1, tr, cols), lambda s, r, c_ref: (s // side, c_ref[0] * nb + r, s % side)),
                      pl.BlockSpec((1, tr, cols), lambda s, r, c_ref: (s // side, r, s % side))],
            out_specs=pl.BlockSpec((1, tr, cols), lambda s, r, c_ref: (s, r, 0))),
        out_shape=jax.ShapeDtypeStruct((N_CHIPS, half, cols), grad.dtype),
        compiler_params=_params(dimension_semantics=("arbitrary", "arbitrary")),
    )(c_idx, grad, got)


def _scatter_partials(parts, layer, collective_id):
    n = len(parts)

    def body(*refs):
        ps, gots = refs[:n], refs[n:2 * n]
        send_sems, recv_sems = refs[2 * n:]
        x, y, c = _place()
        chips = _other_chips(x, y)
        barrier = pltpu.get_barrier_semaphore()
        for chip in chips:
            pl.semaphore_signal(barrier, inc=1, device_id=(*chip, c), device_id_type=MESH)
        pl.semaphore_wait(barrier, N_CHIPS - 1)
        copies = [pltpu.make_async_remote_copy(src_ref=ps[w].at[2 * cx + cy], dst_ref=gots[w].at[j],
                                               send_sem=send_sems.at[3 * w + j], recv_sem=recv_sems.at[3 * w + j],
                                               device_id=(cx, cy, c), device_id_type=MESH)
                  for j, (cx, cy) in enumerate(chips) for w in range(n)]
        for cp in copies:
            cp.start()
        for cp in copies:
            cp.wait()

    return pl.kernel(
        body, name=f"scatter_partials{layer}_id{collective_id}",
        out_type=[jax.ShapeDtypeStruct((N_CHIPS - 1,) + p.shape[1:], p.dtype) for p in parts],
        mesh=plsc.ScalarSubcoreMesh(axis_name="sequencer", num_cores=1),
        scratch_types=[pltpu.SemaphoreType.DMA((3 * n,)), pltpu.SemaphoreType.DMA((3 * n,))],
        compiler_params=pltpu.CompilerParams(collective_id=collective_id),
    )(*parts)


def _sum_partials(part, got, s_idx, c_idx, layer, depth, stacked=None):
    n, half, cols = part.shape
    tr = _tile_rows(half, cols, 2, cap=1024 * 1024)
    nb = half // tr

    def body(s_ref, c_ref, a_ref, b_ref, *rest):
        o_ref = rest[-1]
        acc = a_ref[0].astype(F32)
        for j in range(n - 1):
            acc = acc + b_ref[j].astype(F32)
        o_ref[...] = acc

    in_specs = [pl.BlockSpec((1, tr, cols), lambda r, s_ref, c_ref: (s_ref[0], r, 0)),
                pl.BlockSpec((n - 1, tr, cols), lambda r, s_ref, c_ref: (0, r, 0))]
    operands = [s_idx, c_idx, part, got]
    aliases = {}
    if stacked is not None:
        in_specs.append(pl.BlockSpec(memory_space=pl.ANY))
        operands.append(stacked)
        aliases = {len(operands) - 1: 0}
    return pl.pallas_call(
        body, name="sum_partials",
        grid_spec=pltpu.PrefetchScalarGridSpec(
            num_scalar_prefetch=2, grid=(nb,), in_specs=in_specs,
            out_specs=pl.BlockSpec((None, tr, cols), lambda r, s_ref, c_ref: (layer, c_ref[0] * nb + r, 0))),
        out_shape=jax.ShapeDtypeStruct((depth, 2 * half, cols), F32),
        input_output_aliases=aliases,
        compiler_params=_params(dimension_semantics=("arbitrary",)),
    )(*operands)


def _join_halves(bufs, layer):
    n = len(bufs)

    def body(*refs):
        outs = refs[n:2 * n]
        send_sems, recv_sems = refs[2 * n:]
        x, y, c = _place()
        copies = []
        for w in range(n):
            half = bufs[w].shape[1] // 2
            mine = outs[w].at[layer, pl.ds(c * half, half)]
            copies.append(pltpu.make_async_remote_copy(src_ref=mine, dst_ref=mine, send_sem=send_sems.at[w],
                                                       recv_sem=recv_sems.at[w], device_id=(x, y, 1 - c),
                                                       device_id_type=MESH))
        for cp in copies:
            cp.start()
        for cp in copies:
            cp.wait()

    hbm = pl.BlockSpec(memory_space=pl.ANY)
    return pl.pallas_call(
        body, name="join_halves", in_specs=[hbm] * n, out_specs=[hbm] * n,
        out_shape=[jax.ShapeDtypeStruct(b.shape, b.dtype) for b in bufs],
        input_output_aliases={w: w for w in range(n)},
        scratch_shapes=[pltpu.SemaphoreType.DMA((n,)), pltpu.SemaphoreType.DMA((n,))],
    )(*bufs)


def _allreduce_small(v, name):
    rows = v.shape[0]

    def body(v_ref, o_ref, gath, send_sems, recv_sems):
        x, y, c = _place()
        idx = 4 * x + 2 * y + c
        gath[0] = v_ref[...]
        copies = []
        for r in range(1, N_DEV):
            peer = (1 - x if r & 4 else x, 1 - y if r & 2 else y, 1 - c if r & 1 else c)
            cp = pltpu.make_async_remote_copy(src_ref=v_ref, dst_ref=gath.at[r], send_sem=send_sems.at[r - 1],
                                              recv_sem=recv_sems.at[r - 1], device_id=peer, device_id_type=MESH)
            cp.start()
            copies.append(cp)
        for cp in copies:
            cp.wait()
        acc = gath[idx]
        for a in range(1, N_DEV):
            acc = acc + gath[lax.bitwise_xor(idx, a)]
        o_ref[...] = acc

    vmem = pl.BlockSpec(memory_space=pltpu.VMEM)
    return pl.pallas_call(
        body, name=name, in_specs=[vmem], out_specs=vmem,
        out_shape=jax.ShapeDtypeStruct((rows, LANES), F32),
        scratch_shapes=[pltpu.VMEM((N_DEV, rows, LANES), F32), pltpu.SemaphoreType.DMA((N_DEV - 1,)),
                        pltpu.SemaphoreType.DMA((N_DEV - 1,))],
    )(v)


def _join_shards(name, gathered):
    if name in COL_SHARDED:
        return jnp.concatenate([gathered[s] for s in range(N_CHIPS)], axis=1)[None]
    return gathered.reshape(1, N_CHIPS * gathered.shape[1], gathered.shape[2])


def _row_shards(g):
    return g.reshape(N_CHIPS, g.shape[0] // N_CHIPS, g.shape[1])


def _mixer_runs(width, cut, n_small):
    runs = []
    for s in range(N_CHIPS):
        lo, hi = s * width, (s + 1) * width
        spans = ((True, lo, min(hi, cut)), (False, max(lo, cut), min(hi, cut + n_small)),
                 (True, max(lo, cut + n_small), hi))
        runs.append([(is_main, a - lo, b - lo) for is_main, a, b in spans if a < b])
    return runs


def _split_mixer_weight(gathered, runs):
    main = [gathered[s][:, a:b] for s, parts in enumerate(runs) for is_main, a, b in parts if is_main]
    small = [gathered[s][:, a:b] for s, parts in enumerate(runs) for is_main, a, b in parts if not is_main]
    small = small[0] if len(small) == 1 else jnp.concatenate(small, axis=1)
    return jnp.concatenate(main, axis=1)[None], jnp.pad(small, ((0, 0), (0, LANES - small.shape[1])))[None]


def _join_mixer_grad(d_main, d_small, runs):
    shards, m, k = [], 0, 0
    for parts in runs:
        cols = []
        for is_main, a, b in parts:
            if is_main:
                cols.append(d_main[:, m:m + b - a])
                m += b - a
            else:
                cols.append(d_small[:, k:k + b - a].astype(d_main.dtype))
                k += b - a
        shards.append(cols[0] if len(cols) == 1 else jnp.concatenate(cols, axis=1))
    return jnp.stack(shards)


def _pad_small(flat):
    n = flat.shape[0]
    block = 8 * LANES
    padded = -(-n // block) * block
    return jnp.pad(flat, (0, padded - n)).reshape(padded // LANES, LANES)


def kernel(x, ffn1_norm, ffn1_w_in, ffn1_w_out, mix_norm, w_in, dn_conv_w, dn_a_log, dn_dt_bias, dn_out_norm, sb_q_norm, sb_k_norm, w_branch_a, w_branch_b, w_out, ffn2_norm, ffn2_w_in, ffn2_w_out, loss_target, m_ffn1_norm, m_ffn1_w_in, m_ffn1_w_out, m_mix_norm, m_w_in, m_dn_conv_w, m_dn_a_log, m_dn_dt_bias, m_dn_out_norm, m_sb_q_norm, m_sb_k_norm, m_w_branch_a, m_w_branch_b, m_w_out, m_ffn2_norm, m_ffn2_w_in, m_ffn2_w_out, v_ffn1_norm, v_ffn1_w_in, v_ffn1_w_out, v_mix_norm, v_w_in, v_dn_conv_w, v_dn_a_log, v_dn_dt_bias, v_dn_out_norm, v_sb_q_norm, v_sb_k_norm, v_w_branch_a, v_w_branch_b, v_w_out, v_ffn2_norm, v_ffn2_w_in, v_ffn2_w_out):
    w = dict(ffn1_norm=ffn1_norm, ffn1_w_in=ffn1_w_in, ffn1_w_out=ffn1_w_out, mix_norm=mix_norm, w_in=w_in,
             dn_conv_w=dn_conv_w, dn_a_log=dn_a_log, dn_dt_bias=dn_dt_bias, dn_out_norm=dn_out_norm,
             sb_q_norm=sb_q_norm, sb_k_norm=sb_k_norm, w_branch_a=w_branch_a, w_branch_b=w_branch_b, w_out=w_out,
             ffn2_norm=ffn2_norm, ffn2_w_in=ffn2_w_in, ffn2_w_out=ffn2_w_out)
    mom = dict(ffn1_norm=m_ffn1_norm, ffn1_w_in=m_ffn1_w_in, ffn1_w_out=m_ffn1_w_out, mix_norm=m_mix_norm, w_in=m_w_in,
               dn_conv_w=m_dn_conv_w, dn_a_log=m_dn_a_log, dn_dt_bias=m_dn_dt_bias, dn_out_norm=m_dn_out_norm,
               sb_q_norm=m_sb_q_norm, sb_k_norm=m_sb_k_norm, w_branch_a=m_w_branch_a, w_branch_b=m_w_branch_b,
               w_out=m_w_out, ffn2_norm=m_ffn2_norm, ffn2_w_in=m_ffn2_w_in, ffn2_w_out=m_ffn2_w_out)
    var = dict(ffn1_norm=v_ffn1_norm, ffn1_w_in=v_ffn1_w_in, ffn1_w_out=v_ffn1_w_out, mix_norm=v_mix_norm, w_in=v_w_in,
               dn_conv_w=v_dn_conv_w, dn_a_log=v_dn_a_log, dn_dt_bias=v_dn_dt_bias, dn_out_norm=v_dn_out_norm,
               sb_q_norm=v_sb_q_norm, sb_k_norm=v_sb_k_norm, w_branch_a=v_w_branch_a, w_branch_b=v_w_branch_b,
               w_out=v_w_out, ffn2_norm=v_ffn2_norm, ffn2_w_in=v_ffn2_w_in, ffn2_w_out=v_ffn2_w_out)

    _, t, d = x.shape
    depth = ffn1_norm.shape[0]
    n_heads = d // HEAD_DIM
    conv_cols = dn_conv_w.shape[2]
    assert d % HEAD_DIM == 0 and t % SB_BLOCK == 0 and 2 * n_heads <= LANES and depth % 2 == 0
    assert w_in.shape[2] * N_CHIPS == 9 * d + 2 * n_heads and conv_cols * N_CHIPS == 3 * d

    x_idx, y_idx, c_idx = _place()
    shard = 2 * x_idx + y_idx
    c_arr = jnp.reshape(c_idx, (1,)).astype(jnp.int32)
    s_arr = jnp.reshape(shard, (1,)).astype(jnp.int32)

    mine = {n: w[n].astype(BF16) for n in BIG}
    runs = _mixer_runs(w_in.shape[2], 4 * d, 2 * n_heads)
    groups = (("ffn1_w_in", "ffn1_w_out"), ("w_in",), ("w_branch_a", "w_branch_b", "w_out"),
              ("ffn2_w_in", "ffn2_w_out"))
    first = groups[0]
    rest = tuple(n for g in groups[1:] for n in g)
    n_gathers = depth + len(groups) - 1

    def gather(names, l, collective_id):
        return dict(zip(names, _allgather_layer([mine[n] for n in names], l, collective_id)))

    arriving = [{}]
    for k, names in enumerate(groups):
        arriving[0].update(gather(names, 0, 0 if k == 0 else depth + k - 1))
    arriving += [gather(BIG, l, l) for l in range(1, depth)]

    def layer_weights(l, names, after):
        gathered, after = lax.optimization_barrier(({n: arriving[l][n] for n in names}, after))
        full = {}
        for n in names:
            g = lax.dynamic_update_slice(gathered[n], mine[n][l][None], (shard, 0, 0))
            if n == "w_in":
                full["w_main"], full["w_ba"] = _split_mixer_weight(g, runs)
            else:
                full[n] = _join_shards(n, g)
        return full, after

    conv_place = lax.dynamic_update_slice(jnp.zeros((depth, DN_CONV, 3 * d), F32), dn_conv_w, (0, 0, shard * conv_cols))
    conv_rows = _pad_small(conv_place.reshape(-1))
    conv_full = (0.5 * _allreduce_small(conv_rows, "allgather_conv")).reshape(-1)[:depth * DN_CONV * 3 * d]
    conv_full = jnp.pad(conv_full.reshape(depth, DN_CONV, 3 * d), ((0, 0), (0, CONV_ROWS - DN_CONV), (0, 0)))

    def head_row(vals):
        return jnp.pad(vals, (n_heads, LANES - 2 * n_heads)).reshape(1, LANES)

    saved, layers = [], []
    cur = x[0]
    for l in range(depth):
        full, x0 = layer_weights(l, groups[0], cur)
        layers.append(full)
        x1, h1 = _ffn_fwd(x0, ffn1_norm[l][None], full["ffn1_w_in"], full["ffn1_w_out"], 0)
        later, x1 = layer_weights(l, groups[1], x1)
        full.update(later)
        proj, ba, hm = _proj_fwd(x1, mix_norm[l][None], full["w_main"], full["w_ba"], 0)
        act = _dn_prep_fwd(proj, conv_full[l], n_heads)
        alog, dtb = head_row(dn_a_log[l]), head_row(dn_dt_bias[l])
        oa, snaps = _delta_fwd(act, ba, alog, dtb, n_heads)
        qb, kb, vb = _sb_prep_fwd(proj, sb_q_norm[l][None], sb_k_norm[l][None], n_heads)
        ob, ltot = _sb_attn_fwd(qb, kb, vb, n_heads)
        later, (oa, ob) = layer_weights(l, groups[2], (oa, ob))
        full.update(later)
        x2 = _merge_fwd(x1, oa, proj, ob, dn_out_norm[l][None], full["w_branch_a"], full["w_branch_b"],
                        full["w_out"], 0, n_heads)
        later, x2 = layer_weights(l, groups[3], x2)
        full.update(later)
        cur, h2 = _ffn_fwd(x2, ffn2_norm[l][None], full["ffn2_w_in"], full["ffn2_w_out"], 0)
        saved.append((x0, x1, proj, ba, act, alog, dtb, oa, snaps, qb, kb, vb, ob, ltot, x2, h1, hm, h2))

    dcur, loss_part = _loss_head(cur, loss_target[0])

    grads = {n: [None] * depth for n in WEIGHTS}
    reduced = {n: None for n in BIG}
    in_flight = []

    def start_reduce(l, names, collective_id):
        g_major = [grads[n][l] for n in names]
        parts = [_add_half(g, got, c_arr) for g, got in zip(g_major, _swap_halves(g_major))]
        return l, names, parts, _scatter_partials(parts, l, collective_id)

    def finish_reduce(started, after):
        done_names, halves = [], []
        for l, names, parts, arrived in started:
            arrived, after = lax.optimization_barrier((arrived, after))
            halves += [_sum_partials(p, got, s_arr, c_arr, l, depth, stacked=reduced[n])
                       for n, p, got in zip(names, parts, arrived)]
            done_names += names
        if started:
            reduced.update(zip(done_names, _join_halves(halves, started[0][0])))
        return after

    for l in reversed(range(depth)):
        x0, x1, proj, ba, act, alog, dtb, oa, snaps, qb, kb, vb, ob, ltot, x2, h1, hm, h2 = saved[l]
        full = layers[l]
        dx2, dg, dwi, dwo = _ffn_bwd(x2, h2, ffn2_norm[l][None], dcur, full["ffn2_w_in"], full["ffn2_w_out"], 0)
        grads["ffn2_norm"][l] = dg[0]
        grads["ffn2_w_in"][l] = dwi
        grads["ffn2_w_out"][l] = _row_shards(dwo)
        doa, dz, dob, dga, dgb, dgn, dwa, dwb, dwout = _merge_bwd(
            oa, proj, ob, dx2, dn_out_norm[l][None], full["w_branch_a"], full["w_branch_b"], full["w_out"], 0,
            n_heads)
        grads["dn_out_norm"][l] = dgn[0]
        grads["w_branch_a"][l], grads["w_branch_b"][l] = _row_shards(dwa), _row_shards(dwb)
        grads["w_out"][l] = _row_shards(dwout)
        dqb, dkb, dvb = _sb_attn_bwd(qb, kb, vb, ltot, dob, n_heads)
        dsq, dsk, dqn, dkn = _sb_prep_bwd(proj, sb_q_norm[l][None], sb_k_norm[l][None], dqb, dkb, n_heads)
        grads["sb_q_norm"][l], grads["sb_k_norm"][l] = dqn[0], dkn[0]
        dact, dba, dal, ddt = _delta_bwd(act, ba, alog, dtb, snaps, doa, n_heads)
        grads["dn_a_log"][l] = dal[0, n_heads:2 * n_heads]
        grads["dn_dt_bias"][l] = ddt[0, n_heads:2 * n_heads]
        dqkv, dconv = _dn_prep_bwd(proj, conv_full[l], dact, n_heads)
        grads["dn_conv_w"][l] = dconv[:DN_CONV]
        dproj = jnp.concatenate([dqkv, dz, dsq, dsk, dvb.astype(BF16), dga, dgb], axis=1)
        dx1, dg, dwm, dwba = _proj_bwd(x1, hm, mix_norm[l][None], dx2, dproj, dba, full["w_main"], full["w_ba"], 0)
        grads["mix_norm"][l] = dg[0]
        grads["w_in"][l] = _join_mixer_grad(dwm, dwba, runs)
        early = start_reduce(l, rest, collective_id=n_gathers + l)
        dcur, dg, dwi, dwo = _ffn_bwd(x0, h1, ffn1_norm[l][None], dx1, full["ffn1_w_in"], full["ffn1_w_out"], 0)
        grads["ffn1_norm"][l] = dg[0]
        grads["ffn1_w_in"][l] = dwi
        grads["ffn1_w_out"][l] = _row_shards(dwo)

        dcur = finish_reduce(in_flight, dcur)
        in_flight = [early, start_reduce(l, first, collective_id=n_gathers + depth + l)]
    dcur = finish_reduce(in_flight, dcur)
    final = reduced
    grads = {n: jnp.stack(grads[n]) for n in SMALL + ("dn_conv_w",)}

    small_names = SMALL + ("dn_conv_w",)
    small_sizes = [int(np.prod(grads[n].shape)) for n in small_names]
    small_off = np.concatenate([[0], np.cumsum(small_sizes)])
    small = jnp.concatenate([grads[n].reshape(-1) for n in small_names] + [loss_part[0, :1]])
    small_sum = _allreduce_small(_pad_small(small), "allreduce_small").reshape(-1)
    for i, n in enumerate(small_names):
        final[n] = small_sum[small_off[i]:small_off[i + 1]].reshape(grads[n].shape)
    final["dn_conv_w"] = lax.dynamic_slice(final["dn_conv_w"], (0, 0, shard * conv_cols), (depth, DN_CONV, conv_cols))
    loss = small_sum[small_off[-1]]

    deltas, new_m, new_v = {}, {}, {}
    for n in WEIGHTS:
        shape = w[n].shape
        flat = (-1, shape[-1])
        dl, m2, v2 = _adamw(w[n].reshape(flat), final[n].reshape(flat), mom[n].reshape(flat), var[n].reshape(flat))
        deltas[n], new_m[n], new_v[n] = dl.reshape(shape), m2.reshape(shape), v2.reshape(shape)

    grad_x = dcur[None]
    return (loss, grad_x, *[final[n] for n in WEIGHTS], *[deltas[n] for n in WEIGHTS],
            *[new_m[n] for n in WEIGHTS], *[new_v[n] for n in WEIGHTS])
```
